```python
import math
import jax, jax.numpy as jnp
from jax import lax
import numpy as np

D_MODEL = 1024
BATCH = 8
SEQ = 8192
DEPTH = 2

N_META = 16
DN_HEAD_DIM = 128
DN_HEADS = D_MODEL // DN_HEAD_DIM
DN_KEY_DIM = DN_HEADS * DN_HEAD_DIM
DN_VAL_DIM = DN_HEADS * DN_HEAD_DIM
QKV_DIM = 2 * DN_KEY_DIM + DN_VAL_DIM
DN_CONV = 4
CHUNK = 64
LEAD_PAD = (-N_META) % CHUNK
POOL_GROUPS = 4
POOL_WINDOWS = (2, 4, 8, 16)
POOL_GROUP_DIM = D_MODEL // 8
POOL_WIDTH = POOL_GROUPS * POOL_GROUP_DIM
POOL_OUT_GROUP = D_MODEL // POOL_GROUPS
N_BRANCHES = 2
SPLIT_SIZES = (QKV_DIM, DN_VAL_DIM, DN_HEADS, DN_HEADS, POOL_WIDTH, N_BRANCHES * D_MODEL)
IN_DIM = QKV_DIM + DN_VAL_DIM + 2 * DN_HEADS + POOL_WIDTH + N_BRANCHES * D_MODEL
D_FF = 256 * ((8 * D_MODEL // 3 + 255) // 256)
FFN_CONV = 3
NORM_EPS = 1e-6

kernel_name = 'hybrid_gdn_pool_meta_block'


def rms_norm(x, gain):
    xf = x.astype(jnp.float32)
    y = xf * lax.rsqrt(jnp.mean(xf * xf, axis=-1, keepdims=True) + NORM_EPS) * gain.astype(jnp.float32)
    return y.astype(x.dtype)


def l2_normalize(x):
    xf = x.astype(jnp.float32)
    return xf * lax.rsqrt(jnp.sum(xf * xf, axis=-1, keepdims=True) + NORM_EPS)


def causal_depthwise_conv(x, w):
    k_width = w.shape[0]
    length = x.shape[1]
    xp = jnp.pad(x, ((0, 0), (k_width - 1, 0), (0, 0)))
    out = xp[:, 0:length] * w[0]
    for j in range(1, k_width):
        out = out + xp[:, j:j + length] * w[j]
    return out


def chunk_gated_delta_rule(q, k, v, g, beta):
    b, t, h, dk = k.shape
    dv = v.shape[-1]
    n = t // CHUNK

    def chunks(a):
        a = a.reshape((b, n, CHUNK, h) + a.shape[3:])
        return jnp.moveaxis(a, 3, 1)

    q = chunks(q) * (dk ** -0.5)
    k = chunks(k)
    v = chunks(v)
    beta = chunks(beta)
    g = lax.cumsum(chunks(g), axis=3)
    idx = jnp.arange(CHUNK)
    causal = idx[:, None] >= idx[None, :]
    strict = idx[:, None] > idx[None, :]
    decay = jnp.exp(jnp.where(causal, g[..., :, None] - g[..., None, :], -jnp.inf))
    kb = k * beta[..., None]
    lower = jnp.where(strict, jnp.einsum('bhncd,bhnsd->bhncs', kb, k) * decay, 0.0)
    eye = jnp.eye(CHUNK, dtype=lower.dtype)
    tinv = lax.linalg.triangular_solve(lower + eye, jnp.broadcast_to(eye, lower.shape),
                                       left_side=True, lower=True, unit_diagonal=True)
    u = jnp.einsum('bhncs,bhnsv->bhncv', tinv, v * beta[..., None])
    w = jnp.einsum('bhncs,bhnsd->bhncd', tinv, kb * jnp.exp(g)[..., None])
    qk = jnp.where(causal, jnp.einsum('bhncd,bhnsd->bhncs', q, k) * decay, 0.0)
    q_dec = q * jnp.exp(g)[..., None]
    k_dec = k * jnp.exp(g[..., -1:] - g)[..., None]
    g_tot = jnp.exp(g[..., -1])

    def step(state, inp):
        q_i, k_i, u_i, w_i, qk_i, gt_i = inp
        v_new = u_i - jnp.einsum('bhck,bhkv->bhcv', w_i, state)
        o_i = jnp.einsum('bhck,bhkv->bhcv', q_i, state) + jnp.einsum('bhcs,bhsv->bhcv', qk_i, v_new)
        state = state * gt_i[..., None, None] + jnp.einsum('bhck,bhcv->bhkv', k_i, v_new)
        return state, o_i

    xs = (jnp.moveaxis(q_dec, 2, 0), jnp.moveaxis(k_dec, 2, 0), jnp.moveaxis(u, 2, 0),
          jnp.moveaxis(w, 2, 0), jnp.moveaxis(qk, 2, 0), jnp.moveaxis(g_tot, 2, 0))
    state0 = jnp.zeros((b, h, dk, dv), jnp.float32)
    _, o = lax.scan(step, state0, xs)
    o = jnp.moveaxis(o, 0, 2)
    return jnp.moveaxis(o, 1, 3).reshape(b, t, h, dv)


def multiscale_causal_pool(p, w_pool, pool_scale):
    b, l, _ = p.shape
    pg = p.astype(jnp.float32).reshape(b, l, POOL_GROUPS, POOL_GROUP_DIM)
    csum = lax.cumsum(pg, axis=1)
    pos = jnp.arange(l)
    outs = []
    for gi, win in enumerate(POOL_WINDOWS):
        c = csum[:, :, gi]
        lagged = jnp.pad(c, ((0, 0), (win, 0), (0, 0)))[:, :l]
        count = jnp.minimum(pos + 1, win).astype(jnp.float32)[None, :, None]
        outs.append((c - lagged) / count - pg[:, :, gi])
    pooled = jnp.stack(outs, axis=2)
    y = jnp.einsum('blgc,gcd->blgd', pooled, w_pool.astype(jnp.float32)).reshape(b, l, D_MODEL)
    return (y * pool_scale.astype(jnp.float32)).astype(p.dtype)


def hybrid_mixer(u, w_in, conv_qkv, a_log, dt_bias, head_norm, w_pool, pool_scale, w_out):
    b, l, _ = u.shape
    proj = jnp.einsum('bld,dp->blp', u, w_in)
    offsets = [int(o) for o in np.cumsum(SPLIT_SIZES)[:-1]]
    qkv, z, b_raw, a_raw, pool_in, gate_pre = jnp.split(proj, offsets, axis=-1)
    qkv = jax.nn.silu(causal_depthwise_conv(qkv, conv_qkv))
    q, k, v = jnp.split(qkv, [DN_KEY_DIM, 2 * DN_KEY_DIM], axis=-1)
    q = l2_normalize(q.reshape(b, l, DN_HEADS, DN_HEAD_DIM))
    k = l2_normalize(k.reshape(b, l, DN_HEADS, DN_HEAD_DIM))
    v = v.reshape(b, l, DN_HEADS, DN_HEAD_DIM).astype(jnp.float32)
    beta = jax.nn.sigmoid(b_raw.astype(jnp.float32))
    g = -jnp.exp(a_log.astype(jnp.float32)) * jax.nn.softplus(
        a_raw.astype(jnp.float32) + dt_bias.astype(jnp.float32))
    pad_r = (-(LEAD_PAD + l)) % CHUNK

    def pad_t(a):
        return jnp.pad(a, [(0, 0), (LEAD_PAD, pad_r)] + [(0, 0)] * (a.ndim - 2))

    o = chunk_gated_delta_rule(pad_t(q), pad_t(k), pad_t(v), pad_t(g), pad_t(beta))
    o = o[:, LEAD_PAD:LEAD_PAD + l]
    zf = z.astype(jnp.float32).reshape(b, l, DN_HEADS, DN_HEAD_DIM)
    o = (o * lax.rsqrt(jnp.mean(o * o, axis=-1, keepdims=True) + NORM_EPS)
         * head_norm.astype(jnp.float32) * jax.nn.silu(zf))
    y_a = o.reshape(b, l, DN_VAL_DIM).astype(u.dtype)
    y_b = multiscale_causal_pool(pool_in, w_pool, pool_scale)
    g_a, g_b = jnp.split(jax.nn.sigmoid(gate_pre), 2, axis=-1)
    y = g_a * y_a + g_b * y_b
    return jnp.einsum('bld,de->ble', y, w_out)


def conv_gated_mlp(u, w_up, conv_ffn, w_down):
    hid = jnp.einsum('bld,df->blf', u, w_up)
    hid = causal_depthwise_conv(hid, conv_ffn)
    gate, val = jnp.split(hid, 2, axis=-1)
    return jnp.einsum('blf,fd->bld', jax.nn.silu(gate) * val, w_down)


def _fwd_setup_inputs(seed: int = 0) -> dict:
    key = jax.random.key(seed)
    ks = jax.random.split(key, 16)
    f32 = jnp.float32

    def normal(k, shape, scale):
        return jax.random.normal(k, shape, f32) * scale

    x = normal(ks[0], (BATCH, SEQ, D_MODEL), 1.0)
    meta_tokens = normal(ks[1], (N_META, D_MODEL), 1.0)
    norm_mix = 1.0 + normal(ks[2], (DEPTH, D_MODEL), 0.02)
    w_in = normal(ks[3], (DEPTH, D_MODEL, IN_DIM), D_MODEL ** -0.5)
    conv_qkv = normal(ks[4], (DEPTH, DN_CONV, QKV_DIM), DN_CONV ** -0.5)
    a_log = jnp.log(jax.random.uniform(ks[5], (DEPTH, DN_HEADS), f32, 1.0, 16.0))
    dt = jnp.exp(jax.random.uniform(ks[6], (DEPTH, DN_HEADS), f32, math.log(1e-3), math.log(1e-1)))
    dt_bias = dt + jnp.log(-jnp.expm1(-dt))
    head_norm = 1.0 + normal(ks[7], (DEPTH, DN_HEAD_DIM), 0.02)
    w_pool = normal(ks[8], (DEPTH, POOL_GROUPS, POOL_GROUP_DIM, POOL_OUT_GROUP), POOL_GROUP_DIM ** -0.5)
    pool_scale = 1.0 + normal(ks[9], (DEPTH, D_MODEL), 0.02)
    w_out = normal(ks[10], (DEPTH, D_MODEL, D_MODEL), D_MODEL ** -0.5)
    norm_ffn = 1.0 + normal(ks[11], (DEPTH, D_MODEL), 0.02)
    w_up = normal(ks[12], (DEPTH, D_MODEL, 2 * D_FF), D_MODEL ** -0.5)
    conv_ffn = normal(ks[13], (DEPTH, FFN_CONV, 2 * D_FF), FFN_CONV ** -0.5)
    w_down = normal(ks[14], (DEPTH, D_FF, D_MODEL), D_FF ** -0.5)
    norm_final = 1.0 + normal(ks[15], (D_MODEL,), 0.02)
    return {'x': x, 'meta_tokens': meta_tokens, 'norm_mix': norm_mix, 'w_in': w_in,
            'conv_qkv': conv_qkv, 'a_log': a_log, 'dt_bias': dt_bias, 'head_norm': head_norm,
            'w_pool': w_pool, 'pool_scale': pool_scale, 'w_out': w_out, 'norm_ffn': norm_ffn,
            'w_up': w_up, 'conv_ffn': conv_ffn, 'w_down': w_down, 'norm_final': norm_final}


def _fwd_reference(x, meta_tokens, norm_mix, w_in, conv_qkv, a_log, dt_bias, head_norm, w_pool,
              pool_scale, w_out, norm_ffn, w_up, conv_ffn, w_down, norm_final):
    b = x.shape[0]
    meta = jnp.broadcast_to(meta_tokens.astype(x.dtype)[None], (b, N_META, D_MODEL))
    h = jnp.concatenate([meta, x], axis=1)
    for layer in range(DEPTH):
        h = h + hybrid_mixer(rms_norm(h, norm_mix[layer]), w_in[layer], conv_qkv[layer], a_log[layer],
                             dt_bias[layer], head_norm[layer], w_pool[layer], pool_scale[layer], w_out[layer])
        h = h + conv_gated_mlp(rms_norm(h, norm_ffn[layer]), w_up[layer], conv_ffn[layer], w_down[layer])
    h = rms_norm(h, norm_final)
    return h[:, N_META:]


import jax as _jax
import jax.numpy as _jnp

TWIN_FORMAT = 'train_step'
FWD_PARAMS = ['x', 'meta_tokens', 'norm_mix', 'w_in', 'conv_qkv', 'a_log', 'dt_bias', 'head_norm', 'w_pool', 'pool_scale', 'w_out', 'norm_ffn', 'w_up', 'conv_ffn', 'w_down', 'norm_final']
TWIN_WEIGHTS = ['meta_tokens', 'norm_mix', 'w_in', 'conv_qkv', 'a_log', 'dt_bias', 'head_norm', 'w_pool', 'pool_scale', 'w_out', 'norm_ffn', 'w_up', 'conv_ffn', 'w_down', 'norm_final']
TWIN_DIFF_INPUT = 'x'
TWIN_INPUTS = ['x', 'meta_tokens', 'norm_mix', 'w_in', 'conv_qkv', 'a_log', 'dt_bias', 'head_norm', 'w_pool', 'pool_scale', 'w_out', 'norm_ffn', 'w_up', 'conv_ffn', 'w_down', 'norm_final', 'loss_target', 'm_meta_tokens', 'm_norm_mix', 'm_w_in', 'm_conv_qkv', 'm_a_log', 'm_dt_bias', 'm_head_norm', 'm_w_pool', 'm_pool_scale', 'm_w_out', 'm_norm_ffn', 'm_w_up', 'm_conv_ffn', 'm_w_down', 'm_norm_final', 'v_meta_tokens', 'v_norm_mix', 'v_w_in', 'v_conv_qkv', 'v_a_log', 'v_dt_bias', 'v_head_norm', 'v_w_pool', 'v_pool_scale', 'v_w_out', 'v_norm_ffn', 'v_w_up', 'v_conv_ffn', 'v_w_down', 'v_norm_final']
TWIN_OUTPUTS = ['loss', 'grad_x', 'grad_meta_tokens', 'grad_norm_mix', 'grad_w_in', 'grad_conv_qkv', 'grad_a_log', 'grad_dt_bias', 'grad_head_norm', 'grad_w_pool', 'grad_pool_scale', 'grad_w_out', 'grad_norm_ffn', 'grad_w_up', 'grad_conv_ffn', 'grad_w_down', 'grad_norm_final', 'delta_meta_tokens', 'delta_norm_mix', 'delta_w_in', 'delta_conv_qkv', 'delta_a_log', 'delta_dt_bias', 'delta_head_norm', 'delta_w_pool', 'delta_pool_scale', 'delta_w_out', 'delta_norm_ffn', 'delta_w_up', 'delta_conv_ffn', 'delta_w_down', 'delta_norm_final', 'new_m_meta_tokens', 'new_m_norm_mix', 'new_m_w_in', 'new_m_conv_qkv', 'new_m_a_log', 'new_m_dt_bias', 'new_m_head_norm', 'new_m_w_pool', 'new_m_pool_scale', 'new_m_w_out', 'new_m_norm_ffn', 'new_m_w_up', 'new_m_conv_ffn', 'new_m_w_down', 'new_m_norm_final', 'new_v_meta_tokens', 'new_v_norm_mix', 'new_v_w_in', 'new_v_conv_qkv', 'new_v_a_log', 'new_v_dt_bias', 'new_v_head_norm', 'new_v_w_pool', 'new_v_pool_scale', 'new_v_w_out', 'new_v_norm_ffn', 'new_v_w_up', 'new_v_conv_ffn', 'new_v_w_down', 'new_v_norm_final']
TWIN_LEAF_KINDS = {'loss': 'loss', 'grad_x': 'grad_x', 'grad_meta_tokens': 'grad_w', 'grad_norm_mix': 'grad_w', 'grad_w_in': 'grad_w', 'grad_conv_qkv': 'grad_w', 'grad_a_log': 'grad_w', 'grad_dt_bias': 'grad_w', 'grad_head_norm': 'grad_w', 'grad_w_pool': 'grad_w', 'grad_pool_scale': 'grad_w', 'grad_w_out': 'grad_w', 'grad_norm_ffn': 'grad_w', 'grad_w_up': 'grad_w', 'grad_conv_ffn': 'grad_w', 'grad_w_down': 'grad_w', 'grad_norm_final': 'grad_w', 'delta_meta_tokens': 'delta_w', 'delta_norm_mix': 'delta_w', 'delta_w_in': 'delta_w', 'delta_conv_qkv': 'delta_w', 'delta_a_log': 'delta_w', 'delta_dt_bias': 'delta_w', 'delta_head_norm': 'delta_w', 'delta_w_pool': 'delta_w', 'delta_pool_scale': 'delta_w', 'delta_w_out': 'delta_w', 'delta_norm_ffn': 'delta_w', 'delta_w_up': 'delta_w', 'delta_conv_ffn': 'delta_w', 'delta_w_down': 'delta_w', 'delta_norm_final': 'delta_w', 'new_m_meta_tokens': 'new_m', 'new_m_norm_mix': 'new_m', 'new_m_w_in': 'new_m', 'new_m_conv_qkv': 'new_m', 'new_m_a_log': 'new_m', 'new_m_dt_bias': 'new_m', 'new_m_head_norm': 'new_m', 'new_m_w_pool': 'new_m', 'new_m_pool_scale': 'new_m', 'new_m_w_out': 'new_m', 'new_m_norm_ffn': 'new_m', 'new_m_w_up': 'new_m', 'new_m_conv_ffn': 'new_m', 'new_m_w_down': 'new_m', 'new_m_norm_final': 'new_m', 'new_v_meta_tokens': 'new_v', 'new_v_norm_mix': 'new_v', 'new_v_w_in': 'new_v', 'new_v_conv_qkv': 'new_v', 'new_v_a_log': 'new_v', 'new_v_dt_bias': 'new_v', 'new_v_head_norm': 'new_v', 'new_v_w_pool': 'new_v', 'new_v_pool_scale': 'new_v', 'new_v_w_out': 'new_v', 'new_v_norm_ffn': 'new_v', 'new_v_w_up': 'new_v', 'new_v_conv_ffn': 'new_v', 'new_v_w_down': 'new_v', 'new_v_norm_final': 'new_v'}


def _forward(args):
    return _fwd_reference(*[args[k] for k in FWD_PARAMS])


def _output_shape():
    def fwd():
        inp = _fwd_setup_inputs(0)
        return _fwd_reference(*[inp[k] for k in FWD_PARAMS])
    out = _jax.eval_shape(fwd)
    return out.shape, out.dtype

N_MICROBATCH = 1
ADAM_LR = 0.001
ADAM_B1 = 0.9
ADAM_B2 = 0.999
ADAM_EPS = 1e-08
ADAM_WD = 0.01
ADAM_STEP = 10
PER_EXAMPLE_BATCH_AXIS = {'x': 0, 'loss_target': 0}
SHARED_INPUTS = []
_WEIGHT_DTYPES = {'meta_tokens': _jnp.float32, 'norm_mix': _jnp.float32, 'w_in': _jnp.float32, 'conv_qkv': _jnp.float32, 'a_log': _jnp.float32, 'dt_bias': _jnp.float32, 'head_norm': _jnp.float32, 'w_pool': _jnp.float32, 'pool_scale': _jnp.float32, 'w_out': _jnp.float32, 'norm_ffn': _jnp.float32, 'w_up': _jnp.float32, 'conv_ffn': _jnp.float32, 'w_down': _jnp.float32, 'norm_final': _jnp.float32}
MOMENT_SCALE = {'meta_tokens': 6.827511e-03, 'norm_mix': 1.856119e-01, 'w_in': 6.972444e-02, 'conv_qkv': 5.815552e-02, 'a_log': 2.873031e-01, 'dt_bias': 2.751394e-01, 'head_norm': 2.183512e-01, 'w_pool': 1.126472e-01, 'pool_scale': 1.145057e-01, 'w_out': 1.352635e-01, 'norm_ffn': 1.784914e-01, 'w_up': 7.338687e-02, 'conv_ffn': 7.298906e-02, 'w_down': 1.199048e-01, 'norm_final': 6.398148e+01}


def _to_microbatches(a, axis):
    t = _jnp.moveaxis(a, axis, 0)
    t = t.reshape((N_MICROBATCH, t.shape[0] // N_MICROBATCH) + t.shape[1:])
    return _jnp.moveaxis(t, 1, axis + 1)


def setup_inputs(seed: int = 0) -> dict:
    inp = _fwd_setup_inputs(seed)
    key = _jax.random.fold_in(_jax.random.key(seed), 7919)
    shape, _ = _output_shape()
    out = dict(inp)
    out["loss_target"] = _jax.random.normal(_jax.random.fold_in(key, 0), shape, _jnp.float32)
    for i, name in enumerate(TWIN_WEIGHTS):
        w = inp[name].astype(_jnp.float32)
        if MOMENT_SCALE is None:
            s = _jnp.sqrt(_jnp.mean(_jnp.square(w)) + 1e-30)
        else:
            s = MOMENT_SCALE[name]
        km, kv = _jax.random.split(_jax.random.fold_in(key, i + 1))
        out[name] = w
        out["m_" + name] = s * _jax.random.normal(km, w.shape, _jnp.float32)
        out["v_" + name] = (s * s) * _jax.random.uniform(kv, w.shape, _jnp.float32, 0.5, 1.5)
    if N_MICROBATCH > 1:
        for name, axis in PER_EXAMPLE_BATCH_AXIS.items():
            out[name] = _to_microbatches(out[name], axis)
    return {'x': out['x'], 'meta_tokens': out['meta_tokens'], 'norm_mix': out['norm_mix'], 'w_in': out['w_in'], 'conv_qkv': out['conv_qkv'], 'a_log': out['a_log'], 'dt_bias': out['dt_bias'], 'head_norm': out['head_norm'], 'w_pool': out['w_pool'], 'pool_scale': out['pool_scale'], 'w_out': out['w_out'], 'norm_ffn': out['norm_ffn'], 'w_up': out['w_up'], 'conv_ffn': out['conv_ffn'], 'w_down': out['w_down'], 'norm_final': out['norm_final'], 'loss_target': out['loss_target'], 'm_meta_tokens': out['m_meta_tokens'], 'm_norm_mix': out['m_norm_mix'], 'm_w_in': out['m_w_in'], 'm_conv_qkv': out['m_conv_qkv'], 'm_a_log': out['m_a_log'], 'm_dt_bias': out['m_dt_bias'], 'm_head_norm': out['m_head_norm'], 'm_w_pool': out['m_w_pool'], 'm_pool_scale': out['m_pool_scale'], 'm_w_out': out['m_w_out'], 'm_norm_ffn': out['m_norm_ffn'], 'm_w_up': out['m_w_up'], 'm_conv_ffn': out['m_conv_ffn'], 'm_w_down': out['m_w_down'], 'm_norm_final': out['m_norm_final'], 'v_meta_tokens': out['v_meta_tokens'], 'v_norm_mix': out['v_norm_mix'], 'v_w_in': out['v_w_in'], 'v_conv_qkv': out['v_conv_qkv'], 'v_a_log': out['v_a_log'], 'v_dt_bias': out['v_dt_bias'], 'v_head_norm': out['v_head_norm'], 'v_w_pool': out['v_w_pool'], 'v_pool_scale': out['v_pool_scale'], 'v_w_out': out['v_w_out'], 'v_norm_ffn': out['v_norm_ffn'], 'v_w_up': out['v_w_up'], 'v_conv_ffn': out['v_conv_ffn'], 'v_w_down': out['v_w_down'], 'v_norm_final': out['v_norm_final']}


def _loss(weights, diff, rest, loss_target):
    with _jax.named_scope("forward"):
        args = {**rest, TWIN_DIFF_INPUT: diff, **{k: w.astype(_WEIGHT_DTYPES[k]) for k, w in weights.items()}}
        y = _forward(args)
    with _jax.named_scope("loss_head"):
        err = _jnp.square(y.astype(_jnp.float32) - loss_target)
        return 0.5 * _jnp.sum(_jnp.mean(err, axis=-1)) if err.ndim else 0.5 * err


def _adamw(w, g, m, v):
    m = ADAM_B1 * m + (1.0 - ADAM_B1) * g
    v = ADAM_B2 * v + (1.0 - ADAM_B2) * _jnp.square(g)
    m_hat = m / (1.0 - ADAM_B1 ** ADAM_STEP)
    v_hat = v / (1.0 - ADAM_B2 ** ADAM_STEP)
    delta = -ADAM_LR * (m_hat / (_jnp.sqrt(v_hat) + ADAM_EPS) + ADAM_WD * w)
    return delta, m, v


def reference(x, meta_tokens, norm_mix, w_in, conv_qkv, a_log, dt_bias, head_norm, w_pool, pool_scale, w_out, norm_ffn, w_up, conv_ffn, w_down, norm_final, loss_target, m_meta_tokens, m_norm_mix, m_w_in, m_conv_qkv, m_a_log, m_dt_bias, m_head_norm, m_w_pool, m_pool_scale, m_w_out, m_norm_ffn, m_w_up, m_conv_ffn, m_w_down, m_norm_final, v_meta_tokens, v_norm_mix, v_w_in, v_conv_qkv, v_a_log, v_dt_bias, v_head_norm, v_w_pool, v_pool_scale, v_w_out, v_norm_ffn, v_w_up, v_conv_ffn, v_w_down, v_norm_final):
    given = dict(x=x, meta_tokens=meta_tokens, norm_mix=norm_mix, w_in=w_in, conv_qkv=conv_qkv, a_log=a_log, dt_bias=dt_bias, head_norm=head_norm, w_pool=w_pool, pool_scale=pool_scale, w_out=w_out, norm_ffn=norm_ffn, w_up=w_up, conv_ffn=conv_ffn, w_down=w_down, norm_final=norm_final, loss_target=loss_target, m_meta_tokens=m_meta_tokens, m_norm_mix=m_norm_mix, m_w_in=m_w_in, m_conv_qkv=m_conv_qkv, m_a_log=m_a_log, m_dt_bias=m_dt_bias, m_head_norm=m_head_norm, m_w_pool=m_w_pool, m_pool_scale=m_pool_scale, m_w_out=m_w_out, m_norm_ffn=m_norm_ffn, m_w_up=m_w_up, m_conv_ffn=m_conv_ffn, m_w_down=m_w_down, m_norm_final=m_norm_final, v_meta_tokens=v_meta_tokens, v_norm_mix=v_norm_mix, v_w_in=v_w_in, v_conv_qkv=v_conv_qkv, v_a_log=v_a_log, v_dt_bias=v_dt_bias, v_head_norm=v_head_norm, v_w_pool=v_w_pool, v_pool_scale=v_pool_scale, v_w_out=v_w_out, v_norm_ffn=v_norm_ffn, v_w_up=v_w_up, v_conv_ffn=v_conv_ffn, v_w_down=v_w_down, v_norm_final=v_norm_final)
    weights = {n: given[n] for n in TWIN_WEIGHTS}
    shared = {n: given[n] for n in SHARED_INPUTS}
    per_example = {n: given[n] for n in ['x']}
    grad_fn = _jax.value_and_grad(_loss, argnums=(0, 1))

    def one_microbatch(ex, loss_target):
        ex = dict(ex)
        diff = ex.pop(TWIN_DIFF_INPUT)
        return grad_fn(weights, diff, {**shared, **ex}, loss_target)

    if N_MICROBATCH == 1:
        loss, (grad_w, grad_x) = one_microbatch(per_example, given["loss_target"])
    else:
        def body(carry, xs):
            loss_sum, grad_sum = carry
            l_k, (gw_k, gx_k) = one_microbatch(xs[0], xs[1])
            with _jax.named_scope("update"):
                return (loss_sum + l_k, _jax.tree.map(_jnp.add, grad_sum, gw_k)), gx_k

        init = (_jnp.zeros((), _jnp.float32), _jax.tree.map(_jnp.zeros_like, weights))
        (loss, grad_w), grad_x = _jax.lax.scan(body, init, (per_example, given["loss_target"]))
    with _jax.named_scope("update"):
        delta_w, new_m, new_v = {}, {}, {}
        for n in TWIN_WEIGHTS:
            delta_w[n], new_m[n], new_v[n] = _adamw(weights[n], grad_w[n], given["m_" + n], given["v_" + n])
    return (loss, grad_x, *[grad_w[n] for n in TWIN_WEIGHTS], *[delta_w[n] for n in TWIN_WEIGHTS],
            *[new_m[n] for n in TWIN_WEIGHTS], *[new_v[n] for n in TWIN_WEIGHTS])
```

```python
import functools

import jax
import jax.numpy as jnp
from jax import lax
from jax.experimental import pallas as pl
from jax.experimental.pallas import tpu as pltpu

F32 = jnp.float32
BF16 = jnp.bfloat16
HI = lax.Precision.HIGHEST
MESH = pl.DeviceIdType.MESH

D = 1024
H = 8
DH = 128
CH = 64
N_META = 16
LEAD = 48
ROW0 = LEAD + N_META
QKV = 3 * D
POOL_W = 512
POOL_WINDOWS = (2, 4, 8, 16)
FB = 704
NDEV = 8
EPS = 1e-6
MM_TILES = 12
TE = 192
LANE = 128
SUB = 8
VMEM_CAP = 56 << 20

ADAM_LR, ADAM_B1, ADAM_B2, ADAM_EPS, ADAM_WD, ADAM_STEP = 0.001, 0.9, 0.999, 1e-08, 0.01, 10

_NN = (((1,), (0,)), ((), ()))
_NT = (((1,), (1,)), ((), ()))
_TN = (((0,), (0,)), ((), ()))


def _dot(a, b, dims=_NN, precision=None):
    return lax.dot_general(a, b, dims, precision=precision, preferred_element_type=F32)


def _bdot(a, b, dims=_NN):
    return _dot(a.astype(BF16), b.astype(BF16), dims)


def _nbytes(shape, dtype):
    n = 1
    for s in shape:
        n *= s
    return n * jnp.dtype(dtype).itemsize


def _params(sem, block_bytes):
    limit = min(VMEM_CAP, 2 * block_bytes + (20 << 20))
    return pltpu.CompilerParams(dimension_semantics=sem, vmem_limit_bytes=limit)


def _sigmoid(x):
    return 1.0 / (1.0 + jnp.exp(-x))


def _col_tile(n):
    for t in (1024, 512, 256, 128):
        if n % t == 0:
            return t
    return n


def _matmul(a, b, *, dims, grid, a_spec, b_spec, o_spec, out_shape, name, red_axis=None, res=None):
    def body(*refs):
        if res is None:
            a_ref, b_ref, o_ref = refs
        else:
            a_ref, b_ref, r_ref, o_ref = refs
        part = _dot(a_ref[...], b_ref[...], dims)
        if red_axis is None:
            if res is not None:
                part = part + r_ref[...]
            o_ref[...] = part.astype(o_ref.dtype)
        else:
            r = pl.program_id(red_axis)

            @pl.when(r == 0)
            def _():
                o_ref[...] = part + r_ref[...] if res is not None else part

            @pl.when(r > 0)
            def _():
                o_ref[...] += part

    def blk(spec, arr):
        return _nbytes([s for s in spec.block_shape if s is not None], arr.dtype)

    ins = [a, b] + ([res] if res is not None else [])
    specs = [a_spec, b_spec] + ([o_spec] if res is not None else [])
    nb = blk(a_spec, a) + blk(b_spec, b) + 2 * _nbytes([s for s in o_spec.block_shape if s is not None], F32)
    sem = tuple("arbitrary" if i == red_axis else "parallel" for i in range(len(grid)))
    return pl.pallas_call(
        body, name=name, grid=grid, in_specs=specs, out_specs=o_spec, out_shape=out_shape,
        compiler_params=_params(sem, nb),
    )(*ins)


def _mm(a, b, out_dtype, name, res=None, dims=_NN):
    m, k = a.shape
    n = b.shape[1] if dims == _NN else b.shape[0]
    tm, tn = m // MM_TILES, _col_tile(n)
    if dims == _NN:
        b_spec = pl.BlockSpec((k, tn), lambda j, i: (0, j))
    else:
        b_spec = pl.BlockSpec((tn, k), lambda j, i: (j, 0))
    return _matmul(
        a, b, dims=dims, grid=(n // tn, MM_TILES), a_spec=pl.BlockSpec((tm, k), lambda j, i: (i, 0)), b_spec=b_spec,
        o_spec=pl.BlockSpec((tm, tn), lambda j, i: (i, j)), out_shape=jax.ShapeDtypeStruct((m, n), out_dtype),
        name=name, res=res)


def _mm_tn(a, g, name):
    m, k = a.shape
    n = g.shape[1]
    tm, tn = m // MM_TILES, _col_tile(n)
    return _matmul(
        a, g, dims=_TN, grid=(n // tn, MM_TILES), red_axis=1, a_spec=pl.BlockSpec((tm, k), lambda j, i: (i, 0)),
        b_spec=pl.BlockSpec((tm, tn), lambda j, i: (i, j)), o_spec=pl.BlockSpec((k, tn), lambda j, i: (0, j)),
        out_shape=jax.ShapeDtypeStruct((k, n), F32), name=name)


def _mm_up(u, w_up, name):
    t = u.shape[0]
    g = w_up.shape[0]
    tm = t // MM_TILES
    return _matmul(
        u, w_up, dims=_NN, grid=(g, MM_TILES), a_spec=pl.BlockSpec((tm, D), lambda g_, i: (i, 0)),
        b_spec=pl.BlockSpec((None, D, FB), lambda g_, i: (g_, 0, 0)),
        o_spec=pl.BlockSpec((None, tm, FB), lambda g_, i: (g_, i, 0)),
        out_shape=jax.ShapeDtypeStruct((g, t, FB), BF16), name=name)


def _mm_blocks_red(a, b, name, dims, res=None):
    g, t, k = a.shape
    n = b.shape[2] if dims == _NN else b.shape[1]
    tm = t // MM_TILES
    return _matmul(
        a, b, dims=dims, grid=(MM_TILES, g), red_axis=1, a_spec=pl.BlockSpec((None, tm, k), lambda i, g_: (g_, i, 0)),
        b_spec=pl.BlockSpec((None,) + b.shape[1:], lambda i, g_: (g_, 0, 0)),
        o_spec=pl.BlockSpec((tm, n), lambda i, g_: (i, 0)), out_shape=jax.ShapeDtypeStruct((t, n), F32),
        name=name, res=res)


def _mm_to_blocks(a, b, name):
    t, k = a.shape
    g, n, _ = b.shape
    tm = t // MM_TILES
    return _matmul(
        a, b, dims=_NT, grid=(g, MM_TILES), a_spec=pl.BlockSpec((tm, k), lambda g_, i: (i, 0)),
        b_spec=pl.BlockSpec((None, n, k), lambda g_, i: (g_, 0, 0)),
        o_spec=pl.BlockSpec((None, tm, n), lambda g_, i: (g_, i, 0)),
        out_shape=jax.ShapeDtypeStruct((g, t, n), BF16), name=name)


def _mm_tn_blocks(a, g, name, a_blocked, g_blocked):
    nb = a.shape[0] if a_blocked else g.shape[0]
    t = a.shape[-2]
    k, n = a.shape[-1], g.shape[-1]
    tm = t // MM_TILES
    a_spec = (pl.BlockSpec((None, tm, k), lambda g_, i: (g_, i, 0)) if a_blocked
              else pl.BlockSpec((tm, k), lambda g_, i: (i, 0)))
    g_spec = (pl.BlockSpec((None, tm, n), lambda g_, i: (g_, i, 0)) if g_blocked
              else pl.BlockSpec((tm, n), lambda g_, i: (i, 0)))
    return _matmul(
        a, g, dims=_TN, grid=(nb, MM_TILES), red_axis=1, a_spec=a_spec, b_spec=g_spec,
        o_spec=pl.BlockSpec((None, k, n), lambda g_, i: (g_, 0, 0)),
        out_shape=jax.ShapeDtypeStruct((nb, k, n), F32), name=name)


def _mm_cols(a, b, out_dtype, name, dims):
    t = a.shape[0]
    g = b.shape[0]
    ka = a.shape[1] // g
    n = b.shape[2] if dims == _NN else b.shape[1]
    tm = t // MM_TILES
    return _matmul(
        a, b, dims=dims, grid=(g, MM_TILES), a_spec=pl.BlockSpec((tm, ka), lambda g_, i: (i, g_)),
        b_spec=pl.BlockSpec((None,) + b.shape[1:], lambda g_, i: (g_, 0, 0)),
        o_spec=pl.BlockSpec((tm, n), lambda g_, i: (i, g_)), out_shape=jax.ShapeDtypeStruct((t, g * n), out_dtype),
        name=name)


def _mm_tn_cols(a, g, nblk, name):
    t = a.shape[0]
    ka, n = a.shape[1] // nblk, g.shape[1] // nblk
    tm = t // MM_TILES
    return _matmul(
        a, g, dims=_TN, grid=(nblk, MM_TILES), red_axis=1, a_spec=pl.BlockSpec((tm, ka), lambda g_, i: (i, g_)),
        b_spec=pl.BlockSpec((tm, n), lambda g_, i: (i, g_)), o_spec=pl.BlockSpec((None, ka, n), lambda g_, i: (g_, 0, 0)),
        out_shape=jax.ShapeDtypeStruct((nblk, ka, n), F32), name=name)


def _rows(cols, n=None):
    if n is None:
        return pl.BlockSpec((TE, cols), lambda i: (i, 0))
    return pl.BlockSpec((TE, cols), lambda i: (n - 1 - i, 0))


def _whole(shape):
    return pl.BlockSpec(shape, lambda *_: (0,) * len(shape))


def _row_ids(i, rows=TE):
    return i * rows + lax.broadcasted_iota(jnp.int32, (rows, 1), 0)


def _rmsnorm_fwd(h, gain, name):
    t = h.shape[0]

    def body(h_ref, g_ref, u_ref):
        x = h_ref[...]
        r = lax.rsqrt(jnp.mean(x * x, axis=-1, keepdims=True) + EPS)
        u_ref[...] = (x * r * g_ref[...]).astype(BF16)

    return pl.pallas_call(
        body, name=name, grid=(t // TE,), in_specs=[_rows(D), _whole((1, D))], out_specs=_rows(D),
        out_shape=jax.ShapeDtypeStruct((t, D), BF16), compiler_params=_params(("parallel",), 3 * TE * D * 4),
    )(h, gain)


def _rmsnorm_bwd(x, du, dres, gain, name):
    t = x.shape[0]

    def body(x_ref, du_ref, dr_ref, g_ref, dx_ref, dxb_ref, dg_ref):
        i = pl.program_id(0)
        xv = x_ref[...]
        r = lax.rsqrt(jnp.mean(xv * xv, axis=-1, keepdims=True) + EPS)
        gdy = du_ref[...] * g_ref[...]
        dx = dr_ref[...] + r * gdy - xv * (r * r * r) * jnp.mean(xv * gdy, axis=-1, keepdims=True)
        dx = jnp.where(_row_ids(i) >= LEAD, dx, 0.0)
        dx_ref[...] = dx
        dxb_ref[...] = dx.astype(BF16)
        part = jnp.sum(du_ref[...] * xv * r, axis=0, keepdims=True)

        @pl.when(i == 0)
        def _():
            dg_ref[...] = part

        @pl.when(i > 0)
        def _():
            dg_ref[...] += part

    return pl.pallas_call(
        body, name=name, grid=(t // TE,), in_specs=[_rows(D), _rows(D), _rows(D), _whole((1, D))],
        out_specs=[_rows(D), _rows(D), _whole((1, D))],
        out_shape=[jax.ShapeDtypeStruct((t, D), F32), jax.ShapeDtypeStruct((t, D), BF16),
                   jax.ShapeDtypeStruct((1, D), F32)],
        compiler_params=_params(("arbitrary",), 5 * TE * D * 4),
    )(x, du, dres, gain)


def _loss_bwd(h, target, gain, name):
    t = h.shape[0]

    def body(h_ref, t_ref, g_ref, dx_ref, dxb_ref, dg_ref, loss_ref):
        i = pl.program_id(0)
        xv = h_ref[...]
        gain_v = g_ref[...]
        r = lax.rsqrt(jnp.mean(xv * xv, axis=-1, keepdims=True) + EPS)
        real = _row_ids(i) >= ROW0
        err = jnp.where(real, xv * r * gain_v - t_ref[...], 0.0)
        dy = err * (1.0 / D)
        gdy = dy * gain_v
        dx = r * gdy - xv * (r * r * r) * jnp.mean(xv * gdy, axis=-1, keepdims=True)
        dx_ref[...] = dx
        dxb_ref[...] = dx.astype(BF16)
        dgp = jnp.sum(dy * xv * r, axis=0, keepdims=True)
        lp = 0.5 * jnp.sum(jnp.mean(err * err, axis=-1, keepdims=True), axis=0, keepdims=True)

        @pl.when(i == 0)
        def _():
            dg_ref[...] = dgp
            loss_ref[...] = jnp.broadcast_to(lp, (1, LANE))

        @pl.when(i > 0)
        def _():
            dg_ref[...] += dgp
            loss_ref[...] += jnp.broadcast_to(lp, (1, LANE))

    return pl.pallas_call(
        body, name=name, grid=(t // TE,), in_specs=[_rows(D), _rows(D), _whole((1, D))],
        out_specs=[_rows(D), _rows(D), _whole((1, D)), _whole((1, LANE))],
        out_shape=[jax.ShapeDtypeStruct((t, D), F32), jax.ShapeDtypeStruct((t, D), BF16),
                   jax.ShapeDtypeStruct((1, D), F32), jax.ShapeDtypeStruct((1, LANE), F32)],
        compiler_params=_params(("arbitrary",), 4 * TE * D * 4),
    )(h, target, gain)


def _conv4(xs, w, n):
    return (w[3:4] * xs[SUB:SUB + n] + w[2:3] * xs[SUB - 1:SUB - 1 + n]
            + w[1:2] * xs[SUB - 2:SUB - 2 + n] + w[0:1] * xs[SUB - 3:SUB - 3 + n])


def _gdn_pre_fwd(x, w, name):
    t = x.shape[0]

    def body(x_ref, w_ref, o_ref, carry):
        @pl.when(pl.program_id(0) == 0)
        def _():
            carry[...] = jnp.zeros_like(carry)

        for hh in range(3 * H):
            sl = slice(hh * DH, (hh + 1) * DH)
            xv = x_ref[:, sl].astype(F32)
            xs = jnp.concatenate([carry[:, sl], xv], axis=0)
            cv = _conv4(xs, w_ref[:, sl], TE)
            s = cv * _sigmoid(cv)
            if hh < 2 * H:
                s = s * lax.rsqrt(jnp.sum(s * s, axis=-1, keepdims=True) + EPS)
                if hh < H:
                    s = s * (DH ** -0.5)
            o_ref[:, sl] = s
            carry[:, sl] = xv[TE - SUB:TE]

    return pl.pallas_call(
        body, name=name, grid=(t // TE,), in_specs=[_rows(QKV), _whole((4, QKV))], out_specs=_rows(QKV),
        out_shape=jax.ShapeDtypeStruct((t, QKV), F32), scratch_shapes=[pltpu.VMEM((SUB, QKV), F32)],
        compiler_params=_params(("arbitrary",), TE * QKV * 6),
    )(x, w)


def _gdn_pre_bwd(x, w, dqkv, name):
    t = x.shape[0]
    n = t // TE
    hb = TE // 16

    def body(x_ref, xp_ref, w_ref, d_ref, dx_ref, dw_ref, carry):
        i = pl.program_id(0)

        @pl.when(i == 0)
        def _():
            carry[...] = jnp.zeros_like(carry)
            dw_ref[...] = jnp.zeros_like(dw_ref)

        first_tile = i == n - 1
        for hh in range(3 * H):
            sl = slice(hh * DH, (hh + 1) * DH)
            xv = x_ref[:, sl].astype(F32)
            prev = jnp.where(first_tile, 0.0, xp_ref[SUB:2 * SUB, sl].astype(F32))
            xs = jnp.concatenate([prev, xv], axis=0)
            wv = w_ref[:, sl]
            cv = _conv4(xs, wv, TE)
            sg = _sigmoid(cv)
            s = cv * sg
            ds = d_ref[:, sl]
            if hh < 2 * H:
                if hh < H:
                    ds = ds * (DH ** -0.5)
                r = lax.rsqrt(jnp.sum(s * s, axis=-1, keepdims=True) + EPS)
                ds = r * ds - s * (r * r * r) * jnp.sum(s * ds, axis=-1, keepdims=True)
            dcv = ds * (sg * (1.0 + cv * (1.0 - sg)))
            ext = jnp.concatenate([dcv, carry[:, sl]], axis=0)
            dx = (wv[3:4] * ext[0:TE] + wv[2:3] * ext[1:TE + 1] + wv[1:2] * ext[2:TE + 2] + wv[0:1] * ext[3:TE + 3])
            dx_ref[:, sl] = dx.astype(BF16)
            dwp = jnp.concatenate(
                [jnp.sum(xs[SUB - 3 + j:SUB - 3 + j + TE] * dcv, axis=0, keepdims=True) for j in range(4)], axis=0)
            dw_ref[:, sl] += dwp
            carry[:, sl] = dcv[0:SUB]

    return pl.pallas_call(
        body, name=name, grid=(n,),
        in_specs=[_rows(QKV, n), pl.BlockSpec((16, QKV), lambda i: (jnp.maximum((n - 1 - i) * hb - 1, 0), 0)),
                  _whole((4, QKV)), _rows(QKV, n)],
        out_specs=[_rows(QKV, n), _whole((4, QKV))],
        out_shape=[jax.ShapeDtypeStruct((t, QKV), BF16), jax.ShapeDtypeStruct((4, QKV), F32)],
        scratch_shapes=[pltpu.VMEM((SUB, QKV), F32)], compiler_params=_params(("arbitrary",), TE * QKV * 10),
    )(x, x, w, dqkv)


def _softplus(x):
    return jnp.maximum(x, 0.0) + jnp.log(1.0 + jnp.exp(-jnp.abs(x)))


def _gates_fwd(ba, a_row, dt_row, name):
    t = ba.shape[0]

    def body(ba_ref, a_ref, dt_ref, b_out, g_out):
        real = _row_ids(pl.program_id(0)) >= LEAD
        b_out[...] = jnp.where(real, _sigmoid(ba_ref[:, 0:LANE]), 0.0)
        g = -jnp.exp(a_ref[...]) * _softplus(ba_ref[:, LANE:2 * LANE] + dt_ref[...])
        g_out[...] = jnp.where(real, g, 0.0)

    return pl.pallas_call(
        body, name=name, grid=(t // TE,), in_specs=[_rows(2 * LANE), _whole((1, LANE)), _whole((1, LANE))],
        out_specs=[_rows(LANE), _rows(LANE)],
        out_shape=[jax.ShapeDtypeStruct((t, LANE), F32), jax.ShapeDtypeStruct((t, LANE), F32)],
        compiler_params=_params(("parallel",), TE * LANE * 16),
    )(ba, a_row, dt_row)


def _gates_bwd(ba, a_row, dt_row, dbeta, dg, name):
    t = ba.shape[0]

    def body(ba_ref, a_ref, dt_ref, db_ref, dg_ref, dba_ref, da_out, ddt_out):
        i = pl.program_id(0)
        real = _row_ids(i) >= LEAD
        beta = _sigmoid(ba_ref[:, 0:LANE])
        draw_b = jnp.where(real, db_ref[...] * beta * (1.0 - beta), 0.0)
        pre = ba_ref[:, LANE:2 * LANE] + dt_ref[...]
        neg_a = -jnp.exp(a_ref[...])
        dgv = jnp.where(real, dg_ref[...], 0.0)
        draw_a = dgv * neg_a * _sigmoid(pre)
        dba_ref[:, 0:LANE] = draw_b.astype(BF16)
        dba_ref[:, LANE:2 * LANE] = draw_a.astype(BF16)
        dal = jnp.sum(dgv * neg_a * _softplus(pre), axis=0, keepdims=True)
        ddt = jnp.sum(draw_a, axis=0, keepdims=True)

        @pl.when(i == 0)
        def _():
            da_out[...] = dal
            ddt_out[...] = ddt

        @pl.when(i > 0)
        def _():
            da_out[...] += dal
            ddt_out[...] += ddt

    return pl.pallas_call(
        body, name=name, grid=(t // TE,),
        in_specs=[_rows(2 * LANE), _whole((1, LANE)), _whole((1, LANE)), _rows(LANE), _rows(LANE)],
        out_specs=[_rows(2 * LANE), _whole((1, LANE)), _whole((1, LANE))],
        out_shape=[jax.ShapeDtypeStruct((t, 2 * LANE), BF16), jax.ShapeDtypeStruct((1, LANE), F32),
                   jax.ShapeDtypeStruct((1, LANE), F32)],
        compiler_params=_params(("arbitrary",), TE * LANE * 24),
    )(ba, a_row, dt_row, dbeta, dg)


def _chunk_common(g_ref, b_ref):
    row = lax.broadcasted_iota(jnp.int32, (CH, CH), 0)
    col = lax.broadcasted_iota(jnp.int32, (CH, CH), 1)
    incl = row >= col
    gc = _dot(incl.astype(F32), g_ref[...], precision=HI)
    glast = gc[CH - 1:CH, :]
    return dict(row=row, col=col, incl=incl, strict=row > col, gc=gc, gct=gc.T, eg=jnp.exp(gc),
                ek=jnp.exp(glast - gc), gt=jnp.exp(glast), bb=b_ref[...])


def _chunk_head(cm, qkv_ref, h):
    q = qkv_ref[:, h * DH:(h + 1) * DH]
    k = qkv_ref[:, D + h * DH:D + (h + 1) * DH]
    v = qkv_ref[:, 2 * D + h * DH:2 * D + (h + 1) * DH]
    bcol = cm["bb"][:, h:h + 1]
    dec = jnp.exp(jnp.where(cm["incl"], cm["gc"][:, h:h + 1] - cm["gct"][h:h + 1, :], -1e30))
    kb = k * bcol
    ab = _bdot(jnp.concatenate([kb, q], axis=0), k, _NT)
    lm = jnp.where(cm["strict"], ab[:CH] * dec, 0.0)
    qk = ab[CH:] * dec
    egc = cm["eg"][:, h:h + 1]
    return dict(q=q, k=k, v=v, bcol=bcol, dec=dec, kb=kb, lm=lm, qk=qk, egc=egc, ekc=cm["ek"][:, h:h + 1],
                gth=cm["gt"][:, h:h + 1], qd=q * egc, kd=k * cm["ek"][:, h:h + 1], vb=v * bcol, kbg=kb * egc)


def _unit_lower_inverse(lm, eye):
    n = -lm
    x = eye + n
    pw = _dot(n, n, precision=HI)
    for it in range(5):
        if it < 4:
            xp = _dot(jnp.concatenate([x, pw], axis=0), pw, precision=HI)
            x = x + xp[:CH]
            pw = xp[CH:]
        else:
            x = x + _dot(x, pw, precision=HI)
    return x


def _gdn_fwd(qkv, beta, g, name):
    t = qkv.shape[0]
    nc = t // CH

    def body(qkv_ref, b_ref, g_ref, o_ref, sin_ref, vn_ref, ti_ref, w_ref, state):
        @pl.when(pl.program_id(0) == 0)
        def _():
            state[...] = jnp.zeros_like(state)

        cm = _chunk_common(g_ref, b_ref)
        eye = (cm["row"] == cm["col"]).astype(F32)
        for h in range(H):
            hd = _chunk_head(cm, qkv_ref, h)
            tinv = _unit_lower_inverse(hd["lm"], eye)
            uw = _bdot(tinv, jnp.concatenate([hd["vb"], hd["kbg"]], axis=1))
            u, w = uw[:, :DH], uw[:, DH:]
            s_h = state[h]
            ws = _bdot(jnp.concatenate([w, hd["qd"]], axis=0), s_h)
            vn = u - ws[:CH]
            sl = slice(h * DH, (h + 1) * DH)
            o_ref[:, sl] = ws[CH:] + _bdot(hd["qk"], vn)
            state[h] = s_h * hd["gth"] + _bdot(hd["kd"], vn, _TN)
            sin_ref[0, h] = s_h
            vn_ref[:, sl] = vn
            w_ref[:, sl] = w
            ti_ref[0, h] = tinv

    chunk = lambda cols: pl.BlockSpec((CH, cols), lambda c: (c, 0))
    return pl.pallas_call(
        body, name=name, grid=(nc,), in_specs=[chunk(QKV), chunk(LANE), chunk(LANE)],
        out_specs=[chunk(D), pl.BlockSpec((1, H, DH, DH), lambda c: (c, 0, 0, 0)), chunk(D),
                   pl.BlockSpec((1, H, CH, CH), lambda c: (c, 0, 0, 0)), chunk(D)],
        out_shape=[jax.ShapeDtypeStruct((t, D), F32), jax.ShapeDtypeStruct((nc, H, DH, DH), F32),
                   jax.ShapeDtypeStruct((t, D), F32), jax.ShapeDtypeStruct((nc, H, CH, CH), F32),
                   jax.ShapeDtypeStruct((t, D), F32)],
        scratch_shapes=[pltpu.VMEM((H, DH, DH), F32)], compiler_params=_params(("arbitrary",), 4 << 20),
    )(qkv, beta, g)


def _gdn_bwd(qkv, beta, g, do, s_in, vnew, tinv, wsv, name):
    t = qkv.shape[0]
    nc = t // CH

    def body(qkv_ref, b_ref, g_ref, do_ref, sin_ref, vn_ref, ti_ref, w_ref, dqkv_ref, db_ref, dg_ref, dstate):
        @pl.when(pl.program_id(0) == 0)
        def _():
            dstate[...] = jnp.zeros_like(dstate)

        cm = _chunk_common(g_ref, b_ref)
        lane = lax.broadcasted_iota(jnp.int32, (CH, LANE), 1)
        rowl = lax.broadcasted_iota(jnp.int32, (CH, LANE), 0)
        ones = jnp.ones((CH, LANE), F32)
        dgc_all = jnp.zeros((CH, LANE), F32)
        dbeta_all = jnp.zeros((CH, LANE), F32)
        for h in range(H):
            hd = _chunk_head(cm, qkv_ref, h)
            sl = slice(h * DH, (h + 1) * DH)
            ti, w, vn, s_h, doh = ti_ref[0, h], w_ref[:, sl], vn_ref[:, sl], sin_ref[0, h], do_ref[:, sl]
            ds_h = dstate[h]
            dvn = _bdot(hd["kd"], ds_h) + _bdot(hd["qk"], doh, _TN)
            dkd = _bdot(vn, ds_h, _NT)
            dqd = _bdot(doh, s_h, _NT)
            dqk = _bdot(doh, vn, _NT)
            dw = -_bdot(dvn, s_h, _NT)
            dgt = jnp.sum(jnp.sum(ds_h * s_h, axis=1, keepdims=True), axis=0, keepdims=True)
            dstate[h] = ds_h * hd["gth"] + _bdot(hd["qd"], doh, _TN) - _bdot(w, dvn, _TN)
            duw = jnp.concatenate([dvn, dw], axis=1)
            dvk = _bdot(ti, duw, _TN)
            dvb, dkbg = dvk[:, :DH], dvk[:, DH:]
            dti = _bdot(duw, jnp.concatenate([hd["vb"], hd["kbg"]], axis=1), _NT)
            dl = -_dot(_dot(ti, dti, _TN, precision=HI), ti, _NT, precision=HI)
            dl = jnp.where(cm["strict"], dl, 0.0)
            dab = jnp.concatenate([dl * hd["dec"], dqk * hd["dec"]], axis=0)
            r1 = _bdot(dab, hd["k"])
            dkb = r1[:CH] + dkbg * hd["egc"]
            dq = r1[CH:] + dqd * hd["egc"]
            dk = (_bdot(dab, jnp.concatenate([hd["kb"], hd["q"]], axis=0), _TN) + dkb * hd["bcol"] + dkd * hd["ekc"])
            m = dl * hd["lm"] + dqk * hd["qk"]
            colsum = _dot(m, ones, _TN, precision=HI)[:, 0:1]
            kdsum = jnp.sum(dkd * hd["kd"], axis=1, keepdims=True)
            dgc = (jnp.sum(m, axis=1, keepdims=True) - colsum + jnp.sum(dkbg * hd["kbg"], axis=1, keepdims=True)
                   + jnp.sum(dqd * hd["qd"], axis=1, keepdims=True) - kdsum)
            dglast = jnp.sum(kdsum, axis=0, keepdims=True) + dgt * hd["gth"]
            dgc_all = jnp.where(lane == h, dgc + jnp.where(rowl == CH - 1, dglast, 0.0), dgc_all)
            dbeta = jnp.sum(dkb * hd["k"], axis=1, keepdims=True) + jnp.sum(dvb * hd["v"], axis=1, keepdims=True)
            dbeta_all = jnp.where(lane == h, dbeta, dbeta_all)
            dqkv_ref[:, sl] = dq
            dqkv_ref[:, D + h * DH:D + (h + 1) * DH] = dk
            dqkv_ref[:, 2 * D + h * DH:2 * D + (h + 1) * DH] = dvb * hd["bcol"]
        db_ref[...] = dbeta_all
        dg_ref[...] = _dot((cm["row"] <= cm["col"]).astype(F32), dgc_all, precision=HI)

    chunk = lambda cols: pl.BlockSpec((CH, cols), lambda c: (nc - 1 - c, 0))
    sq = lambda a, b: pl.BlockSpec((1, H, a, b), lambda c: (nc - 1 - c, 0, 0, 0))
    return pl.pallas_call(
        body, name=name, grid=(nc,),
        in_specs=[chunk(QKV), chunk(LANE), chunk(LANE), chunk(D), sq(DH, DH), chunk(D), sq(CH, CH), chunk(D)],
        out_specs=[chunk(QKV), chunk(LANE), chunk(LANE)],
        out_shape=[jax.ShapeDtypeStruct((t, QKV), F32), jax.ShapeDtypeStruct((t, LANE), F32),
                   jax.ShapeDtypeStruct((t, LANE), F32)],
        scratch_shapes=[pltpu.VMEM((H, DH, DH), F32)], compiler_params=_params(("arbitrary",), 6 << 20),
    )(qkv, beta, g, do, s_in, vnew, tinv, wsv)


def _pool_counts(row_ids, win):
    return jnp.minimum(jnp.maximum(row_ids - LEAD, 0) + 1, win).astype(F32)


def _pool_fwd(p, name):
    t = p.shape[0]
    ext = TE + 16

    def body(p_ref, o_ref, carry):
        i = pl.program_id(0)

        @pl.when(i == 0)
        def _():
            carry[...] = jnp.zeros_like(carry)

        ids = _row_ids(i)
        for gi, win in enumerate(POOL_WINDOWS):
            sl = slice(gi * LANE, (gi + 1) * LANE)
            xv = p_ref[:, sl]
            s = jnp.concatenate([carry[:, sl], xv], axis=0)
            sh = 1
            while sh < win:
                s = s + pltpu.roll(s, sh, 0)
                sh *= 2
            o_ref[:, sl] = (s[16:ext] / _pool_counts(ids, win) - xv).astype(BF16)
            carry[:, sl] = xv[TE - 16:TE]

    return pl.pallas_call(
        body, name=name, grid=(t // TE,), in_specs=[_rows(POOL_W)], out_specs=_rows(POOL_W),
        out_shape=jax.ShapeDtypeStruct((t, POOL_W), BF16), scratch_shapes=[pltpu.VMEM((16, POOL_W), F32)],
        compiler_params=_params(("arbitrary",), TE * POOL_W * 8),
    )(p)


def _pool_bwd(dpo, name):
    t = dpo.shape[0]
    n = t // TE
    ext = TE + 16

    def body(d_ref, o_ref, carry):
        i = pl.program_id(0)

        @pl.when(i == 0)
        def _():
            carry[...] = jnp.zeros_like(carry)

        ids = _row_ids(n - 1 - i)
        for gi, win in enumerate(POOL_WINDOWS):
            sl = slice(gi * LANE, (gi + 1) * LANE)
            dv = d_ref[:, sl]
            rv = dv / _pool_counts(ids, win)
            s = jnp.concatenate([rv, carry[:, sl]], axis=0)
            sh = 1
            while sh < win:
                s = s + pltpu.roll(s, ext - sh, 0)
                sh *= 2
            o_ref[:, sl] = (s[0:TE] - dv).astype(BF16)
            carry[:, sl] = rv[0:16]

    return pl.pallas_call(
        body, name=name, grid=(n,), in_specs=[_rows(POOL_W, n)], out_specs=_rows(POOL_W, n),
        out_shape=jax.ShapeDtypeStruct((t, POOL_W), BF16), scratch_shapes=[pltpu.VMEM((16, POOL_W), F32)],
        compiler_params=_params(("arbitrary",), TE * POOL_W * 8),
    )(dpo)


def _post_fwd(o, z, gate, pm, hn, ps, name):
    t = o.shape[0]

    def body(o_ref, z_ref, g_ref, pm_ref, hn_ref, ps_ref, y_ref):
        for h in range(H):
            sl = slice(h * DH, (h + 1) * DH)
            ov = o_ref[:, sl]
            zv = z_ref[:, sl].astype(F32)
            r = lax.rsqrt(jnp.mean(ov * ov, axis=-1, keepdims=True) + EPS)
            ya = ov * r * hn_ref[...] * (zv * _sigmoid(zv))
            ga = _sigmoid(g_ref[:, sl].astype(F32))
            gb = _sigmoid(g_ref[:, D + h * DH:D + (h + 1) * DH].astype(F32))
            y_ref[:, sl] = (ga * ya + gb * (pm_ref[:, sl] * ps_ref[:, sl])).astype(BF16)

    return pl.pallas_call(
        body, name=name, grid=(t // TE,),
        in_specs=[_rows(D), _rows(D), _rows(2 * D), _rows(D), _whole((1, DH)), _whole((1, D))], out_specs=_rows(D),
        out_shape=jax.ShapeDtypeStruct((t, D), BF16), compiler_params=_params(("parallel",), TE * D * 16),
    )(o, z, gate, pm, hn, ps)


def _post_bwd(dy, o, z, gate, pm, hn, ps, name):
    t = o.shape[0]

    def body(dy_ref, o_ref, z_ref, g_ref, pm_ref, hn_ref, ps_ref, do_ref, dz_ref, dgate_ref, dpm_ref, dhn_ref, dps_ref):
        i = pl.program_id(0)

        @pl.when(i == 0)
        def _():
            dhn_ref[...] = jnp.zeros_like(dhn_ref)
            dps_ref[...] = jnp.zeros_like(dps_ref)

        hnv = hn_ref[...]
        dhn = jnp.zeros((1, DH), F32)
        for h in range(H):
            sl = slice(h * DH, (h + 1) * DH)
            slb = slice(D + h * DH, D + (h + 1) * DH)
            dyv = dy_ref[:, sl]
            ov = o_ref[:, sl]
            zv = z_ref[:, sl].astype(F32)
            r = lax.rsqrt(jnp.mean(ov * ov, axis=-1, keepdims=True) + EPS)
            sz = _sigmoid(zv)
            silu = zv * sz
            on = ov * r
            ya = on * hnv * silu
            ga = _sigmoid(g_ref[:, sl].astype(F32))
            gb = _sigmoid(g_ref[:, slb].astype(F32))
            pmv = pm_ref[:, sl]
            psv = ps_ref[:, sl]
            dya = dyv * ga
            dyb = dyv * gb
            dgate_ref[:, sl] = (dyv * ya * ga * (1.0 - ga)).astype(BF16)
            dgate_ref[:, slb] = (dyv * (pmv * psv) * gb * (1.0 - gb)).astype(BF16)
            tt = dya * hnv * silu
            do_ref[:, sl] = r * tt - ov * (r * r * r) * jnp.mean(ov * tt, axis=-1, keepdims=True)
            dz_ref[:, sl] = (dya * on * hnv * (sz * (1.0 + zv * (1.0 - sz)))).astype(BF16)
            dhn = dhn + jnp.sum(dya * on * silu, axis=0, keepdims=True)
            dps_ref[:, sl] += jnp.sum(dyb * pmv, axis=0, keepdims=True)
            dpm_ref[:, sl] = (dyb * psv).astype(BF16)
        dhn_ref[...] += dhn

    return pl.pallas_call(
        body, name=name, grid=(t // TE,),
        in_specs=[_rows(D), _rows(D), _rows(D), _rows(2 * D), _rows(D), _whole((1, DH)), _whole((1, D))],
        out_specs=[_rows(D), _rows(D), _rows(2 * D), _rows(D), _whole((1, DH)), _whole((1, D))],
        out_shape=[jax.ShapeDtypeStruct((t, D), F32), jax.ShapeDtypeStruct((t, D), BF16),
                   jax.ShapeDtypeStruct((t, 2 * D), BF16), jax.ShapeDtypeStruct((t, D), BF16),
                   jax.ShapeDtypeStruct((1, DH), F32), jax.ShapeDtypeStruct((1, D), F32)],
        compiler_params=_params(("arbitrary",), TE * D * 28),
    )(dy, o, z, gate, pm, hn, ps)


_FB_COLS = [(c, min(c + LANE, FB)) for c in range(0, FB, LANE)]


def _conv3(xs, w, n):
    return w[2:3] * xs[SUB:SUB + n] + w[1:2] * xs[SUB - 1:SUB - 1 + n] + w[0:1] * xs[SUB - 2:SUB - 2 + n]


def _mlp_act_fwd(hid, cw, name):
    t = hid.shape[1]
    n = t // TE

    def body(hg_ref, hv_ref, wg_ref, wv_ref, a_ref, cg, cv):
        @pl.when(pl.program_id(1) == 0)
        def _():
            cg[...] = jnp.zeros_like(cg)
            cv[...] = jnp.zeros_like(cv)

        for c0, c1 in _FB_COLS:
            sl = slice(c0, c1)
            xg = hg_ref[:, sl].astype(F32)
            xv = hv_ref[:, sl].astype(F32)
            gg = _conv3(jnp.concatenate([cg[:, sl], xg], axis=0), wg_ref[:, sl], TE)
            vv = _conv3(jnp.concatenate([cv[:, sl], xv], axis=0), wv_ref[:, sl], TE)
            a_ref[:, sl] = (gg * _sigmoid(gg) * vv).astype(BF16)
            cg[:, sl] = xg[TE - SUB:TE]
            cv[:, sl] = xv[TE - SUB:TE]

    hspec = lambda off: pl.BlockSpec((None, TE, FB), lambda p, i: (p + off, i, 0))
    wspec = lambda off: pl.BlockSpec((None, 3, FB), lambda p, i: (p + off, 0, 0))
    return pl.pallas_call(
        body, name=name, grid=(4, n), in_specs=[hspec(0), hspec(4), wspec(0), wspec(4)],
        out_specs=pl.BlockSpec((None, TE, FB), lambda p, i: (p, i, 0)),
        out_shape=jax.ShapeDtypeStruct((4, t, FB), BF16),
        scratch_shapes=[pltpu.VMEM((SUB, FB), F32), pltpu.VMEM((SUB, FB), F32)],
        compiler_params=_params(("parallel", "arbitrary"), TE * FB * 8),
    )(hid, hid, cw, cw)


def _mlp_act_bwd(da, hid, cw, name):
    t = hid.shape[1]
    n = t // TE
    hb = TE // 16

    def body(da_ref, hg_ref, hv_ref, pg_ref, pv_ref, wg_ref, wv_ref, dhg_ref, dhv_ref, dwg_ref, dwv_ref, cg, cv):
        i = pl.program_id(1)

        @pl.when(i == 0)
        def _():
            cg[...] = jnp.zeros_like(cg)
            cv[...] = jnp.zeros_like(cv)
            dwg_ref[...] = jnp.zeros_like(dwg_ref)
            dwv_ref[...] = jnp.zeros_like(dwv_ref)

        first_tile = i == n - 1
        for c0, c1 in _FB_COLS:
            sl = slice(c0, c1)
            xg = jnp.concatenate([jnp.where(first_tile, 0.0, pg_ref[SUB:2 * SUB, sl].astype(F32)),
                                  hg_ref[:, sl].astype(F32)], axis=0)
            xv = jnp.concatenate([jnp.where(first_tile, 0.0, pv_ref[SUB:2 * SUB, sl].astype(F32)),
                                  hv_ref[:, sl].astype(F32)], axis=0)
            wg = wg_ref[:, sl]
            wv = wv_ref[:, sl]
            gg = _conv3(xg, wg, TE)
            vv = _conv3(xv, wv, TE)
            sg = _sigmoid(gg)
            dav = da_ref[:, sl].astype(F32)
            dgg = dav * vv * (sg * (1.0 + gg * (1.0 - sg)))
            dvv = dav * (gg * sg)
            for dc, xs, w, carry, dh_ref, dw_ref in ((dgg, xg, wg, cg, dhg_ref, dwg_ref), (dvv, xv, wv, cv, dhv_ref, dwv_ref)):
                ext = jnp.concatenate([dc, carry[:, sl]], axis=0)
                dh_ref[:, sl] = (w[2:3] * ext[0:TE] + w[1:2] * ext[1:TE + 1] + w[0:1] * ext[2:TE + 2]).astype(BF16)
                dw_ref[:, sl] += jnp.concatenate(
                    [jnp.sum(xs[SUB - 2 + j:SUB - 2 + j + TE] * dc, axis=0, keepdims=True) for j in range(3)], axis=0)
                carry[:, sl] = dc[0:SUB]

    rev = lambda off: pl.BlockSpec((None, TE, FB), lambda p, i: (p + off, n - 1 - i, 0))
    halo = lambda off: pl.BlockSpec((None, 16, FB), lambda p, i: (p + off, jnp.maximum((n - 1 - i) * hb - 1, 0), 0))
    wspec = lambda off: pl.BlockSpec((None, 3, FB), lambda p, i: (p + off, 0, 0))
    dwspec = pl.BlockSpec((None, 3, FB), lambda p, i: (p, 0, 0))
    return pl.pallas_call(
        body, name=name, grid=(4, n), in_specs=[rev(0), rev(0), rev(4), halo(0), halo(4), wspec(0), wspec(4)],
        out_specs=[rev(0), rev(0), dwspec, dwspec],
        out_shape=[jax.ShapeDtypeStruct((4, t, FB), BF16), jax.ShapeDtypeStruct((4, t, FB), BF16),
                   jax.ShapeDtypeStruct((4, 3, FB), F32), jax.ShapeDtypeStruct((4, 3, FB), F32)],
        scratch_shapes=[pltpu.VMEM((SUB, FB), F32), pltpu.VMEM((SUB, FB), F32)],
        compiler_params=_params(("parallel", "arbitrary"), TE * FB * 16),
    )(da, hid, hid, hid, hid, cw, cw)


_OFFSETS = [(dx, dy, dc) for dx in (0, 1) for dy in (0, 1) for dc in (0, 1)][1:]
NPEER = len(_OFFSETS)


def _exchange(arrs, scatter, name):
    n = len(arrs)

    def body(*refs):
        ins, outs = refs[:n], refs[n:2 * n]
        send, recv, loc = refs[2 * n:]
        x, y, c = lax.axis_index("x"), lax.axis_index("y"), lax.axis_index("c")
        me = 4 * x + 2 * y + c

        def src(k, idx):
            return ins[k].at[idx] if scatter else ins[k]

        started = []
        for k in range(n):
            own = pltpu.make_async_copy(src(k, me), outs[k].at[me], loc.at[k])
            own.start()
            started.append(own)
        remote = []
        for j, (dx, dy, dc) in enumerate(_OFFSETS):
            px, py, pc = (1 - x if dx else x), (1 - y if dy else y), (1 - c if dc else c)
            pidx = 4 * px + 2 * py + pc
            for k in range(n):
                cp = pltpu.make_async_remote_copy(
                    src_ref=src(k, pidx), dst_ref=outs[k].at[me], send_sem=send.at[k * NPEER + j],
                    recv_sem=recv.at[k * NPEER + j], device_id=(px, py, pc), device_id_type=MESH)
                cp.start()
                remote.append((cp, k, j, pidx, (px, py, pc)))
        for cp, k, j, pidx, peer in remote:
            pltpu.make_async_remote_copy(
                src_ref=src(k, pidx), dst_ref=outs[k].at[pidx], send_sem=send.at[k * NPEER + j],
                recv_sem=recv.at[k * NPEER + j], device_id=peer, device_id_type=MESH).wait_recv()
        for cp, *_ in remote:
            cp.wait_send()
        for own in started:
            own.wait()

    out_shape = [jax.ShapeDtypeStruct(a.shape if scatter else (NDEV,) + a.shape, a.dtype) for a in arrs]
    any_spec = pl.BlockSpec(memory_space=pl.ANY)
    return pl.pallas_call(
        body, name=name, in_specs=[any_spec] * n, out_specs=[any_spec] * n, out_shape=out_shape,
        scratch_shapes=[pltpu.SemaphoreType.DMA((n * NPEER,)), pltpu.SemaphoreType.DMA((n * NPEER,)),
                        pltpu.SemaphoreType.DMA((n,))],
    )(*arrs)


def _adamw(land, w, m, v, name):
    r, c = w.shape
    tr = r
    for cand in (128, 64, 56, 8):
        if r % cand == 0:
            tr = cand
            break
    c1 = 1.0 - ADAM_B1 ** ADAM_STEP
    c2 = 1.0 - ADAM_B2 ** ADAM_STEP

    def body(l_ref, w_ref, m_ref, v_ref, g_out, d_out, m_out, v_out):
        g = l_ref[0]
        for i in range(1, NDEV):
            g = g + l_ref[i]
        mn = ADAM_B1 * m_ref[...] + (1.0 - ADAM_B1) * g
        vn = ADAM_B2 * v_ref[...] + (1.0 - ADAM_B2) * (g * g)
        g_out[...] = g
        m_out[...] = mn
        v_out[...] = vn
        d_out[...] = -ADAM_LR * ((mn / c1) / (jnp.sqrt(vn / c2) + ADAM_EPS) + ADAM_WD * w_ref[...])

    spec = pl.BlockSpec((tr, c), lambda i: (i, 0))
    shp = jax.ShapeDtypeStruct((r, c), F32)
    return pl.pallas_call(
        body, name=name, grid=(r // tr,), in_specs=[pl.BlockSpec((NDEV, tr, c), lambda i: (0, i, 0)), spec, spec, spec],
        out_specs=[spec] * 4, out_shape=[shp] * 4, compiler_params=_params(("parallel",), 15 * tr * c * 4),
    )(land, w, m, v)


def _layer_fwd(h, p, tag):
    u = _rmsnorm_fwd(h, p["norm_mix"], f"norm_mix_{tag}")
    qkv_pre = _mm(u, p["w_qkv"], BF16, f"proj_qkv_{tag}")
    z = _mm(u, p["w_z"], BF16, f"proj_z_{tag}")
    ba = _mm(u, p["w_ba"], F32, f"proj_ba_{tag}")
    pool_in = _mm(u, p["w_pl"], F32, f"proj_pool_{tag}")
    gate = _mm(u, p["w_gate"], BF16, f"proj_gate_{tag}")
    qkv = _gdn_pre_fwd(qkv_pre, p["conv_qkv"], f"gdn_pre_{tag}")
    beta, g = _gates_fwd(ba, p["a_row"], p["dt_row"], f"gates_{tag}")
    o, s_in, vnew, tinv, wsv = _gdn_fwd(qkv, beta, g, f"gdn_{tag}")
    pooled = _pool_fwd(pool_in, f"pool_{tag}")
    pm = _mm_cols(pooled, p["w_pool"], F32, f"pool_mm_{tag}", _NN)
    y = _post_fwd(o, z, gate, pm, p["head_norm"], p["pool_scale"], f"post_{tag}")
    h1 = _mm(y, p["w_out"], F32, f"out_proj_{tag}", res=h)
    u2 = _rmsnorm_fwd(h1, p["norm_ffn"], f"norm_ffn_{tag}")
    hid = _mm_up(u2, p["w_up"], f"up_proj_{tag}")
    act = _mlp_act_fwd(hid, p["conv_ffn"], f"mlp_act_{tag}")
    h2 = _mm_blocks_red(act, p["w_down"], f"down_proj_{tag}", _NN, res=h1)
    saved = dict(h=h, u=u, qkv_pre=qkv_pre, z=z, ba=ba, gate=gate, qkv=qkv, beta=beta, g=g, o=o, s_in=s_in, vnew=vnew,
                 tinv=tinv, wsv=wsv, pooled=pooled, pm=pm, y=y, h1=h1, u2=u2, hid=hid, act=act)
    return h2, saved


def _layer_bwd(dh, dh_b, p, s, tag):
    gr = {}
    da = _mm_to_blocks(dh_b, p["w_down"], f"d_act_{tag}")
    gr["w_down"] = _mm_tn_blocks(s["act"], dh_b, f"dw_down_{tag}", True, False)
    dhg, dhv, dwg, dwv = _mlp_act_bwd(da, s["hid"], p["conv_ffn"], f"mlp_act_bwd_{tag}")
    gr["conv_ffn"] = jnp.concatenate([dwg, dwv], axis=0)
    w_up = p["w_up"]
    du2 = _mm_blocks_red(dhg, w_up[:4], f"d_u2g_{tag}", _NT)
    du2 = _mm_blocks_red(dhv, w_up[4:], f"d_u2v_{tag}", _NT, res=du2)
    gr["w_up"] = jnp.concatenate([_mm_tn_blocks(s["u2"], dhg, f"dw_upg_{tag}", False, True),
                                  _mm_tn_blocks(s["u2"], dhv, f"dw_upv_{tag}", False, True)], axis=0)
    dh1, dh1_b, gr["norm_ffn"] = _rmsnorm_bwd(s["h1"], du2, dh, p["norm_ffn"], f"norm_ffn_bwd_{tag}")
    dy = _mm(dh1_b, p["w_out"], F32, f"d_y_{tag}", dims=_NT)
    gr["w_out"] = _mm_tn(s["y"], dh1_b, f"dw_out_{tag}")
    do, dz, dgate, dpm, gr["head_norm"], gr["pool_scale"] = _post_bwd(
        dy, s["o"], s["z"], s["gate"], s["pm"], p["head_norm"], p["pool_scale"], f"post_bwd_{tag}")
    dpooled = _mm_cols(dpm, p["w_pool"], F32, f"d_pooled_{tag}", _NT)
    gr["w_pool"] = _mm_tn_cols(s["pooled"], dpm, 4, f"dw_pool_{tag}")
    dpool_in = _pool_bwd(dpooled, f"pool_bwd_{tag}")
    dqkv, dbeta, dg = _gdn_bwd(s["qkv"], s["beta"], s["g"], do, s["s_in"], s["vnew"], s["tinv"], s["wsv"], f"gdn_bwd_{tag}")
    dba, gr["a_log"], gr["dt_bias"] = _gates_bwd(s["ba"], p["a_row"], p["dt_row"], dbeta, dg, f"gates_bwd_{tag}")
    dqkv_pre, gr["conv_qkv"] = _gdn_pre_bwd(s["qkv_pre"], p["conv_qkv"], dqkv, f"gdn_pre_bwd_{tag}")
    du = None
    dws = []
    for nm, dseg, wseg in (("qkv", dqkv_pre, p["w_qkv"]), ("z", dz, p["w_z"]), ("ba", dba, p["w_ba"]),
                           ("pool", dpool_in, p["w_pl"]), ("gate", dgate, p["w_gate"])):
        du = _mm(dseg, wseg, F32, f"d_u_{nm}_{tag}", res=du, dims=_NT)
        dws.append(_mm_tn(s["u"], dseg, f"dw_{nm}_{tag}"))
    gr["w_in"] = jnp.concatenate([dws[0], dws[1], dws[2][:, 0:H], dws[2][:, LANE:LANE + H], dws[3], dws[4]], axis=1)
    dh0, dh0_b, gr["norm_mix"] = _rmsnorm_bwd(s["h"], du, dh1, p["norm_mix"], f"norm_mix_bwd_{tag}")
    return dh0, dh0_b, gr


def _pad_lanes(v8):
    return jnp.pad(v8.reshape(1, H), ((0, 0), (0, LANE - H)))


def _pack(parts, rows):
    flat = jnp.concatenate([q.reshape(-1) for q in parts])
    return jnp.pad(flat, (0, rows * LANE - flat.shape[0])).reshape(rows, LANE)


def _unpack(packed, shapes):
    flat = packed.reshape(-1)
    out, off = [], 0
    for shp in shapes:
        n = 1
        for s_ in shp:
            n *= s_
        out.append(flat[off:off + n].reshape(shp))
        off += n
    return out


SMALL_ROWS = 336
REPL_ROWS = 64


def kernel(x, meta_tokens, norm_mix, w_in, conv_qkv, a_log, dt_bias, head_norm, w_pool, pool_scale, w_out, norm_ffn, w_up, conv_ffn, w_down, norm_final, loss_target, m_meta_tokens, m_norm_mix, m_w_in, m_conv_qkv, m_a_log, m_dt_bias, m_head_norm, m_w_pool, m_pool_scale, m_w_out, m_norm_ffn, m_w_up, m_conv_ffn, m_w_down, m_norm_final, v_meta_tokens, v_norm_mix, v_w_in, v_conv_qkv, v_a_log, v_dt_bias, v_head_norm, v_w_pool, v_pool_scale, v_w_out, v_norm_ffn, v_w_up, v_conv_ffn, v_w_down, v_norm_final):
    seq = x.shape[1]
    t = ROW0 + seq
    assert t % (TE * 1) == 0 and t % (MM_TILES * 16) == 0 and t % CH == 0
    depth = w_in.shape[0]

    small_shapes = [conv_qkv.shape, conv_ffn.shape, w_pool.shape, meta_tokens.shape]
    small = _pack([conv_qkv, conv_ffn, w_pool, meta_tokens], SMALL_ROWS)
    g_in, g_up, g_out, g_down, g_small = _exchange(
        [w_in.astype(BF16), w_up.astype(BF16), w_out.astype(BF16), w_down.astype(BF16), small], False, "gather_weights")
    w_in_full = jnp.transpose(g_in, (1, 2, 0, 3)).reshape(depth, D, NDEV * w_in.shape[2])
    smalls = [_unpack(g_small[i], small_shapes) for i in range(NDEV)]
    conv_qkv_full = jnp.concatenate([sm[0] for sm in smalls], axis=2)
    conv_ffn_blk = jnp.stack([sm[1] for sm in smalls], axis=1)
    w_pool_full = jnp.concatenate([sm[2] for sm in smalls], axis=3).astype(BF16)
    meta_full = jnp.concatenate([sm[3] for sm in smalls], axis=1)
    zpad = jnp.zeros((D, LANE - H), BF16)
    layers = []
    for l in range(depth):
        wf = w_in_full[l]
        layers.append(dict(
            w_qkv=wf[:, 0:QKV], w_z=wf[:, QKV:QKV + D],
            w_ba=jnp.concatenate([wf[:, 4096:4104], zpad, wf[:, 4104:4112], zpad], axis=1),
            w_pl=wf[:, 4112:4624], w_gate=wf[:, 4624:6672],
            conv_qkv=conv_qkv_full[l], conv_ffn=conv_ffn_blk[l], w_pool=w_pool_full[l],
            w_out=g_out[:, l].reshape(D, D), w_up=g_up[:, l], w_down=g_down[:, l].reshape(4, FB, D),
            norm_mix=norm_mix[l].reshape(1, D), norm_ffn=norm_ffn[l].reshape(1, D),
            pool_scale=pool_scale[l].reshape(1, D), head_norm=head_norm[l].reshape(1, DH),
            a_row=_pad_lanes(a_log[l]), dt_row=_pad_lanes(dt_bias[l])))

    h = jnp.concatenate([jnp.zeros((LEAD, D), F32), meta_full, x[0]], axis=0)
    saved = []
    for l in range(depth):
        h, sv = _layer_fwd(h, layers[l], f"l{l}")
        saved.append(sv)
    target = jnp.concatenate([jnp.zeros((ROW0, D), F32), loss_target[0]], axis=0)
    dh, dh_b, d_norm_final, loss_row = _loss_bwd(h, target, norm_final.reshape(1, D), "loss")

    grads = [None] * depth
    for l in reversed(range(depth)):
        dh, dh_b, grads[l] = _layer_bwd(dh, dh_b, layers[l], saved[l], f"l{l}")
    grad_x = dh[ROW0:].reshape(1, seq, D)
    d_meta = dh[LEAD:ROW0]

    stk = lambda name: jnp.stack([grads[l][name] for l in range(depth)], axis=0)
    cin = w_in.shape[2]
    b_in = jnp.transpose(stk("w_in").reshape(depth, D, NDEV, cin), (2, 0, 1, 3)).reshape(NDEV, depth * D, cin)
    b_up = jnp.transpose(stk("w_up"), (1, 0, 2, 3)).reshape(NDEV, depth * D, FB)
    b_out = jnp.transpose(stk("w_out").reshape(depth, NDEV, D // NDEV, D), (1, 0, 2, 3)).reshape(NDEV, depth * D // NDEV, D)
    rd = w_down.shape[1]
    b_down = jnp.transpose(stk("w_down").reshape(depth, NDEV, rd, D), (1, 0, 2, 3)).reshape(NDEV, depth * rd, D)
    cq = conv_qkv.shape[2]
    pw = w_pool.shape[3]
    s_cq = jnp.transpose(stk("conv_qkv").reshape(depth, 4, NDEV, cq), (2, 0, 1, 3))
    s_cf = jnp.transpose(stk("conv_ffn"), (1, 0, 2, 3))
    s_wp = jnp.transpose(stk("w_pool").reshape(depth, 4, DH, NDEV, pw), (3, 0, 1, 2, 4))
    s_mt = jnp.transpose(d_meta.reshape(N_META, NDEV, D // NDEV), (1, 0, 2))
    b_small = jnp.stack([_pack([s_cq[i], s_cf[i], s_wp[i], s_mt[i]], SMALL_ROWS) for i in range(NDEV)], axis=0)
    l_in, l_up, l_out, l_down, l_small = _exchange([b_in, b_up, b_out, b_down, b_small], True, "exchange_grads")

    def upd(land, w, m, v, name):
        shp = w.shape
        r2 = land.shape[1:]
        outs = _adamw(land, w.reshape(r2), m.reshape(r2), v.reshape(r2), name)
        return [o_.reshape(shp) for o_ in outs]

    r_in = upd(l_in, w_in, m_w_in, v_w_in, "adamw_w_in")
    r_up = upd(l_up, w_up, m_w_up, v_w_up, "adamw_w_up")
    r_out = upd(l_out, w_out, m_w_out, v_w_out, "adamw_w_out")
    r_down = upd(l_down, w_down, m_w_down, v_w_down, "adamw_w_down")
    r_small = _adamw(l_small, small, _pack([m_conv_qkv, m_conv_ffn, m_w_pool, m_meta_tokens], SMALL_ROWS),
                     _pack([v_conv_qkv, v_conv_ffn, v_w_pool, v_meta_tokens], SMALL_ROWS), "adamw_small")
    r_small = [_unpack(o_, small_shapes) for o_ in r_small]

    repl_shapes = [norm_mix.shape, a_log.shape, dt_bias.shape, head_norm.shape, pool_scale.shape, norm_ffn.shape,
                   norm_final.shape, (1,)]
    rp = lambda name, n: jnp.stack([grads[l][name][0, :n] for l in range(depth)], axis=0)
    part = _pack([rp("norm_mix", D), rp("a_log", H), rp("dt_bias", H), rp("head_norm", DH), rp("pool_scale", D),
                  rp("norm_ffn", D), d_norm_final[0], loss_row[0, 0:1]], REPL_ROWS)
    (l_repl,) = _exchange([part], False, "gather_replicated")
    zero1 = jnp.zeros((1,), F32)
    r_repl = _adamw(l_repl, _pack([norm_mix, a_log, dt_bias, head_norm, pool_scale, norm_ffn, norm_final, zero1], REPL_ROWS),
                    _pack([m_norm_mix, m_a_log, m_dt_bias, m_head_norm, m_pool_scale, m_norm_ffn, m_norm_final, zero1], REPL_ROWS),
                    _pack([v_norm_mix, v_a_log, v_dt_bias, v_head_norm, v_pool_scale, v_norm_ffn, v_norm_final, zero1], REPL_ROWS),
                    "adamw_replicated")
    r_repl = [_unpack(o_, repl_shapes) for o_ in r_repl]
    loss = r_repl[0][7].reshape(())

    def leaf(kind):
        sm, rr = r_small[kind], r_repl[kind]
        return [sm[3], rr[0], r_in[kind], sm[0], rr[1], rr[2], rr[3], sm[2], rr[4], r_out[kind], rr[5], r_up[kind],
                sm[1], r_down[kind], rr[6]]

    return (loss, grad_x, *leaf(0), *leaf(1), *leaf(2), *leaf(3))
```

```python
import functools

import jax
import jax.numpy as jnp
from jax import lax
from jax.experimental import pallas as pl
from jax.experimental.pallas import tpu as pltpu

F32 = jnp.float32
BF16 = jnp.bfloat16
HI = lax.Precision.HIGHEST
MESH = pl.DeviceIdType.MESH

D = 1024
H = 8
DH = 128
CH = 64
N_META = 16
LEAD = 48
ROW0 = LEAD + N_META
QKV = 3 * D
POOL_W = 512
POOL_WINDOWS = (2, 4, 8, 16)
FB = 704
NDEV = 8
EPS = 1e-6
MM_TILES = 12
TE = 192
LANE = 128
SUB = 8
VMEM_CAP = 56 << 20

ADAM_LR, ADAM_B1, ADAM_B2, ADAM_EPS, ADAM_WD, ADAM_STEP = 0.001, 0.9, 0.999, 1e-08, 0.01, 10

_NN = (((1,), (0,)), ((), ()))
_NT = (((1,), (1,)), ((), ()))
_TN = (((0,), (0,)), ((), ()))


def _dot(a, b, dims=_NN, precision=None):
    return lax.dot_general(a, b, dims, precision=precision, preferred_element_type=F32)


def _bdot(a, b, dims=_NN):
    return _dot(a.astype(BF16), b.astype(BF16), dims)


def _nbytes(shape, dtype):
    n = 1
    for s in shape:
        n *= s
    return n * jnp.dtype(dtype).itemsize


def _params(sem, block_bytes):
    limit = min(VMEM_CAP, 2 * block_bytes + (20 << 20))
    return pltpu.CompilerParams(dimension_semantics=sem, vmem_limit_bytes=limit)


def _sigmoid(x):
    return 1.0 / (1.0 + jnp.exp(-x))


def _col_tile(n):
    for t in (1024, 512, 256, 128):
        if n % t == 0:
            return t
    return n


def _matmul(a, b, *, dims, grid, a_spec, b_spec, o_spec, out_shape, name, red_axis=None, res=None):
    def body(*refs):
        if res is None:
            a_ref, b_ref, o_ref = refs
        else:
            a_ref, b_ref, r_ref, o_ref = refs
        part = _dot(a_ref[...], b_ref[...], dims)
        if red_axis is None:
            if res is not None:
                part = part + r_ref[...]
            o_ref[...] = part.astype(o_ref.dtype)
        else:
            r = pl.program_id(red_axis)

            @pl.when(r == 0)
            def _():
                o_ref[...] = part + r_ref[...] if res is not None else part

            @pl.when(r > 0)
            def _():
                o_ref[...] += part

    def blk(spec, arr):
        return _nbytes([s for s in spec.block_shape if s is not None], arr.dtype)

    ins = [a, b] + ([res] if res is not None else [])
    specs = [a_spec, b_spec] + ([o_spec] if res is not None else [])
    nb = blk(a_spec, a) + blk(b_spec, b) + 2 * _nbytes([s for s in o_spec.block_shape if s is not None], F32)
    sem = tuple("arbitrary" if i == red_axis else "parallel" for i in range(len(grid)))
    return pl.pallas_call(
        body, name=name, grid=grid, in_specs=specs, out_specs=o_spec, out_shape=out_shape,
        compiler_params=_params(sem, nb),
    )(*ins)


def _mm(a, b, out_dtype, name, res=None, dims=_NN):
    m, k = a.shape
    n = b.shape[1] if dims == _NN else b.shape[0]
    tm, tn = m // MM_TILES, _col_tile(n)
    if dims == _NN:
        b_spec = pl.BlockSpec((k, tn), lambda j, i: (0, j))
    else:
        b_spec = pl.BlockSpec((tn, k), lambda j, i: (j, 0))
    return _matmul(
        a, b, dims=dims, grid=(n // tn, MM_TILES), a_spec=pl.BlockSpec((tm, k), lambda j, i: (i, 0)), b_spec=b_spec,
        o_spec=pl.BlockSpec((tm, tn), lambda j, i: (i, j)), out_shape=jax.ShapeDtypeStruct((m, n), out_dtype),
        name=name, res=res)


def _mm_tn(a, g, name):
    m, k = a.shape
    n = g.shape[1]
    tm, tn = m // MM_TILES, _col_tile(n)
    return _matmul(
        a, g, dims=_TN, grid=(n // tn, MM_TILES), red_axis=1, a_spec=pl.BlockSpec((tm, k), lambda j, i: (i, 0)),
        b_spec=pl.BlockSpec((tm, tn), lambda j, i: (i, j)), o_spec=pl.BlockSpec((k, tn), lambda j, i: (0, j)),
        out_shape=jax.ShapeDtypeStruct((k, n), F32), name=name)


def _mm_up(u, w_up, name):
    t = u.shape[0]
    g = w_up.shape[0]
    tm = t // MM_TILES
    return _matmul(
        u, w_up, dims=_NN, grid=(g, MM_TILES), a_spec=pl.BlockSpec((tm, D), lambda g_, i: (i, 0)),
        b_spec=pl.BlockSpec((None, D, FB), lambda g_, i: (g_, 0, 0)),
        o_spec=pl.BlockSpec((None, tm, FB), lambda g_, i: (g_, i, 0)),
        out_shape=jax.ShapeDtypeStruct((g, t, FB), BF16), name=name)


def _mm_blocks_red(a, b, name, dims, res=None):
    g, t, k = a.shape
    n = b.shape[2] if dims == _NN else b.shape[1]
    tm = t // MM_TILES
    return _matmul(
        a, b, dims=dims, grid=(MM_TILES, g), red_axis=1, a_spec=pl.BlockSpec((None, tm, k), lambda i, g_: (g_, i, 0)),
        b_spec=pl.BlockSpec((None,) + b.shape[1:], lambda i, g_: (g_, 0, 0)),
        o_spec=pl.BlockSpec((tm, n), lambda i, g_: (i, 0)), out_shape=jax.ShapeDtypeStruct((t, n), F32),
        name=name, res=res)


def _mm_to_blocks(a, b, name):
    t, k = a.shape
    g, n, _ = b.shape
    tm = t // MM_TILES
    return _matmul(
        a, b, dims=_NT, grid=(g, MM_TILES), a_spec=pl.BlockSpec((tm, k), lambda g_, i: (i, 0)),
        b_spec=pl.BlockSpec((None, n, k), lambda g_, i: (g_, 0, 0)),
        o_spec=pl.BlockSpec((None, tm, n), lambda g_, i: (g_, i, 0)),
        out_shape=jax.ShapeDtypeStruct((g, t, n), BF16), name=name)


def _mm_tn_blocks(a, g, name, a_blocked, g_blocked):
    nb = a.shape[0] if a_blocked else g.shape[0]
    t = a.shape[-2]
    k, n = a.shape[-1], g.shape[-1]
    tm = t // MM_TILES
    a_spec = (pl.BlockSpec((None, tm, k), lambda g_, i: (g_, i, 0)) if a_blocked
              else pl.BlockSpec((tm, k), lambda g_, i: (i, 0)))
    g_spec = (pl.BlockSpec((None, tm, n), lambda g_, i: (g_, i, 0)) if g_blocked
              else pl.BlockSpec((tm, n), lambda g_, i: (i, 0)))
    return _matmul(
        a, g, dims=_TN, grid=(nb, MM_TILES), red_axis=1, a_spec=a_spec, b_spec=g_spec,
        o_spec=pl.BlockSpec((None, k, n), lambda g_, i: (g_, 0, 0)),
        out_shape=jax.ShapeDtypeStruct((nb, k, n), F32), name=name)


def _mm_cols(a, b, out_dtype, name, dims):
    t = a.shape[0]
    g = b.shape[0]
    ka = a.shape[1] // g
    n = b.shape[2] if dims == _NN else b.shape[1]
    tm = t // MM_TILES
    return _matmul(
        a, b, dims=dims, grid=(g, MM_TILES), a_spec=pl.BlockSpec((tm, ka), lambda g_, i: (i, g_)),
        b_spec=pl.BlockSpec((None,) + b.shape[1:], lambda g_, i: (g_, 0, 0)),
        o_spec=pl.BlockSpec((tm, n), lambda g_, i: (i, g_)), out_shape=jax.ShapeDtypeStruct((t, g * n), out_dtype),
        name=name)


def _mm_tn_cols(a, g, nblk, name):
    t = a.shape[0]
    ka, n = a.shape[1] // nblk, g.shape[1] // nblk
    tm = t // MM_TILES
    return _matmul(
        a, g, dims=_TN, grid=(nblk, MM_TILES), red_axis=1, a_spec=pl.BlockSpec((tm, ka), lambda g_, i: (i, g_)),
        b_spec=pl.BlockSpec((tm, n), lambda g_, i: (i, g_)), o_spec=pl.BlockSpec((None, ka, n), lambda g_, i: (g_, 0, 0)),
        out_shape=jax.ShapeDtypeStruct((nblk, ka, n), F32), name=name)


def _rows(cols, n=None):
    if n is None:
        return pl.BlockSpec((TE, cols), lambda i: (i, 0))
    return pl.BlockSpec((TE, cols), lambda i: (n - 1 - i, 0))


def _whole(shape):
    return pl.BlockSpec(shape, lambda *_: (0,) * len(shape))


def _row_ids(i, rows=TE):
    return i * rows + lax.broadcasted_iota(jnp.int32, (rows, 1), 0)


def _rmsnorm_fwd(h, gain, name):
    t = h.shape[0]

    def body(h_ref, g_ref, u_ref):
        x = h_ref[...]
        r = lax.rsqrt(jnp.mean(x * x, axis=-1, keepdims=True) + EPS)
        u_ref[...] = (x * r * g_ref[...]).astype(BF16)

    return pl.pallas_call(
        body, name=name, grid=(t // TE,), in_specs=[_rows(D), _whole((1, D))], out_specs=_rows(D),
        out_shape=jax.ShapeDtypeStruct((t, D), BF16), compiler_params=_params(("parallel",), 3 * TE * D * 4),
    )(h, gain)


def _rmsnorm_bwd(x, du, dres, gain, name):
    t = x.shape[0]

    def body(x_ref, du_ref, dr_ref, g_ref, dx_ref, dxb_ref, dg_ref):
        i = pl.program_id(0)
        xv = x_ref[...]
        r = lax.rsqrt(jnp.mean(xv * xv, axis=-1, keepdims=True) + EPS)
        gdy = du_ref[...] * g_ref[...]
        dx = dr_ref[...] + r * gdy - xv * (r * r * r) * jnp.mean(xv * gdy, axis=-1, keepdims=True)
        dx = jnp.where(_row_ids(i) >= LEAD, dx, 0.0)
        dx_ref[...] = dx
        dxb_ref[...] = dx.astype(BF16)
        part = jnp.sum(du_ref[...] * xv * r, axis=0, keepdims=True)

        @pl.when(i == 0)
        def _():
            dg_ref[...] = part

        @pl.when(i > 0)
        def _():
            dg_ref[...] += part

    return pl.pallas_call(
        body, name=name, grid=(t // TE,), in_specs=[_rows(D), _rows(D), _rows(D), _whole((1, D))],
        out_specs=[_rows(D), _rows(D), _whole((1, D))],
        out_shape=[jax.ShapeDtypeStruct((t, D), F32), jax.ShapeDtypeStruct((t, D), BF16),
                   jax.ShapeDtypeStruct((1, D), F32)],
        compiler_params=_params(("arbitrary",), 5 * TE * D * 4),
    )(x, du, dres, gain)


def _loss_bwd(h, target, gain, name):
    t = h.shape[0]

    def body(h_ref, t_ref, g_ref, dx_ref, dxb_ref, dg_ref, loss_ref):
        i = pl.program_id(0)
        xv = h_ref[...]
        gain_v = g_ref[...]
        r = lax.rsqrt(jnp.mean(xv * xv, axis=-1, keepdims=True) + EPS)
        real = _row_ids(i) >= ROW0
        err = jnp.where(real, xv * r * gain_v - t_ref[...], 0.0)
        dy = err * (1.0 / D)
        gdy = dy * gain_v
        dx = r * gdy - xv * (r * r * r) * jnp.mean(xv * gdy, axis=-1, keepdims=True)
        dx_ref[...] = dx
        dxb_ref[...] = dx.astype(BF16)
        dgp = jnp.sum(dy * xv * r, axis=0, keepdims=True)
        lp = 0.5 * jnp.sum(jnp.mean(err * err, axis=-1, keepdims=True), axis=0, keepdims=True)

        @pl.when(i == 0)
        def _():
            dg_ref[...] = dgp
            loss_ref[...] = jnp.broadcast_to(lp, (1, LANE))

        @pl.when(i > 0)
        def _():
            dg_ref[...] += dgp
            loss_ref[...] += jnp.broadcast_to(lp, (1, LANE))

    return pl.pallas_call(
        body, name=name, grid=(t // TE,), in_specs=[_rows(D), _rows(D), _whole((1, D))],
        out_specs=[_rows(D), _rows(D), _whole((1, D)), _whole((1, LANE))],
        out_shape=[jax.ShapeDtypeStruct((t, D), F32), jax.ShapeDtypeStruct((t, D), BF16),
                   jax.ShapeDtypeStruct((1, D), F32), jax.ShapeDtypeStruct((1, LANE), F32)],
        compiler_params=_params(("arbitrary",), 4 * TE * D * 4),
    )(h, target, gain)


def _conv4(xs, w, n):
    return (w[3:4] * xs[SUB:SUB + n] + w[2:3] * xs[SUB - 1:SUB - 1 + n]
            + w[1:2] * xs[SUB - 2:SUB - 2 + n] + w[0:1] * xs[SUB - 3:SUB - 3 + n])


def _gdn_pre_fwd(x, w, name):
    t = x.shape[0]

    def body(x_ref, w_ref, o_ref, carry):
        @pl.when(pl.program_id(0) == 0)
        def _():
            carry[...] = jnp.zeros_like(carry)

        for hh in range(3 * H):
            sl = slice(hh * DH, (hh + 1) * DH)
            xv = x_ref[:, sl].astype(F32)
            xs = jnp.concatenate([carry[:, sl], xv], axis=0)
            cv = _conv4(xs, w_ref[:, sl], TE)
            s = cv * _sigmoid(cv)
            if hh < 2 * H:
                s = s * lax.rsqrt(jnp.sum(s * s, axis=-1, keepdims=True) + EPS)
                if hh < H:
                    s = s * (DH ** -0.5)
            o_ref[:, sl] = s
            carry[:, sl] = xv[TE - SUB:TE]

    return pl.pallas_call(
        body, name=name, grid=(t // TE,), in_specs=[_rows(QKV), _whole((4, QKV))], out_specs=_rows(QKV),
        out_shape=jax.ShapeDtypeStruct((t, QKV), F32), scratch_shapes=[pltpu.VMEM((SUB, QKV), F32)],
        compiler_params=_params(("arbitrary",), TE * QKV * 6),
    )(x, w)


def _gdn_pre_bwd(x, w, dqkv, name):
    t = x.shape[0]
    n = t // TE
    hb = TE // 16

    def body(x_ref, xp_ref, w_ref, d_ref, dx_ref, dw_ref, carry):
        i = pl.program_id(0)

        @pl.when(i == 0)
        def _():
            carry[...] = jnp.zeros_like(carry)
            dw_ref[...] = jnp.zeros_like(dw_ref)

        first_tile = i == n - 1
        for hh in range(3 * H):
            sl = slice(hh * DH, (hh + 1) * DH)
            xv = x_ref[:, sl].astype(F32)
            prev = jnp.where(first_tile, 0.0, xp_ref[SUB:2 * SUB, sl].astype(F32))
            xs = jnp.concatenate([prev, xv], axis=0)
            wv = w_ref[:, sl]
            cv = _conv4(xs, wv, TE)
            sg = _sigmoid(cv)
            s = cv * sg
            ds = d_ref[:, sl]
            if hh < 2 * H:
                if hh < H:
                    ds = ds * (DH ** -0.5)
                r = lax.rsqrt(jnp.sum(s * s, axis=-1, keepdims=True) + EPS)
                ds = r * ds - s * (r * r * r) * jnp.sum(s * ds, axis=-1, keepdims=True)
            dcv = ds * (sg * (1.0 + cv * (1.0 - sg)))
            ext = jnp.concatenate([dcv, carry[:, sl]], axis=0)
            dx = (wv[3:4] * ext[0:TE] + wv[2:3] * ext[1:TE + 1] + wv[1:2] * ext[2:TE + 2] + wv[0:1] * ext[3:TE + 3])
            dx_ref[:, sl] = dx.astype(BF16)
            dwp = jnp.concatenate(
                [jnp.sum(xs[SUB - 3 + j:SUB - 3 + j + TE] * dcv, axis=0, keepdims=True) for j in range(4)], axis=0)
            dw_ref[:, sl] += dwp
            carry[:, sl] = dcv[0:SUB]

    return pl.pallas_call(
        body, name=name, grid=(n,),
        in_specs=[_rows(QKV, n), pl.BlockSpec((16, QKV), lambda i: (jnp.maximum((n - 1 - i) * hb - 1, 0), 0)),
                  _whole((4, QKV)), _rows(QKV, n)],
        out_specs=[_rows(QKV, n), _whole((4, QKV))],
        out_shape=[jax.ShapeDtypeStruct((t, QKV), BF16), jax.ShapeDtypeStruct((4, QKV), F32)],
        scratch_shapes=[pltpu.VMEM((SUB, QKV), F32)], compiler_params=_params(("arbitrary",), TE * QKV * 10),
    )(x, x, w, dqkv)


def _softplus(x):
    return jnp.maximum(x, 0.0) + jnp.log(1.0 + jnp.exp(-jnp.abs(x)))


def _gates_fwd(ba, a_row, dt_row, name):
    t = ba.shape[0]

    def body(ba_ref, a_ref, dt_ref, b_out, g_out):
        real = _row_ids(pl.program_id(0)) >= LEAD
        b_out[...] = jnp.where(real, _sigmoid(ba_ref[:, 0:LANE]), 0.0)
        g = -jnp.exp(a_ref[...]) * _softplus(ba_ref[:, LANE:2 * LANE] + dt_ref[...])
        g_out[...] = jnp.where(real, g, 0.0)

    return pl.pallas_call(
        body, name=name, grid=(t // TE,), in_specs=[_rows(2 * LANE), _whole((1, LANE)), _whole((1, LANE))],
        out_specs=[_rows(LANE), _rows(LANE)],
        out_shape=[jax.ShapeDtypeStruct((t, LANE), F32), jax.ShapeDtypeStruct((t, LANE), F32)],
        compiler_params=_params(("parallel",), TE * LANE * 16),
    )(ba, a_row, dt_row)


def _gates_bwd(ba, a_row, dt_row, dbeta, dg, name):
    t = ba.shape[0]

    def body(ba_ref, a_ref, dt_ref, db_ref, dg_ref, dba_ref, da_out, ddt_out):
        i = pl.program_id(0)
        real = _row_ids(i) >= LEAD
        beta = _sigmoid(ba_ref[:, 0:LANE])
        draw_b = jnp.where(real, db_ref[...] * beta * (1.0 - beta), 0.0)
        pre = ba_ref[:, LANE:2 * LANE] + dt_ref[...]
        neg_a = -jnp.exp(a_ref[...])
        dgv = jnp.where(real, dg_ref[...], 0.0)
        draw_a = dgv * neg_a * _sigmoid(pre)
        dba_ref[:, 0:LANE] = draw_b.astype(BF16)
        dba_ref[:, LANE:2 * LANE] = draw_a.astype(BF16)
        dal = jnp.sum(dgv * neg_a * _softplus(pre), axis=0, keepdims=True)
        ddt = jnp.sum(draw_a, axis=0, keepdims=True)

        @pl.when(i == 0)
        def _():
            da_out[...] = dal
            ddt_out[...] = ddt

        @pl.when(i > 0)
        def _():
            da_out[...] += dal
            ddt_out[...] += ddt

    return pl.pallas_call(
        body, name=name, grid=(t // TE,),
        in_specs=[_rows(2 * LANE), _whole((1, LANE)), _whole((1, LANE)), _rows(LANE), _rows(LANE)],
        out_specs=[_rows(2 * LANE), _whole((1, LANE)), _whole((1, LANE))],
        out_shape=[jax.ShapeDtypeStruct((t, 2 * LANE), BF16), jax.ShapeDtypeStruct((1, LANE), F32),
                   jax.ShapeDtypeStruct((1, LANE), F32)],
        compiler_params=_params(("arbitrary",), TE * LANE * 24),
    )(ba, a_row, dt_row, dbeta, dg)


_BNN = (((2,), (1,)), ((0,), (0,)))
_BNT = (((2,), (2,)), ((0,), (0,)))
_BTN = (((1,), (1,)), ((0,), (0,)))


def _split(a):
    hi = a.astype(BF16)
    return hi, (a - hi.astype(F32)).astype(BF16)


def _dot3(a, b):
    ah, al = _split(a)
    bh, bl = _split(b)
    m = a.shape[1]
    r = _dot(jnp.concatenate([ah, al], axis=1), bh, _BNN)
    return r[:, :m] + r[:, m:] + _dot(ah, bl, _BNN)


def _heads(ref, off):
    return jnp.stack([ref[:, off + h * DH:off + (h + 1) * DH] for h in range(H)])


def _cols(a):
    return jnp.stack([a[:, h:h + 1] for h in range(H)])


def _lanes(a):
    lane = lax.broadcasted_iota(jnp.int32, (CH, LANE), 1)
    out = jnp.zeros((CH, LANE), F32)
    for h in range(H):
        out = jnp.where(lane == h, a[h], out)
    return out


def _chunk_prep(qkv_ref, b_ref, g_ref):
    row = lax.broadcasted_iota(jnp.int32, (CH, CH), 0)
    col = lax.broadcasted_iota(jnp.int32, (CH, CH), 1)
    incl, strict = row >= col, row > col
    gc = _dot(incl.astype(F32), g_ref[...], precision=HI)
    gct = gc.T
    q, k, v = _heads(qkv_ref, 0), _heads(qkv_ref, D), _heads(qkv_ref, 2 * D)
    bcol, gcol = _cols(b_ref[...]), _cols(gc)
    grow = jnp.stack([gct[h:h + 1, :] for h in range(H)])
    glast = _cols(gc[CH - 1:CH, :])
    dec = jnp.exp(jnp.where(incl[None], gcol - grow, -1e30))
    kb = k * bcol
    ab = _bdot(jnp.concatenate([kb, q], axis=1), k, _BNT)
    egc, ekc = jnp.exp(gcol), jnp.exp(glast - gcol)
    return dict(row=row, col=col, strict=strict[None], q=q, k=k, v=v, bcol=bcol, dec=dec, kb=kb,
                lm=jnp.where(strict[None], ab[:, :CH] * dec, 0.0), qk=ab[:, CH:] * dec, egc=egc, ekc=ekc,
                gth=jnp.exp(glast), qd=q * egc, kd=k * ekc, vb=v * bcol, kbg=kb * egc)


def _unit_lower_inverse(lm, eye):
    n = -lm
    x = eye + n
    pw = _dot3(n, n)
    for it in range(5):
        if it < 4:
            xp = _dot3(jnp.concatenate([x, pw], axis=1), pw)
            x = x + xp[:, :CH]
            pw = xp[:, CH:]
        else:
            x = x + _dot3(x, pw)
    return x


def _gdn_fwd(qkv, beta, g, name):
    t = qkv.shape[0]
    nc = t // CH

    def body(qkv_ref, b_ref, g_ref, o_ref, sin_ref, vn_ref, ti_ref, w_ref, state):
        @pl.when(pl.program_id(0) == 0)
        def _():
            state[...] = jnp.zeros_like(state)

        pr = _chunk_prep(qkv_ref, b_ref, g_ref)
        tinv = _unit_lower_inverse(pr["lm"], (pr["row"] == pr["col"]).astype(F32)[None])
        uw = _bdot(tinv, jnp.concatenate([pr["vb"], pr["kbg"]], axis=2), _BNN)
        u, w = uw[:, :, :DH], uw[:, :, DH:]
        s = state[...]
        ws = _bdot(jnp.concatenate([w, pr["qd"]], axis=1), s, _BNN)
        vn = u - ws[:, :CH]
        o = ws[:, CH:] + _bdot(pr["qk"], vn, _BNN)
        state[...] = s * pr["gth"] + _bdot(pr["kd"], vn, _BTN)
        sin_ref[0] = s
        ti_ref[0] = tinv
        for h in range(H):
            sl = slice(h * DH, (h + 1) * DH)
            o_ref[:, sl] = o[h]
            vn_ref[:, sl] = vn[h]
            w_ref[:, sl] = w[h]

    chunk = lambda cols: pl.BlockSpec((CH, cols), lambda c: (c, 0))
    return pl.pallas_call(
        body, name=name, grid=(nc,), in_specs=[chunk(QKV), chunk(LANE), chunk(LANE)],
        out_specs=[chunk(D), pl.BlockSpec((1, H, DH, DH), lambda c: (c, 0, 0, 0)), chunk(D),
                   pl.BlockSpec((1, H, CH, CH), lambda c: (c, 0, 0, 0)), chunk(D)],
        out_shape=[jax.ShapeDtypeStruct((t, D), F32), jax.ShapeDtypeStruct((nc, H, DH, DH), F32),
                   jax.ShapeDtypeStruct((t, D), F32), jax.ShapeDtypeStruct((nc, H, CH, CH), F32),
                   jax.ShapeDtypeStruct((t, D), F32)],
        scratch_shapes=[pltpu.VMEM((H, DH, DH), F32)], compiler_params=_params(("arbitrary",), 4 << 20),
    )(qkv, beta, g)


def _gdn_bwd(qkv, beta, g, do, s_in, vnew, tinv, wsv, name):
    t = qkv.shape[0]
    nc = t // CH

    def body(qkv_ref, b_ref, g_ref, do_ref, sin_ref, vn_ref, ti_ref, w_ref, dqkv_ref, db_ref, dg_ref, dstate):
        @pl.when(pl.program_id(0) == 0)
        def _():
            dstate[...] = jnp.zeros_like(dstate)

        pr = _chunk_prep(qkv_ref, b_ref, g_ref)
        ti, s = ti_ref[0], sin_ref[0]
        w, vn, doh = _heads(w_ref, 0), _heads(vn_ref, 0), _heads(do_ref, 0)
        ds = dstate[...]
        dvn = _bdot(pr["kd"], ds, _BNN) + _bdot(pr["qk"], doh, _BTN)
        dkd = _bdot(vn, ds, _BNT)
        dqd = _bdot(doh, s, _BNT)
        dqk = _bdot(doh, vn, _BNT)
        dw = -_bdot(dvn, s, _BNT)
        dgt = jnp.sum(jnp.sum(ds * s, axis=2, keepdims=True), axis=1, keepdims=True)
        dstate[...] = ds * pr["gth"] + _bdot(pr["qd"], doh, _BTN) - _bdot(w, dvn, _BTN)
        duw = jnp.concatenate([dvn, dw], axis=2)
        dvk = _bdot(ti, duw, _BTN)
        dvb, dkbg = dvk[:, :, :DH], dvk[:, :, DH:]
        dti = _bdot(duw, jnp.concatenate([pr["vb"], pr["kbg"]], axis=2), _BNT)
        dl = -_dot(_dot(ti, dti, _BTN, precision=HI), ti, _BNT, precision=HI)
        dl = jnp.where(pr["strict"], dl, 0.0)
        dab = jnp.concatenate([dl * pr["dec"], dqk * pr["dec"]], axis=1)
        r1 = _bdot(dab, pr["k"], _BNN)
        dkb = r1[:, :CH] + dkbg * pr["egc"]
        dq = r1[:, CH:] + dqd * pr["egc"]
        dk = _bdot(dab, jnp.concatenate([pr["kb"], pr["q"]], axis=1), _BTN) + dkb * pr["bcol"] + dkd * pr["ekc"]
        m = dl * pr["lm"] + dqk * pr["qk"]
        colsum = _dot(m, jnp.ones((H, CH, LANE), F32), _BTN, precision=HI)[:, :, 0:1]
        kdsum = jnp.sum(dkd * pr["kd"], axis=2, keepdims=True)
        dgc = (jnp.sum(m, axis=2, keepdims=True) - colsum + jnp.sum(dkbg * pr["kbg"], axis=2, keepdims=True)
               + jnp.sum(dqd * pr["qd"], axis=2, keepdims=True) - kdsum)
        dglast = jnp.sum(kdsum, axis=1, keepdims=True) + dgt * pr["gth"]
        last_row = lax.broadcasted_iota(jnp.int32, (1, CH, 1), 1) == CH - 1
        dgc = dgc + jnp.where(last_row, dglast, 0.0)
        dbeta = jnp.sum(dkb * pr["k"], axis=2, keepdims=True) + jnp.sum(dvb * pr["v"], axis=2, keepdims=True)
        dv = dvb * pr["bcol"]
        for h in range(H):
            dqkv_ref[:, h * DH:(h + 1) * DH] = dq[h]
            dqkv_ref[:, D + h * DH:D + (h + 1) * DH] = dk[h]
            dqkv_ref[:, 2 * D + h * DH:2 * D + (h + 1) * DH] = dv[h]
        db_ref[...] = _lanes(dbeta)
        dg_ref[...] = _dot((pr["row"] <= pr["col"]).astype(F32), _lanes(dgc), precision=HI)

    chunk = lambda cols: pl.BlockSpec((CH, cols), lambda c: (nc - 1 - c, 0))
    sq = lambda a, b: pl.BlockSpec((1, H, a, b), lambda c: (nc - 1 - c, 0, 0, 0))
    return pl.pallas_call(
        body, name=name, grid=(nc,),
        in_specs=[chunk(QKV), chunk(LANE), chunk(LANE), chunk(D), sq(DH, DH), chunk(D), sq(CH, CH), chunk(D)],
        out_specs=[chunk(QKV), chunk(LANE), chunk(LANE)],
        out_shape=[jax.ShapeDtypeStruct((t, QKV), F32), jax.ShapeDtypeStruct((t, LANE), F32),
                   jax.ShapeDtypeStruct((t, LANE), F32)],
        scratch_shapes=[pltpu.VMEM((H, DH, DH), F32)], compiler_params=_params(("arbitrary",), 6 << 20),
    )(qkv, beta, g, do, s_in, vnew, tinv, wsv)


def _pool_counts(row_ids, win):
    return jnp.minimum(jnp.maximum(row_ids - LEAD, 0) + 1, win).astype(F32)


def _pool_fwd(p, name):
    t = p.shape[0]
    ext = TE + 16

    def body(p_ref, o_ref, carry):
        i = pl.program_id(0)

        @pl.when(i == 0)
        def _():
            carry[...] = jnp.zeros_like(carry)

        ids = _row_ids(i)
        for gi, win in enumerate(POOL_WINDOWS):
            sl = slice(gi * LANE, (gi + 1) * LANE)
            xv = p_ref[:, sl]
            s = jnp.concatenate([carry[:, sl], xv], axis=0)
            sh = 1
            while sh < win:
                s = s + pltpu.roll(s, sh, 0)
                sh *= 2
            o_ref[:, sl] = (s[16:ext] / _pool_counts(ids, win) - xv).astype(BF16)
            carry[:, sl] = xv[TE - 16:TE]

    return pl.pallas_call(
        body, name=name, grid=(t // TE,), in_specs=[_rows(POOL_W)], out_specs=_rows(POOL_W),
        out_shape=jax.ShapeDtypeStruct((t, POOL_W), BF16), scratch_shapes=[pltpu.VMEM((16, POOL_W), F32)],
        compiler_params=_params(("arbitrary",), TE * POOL_W * 8),
    )(p)


def _pool_bwd(dpo, name):
    t = dpo.shape[0]
    n = t // TE
    ext = TE + 16

    def body(d_ref, o_ref, carry):
        i = pl.program_id(0)

        @pl.when(i == 0)
        def _():
            carry[...] = jnp.zeros_like(carry)

        ids = _row_ids(n - 1 - i)
        for gi, win in enumerate(POOL_WINDOWS):
            sl = slice(gi * LANE, (gi + 1) * LANE)
            dv = d_ref[:, sl]
            rv = dv / _pool_counts(ids, win)
            s = jnp.concatenate([rv, carry[:, sl]], axis=0)
            sh = 1
            while sh < win:
                s = s + pltpu.roll(s, ext - sh, 0)
                sh *= 2
            o_ref[:, sl] = (s[0:TE] - dv).astype(BF16)
            carry[:, sl] = rv[0:16]

    return pl.pallas_call(
        body, name=name, grid=(n,), in_specs=[_rows(POOL_W, n)], out_specs=_rows(POOL_W, n),
        out_shape=jax.ShapeDtypeStruct((t, POOL_W), BF16), scratch_shapes=[pltpu.VMEM((16, POOL_W), F32)],
        compiler_params=_params(("arbitrary",), TE * POOL_W * 8),
    )(dpo)


def _post_fwd(o, z, gate, pm, hn, ps, name):
    t = o.shape[0]

    def body(o_ref, z_ref, g_ref, pm_ref, hn_ref, ps_ref, y_ref):
        for h in range(H):
            sl = slice(h * DH, (h + 1) * DH)
            ov = o_ref[:, sl]
            zv = z_ref[:, sl].astype(F32)
            r = lax.rsqrt(jnp.mean(ov * ov, axis=-1, keepdims=True) + EPS)
            ya = ov * r * hn_ref[...] * (zv * _sigmoid(zv))
            ga = _sigmoid(g_ref[:, sl].astype(F32))
            gb = _sigmoid(g_ref[:, D + h * DH:D + (h + 1) * DH].astype(F32))
            y_ref[:, sl] = (ga * ya + gb * (pm_ref[:, sl] * ps_ref[:, sl])).astype(BF16)

    return pl.pallas_call(
        body, name=name, grid=(t // TE,),
        in_specs=[_rows(D), _rows(D), _rows(2 * D), _rows(D), _whole((1, DH)), _whole((1, D))], out_specs=_rows(D),
        out_shape=jax.ShapeDtypeStruct((t, D), BF16), compiler_params=_params(("parallel",), TE * D * 16),
    )(o, z, gate, pm, hn, ps)


def _post_bwd(dy, o, z, gate, pm, hn, ps, name):
    t = o.shape[0]

    def body(dy_ref, o_ref, z_ref, g_ref, pm_ref, hn_ref, ps_ref, do_ref, dz_ref, dgate_ref, dpm_ref, dhn_ref, dps_ref):
        i = pl.program_id(0)

        @pl.when(i == 0)
        def _():
            dhn_ref[...] = jnp.zeros_like(dhn_ref)
            dps_ref[...] = jnp.zeros_like(dps_ref)

        hnv = hn_ref[...]
        dhn = jnp.zeros((1, DH), F32)
        for h in range(H):
            sl = slice(h * DH, (h + 1) * DH)
            slb = slice(D + h * DH, D + (h + 1) * DH)
            dyv = dy_ref[:, sl]
            ov = o_ref[:, sl]
            zv = z_ref[:, sl].astype(F32)
            r = lax.rsqrt(jnp.mean(ov * ov, axis=-1, keepdims=True) + EPS)
            sz = _sigmoid(zv)
            silu = zv * sz
            on = ov * r
            ya = on * hnv * silu
            ga = _sigmoid(g_ref[:, sl].astype(F32))
            gb = _sigmoid(g_ref[:, slb].astype(F32))
            pmv = pm_ref[:, sl]
            psv = ps_ref[:, sl]
            dya = dyv * ga
            dyb = dyv * gb
            dgate_ref[:, sl] = (dyv * ya * ga * (1.0 - ga)).astype(BF16)
            dgate_ref[:, slb] = (dyv * (pmv * psv) * gb * (1.0 - gb)).astype(BF16)
            tt = dya * hnv * silu
            do_ref[:, sl] = r * tt - ov * (r * r * r) * jnp.mean(ov * tt, axis=-1, keepdims=True)
            dz_ref[:, sl] = (dya * on * hnv * (sz * (1.0 + zv * (1.0 - sz)))).astype(BF16)
            dhn = dhn + jnp.sum(dya * on * silu, axis=0, keepdims=True)
            dps_ref[:, sl] += jnp.sum(dyb * pmv, axis=0, keepdims=True)
            dpm_ref[:, sl] = (dyb * psv).astype(BF16)
        dhn_ref[...] += dhn

    return pl.pallas_call(
        body, name=name, grid=(t // TE,),
        in_specs=[_rows(D), _rows(D), _rows(D), _rows(2 * D), _rows(D), _whole((1, DH)), _whole((1, D))],
        out_specs=[_rows(D), _rows(D), _rows(2 * D), _rows(D), _whole((1, DH)), _whole((1, D))],
        out_shape=[jax.ShapeDtypeStruct((t, D), F32), jax.ShapeDtypeStruct((t, D), BF16),
                   jax.ShapeDtypeStruct((t, 2 * D), BF16), jax.ShapeDtypeStruct((t, D), BF16),
                   jax.ShapeDtypeStruct((1, DH), F32), jax.ShapeDtypeStruct((1, D), F32)],
        compiler_params=_params(("arbitrary",), TE * D * 28),
    )(dy, o, z, gate, pm, hn, ps)


_FB_COLS = [(c, min(c + LANE, FB)) for c in range(0, FB, LANE)]


def _conv3(xs, w, n):
    return w[2:3] * xs[SUB:SUB + n] + w[1:2] * xs[SUB - 1:SUB - 1 + n] + w[0:1] * xs[SUB - 2:SUB - 2 + n]


def _mlp_act_fwd(hid, cw, name):
    t = hid.shape[1]
    n = t // TE

    def body(hg_ref, hv_ref, wg_ref, wv_ref, a_ref, cg, cv):
        @pl.when(pl.program_id(1) == 0)
        def _():
            cg[...] = jnp.zeros_like(cg)
            cv[...] = jnp.zeros_like(cv)

        for c0, c1 in _FB_COLS:
            sl = slice(c0, c1)
            xg = hg_ref[:, sl].astype(F32)
            xv = hv_ref[:, sl].astype(F32)
            gg = _conv3(jnp.concatenate([cg[:, sl], xg], axis=0), wg_ref[:, sl], TE)
            vv = _conv3(jnp.concatenate([cv[:, sl], xv], axis=0), wv_ref[:, sl], TE)
            a_ref[:, sl] = (gg * _sigmoid(gg) * vv).astype(BF16)
            cg[:, sl] = xg[TE - SUB:TE]
            cv[:, sl] = xv[TE - SUB:TE]

    hspec = lambda off: pl.BlockSpec((None, TE, FB), lambda p, i: (p + off, i, 0))
    wspec = lambda off: pl.BlockSpec((None, 3, FB), lambda p, i: (p + off, 0, 0))
    return pl.pallas_call(
        body, name=name, grid=(4, n), in_specs=[hspec(0), hspec(4), wspec(0), wspec(4)],
        out_specs=pl.BlockSpec((None, TE, FB), lambda p, i: (p, i, 0)),
        out_shape=jax.ShapeDtypeStruct((4, t, FB), BF16),
        scratch_shapes=[pltpu.VMEM((SUB, FB), F32), pltpu.VMEM((SUB, FB), F32)],
        compiler_params=_params(("parallel", "arbitrary"), TE * FB * 8),
    )(hid, hid, cw, cw)


def _mlp_act_bwd(da, hid, cw, name):
    t = hid.shape[1]
    n = t // TE
    hb = TE // 16

    def body(da_ref, hg_ref, hv_ref, pg_ref, pv_ref, wg_ref, wv_ref, dhg_ref, dhv_ref, dwg_ref, dwv_ref, cg, cv):
        i = pl.program_id(1)

        @pl.when(i == 0)
        def _():
            cg[...] = jnp.zeros_like(cg)
            cv[...] = jnp.zeros_like(cv)
            dwg_ref[...] = jnp.zeros_like(dwg_ref)
            dwv_ref[...] = jnp.zeros_like(dwv_ref)

        first_tile = i == n - 1
        for c0, c1 in _FB_COLS:
            sl = slice(c0, c1)
            xg = jnp.concatenate([jnp.where(first_tile, 0.0, pg_ref[SUB:2 * SUB, sl].astype(F32)),
                                  hg_ref[:, sl].astype(F32)], axis=0)
            xv = jnp.concatenate([jnp.where(first_tile, 0.0, pv_ref[SUB:2 * SUB, sl].astype(F32)),
                                  hv_ref[:, sl].astype(F32)], axis=0)
            wg = wg_ref[:, sl]
            wv = wv_ref[:, sl]
            gg = _conv3(xg, wg, TE)
            vv = _conv3(xv, wv, TE)
            sg = _sigmoid(gg)
            dav = da_ref[:, sl].astype(F32)
            dgg = dav * vv * (sg * (1.0 + gg * (1.0 - sg)))
            dvv = dav * (gg * sg)
            for dc, xs, w, carry, dh_ref, dw_ref in ((dgg, xg, wg, cg, dhg_ref, dwg_ref), (dvv, xv, wv, cv, dhv_ref, dwv_ref)):
                ext = jnp.concatenate([dc, carry[:, sl]], axis=0)
                dh_ref[:, sl] = (w[2:3] * ext[0:TE] + w[1:2] * ext[1:TE + 1] + w[0:1] * ext[2:TE + 2]).astype(BF16)
                dw_ref[:, sl] += jnp.concatenate(
                    [jnp.sum(xs[SUB - 2 + j:SUB - 2 + j + TE] * dc, axis=0, keepdims=True) for j in range(3)], axis=0)
                carry[:, sl] = dc[0:SUB]

    rev = lambda off: pl.BlockSpec((None, TE, FB), lambda p, i: (p + off, n - 1 - i, 0))
    halo = lambda off: pl.BlockSpec((None, 16, FB), lambda p, i: (p + off, jnp.maximum((n - 1 - i) * hb - 1, 0), 0))
    wspec = lambda off: pl.BlockSpec((None, 3, FB), lambda p, i: (p + off, 0, 0))
    dwspec = pl.BlockSpec((None, 3, FB), lambda p, i: (p, 0, 0))
    return pl.pallas_call(
        body, name=name, grid=(4, n), in_specs=[rev(0), rev(0), rev(4), halo(0), halo(4), wspec(0), wspec(4)],
        out_specs=[rev(0), rev(0), dwspec, dwspec],
        out_shape=[jax.ShapeDtypeStruct((4, t, FB), BF16), jax.ShapeDtypeStruct((4, t, FB), BF16),
                   jax.ShapeDtypeStruct((4, 3, FB), F32), jax.ShapeDtypeStruct((4, 3, FB), F32)],
        scratch_shapes=[pltpu.VMEM((SUB, FB), F32), pltpu.VMEM((SUB, FB), F32)],
        compiler_params=_params(("parallel", "arbitrary"), TE * FB * 16),
    )(da, hid, hid, hid, hid, cw, cw)


_OFFSETS = [(dx, dy, dc) for dx in (0, 1) for dy in (0, 1) for dc in (0, 1)][1:]
NPEER = len(_OFFSETS)


def _exchange(arrs, scatter, name):
    n = len(arrs)

    def body(*refs):
        ins, outs = refs[:n], refs[n:2 * n]
        send, recv, loc = refs[2 * n:]
        x, y, c = lax.axis_index("x"), lax.axis_index("y"), lax.axis_index("c")
        me = 4 * x + 2 * y + c

        def src(k, idx):
            return ins[k].at[idx] if scatter else ins[k]

        started = []
        for k in range(n):
            own = pltpu.make_async_copy(src(k, me), outs[k].at[me], loc.at[k])
            own.start()
            started.append(own)
        remote = []
        for j, (dx, dy, dc) in enumerate(_OFFSETS):
            px, py, pc = (1 - x if dx else x), (1 - y if dy else y), (1 - c if dc else c)
            pidx = 4 * px + 2 * py + pc
            for k in range(n):
                cp = pltpu.make_async_remote_copy(
                    src_ref=src(k, pidx), dst_ref=outs[k].at[me], send_sem=send.at[k * NPEER + j],
                    recv_sem=recv.at[k * NPEER + j], device_id=(px, py, pc), device_id_type=MESH)
                cp.start()
                remote.append((cp, k, j, pidx, (px, py, pc)))
        for cp, k, j, pidx, peer in remote:
            pltpu.make_async_remote_copy(
                src_ref=src(k, pidx), dst_ref=outs[k].at[pidx], send_sem=send.at[k * NPEER + j],
                recv_sem=recv.at[k * NPEER + j], device_id=peer, device_id_type=MESH).wait_recv()
        for cp, *_ in remote:
            cp.wait_send()
        for own in started:
            own.wait()

    out_shape = [jax.ShapeDtypeStruct(a.shape if scatter else (NDEV,) + a.shape, a.dtype) for a in arrs]
    any_spec = pl.BlockSpec(memory_space=pl.ANY)
    return pl.pallas_call(
        body, name=name, in_specs=[any_spec] * n, out_specs=[any_spec] * n, out_shape=out_shape,
        scratch_shapes=[pltpu.SemaphoreType.DMA((n * NPEER,)), pltpu.SemaphoreType.DMA((n * NPEER,)),
                        pltpu.SemaphoreType.DMA((n,))],
    )(*arrs)


def _adamw(land, w, m, v, name):
    r, c = w.shape
    tr = r
    for cand in (128, 64, 56, 8):
        if r % cand == 0:
            tr = cand
            break
    c1 = 1.0 - ADAM_B1 ** ADAM_STEP
    c2 = 1.0 - ADAM_B2 ** ADAM_STEP

    def body(l_ref, w_ref, m_ref, v_ref, g_out, d_out, m_out, v_out):
        g = l_ref[0].astype(F32)
        for i in range(1, NDEV):
            g = g + l_ref[i].astype(F32)
        mn = ADAM_B1 * m_ref[...] + (1.0 - ADAM_B1) * g
        vn = ADAM_B2 * v_ref[...] + (1.0 - ADAM_B2) * (g * g)
        g_out[...] = g
        m_out[...] = mn
        v_out[...] = vn
        d_out[...] = -ADAM_LR * ((mn / c1) / (jnp.sqrt(vn / c2) + ADAM_EPS) + ADAM_WD * w_ref[...])

    spec = pl.BlockSpec((tr, c), lambda i: (i, 0))
    shp = jax.ShapeDtypeStruct((r, c), F32)
    return pl.pallas_call(
        body, name=name, grid=(r // tr,), in_specs=[pl.BlockSpec((NDEV, tr, c), lambda i: (0, i, 0)), spec, spec, spec],
        out_specs=[spec] * 4, out_shape=[shp] * 4, compiler_params=_params(("parallel",), 15 * tr * c * 4),
    )(land, w, m, v)


def _layer_fwd(h, p, tag):
    u = _rmsnorm_fwd(h, p["norm_mix"], f"norm_mix_{tag}")
    qkv_pre = _mm(u, p["w_qkv"], BF16, f"proj_qkv_{tag}")
    z = _mm(u, p["w_z"], BF16, f"proj_z_{tag}")
    ba = _mm(u, p["w_ba"], F32, f"proj_ba_{tag}")
    pool_in = _mm(u, p["w_pl"], F32, f"proj_pool_{tag}")
    gate = _mm(u, p["w_gate"], BF16, f"proj_gate_{tag}")
    qkv = _gdn_pre_fwd(qkv_pre, p["conv_qkv"], f"gdn_pre_{tag}")
    beta, g = _gates_fwd(ba, p["a_row"], p["dt_row"], f"gates_{tag}")
    o, s_in, vnew, tinv, wsv = _gdn_fwd(qkv, beta, g, f"gdn_{tag}")
    pooled = _pool_fwd(pool_in, f"pool_{tag}")
    pm = _mm_cols(pooled, p["w_pool"], F32, f"pool_mm_{tag}", _NN)
    y = _post_fwd(o, z, gate, pm, p["head_norm"], p["pool_scale"], f"post_{tag}")
    h1 = _mm(y, p["w_out"], F32, f"out_proj_{tag}", res=h)
    u2 = _rmsnorm_fwd(h1, p["norm_ffn"], f"norm_ffn_{tag}")
    hid = _mm_up(u2, p["w_up"], f"up_proj_{tag}")
    act = _mlp_act_fwd(hid, p["conv_ffn"], f"mlp_act_{tag}")
    h2 = _mm_blocks_red(act, p["w_down"], f"down_proj_{tag}", _NN, res=h1)
    saved = dict(h=h, u=u, qkv_pre=qkv_pre, z=z, ba=ba, gate=gate, qkv=qkv, beta=beta, g=g, o=o, s_in=s_in, vnew=vnew,
                 tinv=tinv, wsv=wsv, pooled=pooled, pm=pm, y=y, h1=h1, u2=u2, hid=hid, act=act)
    return h2, saved


def _layer_bwd(dh, dh_b, p, s, tag):
    gr = {}
    da = _mm_to_blocks(dh_b, p["w_down"], f"d_act_{tag}")
    gr["w_down"] = _mm_tn_blocks(s["act"], dh_b, f"dw_down_{tag}", True, False)
    dhg, dhv, dwg, dwv = _mlp_act_bwd(da, s["hid"], p["conv_ffn"], f"mlp_act_bwd_{tag}")
    gr["conv_ffn"] = jnp.concatenate([dwg, dwv], axis=0)
    w_up = p["w_up"]
    du2 = _mm_blocks_red(dhg, w_up[:4], f"d_u2g_{tag}", _NT)
    du2 = _mm_blocks_red(dhv, w_up[4:], f"d_u2v_{tag}", _NT, res=du2)
    gr["w_up"] = jnp.concatenate([_mm_tn_blocks(s["u2"], dhg, f"dw_upg_{tag}", False, True),
                                  _mm_tn_blocks(s["u2"], dhv, f"dw_upv_{tag}", False, True)], axis=0)
    dh1, dh1_b, gr["norm_ffn"] = _rmsnorm_bwd(s["h1"], du2, dh, p["norm_ffn"], f"norm_ffn_bwd_{tag}")
    dy = _mm(dh1_b, p["w_out"], F32, f"d_y_{tag}", dims=_NT)
    gr["w_out"] = _mm_tn(s["y"], dh1_b, f"dw_out_{tag}")
    do, dz, dgate, dpm, gr["head_norm"], gr["pool_scale"] = _post_bwd(
        dy, s["o"], s["z"], s["gate"], s["pm"], p["head_norm"], p["pool_scale"], f"post_bwd_{tag}")
    dpooled = _mm_cols(dpm, p["w_pool"], F32, f"d_pooled_{tag}", _NT)
    gr["w_pool"] = _mm_tn_cols(s["pooled"], dpm, 4, f"dw_pool_{tag}")
    dpool_in = _pool_bwd(dpooled, f"pool_bwd_{tag}")
    dqkv, dbeta, dg = _gdn_bwd(s["qkv"], s["beta"], s["g"], do, s["s_in"], s["vnew"], s["tinv"], s["wsv"], f"gdn_bwd_{tag}")
    dba, gr["a_log"], gr["dt_bias"] = _gates_bwd(s["ba"], p["a_row"], p["dt_row"], dbeta, dg, f"gates_bwd_{tag}")
    dqkv_pre, gr["conv_qkv"] = _gdn_pre_bwd(s["qkv_pre"], p["conv_qkv"], dqkv, f"gdn_pre_bwd_{tag}")
    du = None
    dws = []
    for nm, dseg, wseg in (("qkv", dqkv_pre, p["w_qkv"]), ("z", dz, p["w_z"]), ("ba", dba, p["w_ba"]),
                           ("pool", dpool_in, p["w_pl"]), ("gate", dgate, p["w_gate"])):
        du = _mm(dseg, wseg, F32, f"d_u_{nm}_{tag}", res=du, dims=_NT)
        dws.append(_mm_tn(s["u"], dseg, f"dw_{nm}_{tag}"))
    gr["w_in"] = jnp.concatenate([dws[0], dws[1], dws[2][:, 0:H], dws[2][:, LANE:LANE + H], dws[3], dws[4]], axis=1)
    dh0, dh0_b, gr["norm_mix"] = _rmsnorm_bwd(s["h"], du, dh1, p["norm_mix"], f"norm_mix_bwd_{tag}")
    return dh0, dh0_b, gr


def _pad_lanes(v8):
    return jnp.pad(v8.reshape(1, H), ((0, 0), (0, LANE - H)))


def _pack(parts, rows):
    flat = jnp.concatenate([q.reshape(-1) for q in parts])
    return jnp.pad(flat, (0, rows * LANE - flat.shape[0])).reshape(rows, LANE)


def _unpack(packed, shapes):
    flat = packed.reshape(-1)
    out, off = [], 0
    for shp in shapes:
        n = 1
        for s_ in shp:
            n *= s_
        out.append(flat[off:off + n].reshape(shp))
        off += n
    return out


SMALL_ROWS = 336
REPL_ROWS = 64


def kernel(x, meta_tokens, norm_mix, w_in, conv_qkv, a_log, dt_bias, head_norm, w_pool, pool_scale, w_out, norm_ffn, w_up, conv_ffn, w_down, norm_final, loss_target, m_meta_tokens, m_norm_mix, m_w_in, m_conv_qkv, m_a_log, m_dt_bias, m_head_norm, m_w_pool, m_pool_scale, m_w_out, m_norm_ffn, m_w_up, m_conv_ffn, m_w_down, m_norm_final, v_meta_tokens, v_norm_mix, v_w_in, v_conv_qkv, v_a_log, v_dt_bias, v_head_norm, v_w_pool, v_pool_scale, v_w_out, v_norm_ffn, v_w_up, v_conv_ffn, v_w_down, v_norm_final):
    seq = x.shape[1]
    t = ROW0 + seq
    assert t % (TE * 1) == 0 and t % (MM_TILES * 16) == 0 and t % CH == 0
    depth = w_in.shape[0]

    small_shapes = [conv_qkv.shape, conv_ffn.shape, w_pool.shape, meta_tokens.shape]
    small = _pack([conv_qkv, conv_ffn, w_pool, meta_tokens], SMALL_ROWS)
    g_in, g_up, g_out, g_down, g_small = _exchange(
        [w_in.astype(BF16), w_up.astype(BF16), w_out.astype(BF16), w_down.astype(BF16), small], False, "gather_weights")
    w_in_full = jnp.transpose(g_in, (1, 2, 0, 3)).reshape(depth, D, NDEV * w_in.shape[2])
    smalls = [_unpack(g_small[i], small_shapes) for i in range(NDEV)]
    conv_qkv_full = jnp.concatenate([sm[0] for sm in smalls], axis=2)
    conv_ffn_blk = jnp.stack([sm[1] for sm in smalls], axis=1)
    w_pool_full = jnp.concatenate([sm[2] for sm in smalls], axis=3).astype(BF16)
    meta_full = jnp.concatenate([sm[3] for sm in smalls], axis=1)
    zpad = jnp.zeros((D, LANE - H), BF16)
    layers = []
    for l in range(depth):
        wf = w_in_full[l]
        layers.append(dict(
            w_qkv=wf[:, 0:QKV], w_z=wf[:, QKV:QKV + D],
            w_ba=jnp.concatenate([wf[:, 4096:4104], zpad, wf[:, 4104:4112], zpad], axis=1),
            w_pl=wf[:, 4112:4624], w_gate=wf[:, 4624:6672],
            conv_qkv=conv_qkv_full[l], conv_ffn=conv_ffn_blk[l], w_pool=w_pool_full[l],
            w_out=g_out[:, l].reshape(D, D), w_up=g_up[:, l], w_down=g_down[:, l].reshape(4, FB, D),
            norm_mix=norm_mix[l].reshape(1, D), norm_ffn=norm_ffn[l].reshape(1, D),
            pool_scale=pool_scale[l].reshape(1, D), head_norm=head_norm[l].reshape(1, DH),
            a_row=_pad_lanes(a_log[l]), dt_row=_pad_lanes(dt_bias[l])))

    h = jnp.concatenate([jnp.zeros((LEAD, D), F32), meta_full, x[0]], axis=0)
    saved = []
    for l in range(depth):
        h, sv = _layer_fwd(h, layers[l], f"l{l}")
        saved.append(sv)
    target = jnp.concatenate([jnp.zeros((ROW0, D), F32), loss_target[0]], axis=0)
    dh, dh_b, d_norm_final, loss_row = _loss_bwd(h, target, norm_final.reshape(1, D), "loss")

    grads = [None] * depth
    for l in reversed(range(depth)):
        dh, dh_b, grads[l] = _layer_bwd(dh, dh_b, layers[l], saved[l], f"l{l}")
    grad_x = dh[ROW0:].reshape(1, seq, D)
    d_meta = dh[LEAD:ROW0]

    stk = lambda name: jnp.stack([grads[l][name] for l in range(depth)], axis=0)
    cin = w_in.shape[2]
    b_in = jnp.transpose(stk("w_in").reshape(depth, D, NDEV, cin), (2, 0, 1, 3)).reshape(NDEV, depth * D, cin)
    b_up = jnp.transpose(stk("w_up"), (1, 0, 2, 3)).reshape(NDEV, depth * D, FB)
    b_out = jnp.transpose(stk("w_out").reshape(depth, NDEV, D // NDEV, D), (1, 0, 2, 3)).reshape(NDEV, depth * D // NDEV, D)
    rd = w_down.shape[1]
    b_down = jnp.transpose(stk("w_down").reshape(depth, NDEV, rd, D), (1, 0, 2, 3)).reshape(NDEV, depth * rd, D)
    cq = conv_qkv.shape[2]
    pw = w_pool.shape[3]
    s_cq = jnp.transpose(stk("conv_qkv").reshape(depth, 4, NDEV, cq), (2, 0, 1, 3))
    s_cf = jnp.transpose(stk("conv_ffn"), (1, 0, 2, 3))
    s_wp = jnp.transpose(stk("w_pool").reshape(depth, 4, DH, NDEV, pw), (3, 0, 1, 2, 4))
    s_mt = jnp.transpose(d_meta.reshape(N_META, NDEV, D // NDEV), (1, 0, 2))
    b_small = jnp.stack([_pack([s_cq[i], s_cf[i], s_wp[i], s_mt[i]], SMALL_ROWS) for i in range(NDEV)], axis=0)
    l_in, l_up, l_out, l_down, l_small = _exchange(
        [b_in.astype(BF16), b_up.astype(BF16), b_out.astype(BF16), b_down.astype(BF16), b_small], True, "exchange_grads")

    def upd(land, w, m, v, name):
        shp = w.shape
        r2 = land.shape[1:]
        outs = _adamw(land, w.reshape(r2), m.reshape(r2), v.reshape(r2), name)
        return [o_.reshape(shp) for o_ in outs]

    r_in = upd(l_in, w_in, m_w_in, v_w_in, "adamw_w_in")
    r_up = upd(l_up, w_up, m_w_up, v_w_up, "adamw_w_up")
    r_out = upd(l_out, w_out, m_w_out, v_w_out, "adamw_w_out")
    r_down = upd(l_down, w_down, m_w_down, v_w_down, "adamw_w_down")
    r_small = _adamw(l_small, small, _pack([m_conv_qkv, m_conv_ffn, m_w_pool, m_meta_tokens], SMALL_ROWS),
                     _pack([v_conv_qkv, v_conv_ffn, v_w_pool, v_meta_tokens], SMALL_ROWS), "adamw_small")
    r_small = [_unpack(o_, small_shapes) for o_ in r_small]

    repl_shapes = [norm_mix.shape, a_log.shape, dt_bias.shape, head_norm.shape, pool_scale.shape, norm_ffn.shape,
                   norm_final.shape, (1,)]
    rp = lambda name, n: jnp.stack([grads[l][name][0, :n] for l in range(depth)], axis=0)
    part = _pack([rp("norm_mix", D), rp("a_log", H), rp("dt_bias", H), rp("head_norm", DH), rp("pool_scale", D),
                  rp("norm_ffn", D), d_norm_final[0], loss_row[0, 0:1]], REPL_ROWS)
    (l_repl,) = _exchange([part], False, "gather_replicated")
    zero1 = jnp.zeros((1,), F32)
    r_repl = _adamw(l_repl, _pack([norm_mix, a_log, dt_bias, head_norm, pool_scale, norm_ffn, norm_final, zero1], REPL_ROWS),
                    _pack([m_norm_mix, m_a_log, m_dt_bias, m_head_norm, m_pool_scale, m_norm_ffn, m_norm_final, zero1], REPL_ROWS),
                    _pack([v_norm_mix, v_a_log, v_dt_bias, v_head_norm, v_pool_scale, v_norm_ffn, v_norm_final, zero1], REPL_ROWS),
                    "adamw_replicated")
    r_repl = [_unpack(o_, repl_shapes) for o_ in r_repl]
    loss = r_repl[0][7].reshape(())

    def leaf(kind):
        sm, rr = r_small[kind], r_repl[kind]
        return [sm[3], rr[0], r_in[kind], sm[0], rr[1], rr[2], rr[3], sm[2], rr[4], r_out[kind], rr[5], r_up[kind],
                sm[1], r_down[kind], rr[6]]

    return (loss, grad_x, *leaf(0), *leaf(1), *leaf(2), *leaf(3))
```

```python
import jax
import jax.numpy as jnp
from jax import lax
from jax.experimental import pallas as pl
from jax.experimental.pallas import tpu as pltpu

F32 = jnp.float32
BF16 = jnp.bfloat16
HI = lax.Precision.HIGHEST
MESH = pl.DeviceIdType.MESH

D = 1024
H = 8
DH = 128
CH = 64
N_META = 16
LEAD = 48
ROW0 = LEAD + N_META
QKV = 3 * D
POOL_W = 512
POOL_WINDOWS = (2, 4, 8, 16)
FB = 704
NDEV = 8
EPS = 1e-6
MM_TILES = 12
TE = 192
LANE = 128
SUB = 8
VMEM_CAP = 56 << 20

ADAM_LR, ADAM_B1, ADAM_B2, ADAM_EPS, ADAM_WD, ADAM_STEP = 0.001, 0.9, 0.999, 1e-08, 0.01, 10

_NN = (((1,), (0,)), ((), ()))
_NT = (((1,), (1,)), ((), ()))
_TN = (((0,), (0,)), ((), ()))


def _dot(a, b, dims=_NN, precision=None):
    return lax.dot_general(a, b, dims, precision=precision, preferred_element_type=F32)


def _bdot(a, b, dims=_NN):
    return _dot(a.astype(BF16), b.astype(BF16), dims)


def _nbytes(shape, dtype):
    n = 1
    for s in shape:
        n *= s
    return n * jnp.dtype(dtype).itemsize


def _params(sem, block_bytes):
    limit = min(VMEM_CAP, 2 * block_bytes + (20 << 20))
    return pltpu.CompilerParams(dimension_semantics=sem, vmem_limit_bytes=limit)


def _sigmoid(x):
    return 1.0 / (1.0 + jnp.exp(-x))


def _col_tile(n):
    for t in (1024, 512, 256, 128):
        if n % t == 0:
            return t
    return n


def _matmul(a, b, *, dims, grid, a_spec, b_spec, o_spec, out_shape, name, red_axis=None, res=None):
    def body(*refs):
        if res is None:
            a_ref, b_ref, o_ref = refs
        else:
            a_ref, b_ref, r_ref, o_ref = refs
        part = _dot(a_ref[...], b_ref[...], dims)
        if red_axis is None:
            if res is not None:
                part = part + r_ref[...]
            o_ref[...] = part.astype(o_ref.dtype)
        else:
            r = pl.program_id(red_axis)

            @pl.when(r == 0)
            def _():
                o_ref[...] = part + r_ref[...] if res is not None else part

            @pl.when(r > 0)
            def _():
                o_ref[...] += part

    def blk(spec, arr):
        return _nbytes([s for s in spec.block_shape if s is not None], arr.dtype)

    ins = [a, b] + ([res] if res is not None else [])
    specs = [a_spec, b_spec] + ([o_spec] if res is not None else [])
    nb = blk(a_spec, a) + blk(b_spec, b) + 2 * _nbytes([s for s in o_spec.block_shape if s is not None], F32)
    sem = tuple("arbitrary" if i == red_axis else "parallel" for i in range(len(grid)))
    return pl.pallas_call(
        body, name=name, grid=grid, in_specs=specs, out_specs=o_spec, out_shape=out_shape,
        compiler_params=_params(sem, nb),
    )(*ins)


def _mm(a, b, out_dtype, name, res=None, dims=_NN):
    m, k = a.shape
    n = b.shape[1] if dims == _NN else b.shape[0]
    tm, tn = m // MM_TILES, _col_tile(n)
    if dims == _NN:
        b_spec = pl.BlockSpec((k, tn), lambda j, i: (0, j))
    else:
        b_spec = pl.BlockSpec((tn, k), lambda j, i: (j, 0))
    return _matmul(
        a, b, dims=dims, grid=(n // tn, MM_TILES), a_spec=pl.BlockSpec((tm, k), lambda j, i: (i, 0)), b_spec=b_spec,
        o_spec=pl.BlockSpec((tm, tn), lambda j, i: (i, j)), out_shape=jax.ShapeDtypeStruct((m, n), out_dtype),
        name=name, res=res)


def _mm_tn(a, g, name):
    m, k = a.shape
    n = g.shape[1]
    tm, tn = m // MM_TILES, _col_tile(n)
    return _matmul(
        a, g, dims=_TN, grid=(n // tn, MM_TILES), red_axis=1, a_spec=pl.BlockSpec((tm, k), lambda j, i: (i, 0)),
        b_spec=pl.BlockSpec((tm, tn), lambda j, i: (i, j)), o_spec=pl.BlockSpec((k, tn), lambda j, i: (0, j)),
        out_shape=jax.ShapeDtypeStruct((k, n), F32), name=name)


def _mm_up(u, w_up, name):
    t = u.shape[0]
    g = w_up.shape[0]
    tm = t // MM_TILES
    return _matmul(
        u, w_up, dims=_NN, grid=(g, MM_TILES), a_spec=pl.BlockSpec((tm, D), lambda g_, i: (i, 0)),
        b_spec=pl.BlockSpec((None, D, FB), lambda g_, i: (g_, 0, 0)),
        o_spec=pl.BlockSpec((None, tm, FB), lambda g_, i: (g_, i, 0)),
        out_shape=jax.ShapeDtypeStruct((g, t, FB), BF16), name=name)


def _mm_blocks_red(a, b, name, dims, res=None):
    g, t, k = a.shape
    n = b.shape[2] if dims == _NN else b.shape[1]
    tm = t // MM_TILES
    return _matmul(
        a, b, dims=dims, grid=(MM_TILES, g), red_axis=1, a_spec=pl.BlockSpec((None, tm, k), lambda i, g_: (g_, i, 0)),
        b_spec=pl.BlockSpec((None,) + b.shape[1:], lambda i, g_: (g_, 0, 0)),
        o_spec=pl.BlockSpec((tm, n), lambda i, g_: (i, 0)), out_shape=jax.ShapeDtypeStruct((t, n), F32),
        name=name, res=res)


def _mm_to_blocks(a, b, name):
    t, k = a.shape
    g, n, _ = b.shape
    tm = t // MM_TILES
    return _matmul(
        a, b, dims=_NT, grid=(g, MM_TILES), a_spec=pl.BlockSpec((tm, k), lambda g_, i: (i, 0)),
        b_spec=pl.BlockSpec((None, n, k), lambda g_, i: (g_, 0, 0)),
        o_spec=pl.BlockSpec((None, tm, n), lambda g_, i: (g_, i, 0)),
        out_shape=jax.ShapeDtypeStruct((g, t, n), BF16), name=name)


def _mm_tn_blocks(a, g, name, a_blocked, g_blocked):
    nb = a.shape[0] if a_blocked else g.shape[0]
    t = a.shape[-2]
    k, n = a.shape[-1], g.shape[-1]
    tm = t // MM_TILES
    a_spec = (pl.BlockSpec((None, tm, k), lambda g_, i: (g_, i, 0)) if a_blocked
              else pl.BlockSpec((tm, k), lambda g_, i: (i, 0)))
    g_spec = (pl.BlockSpec((None, tm, n), lambda g_, i: (g_, i, 0)) if g_blocked
              else pl.BlockSpec((tm, n), lambda g_, i: (i, 0)))
    return _matmul(
        a, g, dims=_TN, grid=(nb, MM_TILES), red_axis=1, a_spec=a_spec, b_spec=g_spec,
        o_spec=pl.BlockSpec((None, k, n), lambda g_, i: (g_, 0, 0)),
        out_shape=jax.ShapeDtypeStruct((nb, k, n), F32), name=name)


def _mm_cols(a, b, out_dtype, name, dims):
    t = a.shape[0]
    g = b.shape[0]
    ka = a.shape[1] // g
    n = b.shape[2] if dims == _NN else b.shape[1]
    tm = t // MM_TILES
    return _matmul(
        a, b, dims=dims, grid=(g, MM_TILES), a_spec=pl.BlockSpec((tm, ka), lambda g_, i: (i, g_)),
        b_spec=pl.BlockSpec((None,) + b.shape[1:], lambda g_, i: (g_, 0, 0)),
        o_spec=pl.BlockSpec((tm, n), lambda g_, i: (i, g_)), out_shape=jax.ShapeDtypeStruct((t, g * n), out_dtype),
        name=name)


def _mm_tn_cols(a, g, nblk, name):
    t = a.shape[0]
    ka, n = a.shape[1] // nblk, g.shape[1] // nblk
    tm = t // MM_TILES
    return _matmul(
        a, g, dims=_TN, grid=(nblk, MM_TILES), red_axis=1, a_spec=pl.BlockSpec((tm, ka), lambda g_, i: (i, g_)),
        b_spec=pl.BlockSpec((tm, n), lambda g_, i: (i, g_)), o_spec=pl.BlockSpec((None, ka, n), lambda g_, i: (g_, 0, 0)),
        out_shape=jax.ShapeDtypeStruct((nblk, ka, n), F32), name=name)


def _rows(cols, n=None):
    if n is None:
        return pl.BlockSpec((TE, cols), lambda i: (i, 0))
    return pl.BlockSpec((TE, cols), lambda i: (n - 1 - i, 0))


def _whole(shape):
    return pl.BlockSpec(shape, lambda *_: (0,) * len(shape))


def _row_ids(i, rows=TE):
    return i * rows + lax.broadcasted_iota(jnp.int32, (rows, 1), 0)


def _rmsnorm_fwd(h, gain, name):
    t = h.shape[0]

    def body(h_ref, g_ref, u_ref):
        x = h_ref[...]
        r = lax.rsqrt(jnp.mean(x * x, axis=-1, keepdims=True) + EPS)
        u_ref[...] = (x * r * g_ref[...]).astype(BF16)

    return pl.pallas_call(
        body, name=name, grid=(t // TE,), in_specs=[_rows(D), _whole((1, D))], out_specs=_rows(D),
        out_shape=jax.ShapeDtypeStruct((t, D), BF16), compiler_params=_params(("parallel",), 3 * TE * D * 4),
    )(h, gain)


def _rmsnorm_bwd(x, du, dres, gain, name):
    t = x.shape[0]

    def body(x_ref, du_ref, dr_ref, g_ref, dx_ref, dxb_ref, dg_ref):
        i = pl.program_id(0)
        xv = x_ref[...]
        r = lax.rsqrt(jnp.mean(xv * xv, axis=-1, keepdims=True) + EPS)
        gdy = du_ref[...] * g_ref[...]
        dx = dr_ref[...] + r * gdy - xv * (r * r * r) * jnp.mean(xv * gdy, axis=-1, keepdims=True)
        dx = jnp.where(_row_ids(i) >= LEAD, dx, 0.0)
        dx_ref[...] = dx
        dxb_ref[...] = dx.astype(BF16)
        part = jnp.sum(du_ref[...] * xv * r, axis=0, keepdims=True)

        @pl.when(i == 0)
        def _():
            dg_ref[...] = part

        @pl.when(i > 0)
        def _():
            dg_ref[...] += part

    return pl.pallas_call(
        body, name=name, grid=(t // TE,), in_specs=[_rows(D), _rows(D), _rows(D), _whole((1, D))],
        out_specs=[_rows(D), _rows(D), _whole((1, D))],
        out_shape=[jax.ShapeDtypeStruct((t, D), F32), jax.ShapeDtypeStruct((t, D), BF16),
                   jax.ShapeDtypeStruct((1, D), F32)],
        compiler_params=_params(("arbitrary",), 5 * TE * D * 4),
    )(x, du, dres, gain)


def _loss_bwd(h, target, gain, name):
    t = h.shape[0]

    def body(h_ref, t_ref, g_ref, dx_ref, dxb_ref, dg_ref, loss_ref):
        i = pl.program_id(0)
        xv = h_ref[...]
        gain_v = g_ref[...]
        r = lax.rsqrt(jnp.mean(xv * xv, axis=-1, keepdims=True) + EPS)
        real = _row_ids(i) >= ROW0
        err = jnp.where(real, xv * r * gain_v - t_ref[...], 0.0)
        dy = err * (1.0 / D)
        gdy = dy * gain_v
        dx = r * gdy - xv * (r * r * r) * jnp.mean(xv * gdy, axis=-1, keepdims=True)
        dx_ref[...] = dx
        dxb_ref[...] = dx.astype(BF16)
        dgp = jnp.sum(dy * xv * r, axis=0, keepdims=True)
        lp = 0.5 * jnp.sum(jnp.mean(err * err, axis=-1, keepdims=True), axis=0, keepdims=True)

        @pl.when(i == 0)
        def _():
            dg_ref[...] = dgp
            loss_ref[...] = jnp.broadcast_to(lp, (1, LANE))

        @pl.when(i > 0)
        def _():
            dg_ref[...] += dgp
            loss_ref[...] += jnp.broadcast_to(lp, (1, LANE))

    return pl.pallas_call(
        body, name=name, grid=(t // TE,), in_specs=[_rows(D), _rows(D), _whole((1, D))],
        out_specs=[_rows(D), _rows(D), _whole((1, D)), _whole((1, LANE))],
        out_shape=[jax.ShapeDtypeStruct((t, D), F32), jax.ShapeDtypeStruct((t, D), BF16),
                   jax.ShapeDtypeStruct((1, D), F32), jax.ShapeDtypeStruct((1, LANE), F32)],
        compiler_params=_params(("arbitrary",), 4 * TE * D * 4),
    )(h, target, gain)


def _taps(ref, sl, first, n, k):
    return [ref[first - (k - 1) + j:first - (k - 1) + j + n, sl] for j in range(k)]


def _stage_causal(scr, tile, i):
    @pl.when(i == 0)
    def _():
        scr[0:SUB, :] = jnp.zeros((SUB, scr.shape[1]), F32)

    @pl.when(i > 0)
    def _():
        scr[0:SUB, :] = scr[TE:TE + SUB, :]

    scr[SUB:SUB + TE, :] = tile


def _stage_anticausal(scr, i):
    @pl.when(i == 0)
    def _():
        scr[TE:TE + SUB, :] = jnp.zeros((SUB, scr.shape[1]), F32)

    @pl.when(i > 0)
    def _():
        scr[TE:TE + SUB, :] = scr[0:SUB, :]


def _conv(tp, w):
    out = w[0:1] * tp[0]
    for j in range(1, len(tp)):
        out = out + w[j:j + 1] * tp[j]
    return out


def _conv_t(ds, sl, w):
    k = w.shape[0]
    out = w[k - 1:k] * ds[0:TE, sl]
    for j in range(k - 1):
        out = out + w[j:j + 1] * ds[k - 1 - j:k - 1 - j + TE, sl]
    return out


def _gdn_pre_fwd(x, w, name):
    t = x.shape[0]

    def body(x_ref, w_ref, o_ref, xs):
        _stage_causal(xs, x_ref[...].astype(F32), pl.program_id(0))
        for hh in range(3 * H):
            sl = slice(hh * DH, (hh + 1) * DH)
            cv = _conv(_taps(xs, sl, SUB, TE, 4), w_ref[:, sl])
            s = cv * _sigmoid(cv)
            if hh < 2 * H:
                s = s * lax.rsqrt(jnp.sum(s * s, axis=-1, keepdims=True) + EPS)
                if hh < H:
                    s = s * (DH ** -0.5)
            o_ref[:, sl] = s

    return pl.pallas_call(
        body, name=name, grid=(t // TE,), in_specs=[_rows(QKV), _whole((4, QKV))], out_specs=_rows(QKV),
        out_shape=jax.ShapeDtypeStruct((t, QKV), F32), scratch_shapes=[pltpu.VMEM((TE + SUB, QKV), F32)],
        compiler_params=_params(("arbitrary",), TE * QKV * 8),
    )(x, w)


def _gdn_pre_bwd(x, w, dqkv, name):
    t = x.shape[0]
    n = t // TE
    hb = TE // 16

    def body(x_ref, xp_ref, w_ref, d_ref, dx_ref, dw_ref, xs, ds):
        i = pl.program_id(0)

        @pl.when(i == 0)
        def _():
            dw_ref[...] = jnp.zeros_like(dw_ref)

        _stage_anticausal(ds, i)
        xs[0:SUB, :] = jnp.where(i == n - 1, 0.0, xp_ref[SUB:2 * SUB, :].astype(F32))
        xs[SUB:SUB + TE, :] = x_ref[...].astype(F32)
        for hh in range(3 * H):
            sl = slice(hh * DH, (hh + 1) * DH)
            wv = w_ref[:, sl]
            tp = _taps(xs, sl, SUB, TE, 4)
            cv = _conv(tp, wv)
            sg = _sigmoid(cv)
            s = cv * sg
            dsv = d_ref[:, sl]
            if hh < 2 * H:
                if hh < H:
                    dsv = dsv * (DH ** -0.5)
                r = lax.rsqrt(jnp.sum(s * s, axis=-1, keepdims=True) + EPS)
                dsv = r * dsv - s * (r * r * r) * jnp.sum(s * dsv, axis=-1, keepdims=True)
            dcv = dsv * (sg * (1.0 + cv * (1.0 - sg)))
            ds[0:TE, sl] = dcv
            dx_ref[:, sl] = _conv_t(ds, sl, wv).astype(BF16)
            dw_ref[:, sl] += jnp.concatenate([jnp.sum(tp[j] * dcv, axis=0, keepdims=True) for j in range(4)], axis=0)

    return pl.pallas_call(
        body, name=name, grid=(n,),
        in_specs=[_rows(QKV, n), pl.BlockSpec((16, QKV), lambda i: (jnp.maximum((n - 1 - i) * hb - 1, 0), 0)),
                  _whole((4, QKV)), _rows(QKV, n)],
        out_specs=[_rows(QKV, n), _whole((4, QKV))],
        out_shape=[jax.ShapeDtypeStruct((t, QKV), BF16), jax.ShapeDtypeStruct((4, QKV), F32)],
        scratch_shapes=[pltpu.VMEM((TE + SUB, QKV), F32), pltpu.VMEM((TE + SUB, QKV), F32)],
        compiler_params=_params(("arbitrary",), TE * QKV * 14),
    )(x, x, w, dqkv)


def _softplus(x):
    return jnp.maximum(x, 0.0) + jnp.log(1.0 + jnp.exp(-jnp.abs(x)))


def _gates_fwd(ba, a_row, dt_row, name):
    t = ba.shape[0]

    def body(ba_ref, a_ref, dt_ref, b_out, g_out):
        real = _row_ids(pl.program_id(0)) >= LEAD
        b_out[...] = jnp.where(real, _sigmoid(ba_ref[:, 0:LANE]), 0.0)
        g = -jnp.exp(a_ref[...]) * _softplus(ba_ref[:, LANE:2 * LANE] + dt_ref[...])
        g_out[...] = jnp.where(real, g, 0.0)

    return pl.pallas_call(
        body, name=name, grid=(t // TE,), in_specs=[_rows(2 * LANE), _whole((1, LANE)), _whole((1, LANE))],
        out_specs=[_rows(LANE), _rows(LANE)],
        out_shape=[jax.ShapeDtypeStruct((t, LANE), F32), jax.ShapeDtypeStruct((t, LANE), F32)],
        compiler_params=_params(("parallel",), TE * LANE * 16),
    )(ba, a_row, dt_row)


def _gates_bwd(ba, a_row, dt_row, dbeta, dg, name):
    t = ba.shape[0]

    def body(ba_ref, a_ref, dt_ref, db_ref, dg_ref, dba_ref, da_out, ddt_out):
        i = pl.program_id(0)
        real = _row_ids(i) >= LEAD
        beta = _sigmoid(ba_ref[:, 0:LANE])
        draw_b = jnp.where(real, db_ref[...] * beta * (1.0 - beta), 0.0)
        pre = ba_ref[:, LANE:2 * LANE] + dt_ref[...]
        neg_a = -jnp.exp(a_ref[...])
        dgv = jnp.where(real, dg_ref[...], 0.0)
        draw_a = dgv * neg_a * _sigmoid(pre)
        dba_ref[:, 0:LANE] = draw_b.astype(BF16)
        dba_ref[:, LANE:2 * LANE] = draw_a.astype(BF16)
        dal = jnp.sum(dgv * neg_a * _softplus(pre), axis=0, keepdims=True)
        ddt = jnp.sum(draw_a, axis=0, keepdims=True)

        @pl.when(i == 0)
        def _():
            da_out[...] = dal
            ddt_out[...] = ddt

        @pl.when(i > 0)
        def _():
            da_out[...] += dal
            ddt_out[...] += ddt

    return pl.pallas_call(
        body, name=name, grid=(t // TE,),
        in_specs=[_rows(2 * LANE), _whole((1, LANE)), _whole((1, LANE)), _rows(LANE), _rows(LANE)],
        out_specs=[_rows(2 * LANE), _whole((1, LANE)), _whole((1, LANE))],
        out_shape=[jax.ShapeDtypeStruct((t, 2 * LANE), BF16), jax.ShapeDtypeStruct((1, LANE), F32),
                   jax.ShapeDtypeStruct((1, LANE), F32)],
        compiler_params=_params(("arbitrary",), TE * LANE * 24),
    )(ba, a_row, dt_row, dbeta, dg)


_OFFSETS = [(dx, dy, dc) for dx in (0, 1) for dy in (0, 1) for dc in (0, 1)][1:]
NPEER = len(_OFFSETS)
ANY_SPEC = pl.BlockSpec(memory_space=pl.ANY)


def _place():
    return lax.axis_index("x"), lax.axis_index("y"), lax.axis_index("c")


def _index(p):
    return 4 * p[0] + 2 * p[1] + p[2]


def _comm_scratch(n):
    return [pltpu.SemaphoreType.DMA((n * NPEER,)), pltpu.SemaphoreType.DMA((n * NPEER,)), pltpu.SemaphoreType.DMA((n,))]


def _scatter_copies(ins, outs, send, recv):
    me = _place()
    mi = _index(me)
    res = []
    for j, d in enumerate(_OFFSETS):
        peer = tuple(1 - v if bit else v for v, bit in zip(me, d))
        pi = _index(peer)
        for k in range(len(ins)):
            sem = k * NPEER + j
            mine = pltpu.make_async_remote_copy(src_ref=ins[k].at[pi], dst_ref=outs[k].at[mi], send_sem=send.at[sem],
                                                recv_sem=recv.at[sem], device_id=peer, device_id_type=MESH)
            theirs = pltpu.make_async_remote_copy(src_ref=ins[k].at[pi], dst_ref=outs[k].at[pi], send_sem=send.at[sem],
                                                  recv_sem=recv.at[sem], device_id=peer, device_id_type=MESH)
            res.append((mine, theirs))
    return res


def _scatter_own(ins, outs, loc):
    mi = _index(_place())
    return [pltpu.make_async_copy(ins[k].at[mi], outs[k].at[mi], loc.at[k]) for k in range(len(ins))]


def _scatter_start(ins, outs, send, recv, loc):
    for cp in _scatter_own(ins, outs, loc):
        cp.start()
    for mine, _ in _scatter_copies(ins, outs, send, recv):
        mine.start()


def _scatter_wait(ins, outs, send, recv, loc):
    cps = _scatter_copies(ins, outs, send, recv)
    for _, theirs in cps:
        theirs.wait_recv()
    for mine, _ in cps:
        mine.wait_send()
    for cp in _scatter_own(ins, outs, loc):
        cp.wait()


def _gather_parts(ins, outs, send, recv):
    x, y, c = _place()
    chips = [(1 - x, y), (x, 1 - y), (1 - x, 1 - y)]

    def cp(k, slot, src, block, to):
        return pltpu.make_async_remote_copy(src_ref=src, dst_ref=outs[k].at[_index(block)], send_sem=send.at[k * NPEER + slot],
                                            recv_sem=recv.at[k * NPEER + slot], device_id=to, device_id_type=MESH)

    return (x, y, c), (x, y, 1 - c), chips, cp


def _gather_start(ins, outs, send, recv, loc):
    me, sib, chips, cp = _gather_parts(ins, outs, send, recv)
    for k in range(len(ins)):
        pltpu.make_async_copy(ins[k], outs[k].at[_index(me)], loc.at[k]).start()
        cp(k, 0, ins[k], me, sib).start()
        for j, chip in enumerate(chips):
            cp(k, 1 + j, ins[k], me, (*chip, me[2])).start()


def _gather_forward(ins, outs, send, recv, loc):
    me, sib, chips, cp = _gather_parts(ins, outs, send, recv)
    for j, chip in enumerate(chips):
        blk = (*chip, me[2])
        for k in range(len(ins)):
            cp(k, 1 + j, ins[k], blk, me).wait_recv()
            cp(k, 4 + j, outs[k].at[_index(blk)], blk, sib).start()


def _gather_finish(ins, outs, send, recv, loc):
    me, sib, chips, cp = _gather_parts(ins, outs, send, recv)
    for k in range(len(ins)):
        cp(k, 0, ins[k], sib, me).wait_recv()
        for j, chip in enumerate(chips):
            cp(k, 4 + j, ins[k], (*chip, sib[2]), me).wait_recv()
        cp(k, 0, ins[k], me, sib).wait_send()
        for j, chip in enumerate(chips):
            cp(k, 1 + j, ins[k], me, (*chip, me[2])).wait_send()
            cp(k, 4 + j, outs[k].at[_index((*chip, me[2]))], (*chip, me[2]), sib).wait_send()
        pltpu.make_async_copy(ins[k], outs[k].at[_index(me)], loc.at[k]).wait()


def _gathered_shapes(arrs):
    return [jax.ShapeDtypeStruct((NDEV,) + a.shape, a.dtype) for a in arrs]


def _gather(arrs, name):
    n = len(arrs)

    def body(*refs):
        ins, outs, sems = refs[:n], refs[n:2 * n], refs[2 * n:]
        _gather_start(ins, outs, *sems)
        _gather_forward(ins, outs, *sems)
        _gather_finish(ins, outs, *sems)

    return pl.pallas_call(body, name=name, in_specs=[ANY_SPEC] * n, out_specs=[ANY_SPEC] * n,
                          out_shape=_gathered_shapes(arrs), scratch_shapes=_comm_scratch(n))(*arrs)


def _scatter(arrs, name):
    n = len(arrs)

    def body(*refs):
        ins, outs, sems = refs[:n], refs[n:2 * n], refs[2 * n:]
        _scatter_start(ins, outs, *sems)
        _scatter_wait(ins, outs, *sems)

    return pl.pallas_call(body, name=name, in_specs=[ANY_SPEC] * n, out_specs=[ANY_SPEC] * n,
                          out_shape=[jax.ShapeDtypeStruct(a.shape, a.dtype) for a in arrs],
                          scratch_shapes=_comm_scratch(n))(*arrs)


_BNN = (((2,), (1,)), ((0,), (0,)))
_BNT = (((2,), (2,)), ((0,), (0,)))
_BTN = (((1,), (1,)), ((0,), (0,)))


def _split(a):
    hi = a.astype(BF16)
    return hi, (a - hi.astype(F32)).astype(BF16)


def _dot3(a, b):
    ah, al = _split(a)
    bh, bl = _split(b)
    m = a.shape[1]
    r = _dot(jnp.concatenate([ah, al], axis=1), bh, _BNN)
    return r[:, :m] + r[:, m:] + _dot(ah, bl, _BNN)


def _heads(ref, off):
    return jnp.stack([ref[:, off + h * DH:off + (h + 1) * DH] for h in range(H)])


def _cols(a):
    return jnp.stack([a[:, h:h + 1] for h in range(H)])


def _lanes(a):
    lane = lax.broadcasted_iota(jnp.int32, (CH, LANE), 1)
    out = jnp.zeros((CH, LANE), F32)
    for h in range(H):
        out = jnp.where(lane == h, a[h], out)
    return out


def _chunk_prep(qkv_ref, b_ref, g_ref):
    row = lax.broadcasted_iota(jnp.int32, (CH, CH), 0)
    col = lax.broadcasted_iota(jnp.int32, (CH, CH), 1)
    incl, strict = row >= col, row > col
    gc = _dot(incl.astype(F32), g_ref[...], precision=HI)
    gct = gc.T
    q, k, v = _heads(qkv_ref, 0), _heads(qkv_ref, D), _heads(qkv_ref, 2 * D)
    bcol, gcol = _cols(b_ref[...]), _cols(gc)
    grow = jnp.stack([gct[h:h + 1, :] for h in range(H)])
    glast = _cols(gc[CH - 1:CH, :])
    dec = jnp.exp(jnp.where(incl[None], gcol - grow, -1e30))
    kb = k * bcol
    ab = _bdot(jnp.concatenate([kb, q], axis=1), k, _BNT)
    egc, ekc = jnp.exp(gcol), jnp.exp(glast - gcol)
    return dict(row=row, col=col, strict=strict[None], q=q, k=k, v=v, bcol=bcol, dec=dec, kb=kb,
                lm=jnp.where(strict[None], ab[:, :CH] * dec, 0.0), qk=ab[:, CH:] * dec, egc=egc, ekc=ekc,
                gth=jnp.exp(glast), qd=q * egc, kd=k * ekc, vb=v * bcol, kbg=kb * egc)


def _unit_lower_inverse(lm, eye):
    n = -lm
    x = eye + n
    pw = _dot3(n, n)
    for it in range(5):
        if it < 4:
            xp = _dot3(jnp.concatenate([x, pw], axis=1), pw)
            x = x + xp[:, :CH]
            pw = xp[:, CH:]
        else:
            x = x + _dot3(x, pw)
    return x


def _gdn_fwd(qkv, beta, g, name, gather=()):
    t = qkv.shape[0]
    nc = t // CH
    ng = len(gather)

    def body(qkv_ref, b_ref, g_ref, *rest):
        c_ins, (o_ref, sin_ref, vn_ref, ti_ref, w_ref) = rest[:ng], rest[ng:ng + 5]
        c_outs, state, sems = rest[ng + 5:2 * ng + 5], rest[2 * ng + 5], rest[2 * ng + 6:]
        step = pl.program_id(0)

        @pl.when(step == 0)
        def _():
            state[...] = jnp.zeros_like(state)
            if ng:
                _gather_start(c_ins, c_outs, *sems)

        if ng:
            @pl.when(step == nc // 2)
            def _():
                _gather_forward(c_ins, c_outs, *sems)

            @pl.when(step == nc - 1)
            def _():
                _gather_finish(c_ins, c_outs, *sems)

        pr = _chunk_prep(qkv_ref, b_ref, g_ref)
        tinv = _unit_lower_inverse(pr["lm"], (pr["row"] == pr["col"]).astype(F32)[None])
        uw = _bdot(tinv, jnp.concatenate([pr["vb"], pr["kbg"]], axis=2), _BNN)
        u, w = uw[:, :, :DH], uw[:, :, DH:]
        s = state[...]
        ws = _bdot(jnp.concatenate([w, pr["qd"]], axis=1), s, _BNN)
        vn = u - ws[:, :CH]
        o = ws[:, CH:] + _bdot(pr["qk"], vn, _BNN)
        state[...] = s * pr["gth"] + _bdot(pr["kd"], vn, _BTN)
        sin_ref[0] = s
        ti_ref[0] = tinv
        for h in range(H):
            sl = slice(h * DH, (h + 1) * DH)
            o_ref[:, sl] = o[h]
            vn_ref[:, sl] = vn[h]
            w_ref[:, sl] = w[h]

    chunk = lambda cols: pl.BlockSpec((CH, cols), lambda c: (c, 0))
    outs = pl.pallas_call(
        body, name=name, grid=(nc,), in_specs=[chunk(QKV), chunk(LANE), chunk(LANE)] + [ANY_SPEC] * ng,
        out_specs=[chunk(D), pl.BlockSpec((1, H, DH, DH), lambda c: (c, 0, 0, 0)), chunk(D),
                   pl.BlockSpec((1, H, CH, CH), lambda c: (c, 0, 0, 0)), chunk(D)] + [ANY_SPEC] * ng,
        out_shape=[jax.ShapeDtypeStruct((t, D), F32), jax.ShapeDtypeStruct((nc, H, DH, DH), F32),
                   jax.ShapeDtypeStruct((t, D), F32), jax.ShapeDtypeStruct((nc, H, CH, CH), F32),
                   jax.ShapeDtypeStruct((t, D), F32)] + _gathered_shapes(gather),
        scratch_shapes=[pltpu.VMEM((H, DH, DH), F32)] + (_comm_scratch(ng) if ng else []),
        compiler_params=_params(("arbitrary",), 4 << 20),
    )(qkv, beta, g, *gather)
    return outs[:5], outs[5:]


def _gdn_bwd(qkv, beta, g, do, s_in, vnew, tinv, wsv, name, scatter=()):
    t = qkv.shape[0]
    nc = t // CH
    ns = len(scatter)

    def body(qkv_ref, b_ref, g_ref, do_ref, sin_ref, vn_ref, ti_ref, w_ref, *rest):
        c_ins, (dqkv_ref, db_ref, dg_ref) = rest[:ns], rest[ns:ns + 3]
        c_outs, dstate, sems = rest[ns + 3:2 * ns + 3], rest[2 * ns + 3], rest[2 * ns + 4:]
        step = pl.program_id(0)

        @pl.when(step == 0)
        def _():
            dstate[...] = jnp.zeros_like(dstate)
            if ns:
                _scatter_start(c_ins, c_outs, *sems)

        if ns:
            @pl.when(step == nc - 1)
            def _():
                _scatter_wait(c_ins, c_outs, *sems)

        pr = _chunk_prep(qkv_ref, b_ref, g_ref)
        ti, s = ti_ref[0], sin_ref[0]
        w, vn, doh = _heads(w_ref, 0), _heads(vn_ref, 0), _heads(do_ref, 0)
        ds = dstate[...]
        dvn = _bdot(pr["kd"], ds, _BNN) + _bdot(pr["qk"], doh, _BTN)
        dkd = _bdot(vn, ds, _BNT)
        dqd = _bdot(doh, s, _BNT)
        dqk = _bdot(doh, vn, _BNT)
        dw = -_bdot(dvn, s, _BNT)
        dgt = jnp.sum(jnp.sum(ds * s, axis=2, keepdims=True), axis=1, keepdims=True)
        dstate[...] = ds * pr["gth"] + _bdot(pr["qd"], doh, _BTN) - _bdot(w, dvn, _BTN)
        duw = jnp.concatenate([dvn, dw], axis=2)
        dvk = _bdot(ti, duw, _BTN)
        dvb, dkbg = dvk[:, :, :DH], dvk[:, :, DH:]
        dti = _bdot(duw, jnp.concatenate([pr["vb"], pr["kbg"]], axis=2), _BNT)
        dl = -_dot(_dot(ti, dti, _BTN, precision=HI), ti, _BNT, precision=HI)
        dl = jnp.where(pr["strict"], dl, 0.0)
        dab = jnp.concatenate([dl * pr["dec"], dqk * pr["dec"]], axis=1)
        r1 = _bdot(dab, pr["k"], _BNN)
        dkb = r1[:, :CH] + dkbg * pr["egc"]
        dq = r1[:, CH:] + dqd * pr["egc"]
        dk = _bdot(dab, jnp.concatenate([pr["kb"], pr["q"]], axis=1), _BTN) + dkb * pr["bcol"] + dkd * pr["ekc"]
        m = dl * pr["lm"] + dqk * pr["qk"]
        colsum = _dot(m, jnp.ones((H, CH, LANE), F32), _BTN, precision=HI)[:, :, 0:1]
        kdsum = jnp.sum(dkd * pr["kd"], axis=2, keepdims=True)
        dgc = (jnp.sum(m, axis=2, keepdims=True) - colsum + jnp.sum(dkbg * pr["kbg"], axis=2, keepdims=True)
               + jnp.sum(dqd * pr["qd"], axis=2, keepdims=True) - kdsum)
        dglast = jnp.sum(kdsum, axis=1, keepdims=True) + dgt * pr["gth"]
        last_row = lax.broadcasted_iota(jnp.int32, (1, CH, 1), 1) == CH - 1
        dgc = dgc + jnp.where(last_row, dglast, 0.0)
        dbeta = jnp.sum(dkb * pr["k"], axis=2, keepdims=True) + jnp.sum(dvb * pr["v"], axis=2, keepdims=True)
        dv = dvb * pr["bcol"]
        for h in range(H):
            dqkv_ref[:, h * DH:(h + 1) * DH] = dq[h]
            dqkv_ref[:, D + h * DH:D + (h + 1) * DH] = dk[h]
            dqkv_ref[:, 2 * D + h * DH:2 * D + (h + 1) * DH] = dv[h]
        db_ref[...] = _lanes(dbeta)
        dg_ref[...] = _dot((pr["row"] <= pr["col"]).astype(F32), _lanes(dgc), precision=HI)

    chunk = lambda cols: pl.BlockSpec((CH, cols), lambda c: (nc - 1 - c, 0))
    sq = lambda a, b: pl.BlockSpec((1, H, a, b), lambda c: (nc - 1 - c, 0, 0, 0))
    outs = pl.pallas_call(
        body, name=name, grid=(nc,),
        in_specs=[chunk(QKV), chunk(LANE), chunk(LANE), chunk(D), sq(DH, DH), chunk(D), sq(CH, CH), chunk(D)] + [ANY_SPEC] * ns,
        out_specs=[chunk(QKV), chunk(LANE), chunk(LANE)] + [ANY_SPEC] * ns,
        out_shape=[jax.ShapeDtypeStruct((t, QKV), F32), jax.ShapeDtypeStruct((t, LANE), F32),
                   jax.ShapeDtypeStruct((t, LANE), F32)] + [jax.ShapeDtypeStruct(a.shape, a.dtype) for a in scatter],
        scratch_shapes=[pltpu.VMEM((H, DH, DH), F32)] + (_comm_scratch(ns) if ns else []),
        compiler_params=_params(("arbitrary",), 6 << 20),
    )(qkv, beta, g, do, s_in, vnew, tinv, wsv, *scatter)
    return outs[:3], outs[3:]


def _pool_counts(row_ids, win):
    return jnp.minimum(jnp.maximum(row_ids - LEAD, 0) + 1, win).astype(F32)


def _pool_fwd(p, name):
    t = p.shape[0]
    ext = TE + 16

    def body(p_ref, o_ref, carry):
        i = pl.program_id(0)

        @pl.when(i == 0)
        def _():
            carry[...] = jnp.zeros_like(carry)

        ids = _row_ids(i)
        for gi, win in enumerate(POOL_WINDOWS):
            sl = slice(gi * LANE, (gi + 1) * LANE)
            xv = p_ref[:, sl]
            s = jnp.concatenate([carry[:, sl], xv], axis=0)
            sh = 1
            while sh < win:
                s = s + pltpu.roll(s, sh, 0)
                sh *= 2
            o_ref[:, sl] = (s[16:ext] / _pool_counts(ids, win) - xv).astype(BF16)
            carry[:, sl] = xv[TE - 16:TE]

    return pl.pallas_call(
        body, name=name, grid=(t // TE,), in_specs=[_rows(POOL_W)], out_specs=_rows(POOL_W),
        out_shape=jax.ShapeDtypeStruct((t, POOL_W), BF16), scratch_shapes=[pltpu.VMEM((16, POOL_W), F32)],
        compiler_params=_params(("arbitrary",), TE * POOL_W * 8),
    )(p)


def _pool_bwd(dpo, name):
    t = dpo.shape[0]
    n = t // TE
    ext = TE + 16

    def body(d_ref, o_ref, carry):
        i = pl.program_id(0)

        @pl.when(i == 0)
        def _():
            carry[...] = jnp.zeros_like(carry)

        ids = _row_ids(n - 1 - i)
        for gi, win in enumerate(POOL_WINDOWS):
            sl = slice(gi * LANE, (gi + 1) * LANE)
            dv = d_ref[:, sl]
            rv = dv / _pool_counts(ids, win)
            s = jnp.concatenate([rv, carry[:, sl]], axis=0)
            sh = 1
            while sh < win:
                s = s + pltpu.roll(s, ext - sh, 0)
                sh *= 2
            o_ref[:, sl] = (s[0:TE] - dv).astype(BF16)
            carry[:, sl] = rv[0:16]

    return pl.pallas_call(
        body, name=name, grid=(n,), in_specs=[_rows(POOL_W, n)], out_specs=_rows(POOL_W, n),
        out_shape=jax.ShapeDtypeStruct((t, POOL_W), BF16), scratch_shapes=[pltpu.VMEM((16, POOL_W), F32)],
        compiler_params=_params(("arbitrary",), TE * POOL_W * 8),
    )(dpo)


def _post_fwd(o, z, gate, pm, hn, ps, name):
    t = o.shape[0]

    def body(o_ref, z_ref, g_ref, pm_ref, hn_ref, ps_ref, y_ref):
        for h in range(H):
            sl = slice(h * DH, (h + 1) * DH)
            ov = o_ref[:, sl]
            zv = z_ref[:, sl].astype(F32)
            r = lax.rsqrt(jnp.mean(ov * ov, axis=-1, keepdims=True) + EPS)
            ya = ov * r * hn_ref[...] * (zv * _sigmoid(zv))
            ga = _sigmoid(g_ref[:, sl].astype(F32))
            gb = _sigmoid(g_ref[:, D + h * DH:D + (h + 1) * DH].astype(F32))
            y_ref[:, sl] = (ga * ya + gb * (pm_ref[:, sl] * ps_ref[:, sl])).astype(BF16)

    return pl.pallas_call(
        body, name=name, grid=(t // TE,),
        in_specs=[_rows(D), _rows(D), _rows(2 * D), _rows(D), _whole((1, DH)), _whole((1, D))], out_specs=_rows(D),
        out_shape=jax.ShapeDtypeStruct((t, D), BF16), compiler_params=_params(("parallel",), TE * D * 16),
    )(o, z, gate, pm, hn, ps)


def _post_bwd(dy, o, z, gate, pm, hn, ps, name):
    t = o.shape[0]

    def body(dy_ref, o_ref, z_ref, g_ref, pm_ref, hn_ref, ps_ref, do_ref, dz_ref, dgate_ref, dpm_ref, dhn_ref, dps_ref):
        i = pl.program_id(0)

        @pl.when(i == 0)
        def _():
            dhn_ref[...] = jnp.zeros_like(dhn_ref)
            dps_ref[...] = jnp.zeros_like(dps_ref)

        hnv = hn_ref[...]
        dhn = jnp.zeros((1, DH), F32)
        for h in range(H):
            sl = slice(h * DH, (h + 1) * DH)
            slb = slice(D + h * DH, D + (h + 1) * DH)
            dyv = dy_ref[:, sl]
            ov = o_ref[:, sl]
            zv = z_ref[:, sl].astype(F32)
            r = lax.rsqrt(jnp.mean(ov * ov, axis=-1, keepdims=True) + EPS)
            sz = _sigmoid(zv)
            silu = zv * sz
            on = ov * r
            ya = on * hnv * silu
            ga = _sigmoid(g_ref[:, sl].astype(F32))
            gb = _sigmoid(g_ref[:, slb].astype(F32))
            pmv = pm_ref[:, sl]
            psv = ps_ref[:, sl]
            dya = dyv * ga
            dyb = dyv * gb
            dgate_ref[:, sl] = (dyv * ya * ga * (1.0 - ga)).astype(BF16)
            dgate_ref[:, slb] = (dyv * (pmv * psv) * gb * (1.0 - gb)).astype(BF16)
            tt = dya * hnv * silu
            do_ref[:, sl] = r * tt - ov * (r * r * r) * jnp.mean(ov * tt, axis=-1, keepdims=True)
            dz_ref[:, sl] = (dya * on * hnv * (sz * (1.0 + zv * (1.0 - sz)))).astype(BF16)
            dhn = dhn + jnp.sum(dya * on * silu, axis=0, keepdims=True)
            dps_ref[:, sl] += jnp.sum(dyb * pmv, axis=0, keepdims=True)
            dpm_ref[:, sl] = (dyb * psv).astype(BF16)
        dhn_ref[...] += dhn

    return pl.pallas_call(
        body, name=name, grid=(t // TE,),
        in_specs=[_rows(D), _rows(D), _rows(D), _rows(2 * D), _rows(D), _whole((1, DH)), _whole((1, D))],
        out_specs=[_rows(D), _rows(D), _rows(2 * D), _rows(D), _whole((1, DH)), _whole((1, D))],
        out_shape=[jax.ShapeDtypeStruct((t, D), F32), jax.ShapeDtypeStruct((t, D), BF16),
                   jax.ShapeDtypeStruct((t, 2 * D), BF16), jax.ShapeDtypeStruct((t, D), BF16),
                   jax.ShapeDtypeStruct((1, DH), F32), jax.ShapeDtypeStruct((1, D), F32)],
        compiler_params=_params(("arbitrary",), TE * D * 28),
    )(dy, o, z, gate, pm, hn, ps)


_FB_COLS = [(c, min(c + LANE, FB)) for c in range(0, FB, LANE)]


def _mlp_act_fwd(hid, cw, name):
    t = hid.shape[1]
    n = t // TE

    def body(hg_ref, hv_ref, wg_ref, wv_ref, a_ref, xg, xv):
        i = pl.program_id(1)
        _stage_causal(xg, hg_ref[...].astype(F32), i)
        _stage_causal(xv, hv_ref[...].astype(F32), i)
        for c0, c1 in _FB_COLS:
            sl = slice(c0, c1)
            gg = _conv(_taps(xg, sl, SUB, TE, 3), wg_ref[:, sl])
            vv = _conv(_taps(xv, sl, SUB, TE, 3), wv_ref[:, sl])
            a_ref[:, sl] = (gg * _sigmoid(gg) * vv).astype(BF16)

    hspec = lambda off: pl.BlockSpec((None, TE, FB), lambda p, i: (p + off, i, 0))
    wspec = lambda off: pl.BlockSpec((None, 3, FB), lambda p, i: (p + off, 0, 0))
    return pl.pallas_call(
        body, name=name, grid=(4, n), in_specs=[hspec(0), hspec(4), wspec(0), wspec(4)],
        out_specs=pl.BlockSpec((None, TE, FB), lambda p, i: (p, i, 0)),
        out_shape=jax.ShapeDtypeStruct((4, t, FB), BF16),
        scratch_shapes=[pltpu.VMEM((TE + SUB, FB), F32), pltpu.VMEM((TE + SUB, FB), F32)],
        compiler_params=_params(("parallel", "arbitrary"), TE * FB * 12),
    )(hid, hid, cw, cw)


def _mlp_act_bwd(da, hid, cw, name):
    t = hid.shape[1]
    n = t // TE
    hb = TE // 16

    def body(da_ref, hg_ref, hv_ref, pg_ref, pv_ref, wg_ref, wv_ref, dhg_ref, dhv_ref, dwg_ref, dwv_ref, xg, xv, dg, dv):
        i = pl.program_id(1)

        @pl.when(i == 0)
        def _():
            dwg_ref[...] = jnp.zeros_like(dwg_ref)
            dwv_ref[...] = jnp.zeros_like(dwv_ref)

        _stage_anticausal(dg, i)
        _stage_anticausal(dv, i)
        first_tile = i == n - 1
        for scr, p_ref, h_ref in ((xg, pg_ref, hg_ref), (xv, pv_ref, hv_ref)):
            scr[0:SUB, :] = jnp.where(first_tile, 0.0, p_ref[SUB:2 * SUB, :].astype(F32))
            scr[SUB:SUB + TE, :] = h_ref[...].astype(F32)
        for c0, c1 in _FB_COLS:
            sl = slice(c0, c1)
            wg = wg_ref[:, sl]
            wv = wv_ref[:, sl]
            tg = _taps(xg, sl, SUB, TE, 3)
            tv = _taps(xv, sl, SUB, TE, 3)
            gg = _conv(tg, wg)
            vv = _conv(tv, wv)
            sg = _sigmoid(gg)
            dav = da_ref[:, sl].astype(F32)
            dgg = dav * vv * (sg * (1.0 + gg * (1.0 - sg)))
            dvv = dav * (gg * sg)
            for dc, tp, w, scr, dh_ref, dw_ref in ((dgg, tg, wg, dg, dhg_ref, dwg_ref), (dvv, tv, wv, dv, dhv_ref, dwv_ref)):
                scr[0:TE, sl] = dc
                dh_ref[:, sl] = _conv_t(scr, sl, w).astype(BF16)
                dw_ref[:, sl] += jnp.concatenate([jnp.sum(tp[j] * dc, axis=0, keepdims=True) for j in range(3)], axis=0)

    rev = lambda off: pl.BlockSpec((None, TE, FB), lambda p, i: (p + off, n - 1 - i, 0))
    halo = lambda off: pl.BlockSpec((None, 16, FB), lambda p, i: (p + off, jnp.maximum((n - 1 - i) * hb - 1, 0), 0))
    wspec = lambda off: pl.BlockSpec((None, 3, FB), lambda p, i: (p + off, 0, 0))
    dwspec = pl.BlockSpec((None, 3, FB), lambda p, i: (p, 0, 0))
    return pl.pallas_call(
        body, name=name, grid=(4, n), in_specs=[rev(0), rev(0), rev(4), halo(0), halo(4), wspec(0), wspec(4)],
        out_specs=[rev(0), rev(0), dwspec, dwspec],
        out_shape=[jax.ShapeDtypeStruct((4, t, FB), BF16), jax.ShapeDtypeStruct((4, t, FB), BF16),
                   jax.ShapeDtypeStruct((4, 3, FB), F32), jax.ShapeDtypeStruct((4, 3, FB), F32)],
        scratch_shapes=[pltpu.VMEM((TE + SUB, FB), F32)] * 4,
        compiler_params=_params(("parallel", "arbitrary"), TE * FB * 24),
    )(da, hid, hid, hid, hid, cw, cw)


def _adamw(land, w, m, v, name):
    r, c = w.shape
    tr = r
    for cand in (128, 64, 32, 16):
        if r % cand == 0:
            tr = cand
            break
    c1 = 1.0 - ADAM_B1 ** ADAM_STEP
    c2 = 1.0 - ADAM_B2 ** ADAM_STEP

    def body(l_ref, w_ref, m_ref, v_ref, g_out, d_out, m_out, v_out):
        g = l_ref[0].astype(F32)
        for i in range(1, NDEV):
            g = g + l_ref[i].astype(F32)
        mn = ADAM_B1 * m_ref[...] + (1.0 - ADAM_B1) * g
        vn = ADAM_B2 * v_ref[...] + (1.0 - ADAM_B2) * (g * g)
        g_out[...] = g
        m_out[...] = mn
        v_out[...] = vn
        d_out[...] = -ADAM_LR * ((mn / c1) / (jnp.sqrt(vn / c2) + ADAM_EPS) + ADAM_WD * w_ref[...])

    spec = pl.BlockSpec((tr, c), lambda i: (i, 0))
    shp = jax.ShapeDtypeStruct((r, c), F32)
    return pl.pallas_call(
        body, name=name, grid=(r // tr,), in_specs=[pl.BlockSpec((NDEV, tr, c), lambda i: (0, i, 0)), spec, spec, spec],
        out_specs=[spec] * 4, out_shape=[shp] * 4, compiler_params=_params(("parallel",), 15 * tr * c * 4),
    )(land, w, m, v)


def _layer_fwd(h, p, tag, gather=()):
    u = _rmsnorm_fwd(h, p["norm_mix"], f"norm_mix_{tag}")
    qkv_pre = _mm(u, p["w_qkv"], BF16, f"proj_qkv_{tag}")
    z = _mm(u, p["w_z"], BF16, f"proj_z_{tag}")
    ba = _mm(u, p["w_ba"], F32, f"proj_ba_{tag}")
    pool_in = _mm(u, p["w_pl"], F32, f"proj_pool_{tag}")
    gate = _mm(u, p["w_gate"], BF16, f"proj_gate_{tag}")
    qkv = _gdn_pre_fwd(qkv_pre, p["conv_qkv"], f"gdn_pre_{tag}")
    beta, g = _gates_fwd(ba, p["a_row"], p["dt_row"], f"gates_{tag}")
    (o, s_in, vnew, tinv, wsv), gathered = _gdn_fwd(qkv, beta, g, f"gdn_{tag}", gather)
    pooled = _pool_fwd(pool_in, f"pool_{tag}")
    pm = _mm_cols(pooled, p["w_pool"], F32, f"pool_mm_{tag}", _NN)
    y = _post_fwd(o, z, gate, pm, p["head_norm"], p["pool_scale"], f"post_{tag}")
    h1 = _mm(y, p["w_out"], F32, f"out_proj_{tag}", res=h)
    u2 = _rmsnorm_fwd(h1, p["norm_ffn"], f"norm_ffn_{tag}")
    hid = _mm_up(u2, p["w_up"], f"up_proj_{tag}")
    act = _mlp_act_fwd(hid, p["conv_ffn"], f"mlp_act_{tag}")
    h2 = _mm_blocks_red(act, p["w_down"], f"down_proj_{tag}", _NN, res=h1)
    saved = dict(h=h, u=u, qkv_pre=qkv_pre, z=z, ba=ba, gate=gate, qkv=qkv, beta=beta, g=g, o=o, s_in=s_in, vnew=vnew,
                 tinv=tinv, wsv=wsv, pooled=pooled, pm=pm, y=y, h1=h1, u2=u2, hid=hid, act=act)
    return h2, saved, gathered


def _layer_bwd(dh, dh_b, p, s, tag, scatter=()):
    gr = {}
    da = _mm_to_blocks(dh_b, p["w_down"], f"d_act_{tag}")
    gr["w_down"] = _mm_tn_blocks(s["act"], dh_b, f"dw_down_{tag}", True, False)
    dhg, dhv, dwg, dwv = _mlp_act_bwd(da, s["hid"], p["conv_ffn"], f"mlp_act_bwd_{tag}")
    gr["conv_ffn"] = jnp.concatenate([dwg, dwv], axis=0)
    w_up = p["w_up"]
    du2 = _mm_blocks_red(dhg, w_up[:4], f"d_u2g_{tag}", _NT)
    du2 = _mm_blocks_red(dhv, w_up[4:], f"d_u2v_{tag}", _NT, res=du2)
    gr["w_up"] = jnp.concatenate([_mm_tn_blocks(s["u2"], dhg, f"dw_upg_{tag}", False, True),
                                  _mm_tn_blocks(s["u2"], dhv, f"dw_upv_{tag}", False, True)], axis=0)
    dh1, dh1_b, gr["norm_ffn"] = _rmsnorm_bwd(s["h1"], du2, dh, p["norm_ffn"], f"norm_ffn_bwd_{tag}")
    dy = _mm(dh1_b, p["w_out"], F32, f"d_y_{tag}", dims=_NT)
    gr["w_out"] = _mm_tn(s["y"], dh1_b, f"dw_out_{tag}")
    do, dz, dgate, dpm, gr["head_norm"], gr["pool_scale"] = _post_bwd(
        dy, s["o"], s["z"], s["gate"], s["pm"], p["head_norm"], p["pool_scale"], f"post_bwd_{tag}")
    dpooled = _mm_cols(dpm, p["w_pool"], F32, f"d_pooled_{tag}", _NT)
    gr["w_pool"] = _mm_tn_cols(s["pooled"], dpm, 4, f"dw_pool_{tag}")
    dpool_in = _pool_bwd(dpooled, f"pool_bwd_{tag}")
    (dqkv, dbeta, dg), landed = _gdn_bwd(s["qkv"], s["beta"], s["g"], do, s["s_in"], s["vnew"], s["tinv"], s["wsv"],
                                         f"gdn_bwd_{tag}", scatter)
    dba, gr["a_log"], gr["dt_bias"] = _gates_bwd(s["ba"], p["a_row"], p["dt_row"], dbeta, dg, f"gates_bwd_{tag}")
    dqkv_pre, gr["conv_qkv"] = _gdn_pre_bwd(s["qkv_pre"], p["conv_qkv"], dqkv, f"gdn_pre_bwd_{tag}")
    du = None
    dws = []
    for nm, dseg, wseg in (("qkv", dqkv_pre, p["w_qkv"]), ("z", dz, p["w_z"]), ("ba", dba, p["w_ba"]),
                           ("pool", dpool_in, p["w_pl"]), ("gate", dgate, p["w_gate"])):
        du = _mm(dseg, wseg, F32, f"d_u_{nm}_{tag}", res=du, dims=_NT)
        dws.append(_mm_tn(s["u"], dseg, f"dw_{nm}_{tag}"))
    gr["w_in"] = jnp.concatenate([dws[0], dws[1], dws[2][:, 0:H], dws[2][:, LANE:LANE + H], dws[3], dws[4]], axis=1)
    dh0, dh0_b, gr["norm_mix"] = _rmsnorm_bwd(s["h"], du, dh1, p["norm_mix"], f"norm_mix_bwd_{tag}")
    return dh0, dh0_b, gr, landed


def _pad_lanes(v8):
    return jnp.pad(v8.reshape(1, H), ((0, 0), (0, LANE - H)))


def _pack(parts, rows):
    flat = jnp.concatenate([q.reshape(-1) for q in parts])
    return jnp.pad(flat, (0, rows * LANE - flat.shape[0])).reshape(rows, LANE)


def _unpack(packed, shapes):
    flat = packed.reshape(-1)
    out, off = [], 0
    for shp in shapes:
        n = 1
        for s_ in shp:
            n *= s_
        out.append(flat[off:off + n].reshape(shp))
        off += n
    return out


SMALL_ROWS = 336
REPL_ROWS = 64


def kernel(x, meta_tokens, norm_mix, w_in, conv_qkv, a_log, dt_bias, head_norm, w_pool, pool_scale, w_out, norm_ffn, w_up, conv_ffn, w_down, norm_final, loss_target, m_meta_tokens, m_norm_mix, m_w_in, m_conv_qkv, m_a_log, m_dt_bias, m_head_norm, m_w_pool, m_pool_scale, m_w_out, m_norm_ffn, m_w_up, m_conv_ffn, m_w_down, m_norm_final, v_meta_tokens, v_norm_mix, v_w_in, v_conv_qkv, v_a_log, v_dt_bias, v_head_norm, v_w_pool, v_pool_scale, v_w_out, v_norm_ffn, v_w_up, v_conv_ffn, v_w_down, v_norm_final):
    seq = x.shape[1]
    t = ROW0 + seq
    assert t % TE == 0 and t % (MM_TILES * 16) == 0 and t % CH == 0
    depth = w_in.shape[0]
    assert depth == 2
    cin = w_in.shape[2]

    def layer_params(l, g_in, g_out, g_up, g_down, conv_q, conv_f, wp):
        wf = jnp.transpose(g_in, (1, 0, 2)).reshape(D, NDEV * cin)
        zpad = jnp.zeros((D, LANE - H), BF16)
        return dict(
            w_qkv=wf[:, 0:QKV], w_z=wf[:, QKV:QKV + D],
            w_ba=jnp.concatenate([wf[:, 4096:4104], zpad, wf[:, 4104:4112], zpad], axis=1),
            w_pl=wf[:, 4112:4624], w_gate=wf[:, 4624:6672], conv_qkv=conv_q, conv_ffn=conv_f, w_pool=wp,
            w_out=g_out.reshape(D, D), w_up=g_up, w_down=g_down.reshape(4, FB, D),
            norm_mix=norm_mix[l].reshape(1, D), norm_ffn=norm_ffn[l].reshape(1, D),
            pool_scale=pool_scale[l].reshape(1, D), head_norm=head_norm[l].reshape(1, DH),
            a_row=_pad_lanes(a_log[l]), dt_row=_pad_lanes(dt_bias[l]))

    small_shapes = [conv_qkv.shape, conv_ffn.shape, w_pool.shape, meta_tokens.shape]
    small = _pack([conv_qkv, conv_ffn, w_pool, meta_tokens], SMALL_ROWS)
    w_in_b, w_up_b, w_out_b, w_down_b = w_in.astype(BF16), w_up.astype(BF16), w_out.astype(BF16), w_down.astype(BF16)
    g_in0, g_up0, g_out0, g_down0, g_small = _gather([w_in_b[0], w_up_b[0], w_out_b[0], w_down_b[0], small], "gather_layer0")
    smalls = [_unpack(g_small[i], small_shapes) for i in range(NDEV)]
    conv_qkv_full = jnp.concatenate([sm[0] for sm in smalls], axis=2)
    conv_ffn_blk = jnp.stack([sm[1] for sm in smalls], axis=1)
    w_pool_full = jnp.concatenate([sm[2] for sm in smalls], axis=3).astype(BF16)
    meta_full = jnp.concatenate([sm[3] for sm in smalls], axis=1)
    layers = [layer_params(0, g_in0, g_out0, g_up0, g_down0, conv_qkv_full[0], conv_ffn_blk[0], w_pool_full[0])]

    h = jnp.concatenate([jnp.zeros((LEAD, D), F32), meta_full, x[0]], axis=0)
    h, sv0, (g_in1, g_up1, g_out1, g_down1) = _layer_fwd(h, layers[0], "l0", (w_in_b[1], w_up_b[1], w_out_b[1], w_down_b[1]))
    layers.append(layer_params(1, g_in1, g_out1, g_up1, g_down1, conv_qkv_full[1], conv_ffn_blk[1], w_pool_full[1]))
    h, sv1, _ = _layer_fwd(h, layers[1], "l1")
    saved = [sv0, sv1]
    target = jnp.concatenate([jnp.zeros((ROW0, D), F32), loss_target[0]], axis=0)
    dh, dh_b, d_norm_final, loss_row = _loss_bwd(h, target, norm_final.reshape(1, D), "loss")

    rd = w_down.shape[1]

    def blocks(gr):
        return (jnp.transpose(gr["w_in"].reshape(D, NDEV, cin), (1, 0, 2)).astype(BF16), gr["w_up"].astype(BF16),
                gr["w_out"].reshape(NDEV, D // NDEV, D).astype(BF16), gr["w_down"].reshape(NDEV, rd, D).astype(BF16))

    grads = [None] * depth
    dh, dh_b, grads[1], _ = _layer_bwd(dh, dh_b, layers[1], saved[1], "l1")
    dh, dh_b, grads[0], land1 = _layer_bwd(dh, dh_b, layers[0], saved[0], "l0", blocks(grads[1]))
    grad_x = dh[ROW0:].reshape(1, seq, D)
    d_meta = dh[LEAD:ROW0]

    stk = lambda name: jnp.stack([grads[l][name] for l in range(depth)], axis=0)
    cq = conv_qkv.shape[2]
    pw = w_pool.shape[3]
    s_cq = jnp.transpose(stk("conv_qkv").reshape(depth, 4, NDEV, cq), (2, 0, 1, 3))
    s_cf = jnp.transpose(stk("conv_ffn"), (1, 0, 2, 3))
    s_wp = jnp.transpose(stk("w_pool").reshape(depth, 4, DH, NDEV, pw), (3, 0, 1, 2, 4))
    s_mt = jnp.transpose(d_meta.reshape(N_META, NDEV, D // NDEV), (1, 0, 2))
    b_small = jnp.stack([_pack([s_cq[i], s_cf[i], s_wp[i], s_mt[i]], SMALL_ROWS) for i in range(NDEV)], axis=0)
    *land0, l_small = _scatter([*blocks(grads[0]), b_small], "exchange_layer0")

    def upd(i, w, m, v, name):
        res = [_adamw(land[i], w[l], m[l], v[l], f"adamw_{name}_l{l}") for l, land in enumerate((land0, land1))]
        return [jnp.stack([res[0][kind], res[1][kind]], axis=0) for kind in range(4)]

    r_in = upd(0, w_in, m_w_in, v_w_in, "w_in")
    r_up = upd(1, w_up, m_w_up, v_w_up, "w_up")
    r_out = upd(2, w_out, m_w_out, v_w_out, "w_out")
    r_down = upd(3, w_down, m_w_down, v_w_down, "w_down")
    r_small = _adamw(l_small, small, _pack([m_conv_qkv, m_conv_ffn, m_w_pool, m_meta_tokens], SMALL_ROWS),
                     _pack([v_conv_qkv, v_conv_ffn, v_w_pool, v_meta_tokens], SMALL_ROWS), "adamw_small")
    r_small = [_unpack(o_, small_shapes) for o_ in r_small]

    repl_shapes = [norm_mix.shape, a_log.shape, dt_bias.shape, head_norm.shape, pool_scale.shape, norm_ffn.shape,
                   norm_final.shape, (1,)]
    rp = lambda name, n: jnp.stack([grads[l][name][0, :n] for l in range(depth)], axis=0)
    part = _pack([rp("norm_mix", D), rp("a_log", H), rp("dt_bias", H), rp("head_norm", DH), rp("pool_scale", D),
                  rp("norm_ffn", D), d_norm_final[0], loss_row[0, 0:1]], REPL_ROWS)
    (l_repl,) = _gather([part], "gather_replicated")
    zero1 = jnp.zeros((1,), F32)
    r_repl = _adamw(l_repl, _pack([norm_mix, a_log, dt_bias, head_norm, pool_scale, norm_ffn, norm_final, zero1], REPL_ROWS),
                    _pack([m_norm_mix, m_a_log, m_dt_bias, m_head_norm, m_pool_scale, m_norm_ffn, m_norm_final, zero1], REPL_ROWS),
                    _pack([v_norm_mix, v_a_log, v_dt_bias, v_head_norm, v_pool_scale, v_norm_ffn, v_norm_final, zero1], REPL_ROWS),
                    "adamw_replicated")
    r_repl = [_unpack(o_, repl_shapes) for o_ in r_repl]
    loss = r_repl[0][7].reshape(())

    def leaf(kind):
        sm, rr = r_small[kind], r_repl[kind]
        return [sm[3], rr[0], r_in[kind], sm[0], rr[1], rr[2], rr[3], sm[2], rr[4], r_out[kind], rr[5], r_up[kind],
                sm[1], r_down[kind], rr[6]]

    return (loss, grad_x, *leaf(0), *leaf(1), *leaf(2), *leaf(3))
```

```python
import jax
import jax.numpy as jnp
from jax import lax
from jax.experimental import pallas as pl
from jax.experimental.pallas import tpu as pltpu

F32 = jnp.float32
BF16 = jnp.bfloat16
HI = lax.Precision.HIGHEST
MESH = pl.DeviceIdType.MESH

D = 1024
H = 8
DH = 128
CH = 64
N_META = 16
LEAD = 48
ROW0 = LEAD + N_META
QKV = 3 * D
POOL_W = 512
POOL_WINDOWS = (2, 4, 8, 16)
FB = 704
NDEV = 8
EPS = 1e-6
MM_TILES = 12
TE = 192
LANE = 128
SUB = 8
VMEM_CAP = 56 << 20

ADAM_LR, ADAM_B1, ADAM_B2, ADAM_EPS, ADAM_WD, ADAM_STEP = 0.001, 0.9, 0.999, 1e-08, 0.01, 10

_NN = (((1,), (0,)), ((), ()))
_NT = (((1,), (1,)), ((), ()))
_TN = (((0,), (0,)), ((), ()))


def _dot(a, b, dims=_NN, precision=None):
    return lax.dot_general(a, b, dims, precision=precision, preferred_element_type=F32)


def _bdot(a, b, dims=_NN):
    return _dot(a.astype(BF16), b.astype(BF16), dims)


def _nbytes(shape, dtype):
    n = 1
    for s in shape:
        n *= s
    return n * jnp.dtype(dtype).itemsize


def _params(sem, block_bytes):
    limit = min(VMEM_CAP, 2 * block_bytes + (20 << 20))
    return pltpu.CompilerParams(dimension_semantics=sem, vmem_limit_bytes=limit)


def _sigmoid(x):
    return 1.0 / (1.0 + jnp.exp(-x))


def _col_tile(n):
    for t in (1024, 512, 256, 128):
        if n % t == 0:
            return t
    return n


def _matmul(a, b, *, dims, grid, a_spec, b_spec, o_spec, out_shape, name, red_axis=None, res=None):
    def body(*refs):
        if res is None:
            a_ref, b_ref, o_ref = refs
        else:
            a_ref, b_ref, r_ref, o_ref = refs
        part = _dot(a_ref[...], b_ref[...], dims)
        if red_axis is None:
            if res is not None:
                part = part + r_ref[...]
            o_ref[...] = part.astype(o_ref.dtype)
        else:
            r = pl.program_id(red_axis)

            @pl.when(r == 0)
            def _():
                o_ref[...] = part + r_ref[...] if res is not None else part

            @pl.when(r > 0)
            def _():
                o_ref[...] += part

    def blk(spec, arr):
        return _nbytes([s for s in spec.block_shape if s is not None], arr.dtype)

    ins = [a, b] + ([res] if res is not None else [])
    specs = [a_spec, b_spec] + ([o_spec] if res is not None else [])
    nb = blk(a_spec, a) + blk(b_spec, b) + 2 * _nbytes([s for s in o_spec.block_shape if s is not None], F32)
    sem = tuple("arbitrary" if i == red_axis else "parallel" for i in range(len(grid)))
    return pl.pallas_call(
        body, name=name, grid=grid, in_specs=specs, out_specs=o_spec, out_shape=out_shape,
        compiler_params=_params(sem, nb),
    )(*ins)


def _mm(a, b, out_dtype, name, res=None, dims=_NN):
    m, k = a.shape
    n = b.shape[1] if dims == _NN else b.shape[0]
    tm, tn = m // MM_TILES, _col_tile(n)
    if dims == _NN:
        b_spec = pl.BlockSpec((k, tn), lambda j, i: (0, j))
    else:
        b_spec = pl.BlockSpec((tn, k), lambda j, i: (j, 0))
    return _matmul(
        a, b, dims=dims, grid=(n // tn, MM_TILES), a_spec=pl.BlockSpec((tm, k), lambda j, i: (i, 0)), b_spec=b_spec,
        o_spec=pl.BlockSpec((tm, tn), lambda j, i: (i, j)), out_shape=jax.ShapeDtypeStruct((m, n), out_dtype),
        name=name, res=res)


def _mm_tn(a, g, name):
    m, k = a.shape
    n = g.shape[1]
    tm, tn = m // MM_TILES, _col_tile(n)
    return _matmul(
        a, g, dims=_TN, grid=(n // tn, MM_TILES), red_axis=1, a_spec=pl.BlockSpec((tm, k), lambda j, i: (i, 0)),
        b_spec=pl.BlockSpec((tm, tn), lambda j, i: (i, j)), o_spec=pl.BlockSpec((k, tn), lambda j, i: (0, j)),
        out_shape=jax.ShapeDtypeStruct((k, n), F32), name=name)


def _mm_up(u, w_up, name):
    t = u.shape[0]
    g = w_up.shape[0]
    tm = t // MM_TILES
    return _matmul(
        u, w_up, dims=_NN, grid=(g, MM_TILES), a_spec=pl.BlockSpec((tm, D), lambda g_, i: (i, 0)),
        b_spec=pl.BlockSpec((None, D, FB), lambda g_, i: (g_, 0, 0)),
        o_spec=pl.BlockSpec((None, tm, FB), lambda g_, i: (g_, i, 0)),
        out_shape=jax.ShapeDtypeStruct((g, t, FB), BF16), name=name)


def _mm_blocks_red(a, b, name, dims, res=None):
    g, t, k = a.shape
    n = b.shape[2] if dims == _NN else b.shape[1]
    tm = t // MM_TILES
    return _matmul(
        a, b, dims=dims, grid=(MM_TILES, g), red_axis=1, a_spec=pl.BlockSpec((None, tm, k), lambda i, g_: (g_, i, 0)),
        b_spec=pl.BlockSpec((None,) + b.shape[1:], lambda i, g_: (g_, 0, 0)),
        o_spec=pl.BlockSpec((tm, n), lambda i, g_: (i, 0)), out_shape=jax.ShapeDtypeStruct((t, n), F32),
        name=name, res=res)


def _mm_to_blocks(a, b, name):
    t, k = a.shape
    g, n, _ = b.shape
    tm = t // MM_TILES
    return _matmul(
        a, b, dims=_NT, grid=(g, MM_TILES), a_spec=pl.BlockSpec((tm, k), lambda g_, i: (i, 0)),
        b_spec=pl.BlockSpec((None, n, k), lambda g_, i: (g_, 0, 0)),
        o_spec=pl.BlockSpec((None, tm, n), lambda g_, i: (g_, i, 0)),
        out_shape=jax.ShapeDtypeStruct((g, t, n), BF16), name=name)


def _mm_tn_blocks(a, g, name, a_blocked, g_blocked):
    nb = a.shape[0] if a_blocked else g.shape[0]
    t = a.shape[-2]
    k, n = a.shape[-1], g.shape[-1]
    tm = t // MM_TILES
    a_spec = (pl.BlockSpec((None, tm, k), lambda g_, i: (g_, i, 0)) if a_blocked
              else pl.BlockSpec((tm, k), lambda g_, i: (i, 0)))
    g_spec = (pl.BlockSpec((None, tm, n), lambda g_, i: (g_, i, 0)) if g_blocked
              else pl.BlockSpec((tm, n), lambda g_, i: (i, 0)))
    return _matmul(
        a, g, dims=_TN, grid=(nb, MM_TILES), red_axis=1, a_spec=a_spec, b_spec=g_spec,
        o_spec=pl.BlockSpec((None, k, n), lambda g_, i: (g_, 0, 0)),
        out_shape=jax.ShapeDtypeStruct((nb, k, n), F32), name=name)


def _mm_cols(a, b, out_dtype, name, dims):
    t = a.shape[0]
    g = b.shape[0]
    ka = a.shape[1] // g
    n = b.shape[2] if dims == _NN else b.shape[1]
    tm = t // MM_TILES
    return _matmul(
        a, b, dims=dims, grid=(g, MM_TILES), a_spec=pl.BlockSpec((tm, ka), lambda g_, i: (i, g_)),
        b_spec=pl.BlockSpec((None,) + b.shape[1:], lambda g_, i: (g_, 0, 0)),
        o_spec=pl.BlockSpec((tm, n), lambda g_, i: (i, g_)), out_shape=jax.ShapeDtypeStruct((t, g * n), out_dtype),
        name=name)


def _mm_tn_cols(a, g, nblk, name):
    t = a.shape[0]
    ka, n = a.shape[1] // nblk, g.shape[1] // nblk
    tm = t // MM_TILES
    return _matmul(
        a, g, dims=_TN, grid=(nblk, MM_TILES), red_axis=1, a_spec=pl.BlockSpec((tm, ka), lambda g_, i: (i, g_)),
        b_spec=pl.BlockSpec((tm, n), lambda g_, i: (i, g_)), o_spec=pl.BlockSpec((None, ka, n), lambda g_, i: (g_, 0, 0)),
        out_shape=jax.ShapeDtypeStruct((nblk, ka, n), F32), name=name)


def _rows(cols, n=None):
    if n is None:
        return pl.BlockSpec((TE, cols), lambda i: (i, 0))
    return pl.BlockSpec((TE, cols), lambda i: (n - 1 - i, 0))


def _whole(shape):
    return pl.BlockSpec(shape, lambda *_: (0,) * len(shape))


def _row_ids(i, rows=TE):
    return i * rows + lax.broadcasted_iota(jnp.int32, (rows, 1), 0)


def _rmsnorm_fwd(h, gain, name):
    t = h.shape[0]

    def body(h_ref, g_ref, u_ref):
        x = h_ref[...]
        r = lax.rsqrt(jnp.mean(x * x, axis=-1, keepdims=True) + EPS)
        u_ref[...] = (x * r * g_ref[...]).astype(BF16)

    return pl.pallas_call(
        body, name=name, grid=(t // TE,), in_specs=[_rows(D), _whole((1, D))], out_specs=_rows(D),
        out_shape=jax.ShapeDtypeStruct((t, D), BF16), compiler_params=_params(("parallel",), 3 * TE * D * 4),
    )(h, gain)


def _rmsnorm_bwd(x, du, dres, gain, name):
    t = x.shape[0]

    def body(x_ref, du_ref, dr_ref, g_ref, dx_ref, dxb_ref, dg_ref):
        i = pl.program_id(0)
        xv = x_ref[...]
        r = lax.rsqrt(jnp.mean(xv * xv, axis=-1, keepdims=True) + EPS)
        gdy = du_ref[...] * g_ref[...]
        dx = dr_ref[...] + r * gdy - xv * (r * r * r) * jnp.mean(xv * gdy, axis=-1, keepdims=True)
        dx = jnp.where(_row_ids(i) >= LEAD, dx, 0.0)
        dx_ref[...] = dx
        dxb_ref[...] = dx.astype(BF16)
        part = jnp.sum(du_ref[...] * xv * r, axis=0, keepdims=True)

        @pl.when(i == 0)
        def _():
            dg_ref[...] = part

        @pl.when(i > 0)
        def _():
            dg_ref[...] += part

    return pl.pallas_call(
        body, name=name, grid=(t // TE,), in_specs=[_rows(D), _rows(D), _rows(D), _whole((1, D))],
        out_specs=[_rows(D), _rows(D), _whole((1, D))],
        out_shape=[jax.ShapeDtypeStruct((t, D), F32), jax.ShapeDtypeStruct((t, D), BF16),
                   jax.ShapeDtypeStruct((1, D), F32)],
        compiler_params=_params(("arbitrary",), 5 * TE * D * 4),
    )(x, du, dres, gain)


def _loss_bwd(h, target, gain, name):
    t = h.shape[0]

    def body(h_ref, t_ref, g_ref, dx_ref, dxb_ref, dg_ref, loss_ref):
        i = pl.program_id(0)
        xv = h_ref[...]
        gain_v = g_ref[...]
        r = lax.rsqrt(jnp.mean(xv * xv, axis=-1, keepdims=True) + EPS)
        real = _row_ids(i) >= ROW0
        err = jnp.where(real, xv * r * gain_v - t_ref[...], 0.0)
        dy = err * (1.0 / D)
        gdy = dy * gain_v
        dx = r * gdy - xv * (r * r * r) * jnp.mean(xv * gdy, axis=-1, keepdims=True)
        dx_ref[...] = dx
        dxb_ref[...] = dx.astype(BF16)
        dgp = jnp.sum(dy * xv * r, axis=0, keepdims=True)
        lp = 0.5 * jnp.sum(jnp.mean(err * err, axis=-1, keepdims=True), axis=0, keepdims=True)

        @pl.when(i == 0)
        def _():
            dg_ref[...] = dgp
            loss_ref[...] = jnp.broadcast_to(lp, (1, LANE))

        @pl.when(i > 0)
        def _():
            dg_ref[...] += dgp
            loss_ref[...] += jnp.broadcast_to(lp, (1, LANE))

    return pl.pallas_call(
        body, name=name, grid=(t // TE,), in_specs=[_rows(D), _rows(D), _whole((1, D))],
        out_specs=[_rows(D), _rows(D), _whole((1, D)), _whole((1, LANE))],
        out_shape=[jax.ShapeDtypeStruct((t, D), F32), jax.ShapeDtypeStruct((t, D), BF16),
                   jax.ShapeDtypeStruct((1, D), F32), jax.ShapeDtypeStruct((1, LANE), F32)],
        compiler_params=_params(("arbitrary",), 4 * TE * D * 4),
    )(h, target, gain)


def _seq_scratch(cols):
    return pltpu.VMEM((-(-cols // LANE), TE + SUB, LANE), F32)


def _taps(scr, c, wd, first, n, k):
    return [scr[c, first - (k - 1) + j:first - (k - 1) + j + n, 0:wd] for j in range(k)]


def _stage_history(scr, i):
    @pl.when(i == 0)
    def _():
        scr[...] = jnp.zeros(scr.shape, F32)

    @pl.when(i > 0)
    def _():
        scr[:, 0:SUB, :] = scr[:, TE:TE + SUB, :]


def _stage_future(scr, i):
    @pl.when(i == 0)
    def _():
        scr[...] = jnp.zeros(scr.shape, F32)

    @pl.when(i > 0)
    def _():
        scr[:, TE:TE + SUB, :] = scr[:, 0:SUB, :]


def _conv(tp, w):
    out = w[0:1] * tp[0]
    for j in range(1, len(tp)):
        out = out + w[j:j + 1] * tp[j]
    return out


def _conv_t(ds, c, wd, w):
    k = w.shape[0]
    out = w[k - 1:k] * ds[c, 0:TE, 0:wd]
    for j in range(k - 1):
        out = out + w[j:j + 1] * ds[c, k - 1 - j:k - 1 - j + TE, 0:wd]
    return out


def _gdn_pre_fwd(x, w, name):
    t = x.shape[0]

    def body(x_ref, w_ref, o_ref, xs):
        _stage_history(xs, pl.program_id(0))
        for hh in range(3 * H):
            sl = slice(hh * DH, (hh + 1) * DH)
            xs[hh, SUB:SUB + TE, :] = x_ref[:, sl].astype(F32)
            cv = _conv(_taps(xs, hh, DH, SUB, TE, 4), w_ref[:, sl])
            s = cv * _sigmoid(cv)
            if hh < 2 * H:
                s = s * lax.rsqrt(jnp.sum(s * s, axis=-1, keepdims=True) + EPS)
                if hh < H:
                    s = s * (DH ** -0.5)
            o_ref[:, sl] = s

    return pl.pallas_call(
        body, name=name, grid=(t // TE,), in_specs=[_rows(QKV), _whole((4, QKV))], out_specs=_rows(QKV),
        out_shape=jax.ShapeDtypeStruct((t, QKV), F32), scratch_shapes=[_seq_scratch(QKV)],
        compiler_params=_params(("arbitrary",), TE * QKV * 8),
    )(x, w)


def _gdn_pre_bwd(x, w, dqkv, name):
    t = x.shape[0]
    n = t // TE
    hb = TE // 16

    def body(x_ref, xp_ref, w_ref, d_ref, dx_ref, dw_ref, xs, ds):
        i = pl.program_id(0)

        @pl.when(i == 0)
        def _():
            dw_ref[...] = jnp.zeros_like(dw_ref)

        _stage_future(ds, i)
        for hh in range(3 * H):
            sl = slice(hh * DH, (hh + 1) * DH)
            xs[hh, 0:SUB, :] = jnp.where(i == n - 1, 0.0, xp_ref[SUB:2 * SUB, sl].astype(F32))
            xs[hh, SUB:SUB + TE, :] = x_ref[:, sl].astype(F32)
            wv = w_ref[:, sl]
            tp = _taps(xs, hh, DH, SUB, TE, 4)
            cv = _conv(tp, wv)
            sg = _sigmoid(cv)
            s = cv * sg
            dsv = d_ref[:, sl]
            if hh < 2 * H:
                if hh < H:
                    dsv = dsv * (DH ** -0.5)
                r = lax.rsqrt(jnp.sum(s * s, axis=-1, keepdims=True) + EPS)
                dsv = r * dsv - s * (r * r * r) * jnp.sum(s * dsv, axis=-1, keepdims=True)
            dcv = dsv * (sg * (1.0 + cv * (1.0 - sg)))
            ds[hh, 0:TE, :] = dcv
            dx_ref[:, sl] = _conv_t(ds, hh, DH, wv).astype(BF16)
            dw_ref[:, sl] += jnp.concatenate([jnp.sum(tp[j] * dcv, axis=0, keepdims=True) for j in range(4)], axis=0)

    return pl.pallas_call(
        body, name=name, grid=(n,),
        in_specs=[_rows(QKV, n), pl.BlockSpec((16, QKV), lambda i: (jnp.maximum((n - 1 - i) * hb - 1, 0), 0)),
                  _whole((4, QKV)), _rows(QKV, n)],
        out_specs=[_rows(QKV, n), _whole((4, QKV))],
        out_shape=[jax.ShapeDtypeStruct((t, QKV), BF16), jax.ShapeDtypeStruct((4, QKV), F32)],
        scratch_shapes=[_seq_scratch(QKV), _seq_scratch(QKV)],
        compiler_params=_params(("arbitrary",), TE * QKV * 14),
    )(x, x, w, dqkv)


def _softplus(x):
    return jnp.maximum(x, 0.0) + jnp.log(1.0 + jnp.exp(-jnp.abs(x)))


def _gates_fwd(ba, a_row, dt_row, name):
    t = ba.shape[0]

    def body(ba_ref, a_ref, dt_ref, b_out, g_out):
        real = _row_ids(pl.program_id(0)) >= LEAD
        b_out[...] = jnp.where(real, _sigmoid(ba_ref[:, 0:LANE]), 0.0)
        g = -jnp.exp(a_ref[...]) * _softplus(ba_ref[:, LANE:2 * LANE] + dt_ref[...])
        g_out[...] = jnp.where(real, g, 0.0)

    return pl.pallas_call(
        body, name=name, grid=(t // TE,), in_specs=[_rows(2 * LANE), _whole((1, LANE)), _whole((1, LANE))],
        out_specs=[_rows(LANE), _rows(LANE)],
        out_shape=[jax.ShapeDtypeStruct((t, LANE), F32), jax.ShapeDtypeStruct((t, LANE), F32)],
        compiler_params=_params(("parallel",), TE * LANE * 16),
    )(ba, a_row, dt_row)


def _gates_bwd(ba, a_row, dt_row, dbeta, dg, name):
    t = ba.shape[0]

    def body(ba_ref, a_ref, dt_ref, db_ref, dg_ref, dba_ref, da_out, ddt_out):
        i = pl.program_id(0)
        real = _row_ids(i) >= LEAD
        beta = _sigmoid(ba_ref[:, 0:LANE])
        draw_b = jnp.where(real, db_ref[...] * beta * (1.0 - beta), 0.0)
        pre = ba_ref[:, LANE:2 * LANE] + dt_ref[...]
        neg_a = -jnp.exp(a_ref[...])
        dgv = jnp.where(real, dg_ref[...], 0.0)
        draw_a = dgv * neg_a * _sigmoid(pre)
        dba_ref[:, 0:LANE] = draw_b.astype(BF16)
        dba_ref[:, LANE:2 * LANE] = draw_a.astype(BF16)
        dal = jnp.sum(dgv * neg_a * _softplus(pre), axis=0, keepdims=True)
        ddt = jnp.sum(draw_a, axis=0, keepdims=True)

        @pl.when(i == 0)
        def _():
            da_out[...] = dal
            ddt_out[...] = ddt

        @pl.when(i > 0)
        def _():
            da_out[...] += dal
            ddt_out[...] += ddt

    return pl.pallas_call(
        body, name=name, grid=(t // TE,),
        in_specs=[_rows(2 * LANE), _whole((1, LANE)), _whole((1, LANE)), _rows(LANE), _rows(LANE)],
        out_specs=[_rows(2 * LANE), _whole((1, LANE)), _whole((1, LANE))],
        out_shape=[jax.ShapeDtypeStruct((t, 2 * LANE), BF16), jax.ShapeDtypeStruct((1, LANE), F32),
                   jax.ShapeDtypeStruct((1, LANE), F32)],
        compiler_params=_params(("arbitrary",), TE * LANE * 24),
    )(ba, a_row, dt_row, dbeta, dg)


_OFFSETS = [(dx, dy, dc) for dx in (0, 1) for dy in (0, 1) for dc in (0, 1)][1:]
NPEER = len(_OFFSETS)
ANY_SPEC = pl.BlockSpec(memory_space=pl.ANY)


def _place():
    return lax.axis_index("x"), lax.axis_index("y"), lax.axis_index("c")


def _index(p):
    return 4 * p[0] + 2 * p[1] + p[2]


def _comm_scratch(n):
    return [pltpu.SemaphoreType.DMA((n * NPEER,)), pltpu.SemaphoreType.DMA((n * NPEER,)), pltpu.SemaphoreType.DMA((n,))]


def _scatter_copies(ins, outs, send, recv):
    me = _place()
    mi = _index(me)
    res = []
    for j, d in enumerate(_OFFSETS):
        peer = tuple(1 - v if bit else v for v, bit in zip(me, d))
        pi = _index(peer)
        for k in range(len(ins)):
            sem = k * NPEER + j
            mine = pltpu.make_async_remote_copy(src_ref=ins[k].at[pi], dst_ref=outs[k].at[mi], send_sem=send.at[sem],
                                                recv_sem=recv.at[sem], device_id=peer, device_id_type=MESH)
            theirs = pltpu.make_async_remote_copy(src_ref=ins[k].at[pi], dst_ref=outs[k].at[pi], send_sem=send.at[sem],
                                                  recv_sem=recv.at[sem], device_id=peer, device_id_type=MESH)
            res.append((mine, theirs))
    return res


def _scatter_own(ins, outs, loc):
    mi = _index(_place())
    return [pltpu.make_async_copy(ins[k].at[mi], outs[k].at[mi], loc.at[k]) for k in range(len(ins))]


def _scatter_start(ins, outs, send, recv, loc):
    for cp in _scatter_own(ins, outs, loc):
        cp.start()
    for mine, _ in _scatter_copies(ins, outs, send, recv):
        mine.start()


def _scatter_wait(ins, outs, send, recv, loc):
    cps = _scatter_copies(ins, outs, send, recv)
    for _, theirs in cps:
        theirs.wait_recv()
    for mine, _ in cps:
        mine.wait_send()
    for cp in _scatter_own(ins, outs, loc):
        cp.wait()


def _gather_parts(ins, outs, send, recv):
    x, y, c = _place()
    chips = [(1 - x, y), (x, 1 - y), (1 - x, 1 - y)]

    def cp(k, slot, src, block, to):
        return pltpu.make_async_remote_copy(src_ref=src, dst_ref=outs[k].at[_index(block)], send_sem=send.at[k * NPEER + slot],
                                            recv_sem=recv.at[k * NPEER + slot], device_id=to, device_id_type=MESH)

    return (x, y, c), (x, y, 1 - c), chips, cp


def _gather_start(ins, outs, send, recv, loc):
    me, sib, chips, cp = _gather_parts(ins, outs, send, recv)
    for k in range(len(ins)):
        pltpu.make_async_copy(ins[k], outs[k].at[_index(me)], loc.at[k]).start()
        cp(k, 0, ins[k], me, sib).start()
        for j, chip in enumerate(chips):
            cp(k, 1 + j, ins[k], me, (*chip, me[2])).start()


def _gather_forward(ins, outs, send, recv, loc):
    me, sib, chips, cp = _gather_parts(ins, outs, send, recv)
    for j, chip in enumerate(chips):
        blk = (*chip, me[2])
        for k in range(len(ins)):
            cp(k, 1 + j, ins[k], blk, me).wait_recv()
            cp(k, 4 + j, outs[k].at[_index(blk)], blk, sib).start()


def _gather_finish(ins, outs, send, recv, loc):
    me, sib, chips, cp = _gather_parts(ins, outs, send, recv)
    for k in range(len(ins)):
        cp(k, 0, ins[k], sib, me).wait_recv()
        for j, chip in enumerate(chips):
            cp(k, 4 + j, ins[k], (*chip, sib[2]), me).wait_recv()
        cp(k, 0, ins[k], me, sib).wait_send()
        for j, chip in enumerate(chips):
            cp(k, 1 + j, ins[k], me, (*chip, me[2])).wait_send()
            cp(k, 4 + j, outs[k].at[_index((*chip, me[2]))], (*chip, me[2]), sib).wait_send()
        pltpu.make_async_copy(ins[k], outs[k].at[_index(me)], loc.at[k]).wait()


def _gathered_shapes(arrs):
    return [jax.ShapeDtypeStruct((NDEV,) + a.shape, a.dtype) for a in arrs]


def _gather(arrs, name):
    n = len(arrs)

    def body(*refs):
        ins, outs, sems = refs[:n], refs[n:2 * n], refs[2 * n:]
        _gather_start(ins, outs, *sems)
        _gather_forward(ins, outs, *sems)
        _gather_finish(ins, outs, *sems)

    return pl.pallas_call(body, name=name, in_specs=[ANY_SPEC] * n, out_specs=[ANY_SPEC] * n,
                          out_shape=_gathered_shapes(arrs), scratch_shapes=_comm_scratch(n))(*arrs)


def _scatter(arrs, name):
    n = len(arrs)

    def body(*refs):
        ins, outs, sems = refs[:n], refs[n:2 * n], refs[2 * n:]
        _scatter_start(ins, outs, *sems)
        _scatter_wait(ins, outs, *sems)

    return pl.pallas_call(body, name=name, in_specs=[ANY_SPEC] * n, out_specs=[ANY_SPEC] * n,
                          out_shape=[jax.ShapeDtypeStruct(a.shape, a.dtype) for a in arrs],
                          scratch_shapes=_comm_scratch(n))(*arrs)


_BNN = (((2,), (1,)), ((0,), (0,)))
_BNT = (((2,), (2,)), ((0,), (0,)))
_BTN = (((1,), (1,)), ((0,), (0,)))


def _split(a):
    hi = a.astype(BF16)
    return hi, (a - hi.astype(F32)).astype(BF16)


def _dot3(a, b):
    ah, al = _split(a)
    bh, bl = _split(b)
    m = a.shape[1]
    r = _dot(jnp.concatenate([ah, al], axis=1), bh, _BNN)
    return r[:, :m] + r[:, m:] + _dot(ah, bl, _BNN)


def _heads(ref, off):
    return jnp.stack([ref[:, off + h * DH:off + (h + 1) * DH] for h in range(H)])


def _cols(a):
    return jnp.stack([a[:, h:h + 1] for h in range(H)])


def _lanes(a):
    lane = lax.broadcasted_iota(jnp.int32, (CH, LANE), 1)
    out = jnp.zeros((CH, LANE), F32)
    for h in range(H):
        out = jnp.where(lane == h, a[h], out)
    return out


def _chunk_prep(qkv_ref, b_ref, g_ref):
    row = lax.broadcasted_iota(jnp.int32, (CH, CH), 0)
    col = lax.broadcasted_iota(jnp.int32, (CH, CH), 1)
    incl, strict = row >= col, row > col
    gc = _dot(incl.astype(F32), g_ref[...], precision=HI)
    gct = gc.T
    q, k, v = _heads(qkv_ref, 0), _heads(qkv_ref, D), _heads(qkv_ref, 2 * D)
    bcol, gcol = _cols(b_ref[...]), _cols(gc)
    grow = jnp.stack([gct[h:h + 1, :] for h in range(H)])
    glast = _cols(gc[CH - 1:CH, :])
    dec = jnp.exp(jnp.where(incl[None], gcol - grow, -1e30))
    kb = k * bcol
    ab = _bdot(jnp.concatenate([kb, q], axis=1), k, _BNT)
    egc, ekc = jnp.exp(gcol), jnp.exp(glast - gcol)
    return dict(row=row, col=col, strict=strict[None], q=q, k=k, v=v, bcol=bcol, dec=dec, kb=kb,
                lm=jnp.where(strict[None], ab[:, :CH] * dec, 0.0), qk=ab[:, CH:] * dec, egc=egc, ekc=ekc,
                gth=jnp.exp(glast), qd=q * egc, kd=k * ekc, vb=v * bcol, kbg=kb * egc)


def _unit_lower_inverse(lm, eye):
    n = -lm
    x = eye + n
    pw = _dot3(n, n)
    for it in range(5):
        if it < 4:
            xp = _dot3(jnp.concatenate([x, pw], axis=1), pw)
            x = x + xp[:, :CH]
            pw = xp[:, CH:]
        else:
            x = x + _dot3(x, pw)
    return x


def _gdn_fwd(qkv, beta, g, name, gather=()):
    t = qkv.shape[0]
    nc = t // CH
    ng = len(gather)

    def body(qkv_ref, b_ref, g_ref, *rest):
        c_ins, (o_ref, sin_ref, vn_ref, ti_ref, w_ref) = rest[:ng], rest[ng:ng + 5]
        c_outs, state, sems = rest[ng + 5:2 * ng + 5], rest[2 * ng + 5], rest[2 * ng + 6:]
        step = pl.program_id(0)

        @pl.when(step == 0)
        def _():
            state[...] = jnp.zeros_like(state)
            if ng:
                _gather_start(c_ins, c_outs, *sems)

        if ng:
            @pl.when(step == nc // 2)
            def _():
                _gather_forward(c_ins, c_outs, *sems)

            @pl.when(step == nc - 1)
            def _():
                _gather_finish(c_ins, c_outs, *sems)

        pr = _chunk_prep(qkv_ref, b_ref, g_ref)
        tinv = _unit_lower_inverse(pr["lm"], (pr["row"] == pr["col"]).astype(F32)[None])
        uw = _bdot(tinv, jnp.concatenate([pr["vb"], pr["kbg"]], axis=2), _BNN)
        u, w = uw[:, :, :DH], uw[:, :, DH:]
        s = state[...]
        ws = _bdot(jnp.concatenate([w, pr["qd"]], axis=1), s, _BNN)
        vn = u - ws[:, :CH]
        o = ws[:, CH:] + _bdot(pr["qk"], vn, _BNN)
        state[...] = s * pr["gth"] + _bdot(pr["kd"], vn, _BTN)
        sin_ref[0] = s
        ti_ref[0] = tinv
        for h in range(H):
            sl = slice(h * DH, (h + 1) * DH)
            o_ref[:, sl] = o[h]
            vn_ref[:, sl] = vn[h]
            w_ref[:, sl] = w[h]

    chunk = lambda cols: pl.BlockSpec((CH, cols), lambda c: (c, 0))
    outs = pl.pallas_call(
        body, name=name, grid=(nc,), in_specs=[chunk(QKV), chunk(LANE), chunk(LANE)] + [ANY_SPEC] * ng,
        out_specs=[chunk(D), pl.BlockSpec((1, H, DH, DH), lambda c: (c, 0, 0, 0)), chunk(D),
                   pl.BlockSpec((1, H, CH, CH), lambda c: (c, 0, 0, 0)), chunk(D)] + [ANY_SPEC] * ng,
        out_shape=[jax.ShapeDtypeStruct((t, D), F32), jax.ShapeDtypeStruct((nc, H, DH, DH), F32),
                   jax.ShapeDtypeStruct((t, D), F32), jax.ShapeDtypeStruct((nc, H, CH, CH), F32),
                   jax.ShapeDtypeStruct((t, D), F32)] + _gathered_shapes(gather),
        scratch_shapes=[pltpu.VMEM((H, DH, DH), F32)] + (_comm_scratch(ng) if ng else []),
        compiler_params=_params(("arbitrary",), 4 << 20),
    )(qkv, beta, g, *gather)
    return outs[:5], outs[5:]


def _gdn_bwd(qkv, beta, g, do, s_in, vnew, tinv, wsv, name, scatter=()):
    t = qkv.shape[0]
    nc = t // CH
    ns = len(scatter)

    def body(qkv_ref, b_ref, g_ref, do_ref, sin_ref, vn_ref, ti_ref, w_ref, *rest):
        c_ins, (dqkv_ref, db_ref, dg_ref) = rest[:ns], rest[ns:ns + 3]
        c_outs, dstate, sems = rest[ns + 3:2 * ns + 3], rest[2 * ns + 3], rest[2 * ns + 4:]
        step = pl.program_id(0)

        @pl.when(step == 0)
        def _():
            dstate[...] = jnp.zeros_like(dstate)
            if ns:
                _scatter_start(c_ins, c_outs, *sems)

        if ns:
            @pl.when(step == nc - 1)
            def _():
                _scatter_wait(c_ins, c_outs, *sems)

        pr = _chunk_prep(qkv_ref, b_ref, g_ref)
        ti, s = ti_ref[0], sin_ref[0]
        w, vn, doh = _heads(w_ref, 0), _heads(vn_ref, 0), _heads(do_ref, 0)
        ds = dstate[...]
        dvn = _bdot(pr["kd"], ds, _BNN) + _bdot(pr["qk"], doh, _BTN)
        dkd = _bdot(vn, ds, _BNT)
        dqd = _bdot(doh, s, _BNT)
        dqk = _bdot(doh, vn, _BNT)
        dw = -_bdot(dvn, s, _BNT)
        dgt = jnp.sum(jnp.sum(ds * s, axis=2, keepdims=True), axis=1, keepdims=True)
        dstate[...] = ds * pr["gth"] + _bdot(pr["qd"], doh, _BTN) - _bdot(w, dvn, _BTN)
        duw = jnp.concatenate([dvn, dw], axis=2)
        dvk = _bdot(ti, duw, _BTN)
        dvb, dkbg = dvk[:, :, :DH], dvk[:, :, DH:]
        dti = _bdot(duw, jnp.concatenate([pr["vb"], pr["kbg"]], axis=2), _BNT)
        dl = -_dot(_dot(ti, dti, _BTN, precision=HI), ti, _BNT, precision=HI)
        dl = jnp.where(pr["strict"], dl, 0.0)
        dab = jnp.concatenate([dl * pr["dec"], dqk * pr["dec"]], axis=1)
        r1 = _bdot(dab, pr["k"], _BNN)
        dkb = r1[:, :CH] + dkbg * pr["egc"]
        dq = r1[:, CH:] + dqd * pr["egc"]
        dk = _bdot(dab, jnp.concatenate([pr["kb"], pr["q"]], axis=1), _BTN) + dkb * pr["bcol"] + dkd * pr["ekc"]
        m = dl * pr["lm"] + dqk * pr["qk"]
        colsum = _dot(m, jnp.ones((H, CH, LANE), F32), _BTN, precision=HI)[:, :, 0:1]
        kdsum = jnp.sum(dkd * pr["kd"], axis=2, keepdims=True)
        dgc = (jnp.sum(m, axis=2, keepdims=True) - colsum + jnp.sum(dkbg * pr["kbg"], axis=2, keepdims=True)
               + jnp.sum(dqd * pr["qd"], axis=2, keepdims=True) - kdsum)
        dglast = jnp.sum(kdsum, axis=1, keepdims=True) + dgt * pr["gth"]
        last_row = lax.broadcasted_iota(jnp.int32, (1, CH, 1), 1) == CH - 1
        dgc = dgc + jnp.where(last_row, dglast, 0.0)
        dbeta = jnp.sum(dkb * pr["k"], axis=2, keepdims=True) + jnp.sum(dvb * pr["v"], axis=2, keepdims=True)
        dv = dvb * pr["bcol"]
        for h in range(H):
            dqkv_ref[:, h * DH:(h + 1) * DH] = dq[h]
            dqkv_ref[:, D + h * DH:D + (h + 1) * DH] = dk[h]
            dqkv_ref[:, 2 * D + h * DH:2 * D + (h + 1) * DH] = dv[h]
        db_ref[...] = _lanes(dbeta)
        dg_ref[...] = _dot((pr["row"] <= pr["col"]).astype(F32), _lanes(dgc), precision=HI)

    chunk = lambda cols: pl.BlockSpec((CH, cols), lambda c: (nc - 1 - c, 0))
    sq = lambda a, b: pl.BlockSpec((1, H, a, b), lambda c: (nc - 1 - c, 0, 0, 0))
    outs = pl.pallas_call(
        body, name=name, grid=(nc,),
        in_specs=[chunk(QKV), chunk(LANE), chunk(LANE), chunk(D), sq(DH, DH), chunk(D), sq(CH, CH), chunk(D)] + [ANY_SPEC] * ns,
        out_specs=[chunk(QKV), chunk(LANE), chunk(LANE)] + [ANY_SPEC] * ns,
        out_shape=[jax.ShapeDtypeStruct((t, QKV), F32), jax.ShapeDtypeStruct((t, LANE), F32),
                   jax.ShapeDtypeStruct((t, LANE), F32)] + [jax.ShapeDtypeStruct(a.shape, a.dtype) for a in scatter],
        scratch_shapes=[pltpu.VMEM((H, DH, DH), F32)] + (_comm_scratch(ns) if ns else []),
        compiler_params=_params(("arbitrary",), 6 << 20),
    )(qkv, beta, g, do, s_in, vnew, tinv, wsv, *scatter)
    return outs[:3], outs[3:]


def _pool_counts(row_ids, win):
    return jnp.minimum(jnp.maximum(row_ids - LEAD, 0) + 1, win).astype(F32)


def _pool_fwd(p, name):
    t = p.shape[0]
    ext = TE + 16

    def body(p_ref, o_ref, carry):
        i = pl.program_id(0)

        @pl.when(i == 0)
        def _():
            carry[...] = jnp.zeros_like(carry)

        ids = _row_ids(i)
        for gi, win in enumerate(POOL_WINDOWS):
            sl = slice(gi * LANE, (gi + 1) * LANE)
            xv = p_ref[:, sl]
            s = jnp.concatenate([carry[:, sl], xv], axis=0)
            sh = 1
            while sh < win:
                s = s + pltpu.roll(s, sh, 0)
                sh *= 2
            o_ref[:, sl] = (s[16:ext] / _pool_counts(ids, win) - xv).astype(BF16)
            carry[:, sl] = xv[TE - 16:TE]

    return pl.pallas_call(
        body, name=name, grid=(t // TE,), in_specs=[_rows(POOL_W)], out_specs=_rows(POOL_W),
        out_shape=jax.ShapeDtypeStruct((t, POOL_W), BF16), scratch_shapes=[pltpu.VMEM((16, POOL_W), F32)],
        compiler_params=_params(("arbitrary",), TE * POOL_W * 8),
    )(p)


def _pool_bwd(dpo, name):
    t = dpo.shape[0]
    n = t // TE
    ext = TE + 16

    def body(d_ref, o_ref, carry):
        i = pl.program_id(0)

        @pl.when(i == 0)
        def _():
            carry[...] = jnp.zeros_like(carry)

        ids = _row_ids(n - 1 - i)
        for gi, win in enumerate(POOL_WINDOWS):
            sl = slice(gi * LANE, (gi + 1) * LANE)
            dv = d_ref[:, sl]
            rv = dv / _pool_counts(ids, win)
            s = jnp.concatenate([rv, carry[:, sl]], axis=0)
            sh = 1
            while sh < win:
                s = s + pltpu.roll(s, ext - sh, 0)
                sh *= 2
            o_ref[:, sl] = (s[0:TE] - dv).astype(BF16)
            carry[:, sl] = rv[0:16]

    return pl.pallas_call(
        body, name=name, grid=(n,), in_specs=[_rows(POOL_W, n)], out_specs=_rows(POOL_W, n),
        out_shape=jax.ShapeDtypeStruct((t, POOL_W), BF16), scratch_shapes=[pltpu.VMEM((16, POOL_W), F32)],
        compiler_params=_params(("arbitrary",), TE * POOL_W * 8),
    )(dpo)


def _post_fwd(o, z, gate, pm, hn, ps, name):
    t = o.shape[0]

    def body(o_ref, z_ref, g_ref, pm_ref, hn_ref, ps_ref, y_ref):
        for h in range(H):
            sl = slice(h * DH, (h + 1) * DH)
            ov = o_ref[:, sl]
            zv = z_ref[:, sl].astype(F32)
            r = lax.rsqrt(jnp.mean(ov * ov, axis=-1, keepdims=True) + EPS)
            ya = ov * r * hn_ref[...] * (zv * _sigmoid(zv))
            ga = _sigmoid(g_ref[:, sl].astype(F32))
            gb = _sigmoid(g_ref[:, D + h * DH:D + (h + 1) * DH].astype(F32))
            y_ref[:, sl] = (ga * ya + gb * (pm_ref[:, sl] * ps_ref[:, sl])).astype(BF16)

    return pl.pallas_call(
        body, name=name, grid=(t // TE,),
        in_specs=[_rows(D), _rows(D), _rows(2 * D), _rows(D), _whole((1, DH)), _whole((1, D))], out_specs=_rows(D),
        out_shape=jax.ShapeDtypeStruct((t, D), BF16), compiler_params=_params(("parallel",), TE * D * 16),
    )(o, z, gate, pm, hn, ps)


def _post_bwd(dy, o, z, gate, pm, hn, ps, name):
    t = o.shape[0]

    def body(dy_ref, o_ref, z_ref, g_ref, pm_ref, hn_ref, ps_ref, do_ref, dz_ref, dgate_ref, dpm_ref, dhn_ref, dps_ref):
        i = pl.program_id(0)

        @pl.when(i == 0)
        def _():
            dhn_ref[...] = jnp.zeros_like(dhn_ref)
            dps_ref[...] = jnp.zeros_like(dps_ref)

        hnv = hn_ref[...]
        dhn = jnp.zeros((1, DH), F32)
        for h in range(H):
            sl = slice(h * DH, (h + 1) * DH)
            slb = slice(D + h * DH, D + (h + 1) * DH)
            dyv = dy_ref[:, sl]
            ov = o_ref[:, sl]
            zv = z_ref[:, sl].astype(F32)
            r = lax.rsqrt(jnp.mean(ov * ov, axis=-1, keepdims=True) + EPS)
            sz = _sigmoid(zv)
            silu = zv * sz
            on = ov * r
            ya = on * hnv * silu
            ga = _sigmoid(g_ref[:, sl].astype(F32))
            gb = _sigmoid(g_ref[:, slb].astype(F32))
            pmv = pm_ref[:, sl]
            psv = ps_ref[:, sl]
            dya = dyv * ga
            dyb = dyv * gb
            dgate_ref[:, sl] = (dyv * ya * ga * (1.0 - ga)).astype(BF16)
            dgate_ref[:, slb] = (dyv * (pmv * psv) * gb * (1.0 - gb)).astype(BF16)
            tt = dya * hnv * silu
            do_ref[:, sl] = r * tt - ov * (r * r * r) * jnp.mean(ov * tt, axis=-1, keepdims=True)
            dz_ref[:, sl] = (dya * on * hnv * (sz * (1.0 + zv * (1.0 - sz)))).astype(BF16)
            dhn = dhn + jnp.sum(dya * on * silu, axis=0, keepdims=True)
            dps_ref[:, sl] += jnp.sum(dyb * pmv, axis=0, keepdims=True)
            dpm_ref[:, sl] = (dyb * psv).astype(BF16)
        dhn_ref[...] += dhn

    return pl.pallas_call(
        body, name=name, grid=(t // TE,),
        in_specs=[_rows(D), _rows(D), _rows(D), _rows(2 * D), _rows(D), _whole((1, DH)), _whole((1, D))],
        out_specs=[_rows(D), _rows(D), _rows(2 * D), _rows(D), _whole((1, DH)), _whole((1, D))],
        out_shape=[jax.ShapeDtypeStruct((t, D), F32), jax.ShapeDtypeStruct((t, D), BF16),
                   jax.ShapeDtypeStruct((t, 2 * D), BF16), jax.ShapeDtypeStruct((t, D), BF16),
                   jax.ShapeDtypeStruct((1, DH), F32), jax.ShapeDtypeStruct((1, D), F32)],
        compiler_params=_params(("arbitrary",), TE * D * 28),
    )(dy, o, z, gate, pm, hn, ps)


_FB_COLS = [(c, min(c + LANE, FB)) for c in range(0, FB, LANE)]


def _mlp_act_fwd(hid, cw, name):
    t = hid.shape[1]
    n = t // TE

    def body(hg_ref, hv_ref, wg_ref, wv_ref, a_ref, xg, xv):
        i = pl.program_id(1)
        _stage_history(xg, i)
        _stage_history(xv, i)
        for c, (c0, c1) in enumerate(_FB_COLS):
            sl, wd = slice(c0, c1), c1 - c0
            xg[c, SUB:SUB + TE, 0:wd] = hg_ref[:, sl].astype(F32)
            xv[c, SUB:SUB + TE, 0:wd] = hv_ref[:, sl].astype(F32)
            gg = _conv(_taps(xg, c, wd, SUB, TE, 3), wg_ref[:, sl])
            vv = _conv(_taps(xv, c, wd, SUB, TE, 3), wv_ref[:, sl])
            a_ref[:, sl] = (gg * _sigmoid(gg) * vv).astype(BF16)

    hspec = lambda off: pl.BlockSpec((None, TE, FB), lambda p, i: (p + off, i, 0))
    wspec = lambda off: pl.BlockSpec((None, 3, FB), lambda p, i: (p + off, 0, 0))
    return pl.pallas_call(
        body, name=name, grid=(4, n), in_specs=[hspec(0), hspec(4), wspec(0), wspec(4)],
        out_specs=pl.BlockSpec((None, TE, FB), lambda p, i: (p, i, 0)),
        out_shape=jax.ShapeDtypeStruct((4, t, FB), BF16),
        scratch_shapes=[_seq_scratch(FB), _seq_scratch(FB)],
        compiler_params=_params(("parallel", "arbitrary"), TE * FB * 12),
    )(hid, hid, cw, cw)


def _mlp_act_bwd(da, hid, cw, name):
    t = hid.shape[1]
    n = t // TE
    hb = TE // 16

    def body(da_ref, hg_ref, hv_ref, pg_ref, pv_ref, wg_ref, wv_ref, dhg_ref, dhv_ref, dwg_ref, dwv_ref, xg, xv, dg, dv):
        i = pl.program_id(1)

        @pl.when(i == 0)
        def _():
            dwg_ref[...] = jnp.zeros_like(dwg_ref)
            dwv_ref[...] = jnp.zeros_like(dwv_ref)

        _stage_future(dg, i)
        _stage_future(dv, i)
        first_tile = i == n - 1
        for c, (c0, c1) in enumerate(_FB_COLS):
            sl, wd = slice(c0, c1), c1 - c0
            for scr, p_ref, h_ref in ((xg, pg_ref, hg_ref), (xv, pv_ref, hv_ref)):
                scr[c, 0:SUB, 0:wd] = jnp.where(first_tile, 0.0, p_ref[SUB:2 * SUB, sl].astype(F32))
                scr[c, SUB:SUB + TE, 0:wd] = h_ref[:, sl].astype(F32)
            wg = wg_ref[:, sl]
            wv = wv_ref[:, sl]
            tg = _taps(xg, c, wd, SUB, TE, 3)
            tv = _taps(xv, c, wd, SUB, TE, 3)
            gg = _conv(tg, wg)
            vv = _conv(tv, wv)
            sg = _sigmoid(gg)
            dav = da_ref[:, sl].astype(F32)
            dgg = dav * vv * (sg * (1.0 + gg * (1.0 - sg)))
            dvv = dav * (gg * sg)
            for dc, tp, w, scr, dh_ref, dw_ref in ((dgg, tg, wg, dg, dhg_ref, dwg_ref), (dvv, tv, wv, dv, dhv_ref, dwv_ref)):
                scr[c, 0:TE, 0:wd] = dc
                dh_ref[:, sl] = _conv_t(scr, c, wd, w).astype(BF16)
                dw_ref[:, sl] += jnp.concatenate([jnp.sum(tp[j] * dc, axis=0, keepdims=True) for j in range(3)], axis=0)

    rev = lambda off: pl.BlockSpec((None, TE, FB), lambda p, i: (p + off, n - 1 - i, 0))
    halo = lambda off: pl.BlockSpec((None, 16, FB), lambda p, i: (p + off, jnp.maximum((n - 1 - i) * hb - 1, 0), 0))
    wspec = lambda off: pl.BlockSpec((None, 3, FB), lambda p, i: (p + off, 0, 0))
    dwspec = pl.BlockSpec((None, 3, FB), lambda p, i: (p, 0, 0))
    return pl.pallas_call(
        body, name=name, grid=(4, n), in_specs=[rev(0), rev(0), rev(4), halo(0), halo(4), wspec(0), wspec(4)],
        out_specs=[rev(0), rev(0), dwspec, dwspec],
        out_shape=[jax.ShapeDtypeStruct((4, t, FB), BF16), jax.ShapeDtypeStruct((4, t, FB), BF16),
                   jax.ShapeDtypeStruct((4, 3, FB), F32), jax.ShapeDtypeStruct((4, 3, FB), F32)],
        scratch_shapes=[_seq_scratch(FB)] * 4,
        compiler_params=_params(("parallel", "arbitrary"), TE * FB * 24),
    )(da, hid, hid, hid, hid, cw, cw)


def _adamw(land, w, m, v, name):
    r, c = w.shape
    tr = r
    for cand in (128, 64, 32, 16):
        if r % cand == 0:
            tr = cand
            break
    c1 = 1.0 - ADAM_B1 ** ADAM_STEP
    c2 = 1.0 - ADAM_B2 ** ADAM_STEP

    def body(l_ref, w_ref, m_ref, v_ref, g_out, d_out, m_out, v_out):
        g = l_ref[0].astype(F32)
        for i in range(1, NDEV):
            g = g + l_ref[i].astype(F32)
        mn = ADAM_B1 * m_ref[...] + (1.0 - ADAM_B1) * g
        vn = ADAM_B2 * v_ref[...] + (1.0 - ADAM_B2) * (g * g)
        g_out[...] = g
        m_out[...] = mn
        v_out[...] = vn
        d_out[...] = -ADAM_LR * ((mn / c1) / (jnp.sqrt(vn / c2) + ADAM_EPS) + ADAM_WD * w_ref[...])

    spec = pl.BlockSpec((tr, c), lambda i: (i, 0))
    shp = jax.ShapeDtypeStruct((r, c), F32)
    return pl.pallas_call(
        body, name=name, grid=(r // tr,), in_specs=[pl.BlockSpec((NDEV, tr, c), lambda i: (0, i, 0)), spec, spec, spec],
        out_specs=[spec] * 4, out_shape=[shp] * 4, compiler_params=_params(("parallel",), 15 * tr * c * 4),
    )(land, w, m, v)


def _layer_fwd(h, p, tag, gather=(), finish=None):
    u = _rmsnorm_fwd(h, p["norm_mix"], f"norm_mix_{tag}")
    qkv_pre = _mm(u, p["w_qkv"], BF16, f"proj_qkv_{tag}")
    z = _mm(u, p["w_z"], BF16, f"proj_z_{tag}")
    ba = _mm(u, p["w_ba"], F32, f"proj_ba_{tag}")
    pool_in = _mm(u, p["w_pl"], F32, f"proj_pool_{tag}")
    gate = _mm(u, p["w_gate"], BF16, f"proj_gate_{tag}")
    qkv = _gdn_pre_fwd(qkv_pre, p["conv_qkv"], f"gdn_pre_{tag}")
    beta, g = _gates_fwd(ba, p["a_row"], p["dt_row"], f"gates_{tag}")
    (o, s_in, vnew, tinv, wsv), gathered = _gdn_fwd(qkv, beta, g, f"gdn_{tag}", gather)
    if finish is not None:
        p = {**p, **finish(gathered)}
    pooled = _pool_fwd(pool_in, f"pool_{tag}")
    pm = _mm_cols(pooled, p["w_pool"], F32, f"pool_mm_{tag}", _NN)
    y = _post_fwd(o, z, gate, pm, p["head_norm"], p["pool_scale"], f"post_{tag}")
    h1 = _mm(y, p["w_out"], F32, f"out_proj_{tag}", res=h)
    u2 = _rmsnorm_fwd(h1, p["norm_ffn"], f"norm_ffn_{tag}")
    hid = _mm_up(u2, p["w_up"], f"up_proj_{tag}")
    act = _mlp_act_fwd(hid, p["conv_ffn"], f"mlp_act_{tag}")
    h2 = _mm_blocks_red(act, p["w_down"], f"down_proj_{tag}", _NN, res=h1)
    saved = dict(h=h, u=u, qkv_pre=qkv_pre, z=z, ba=ba, gate=gate, qkv=qkv, beta=beta, g=g, o=o, s_in=s_in, vnew=vnew,
                 tinv=tinv, wsv=wsv, pooled=pooled, pm=pm, y=y, h1=h1, u2=u2, hid=hid, act=act)
    return h2, saved, gathered, p


def _layer_bwd(dh, dh_b, p, s, tag, scatter=()):
    gr = {}
    da = _mm_to_blocks(dh_b, p["w_down"], f"d_act_{tag}")
    gr["w_down"] = _mm_tn_blocks(s["act"], dh_b, f"dw_down_{tag}", True, False)
    dhg, dhv, dwg, dwv = _mlp_act_bwd(da, s["hid"], p["conv_ffn"], f"mlp_act_bwd_{tag}")
    gr["conv_ffn"] = jnp.concatenate([dwg, dwv], axis=0)
    w_up = p["w_up"]
    du2 = _mm_blocks_red(dhg, w_up[:4], f"d_u2g_{tag}", _NT)
    du2 = _mm_blocks_red(dhv, w_up[4:], f"d_u2v_{tag}", _NT, res=du2)
    gr["w_up"] = jnp.concatenate([_mm_tn_blocks(s["u2"], dhg, f"dw_upg_{tag}", False, True),
                                  _mm_tn_blocks(s["u2"], dhv, f"dw_upv_{tag}", False, True)], axis=0)
    dh1, dh1_b, gr["norm_ffn"] = _rmsnorm_bwd(s["h1"], du2, dh, p["norm_ffn"], f"norm_ffn_bwd_{tag}")
    dy = _mm(dh1_b, p["w_out"], F32, f"d_y_{tag}", dims=_NT)
    gr["w_out"] = _mm_tn(s["y"], dh1_b, f"dw_out_{tag}")
    do, dz, dgate, dpm, gr["head_norm"], gr["pool_scale"] = _post_bwd(
        dy, s["o"], s["z"], s["gate"], s["pm"], p["head_norm"], p["pool_scale"], f"post_bwd_{tag}")
    dpooled = _mm_cols(dpm, p["w_pool"], F32, f"d_pooled_{tag}", _NT)
    gr["w_pool"] = _mm_tn_cols(s["pooled"], dpm, 4, f"dw_pool_{tag}")
    dpool_in = _pool_bwd(dpooled, f"pool_bwd_{tag}")
    own = (gr["w_up"].astype(BF16), gr["w_down"].reshape(NDEV, -1, D).astype(BF16))
    (dqkv, dbeta, dg), landed = _gdn_bwd(s["qkv"], s["beta"], s["g"], do, s["s_in"], s["vnew"], s["tinv"], s["wsv"],
                                         f"gdn_bwd_{tag}", own + tuple(scatter))
    dba, gr["a_log"], gr["dt_bias"] = _gates_bwd(s["ba"], p["a_row"], p["dt_row"], dbeta, dg, f"gates_bwd_{tag}")
    dqkv_pre, gr["conv_qkv"] = _gdn_pre_bwd(s["qkv_pre"], p["conv_qkv"], dqkv, f"gdn_pre_bwd_{tag}")
    du = None
    dws = []
    for nm, dseg, wseg in (("qkv", dqkv_pre, p["w_qkv"]), ("z", dz, p["w_z"]), ("ba", dba, p["w_ba"]),
                           ("pool", dpool_in, p["w_pl"]), ("gate", dgate, p["w_gate"])):
        du = _mm(dseg, wseg, F32, f"d_u_{nm}_{tag}", res=du, dims=_NT)
        dws.append(_mm_tn(s["u"], dseg, f"dw_{nm}_{tag}"))
    gr["w_in"] = jnp.concatenate([dws[0], dws[1], dws[2][:, 0:H], dws[2][:, LANE:LANE + H], dws[3], dws[4]], axis=1)
    dh0, dh0_b, gr["norm_mix"] = _rmsnorm_bwd(s["h"], du, dh1, p["norm_mix"], f"norm_mix_bwd_{tag}")
    return dh0, dh0_b, gr, landed


def _pad_lanes(v8):
    return jnp.pad(v8.reshape(1, H), ((0, 0), (0, LANE - H)))


def _pack(parts, rows):
    flat = jnp.concatenate([q.reshape(-1) for q in parts])
    return jnp.pad(flat, (0, rows * LANE - flat.shape[0])).reshape(rows, LANE)


def _unpack(packed, shapes):
    flat = packed.reshape(-1)
    out, off = [], 0
    for shp in shapes:
        n = 1
        for s_ in shp:
            n *= s_
        out.append(flat[off:off + n].reshape(shp))
        off += n
    return out


SMALL_ROWS = 336
REPL_ROWS = 64


def kernel(x, meta_tokens, norm_mix, w_in, conv_qkv, a_log, dt_bias, head_norm, w_pool, pool_scale, w_out, norm_ffn, w_up, conv_ffn, w_down, norm_final, loss_target, m_meta_tokens, m_norm_mix, m_w_in, m_conv_qkv, m_a_log, m_dt_bias, m_head_norm, m_w_pool, m_pool_scale, m_w_out, m_norm_ffn, m_w_up, m_conv_ffn, m_w_down, m_norm_final, v_meta_tokens, v_norm_mix, v_w_in, v_conv_qkv, v_a_log, v_dt_bias, v_head_norm, v_w_pool, v_pool_scale, v_w_out, v_norm_ffn, v_w_up, v_conv_ffn, v_w_down, v_norm_final):
    seq = x.shape[1]
    t = ROW0 + seq
    assert t % TE == 0 and t % (MM_TILES * 16) == 0 and t % CH == 0
    depth = w_in.shape[0]
    assert depth == 2
    cin = w_in.shape[2]

    def mixer_params(l, g_in, conv_q, conv_f, wp):
        wf = jnp.transpose(g_in, (1, 0, 2)).reshape(D, NDEV * cin)
        zpad = jnp.zeros((D, LANE - H), BF16)
        return dict(
            w_qkv=wf[:, 0:QKV], w_z=wf[:, QKV:QKV + D],
            w_ba=jnp.concatenate([wf[:, 4096:4104], zpad, wf[:, 4104:4112], zpad], axis=1),
            w_pl=wf[:, 4112:4624], w_gate=wf[:, 4624:6672], conv_qkv=conv_q, conv_ffn=conv_f, w_pool=wp,
            norm_mix=norm_mix[l].reshape(1, D), norm_ffn=norm_ffn[l].reshape(1, D),
            pool_scale=pool_scale[l].reshape(1, D), head_norm=head_norm[l].reshape(1, DH),
            a_row=_pad_lanes(a_log[l]), dt_row=_pad_lanes(dt_bias[l]))

    def late_params(g_up, g_out, g_down):
        return dict(w_out=g_out.reshape(D, D), w_up=g_up, w_down=g_down.reshape(4, FB, D))

    small_shapes = [conv_qkv.shape, conv_ffn.shape, w_pool.shape, meta_tokens.shape]
    small = _pack([conv_qkv, conv_ffn, w_pool, meta_tokens], SMALL_ROWS)
    w_in_b, w_up_b, w_out_b, w_down_b = w_in.astype(BF16), w_up.astype(BF16), w_out.astype(BF16), w_down.astype(BF16)
    g_in0, g_small = _gather([w_in_b[0], small], "gather_first")
    smalls = [_unpack(g_small[i], small_shapes) for i in range(NDEV)]
    conv_qkv_full = jnp.concatenate([sm[0] for sm in smalls], axis=2)
    conv_ffn_blk = jnp.stack([sm[1] for sm in smalls], axis=1)
    w_pool_full = jnp.concatenate([sm[2] for sm in smalls], axis=3).astype(BF16)
    meta_full = jnp.concatenate([sm[3] for sm in smalls], axis=1)

    h = jnp.concatenate([jnp.zeros((LEAD, D), F32), meta_full, x[0]], axis=0)
    p0 = mixer_params(0, g_in0, conv_qkv_full[0], conv_ffn_blk[0], w_pool_full[0])
    h, sv0, rest, p0 = _layer_fwd(
        h, p0, "l0", (w_up_b[0], w_out_b[0], w_down_b[0], w_in_b[1], w_up_b[1], w_out_b[1], w_down_b[1]),
        lambda got: late_params(*got[:3]))
    p1 = {**mixer_params(1, rest[3], conv_qkv_full[1], conv_ffn_blk[1], w_pool_full[1]), **late_params(*rest[4:])}
    h, sv1, _, _ = _layer_fwd(h, p1, "l1")
    layers = [p0, p1]
    saved = [sv0, sv1]
    target = jnp.concatenate([jnp.zeros((ROW0, D), F32), loss_target[0]], axis=0)
    dh, dh_b, d_norm_final, loss_row = _loss_bwd(h, target, norm_final.reshape(1, D), "loss")

    def mixer_blocks(gr):
        return (jnp.transpose(gr["w_in"].reshape(D, NDEV, cin), (1, 0, 2)).astype(BF16),
                gr["w_out"].reshape(NDEV, D // NDEV, D).astype(BF16))

    grads = [None] * depth
    dh, dh_b, grads[1], (l_up1, l_down1) = _layer_bwd(dh, dh_b, layers[1], saved[1], "l1")
    dh, dh_b, grads[0], (l_up0, l_down0, l_in1, l_out1) = _layer_bwd(dh, dh_b, layers[0], saved[0], "l0", mixer_blocks(grads[1]))
    grad_x = dh[ROW0:].reshape(1, seq, D)
    d_meta = dh[LEAD:ROW0]

    stk = lambda name: jnp.stack([grads[l][name] for l in range(depth)], axis=0)
    cq = conv_qkv.shape[2]
    pw = w_pool.shape[3]
    s_cq = jnp.transpose(stk("conv_qkv").reshape(depth, 4, NDEV, cq), (2, 0, 1, 3))
    s_cf = jnp.transpose(stk("conv_ffn"), (1, 0, 2, 3))
    s_wp = jnp.transpose(stk("w_pool").reshape(depth, 4, DH, NDEV, pw), (3, 0, 1, 2, 4))
    s_mt = jnp.transpose(d_meta.reshape(N_META, NDEV, D // NDEV), (1, 0, 2))
    b_small = jnp.stack([_pack([s_cq[i], s_cf[i], s_wp[i], s_mt[i]], SMALL_ROWS) for i in range(NDEV)], axis=0)
    l_in0, l_out0, l_small = _scatter([*mixer_blocks(grads[0]), b_small], "exchange_last")

    def upd(lands, w, m, v, name):
        res = [_adamw(land, w[l], m[l], v[l], f"adamw_{name}_l{l}") for l, land in enumerate(lands)]
        return [jnp.stack([res[0][kind], res[1][kind]], axis=0) for kind in range(4)]

    r_in = upd((l_in0, l_in1), w_in, m_w_in, v_w_in, "w_in")
    r_up = upd((l_up0, l_up1), w_up, m_w_up, v_w_up, "w_up")
    r_out = upd((l_out0, l_out1), w_out, m_w_out, v_w_out, "w_out")
    r_down = upd((l_down0, l_down1), w_down, m_w_down, v_w_down, "w_down")
    r_small = _adamw(l_small, small, _pack([m_conv_qkv, m_conv_ffn, m_w_pool, m_meta_tokens], SMALL_ROWS),
                     _pack([v_conv_qkv, v_conv_ffn, v_w_pool, v_meta_tokens], SMALL_ROWS), "adamw_small")
    r_small = [_unpack(o_, small_shapes) for o_ in r_small]

    repl_shapes = [norm_mix.shape, a_log.shape, dt_bias.shape, head_norm.shape, pool_scale.shape, norm_ffn.shape,
                   norm_final.shape, (1,)]
    rp = lambda name, n: jnp.stack([grads[l][name][0, :n] for l in range(depth)], axis=0)
    part = _pack([rp("norm_mix", D), rp("a_log", H), rp("dt_bias", H), rp("head_norm", DH), rp("pool_scale", D),
                  rp("norm_ffn", D), d_norm_final[0], loss_row[0, 0:1]], REPL_ROWS)
    (l_repl,) = _gather([part], "gather_replicated")
    zero1 = jnp.zeros((1,), F32)
    r_repl = _adamw(l_repl, _pack([norm_mix, a_log, dt_bias, head_norm, pool_scale, norm_ffn, norm_final, zero1], REPL_ROWS),
                    _pack([m_norm_mix, m_a_log, m_dt_bias, m_head_norm, m_pool_scale, m_norm_ffn, m_norm_final, zero1], REPL_ROWS),
                    _pack([v_norm_mix, v_a_log, v_dt_bias, v_head_norm, v_pool_scale, v_norm_ffn, v_norm_final, zero1], REPL_ROWS),
                    "adamw_replicated")
    r_repl = [_unpack(o_, repl_shapes) for o_ in r_repl]
    loss = r_repl[0][7].reshape(())

    def leaf(kind):
        sm, rr = r_small[kind], r_repl[kind]
        return [sm[3], rr[0], r_in[kind], sm[0], rr[1], rr[2], rr[3], sm[2], rr[4], r_out[kind], rr[5], r_up[kind],
                sm[1], r_down[kind], rr[6]]

    return (loss, grad_x, *leaf(0), *leaf(1), *leaf(2), *leaf(3))
```

```python
import jax
import jax.numpy as jnp
from jax import lax
from jax.experimental import pallas as pl
from jax.experimental.pallas import tpu as pltpu

F32 = jnp.float32
BF16 = jnp.bfloat16
HI = lax.Precision.HIGHEST
MESH = pl.DeviceIdType.MESH

D = 1024
H = 8
DH = 128
CH = 64
N_META = 16
LEAD = 48
ROW0 = LEAD + N_META
QKV = 3 * D
POOL_W = 512
POOL_WINDOWS = (2, 4, 8, 16)
FB = 704
NDEV = 8
EPS = 1e-6
MM_TILES = 12
TE = 192
LANE = 128
SUB = 8
VMEM_CAP = 56 << 20

ADAM_LR, ADAM_B1, ADAM_B2, ADAM_EPS, ADAM_WD, ADAM_STEP = 0.001, 0.9, 0.999, 1e-08, 0.01, 10

_NN = (((1,), (0,)), ((), ()))
_NT = (((1,), (1,)), ((), ()))
_TN = (((0,), (0,)), ((), ()))


def _dot(a, b, dims=_NN, precision=None):
    return lax.dot_general(a, b, dims, precision=precision, preferred_element_type=F32)


def _bdot(a, b, dims=_NN):
    return _dot(a.astype(BF16), b.astype(BF16), dims)


def _nbytes(shape, dtype):
    n = 1
    for s in shape:
        n *= s
    return n * jnp.dtype(dtype).itemsize


def _params(sem, block_bytes):
    limit = min(VMEM_CAP, 2 * block_bytes + (20 << 20))
    return pltpu.CompilerParams(dimension_semantics=sem, vmem_limit_bytes=limit)


def _sigmoid(x):
    return 1.0 / (1.0 + jnp.exp(-x))


def _col_tile(n):
    for t in (1024, 512, 256, 128):
        if n % t == 0:
            return t
    return n


def _matmul(a, b, *, dims, grid, a_spec, b_spec, o_spec, out_shape, name, red_axis=None, res=None):
    def body(*refs):
        if res is None:
            a_ref, b_ref, o_ref = refs
        else:
            a_ref, b_ref, r_ref, o_ref = refs
        part = _dot(a_ref[...], b_ref[...], dims)
        if red_axis is None:
            if res is not None:
                part = part + r_ref[...]
            o_ref[...] = part.astype(o_ref.dtype)
        else:
            r = pl.program_id(red_axis)

            @pl.when(r == 0)
            def _():
                o_ref[...] = part + r_ref[...] if res is not None else part

            @pl.when(r > 0)
            def _():
                o_ref[...] += part

    def blk(spec, arr):
        return _nbytes([s for s in spec.block_shape if s is not None], arr.dtype)

    ins = [a, b] + ([res] if res is not None else [])
    specs = [a_spec, b_spec] + ([o_spec] if res is not None else [])
    nb = blk(a_spec, a) + blk(b_spec, b) + 2 * _nbytes([s for s in o_spec.block_shape if s is not None], F32)
    sem = tuple("arbitrary" if i == red_axis else "parallel" for i in range(len(grid)))
    return pl.pallas_call(
        body, name=name, grid=grid, in_specs=specs, out_specs=o_spec, out_shape=out_shape,
        compiler_params=_params(sem, nb),
    )(*ins)


def _row_tiles(m, row_bytes, fixed_bytes, temp_row_bytes=0):
    for nt in (MM_TILES // 2, MM_TILES):
        tm = m // nt
        if 2 * (row_bytes * tm + fixed_bytes) + temp_row_bytes * tm <= VMEM_CAP - (10 << 20):
            return nt
    return MM_TILES


def _mm(a, b, out_dtype, name, res=None, dims=_NN):
    m, k = a.shape
    n = b.shape[1] if dims == _NN else b.shape[0]
    tn = _col_tile(n)
    nt = _row_tiles(m, 2 * k + tn * (jnp.dtype(out_dtype).itemsize + (4 if res is not None else 0)), 2 * k * tn, 4 * tn)
    tm = m // nt
    if dims == _NN:
        b_spec = pl.BlockSpec((k, tn), lambda j, i: (0, j))
    else:
        b_spec = pl.BlockSpec((tn, k), lambda j, i: (j, 0))
    return _matmul(
        a, b, dims=dims, grid=(n // tn, nt), a_spec=pl.BlockSpec((tm, k), lambda j, i: (i, 0)), b_spec=b_spec,
        o_spec=pl.BlockSpec((tm, tn), lambda j, i: (i, j)), out_shape=jax.ShapeDtypeStruct((m, n), out_dtype),
        name=name, res=res)


def _mm_tn(a, g, name):
    m, k = a.shape
    n = g.shape[1]
    tn = _col_tile(n)
    nt = _row_tiles(m, 2 * k + 2 * tn, 4 * k * tn)
    tm = m // nt
    return _matmul(
        a, g, dims=_TN, grid=(n // tn, nt), red_axis=1, a_spec=pl.BlockSpec((tm, k), lambda j, i: (i, 0)),
        b_spec=pl.BlockSpec((tm, tn), lambda j, i: (i, j)), o_spec=pl.BlockSpec((k, tn), lambda j, i: (0, j)),
        out_shape=jax.ShapeDtypeStruct((k, n), F32), name=name)


def _mm_up(u, w_up, name):
    t = u.shape[0]
    g = w_up.shape[0]
    nt = _row_tiles(t, 2 * D + 2 * FB, 2 * D * FB, 4 * FB)
    tm = t // nt
    return _matmul(
        u, w_up, dims=_NN, grid=(g, nt), a_spec=pl.BlockSpec((tm, D), lambda g_, i: (i, 0)),
        b_spec=pl.BlockSpec((None, D, FB), lambda g_, i: (g_, 0, 0)),
        o_spec=pl.BlockSpec((None, tm, FB), lambda g_, i: (g_, i, 0)),
        out_shape=jax.ShapeDtypeStruct((g, t, FB), BF16), name=name)


def _mm_blocks_red(a, b, name, dims, res=None):
    g, t, k = a.shape
    n = b.shape[2] if dims == _NN else b.shape[1]
    nt = _row_tiles(t, 2 * k + n * (8 if res is not None else 4), 2 * k * n, 4 * n)
    tm = t // nt
    return _matmul(
        a, b, dims=dims, grid=(nt, g), red_axis=1, a_spec=pl.BlockSpec((None, tm, k), lambda i, g_: (g_, i, 0)),
        b_spec=pl.BlockSpec((None,) + b.shape[1:], lambda i, g_: (g_, 0, 0)),
        o_spec=pl.BlockSpec((tm, n), lambda i, g_: (i, 0)), out_shape=jax.ShapeDtypeStruct((t, n), F32),
        name=name, res=res)


def _mm_to_blocks(a, b, name):
    t, k = a.shape
    g, n, _ = b.shape
    nt = _row_tiles(t, 2 * k + 2 * n, 2 * k * n, 4 * n)
    tm = t // nt
    return _matmul(
        a, b, dims=_NT, grid=(g, nt), a_spec=pl.BlockSpec((tm, k), lambda g_, i: (i, 0)),
        b_spec=pl.BlockSpec((None, n, k), lambda g_, i: (g_, 0, 0)),
        o_spec=pl.BlockSpec((None, tm, n), lambda g_, i: (g_, i, 0)),
        out_shape=jax.ShapeDtypeStruct((g, t, n), BF16), name=name)


def _mm_tn_blocks(a, g, name, a_blocked, g_blocked):
    nb = a.shape[0] if a_blocked else g.shape[0]
    t = a.shape[-2]
    k, n = a.shape[-1], g.shape[-1]
    nt = _row_tiles(t, 2 * k + 2 * n, 4 * k * n)
    tm = t // nt
    a_spec = (pl.BlockSpec((None, tm, k), lambda g_, i: (g_, i, 0)) if a_blocked
              else pl.BlockSpec((tm, k), lambda g_, i: (i, 0)))
    g_spec = (pl.BlockSpec((None, tm, n), lambda g_, i: (g_, i, 0)) if g_blocked
              else pl.BlockSpec((tm, n), lambda g_, i: (i, 0)))
    return _matmul(
        a, g, dims=_TN, grid=(nb, nt), red_axis=1, a_spec=a_spec, b_spec=g_spec,
        o_spec=pl.BlockSpec((None, k, n), lambda g_, i: (g_, 0, 0)),
        out_shape=jax.ShapeDtypeStruct((nb, k, n), F32), name=name)


def _mm_cols(a, b, out_dtype, name, dims):
    t = a.shape[0]
    g = b.shape[0]
    ka = a.shape[1] // g
    n = b.shape[2] if dims == _NN else b.shape[1]
    tm = t // MM_TILES
    return _matmul(
        a, b, dims=dims, grid=(g, MM_TILES), a_spec=pl.BlockSpec((tm, ka), lambda g_, i: (i, g_)),
        b_spec=pl.BlockSpec((None,) + b.shape[1:], lambda g_, i: (g_, 0, 0)),
        o_spec=pl.BlockSpec((tm, n), lambda g_, i: (i, g_)), out_shape=jax.ShapeDtypeStruct((t, g * n), out_dtype),
        name=name)


def _mm_tn_cols(a, g, nblk, name):
    t = a.shape[0]
    ka, n = a.shape[1] // nblk, g.shape[1] // nblk
    tm = t // MM_TILES
    return _matmul(
        a, g, dims=_TN, grid=(nblk, MM_TILES), red_axis=1, a_spec=pl.BlockSpec((tm, ka), lambda g_, i: (i, g_)),
        b_spec=pl.BlockSpec((tm, n), lambda g_, i: (i, g_)), o_spec=pl.BlockSpec((None, ka, n), lambda g_, i: (g_, 0, 0)),
        out_shape=jax.ShapeDtypeStruct((nblk, ka, n), F32), name=name)


def _rows(cols, n=None):
    if n is None:
        return pl.BlockSpec((TE, cols), lambda i: (i, 0))
    return pl.BlockSpec((TE, cols), lambda i: (n - 1 - i, 0))


def _whole(shape):
    return pl.BlockSpec(shape, lambda *_: (0,) * len(shape))


def _row_ids(i, rows=TE):
    return i * rows + lax.broadcasted_iota(jnp.int32, (rows, 1), 0)


def _rmsnorm_fwd(h, gain, name):
    t = h.shape[0]

    def body(h_ref, g_ref, u_ref):
        x = h_ref[...]
        r = lax.rsqrt(jnp.mean(x * x, axis=-1, keepdims=True) + EPS)
        u_ref[...] = (x * r * g_ref[...]).astype(BF16)

    return pl.pallas_call(
        body, name=name, grid=(t // TE,), in_specs=[_rows(D), _whole((1, D))], out_specs=_rows(D),
        out_shape=jax.ShapeDtypeStruct((t, D), BF16), compiler_params=_params(("parallel",), 3 * TE * D * 4),
    )(h, gain)


def _rmsnorm_bwd(x, du, dres, gain, name):
    t = x.shape[0]

    def body(x_ref, du_ref, dr_ref, g_ref, dx_ref, dxb_ref, dg_ref):
        i = pl.program_id(0)
        xv = x_ref[...]
        r = lax.rsqrt(jnp.mean(xv * xv, axis=-1, keepdims=True) + EPS)
        gdy = du_ref[...] * g_ref[...]
        dx = dr_ref[...] + r * gdy - xv * (r * r * r) * jnp.mean(xv * gdy, axis=-1, keepdims=True)
        dx = jnp.where(_row_ids(i) >= LEAD, dx, 0.0)
        dx_ref[...] = dx
        dxb_ref[...] = dx.astype(BF16)
        part = jnp.sum(du_ref[...] * xv * r, axis=0, keepdims=True)

        @pl.when(i == 0)
        def _():
            dg_ref[...] = part

        @pl.when(i > 0)
        def _():
            dg_ref[...] += part

    return pl.pallas_call(
        body, name=name, grid=(t // TE,), in_specs=[_rows(D), _rows(D), _rows(D), _whole((1, D))],
        out_specs=[_rows(D), _rows(D), _whole((1, D))],
        out_shape=[jax.ShapeDtypeStruct((t, D), F32), jax.ShapeDtypeStruct((t, D), BF16),
                   jax.ShapeDtypeStruct((1, D), F32)],
        compiler_params=_params(("arbitrary",), 5 * TE * D * 4),
    )(x, du, dres, gain)


def _loss_bwd(h, target, gain, name):
    t = h.shape[0]

    def body(h_ref, t_ref, g_ref, dx_ref, dxb_ref, dg_ref, loss_ref):
        i = pl.program_id(0)
        xv = h_ref[...]
        gain_v = g_ref[...]
        r = lax.rsqrt(jnp.mean(xv * xv, axis=-1, keepdims=True) + EPS)
        real = _row_ids(i) >= ROW0
        err = jnp.where(real, xv * r * gain_v - t_ref[...], 0.0)
        dy = err * (1.0 / D)
        gdy = dy * gain_v
        dx = r * gdy - xv * (r * r * r) * jnp.mean(xv * gdy, axis=-1, keepdims=True)
        dx_ref[...] = dx
        dxb_ref[...] = dx.astype(BF16)
        dgp = jnp.sum(dy * xv * r, axis=0, keepdims=True)
        lp = 0.5 * jnp.sum(jnp.mean(err * err, axis=-1, keepdims=True), axis=0, keepdims=True)

        @pl.when(i == 0)
        def _():
            dg_ref[...] = dgp
            loss_ref[...] = jnp.broadcast_to(lp, (1, LANE))

        @pl.when(i > 0)
        def _():
            dg_ref[...] += dgp
            loss_ref[...] += jnp.broadcast_to(lp, (1, LANE))

    return pl.pallas_call(
        body, name=name, grid=(t // TE,), in_specs=[_rows(D), _rows(D), _whole((1, D))],
        out_specs=[_rows(D), _rows(D), _whole((1, D)), _whole((1, LANE))],
        out_shape=[jax.ShapeDtypeStruct((t, D), F32), jax.ShapeDtypeStruct((t, D), BF16),
                   jax.ShapeDtypeStruct((1, D), F32), jax.ShapeDtypeStruct((1, LANE), F32)],
        compiler_params=_params(("arbitrary",), 4 * TE * D * 4),
    )(h, target, gain)


def _seq_scratch(cols):
    return pltpu.VMEM((-(-cols // LANE), TE + SUB, LANE), F32)


def _taps(scr, c, wd, first, n, k):
    return [scr[c, first - (k - 1) + j:first - (k - 1) + j + n, 0:wd] for j in range(k)]


def _stage_history(scr, i):
    @pl.when(i == 0)
    def _():
        scr[...] = jnp.zeros(scr.shape, F32)

    @pl.when(i > 0)
    def _():
        scr[:, 0:SUB, :] = scr[:, TE:TE + SUB, :]


def _stage_future(scr, i):
    @pl.when(i == 0)
    def _():
        scr[...] = jnp.zeros(scr.shape, F32)

    @pl.when(i > 0)
    def _():
        scr[:, TE:TE + SUB, :] = scr[:, 0:SUB, :]


def _conv(tp, w):
    out = w[0:1] * tp[0]
    for j in range(1, len(tp)):
        out = out + w[j:j + 1] * tp[j]
    return out


def _conv_t(ds, c, wd, w):
    k = w.shape[0]
    out = w[k - 1:k] * ds[c, 0:TE, 0:wd]
    for j in range(k - 1):
        out = out + w[j:j + 1] * ds[c, k - 1 - j:k - 1 - j + TE, 0:wd]
    return out


def _gdn_pre_fwd(x, w, name):
    t = x.shape[0]

    def body(x_ref, w_ref, o_ref, xs):
        _stage_history(xs, pl.program_id(0))
        for hh in range(3 * H):
            sl = slice(hh * DH, (hh + 1) * DH)
            xs[hh, SUB:SUB + TE, :] = x_ref[:, sl].astype(F32)
            cv = _conv(_taps(xs, hh, DH, SUB, TE, 4), w_ref[:, sl])
            s = cv * _sigmoid(cv)
            if hh < 2 * H:
                s = s * lax.rsqrt(jnp.sum(s * s, axis=-1, keepdims=True) + EPS)
                if hh < H:
                    s = s * (DH ** -0.5)
            o_ref[:, sl] = s

    return pl.pallas_call(
        body, name=name, grid=(t // TE,), in_specs=[_rows(QKV), _whole((4, QKV))], out_specs=_rows(QKV),
        out_shape=jax.ShapeDtypeStruct((t, QKV), F32), scratch_shapes=[_seq_scratch(QKV)],
        compiler_params=_params(("arbitrary",), TE * QKV * 8),
    )(x, w)


def _gdn_pre_bwd(x, w, dqkv, name):
    t = x.shape[0]
    n = t // TE
    hb = TE // 16

    def body(x_ref, xp_ref, w_ref, d_ref, dx_ref, dw_ref, xs, ds):
        i = pl.program_id(0)

        @pl.when(i == 0)
        def _():
            dw_ref[...] = jnp.zeros_like(dw_ref)

        _stage_future(ds, i)
        for hh in range(3 * H):
            sl = slice(hh * DH, (hh + 1) * DH)
            xs[hh, 0:SUB, :] = jnp.where(i == n - 1, 0.0, xp_ref[SUB:2 * SUB, sl].astype(F32))
            xs[hh, SUB:SUB + TE, :] = x_ref[:, sl].astype(F32)
            wv = w_ref[:, sl]
            tp = _taps(xs, hh, DH, SUB, TE, 4)
            cv = _conv(tp, wv)
            sg = _sigmoid(cv)
            s = cv * sg
            dsv = d_ref[:, sl]
            if hh < 2 * H:
                if hh < H:
                    dsv = dsv * (DH ** -0.5)
                r = lax.rsqrt(jnp.sum(s * s, axis=-1, keepdims=True) + EPS)
                dsv = r * dsv - s * (r * r * r) * jnp.sum(s * dsv, axis=-1, keepdims=True)
            dcv = dsv * (sg * (1.0 + cv * (1.0 - sg)))
            ds[hh, 0:TE, :] = dcv
            dx_ref[:, sl] = _conv_t(ds, hh, DH, wv).astype(BF16)
            dw_ref[:, sl] += jnp.concatenate([jnp.sum(tp[j] * dcv, axis=0, keepdims=True) for j in range(4)], axis=0)

    return pl.pallas_call(
        body, name=name, grid=(n,),
        in_specs=[_rows(QKV, n), pl.BlockSpec((16, QKV), lambda i: (jnp.maximum((n - 1 - i) * hb - 1, 0), 0)),
                  _whole((4, QKV)), _rows(QKV, n)],
        out_specs=[_rows(QKV, n), _whole((4, QKV))],
        out_shape=[jax.ShapeDtypeStruct((t, QKV), BF16), jax.ShapeDtypeStruct((4, QKV), F32)],
        scratch_shapes=[_seq_scratch(QKV), _seq_scratch(QKV)],
        compiler_params=_params(("arbitrary",), TE * QKV * 14),
    )(x, x, w, dqkv)


def _softplus(x):
    return jnp.maximum(x, 0.0) + jnp.log(1.0 + jnp.exp(-jnp.abs(x)))


def _gates_fwd(ba, a_row, dt_row, name):
    t = ba.shape[0]

    def body(ba_ref, a_ref, dt_ref, b_out, g_out):
        real = _row_ids(pl.program_id(0)) >= LEAD
        b_out[...] = jnp.where(real, _sigmoid(ba_ref[:, 0:LANE]), 0.0)
        g = -jnp.exp(a_ref[...]) * _softplus(ba_ref[:, LANE:2 * LANE] + dt_ref[...])
        g_out[...] = jnp.where(real, g, 0.0)

    return pl.pallas_call(
        body, name=name, grid=(t // TE,), in_specs=[_rows(2 * LANE), _whole((1, LANE)), _whole((1, LANE))],
        out_specs=[_rows(LANE), _rows(LANE)],
        out_shape=[jax.ShapeDtypeStruct((t, LANE), F32), jax.ShapeDtypeStruct((t, LANE), F32)],
        compiler_params=_params(("parallel",), TE * LANE * 16),
    )(ba, a_row, dt_row)


def _gates_bwd(ba, a_row, dt_row, dbeta, dg, name):
    t = ba.shape[0]

    def body(ba_ref, a_ref, dt_ref, db_ref, dg_ref, dba_ref, da_out, ddt_out):
        i = pl.program_id(0)
        real = _row_ids(i) >= LEAD
        beta = _sigmoid(ba_ref[:, 0:LANE])
        draw_b = jnp.where(real, db_ref[...] * beta * (1.0 - beta), 0.0)
        pre = ba_ref[:, LANE:2 * LANE] + dt_ref[...]
        neg_a = -jnp.exp(a_ref[...])
        dgv = jnp.where(real, dg_ref[...], 0.0)
        draw_a = dgv * neg_a * _sigmoid(pre)
        dba_ref[:, 0:LANE] = draw_b.astype(BF16)
        dba_ref[:, LANE:2 * LANE] = draw_a.astype(BF16)
        dal = jnp.sum(dgv * neg_a * _softplus(pre), axis=0, keepdims=True)
        ddt = jnp.sum(draw_a, axis=0, keepdims=True)

        @pl.when(i == 0)
        def _():
            da_out[...] = dal
            ddt_out[...] = ddt

        @pl.when(i > 0)
        def _():
            da_out[...] += dal
            ddt_out[...] += ddt

    return pl.pallas_call(
        body, name=name, grid=(t // TE,),
        in_specs=[_rows(2 * LANE), _whole((1, LANE)), _whole((1, LANE)), _rows(LANE), _rows(LANE)],
        out_specs=[_rows(2 * LANE), _whole((1, LANE)), _whole((1, LANE))],
        out_shape=[jax.ShapeDtypeStruct((t, 2 * LANE), BF16), jax.ShapeDtypeStruct((1, LANE), F32),
                   jax.ShapeDtypeStruct((1, LANE), F32)],
        compiler_params=_params(("arbitrary",), TE * LANE * 24),
    )(ba, a_row, dt_row, dbeta, dg)


_OFFSETS = [(dx, dy, dc) for dx in (0, 1) for dy in (0, 1) for dc in (0, 1)][1:]
NPEER = len(_OFFSETS)
ANY_SPEC = pl.BlockSpec(memory_space=pl.ANY)


def _place():
    return lax.axis_index("x"), lax.axis_index("y"), lax.axis_index("c")


def _index(p):
    return 4 * p[0] + 2 * p[1] + p[2]


def _comm_scratch(n):
    return [pltpu.SemaphoreType.DMA((n * NPEER,)), pltpu.SemaphoreType.DMA((n * NPEER,)), pltpu.SemaphoreType.DMA((n,))]


def _scatter_copies(ins, outs, send, recv):
    me = _place()
    mi = _index(me)
    res = []
    for j, d in enumerate(_OFFSETS):
        peer = tuple(1 - v if bit else v for v, bit in zip(me, d))
        pi = _index(peer)
        for k in range(len(ins)):
            sem = k * NPEER + j
            mine = pltpu.make_async_remote_copy(src_ref=ins[k].at[pi], dst_ref=outs[k].at[mi], send_sem=send.at[sem],
                                                recv_sem=recv.at[sem], device_id=peer, device_id_type=MESH)
            theirs = pltpu.make_async_remote_copy(src_ref=ins[k].at[pi], dst_ref=outs[k].at[pi], send_sem=send.at[sem],
                                                  recv_sem=recv.at[sem], device_id=peer, device_id_type=MESH)
            res.append((mine, theirs))
    return res


def _scatter_own(ins, outs, loc):
    mi = _index(_place())
    return [pltpu.make_async_copy(ins[k].at[mi], outs[k].at[mi], loc.at[k]) for k in range(len(ins))]


def _scatter_start(ins, outs, send, recv, loc):
    for cp in _scatter_own(ins, outs, loc):
        cp.start()
    for mine, _ in _scatter_copies(ins, outs, send, recv):
        mine.start()


def _scatter_wait(ins, outs, send, recv, loc):
    cps = _scatter_copies(ins, outs, send, recv)
    for _, theirs in cps:
        theirs.wait_recv()
    for mine, _ in cps:
        mine.wait_send()
    for cp in _scatter_own(ins, outs, loc):
        cp.wait()


def _gather_parts(ins, outs, send, recv):
    x, y, c = _place()
    chips = [(1 - x, y), (x, 1 - y), (1 - x, 1 - y)]

    def cp(k, slot, src, block, to):
        return pltpu.make_async_remote_copy(src_ref=src, dst_ref=outs[k].at[_index(block)], send_sem=send.at[k * NPEER + slot],
                                            recv_sem=recv.at[k * NPEER + slot], device_id=to, device_id_type=MESH)

    return (x, y, c), (x, y, 1 - c), chips, cp


def _gather_start(ins, outs, send, recv, loc):
    me, sib, chips, cp = _gather_parts(ins, outs, send, recv)
    for k in range(len(ins)):
        pltpu.make_async_copy(ins[k], outs[k].at[_index(me)], loc.at[k]).start()
        cp(k, 0, ins[k], me, sib).start()
        for j, chip in enumerate(chips):
            cp(k, 1 + j, ins[k], me, (*chip, me[2])).start()


def _gather_forward(ins, outs, send, recv, loc):
    me, sib, chips, cp = _gather_parts(ins, outs, send, recv)
    for j, chip in enumerate(chips):
        blk = (*chip, me[2])
        for k in range(len(ins)):
            cp(k, 1 + j, ins[k], blk, me).wait_recv()
            cp(k, 4 + j, outs[k].at[_index(blk)], blk, sib).start()


def _gather_finish(ins, outs, send, recv, loc):
    me, sib, chips, cp = _gather_parts(ins, outs, send, recv)
    for k in range(len(ins)):
        cp(k, 0, ins[k], sib, me).wait_recv()
        for j, chip in enumerate(chips):
            cp(k, 4 + j, ins[k], (*chip, sib[2]), me).wait_recv()
        cp(k, 0, ins[k], me, sib).wait_send()
        for j, chip in enumerate(chips):
            cp(k, 1 + j, ins[k], me, (*chip, me[2])).wait_send()
            cp(k, 4 + j, outs[k].at[_index((*chip, me[2]))], (*chip, me[2]), sib).wait_send()
        pltpu.make_async_copy(ins[k], outs[k].at[_index(me)], loc.at[k]).wait()


def _gathered_shapes(arrs):
    return [jax.ShapeDtypeStruct((NDEV,) + a.shape, a.dtype) for a in arrs]


def _gather(arrs, name):
    n = len(arrs)

    def body(*refs):
        ins, outs, sems = refs[:n], refs[n:2 * n], refs[2 * n:]
        _gather_start(ins, outs, *sems)
        _gather_forward(ins, outs, *sems)
        _gather_finish(ins, outs, *sems)

    return pl.pallas_call(body, name=name, in_specs=[ANY_SPEC] * n, out_specs=[ANY_SPEC] * n,
                          out_shape=_gathered_shapes(arrs), scratch_shapes=_comm_scratch(n))(*arrs)


def _scatter(arrs, name):
    n = len(arrs)

    def body(*refs):
        ins, outs, sems = refs[:n], refs[n:2 * n], refs[2 * n:]
        _scatter_start(ins, outs, *sems)
        _scatter_wait(ins, outs, *sems)

    return pl.pallas_call(body, name=name, in_specs=[ANY_SPEC] * n, out_specs=[ANY_SPEC] * n,
                          out_shape=[jax.ShapeDtypeStruct(a.shape, a.dtype) for a in arrs],
                          scratch_shapes=_comm_scratch(n))(*arrs)


_BNN = (((2,), (1,)), ((0,), (0,)))
_BNT = (((2,), (2,)), ((0,), (0,)))
_BTN = (((1,), (1,)), ((0,), (0,)))


def _split(a):
    hi = a.astype(BF16)
    return hi, (a - hi.astype(F32)).astype(BF16)


def _dot3(a, b):
    ah, al = _split(a)
    bh, bl = _split(b)
    m = a.shape[1]
    r = _dot(jnp.concatenate([ah, al], axis=1), bh, _BNN)
    return r[:, :m] + r[:, m:] + _dot(ah, bl, _BNN)


def _heads(ref, off):
    return jnp.stack([ref[:, off + h * DH:off + (h + 1) * DH] for h in range(H)])


def _cols(a):
    return jnp.stack([a[:, h:h + 1] for h in range(H)])


def _lanes(a):
    lane = lax.broadcasted_iota(jnp.int32, (CH, LANE), 1)
    out = jnp.zeros((CH, LANE), F32)
    for h in range(H):
        out = jnp.where(lane == h, a[h], out)
    return out


def _chunk_prep(qkv_ref, b_ref, g_ref):
    row = lax.broadcasted_iota(jnp.int32, (CH, CH), 0)
    col = lax.broadcasted_iota(jnp.int32, (CH, CH), 1)
    incl, strict = row >= col, row > col
    gc = _dot(incl.astype(F32), g_ref[...], precision=HI)
    gct = gc.T
    q, k, v = _heads(qkv_ref, 0), _heads(qkv_ref, D), _heads(qkv_ref, 2 * D)
    bcol, gcol = _cols(b_ref[...]), _cols(gc)
    grow = jnp.stack([gct[h:h + 1, :] for h in range(H)])
    glast = _cols(gc[CH - 1:CH, :])
    dec = jnp.exp(jnp.where(incl[None], gcol - grow, -1e30))
    kb = k * bcol
    ab = _bdot(jnp.concatenate([kb, q], axis=1), k, _BNT)
    egc, ekc = jnp.exp(gcol), jnp.exp(glast - gcol)
    return dict(row=row, col=col, strict=strict[None], q=q, k=k, v=v, bcol=bcol, dec=dec, kb=kb,
                lm=jnp.where(strict[None], ab[:, :CH] * dec, 0.0), qk=ab[:, CH:] * dec, egc=egc, ekc=ekc,
                gth=jnp.exp(glast), qd=q * egc, kd=k * ekc, vb=v * bcol, kbg=kb * egc)


def _unit_lower_inverse(lm, eye):
    n = -lm
    x = eye + n
    pw = _dot3(n, n)
    for it in range(5):
        if it < 4:
            xp = _dot3(jnp.concatenate([x, pw], axis=1), pw)
            x = x + xp[:, :CH]
            pw = xp[:, CH:]
        else:
            x = x + _dot3(x, pw)
    return x


def _gdn_fwd(qkv, beta, g, name, gather=()):
    t = qkv.shape[0]
    nc = t // CH
    ng = len(gather)

    def body(qkv_ref, b_ref, g_ref, *rest):
        c_ins, (o_ref, sin_ref, vn_ref, ti_ref, w_ref) = rest[:ng], rest[ng:ng + 5]
        c_outs, state, sems = rest[ng + 5:2 * ng + 5], rest[2 * ng + 5], rest[2 * ng + 6:]
        step = pl.program_id(0)

        @pl.when(step == 0)
        def _():
            state[...] = jnp.zeros_like(state)
            if ng:
                _gather_start(c_ins, c_outs, *sems)

        if ng:
            @pl.when(step == max(nc - 12, 0))
            def _():
                _gather_forward(c_ins, c_outs, *sems)

            @pl.when(step == nc - 1)
            def _():
                _gather_finish(c_ins, c_outs, *sems)

        pr = _chunk_prep(qkv_ref, b_ref, g_ref)
        tinv = _unit_lower_inverse(pr["lm"], (pr["row"] == pr["col"]).astype(F32)[None])
        uw = _bdot(tinv, jnp.concatenate([pr["vb"], pr["kbg"]], axis=2), _BNN)
        u, w = uw[:, :, :DH], uw[:, :, DH:]
        s = state[...]
        ws = _bdot(jnp.concatenate([w, pr["qd"]], axis=1), s, _BNN)
        vn = u - ws[:, :CH]
        o = ws[:, CH:] + _bdot(pr["qk"], vn, _BNN)
        state[...] = s * pr["gth"] + _bdot(pr["kd"], vn, _BTN)
        sin_ref[0] = s
        ti_ref[0] = tinv
        for h in range(H):
            sl = slice(h * DH, (h + 1) * DH)
            o_ref[:, sl] = o[h]
            vn_ref[:, sl] = vn[h]
            w_ref[:, sl] = w[h]

    chunk = lambda cols: pl.BlockSpec((CH, cols), lambda c: (c, 0))
    outs = pl.pallas_call(
        body, name=name, grid=(nc,), in_specs=[chunk(QKV), chunk(LANE), chunk(LANE)] + [ANY_SPEC] * ng,
        out_specs=[chunk(D), pl.BlockSpec((1, H, DH, DH), lambda c: (c, 0, 0, 0)), chunk(D),
                   pl.BlockSpec((1, H, CH, CH), lambda c: (c, 0, 0, 0)), chunk(D)] + [ANY_SPEC] * ng,
        out_shape=[jax.ShapeDtypeStruct((t, D), F32), jax.ShapeDtypeStruct((nc, H, DH, DH), F32),
                   jax.ShapeDtypeStruct((t, D), F32), jax.ShapeDtypeStruct((nc, H, CH, CH), F32),
                   jax.ShapeDtypeStruct((t, D), F32)] + _gathered_shapes(gather),
        scratch_shapes=[pltpu.VMEM((H, DH, DH), F32)] + (_comm_scratch(ng) if ng else []),
        compiler_params=_params(("arbitrary",), 4 << 20),
    )(qkv, beta, g, *gather)
    return outs[:5], outs[5:]


def _gdn_bwd(qkv, beta, g, do, s_in, vnew, tinv, wsv, name, scatter=()):
    t = qkv.shape[0]
    nc = t // CH
    ns = len(scatter)

    def body(qkv_ref, b_ref, g_ref, do_ref, sin_ref, vn_ref, ti_ref, w_ref, *rest):
        c_ins, (dqkv_ref, db_ref, dg_ref) = rest[:ns], rest[ns:ns + 3]
        c_outs, dstate, sems = rest[ns + 3:2 * ns + 3], rest[2 * ns + 3], rest[2 * ns + 4:]
        step = pl.program_id(0)

        @pl.when(step == 0)
        def _():
            dstate[...] = jnp.zeros_like(dstate)
            if ns:
                _scatter_start(c_ins, c_outs, *sems)

        if ns:
            @pl.when(step == nc - 1)
            def _():
                _scatter_wait(c_ins, c_outs, *sems)

        pr = _chunk_prep(qkv_ref, b_ref, g_ref)
        ti, s = ti_ref[0], sin_ref[0]
        w, vn, doh = _heads(w_ref, 0), _heads(vn_ref, 0), _heads(do_ref, 0)
        ds = dstate[...]
        dvn = _bdot(pr["kd"], ds, _BNN) + _bdot(pr["qk"], doh, _BTN)
        dkd = _bdot(vn, ds, _BNT)
        dqd = _bdot(doh, s, _BNT)
        dqk = _bdot(doh, vn, _BNT)
        dw = -_bdot(dvn, s, _BNT)
        dgt = jnp.sum(jnp.sum(ds * s, axis=2, keepdims=True), axis=1, keepdims=True)
        dstate[...] = ds * pr["gth"] + _bdot(pr["qd"], doh, _BTN) - _bdot(w, dvn, _BTN)
        duw = jnp.concatenate([dvn, dw], axis=2)
        dvk = _bdot(ti, duw, _BTN)
        dvb, dkbg = dvk[:, :, :DH], dvk[:, :, DH:]
        dti = _bdot(duw, jnp.concatenate([pr["vb"], pr["kbg"]], axis=2), _BNT)
        dl = -_dot(_dot(ti, dti, _BTN, precision=HI), ti, _BNT, precision=HI)
        dl = jnp.where(pr["strict"], dl, 0.0)
        dab = jnp.concatenate([dl * pr["dec"], dqk * pr["dec"]], axis=1)
        r1 = _bdot(dab, pr["k"], _BNN)
        dkb = r1[:, :CH] + dkbg * pr["egc"]
        dq = r1[:, CH:] + dqd * pr["egc"]
        dk = _bdot(dab, jnp.concatenate([pr["kb"], pr["q"]], axis=1), _BTN) + dkb * pr["bcol"] + dkd * pr["ekc"]
        m = dl * pr["lm"] + dqk * pr["qk"]
        colsum = _dot(m, jnp.ones((H, CH, LANE), F32), _BTN, precision=HI)[:, :, 0:1]
        kdsum = jnp.sum(dkd * pr["kd"], axis=2, keepdims=True)
        dgc = (jnp.sum(m, axis=2, keepdims=True) - colsum + jnp.sum(dkbg * pr["kbg"], axis=2, keepdims=True)
               + jnp.sum(dqd * pr["qd"], axis=2, keepdims=True) - kdsum)
        dglast = jnp.sum(kdsum, axis=1, keepdims=True) + dgt * pr["gth"]
        last_row = lax.broadcasted_iota(jnp.int32, (1, CH, 1), 1) == CH - 1
        dgc = dgc + jnp.where(last_row, dglast, 0.0)
        dbeta = jnp.sum(dkb * pr["k"], axis=2, keepdims=True) + jnp.sum(dvb * pr["v"], axis=2, keepdims=True)
        dv = dvb * pr["bcol"]
        for h in range(H):
            dqkv_ref[:, h * DH:(h + 1) * DH] = dq[h]
            dqkv_ref[:, D + h * DH:D + (h + 1) * DH] = dk[h]
            dqkv_ref[:, 2 * D + h * DH:2 * D + (h + 1) * DH] = dv[h]
        db_ref[...] = _lanes(dbeta)
        dg_ref[...] = _dot((pr["row"] <= pr["col"]).astype(F32), _lanes(dgc), precision=HI)

    chunk = lambda cols: pl.BlockSpec((CH, cols), lambda c: (nc - 1 - c, 0))
    sq = lambda a, b: pl.BlockSpec((1, H, a, b), lambda c: (nc - 1 - c, 0, 0, 0))
    outs = pl.pallas_call(
        body, name=name, grid=(nc,),
        in_specs=[chunk(QKV), chunk(LANE), chunk(LANE), chunk(D), sq(DH, DH), chunk(D), sq(CH, CH), chunk(D)] + [ANY_SPEC] * ns,
        out_specs=[chunk(QKV), chunk(LANE), chunk(LANE)] + [ANY_SPEC] * ns,
        out_shape=[jax.ShapeDtypeStruct((t, QKV), F32), jax.ShapeDtypeStruct((t, LANE), F32),
                   jax.ShapeDtypeStruct((t, LANE), F32)] + [jax.ShapeDtypeStruct(a.shape, a.dtype) for a in scatter],
        scratch_shapes=[pltpu.VMEM((H, DH, DH), F32)] + (_comm_scratch(ns) if ns else []),
        compiler_params=_params(("arbitrary",), 6 << 20),
    )(qkv, beta, g, do, s_in, vnew, tinv, wsv, *scatter)
    return outs[:3], outs[3:]


def _pool_counts(row_ids, win):
    return jnp.minimum(jnp.maximum(row_ids - LEAD, 0) + 1, win).astype(F32)


def _pool_fwd(p, name):
    t = p.shape[0]
    ext = TE + 16

    def body(p_ref, o_ref, carry):
        i = pl.program_id(0)

        @pl.when(i == 0)
        def _():
            carry[...] = jnp.zeros_like(carry)

        ids = _row_ids(i)
        for gi, win in enumerate(POOL_WINDOWS):
            sl = slice(gi * LANE, (gi + 1) * LANE)
            xv = p_ref[:, sl]
            s = jnp.concatenate([carry[:, sl], xv], axis=0)
            sh = 1
            while sh < win:
                s = s + pltpu.roll(s, sh, 0)
                sh *= 2
            o_ref[:, sl] = (s[16:ext] / _pool_counts(ids, win) - xv).astype(BF16)
            carry[:, sl] = xv[TE - 16:TE]

    return pl.pallas_call(
        body, name=name, grid=(t // TE,), in_specs=[_rows(POOL_W)], out_specs=_rows(POOL_W),
        out_shape=jax.ShapeDtypeStruct((t, POOL_W), BF16), scratch_shapes=[pltpu.VMEM((16, POOL_W), F32)],
        compiler_params=_params(("arbitrary",), TE * POOL_W * 8),
    )(p)


def _pool_bwd(dpo, name):
    t = dpo.shape[0]
    n = t // TE
    ext = TE + 16

    def body(d_ref, o_ref, carry):
        i = pl.program_id(0)

        @pl.when(i == 0)
        def _():
            carry[...] = jnp.zeros_like(carry)

        ids = _row_ids(n - 1 - i)
        for gi, win in enumerate(POOL_WINDOWS):
            sl = slice(gi * LANE, (gi + 1) * LANE)
            dv = d_ref[:, sl]
            rv = dv / _pool_counts(ids, win)
            s = jnp.concatenate([rv, carry[:, sl]], axis=0)
            sh = 1
            while sh < win:
                s = s + pltpu.roll(s, ext - sh, 0)
                sh *= 2
            o_ref[:, sl] = (s[0:TE] - dv).astype(BF16)
            carry[:, sl] = rv[0:16]

    return pl.pallas_call(
        body, name=name, grid=(n,), in_specs=[_rows(POOL_W, n)], out_specs=_rows(POOL_W, n),
        out_shape=jax.ShapeDtypeStruct((t, POOL_W), BF16), scratch_shapes=[pltpu.VMEM((16, POOL_W), F32)],
        compiler_params=_params(("arbitrary",), TE * POOL_W * 8),
    )(dpo)


def _post_fwd(o, z, gate, pm, hn, ps, name):
    t = o.shape[0]

    def body(o_ref, z_ref, g_ref, pm_ref, hn_ref, ps_ref, y_ref):
        for h in range(H):
            sl = slice(h * DH, (h + 1) * DH)
            ov = o_ref[:, sl]
            zv = z_ref[:, sl].astype(F32)
            r = lax.rsqrt(jnp.mean(ov * ov, axis=-1, keepdims=True) + EPS)
            ya = ov * r * hn_ref[...] * (zv * _sigmoid(zv))
            ga = _sigmoid(g_ref[:, sl].astype(F32))
            gb = _sigmoid(g_ref[:, D + h * DH:D + (h + 1) * DH].astype(F32))
            y_ref[:, sl] = (ga * ya + gb * (pm_ref[:, sl] * ps_ref[:, sl])).astype(BF16)

    return pl.pallas_call(
        body, name=name, grid=(t // TE,),
        in_specs=[_rows(D), _rows(D), _rows(2 * D), _rows(D), _whole((1, DH)), _whole((1, D))], out_specs=_rows(D),
        out_shape=jax.ShapeDtypeStruct((t, D), BF16), compiler_params=_params(("parallel",), TE * D * 16),
    )(o, z, gate, pm, hn, ps)


def _post_bwd(dy, o, z, gate, pm, hn, ps, name):
    t = o.shape[0]

    def body(dy_ref, o_ref, z_ref, g_ref, pm_ref, hn_ref, ps_ref, do_ref, dz_ref, dgate_ref, dpm_ref, dhn_ref, dps_ref):
        i = pl.program_id(0)

        @pl.when(i == 0)
        def _():
            dhn_ref[...] = jnp.zeros_like(dhn_ref)
            dps_ref[...] = jnp.zeros_like(dps_ref)

        hnv = hn_ref[...]
        dhn = jnp.zeros((1, DH), F32)
        for h in range(H):
            sl = slice(h * DH, (h + 1) * DH)
            slb = slice(D + h * DH, D + (h + 1) * DH)
            dyv = dy_ref[:, sl]
            ov = o_ref[:, sl]
            zv = z_ref[:, sl].astype(F32)
            r = lax.rsqrt(jnp.mean(ov * ov, axis=-1, keepdims=True) + EPS)
            sz = _sigmoid(zv)
            silu = zv * sz
            on = ov * r
            ya = on * hnv * silu
            ga = _sigmoid(g_ref[:, sl].astype(F32))
            gb = _sigmoid(g_ref[:, slb].astype(F32))
            pmv = pm_ref[:, sl]
            psv = ps_ref[:, sl]
            dya = dyv * ga
            dyb = dyv * gb
            dgate_ref[:, sl] = (dyv * ya * ga * (1.0 - ga)).astype(BF16)
            dgate_ref[:, slb] = (dyv * (pmv * psv) * gb * (1.0 - gb)).astype(BF16)
            tt = dya * hnv * silu
            do_ref[:, sl] = r * tt - ov * (r * r * r) * jnp.mean(ov * tt, axis=-1, keepdims=True)
            dz_ref[:, sl] = (dya * on * hnv * (sz * (1.0 + zv * (1.0 - sz)))).astype(BF16)
            dhn = dhn + jnp.sum(dya * on * silu, axis=0, keepdims=True)
            dps_ref[:, sl] += jnp.sum(dyb * pmv, axis=0, keepdims=True)
            dpm_ref[:, sl] = (dyb * psv).astype(BF16)
        dhn_ref[...] += dhn

    return pl.pallas_call(
        body, name=name, grid=(t // TE,),
        in_specs=[_rows(D), _rows(D), _rows(D), _rows(2 * D), _rows(D), _whole((1, DH)), _whole((1, D))],
        out_specs=[_rows(D), _rows(D), _rows(2 * D), _rows(D), _whole((1, DH)), _whole((1, D))],
        out_shape=[jax.ShapeDtypeStruct((t, D), F32), jax.ShapeDtypeStruct((t, D), BF16),
                   jax.ShapeDtypeStruct((t, 2 * D), BF16), jax.ShapeDtypeStruct((t, D), BF16),
                   jax.ShapeDtypeStruct((1, DH), F32), jax.ShapeDtypeStruct((1, D), F32)],
        compiler_params=_params(("arbitrary",), TE * D * 28),
    )(dy, o, z, gate, pm, hn, ps)


_FB_COLS = [(c, min(c + LANE, FB)) for c in range(0, FB, LANE)]


def _mlp_act_fwd(hid, cw, name):
    t = hid.shape[1]
    n = t // TE

    def body(hg_ref, hv_ref, wg_ref, wv_ref, a_ref, xg, xv):
        i = pl.program_id(1)
        _stage_history(xg, i)
        _stage_history(xv, i)
        for c, (c0, c1) in enumerate(_FB_COLS):
            sl, wd = slice(c0, c1), c1 - c0
            xg[c, SUB:SUB + TE, 0:wd] = hg_ref[:, sl].astype(F32)
            xv[c, SUB:SUB + TE, 0:wd] = hv_ref[:, sl].astype(F32)
            gg = _conv(_taps(xg, c, wd, SUB, TE, 3), wg_ref[:, sl])
            vv = _conv(_taps(xv, c, wd, SUB, TE, 3), wv_ref[:, sl])
            a_ref[:, sl] = (gg * _sigmoid(gg) * vv).astype(BF16)

    hspec = lambda off: pl.BlockSpec((None, TE, FB), lambda p, i: (p + off, i, 0))
    wspec = lambda off: pl.BlockSpec((None, 3, FB), lambda p, i: (p + off, 0, 0))
    return pl.pallas_call(
        body, name=name, grid=(4, n), in_specs=[hspec(0), hspec(4), wspec(0), wspec(4)],
        out_specs=pl.BlockSpec((None, TE, FB), lambda p, i: (p, i, 0)),
        out_shape=jax.ShapeDtypeStruct((4, t, FB), BF16),
        scratch_shapes=[_seq_scratch(FB), _seq_scratch(FB)],
        compiler_params=_params(("parallel", "arbitrary"), TE * FB * 12),
    )(hid, hid, cw, cw)


def _mlp_act_bwd(da, hid, cw, name):
    t = hid.shape[1]
    n = t // TE
    hb = TE // 16

    def body(da_ref, hg_ref, hv_ref, pg_ref, pv_ref, wg_ref, wv_ref, dhg_ref, dhv_ref, dwg_ref, dwv_ref, xg, xv, dg, dv):
        i = pl.program_id(1)

        @pl.when(i == 0)
        def _():
            dwg_ref[...] = jnp.zeros_like(dwg_ref)
            dwv_ref[...] = jnp.zeros_like(dwv_ref)

        _stage_future(dg, i)
        _stage_future(dv, i)
        first_tile = i == n - 1
        for c, (c0, c1) in enumerate(_FB_COLS):
            sl, wd = slice(c0, c1), c1 - c0
            for scr, p_ref, h_ref in ((xg, pg_ref, hg_ref), (xv, pv_ref, hv_ref)):
                scr[c, 0:SUB, 0:wd] = jnp.where(first_tile, 0.0, p_ref[SUB:2 * SUB, sl].astype(F32))
                scr[c, SUB:SUB + TE, 0:wd] = h_ref[:, sl].astype(F32)
            wg = wg_ref[:, sl]
            wv = wv_ref[:, sl]
            tg = _taps(xg, c, wd, SUB, TE, 3)
            tv = _taps(xv, c, wd, SUB, TE, 3)
            gg = _conv(tg, wg)
            vv = _conv(tv, wv)
            sg = _sigmoid(gg)
            dav = da_ref[:, sl].astype(F32)
            dgg = dav * vv * (sg * (1.0 + gg * (1.0 - sg)))
            dvv = dav * (gg * sg)
            for dc, tp, w, scr, dh_ref, dw_ref in ((dgg, tg, wg, dg, dhg_ref, dwg_ref), (dvv, tv, wv, dv, dhv_ref, dwv_ref)):
                scr[c, 0:TE, 0:wd] = dc
                dh_ref[:, sl] = _conv_t(scr, c, wd, w).astype(BF16)
                dw_ref[:, sl] += jnp.concatenate([jnp.sum(tp[j] * dc, axis=0, keepdims=True) for j in range(3)], axis=0)

    rev = lambda off: pl.BlockSpec((None, TE, FB), lambda p, i: (p + off, n - 1 - i, 0))
    halo = lambda off: pl.BlockSpec((None, 16, FB), lambda p, i: (p + off, jnp.maximum((n - 1 - i) * hb - 1, 0), 0))
    wspec = lambda off: pl.BlockSpec((None, 3, FB), lambda p, i: (p + off, 0, 0))
    dwspec = pl.BlockSpec((None, 3, FB), lambda p, i: (p, 0, 0))
    return pl.pallas_call(
        body, name=name, grid=(4, n), in_specs=[rev(0), rev(0), rev(4), halo(0), halo(4), wspec(0), wspec(4)],
        out_specs=[rev(0), rev(0), dwspec, dwspec],
        out_shape=[jax.ShapeDtypeStruct((4, t, FB), BF16), jax.ShapeDtypeStruct((4, t, FB), BF16),
                   jax.ShapeDtypeStruct((4, 3, FB), F32), jax.ShapeDtypeStruct((4, 3, FB), F32)],
        scratch_shapes=[_seq_scratch(FB)] * 4,
        compiler_params=_params(("parallel", "arbitrary"), TE * FB * 24),
    )(da, hid, hid, hid, hid, cw, cw)


def _adamw(land, w, m, v, name):
    r, c = w.shape
    tr = r
    for cand in (128, 64, 32, 16):
        if r % cand == 0:
            tr = cand
            break
    c1 = 1.0 - ADAM_B1 ** ADAM_STEP
    c2 = 1.0 - ADAM_B2 ** ADAM_STEP

    def body(l_ref, w_ref, m_ref, v_ref, g_out, d_out, m_out, v_out):
        g = l_ref[0].astype(F32)
        for i in range(1, NDEV):
            g = g + l_ref[i].astype(F32)
        mn = ADAM_B1 * m_ref[...] + (1.0 - ADAM_B1) * g
        vn = ADAM_B2 * v_ref[...] + (1.0 - ADAM_B2) * (g * g)
        g_out[...] = g
        m_out[...] = mn
        v_out[...] = vn
        d_out[...] = -ADAM_LR * ((mn / c1) / (jnp.sqrt(vn / c2) + ADAM_EPS) + ADAM_WD * w_ref[...])

    spec = pl.BlockSpec((tr, c), lambda i: (i, 0))
    shp = jax.ShapeDtypeStruct((r, c), F32)
    return pl.pallas_call(
        body, name=name, grid=(r // tr,), in_specs=[pl.BlockSpec((NDEV, tr, c), lambda i: (0, i, 0)), spec, spec, spec],
        out_specs=[spec] * 4, out_shape=[shp] * 4, compiler_params=_params(("parallel",), 15 * tr * c * 4),
    )(land, w, m, v)


def _layer_fwd(h, p, tag, gather=(), finish=None):
    u = _rmsnorm_fwd(h, p["norm_mix"], f"norm_mix_{tag}")
    qkv_pre = _mm(u, p["w_qkv"], BF16, f"proj_qkv_{tag}")
    z = _mm(u, p["w_z"], BF16, f"proj_z_{tag}")
    ba = _mm(u, p["w_ba"], F32, f"proj_ba_{tag}")
    pool_in = _mm(u, p["w_pl"], F32, f"proj_pool_{tag}")
    gate = _mm(u, p["w_gate"], BF16, f"proj_gate_{tag}")
    qkv = _gdn_pre_fwd(qkv_pre, p["conv_qkv"], f"gdn_pre_{tag}")
    beta, g = _gates_fwd(ba, p["a_row"], p["dt_row"], f"gates_{tag}")
    (o, s_in, vnew, tinv, wsv), gathered = _gdn_fwd(qkv, beta, g, f"gdn_{tag}", gather)
    if finish is not None:
        p = {**p, **finish(gathered)}
    pooled = _pool_fwd(pool_in, f"pool_{tag}")
    pm = _mm_cols(pooled, p["w_pool"], F32, f"pool_mm_{tag}", _NN)
    y = _post_fwd(o, z, gate, pm, p["head_norm"], p["pool_scale"], f"post_{tag}")
    h1 = _mm(y, p["w_out"], F32, f"out_proj_{tag}", res=h)
    u2 = _rmsnorm_fwd(h1, p["norm_ffn"], f"norm_ffn_{tag}")
    hid = _mm_up(u2, p["w_up"], f"up_proj_{tag}")
    act = _mlp_act_fwd(hid, p["conv_ffn"], f"mlp_act_{tag}")
    h2 = _mm_blocks_red(act, p["w_down"], f"down_proj_{tag}", _NN, res=h1)
    saved = dict(h=h, u=u, qkv_pre=qkv_pre, z=z, ba=ba, gate=gate, qkv=qkv, beta=beta, g=g, o=o, s_in=s_in, vnew=vnew,
                 tinv=tinv, wsv=wsv, pooled=pooled, pm=pm, y=y, h1=h1, u2=u2, hid=hid, act=act)
    return h2, saved, gathered, p


def _layer_bwd(dh, dh_b, p, s, tag, scatter=()):
    gr = {}
    da = _mm_to_blocks(dh_b, p["w_down"], f"d_act_{tag}")
    gr["w_down"] = _mm_tn_blocks(s["act"], dh_b, f"dw_down_{tag}", True, False)
    dhg, dhv, dwg, dwv = _mlp_act_bwd(da, s["hid"], p["conv_ffn"], f"mlp_act_bwd_{tag}")
    gr["conv_ffn"] = jnp.concatenate([dwg, dwv], axis=0)
    w_up = p["w_up"]
    du2 = _mm_blocks_red(dhg, w_up[:4], f"d_u2g_{tag}", _NT)
    du2 = _mm_blocks_red(dhv, w_up[4:], f"d_u2v_{tag}", _NT, res=du2)
    gr["w_up"] = jnp.concatenate([_mm_tn_blocks(s["u2"], dhg, f"dw_upg_{tag}", False, True),
                                  _mm_tn_blocks(s["u2"], dhv, f"dw_upv_{tag}", False, True)], axis=0)
    dh1, dh1_b, gr["norm_ffn"] = _rmsnorm_bwd(s["h1"], du2, dh, p["norm_ffn"], f"norm_ffn_bwd_{tag}")
    dy = _mm(dh1_b, p["w_out"], F32, f"d_y_{tag}", dims=_NT)
    gr["w_out"] = _mm_tn(s["y"], dh1_b, f"dw_out_{tag}")
    do, dz, dgate, dpm, gr["head_norm"], gr["pool_scale"] = _post_bwd(
        dy, s["o"], s["z"], s["gate"], s["pm"], p["head_norm"], p["pool_scale"], f"post_bwd_{tag}")
    dpooled = _mm_cols(dpm, p["w_pool"], F32, f"d_pooled_{tag}", _NT)
    gr["w_pool"] = _mm_tn_cols(s["pooled"], dpm, 4, f"dw_pool_{tag}")
    dpool_in = _pool_bwd(dpooled, f"pool_bwd_{tag}")
    own = (gr["w_up"].astype(BF16), gr["w_down"].reshape(NDEV, -1, D).astype(BF16))
    (dqkv, dbeta, dg), landed = _gdn_bwd(s["qkv"], s["beta"], s["g"], do, s["s_in"], s["vnew"], s["tinv"], s["wsv"],
                                         f"gdn_bwd_{tag}", own + tuple(scatter))
    dba, gr["a_log"], gr["dt_bias"] = _gates_bwd(s["ba"], p["a_row"], p["dt_row"], dbeta, dg, f"gates_bwd_{tag}")
    dqkv_pre, gr["conv_qkv"] = _gdn_pre_bwd(s["qkv_pre"], p["conv_qkv"], dqkv, f"gdn_pre_bwd_{tag}")
    du = None
    dws = []
    for nm, dseg, wseg in (("qkv", dqkv_pre, p["w_qkv"]), ("z", dz, p["w_z"]), ("ba", dba, p["w_ba"]),
                           ("pool", dpool_in, p["w_pl"]), ("gate", dgate, p["w_gate"])):
        du = _mm(dseg, wseg, F32, f"d_u_{nm}_{tag}", res=du, dims=_NT)
        dws.append(_mm_tn(s["u"], dseg, f"dw_{nm}_{tag}"))
    gr["w_in"] = jnp.concatenate([dws[0], dws[1], dws[2][:, 0:H], dws[2][:, LANE:LANE + H], dws[3], dws[4]], axis=1)
    dh0, dh0_b, gr["norm_mix"] = _rmsnorm_bwd(s["h"], du, dh1, p["norm_mix"], f"norm_mix_bwd_{tag}")
    return dh0, dh0_b, gr, landed


def _pad_lanes(v8):
    return jnp.pad(v8.reshape(1, H), ((0, 0), (0, LANE - H)))


def _pack(parts, rows):
    flat = jnp.concatenate([q.reshape(-1) for q in parts])
    return jnp.pad(flat, (0, rows * LANE - flat.shape[0])).reshape(rows, LANE)


def _unpack(packed, shapes):
    flat = packed.reshape(-1)
    out, off = [], 0
    for shp in shapes:
        n = 1
        for s_ in shp:
            n *= s_
        out.append(flat[off:off + n].reshape(shp))
        off += n
    return out


SMALL_ROWS = 336
REPL_ROWS = 64


def kernel(x, meta_tokens, norm_mix, w_in, conv_qkv, a_log, dt_bias, head_norm, w_pool, pool_scale, w_out, norm_ffn, w_up, conv_ffn, w_down, norm_final, loss_target, m_meta_tokens, m_norm_mix, m_w_in, m_conv_qkv, m_a_log, m_dt_bias, m_head_norm, m_w_pool, m_pool_scale, m_w_out, m_norm_ffn, m_w_up, m_conv_ffn, m_w_down, m_norm_final, v_meta_tokens, v_norm_mix, v_w_in, v_conv_qkv, v_a_log, v_dt_bias, v_head_norm, v_w_pool, v_pool_scale, v_w_out, v_norm_ffn, v_w_up, v_conv_ffn, v_w_down, v_norm_final):
    seq = x.shape[1]
    t = ROW0 + seq
    assert t % TE == 0 and t % (MM_TILES * 16) == 0 and t % CH == 0
    depth = w_in.shape[0]
    assert depth == 2
    cin = w_in.shape[2]

    def mixer_params(l, g_in, conv_q, conv_f, wp):
        wf = jnp.transpose(g_in, (1, 0, 2)).reshape(D, NDEV * cin)
        zpad = jnp.zeros((D, LANE - H), BF16)
        return dict(
            w_qkv=wf[:, 0:QKV], w_z=wf[:, QKV:QKV + D],
            w_ba=jnp.concatenate([wf[:, 4096:4104], zpad, wf[:, 4104:4112], zpad], axis=1),
            w_pl=wf[:, 4112:4624], w_gate=wf[:, 4624:6672], conv_qkv=conv_q, conv_ffn=conv_f, w_pool=wp,
            norm_mix=norm_mix[l].reshape(1, D), norm_ffn=norm_ffn[l].reshape(1, D),
            pool_scale=pool_scale[l].reshape(1, D), head_norm=head_norm[l].reshape(1, DH),
            a_row=_pad_lanes(a_log[l]), dt_row=_pad_lanes(dt_bias[l]))

    def late_params(g_up, g_out, g_down):
        return dict(w_out=g_out.reshape(D, D), w_up=g_up, w_down=g_down.reshape(4, FB, D))

    small_shapes = [conv_qkv.shape, conv_ffn.shape, w_pool.shape, meta_tokens.shape]
    small = _pack([conv_qkv, conv_ffn, w_pool, meta_tokens], SMALL_ROWS)
    w_in_b, w_up_b, w_out_b, w_down_b = w_in.astype(BF16), w_up.astype(BF16), w_out.astype(BF16), w_down.astype(BF16)
    g_in0, g_small = _gather([w_in_b[0], small], "gather_first")
    smalls = [_unpack(g_small[i], small_shapes) for i in range(NDEV)]
    conv_qkv_full = jnp.concatenate([sm[0] for sm in smalls], axis=2)
    conv_ffn_blk = jnp.stack([sm[1] for sm in smalls], axis=1)
    w_pool_full = jnp.concatenate([sm[2] for sm in smalls], axis=3).astype(BF16)
    meta_full = jnp.concatenate([sm[3] for sm in smalls], axis=1)

    h = jnp.concatenate([jnp.zeros((LEAD, D), F32), meta_full, x[0]], axis=0)
    p0 = mixer_params(0, g_in0, conv_qkv_full[0], conv_ffn_blk[0], w_pool_full[0])
    h, sv0, rest, p0 = _layer_fwd(
        h, p0, "l0", (w_up_b[0], w_out_b[0], w_down_b[0], w_in_b[1], w_up_b[1], w_out_b[1], w_down_b[1]),
        lambda got: late_params(*got[:3]))
    p1 = {**mixer_params(1, rest[3], conv_qkv_full[1], conv_ffn_blk[1], w_pool_full[1]), **late_params(*rest[4:])}
    h, sv1, _, _ = _layer_fwd(h, p1, "l1")
    layers = [p0, p1]
    saved = [sv0, sv1]
    target = jnp.concatenate([jnp.zeros((ROW0, D), F32), loss_target[0]], axis=0)
    dh, dh_b, d_norm_final, loss_row = _loss_bwd(h, target, norm_final.reshape(1, D), "loss")

    def mixer_blocks(gr):
        return (jnp.transpose(gr["w_in"].reshape(D, NDEV, cin), (1, 0, 2)).astype(BF16),
                gr["w_out"].reshape(NDEV, D // NDEV, D).astype(BF16))

    grads = [None] * depth
    dh, dh_b, grads[1], (l_up1, l_down1) = _layer_bwd(dh, dh_b, layers[1], saved[1], "l1")
    dh, dh_b, grads[0], (l_up0, l_down0, l_in1, l_out1) = _layer_bwd(dh, dh_b, layers[0], saved[0], "l0", mixer_blocks(grads[1]))
    grad_x = dh[ROW0:].reshape(1, seq, D)
    d_meta = dh[LEAD:ROW0]

    stk = lambda name: jnp.stack([grads[l][name] for l in range(depth)], axis=0)
    cq = conv_qkv.shape[2]
    pw = w_pool.shape[3]
    s_cq = jnp.transpose(stk("conv_qkv").reshape(depth, 4, NDEV, cq), (2, 0, 1, 3))
    s_cf = jnp.transpose(stk("conv_ffn"), (1, 0, 2, 3))
    s_wp = jnp.transpose(stk("w_pool").reshape(depth, 4, DH, NDEV, pw), (3, 0, 1, 2, 4))
    s_mt = jnp.transpose(d_meta.reshape(N_META, NDEV, D // NDEV), (1, 0, 2))
    b_small = jnp.stack([_pack([s_cq[i], s_cf[i], s_wp[i], s_mt[i]], SMALL_ROWS) for i in range(NDEV)], axis=0)
    l_in0, l_out0, l_small = _scatter([*mixer_blocks(grads[0]), b_small], "exchange_last")

    def upd(lands, w, m, v, name):
        res = [_adamw(land, w[l], m[l], v[l], f"adamw_{name}_l{l}") for l, land in enumerate(lands)]
        return [jnp.stack([res[0][kind], res[1][kind]], axis=0) for kind in range(4)]

    r_in = upd((l_in0, l_in1), w_in, m_w_in, v_w_in, "w_in")
    r_up = upd((l_up0, l_up1), w_up, m_w_up, v_w_up, "w_up")
    r_out = upd((l_out0, l_out1), w_out, m_w_out, v_w_out, "w_out")
    r_down = upd((l_down0, l_down1), w_down, m_w_down, v_w_down, "w_down")
    r_small = _adamw(l_small, small, _pack([m_conv_qkv, m_conv_ffn, m_w_pool, m_meta_tokens], SMALL_ROWS),
                     _pack([v_conv_qkv, v_conv_ffn, v_w_pool, v_meta_tokens], SMALL_ROWS), "adamw_small")
    r_small = [_unpack(o_, small_shapes) for o_ in r_small]

    repl_shapes = [norm_mix.shape, a_log.shape, dt_bias.shape, head_norm.shape, pool_scale.shape, norm_ffn.shape,
                   norm_final.shape, (1,)]
    rp = lambda name, n: jnp.stack([grads[l][name][0, :n] for l in range(depth)], axis=0)
    part = _pack([rp("norm_mix", D), rp("a_log", H), rp("dt_bias", H), rp("head_norm", DH), rp("pool_scale", D),
                  rp("norm_ffn", D), d_norm_final[0], loss_row[0, 0:1]], REPL_ROWS)
    (l_repl,) = _gather([part], "gather_replicated")
    zero1 = jnp.zeros((1,), F32)
    r_repl = _adamw(l_repl, _pack([norm_mix, a_log, dt_bias, head_norm, pool_scale, norm_ffn, norm_final, zero1], REPL_ROWS),
                    _pack([m_norm_mix, m_a_log, m_dt_bias, m_head_norm, m_pool_scale, m_norm_ffn, m_norm_final, zero1], REPL_ROWS),
                    _pack([v_norm_mix, v_a_log, v_dt_bias, v_head_norm, v_pool_scale, v_norm_ffn, v_norm_final, zero1], REPL_ROWS),
                    "adamw_replicated")
    r_repl = [_unpack(o_, repl_shapes) for o_ in r_repl]
    loss = r_repl[0][7].reshape(())

    def leaf(kind):
        sm, rr = r_small[kind], r_repl[kind]
        return [sm[3], rr[0], r_in[kind], sm[0], rr[1], rr[2], rr[3], sm[2], rr[4], r_out[kind], rr[5], r_up[kind],
                sm[1], r_down[kind], rr[6]]

    return (loss, grad_x, *leaf(0), *leaf(1), *leaf(2), *leaf(3))
```

```python
import jax
import jax.numpy as jnp
from jax import lax
from jax.experimental import pallas as pl
from jax.experimental.pallas import tpu as pltpu

F32 = jnp.float32
BF16 = jnp.bfloat16
HI = lax.Precision.HIGHEST
MESH = pl.DeviceIdType.MESH

D = 1024
H = 8
DH = 128
CH = 64
N_META = 16
LEAD = 48
ROW0 = LEAD + N_META
QKV = 3 * D
POOL_W = 512
POOL_WINDOWS = (2, 4, 8, 16)
FB = 704
NDEV = 8
EPS = 1e-6
MM_TILES = 12
TE = 192
LANE = 128
SUB = 8
VMEM_CAP = 56 << 20

ADAM_LR, ADAM_B1, ADAM_B2, ADAM_EPS, ADAM_WD, ADAM_STEP = 0.001, 0.9, 0.999, 1e-08, 0.01, 10

_NN = (((1,), (0,)), ((), ()))
_NT = (((1,), (1,)), ((), ()))
_TN = (((0,), (0,)), ((), ()))


def _dot(a, b, dims=_NN, precision=None):
    return lax.dot_general(a, b, dims, precision=precision, preferred_element_type=F32)


def _bdot(a, b, dims=_NN):
    return _dot(a.astype(BF16), b.astype(BF16), dims)


def _nbytes(shape, dtype):
    n = 1
    for s in shape:
        n *= s
    return n * jnp.dtype(dtype).itemsize


def _params(sem, block_bytes):
    limit = min(VMEM_CAP, 2 * block_bytes + (20 << 20))
    return pltpu.CompilerParams(dimension_semantics=sem, vmem_limit_bytes=limit)


def _sigmoid(x):
    return 1.0 / (1.0 + jnp.exp(-x))


def _col_tile(n):
    for t in (1024, 512, 256, 128):
        if n % t == 0:
            return t
    return n


def _matmul(a, b, *, dims, grid, a_spec, b_spec, o_spec, out_shape, name, red_axis=None, res=None):
    def body(*refs):
        if res is None:
            a_ref, b_ref, o_ref = refs
        else:
            a_ref, b_ref, r_ref, o_ref = refs
        part = _dot(a_ref[...], b_ref[...], dims)
        if red_axis is None:
            if res is not None:
                part = part + r_ref[...]
            o_ref[...] = part.astype(o_ref.dtype)
        else:
            r = pl.program_id(red_axis)

            @pl.when(r == 0)
            def _():
                o_ref[...] = part + r_ref[...] if res is not None else part

            @pl.when(r > 0)
            def _():
                o_ref[...] += part

    def blk(spec, arr):
        return _nbytes([s for s in spec.block_shape if s is not None], arr.dtype)

    ins = [a, b] + ([res] if res is not None else [])
    specs = [a_spec, b_spec] + ([o_spec] if res is not None else [])
    nb = blk(a_spec, a) + blk(b_spec, b) + 2 * _nbytes([s for s in o_spec.block_shape if s is not None], F32)
    sem = tuple("arbitrary" if i == red_axis else "parallel" for i in range(len(grid)))
    return pl.pallas_call(
        body, name=name, grid=grid, in_specs=specs, out_specs=o_spec, out_shape=out_shape,
        compiler_params=_params(sem, nb),
    )(*ins)


def _row_tiles(m, row_bytes, fixed_bytes, temp_row_bytes=0):
    for nt in (MM_TILES // 2, MM_TILES):
        tm = m // nt
        if 2 * (row_bytes * tm + fixed_bytes) + temp_row_bytes * tm <= VMEM_CAP - (10 << 20):
            return nt
    return MM_TILES


def _mm(a, b, out_dtype, name, res=None, dims=_NN):
    m, k = a.shape
    n = b.shape[1] if dims == _NN else b.shape[0]
    tn = _col_tile(n)
    nt = _row_tiles(m, 2 * k + tn * (jnp.dtype(out_dtype).itemsize + (4 if res is not None else 0)), 2 * k * tn, 4 * tn)
    tm = m // nt
    if dims == _NN:
        b_spec = pl.BlockSpec((k, tn), lambda j, i: (0, j))
    else:
        b_spec = pl.BlockSpec((tn, k), lambda j, i: (j, 0))
    return _matmul(
        a, b, dims=dims, grid=(n // tn, nt), a_spec=pl.BlockSpec((tm, k), lambda j, i: (i, 0)), b_spec=b_spec,
        o_spec=pl.BlockSpec((tm, tn), lambda j, i: (i, j)), out_shape=jax.ShapeDtypeStruct((m, n), out_dtype),
        name=name, res=res)


def _mm_tn(a, g, name):
    m, k = a.shape
    n = g.shape[1]
    tn = _col_tile(n)
    nt = _row_tiles(m, 2 * k + 2 * tn, 4 * k * tn)
    tm = m // nt
    return _matmul(
        a, g, dims=_TN, grid=(n // tn, nt), red_axis=1, a_spec=pl.BlockSpec((tm, k), lambda j, i: (i, 0)),
        b_spec=pl.BlockSpec((tm, tn), lambda j, i: (i, j)), o_spec=pl.BlockSpec((k, tn), lambda j, i: (0, j)),
        out_shape=jax.ShapeDtypeStruct((k, n), F32), name=name)


def _mm_up(u, w_up, name):
    t = u.shape[0]
    g = w_up.shape[0]
    nt = _row_tiles(t, 2 * D + 2 * FB, 2 * D * FB, 4 * FB)
    tm = t // nt
    return _matmul(
        u, w_up, dims=_NN, grid=(g, nt), a_spec=pl.BlockSpec((tm, D), lambda g_, i: (i, 0)),
        b_spec=pl.BlockSpec((None, D, FB), lambda g_, i: (g_, 0, 0)),
        o_spec=pl.BlockSpec((None, tm, FB), lambda g_, i: (g_, i, 0)),
        out_shape=jax.ShapeDtypeStruct((g, t, FB), BF16), name=name)


def _mm_blocks_red(a, b, name, dims, res=None):
    g, t, k = a.shape
    n = b.shape[2] if dims == _NN else b.shape[1]
    nt = _row_tiles(t, 2 * k + n * (8 if res is not None else 4), 2 * k * n, 4 * n)
    tm = t // nt
    return _matmul(
        a, b, dims=dims, grid=(nt, g), red_axis=1, a_spec=pl.BlockSpec((None, tm, k), lambda i, g_: (g_, i, 0)),
        b_spec=pl.BlockSpec((None,) + b.shape[1:], lambda i, g_: (g_, 0, 0)),
        o_spec=pl.BlockSpec((tm, n), lambda i, g_: (i, 0)), out_shape=jax.ShapeDtypeStruct((t, n), F32),
        name=name, res=res)


def _mm_to_blocks(a, b, name):
    t, k = a.shape
    g, n, _ = b.shape
    nt = _row_tiles(t, 2 * k + 2 * n, 2 * k * n, 4 * n)
    tm = t // nt
    return _matmul(
        a, b, dims=_NT, grid=(g, nt), a_spec=pl.BlockSpec((tm, k), lambda g_, i: (i, 0)),
        b_spec=pl.BlockSpec((None, n, k), lambda g_, i: (g_, 0, 0)),
        o_spec=pl.BlockSpec((None, tm, n), lambda g_, i: (g_, i, 0)),
        out_shape=jax.ShapeDtypeStruct((g, t, n), BF16), name=name)


def _mm_tn_blocks(a, g, name, a_blocked, g_blocked):
    nb = a.shape[0] if a_blocked else g.shape[0]
    t = a.shape[-2]
    k, n = a.shape[-1], g.shape[-1]
    nt = _row_tiles(t, 2 * k + 2 * n, 4 * k * n)
    tm = t // nt
    a_spec = (pl.BlockSpec((None, tm, k), lambda g_, i: (g_, i, 0)) if a_blocked
              else pl.BlockSpec((tm, k), lambda g_, i: (i, 0)))
    g_spec = (pl.BlockSpec((None, tm, n), lambda g_, i: (g_, i, 0)) if g_blocked
              else pl.BlockSpec((tm, n), lambda g_, i: (i, 0)))
    return _matmul(
        a, g, dims=_TN, grid=(nb, nt), red_axis=1, a_spec=a_spec, b_spec=g_spec,
        o_spec=pl.BlockSpec((None, k, n), lambda g_, i: (g_, 0, 0)),
        out_shape=jax.ShapeDtypeStruct((nb, k, n), F32), name=name)


def _mm_cols(a, b, out_dtype, name, dims):
    t = a.shape[0]
    g = b.shape[0]
    ka = a.shape[1] // g
    n = b.shape[2] if dims == _NN else b.shape[1]
    tm = t // MM_TILES
    return _matmul(
        a, b, dims=dims, grid=(g, MM_TILES), a_spec=pl.BlockSpec((tm, ka), lambda g_, i: (i, g_)),
        b_spec=pl.BlockSpec((None,) + b.shape[1:], lambda g_, i: (g_, 0, 0)),
        o_spec=pl.BlockSpec((tm, n), lambda g_, i: (i, g_)), out_shape=jax.ShapeDtypeStruct((t, g * n), out_dtype),
        name=name)


def _mm_tn_cols(a, g, nblk, name):
    t = a.shape[0]
    ka, n = a.shape[1] // nblk, g.shape[1] // nblk
    tm = t // MM_TILES
    return _matmul(
        a, g, dims=_TN, grid=(nblk, MM_TILES), red_axis=1, a_spec=pl.BlockSpec((tm, ka), lambda g_, i: (i, g_)),
        b_spec=pl.BlockSpec((tm, n), lambda g_, i: (i, g_)), o_spec=pl.BlockSpec((None, ka, n), lambda g_, i: (g_, 0, 0)),
        out_shape=jax.ShapeDtypeStruct((nblk, ka, n), F32), name=name)


def _rows(cols, n=None):
    if n is None:
        return pl.BlockSpec((TE, cols), lambda i: (i, 0))
    return pl.BlockSpec((TE, cols), lambda i: (n - 1 - i, 0))


def _whole(shape):
    return pl.BlockSpec(shape, lambda *_: (0,) * len(shape))


def _row_ids(i, rows=TE):
    return i * rows + lax.broadcasted_iota(jnp.int32, (rows, 1), 0)


def _rmsnorm_fwd(h, gain, name):
    t = h.shape[0]

    def body(h_ref, g_ref, u_ref):
        x = h_ref[...]
        r = lax.rsqrt(jnp.mean(x * x, axis=-1, keepdims=True) + EPS)
        u_ref[...] = (x * r * g_ref[...]).astype(BF16)

    return pl.pallas_call(
        body, name=name, grid=(t // TE,), in_specs=[_rows(D), _whole((1, D))], out_specs=_rows(D),
        out_shape=jax.ShapeDtypeStruct((t, D), BF16), compiler_params=_params(("parallel",), 3 * TE * D * 4),
    )(h, gain)


def _rmsnorm_bwd(x, du, dres, gain, name):
    t = x.shape[0]

    def body(x_ref, du_ref, dr_ref, g_ref, dx_ref, dxb_ref, dg_ref):
        i = pl.program_id(0)
        xv = x_ref[...]
        r = lax.rsqrt(jnp.mean(xv * xv, axis=-1, keepdims=True) + EPS)
        gdy = du_ref[...] * g_ref[...]
        dx = dr_ref[...] + r * gdy - xv * (r * r * r) * jnp.mean(xv * gdy, axis=-1, keepdims=True)
        dx = jnp.where(_row_ids(i) >= LEAD, dx, 0.0)
        dx_ref[...] = dx
        dxb_ref[...] = dx.astype(BF16)
        part = jnp.sum(du_ref[...] * xv * r, axis=0, keepdims=True)

        @pl.when(i == 0)
        def _():
            dg_ref[...] = part

        @pl.when(i > 0)
        def _():
            dg_ref[...] += part

    return pl.pallas_call(
        body, name=name, grid=(t // TE,), in_specs=[_rows(D), _rows(D), _rows(D), _whole((1, D))],
        out_specs=[_rows(D), _rows(D), _whole((1, D))],
        out_shape=[jax.ShapeDtypeStruct((t, D), F32), jax.ShapeDtypeStruct((t, D), BF16),
                   jax.ShapeDtypeStruct((1, D), F32)],
        compiler_params=_params(("arbitrary",), 5 * TE * D * 4),
    )(x, du, dres, gain)


def _loss_bwd(h, target, gain, name):
    t = h.shape[0]

    def body(h_ref, t_ref, g_ref, dx_ref, dxb_ref, dg_ref, loss_ref):
        i = pl.program_id(0)
        xv = h_ref[...]
        gain_v = g_ref[...]
        r = lax.rsqrt(jnp.mean(xv * xv, axis=-1, keepdims=True) + EPS)
        real = _row_ids(i) >= ROW0
        err = jnp.where(real, xv * r * gain_v - t_ref[...], 0.0)
        dy = err * (1.0 / D)
        gdy = dy * gain_v
        dx = r * gdy - xv * (r * r * r) * jnp.mean(xv * gdy, axis=-1, keepdims=True)
        dx_ref[...] = dx
        dxb_ref[...] = dx.astype(BF16)
        dgp = jnp.sum(dy * xv * r, axis=0, keepdims=True)
        lp = 0.5 * jnp.sum(jnp.mean(err * err, axis=-1, keepdims=True), axis=0, keepdims=True)

        @pl.when(i == 0)
        def _():
            dg_ref[...] = dgp
            loss_ref[...] = jnp.broadcast_to(lp, (1, LANE))

        @pl.when(i > 0)
        def _():
            dg_ref[...] += dgp
            loss_ref[...] += jnp.broadcast_to(lp, (1, LANE))

    return pl.pallas_call(
        body, name=name, grid=(t // TE,), in_specs=[_rows(D), _rows(D), _whole((1, D))],
        out_specs=[_rows(D), _rows(D), _whole((1, D)), _whole((1, LANE))],
        out_shape=[jax.ShapeDtypeStruct((t, D), F32), jax.ShapeDtypeStruct((t, D), BF16),
                   jax.ShapeDtypeStruct((1, D), F32), jax.ShapeDtypeStruct((1, LANE), F32)],
        compiler_params=_params(("arbitrary",), 4 * TE * D * 4),
    )(h, target, gain)


def _seq_scratch(cols):
    return pltpu.VMEM((-(-cols // LANE), TE + SUB, LANE), F32)


def _taps(scr, c, wd, first, n, k):
    return [scr[c, first - (k - 1) + j:first - (k - 1) + j + n, 0:wd] for j in range(k)]


def _stage_history(scr, i):
    @pl.when(i == 0)
    def _():
        scr[...] = jnp.zeros(scr.shape, F32)

    @pl.when(i > 0)
    def _():
        scr[:, 0:SUB, :] = scr[:, TE:TE + SUB, :]


def _stage_future(scr, i):
    @pl.when(i == 0)
    def _():
        scr[...] = jnp.zeros(scr.shape, F32)

    @pl.when(i > 0)
    def _():
        scr[:, TE:TE + SUB, :] = scr[:, 0:SUB, :]


def _conv(tp, w):
    out = w[0:1] * tp[0]
    for j in range(1, len(tp)):
        out = out + w[j:j + 1] * tp[j]
    return out


def _conv_t(ds, c, wd, w):
    k = w.shape[0]
    out = w[k - 1:k] * ds[c, 0:TE, 0:wd]
    for j in range(k - 1):
        out = out + w[j:j + 1] * ds[c, k - 1 - j:k - 1 - j + TE, 0:wd]
    return out


def _gdn_pre_fwd(x, w, name):
    t = x.shape[0]

    def body(x_ref, w_ref, o_ref, xs):
        _stage_history(xs, pl.program_id(0))
        for hh in range(3 * H):
            sl = slice(hh * DH, (hh + 1) * DH)
            xs[hh, SUB:SUB + TE, :] = x_ref[:, sl].astype(F32)
            cv = _conv(_taps(xs, hh, DH, SUB, TE, 4), w_ref[:, sl])
            s = cv * _sigmoid(cv)
            if hh < 2 * H:
                s = s * lax.rsqrt(jnp.sum(s * s, axis=-1, keepdims=True) + EPS)
                if hh < H:
                    s = s * (DH ** -0.5)
            o_ref[:, sl] = s

    return pl.pallas_call(
        body, name=name, grid=(t // TE,), in_specs=[_rows(QKV), _whole((4, QKV))], out_specs=_rows(QKV),
        out_shape=jax.ShapeDtypeStruct((t, QKV), F32), scratch_shapes=[_seq_scratch(QKV)],
        compiler_params=_params(("arbitrary",), TE * QKV * 8),
    )(x, w)


def _gdn_pre_bwd(x, w, dqkv, name):
    t = x.shape[0]
    n = t // TE
    hb = TE // 16

    def body(x_ref, xp_ref, w_ref, d_ref, dx_ref, dw_ref, xs, ds):
        i = pl.program_id(0)

        @pl.when(i == 0)
        def _():
            dw_ref[...] = jnp.zeros_like(dw_ref)

        _stage_future(ds, i)
        for hh in range(3 * H):
            sl = slice(hh * DH, (hh + 1) * DH)
            xs[hh, 0:SUB, :] = jnp.where(i == n - 1, 0.0, xp_ref[SUB:2 * SUB, sl].astype(F32))
            xs[hh, SUB:SUB + TE, :] = x_ref[:, sl].astype(F32)
            wv = w_ref[:, sl]
            tp = _taps(xs, hh, DH, SUB, TE, 4)
            cv = _conv(tp, wv)
            sg = _sigmoid(cv)
            s = cv * sg
            dsv = d_ref[:, sl]
            if hh < 2 * H:
                if hh < H:
                    dsv = dsv * (DH ** -0.5)
                r = lax.rsqrt(jnp.sum(s * s, axis=-1, keepdims=True) + EPS)
                dsv = r * dsv - s * (r * r * r) * jnp.sum(s * dsv, axis=-1, keepdims=True)
            dcv = dsv * (sg * (1.0 + cv * (1.0 - sg)))
            ds[hh, 0:TE, :] = dcv
            dx_ref[:, sl] = _conv_t(ds, hh, DH, wv).astype(BF16)
            dw_ref[:, sl] += jnp.concatenate([jnp.sum(tp[j] * dcv, axis=0, keepdims=True) for j in range(4)], axis=0)

    return pl.pallas_call(
        body, name=name, grid=(n,),
        in_specs=[_rows(QKV, n), pl.BlockSpec((16, QKV), lambda i: (jnp.maximum((n - 1 - i) * hb - 1, 0), 0)),
                  _whole((4, QKV)), _rows(QKV, n)],
        out_specs=[_rows(QKV, n), _whole((4, QKV))],
        out_shape=[jax.ShapeDtypeStruct((t, QKV), BF16), jax.ShapeDtypeStruct((4, QKV), F32)],
        scratch_shapes=[_seq_scratch(QKV), _seq_scratch(QKV)],
        compiler_params=_params(("arbitrary",), TE * QKV * 14),
    )(x, x, w, dqkv)


def _softplus(x):
    return jnp.maximum(x, 0.0) + jnp.log(1.0 + jnp.exp(-jnp.abs(x)))


def _gates_fwd(ba, a_row, dt_row, name):
    t = ba.shape[0]

    def body(ba_ref, a_ref, dt_ref, b_out, g_out):
        real = _row_ids(pl.program_id(0)) >= LEAD
        b_out[...] = jnp.where(real, _sigmoid(ba_ref[:, 0:LANE]), 0.0)
        g = -jnp.exp(a_ref[...]) * _softplus(ba_ref[:, LANE:2 * LANE] + dt_ref[...])
        g_out[...] = jnp.where(real, g, 0.0)

    return pl.pallas_call(
        body, name=name, grid=(t // TE,), in_specs=[_rows(2 * LANE), _whole((1, LANE)), _whole((1, LANE))],
        out_specs=[_rows(LANE), _rows(LANE)],
        out_shape=[jax.ShapeDtypeStruct((t, LANE), F32), jax.ShapeDtypeStruct((t, LANE), F32)],
        compiler_params=_params(("parallel",), TE * LANE * 16),
    )(ba, a_row, dt_row)


def _gates_bwd(ba, a_row, dt_row, dbeta, dg, name):
    t = ba.shape[0]

    def body(ba_ref, a_ref, dt_ref, db_ref, dg_ref, dba_ref, da_out, ddt_out):
        i = pl.program_id(0)
        real = _row_ids(i) >= LEAD
        beta = _sigmoid(ba_ref[:, 0:LANE])
        draw_b = jnp.where(real, db_ref[...] * beta * (1.0 - beta), 0.0)
        pre = ba_ref[:, LANE:2 * LANE] + dt_ref[...]
        neg_a = -jnp.exp(a_ref[...])
        dgv = jnp.where(real, dg_ref[...], 0.0)
        draw_a = dgv * neg_a * _sigmoid(pre)
        dba_ref[:, 0:LANE] = draw_b.astype(BF16)
        dba_ref[:, LANE:2 * LANE] = draw_a.astype(BF16)
        dal = jnp.sum(dgv * neg_a * _softplus(pre), axis=0, keepdims=True)
        ddt = jnp.sum(draw_a, axis=0, keepdims=True)

        @pl.when(i == 0)
        def _():
            da_out[...] = dal
            ddt_out[...] = ddt

        @pl.when(i > 0)
        def _():
            da_out[...] += dal
            ddt_out[...] += ddt

    return pl.pallas_call(
        body, name=name, grid=(t // TE,),
        in_specs=[_rows(2 * LANE), _whole((1, LANE)), _whole((1, LANE)), _rows(LANE), _rows(LANE)],
        out_specs=[_rows(2 * LANE), _whole((1, LANE)), _whole((1, LANE))],
        out_shape=[jax.ShapeDtypeStruct((t, 2 * LANE), BF16), jax.ShapeDtypeStruct((1, LANE), F32),
                   jax.ShapeDtypeStruct((1, LANE), F32)],
        compiler_params=_params(("arbitrary",), TE * LANE * 24),
    )(ba, a_row, dt_row, dbeta, dg)


_OFFSETS = [(dx, dy, dc) for dx in (0, 1) for dy in (0, 1) for dc in (0, 1)][1:]
NPEER = len(_OFFSETS)
ANY_SPEC = pl.BlockSpec(memory_space=pl.ANY)


def _place():
    return lax.axis_index("x"), lax.axis_index("y"), lax.axis_index("c")


def _index(p):
    return 4 * p[0] + 2 * p[1] + p[2]


def _comm_scratch(n):
    return [pltpu.SemaphoreType.DMA((n * NPEER,)), pltpu.SemaphoreType.DMA((n * NPEER,)), pltpu.SemaphoreType.DMA((n,))]


def _scatter_copies(ins, outs, send, recv):
    me = _place()
    mi = _index(me)
    res = []
    for j, d in enumerate(_OFFSETS):
        peer = tuple(1 - v if bit else v for v, bit in zip(me, d))
        pi = _index(peer)
        for k in range(len(ins)):
            sem = k * NPEER + j
            mine = pltpu.make_async_remote_copy(src_ref=ins[k].at[pi], dst_ref=outs[k].at[mi], send_sem=send.at[sem],
                                                recv_sem=recv.at[sem], device_id=peer, device_id_type=MESH)
            theirs = pltpu.make_async_remote_copy(src_ref=ins[k].at[pi], dst_ref=outs[k].at[pi], send_sem=send.at[sem],
                                                  recv_sem=recv.at[sem], device_id=peer, device_id_type=MESH)
            res.append((mine, theirs))
    return res


def _scatter_own(ins, outs, loc):
    mi = _index(_place())
    return [pltpu.make_async_copy(ins[k].at[mi], outs[k].at[mi], loc.at[k]) for k in range(len(ins))]


def _scatter_start(ins, outs, send, recv, loc):
    for cp in _scatter_own(ins, outs, loc):
        cp.start()
    for mine, _ in _scatter_copies(ins, outs, send, recv):
        mine.start()


def _scatter_wait(ins, outs, send, recv, loc):
    cps = _scatter_copies(ins, outs, send, recv)
    for _, theirs in cps:
        theirs.wait_recv()
    for mine, _ in cps:
        mine.wait_send()
    for cp in _scatter_own(ins, outs, loc):
        cp.wait()


def _gather_parts(ins, outs, send, recv):
    x, y, c = _place()
    chips = [(1 - x, y), (x, 1 - y), (1 - x, 1 - y)]

    def cp(k, slot, src, block, to):
        return pltpu.make_async_remote_copy(src_ref=src, dst_ref=outs[k].at[_index(block)], send_sem=send.at[k * NPEER + slot],
                                            recv_sem=recv.at[k * NPEER + slot], device_id=to, device_id_type=MESH)

    return (x, y, c), (x, y, 1 - c), chips, cp


def _gather_start(ins, outs, send, recv, loc):
    me, sib, chips, cp = _gather_parts(ins, outs, send, recv)
    for k in range(len(ins)):
        pltpu.make_async_copy(ins[k], outs[k].at[_index(me)], loc.at[k]).start()
        cp(k, 0, ins[k], me, sib).start()
        for j, chip in enumerate(chips):
            cp(k, 1 + j, ins[k], me, (*chip, me[2])).start()


def _gather_forward(ins, outs, send, recv, loc):
    me, sib, chips, cp = _gather_parts(ins, outs, send, recv)
    for j, chip in enumerate(chips):
        blk = (*chip, me[2])
        for k in range(len(ins)):
            cp(k, 1 + j, ins[k], blk, me).wait_recv()
            cp(k, 4 + j, outs[k].at[_index(blk)], blk, sib).start()


def _gather_finish(ins, outs, send, recv, loc):
    me, sib, chips, cp = _gather_parts(ins, outs, send, recv)
    for k in range(len(ins)):
        cp(k, 0, ins[k], sib, me).wait_recv()
        for j, chip in enumerate(chips):
            cp(k, 4 + j, ins[k], (*chip, sib[2]), me).wait_recv()
        cp(k, 0, ins[k], me, sib).wait_send()
        for j, chip in enumerate(chips):
            cp(k, 1 + j, ins[k], me, (*chip, me[2])).wait_send()
            cp(k, 4 + j, outs[k].at[_index((*chip, me[2]))], (*chip, me[2]), sib).wait_send()
        pltpu.make_async_copy(ins[k], outs[k].at[_index(me)], loc.at[k]).wait()


def _gathered_shapes(arrs):
    return [jax.ShapeDtypeStruct((NDEV,) + a.shape, a.dtype) for a in arrs]


def _gather(arrs, name):
    n = len(arrs)

    def body(*refs):
        ins, outs, sems = refs[:n], refs[n:2 * n], refs[2 * n:]
        _gather_start(ins, outs, *sems)
        _gather_forward(ins, outs, *sems)
        _gather_finish(ins, outs, *sems)

    return pl.pallas_call(body, name=name, in_specs=[ANY_SPEC] * n, out_specs=[ANY_SPEC] * n,
                          out_shape=_gathered_shapes(arrs), scratch_shapes=_comm_scratch(n))(*arrs)


def _scatter(arrs, name):
    n = len(arrs)

    def body(*refs):
        ins, outs, sems = refs[:n], refs[n:2 * n], refs[2 * n:]
        _scatter_start(ins, outs, *sems)
        _scatter_wait(ins, outs, *sems)

    return pl.pallas_call(body, name=name, in_specs=[ANY_SPEC] * n, out_specs=[ANY_SPEC] * n,
                          out_shape=[jax.ShapeDtypeStruct(a.shape, a.dtype) for a in arrs],
                          scratch_shapes=_comm_scratch(n))(*arrs)


_BNN = (((2,), (1,)), ((0,), (0,)))
_BNT = (((2,), (2,)), ((0,), (0,)))
_BTN = (((1,), (1,)), ((0,), (0,)))


def _split(a):
    hi = a.astype(BF16)
    return hi, (a - hi.astype(F32)).astype(BF16)


def _dot3(a, b):
    ah, al = _split(a)
    bh, bl = _split(b)
    m = a.shape[1]
    r = _dot(jnp.concatenate([ah, al], axis=1), bh, _BNN)
    return r[:, :m] + r[:, m:] + _dot(ah, bl, _BNN)


GC = 3


def _rows_of(c):
    return slice(c * CH, (c + 1) * CH)


def _heads(ref, off):
    return jnp.stack([ref[_rows_of(c), off + h * DH:off + (h + 1) * DH] for c in range(GC) for h in range(H)])


def _cols(arrs):
    return jnp.stack([a[:, h:h + 1] for a in arrs for h in range(H)])


def _lanes(a):
    lane = lax.broadcasted_iota(jnp.int32, (CH, LANE), 1)
    out = jnp.zeros((CH, LANE), F32)
    for h in range(H):
        out = jnp.where(lane == h, a[h], out)
    return out


def _chunk_prep(qkv_ref, b_ref, g_ref):
    row = lax.broadcasted_iota(jnp.int32, (CH, CH), 0)
    col = lax.broadcasted_iota(jnp.int32, (CH, CH), 1)
    incl, strict = row >= col, row > col
    gcs = [_dot(incl.astype(F32), g_ref[_rows_of(c), :], precision=HI) for c in range(GC)]
    q, k, v = _heads(qkv_ref, 0), _heads(qkv_ref, D), _heads(qkv_ref, 2 * D)
    bcol, gcol = _cols([b_ref[_rows_of(c), :] for c in range(GC)]), _cols(gcs)
    grow = jnp.stack([gct[h:h + 1, :] for gct in [gc.T for gc in gcs] for h in range(H)])
    glast = _cols([gc[CH - 1:CH, :] for gc in gcs])
    dec = jnp.exp(jnp.where(incl[None], gcol - grow, -1e30))
    kb = k * bcol
    ab = _bdot(jnp.concatenate([kb, q], axis=1), k, _BNT)
    egc, ekc = jnp.exp(gcol), jnp.exp(glast - gcol)
    return dict(row=row, col=col, strict=strict[None], q=q, k=k, v=v, bcol=bcol, dec=dec, kb=kb,
                lm=jnp.where(strict[None], ab[:, :CH] * dec, 0.0), qk=ab[:, CH:] * dec, egc=egc, ekc=ekc,
                gth=jnp.exp(glast), qd=q * egc, kd=k * ekc, vb=v * bcol, kbg=kb * egc)


def _unit_lower_inverse(lm, eye):
    n = -lm
    x = eye + n
    pw = _dot3(n, n)
    for it in range(5):
        if it < 4:
            xp = _dot3(jnp.concatenate([x, pw], axis=1), pw)
            x = x + xp[:, :CH]
            pw = xp[:, CH:]
        else:
            x = x + _dot3(x, pw)
    return x


def _gdn_fwd(qkv, beta, g, name, gather=()):
    t = qkv.shape[0]
    nc = t // CH
    ns = nc // GC
    ng = len(gather)

    def body(qkv_ref, b_ref, g_ref, *rest):
        c_ins, (o_ref, sin_ref, vn_ref, ti_ref, w_ref) = rest[:ng], rest[ng:ng + 5]
        c_outs, state, sems = rest[ng + 5:2 * ng + 5], rest[2 * ng + 5], rest[2 * ng + 6:]
        step = pl.program_id(0)

        @pl.when(step == 0)
        def _():
            state[...] = jnp.zeros_like(state)
            if ng:
                _gather_start(c_ins, c_outs, *sems)

        if ng:
            @pl.when(step == max(ns - 4, 0))
            def _():
                _gather_forward(c_ins, c_outs, *sems)

            @pl.when(step == ns - 1)
            def _():
                _gather_finish(c_ins, c_outs, *sems)

        pr = _chunk_prep(qkv_ref, b_ref, g_ref)
        tinv = _unit_lower_inverse(pr["lm"], (pr["row"] == pr["col"]).astype(F32)[None])
        uw = _bdot(tinv, jnp.concatenate([pr["vb"], pr["kbg"]], axis=2), _BNN)
        u, w = uw[:, :, :DH], uw[:, :, DH:]
        s = state[...]
        for c in range(GC):
            hs = slice(c * H, (c + 1) * H)
            ws = _bdot(jnp.concatenate([w[hs], pr["qd"][hs]], axis=1), s, _BNN)
            vn = u[hs] - ws[:, :CH]
            o = ws[:, CH:] + _bdot(pr["qk"][hs], vn, _BNN)
            sin_ref[c] = s
            ti_ref[c] = tinv[hs]
            s = s * pr["gth"][hs] + _bdot(pr["kd"][hs], vn, _BTN)
            for h in range(H):
                sl = slice(h * DH, (h + 1) * DH)
                o_ref[_rows_of(c), sl] = o[h]
                vn_ref[_rows_of(c), sl] = vn[h]
                w_ref[_rows_of(c), sl] = w[c * H + h]
        state[...] = s

    chunk = lambda cols: pl.BlockSpec((GC * CH, cols), lambda c: (c, 0))
    outs = pl.pallas_call(
        body, name=name, grid=(ns,), in_specs=[chunk(QKV), chunk(LANE), chunk(LANE)] + [ANY_SPEC] * ng,
        out_specs=[chunk(D), pl.BlockSpec((GC, H, DH, DH), lambda c: (c, 0, 0, 0)), chunk(D),
                   pl.BlockSpec((GC, H, CH, CH), lambda c: (c, 0, 0, 0)), chunk(D)] + [ANY_SPEC] * ng,
        out_shape=[jax.ShapeDtypeStruct((t, D), F32), jax.ShapeDtypeStruct((nc, H, DH, DH), F32),
                   jax.ShapeDtypeStruct((t, D), F32), jax.ShapeDtypeStruct((nc, H, CH, CH), F32),
                   jax.ShapeDtypeStruct((t, D), F32)] + _gathered_shapes(gather),
        scratch_shapes=[pltpu.VMEM((H, DH, DH), F32)] + (_comm_scratch(ng) if ng else []),
        compiler_params=_params(("arbitrary",), 12 << 20),
    )(qkv, beta, g, *gather)
    return outs[:5], outs[5:]


def _gdn_bwd(qkv, beta, g, do, s_in, vnew, tinv, wsv, name, scatter=()):
    t = qkv.shape[0]
    nsteps = t // CH // GC
    ns = len(scatter)

    def body(qkv_ref, b_ref, g_ref, do_ref, sin_ref, vn_ref, ti_ref, w_ref, *rest):
        c_ins, (dqkv_ref, db_ref, dg_ref) = rest[:ns], rest[ns:ns + 3]
        c_outs, dstate, sems = rest[ns + 3:2 * ns + 3], rest[2 * ns + 3], rest[2 * ns + 4:]
        step = pl.program_id(0)

        @pl.when(step == 0)
        def _():
            dstate[...] = jnp.zeros_like(dstate)
            if ns:
                _scatter_start(c_ins, c_outs, *sems)

        if ns:
            @pl.when(step == nsteps - 1)
            def _():
                _scatter_wait(c_ins, c_outs, *sems)

        pr = _chunk_prep(qkv_ref, b_ref, g_ref)
        ti = jnp.concatenate([ti_ref[c] for c in range(GC)], axis=0)
        s = jnp.concatenate([sin_ref[c] for c in range(GC)], axis=0)
        w, vn, doh = _heads(w_ref, 0), _heads(vn_ref, 0), _heads(do_ref, 0)
        dqd = _bdot(doh, s, _BNT)
        dqk = _bdot(doh, vn, _BNT)
        qk_do = _bdot(pr["qk"], doh, _BTN)
        qd_do = _bdot(pr["qd"], doh, _BTN)
        ds = dstate[...]
        dvn_c, dkd_c, dw_c, dgt_c = [None] * GC, [None] * GC, [None] * GC, [None] * GC
        for c in reversed(range(GC)):
            hs = slice(c * H, (c + 1) * H)
            dvn_c[c] = _bdot(pr["kd"][hs], ds, _BNN) + qk_do[hs]
            dkd_c[c] = _bdot(vn[hs], ds, _BNT)
            dw_c[c] = -_bdot(dvn_c[c], s[hs], _BNT)
            dgt_c[c] = jnp.sum(jnp.sum(ds * s[hs], axis=2, keepdims=True), axis=1, keepdims=True)
            ds = ds * pr["gth"][hs] + qd_do[hs] - _bdot(w[hs], dvn_c[c], _BTN)
        dstate[...] = ds
        dvn, dkd, dw, dgt = (jnp.concatenate(parts, axis=0) for parts in (dvn_c, dkd_c, dw_c, dgt_c))
        duw = jnp.concatenate([dvn, dw], axis=2)
        dvk = _bdot(ti, duw, _BTN)
        dvb, dkbg = dvk[:, :, :DH], dvk[:, :, DH:]
        dti = _bdot(duw, jnp.concatenate([pr["vb"], pr["kbg"]], axis=2), _BNT)
        dl = -_dot(_dot(ti, dti, _BTN, precision=HI), ti, _BNT, precision=HI)
        dl = jnp.where(pr["strict"], dl, 0.0)
        dab = jnp.concatenate([dl * pr["dec"], dqk * pr["dec"]], axis=1)
        r1 = _bdot(dab, pr["k"], _BNN)
        dkb = r1[:, :CH] + dkbg * pr["egc"]
        dq = r1[:, CH:] + dqd * pr["egc"]
        dk = _bdot(dab, jnp.concatenate([pr["kb"], pr["q"]], axis=1), _BTN) + dkb * pr["bcol"] + dkd * pr["ekc"]
        m = dl * pr["lm"] + dqk * pr["qk"]
        colsum = _dot(m, jnp.ones((GC * H, CH, LANE), F32), _BTN, precision=HI)[:, :, 0:1]
        kdsum = jnp.sum(dkd * pr["kd"], axis=2, keepdims=True)
        dgc = (jnp.sum(m, axis=2, keepdims=True) - colsum + jnp.sum(dkbg * pr["kbg"], axis=2, keepdims=True)
               + jnp.sum(dqd * pr["qd"], axis=2, keepdims=True) - kdsum)
        dglast = jnp.sum(kdsum, axis=1, keepdims=True) + dgt * pr["gth"]
        last_row = lax.broadcasted_iota(jnp.int32, (1, CH, 1), 1) == CH - 1
        dgc = dgc + jnp.where(last_row, dglast, 0.0)
        dbeta = jnp.sum(dkb * pr["k"], axis=2, keepdims=True) + jnp.sum(dvb * pr["v"], axis=2, keepdims=True)
        dv = dvb * pr["bcol"]
        upper = (pr["row"] <= pr["col"]).astype(F32)
        for c in range(GC):
            hs = slice(c * H, (c + 1) * H)
            for h in range(H):
                dqkv_ref[_rows_of(c), h * DH:(h + 1) * DH] = dq[c * H + h]
                dqkv_ref[_rows_of(c), D + h * DH:D + (h + 1) * DH] = dk[c * H + h]
                dqkv_ref[_rows_of(c), 2 * D + h * DH:2 * D + (h + 1) * DH] = dv[c * H + h]
            db_ref[_rows_of(c), :] = _lanes(dbeta[hs])
            dg_ref[_rows_of(c), :] = _dot(upper, _lanes(dgc[hs]), precision=HI)

    chunk = lambda cols: pl.BlockSpec((GC * CH, cols), lambda c: (nsteps - 1 - c, 0))
    sq = lambda a, b: pl.BlockSpec((GC, H, a, b), lambda c: (nsteps - 1 - c, 0, 0, 0))
    outs = pl.pallas_call(
        body, name=name, grid=(nsteps,),
        in_specs=[chunk(QKV), chunk(LANE), chunk(LANE), chunk(D), sq(DH, DH), chunk(D), sq(CH, CH), chunk(D)] + [ANY_SPEC] * ns,
        out_specs=[chunk(QKV), chunk(LANE), chunk(LANE)] + [ANY_SPEC] * ns,
        out_shape=[jax.ShapeDtypeStruct((t, QKV), F32), jax.ShapeDtypeStruct((t, LANE), F32),
                   jax.ShapeDtypeStruct((t, LANE), F32)] + [jax.ShapeDtypeStruct(a.shape, a.dtype) for a in scatter],
        scratch_shapes=[pltpu.VMEM((H, DH, DH), F32)] + (_comm_scratch(ns) if ns else []),
        compiler_params=_params(("arbitrary",), 16 << 20),
    )(qkv, beta, g, do, s_in, vnew, tinv, wsv, *scatter)
    return outs[:3], outs[3:]


def _pool_counts(row_ids, win):
    return jnp.minimum(jnp.maximum(row_ids - LEAD, 0) + 1, win).astype(F32)


def _pool_fwd(p, name):
    t = p.shape[0]
    ext = TE + 16

    def body(p_ref, o_ref, carry):
        i = pl.program_id(0)

        @pl.when(i == 0)
        def _():
            carry[...] = jnp.zeros_like(carry)

        ids = _row_ids(i)
        for gi, win in enumerate(POOL_WINDOWS):
            sl = slice(gi * LANE, (gi + 1) * LANE)
            xv = p_ref[:, sl]
            s = jnp.concatenate([carry[:, sl], xv], axis=0)
            sh = 1
            while sh < win:
                s = s + pltpu.roll(s, sh, 0)
                sh *= 2
            o_ref[:, sl] = (s[16:ext] / _pool_counts(ids, win) - xv).astype(BF16)
            carry[:, sl] = xv[TE - 16:TE]

    return pl.pallas_call(
        body, name=name, grid=(t // TE,), in_specs=[_rows(POOL_W)], out_specs=_rows(POOL_W),
        out_shape=jax.ShapeDtypeStruct((t, POOL_W), BF16), scratch_shapes=[pltpu.VMEM((16, POOL_W), F32)],
        compiler_params=_params(("arbitrary",), TE * POOL_W * 8),
    )(p)


def _pool_bwd(dpo, name):
    t = dpo.shape[0]
    n = t // TE
    ext = TE + 16

    def body(d_ref, o_ref, carry):
        i = pl.program_id(0)

        @pl.when(i == 0)
        def _():
            carry[...] = jnp.zeros_like(carry)

        ids = _row_ids(n - 1 - i)
        for gi, win in enumerate(POOL_WINDOWS):
            sl = slice(gi * LANE, (gi + 1) * LANE)
            dv = d_ref[:, sl]
            rv = dv / _pool_counts(ids, win)
            s = jnp.concatenate([rv, carry[:, sl]], axis=0)
            sh = 1
            while sh < win:
                s = s + pltpu.roll(s, ext - sh, 0)
                sh *= 2
            o_ref[:, sl] = (s[0:TE] - dv).astype(BF16)
            carry[:, sl] = rv[0:16]

    return pl.pallas_call(
        body, name=name, grid=(n,), in_specs=[_rows(POOL_W, n)], out_specs=_rows(POOL_W, n),
        out_shape=jax.ShapeDtypeStruct((t, POOL_W), BF16), scratch_shapes=[pltpu.VMEM((16, POOL_W), F32)],
        compiler_params=_params(("arbitrary",), TE * POOL_W * 8),
    )(dpo)


def _post_fwd(o, z, gate, pm, hn, ps, name):
    t = o.shape[0]

    def body(o_ref, z_ref, g_ref, pm_ref, hn_ref, ps_ref, y_ref):
        for h in range(H):
            sl = slice(h * DH, (h + 1) * DH)
            ov = o_ref[:, sl]
            zv = z_ref[:, sl].astype(F32)
            r = lax.rsqrt(jnp.mean(ov * ov, axis=-1, keepdims=True) + EPS)
            ya = ov * r * hn_ref[...] * (zv * _sigmoid(zv))
            ga = _sigmoid(g_ref[:, sl].astype(F32))
            gb = _sigmoid(g_ref[:, D + h * DH:D + (h + 1) * DH].astype(F32))
            y_ref[:, sl] = (ga * ya + gb * (pm_ref[:, sl] * ps_ref[:, sl])).astype(BF16)

    return pl.pallas_call(
        body, name=name, grid=(t // TE,),
        in_specs=[_rows(D), _rows(D), _rows(2 * D), _rows(D), _whole((1, DH)), _whole((1, D))], out_specs=_rows(D),
        out_shape=jax.ShapeDtypeStruct((t, D), BF16), compiler_params=_params(("parallel",), TE * D * 16),
    )(o, z, gate, pm, hn, ps)


def _post_bwd(dy, o, z, gate, pm, hn, ps, name):
    t = o.shape[0]

    def body(dy_ref, o_ref, z_ref, g_ref, pm_ref, hn_ref, ps_ref, do_ref, dz_ref, dgate_ref, dpm_ref, dhn_ref, dps_ref):
        i = pl.program_id(0)

        @pl.when(i == 0)
        def _():
            dhn_ref[...] = jnp.zeros_like(dhn_ref)
            dps_ref[...] = jnp.zeros_like(dps_ref)

        hnv = hn_ref[...]
        dhn = jnp.zeros((1, DH), F32)
        for h in range(H):
            sl = slice(h * DH, (h + 1) * DH)
            slb = slice(D + h * DH, D + (h + 1) * DH)
            dyv = dy_ref[:, sl]
            ov = o_ref[:, sl]
            zv = z_ref[:, sl].astype(F32)
            r = lax.rsqrt(jnp.mean(ov * ov, axis=-1, keepdims=True) + EPS)
            sz = _sigmoid(zv)
            silu = zv * sz
            on = ov * r
            ya = on * hnv * silu
            ga = _sigmoid(g_ref[:, sl].astype(F32))
            gb = _sigmoid(g_ref[:, slb].astype(F32))
            pmv = pm_ref[:, sl]
            psv = ps_ref[:, sl]
            dya = dyv * ga
            dyb = dyv * gb
            dgate_ref[:, sl] = (dyv * ya * ga * (1.0 - ga)).astype(BF16)
            dgate_ref[:, slb] = (dyv * (pmv * psv) * gb * (1.0 - gb)).astype(BF16)
            tt = dya * hnv * silu
            do_ref[:, sl] = r * tt - ov * (r * r * r) * jnp.mean(ov * tt, axis=-1, keepdims=True)
            dz_ref[:, sl] = (dya * on * hnv * (sz * (1.0 + zv * (1.0 - sz)))).astype(BF16)
            dhn = dhn + jnp.sum(dya * on * silu, axis=0, keepdims=True)
            dps_ref[:, sl] += jnp.sum(dyb * pmv, axis=0, keepdims=True)
            dpm_ref[:, sl] = (dyb * psv).astype(BF16)
        dhn_ref[...] += dhn

    return pl.pallas_call(
        body, name=name, grid=(t // TE,),
        in_specs=[_rows(D), _rows(D), _rows(D), _rows(2 * D), _rows(D), _whole((1, DH)), _whole((1, D))],
        out_specs=[_rows(D), _rows(D), _rows(2 * D), _rows(D), _whole((1, DH)), _whole((1, D))],
        out_shape=[jax.ShapeDtypeStruct((t, D), F32), jax.ShapeDtypeStruct((t, D), BF16),
                   jax.ShapeDtypeStruct((t, 2 * D), BF16), jax.ShapeDtypeStruct((t, D), BF16),
                   jax.ShapeDtypeStruct((1, DH), F32), jax.ShapeDtypeStruct((1, D), F32)],
        compiler_params=_params(("arbitrary",), TE * D * 28),
    )(dy, o, z, gate, pm, hn, ps)


_FB_COLS = [(c, min(c + LANE, FB)) for c in range(0, FB, LANE)]


def _mlp_act_fwd(hid, cw, name):
    t = hid.shape[1]
    n = t // TE

    def body(hg_ref, hv_ref, wg_ref, wv_ref, a_ref, xg, xv):
        i = pl.program_id(1)
        _stage_history(xg, i)
        _stage_history(xv, i)
        for c, (c0, c1) in enumerate(_FB_COLS):
            sl, wd = slice(c0, c1), c1 - c0
            xg[c, SUB:SUB + TE, 0:wd] = hg_ref[:, sl].astype(F32)
            xv[c, SUB:SUB + TE, 0:wd] = hv_ref[:, sl].astype(F32)
            gg = _conv(_taps(xg, c, wd, SUB, TE, 3), wg_ref[:, sl])
            vv = _conv(_taps(xv, c, wd, SUB, TE, 3), wv_ref[:, sl])
            a_ref[:, sl] = (gg * _sigmoid(gg) * vv).astype(BF16)

    hspec = lambda off: pl.BlockSpec((None, TE, FB), lambda p, i: (p + off, i, 0))
    wspec = lambda off: pl.BlockSpec((None, 3, FB), lambda p, i: (p + off, 0, 0))
    return pl.pallas_call(
        body, name=name, grid=(4, n), in_specs=[hspec(0), hspec(4), wspec(0), wspec(4)],
        out_specs=pl.BlockSpec((None, TE, FB), lambda p, i: (p, i, 0)),
        out_shape=jax.ShapeDtypeStruct((4, t, FB), BF16),
        scratch_shapes=[_seq_scratch(FB), _seq_scratch(FB)],
        compiler_params=_params(("parallel", "arbitrary"), TE * FB * 12),
    )(hid, hid, cw, cw)


def _mlp_act_bwd(da, hid, cw, name):
    t = hid.shape[1]
    n = t // TE
    hb = TE // 16

    def body(da_ref, hg_ref, hv_ref, pg_ref, pv_ref, wg_ref, wv_ref, dhg_ref, dhv_ref, dwg_ref, dwv_ref, xg, xv, dg, dv):
        i = pl.program_id(1)

        @pl.when(i == 0)
        def _():
            dwg_ref[...] = jnp.zeros_like(dwg_ref)
            dwv_ref[...] = jnp.zeros_like(dwv_ref)

        _stage_future(dg, i)
        _stage_future(dv, i)
        first_tile = i == n - 1
        for c, (c0, c1) in enumerate(_FB_COLS):
            sl, wd = slice(c0, c1), c1 - c0
            for scr, p_ref, h_ref in ((xg, pg_ref, hg_ref), (xv, pv_ref, hv_ref)):
                scr[c, 0:SUB, 0:wd] = jnp.where(first_tile, 0.0, p_ref[SUB:2 * SUB, sl].astype(F32))
                scr[c, SUB:SUB + TE, 0:wd] = h_ref[:, sl].astype(F32)
            wg = wg_ref[:, sl]
            wv = wv_ref[:, sl]
            tg = _taps(xg, c, wd, SUB, TE, 3)
            tv = _taps(xv, c, wd, SUB, TE, 3)
            gg = _conv(tg, wg)
            vv = _conv(tv, wv)
            sg = _sigmoid(gg)
            dav = da_ref[:, sl].astype(F32)
            dgg = dav * vv * (sg * (1.0 + gg * (1.0 - sg)))
            dvv = dav * (gg * sg)
            for dc, tp, w, scr, dh_ref, dw_ref in ((dgg, tg, wg, dg, dhg_ref, dwg_ref), (dvv, tv, wv, dv, dhv_ref, dwv_ref)):
                scr[c, 0:TE, 0:wd] = dc
                dh_ref[:, sl] = _conv_t(scr, c, wd, w).astype(BF16)
                dw_ref[:, sl] += jnp.concatenate([jnp.sum(tp[j] * dc, axis=0, keepdims=True) for j in range(3)], axis=0)

    rev = lambda off: pl.BlockSpec((None, TE, FB), lambda p, i: (p + off, n - 1 - i, 0))
    halo = lambda off: pl.BlockSpec((None, 16, FB), lambda p, i: (p + off, jnp.maximum((n - 1 - i) * hb - 1, 0), 0))
    wspec = lambda off: pl.BlockSpec((None, 3, FB), lambda p, i: (p + off, 0, 0))
    dwspec = pl.BlockSpec((None, 3, FB), lambda p, i: (p, 0, 0))
    return pl.pallas_call(
        body, name=name, grid=(4, n), in_specs=[rev(0), rev(0), rev(4), halo(0), halo(4), wspec(0), wspec(4)],
        out_specs=[rev(0), rev(0), dwspec, dwspec],
        out_shape=[jax.ShapeDtypeStruct((4, t, FB), BF16), jax.ShapeDtypeStruct((4, t, FB), BF16),
                   jax.ShapeDtypeStruct((4, 3, FB), F32), jax.ShapeDtypeStruct((4, 3, FB), F32)],
        scratch_shapes=[_seq_scratch(FB)] * 4,
        compiler_params=_params(("parallel", "arbitrary"), TE * FB * 24),
    )(da, hid, hid, hid, hid, cw, cw)


def _adamw(land, w, m, v, name):
    r, c = w.shape
    tr = r
    for cand in (128, 64, 32, 16):
        if r % cand == 0:
            tr = cand
            break
    c1 = 1.0 - ADAM_B1 ** ADAM_STEP
    c2 = 1.0 - ADAM_B2 ** ADAM_STEP

    def body(l_ref, w_ref, m_ref, v_ref, g_out, d_out, m_out, v_out):
        g = l_ref[0].astype(F32)
        for i in range(1, NDEV):
            g = g + l_ref[i].astype(F32)
        mn = ADAM_B1 * m_ref[...] + (1.0 - ADAM_B1) * g
        vn = ADAM_B2 * v_ref[...] + (1.0 - ADAM_B2) * (g * g)
        g_out[...] = g
        m_out[...] = mn
        v_out[...] = vn
        d_out[...] = -ADAM_LR * ((mn / c1) / (jnp.sqrt(vn / c2) + ADAM_EPS) + ADAM_WD * w_ref[...])

    spec = pl.BlockSpec((tr, c), lambda i: (i, 0))
    shp = jax.ShapeDtypeStruct((r, c), F32)
    return pl.pallas_call(
        body, name=name, grid=(r // tr,), in_specs=[pl.BlockSpec((NDEV, tr, c), lambda i: (0, i, 0)), spec, spec, spec],
        out_specs=[spec] * 4, out_shape=[shp] * 4, compiler_params=_params(("parallel",), 15 * tr * c * 4),
    )(land, w, m, v)


def _layer_fwd(h, p, tag, gather=(), finish=None):
    u = _rmsnorm_fwd(h, p["norm_mix"], f"norm_mix_{tag}")
    qkv_pre = _mm(u, p["w_qkv"], BF16, f"proj_qkv_{tag}")
    z = _mm(u, p["w_z"], BF16, f"proj_z_{tag}")
    ba = _mm(u, p["w_ba"], F32, f"proj_ba_{tag}")
    pool_in = _mm(u, p["w_pl"], F32, f"proj_pool_{tag}")
    gate = _mm(u, p["w_gate"], BF16, f"proj_gate_{tag}")
    qkv = _gdn_pre_fwd(qkv_pre, p["conv_qkv"], f"gdn_pre_{tag}")
    beta, g = _gates_fwd(ba, p["a_row"], p["dt_row"], f"gates_{tag}")
    (o, s_in, vnew, tinv, wsv), gathered = _gdn_fwd(qkv, beta, g, f"gdn_{tag}", gather)
    if finish is not None:
        p = {**p, **finish(gathered)}
    pooled = _pool_fwd(pool_in, f"pool_{tag}")
    pm = _mm_cols(pooled, p["w_pool"], F32, f"pool_mm_{tag}", _NN)
    y = _post_fwd(o, z, gate, pm, p["head_norm"], p["pool_scale"], f"post_{tag}")
    h1 = _mm(y, p["w_out"], F32, f"out_proj_{tag}", res=h)
    u2 = _rmsnorm_fwd(h1, p["norm_ffn"], f"norm_ffn_{tag}")
    hid = _mm_up(u2, p["w_up"], f"up_proj_{tag}")
    act = _mlp_act_fwd(hid, p["conv_ffn"], f"mlp_act_{tag}")
    h2 = _mm_blocks_red(act, p["w_down"], f"down_proj_{tag}", _NN, res=h1)
    saved = dict(h=h, u=u, qkv_pre=qkv_pre, z=z, ba=ba, gate=gate, qkv=qkv, beta=beta, g=g, o=o, s_in=s_in, vnew=vnew,
                 tinv=tinv, wsv=wsv, pooled=pooled, pm=pm, y=y, h1=h1, u2=u2, hid=hid, act=act)
    return h2, saved, gathered, p


def _layer_bwd(dh, dh_b, p, s, tag, scatter=()):
    gr = {}
    da = _mm_to_blocks(dh_b, p["w_down"], f"d_act_{tag}")
    gr["w_down"] = _mm_tn_blocks(s["act"], dh_b, f"dw_down_{tag}", True, False)
    dhg, dhv, dwg, dwv = _mlp_act_bwd(da, s["hid"], p["conv_ffn"], f"mlp_act_bwd_{tag}")
    gr["conv_ffn"] = jnp.concatenate([dwg, dwv], axis=0)
    w_up = p["w_up"]
    du2 = _mm_blocks_red(dhg, w_up[:4], f"d_u2g_{tag}", _NT)
    du2 = _mm_blocks_red(dhv, w_up[4:], f"d_u2v_{tag}", _NT, res=du2)
    gr["w_up"] = jnp.concatenate([_mm_tn_blocks(s["u2"], dhg, f"dw_upg_{tag}", False, True),
                                  _mm_tn_blocks(s["u2"], dhv, f"dw_upv_{tag}", False, True)], axis=0)
    dh1, dh1_b, gr["norm_ffn"] = _rmsnorm_bwd(s["h1"], du2, dh, p["norm_ffn"], f"norm_ffn_bwd_{tag}")
    dy = _mm(dh1_b, p["w_out"], F32, f"d_y_{tag}", dims=_NT)
    gr["w_out"] = _mm_tn(s["y"], dh1_b, f"dw_out_{tag}")
    do, dz, dgate, dpm, gr["head_norm"], gr["pool_scale"] = _post_bwd(
        dy, s["o"], s["z"], s["gate"], s["pm"], p["head_norm"], p["pool_scale"], f"post_bwd_{tag}")
    dpooled = _mm_cols(dpm, p["w_pool"], F32, f"d_pooled_{tag}", _NT)
    gr["w_pool"] = _mm_tn_cols(s["pooled"], dpm, 4, f"dw_pool_{tag}")
    dpool_in = _pool_bwd(dpooled, f"pool_bwd_{tag}")
    own = (gr["w_up"].astype(BF16), gr["w_down"].reshape(NDEV, -1, D).astype(BF16))
    (dqkv, dbeta, dg), landed = _gdn_bwd(s["qkv"], s["beta"], s["g"], do, s["s_in"], s["vnew"], s["tinv"], s["wsv"],
                                         f"gdn_bwd_{tag}", own + tuple(scatter))
    dba, gr["a_log"], gr["dt_bias"] = _gates_bwd(s["ba"], p["a_row"], p["dt_row"], dbeta, dg, f"gates_bwd_{tag}")
    dqkv_pre, gr["conv_qkv"] = _gdn_pre_bwd(s["qkv_pre"], p["conv_qkv"], dqkv, f"gdn_pre_bwd_{tag}")
    du = None
    dws = []
    for nm, dseg, wseg in (("qkv", dqkv_pre, p["w_qkv"]), ("z", dz, p["w_z"]), ("ba", dba, p["w_ba"]),
                           ("pool", dpool_in, p["w_pl"]), ("gate", dgate, p["w_gate"])):
        du = _mm(dseg, wseg, F32, f"d_u_{nm}_{tag}", res=du, dims=_NT)
        dws.append(_mm_tn(s["u"], dseg, f"dw_{nm}_{tag}"))
    gr["w_in"] = jnp.concatenate([dws[0], dws[1], dws[2][:, 0:H], dws[2][:, LANE:LANE + H], dws[3], dws[4]], axis=1)
    dh0, dh0_b, gr["norm_mix"] = _rmsnorm_bwd(s["h"], du, dh1, p["norm_mix"], f"norm_mix_bwd_{tag}")
    return dh0, dh0_b, gr, landed


def _pad_lanes(v8):
    return jnp.pad(v8.reshape(1, H), ((0, 0), (0, LANE - H)))


def _pack(parts, rows):
    flat = jnp.concatenate([q.reshape(-1) for q in parts])
    return jnp.pad(flat, (0, rows * LANE - flat.shape[0])).reshape(rows, LANE)


def _unpack(packed, shapes):
    flat = packed.reshape(-1)
    out, off = [], 0
    for shp in shapes:
        n = 1
        for s_ in shp:
            n *= s_
        out.append(flat[off:off + n].reshape(shp))
        off += n
    return out


SMALL_ROWS = 336
REPL_ROWS = 64


def kernel(x, meta_tokens, norm_mix, w_in, conv_qkv, a_log, dt_bias, head_norm, w_pool, pool_scale, w_out, norm_ffn, w_up, conv_ffn, w_down, norm_final, loss_target, m_meta_tokens, m_norm_mix, m_w_in, m_conv_qkv, m_a_log, m_dt_bias, m_head_norm, m_w_pool, m_pool_scale, m_w_out, m_norm_ffn, m_w_up, m_conv_ffn, m_w_down, m_norm_final, v_meta_tokens, v_norm_mix, v_w_in, v_conv_qkv, v_a_log, v_dt_bias, v_head_norm, v_w_pool, v_pool_scale, v_w_out, v_norm_ffn, v_w_up, v_conv_ffn, v_w_down, v_norm_final):
    seq = x.shape[1]
    t = ROW0 + seq
    assert t % TE == 0 and t % (MM_TILES * 16) == 0 and t % (GC * CH) == 0
    depth = w_in.shape[0]
    assert depth == 2
    cin = w_in.shape[2]

    def mixer_params(l, g_in, conv_q, conv_f, wp):
        wf = jnp.transpose(g_in, (1, 0, 2)).reshape(D, NDEV * cin)
        zpad = jnp.zeros((D, LANE - H), BF16)
        return dict(
            w_qkv=wf[:, 0:QKV], w_z=wf[:, QKV:QKV + D],
            w_ba=jnp.concatenate([wf[:, 4096:4104], zpad, wf[:, 4104:4112], zpad], axis=1),
            w_pl=wf[:, 4112:4624], w_gate=wf[:, 4624:6672], conv_qkv=conv_q, conv_ffn=conv_f, w_pool=wp,
            norm_mix=norm_mix[l].reshape(1, D), norm_ffn=norm_ffn[l].reshape(1, D),
            pool_scale=pool_scale[l].reshape(1, D), head_norm=head_norm[l].reshape(1, DH),
            a_row=_pad_lanes(a_log[l]), dt_row=_pad_lanes(dt_bias[l]))

    def late_params(g_up, g_out, g_down):
        return dict(w_out=g_out.reshape(D, D), w_up=g_up, w_down=g_down.reshape(4, FB, D))

    small_shapes = [conv_qkv.shape, conv_ffn.shape, w_pool.shape, meta_tokens.shape]
    small = _pack([conv_qkv, conv_ffn, w_pool, meta_tokens], SMALL_ROWS)
    w_in_b, w_up_b, w_out_b, w_down_b = w_in.astype(BF16), w_up.astype(BF16), w_out.astype(BF16), w_down.astype(BF16)
    g_in0, g_small = _gather([w_in_b[0], small], "gather_first")
    smalls = [_unpack(g_small[i], small_shapes) for i in range(NDEV)]
    conv_qkv_full = jnp.concatenate([sm[0] for sm in smalls], axis=2)
    conv_ffn_blk = jnp.stack([sm[1] for sm in smalls], axis=1)
    w_pool_full = jnp.concatenate([sm[2] for sm in smalls], axis=3).astype(BF16)
    meta_full = jnp.concatenate([sm[3] for sm in smalls], axis=1)

    h = jnp.concatenate([jnp.zeros((LEAD, D), F32), meta_full, x[0]], axis=0)
    p0 = mixer_params(0, g_in0, conv_qkv_full[0], conv_ffn_blk[0], w_pool_full[0])
    h, sv0, rest, p0 = _layer_fwd(
        h, p0, "l0", (w_up_b[0], w_out_b[0], w_down_b[0], w_in_b[1], w_up_b[1], w_out_b[1], w_down_b[1]),
        lambda got: late_params(*got[:3]))
    p1 = {**mixer_params(1, rest[3], conv_qkv_full[1], conv_ffn_blk[1], w_pool_full[1]), **late_params(*rest[4:])}
    h, sv1, _, _ = _layer_fwd(h, p1, "l1")
    layers = [p0, p1]
    saved = [sv0, sv1]
    target = jnp.concatenate([jnp.zeros((ROW0, D), F32), loss_target[0]], axis=0)
    dh, dh_b, d_norm_final, loss_row = _loss_bwd(h, target, norm_final.reshape(1, D), "loss")

    def mixer_blocks(gr):
        return (jnp.transpose(gr["w_in"].reshape(D, NDEV, cin), (1, 0, 2)).astype(BF16),
                gr["w_out"].reshape(NDEV, D // NDEV, D).astype(BF16))

    grads = [None] * depth
    dh, dh_b, grads[1], (l_up1, l_down1) = _layer_bwd(dh, dh_b, layers[1], saved[1], "l1")
    dh, dh_b, grads[0], (l_up0, l_down0, l_in1, l_out1) = _layer_bwd(dh, dh_b, layers[0], saved[0], "l0", mixer_blocks(grads[1]))
    grad_x = dh[ROW0:].reshape(1, seq, D)
    d_meta = dh[LEAD:ROW0]

    stk = lambda name: jnp.stack([grads[l][name] for l in range(depth)], axis=0)
    cq = conv_qkv.shape[2]
    pw = w_pool.shape[3]
    s_cq = jnp.transpose(stk("conv_qkv").reshape(depth, 4, NDEV, cq), (2, 0, 1, 3))
    s_cf = jnp.transpose(stk("conv_ffn"), (1, 0, 2, 3))
    s_wp = jnp.transpose(stk("w_pool").reshape(depth, 4, DH, NDEV, pw), (3, 0, 1, 2, 4))
    s_mt = jnp.transpose(d_meta.reshape(N_META, NDEV, D // NDEV), (1, 0, 2))
    b_small = jnp.stack([_pack([s_cq[i], s_cf[i], s_wp[i], s_mt[i]], SMALL_ROWS) for i in range(NDEV)], axis=0)
    l_in0, l_out0, l_small = _scatter([*mixer_blocks(grads[0]), b_small], "exchange_last")

    def upd(lands, w, m, v, name):
        res = [_adamw(land, w[l], m[l], v[l], f"adamw_{name}_l{l}") for l, land in enumerate(lands)]
        return [jnp.stack([res[0][kind], res[1][kind]], axis=0) for kind in range(4)]

    r_in = upd((l_in0, l_in1), w_in, m_w_in, v_w_in, "w_in")
    r_up = upd((l_up0, l_up1), w_up, m_w_up, v_w_up, "w_up")
    r_out = upd((l_out0, l_out1), w_out, m_w_out, v_w_out, "w_out")
    r_down = upd((l_down0, l_down1), w_down, m_w_down, v_w_down, "w_down")
    r_small = _adamw(l_small, small, _pack([m_conv_qkv, m_conv_ffn, m_w_pool, m_meta_tokens], SMALL_ROWS),
                     _pack([v_conv_qkv, v_conv_ffn, v_w_pool, v_meta_tokens], SMALL_ROWS), "adamw_small")
    r_small = [_unpack(o_, small_shapes) for o_ in r_small]

    repl_shapes = [norm_mix.shape, a_log.shape, dt_bias.shape, head_norm.shape, pool_scale.shape, norm_ffn.shape,
                   norm_final.shape, (1,)]
    rp = lambda name, n: jnp.stack([grads[l][name][0, :n] for l in range(depth)], axis=0)
    part = _pack([rp("norm_mix", D), rp("a_log", H), rp("dt_bias", H), rp("head_norm", DH), rp("pool_scale", D),
                  rp("norm_ffn", D), d_norm_final[0], loss_row[0, 0:1]], REPL_ROWS)
    (l_repl,) = _gather([part], "gather_replicated")
    zero1 = jnp.zeros((1,), F32)
    r_repl = _adamw(l_repl, _pack([norm_mix, a_log, dt_bias, head_norm, pool_scale, norm_ffn, norm_final, zero1], REPL_ROWS),
                    _pack([m_norm_mix, m_a_log, m_dt_bias, m_head_norm, m_pool_scale, m_norm_ffn, m_norm_final, zero1], REPL_ROWS),
                    _pack([v_norm_mix, v_a_log, v_dt_bias, v_head_norm, v_pool_scale, v_norm_ffn, v_norm_final, zero1], REPL_ROWS),
                    "adamw_replicated")
    r_repl = [_unpack(o_, repl_shapes) for o_ in r_repl]
    loss = r_repl[0][7].reshape(())

    def leaf(kind):
        sm, rr = r_small[kind], r_repl[kind]
        return [sm[3], rr[0], r_in[kind], sm[0], rr[1], rr[2], rr[3], sm[2], rr[4], r_out[kind], rr[5], r_up[kind],
                sm[1], r_down[kind], rr[6]]

    return (loss, grad_x, *leaf(0), *leaf(1), *leaf(2), *leaf(3))
```

```python
import jax
import jax.numpy as jnp
from jax import lax
from jax.experimental import pallas as pl
from jax.experimental.pallas import tpu as pltpu

F32 = jnp.float32
BF16 = jnp.bfloat16
HI = lax.Precision.HIGHEST
MESH = pl.DeviceIdType.MESH

D = 1024
H = 8
DH = 128
CH = 64
N_META = 16
LEAD = 48
ROW0 = LEAD + N_META
QKV = 3 * D
POOL_W = 512
POOL_WINDOWS = (2, 4, 8, 16)
FB = 704
NDEV = 8
EPS = 1e-6
MM_TILES = 12
TE = 192
LANE = 128
SUB = 8
VMEM_CAP = 56 << 20

ADAM_LR, ADAM_B1, ADAM_B2, ADAM_EPS, ADAM_WD, ADAM_STEP = 0.001, 0.9, 0.999, 1e-08, 0.01, 10

_NN = (((1,), (0,)), ((), ()))
_NT = (((1,), (1,)), ((), ()))
_TN = (((0,), (0,)), ((), ()))


def _dot(a, b, dims=_NN, precision=None):
    return lax.dot_general(a, b, dims, precision=precision, preferred_element_type=F32)


def _bdot(a, b, dims=_NN):
    return _dot(a.astype(BF16), b.astype(BF16), dims)


def _nbytes(shape, dtype):
    n = 1
    for s in shape:
        n *= s
    return n * jnp.dtype(dtype).itemsize


def _params(sem, block_bytes):
    limit = min(VMEM_CAP, 2 * block_bytes + (20 << 20))
    return pltpu.CompilerParams(dimension_semantics=sem, vmem_limit_bytes=limit)


def _pallas_call(body, *, out_shape, **kw):
    call = pl.pallas_call
    pinned = jax.tree.map(lambda s: pltpu.HBM(s.shape, s.dtype), out_shape)

    def run(*args):
        return call(body, out_shape=pinned, **kw)(*[pltpu.with_memory_space_constraint(a, pltpu.HBM) for a in args])

    return run


def _sigmoid(x):
    return 1.0 / (1.0 + jnp.exp(-x))


def _col_tile(n):
    for t in (1024, 512, 256, 128):
        if n % t == 0:
            return t
    return n


def _matmul(a, b, *, dims, grid, a_spec, b_spec, o_spec, out_shape, name, red_axis=None, res=None):
    def body(*refs):
        if res is None:
            a_ref, b_ref, o_ref = refs
        else:
            a_ref, b_ref, r_ref, o_ref = refs
        part = _dot(a_ref[...], b_ref[...], dims)
        if red_axis is None:
            if res is not None:
                part = part + r_ref[...]
            o_ref[...] = part.astype(o_ref.dtype)
        else:
            r = pl.program_id(red_axis)

            @pl.when(r == 0)
            def _():
                o_ref[...] = part + r_ref[...] if res is not None else part

            @pl.when(r > 0)
            def _():
                o_ref[...] += part

    def blk(spec, arr):
        return _nbytes([s for s in spec.block_shape if s is not None], arr.dtype)

    ins = [a, b] + ([res] if res is not None else [])
    specs = [a_spec, b_spec] + ([o_spec] if res is not None else [])
    nb = blk(a_spec, a) + blk(b_spec, b) + 2 * _nbytes([s for s in o_spec.block_shape if s is not None], F32)
    sem = tuple("arbitrary" if i == red_axis else "parallel" for i in range(len(grid)))
    return _pallas_call(
        body, name=name, grid=grid, in_specs=specs, out_specs=o_spec, out_shape=out_shape,
        compiler_params=_params(sem, nb),
    )(*ins)


def _row_tiles(m, row_bytes, fixed_bytes, temp_row_bytes=0):
    for nt in (MM_TILES // 2, MM_TILES):
        tm = m // nt
        if 2 * (row_bytes * tm + fixed_bytes) + temp_row_bytes * tm <= VMEM_CAP - (10 << 20):
            return nt
    return MM_TILES


def _mm(a, b, out_dtype, name, res=None, dims=_NN):
    m, k = a.shape
    n = b.shape[1] if dims == _NN else b.shape[0]
    tn = _col_tile(n)
    nt = _row_tiles(m, 2 * k + tn * (jnp.dtype(out_dtype).itemsize + (4 if res is not None else 0)), 2 * k * tn, 4 * tn)
    tm = m // nt
    if dims == _NN:
        b_spec = pl.BlockSpec((k, tn), lambda j, i: (0, j))
    else:
        b_spec = pl.BlockSpec((tn, k), lambda j, i: (j, 0))
    return _matmul(
        a, b, dims=dims, grid=(n // tn, nt), a_spec=pl.BlockSpec((tm, k), lambda j, i: (i, 0)), b_spec=b_spec,
        o_spec=pl.BlockSpec((tm, tn), lambda j, i: (i, j)), out_shape=jax.ShapeDtypeStruct((m, n), out_dtype),
        name=name, res=res)


def _mm_tn(a, g, name):
    m, k = a.shape
    n = g.shape[1]
    tn = _col_tile(n)
    nt = _row_tiles(m, 2 * k + 2 * tn, 4 * k * tn)
    tm = m // nt
    return _matmul(
        a, g, dims=_TN, grid=(n // tn, nt), red_axis=1, a_spec=pl.BlockSpec((tm, k), lambda j, i: (i, 0)),
        b_spec=pl.BlockSpec((tm, tn), lambda j, i: (i, j)), o_spec=pl.BlockSpec((k, tn), lambda j, i: (0, j)),
        out_shape=jax.ShapeDtypeStruct((k, n), F32), name=name)


def _mm_up(u, w_up, name):
    t = u.shape[0]
    g = w_up.shape[0]
    nt = _row_tiles(t, 2 * D + 2 * FB, 2 * D * FB, 4 * FB)
    tm = t // nt
    return _matmul(
        u, w_up, dims=_NN, grid=(g, nt), a_spec=pl.BlockSpec((tm, D), lambda g_, i: (i, 0)),
        b_spec=pl.BlockSpec((None, D, FB), lambda g_, i: (g_, 0, 0)),
        o_spec=pl.BlockSpec((None, tm, FB), lambda g_, i: (g_, i, 0)),
        out_shape=jax.ShapeDtypeStruct((g, t, FB), BF16), name=name)


def _mm_blocks_red(a, b, name, dims, res=None):
    g, t, k = a.shape
    n = b.shape[2] if dims == _NN else b.shape[1]
    nt = _row_tiles(t, 2 * k + n * (8 if res is not None else 4), 2 * k * n, 4 * n)
    tm = t // nt
    return _matmul(
        a, b, dims=dims, grid=(nt, g), red_axis=1, a_spec=pl.BlockSpec((None, tm, k), lambda i, g_: (g_, i, 0)),
        b_spec=pl.BlockSpec((None,) + b.shape[1:], lambda i, g_: (g_, 0, 0)),
        o_spec=pl.BlockSpec((tm, n), lambda i, g_: (i, 0)), out_shape=jax.ShapeDtypeStruct((t, n), F32),
        name=name, res=res)


def _mm_to_blocks(a, b, name):
    t, k = a.shape
    g, n, _ = b.shape
    nt = _row_tiles(t, 2 * k + 2 * n, 2 * k * n, 4 * n)
    tm = t // nt
    return _matmul(
        a, b, dims=_NT, grid=(g, nt), a_spec=pl.BlockSpec((tm, k), lambda g_, i: (i, 0)),
        b_spec=pl.BlockSpec((None, n, k), lambda g_, i: (g_, 0, 0)),
        o_spec=pl.BlockSpec((None, tm, n), lambda g_, i: (g_, i, 0)),
        out_shape=jax.ShapeDtypeStruct((g, t, n), BF16), name=name)


def _mm_tn_blocks(a, g, name, a_blocked, g_blocked):
    nb = a.shape[0] if a_blocked else g.shape[0]
    t = a.shape[-2]
    k, n = a.shape[-1], g.shape[-1]
    nt = _row_tiles(t, 2 * k + 2 * n, 4 * k * n)
    tm = t // nt
    a_spec = (pl.BlockSpec((None, tm, k), lambda g_, i: (g_, i, 0)) if a_blocked
              else pl.BlockSpec((tm, k), lambda g_, i: (i, 0)))
    g_spec = (pl.BlockSpec((None, tm, n), lambda g_, i: (g_, i, 0)) if g_blocked
              else pl.BlockSpec((tm, n), lambda g_, i: (i, 0)))
    return _matmul(
        a, g, dims=_TN, grid=(nb, nt), red_axis=1, a_spec=a_spec, b_spec=g_spec,
        o_spec=pl.BlockSpec((None, k, n), lambda g_, i: (g_, 0, 0)),
        out_shape=jax.ShapeDtypeStruct((nb, k, n), F32), name=name)


def _mm_cols(a, b, out_dtype, name, dims):
    t = a.shape[0]
    g = b.shape[0]
    ka = a.shape[1] // g
    n = b.shape[2] if dims == _NN else b.shape[1]
    tm = t // MM_TILES
    return _matmul(
        a, b, dims=dims, grid=(g, MM_TILES), a_spec=pl.BlockSpec((tm, ka), lambda g_, i: (i, g_)),
        b_spec=pl.BlockSpec((None,) + b.shape[1:], lambda g_, i: (g_, 0, 0)),
        o_spec=pl.BlockSpec((tm, n), lambda g_, i: (i, g_)), out_shape=jax.ShapeDtypeStruct((t, g * n), out_dtype),
        name=name)


def _mm_tn_cols(a, g, nblk, name):
    t = a.shape[0]
    ka, n = a.shape[1] // nblk, g.shape[1] // nblk
    tm = t // MM_TILES
    return _matmul(
        a, g, dims=_TN, grid=(nblk, MM_TILES), red_axis=1, a_spec=pl.BlockSpec((tm, ka), lambda g_, i: (i, g_)),
        b_spec=pl.BlockSpec((tm, n), lambda g_, i: (i, g_)), o_spec=pl.BlockSpec((None, ka, n), lambda g_, i: (g_, 0, 0)),
        out_shape=jax.ShapeDtypeStruct((nblk, ka, n), F32), name=name)


def _rows(cols, n=None):
    if n is None:
        return pl.BlockSpec((TE, cols), lambda i: (i, 0))
    return pl.BlockSpec((TE, cols), lambda i: (n - 1 - i, 0))


def _whole(shape):
    return pl.BlockSpec(shape, lambda *_: (0,) * len(shape))


def _row_ids(i, rows=TE):
    return i * rows + lax.broadcasted_iota(jnp.int32, (rows, 1), 0)


def _rmsnorm_fwd(h, gain, name):
    t = h.shape[0]

    def body(h_ref, g_ref, u_ref):
        x = h_ref[...]
        r = lax.rsqrt(jnp.mean(x * x, axis=-1, keepdims=True) + EPS)
        u_ref[...] = (x * r * g_ref[...]).astype(BF16)

    return _pallas_call(
        body, name=name, grid=(t // TE,), in_specs=[_rows(D), _whole((1, D))], out_specs=_rows(D),
        out_shape=jax.ShapeDtypeStruct((t, D), BF16), compiler_params=_params(("parallel",), 3 * TE * D * 4),
    )(h, gain)


def _rmsnorm_bwd(x, du, dres, gain, name):
    t = x.shape[0]

    def body(x_ref, du_ref, dr_ref, g_ref, dx_ref, dxb_ref, dg_ref):
        i = pl.program_id(0)
        xv = x_ref[...]
        r = lax.rsqrt(jnp.mean(xv * xv, axis=-1, keepdims=True) + EPS)
        gdy = du_ref[...] * g_ref[...]
        dx = dr_ref[...] + r * gdy - xv * (r * r * r) * jnp.mean(xv * gdy, axis=-1, keepdims=True)
        dx = jnp.where(_row_ids(i) >= LEAD, dx, 0.0)
        dx_ref[...] = dx
        dxb_ref[...] = dx.astype(BF16)
        part = jnp.sum(du_ref[...] * xv * r, axis=0, keepdims=True)

        @pl.when(i == 0)
        def _():
            dg_ref[...] = part

        @pl.when(i > 0)
        def _():
            dg_ref[...] += part

    return _pallas_call(
        body, name=name, grid=(t // TE,), in_specs=[_rows(D), _rows(D), _rows(D), _whole((1, D))],
        out_specs=[_rows(D), _rows(D), _whole((1, D))],
        out_shape=[jax.ShapeDtypeStruct((t, D), F32), jax.ShapeDtypeStruct((t, D), BF16),
                   jax.ShapeDtypeStruct((1, D), F32)],
        compiler_params=_params(("arbitrary",), 5 * TE * D * 4),
    )(x, du, dres, gain)


def _loss_bwd(h, target, gain, name):
    t = h.shape[0]

    def body(h_ref, t_ref, g_ref, dx_ref, dxb_ref, dg_ref, loss_ref):
        i = pl.program_id(0)
        xv = h_ref[...]
        gain_v = g_ref[...]
        r = lax.rsqrt(jnp.mean(xv * xv, axis=-1, keepdims=True) + EPS)
        real = _row_ids(i) >= ROW0
        err = jnp.where(real, xv * r * gain_v - t_ref[...], 0.0)
        dy = err * (1.0 / D)
        gdy = dy * gain_v
        dx = r * gdy - xv * (r * r * r) * jnp.mean(xv * gdy, axis=-1, keepdims=True)
        dx_ref[...] = dx
        dxb_ref[...] = dx.astype(BF16)
        dgp = jnp.sum(dy * xv * r, axis=0, keepdims=True)
        lp = 0.5 * jnp.sum(jnp.mean(err * err, axis=-1, keepdims=True), axis=0, keepdims=True)

        @pl.when(i == 0)
        def _():
            dg_ref[...] = dgp
            loss_ref[...] = jnp.broadcast_to(lp, (1, LANE))

        @pl.when(i > 0)
        def _():
            dg_ref[...] += dgp
            loss_ref[...] += jnp.broadcast_to(lp, (1, LANE))

    return _pallas_call(
        body, name=name, grid=(t // TE,), in_specs=[_rows(D), _rows(D), _whole((1, D))],
        out_specs=[_rows(D), _rows(D), _whole((1, D)), _whole((1, LANE))],
        out_shape=[jax.ShapeDtypeStruct((t, D), F32), jax.ShapeDtypeStruct((t, D), BF16),
                   jax.ShapeDtypeStruct((1, D), F32), jax.ShapeDtypeStruct((1, LANE), F32)],
        compiler_params=_params(("arbitrary",), 4 * TE * D * 4),
    )(h, target, gain)


def _seq_scratch(cols):
    return pltpu.VMEM((-(-cols // LANE), TE + SUB, LANE), F32)


def _taps(scr, c, wd, first, n, k):
    return [scr[c, first - (k - 1) + j:first - (k - 1) + j + n, 0:wd] for j in range(k)]


def _stage_history(scr, i):
    @pl.when(i == 0)
    def _():
        scr[...] = jnp.zeros(scr.shape, F32)

    @pl.when(i > 0)
    def _():
        scr[:, 0:SUB, :] = scr[:, TE:TE + SUB, :]


def _stage_future(scr, i):
    @pl.when(i == 0)
    def _():
        scr[...] = jnp.zeros(scr.shape, F32)

    @pl.when(i > 0)
    def _():
        scr[:, TE:TE + SUB, :] = scr[:, 0:SUB, :]


def _conv(tp, w):
    out = w[0:1] * tp[0]
    for j in range(1, len(tp)):
        out = out + w[j:j + 1] * tp[j]
    return out


def _conv_t(ds, c, wd, w):
    k = w.shape[0]
    out = w[k - 1:k] * ds[c, 0:TE, 0:wd]
    for j in range(k - 1):
        out = out + w[j:j + 1] * ds[c, k - 1 - j:k - 1 - j + TE, 0:wd]
    return out


def _gdn_pre_fwd(x, w, name):
    t = x.shape[0]

    def body(x_ref, w_ref, o_ref, xs):
        _stage_history(xs, pl.program_id(0))
        for hh in range(3 * H):
            sl = slice(hh * DH, (hh + 1) * DH)
            xs[hh, SUB:SUB + TE, :] = x_ref[:, sl].astype(F32)
            cv = _conv(_taps(xs, hh, DH, SUB, TE, 4), w_ref[:, sl])
            s = cv * _sigmoid(cv)
            if hh < 2 * H:
                s = s * lax.rsqrt(jnp.sum(s * s, axis=-1, keepdims=True) + EPS)
                if hh < H:
                    s = s * (DH ** -0.5)
            o_ref[:, sl] = s

    return _pallas_call(
        body, name=name, grid=(t // TE,), in_specs=[_rows(QKV), _whole((4, QKV))], out_specs=_rows(QKV),
        out_shape=jax.ShapeDtypeStruct((t, QKV), F32), scratch_shapes=[_seq_scratch(QKV)],
        compiler_params=_params(("arbitrary",), TE * QKV * 8),
    )(x, w)


def _gdn_pre_bwd(x, w, dqkv, name):
    t = x.shape[0]
    n = t // TE
    hb = TE // 16

    def body(x_ref, xp_ref, w_ref, d_ref, dx_ref, dw_ref, xs, ds):
        i = pl.program_id(0)

        @pl.when(i == 0)
        def _():
            dw_ref[...] = jnp.zeros_like(dw_ref)

        _stage_future(ds, i)
        for hh in range(3 * H):
            sl = slice(hh * DH, (hh + 1) * DH)
            xs[hh, 0:SUB, :] = jnp.where(i == n - 1, 0.0, xp_ref[SUB:2 * SUB, sl].astype(F32))
            xs[hh, SUB:SUB + TE, :] = x_ref[:, sl].astype(F32)
            wv = w_ref[:, sl]
            tp = _taps(xs, hh, DH, SUB, TE, 4)
            cv = _conv(tp, wv)
            sg = _sigmoid(cv)
            s = cv * sg
            dsv = d_ref[:, sl]
            if hh < 2 * H:
                if hh < H:
                    dsv = dsv * (DH ** -0.5)
                r = lax.rsqrt(jnp.sum(s * s, axis=-1, keepdims=True) + EPS)
                dsv = r * dsv - s * (r * r * r) * jnp.sum(s * dsv, axis=-1, keepdims=True)
            dcv = dsv * (sg * (1.0 + cv * (1.0 - sg)))
            ds[hh, 0:TE, :] = dcv
            dx_ref[:, sl] = _conv_t(ds, hh, DH, wv).astype(BF16)
            dw_ref[:, sl] += jnp.concatenate([jnp.sum(tp[j] * dcv, axis=0, keepdims=True) for j in range(4)], axis=0)

    return _pallas_call(
        body, name=name, grid=(n,),
        in_specs=[_rows(QKV, n), pl.BlockSpec((16, QKV), lambda i: (jnp.maximum((n - 1 - i) * hb - 1, 0), 0)),
                  _whole((4, QKV)), _rows(QKV, n)],
        out_specs=[_rows(QKV, n), _whole((4, QKV))],
        out_shape=[jax.ShapeDtypeStruct((t, QKV), BF16), jax.ShapeDtypeStruct((4, QKV), F32)],
        scratch_shapes=[_seq_scratch(QKV), _seq_scratch(QKV)],
        compiler_params=_params(("arbitrary",), TE * QKV * 14),
    )(x, x, w, dqkv)


def _softplus(x):
    return jnp.maximum(x, 0.0) + jnp.log(1.0 + jnp.exp(-jnp.abs(x)))


def _gates_fwd(ba, a_row, dt_row, name):
    t = ba.shape[0]

    def body(ba_ref, a_ref, dt_ref, b_out, g_out):
        real = _row_ids(pl.program_id(0)) >= LEAD
        b_out[...] = jnp.where(real, _sigmoid(ba_ref[:, 0:LANE]), 0.0)
        g = -jnp.exp(a_ref[...]) * _softplus(ba_ref[:, LANE:2 * LANE] + dt_ref[...])
        g_out[...] = jnp.where(real, g, 0.0)

    return _pallas_call(
        body, name=name, grid=(t // TE,), in_specs=[_rows(2 * LANE), _whole((1, LANE)), _whole((1, LANE))],
        out_specs=[_rows(LANE), _rows(LANE)],
        out_shape=[jax.ShapeDtypeStruct((t, LANE), F32), jax.ShapeDtypeStruct((t, LANE), F32)],
        compiler_params=_params(("parallel",), TE * LANE * 16),
    )(ba, a_row, dt_row)


def _gates_bwd(ba, a_row, dt_row, dbeta, dg, name):
    t = ba.shape[0]

    def body(ba_ref, a_ref, dt_ref, db_ref, dg_ref, dba_ref, da_out, ddt_out):
        i = pl.program_id(0)
        real = _row_ids(i) >= LEAD
        beta = _sigmoid(ba_ref[:, 0:LANE])
        draw_b = jnp.where(real, db_ref[...] * beta * (1.0 - beta), 0.0)
        pre = ba_ref[:, LANE:2 * LANE] + dt_ref[...]
        neg_a = -jnp.exp(a_ref[...])
        dgv = jnp.where(real, dg_ref[...], 0.0)
        draw_a = dgv * neg_a * _sigmoid(pre)
        dba_ref[:, 0:LANE] = draw_b.astype(BF16)
        dba_ref[:, LANE:2 * LANE] = draw_a.astype(BF16)
        dal = jnp.sum(dgv * neg_a * _softplus(pre), axis=0, keepdims=True)
        ddt = jnp.sum(draw_a, axis=0, keepdims=True)

        @pl.when(i == 0)
        def _():
            da_out[...] = dal
            ddt_out[...] = ddt

        @pl.when(i > 0)
        def _():
            da_out[...] += dal
            ddt_out[...] += ddt

    return _pallas_call(
        body, name=name, grid=(t // TE,),
        in_specs=[_rows(2 * LANE), _whole((1, LANE)), _whole((1, LANE)), _rows(LANE), _rows(LANE)],
        out_specs=[_rows(2 * LANE), _whole((1, LANE)), _whole((1, LANE))],
        out_shape=[jax.ShapeDtypeStruct((t, 2 * LANE), BF16), jax.ShapeDtypeStruct((1, LANE), F32),
                   jax.ShapeDtypeStruct((1, LANE), F32)],
        compiler_params=_params(("arbitrary",), TE * LANE * 24),
    )(ba, a_row, dt_row, dbeta, dg)


_OFFSETS = [(dx, dy, dc) for dx in (0, 1) for dy in (0, 1) for dc in (0, 1)][1:]
NPEER = len(_OFFSETS)
ANY_SPEC = pl.BlockSpec(memory_space=pl.ANY)


def _place():
    return lax.axis_index("x"), lax.axis_index("y"), lax.axis_index("c")


def _index(p):
    return 4 * p[0] + 2 * p[1] + p[2]


def _comm_scratch(n):
    return [pltpu.SemaphoreType.DMA((n * NPEER,)), pltpu.SemaphoreType.DMA((n * NPEER,)), pltpu.SemaphoreType.DMA((n,))]


def _scatter_copies(ins, outs, send, recv):
    me = _place()
    mi = _index(me)
    res = []
    for j, d in enumerate(_OFFSETS):
        peer = tuple(1 - v if bit else v for v, bit in zip(me, d))
        pi = _index(peer)
        for k in range(len(ins)):
            sem = k * NPEER + j
            mine = pltpu.make_async_remote_copy(src_ref=ins[k].at[pi], dst_ref=outs[k].at[mi], send_sem=send.at[sem],
                                                recv_sem=recv.at[sem], device_id=peer, device_id_type=MESH)
            theirs = pltpu.make_async_remote_copy(src_ref=ins[k].at[pi], dst_ref=outs[k].at[pi], send_sem=send.at[sem],
                                                  recv_sem=recv.at[sem], device_id=peer, device_id_type=MESH)
            res.append((mine, theirs))
    return res


def _scatter_own(ins, outs, loc):
    mi = _index(_place())
    return [pltpu.make_async_copy(ins[k].at[mi], outs[k].at[mi], loc.at[k]) for k in range(len(ins))]


def _scatter_start(ins, outs, send, recv, loc):
    for cp in _scatter_own(ins, outs, loc):
        cp.start()
    for mine, _ in _scatter_copies(ins, outs, send, recv):
        mine.start()


def _scatter_wait(ins, outs, send, recv, loc):
    cps = _scatter_copies(ins, outs, send, recv)
    for _, theirs in cps:
        theirs.wait_recv()
    for mine, _ in cps:
        mine.wait_send()
    for cp in _scatter_own(ins, outs, loc):
        cp.wait()


def _gather_parts(ins, outs, send, recv):
    x, y, c = _place()
    chips = [(1 - x, y), (x, 1 - y), (1 - x, 1 - y)]

    def cp(k, slot, src, block, to):
        return pltpu.make_async_remote_copy(src_ref=src, dst_ref=outs[k].at[_index(block)], send_sem=send.at[k * NPEER + slot],
                                            recv_sem=recv.at[k * NPEER + slot], device_id=to, device_id_type=MESH)

    return (x, y, c), (x, y, 1 - c), chips, cp


def _gather_start(ins, outs, send, recv, loc):
    me, sib, chips, cp = _gather_parts(ins, outs, send, recv)
    for k in range(len(ins)):
        pltpu.make_async_copy(ins[k], outs[k].at[_index(me)], loc.at[k]).start()
        cp(k, 0, ins[k], me, sib).start()
        for j, chip in enumerate(chips):
            cp(k, 1 + j, ins[k], me, (*chip, me[2])).start()


def _gather_forward(ins, outs, send, recv, loc):
    me, sib, chips, cp = _gather_parts(ins, outs, send, recv)
    for j, chip in enumerate(chips):
        blk = (*chip, me[2])
        for k in range(len(ins)):
            cp(k, 1 + j, ins[k], blk, me).wait_recv()
            cp(k, 4 + j, outs[k].at[_index(blk)], blk, sib).start()


def _gather_finish(ins, outs, send, recv, loc):
    me, sib, chips, cp = _gather_parts(ins, outs, send, recv)
    for k in range(len(ins)):
        cp(k, 0, ins[k], sib, me).wait_recv()
        for j, chip in enumerate(chips):
            cp(k, 4 + j, ins[k], (*chip, sib[2]), me).wait_recv()
        cp(k, 0, ins[k], me, sib).wait_send()
        for j, chip in enumerate(chips):
            cp(k, 1 + j, ins[k], me, (*chip, me[2])).wait_send()
            cp(k, 4 + j, outs[k].at[_index((*chip, me[2]))], (*chip, me[2]), sib).wait_send()
        pltpu.make_async_copy(ins[k], outs[k].at[_index(me)], loc.at[k]).wait()


def _gathered_shapes(arrs):
    return [jax.ShapeDtypeStruct((NDEV,) + a.shape, a.dtype) for a in arrs]


def _gather(arrs, name):
    n = len(arrs)

    def body(*refs):
        ins, outs, sems = refs[:n], refs[n:2 * n], refs[2 * n:]
        _gather_start(ins, outs, *sems)
        _gather_forward(ins, outs, *sems)
        _gather_finish(ins, outs, *sems)

    return _pallas_call(body, name=name, in_specs=[ANY_SPEC] * n, out_specs=[ANY_SPEC] * n,
                          out_shape=_gathered_shapes(arrs), scratch_shapes=_comm_scratch(n))(*arrs)


def _scatter(arrs, name):
    n = len(arrs)

    def body(*refs):
        ins, outs, sems = refs[:n], refs[n:2 * n], refs[2 * n:]
        _scatter_start(ins, outs, *sems)
        _scatter_wait(ins, outs, *sems)

    return _pallas_call(body, name=name, in_specs=[ANY_SPEC] * n, out_specs=[ANY_SPEC] * n,
                          out_shape=[jax.ShapeDtypeStruct(a.shape, a.dtype) for a in arrs],
                          scratch_shapes=_comm_scratch(n))(*arrs)


_BNN = (((2,), (1,)), ((0,), (0,)))
_BNT = (((2,), (2,)), ((0,), (0,)))
_BTN = (((1,), (1,)), ((0,), (0,)))


def _split(a):
    hi = a.astype(BF16)
    return hi, (a - hi.astype(F32)).astype(BF16)


def _dot3(a, b):
    ah, al = _split(a)
    bh, bl = _split(b)
    m = a.shape[1]
    r = _dot(jnp.concatenate([ah, al], axis=1), bh, _BNN)
    return r[:, :m] + r[:, m:] + _dot(ah, bl, _BNN)


GC = 3


def _rows_of(c):
    return slice(c * CH, (c + 1) * CH)


def _heads(ref, off):
    return jnp.stack([ref[_rows_of(c), off + h * DH:off + (h + 1) * DH] for c in range(GC) for h in range(H)])


def _cols(arrs):
    return jnp.stack([a[:, h:h + 1] for a in arrs for h in range(H)])


def _lanes(a):
    lane = lax.broadcasted_iota(jnp.int32, (CH, LANE), 1)
    out = jnp.zeros((CH, LANE), F32)
    for h in range(H):
        out = jnp.where(lane == h, a[h], out)
    return out


def _chunk_prep(qkv_ref, b_ref, g_ref):
    row = lax.broadcasted_iota(jnp.int32, (CH, CH), 0)
    col = lax.broadcasted_iota(jnp.int32, (CH, CH), 1)
    incl, strict = row >= col, row > col
    gcs = [_dot(incl.astype(F32), g_ref[_rows_of(c), :], precision=HI) for c in range(GC)]
    q, k, v = _heads(qkv_ref, 0), _heads(qkv_ref, D), _heads(qkv_ref, 2 * D)
    bcol, gcol = _cols([b_ref[_rows_of(c), :] for c in range(GC)]), _cols(gcs)
    grow = jnp.stack([gct[h:h + 1, :] for gct in [gc.T for gc in gcs] for h in range(H)])
    glast = _cols([gc[CH - 1:CH, :] for gc in gcs])
    dec = jnp.exp(jnp.where(incl[None], gcol - grow, -1e30))
    kb = k * bcol
    ab = _bdot(jnp.concatenate([kb, q], axis=1), k, _BNT)
    egc, ekc = jnp.exp(gcol), jnp.exp(glast - gcol)
    return dict(row=row, col=col, strict=strict[None], q=q, k=k, v=v, bcol=bcol, dec=dec, kb=kb,
                lm=jnp.where(strict[None], ab[:, :CH] * dec, 0.0), qk=ab[:, CH:] * dec, egc=egc, ekc=ekc,
                gth=jnp.exp(glast), qd=q * egc, kd=k * ekc, vb=v * bcol, kbg=kb * egc)


def _unit_lower_inverse(lm, eye):
    n = -lm
    x = eye + n
    pw = _dot3(n, n)
    for it in range(5):
        if it < 4:
            xp = _dot3(jnp.concatenate([x, pw], axis=1), pw)
            x = x + xp[:, :CH]
            pw = xp[:, CH:]
        else:
            x = x + _dot3(x, pw)
    return x


def _gdn_fwd(qkv, beta, g, name, gather=()):
    t = qkv.shape[0]
    nc = t // CH
    ns = nc // GC
    ng = len(gather)

    def body(qkv_ref, b_ref, g_ref, *rest):
        c_ins, (o_ref, sin_ref, vn_ref, ti_ref, w_ref) = rest[:ng], rest[ng:ng + 5]
        c_outs, state, sems = rest[ng + 5:2 * ng + 5], rest[2 * ng + 5], rest[2 * ng + 6:]
        step = pl.program_id(0)

        @pl.when(step == 0)
        def _():
            state[...] = jnp.zeros_like(state)
            if ng:
                _gather_start(c_ins, c_outs, *sems)

        if ng:
            @pl.when(step == max(ns - 4, 0))
            def _():
                _gather_forward(c_ins, c_outs, *sems)

            @pl.when(step == ns - 1)
            def _():
                _gather_finish(c_ins, c_outs, *sems)

        pr = _chunk_prep(qkv_ref, b_ref, g_ref)
        tinv = _unit_lower_inverse(pr["lm"], (pr["row"] == pr["col"]).astype(F32)[None])
        uw = _bdot(tinv, jnp.concatenate([pr["vb"], pr["kbg"]], axis=2), _BNN)
        u, w = uw[:, :, :DH], uw[:, :, DH:]
        s = state[...]
        for c in range(GC):
            hs = slice(c * H, (c + 1) * H)
            ws = _bdot(jnp.concatenate([w[hs], pr["qd"][hs]], axis=1), s, _BNN)
            vn = u[hs] - ws[:, :CH]
            o = ws[:, CH:] + _bdot(pr["qk"][hs], vn, _BNN)
            sin_ref[c] = s
            ti_ref[c] = tinv[hs]
            s = s * pr["gth"][hs] + _bdot(pr["kd"][hs], vn, _BTN)
            for h in range(H):
                sl = slice(h * DH, (h + 1) * DH)
                o_ref[_rows_of(c), sl] = o[h]
                vn_ref[_rows_of(c), sl] = vn[h]
                w_ref[_rows_of(c), sl] = w[c * H + h]
        state[...] = s

    chunk = lambda cols: pl.BlockSpec((GC * CH, cols), lambda c: (c, 0))
    outs = _pallas_call(
        body, name=name, grid=(ns,), in_specs=[chunk(QKV), chunk(LANE), chunk(LANE)] + [ANY_SPEC] * ng,
        out_specs=[chunk(D), pl.BlockSpec((GC, H, DH, DH), lambda c: (c, 0, 0, 0)), chunk(D),
                   pl.BlockSpec((GC, H, CH, CH), lambda c: (c, 0, 0, 0)), chunk(D)] + [ANY_SPEC] * ng,
        out_shape=[jax.ShapeDtypeStruct((t, D), F32), jax.ShapeDtypeStruct((nc, H, DH, DH), F32),
                   jax.ShapeDtypeStruct((t, D), F32), jax.ShapeDtypeStruct((nc, H, CH, CH), F32),
                   jax.ShapeDtypeStruct((t, D), F32)] + _gathered_shapes(gather),
        scratch_shapes=[pltpu.VMEM((H, DH, DH), F32)] + (_comm_scratch(ng) if ng else []),
        compiler_params=_params(("arbitrary",), 12 << 20),
    )(qkv, beta, g, *gather)
    return outs[:5], outs[5:]


def _gdn_bwd(qkv, beta, g, do, s_in, vnew, tinv, wsv, name, scatter=()):
    t = qkv.shape[0]
    nsteps = t // CH // GC
    ns = len(scatter)

    def body(qkv_ref, b_ref, g_ref, do_ref, sin_ref, vn_ref, ti_ref, w_ref, *rest):
        c_ins, (dqkv_ref, db_ref, dg_ref) = rest[:ns], rest[ns:ns + 3]
        c_outs, dstate, sems = rest[ns + 3:2 * ns + 3], rest[2 * ns + 3], rest[2 * ns + 4:]
        step = pl.program_id(0)

        @pl.when(step == 0)
        def _():
            dstate[...] = jnp.zeros_like(dstate)
            if ns:
                _scatter_start(c_ins, c_outs, *sems)

        if ns:
            @pl.when(step == nsteps - 1)
            def _():
                _scatter_wait(c_ins, c_outs, *sems)

        pr = _chunk_prep(qkv_ref, b_ref, g_ref)
        ti = jnp.concatenate([ti_ref[c] for c in range(GC)], axis=0)
        s = jnp.concatenate([sin_ref[c] for c in range(GC)], axis=0)
        w, vn, doh = _heads(w_ref, 0), _heads(vn_ref, 0), _heads(do_ref, 0)
        dqd = _bdot(doh, s, _BNT)
        dqk = _bdot(doh, vn, _BNT)
        qk_do = _bdot(pr["qk"], doh, _BTN)
        qd_do = _bdot(pr["qd"], doh, _BTN)
        ds = dstate[...]
        dvn_c, dkd_c, dw_c, dgt_c = [None] * GC, [None] * GC, [None] * GC, [None] * GC
        for c in reversed(range(GC)):
            hs = slice(c * H, (c + 1) * H)
            dvn_c[c] = _bdot(pr["kd"][hs], ds, _BNN) + qk_do[hs]
            dkd_c[c] = _bdot(vn[hs], ds, _BNT)
            dw_c[c] = -_bdot(dvn_c[c], s[hs], _BNT)
            dgt_c[c] = jnp.sum(jnp.sum(ds * s[hs], axis=2, keepdims=True), axis=1, keepdims=True)
            ds = ds * pr["gth"][hs] + qd_do[hs] - _bdot(w[hs], dvn_c[c], _BTN)
        dstate[...] = ds
        dvn, dkd, dw, dgt = (jnp.concatenate(parts, axis=0) for parts in (dvn_c, dkd_c, dw_c, dgt_c))
        duw = jnp.concatenate([dvn, dw], axis=2)
        dvk = _bdot(ti, duw, _BTN)
        dvb, dkbg = dvk[:, :, :DH], dvk[:, :, DH:]
        dti = _bdot(duw, jnp.concatenate([pr["vb"], pr["kbg"]], axis=2), _BNT)
        dl = -_dot(_dot(ti, dti, _BTN, precision=HI), ti, _BNT, precision=HI)
        dl = jnp.where(pr["strict"], dl, 0.0)
        dab = jnp.concatenate([dl * pr["dec"], dqk * pr["dec"]], axis=1)
        r1 = _bdot(dab, pr["k"], _BNN)
        dkb = r1[:, :CH] + dkbg * pr["egc"]
        dq = r1[:, CH:] + dqd * pr["egc"]
        dk = _bdot(dab, jnp.concatenate([pr["kb"], pr["q"]], axis=1), _BTN) + dkb * pr["bcol"] + dkd * pr["ekc"]
        m = dl * pr["lm"] + dqk * pr["qk"]
        colsum = _dot(m, jnp.ones((GC * H, CH, LANE), F32), _BTN, precision=HI)[:, :, 0:1]
        kdsum = jnp.sum(dkd * pr["kd"], axis=2, keepdims=True)
        dgc = (jnp.sum(m, axis=2, keepdims=True) - colsum + jnp.sum(dkbg * pr["kbg"], axis=2, keepdims=True)
               + jnp.sum(dqd * pr["qd"], axis=2, keepdims=True) - kdsum)
        dglast = jnp.sum(kdsum, axis=1, keepdims=True) + dgt * pr["gth"]
        last_row = lax.broadcasted_iota(jnp.int32, (1, CH, 1), 1) == CH - 1
        dgc = dgc + jnp.where(last_row, dglast, 0.0)
        dbeta = jnp.sum(dkb * pr["k"], axis=2, keepdims=True) + jnp.sum(dvb * pr["v"], axis=2, keepdims=True)
        dv = dvb * pr["bcol"]
        upper = (pr["row"] <= pr["col"]).astype(F32)
        for c in range(GC):
            hs = slice(c * H, (c + 1) * H)
            for h in range(H):
                dqkv_ref[_rows_of(c), h * DH:(h + 1) * DH] = dq[c * H + h]
                dqkv_ref[_rows_of(c), D + h * DH:D + (h + 1) * DH] = dk[c * H + h]
                dqkv_ref[_rows_of(c), 2 * D + h * DH:2 * D + (h + 1) * DH] = dv[c * H + h]
            db_ref[_rows_of(c), :] = _lanes(dbeta[hs])
            dg_ref[_rows_of(c), :] = _dot(upper, _lanes(dgc[hs]), precision=HI)

    chunk = lambda cols: pl.BlockSpec((GC * CH, cols), lambda c: (nsteps - 1 - c, 0))
    sq = lambda a, b: pl.BlockSpec((GC, H, a, b), lambda c: (nsteps - 1 - c, 0, 0, 0))
    outs = _pallas_call(
        body, name=name, grid=(nsteps,),
        in_specs=[chunk(QKV), chunk(LANE), chunk(LANE), chunk(D), sq(DH, DH), chunk(D), sq(CH, CH), chunk(D)] + [ANY_SPEC] * ns,
        out_specs=[chunk(QKV), chunk(LANE), chunk(LANE)] + [ANY_SPEC] * ns,
        out_shape=[jax.ShapeDtypeStruct((t, QKV), F32), jax.ShapeDtypeStruct((t, LANE), F32),
                   jax.ShapeDtypeStruct((t, LANE), F32)] + [jax.ShapeDtypeStruct(a.shape, a.dtype) for a in scatter],
        scratch_shapes=[pltpu.VMEM((H, DH, DH), F32)] + (_comm_scratch(ns) if ns else []),
        compiler_params=_params(("arbitrary",), 16 << 20),
    )(qkv, beta, g, do, s_in, vnew, tinv, wsv, *scatter)
    return outs[:3], outs[3:]


def _pool_counts(row_ids, win):
    return jnp.minimum(jnp.maximum(row_ids - LEAD, 0) + 1, win).astype(F32)


def _pool_fwd(p, name):
    t = p.shape[0]
    ext = TE + 16

    def body(p_ref, o_ref, carry):
        i = pl.program_id(0)

        @pl.when(i == 0)
        def _():
            carry[...] = jnp.zeros_like(carry)

        ids = _row_ids(i)
        for gi, win in enumerate(POOL_WINDOWS):
            sl = slice(gi * LANE, (gi + 1) * LANE)
            xv = p_ref[:, sl]
            s = jnp.concatenate([carry[:, sl], xv], axis=0)
            sh = 1
            while sh < win:
                s = s + pltpu.roll(s, sh, 0)
                sh *= 2
            o_ref[:, sl] = (s[16:ext] / _pool_counts(ids, win) - xv).astype(BF16)
            carry[:, sl] = xv[TE - 16:TE]

    return _pallas_call(
        body, name=name, grid=(t // TE,), in_specs=[_rows(POOL_W)], out_specs=_rows(POOL_W),
        out_shape=jax.ShapeDtypeStruct((t, POOL_W), BF16), scratch_shapes=[pltpu.VMEM((16, POOL_W), F32)],
        compiler_params=_params(("arbitrary",), TE * POOL_W * 8),
    )(p)


def _pool_bwd(dpo, name):
    t = dpo.shape[0]
    n = t // TE
    ext = TE + 16

    def body(d_ref, o_ref, carry):
        i = pl.program_id(0)

        @pl.when(i == 0)
        def _():
            carry[...] = jnp.zeros_like(carry)

        ids = _row_ids(n - 1 - i)
        for gi, win in enumerate(POOL_WINDOWS):
            sl = slice(gi * LANE, (gi + 1) * LANE)
            dv = d_ref[:, sl]
            rv = dv / _pool_counts(ids, win)
            s = jnp.concatenate([rv, carry[:, sl]], axis=0)
            sh = 1
            while sh < win:
                s = s + pltpu.roll(s, ext - sh, 0)
                sh *= 2
            o_ref[:, sl] = (s[0:TE] - dv).astype(BF16)
            carry[:, sl] = rv[0:16]

    return _pallas_call(
        body, name=name, grid=(n,), in_specs=[_rows(POOL_W, n)], out_specs=_rows(POOL_W, n),
        out_shape=jax.ShapeDtypeStruct((t, POOL_W), BF16), scratch_shapes=[pltpu.VMEM((16, POOL_W), F32)],
        compiler_params=_params(("arbitrary",), TE * POOL_W * 8),
    )(dpo)


def _post_fwd(o, z, gate, pm, hn, ps, name):
    t = o.shape[0]

    def body(o_ref, z_ref, g_ref, pm_ref, hn_ref, ps_ref, y_ref):
        for h in range(H):
            sl = slice(h * DH, (h + 1) * DH)
            ov = o_ref[:, sl]
            zv = z_ref[:, sl].astype(F32)
            r = lax.rsqrt(jnp.mean(ov * ov, axis=-1, keepdims=True) + EPS)
            ya = ov * r * hn_ref[...] * (zv * _sigmoid(zv))
            ga = _sigmoid(g_ref[:, sl].astype(F32))
            gb = _sigmoid(g_ref[:, D + h * DH:D + (h + 1) * DH].astype(F32))
            y_ref[:, sl] = (ga * ya + gb * (pm_ref[:, sl] * ps_ref[:, sl])).astype(BF16)

    return _pallas_call(
        body, name=name, grid=(t // TE,),
        in_specs=[_rows(D), _rows(D), _rows(2 * D), _rows(D), _whole((1, DH)), _whole((1, D))], out_specs=_rows(D),
        out_shape=jax.ShapeDtypeStruct((t, D), BF16), compiler_params=_params(("parallel",), TE * D * 16),
    )(o, z, gate, pm, hn, ps)


def _post_bwd(dy, o, z, gate, pm, hn, ps, name):
    t = o.shape[0]

    def body(dy_ref, o_ref, z_ref, g_ref, pm_ref, hn_ref, ps_ref, do_ref, dz_ref, dgate_ref, dpm_ref, dhn_ref, dps_ref):
        i = pl.program_id(0)

        @pl.when(i == 0)
        def _():
            dhn_ref[...] = jnp.zeros_like(dhn_ref)
            dps_ref[...] = jnp.zeros_like(dps_ref)

        hnv = hn_ref[...]
        dhn = jnp.zeros((1, DH), F32)
        for h in range(H):
            sl = slice(h * DH, (h + 1) * DH)
            slb = slice(D + h * DH, D + (h + 1) * DH)
            dyv = dy_ref[:, sl]
            ov = o_ref[:, sl]
            zv = z_ref[:, sl].astype(F32)
            r = lax.rsqrt(jnp.mean(ov * ov, axis=-1, keepdims=True) + EPS)
            sz = _sigmoid(zv)
            silu = zv * sz
            on = ov * r
            ya = on * hnv * silu
            ga = _sigmoid(g_ref[:, sl].astype(F32))
            gb = _sigmoid(g_ref[:, slb].astype(F32))
            pmv = pm_ref[:, sl]
            psv = ps_ref[:, sl]
            dya = dyv * ga
            dyb = dyv * gb
            dgate_ref[:, sl] = (dyv * ya * ga * (1.0 - ga)).astype(BF16)
            dgate_ref[:, slb] = (dyv * (pmv * psv) * gb * (1.0 - gb)).astype(BF16)
            tt = dya * hnv * silu
            do_ref[:, sl] = r * tt - ov * (r * r * r) * jnp.mean(ov * tt, axis=-1, keepdims=True)
            dz_ref[:, sl] = (dya * on * hnv * (sz * (1.0 + zv * (1.0 - sz)))).astype(BF16)
            dhn = dhn + jnp.sum(dya * on * silu, axis=0, keepdims=True)
            dps_ref[:, sl] += jnp.sum(dyb * pmv, axis=0, keepdims=True)
            dpm_ref[:, sl] = (dyb * psv).astype(BF16)
        dhn_ref[...] += dhn

    return _pallas_call(
        body, name=name, grid=(t // TE,),
        in_specs=[_rows(D), _rows(D), _rows(D), _rows(2 * D), _rows(D), _whole((1, DH)), _whole((1, D))],
        out_specs=[_rows(D), _rows(D), _rows(2 * D), _rows(D), _whole((1, DH)), _whole((1, D))],
        out_shape=[jax.ShapeDtypeStruct((t, D), F32), jax.ShapeDtypeStruct((t, D), BF16),
                   jax.ShapeDtypeStruct((t, 2 * D), BF16), jax.ShapeDtypeStruct((t, D), BF16),
                   jax.ShapeDtypeStruct((1, DH), F32), jax.ShapeDtypeStruct((1, D), F32)],
        compiler_params=_params(("arbitrary",), TE * D * 28),
    )(dy, o, z, gate, pm, hn, ps)


_FB_COLS = [(c, min(c + LANE, FB)) for c in range(0, FB, LANE)]


def _mlp_act_fwd(hid, cw, name):
    t = hid.shape[1]
    n = t // TE

    def body(hg_ref, hv_ref, wg_ref, wv_ref, a_ref, xg, xv):
        i = pl.program_id(1)
        _stage_history(xg, i)
        _stage_history(xv, i)
        for c, (c0, c1) in enumerate(_FB_COLS):
            sl, wd = slice(c0, c1), c1 - c0
            xg[c, SUB:SUB + TE, 0:wd] = hg_ref[:, sl].astype(F32)
            xv[c, SUB:SUB + TE, 0:wd] = hv_ref[:, sl].astype(F32)
            gg = _conv(_taps(xg, c, wd, SUB, TE, 3), wg_ref[:, sl])
            vv = _conv(_taps(xv, c, wd, SUB, TE, 3), wv_ref[:, sl])
            a_ref[:, sl] = (gg * _sigmoid(gg) * vv).astype(BF16)

    hspec = lambda off: pl.BlockSpec((None, TE, FB), lambda p, i: (p + off, i, 0))
    wspec = lambda off: pl.BlockSpec((None, 3, FB), lambda p, i: (p + off, 0, 0))
    return _pallas_call(
        body, name=name, grid=(4, n), in_specs=[hspec(0), hspec(4), wspec(0), wspec(4)],
        out_specs=pl.BlockSpec((None, TE, FB), lambda p, i: (p, i, 0)),
        out_shape=jax.ShapeDtypeStruct((4, t, FB), BF16),
        scratch_shapes=[_seq_scratch(FB), _seq_scratch(FB)],
        compiler_params=_params(("parallel", "arbitrary"), TE * FB * 12),
    )(hid, hid, cw, cw)


def _mlp_act_bwd(da, hid, cw, name):
    t = hid.shape[1]
    n = t // TE
    hb = TE // 16

    def body(da_ref, hg_ref, hv_ref, pg_ref, pv_ref, wg_ref, wv_ref, dhg_ref, dhv_ref, dwg_ref, dwv_ref, xg, xv, dg, dv):
        i = pl.program_id(1)

        @pl.when(i == 0)
        def _():
            dwg_ref[...] = jnp.zeros_like(dwg_ref)
            dwv_ref[...] = jnp.zeros_like(dwv_ref)

        _stage_future(dg, i)
        _stage_future(dv, i)
        first_tile = i == n - 1
        for c, (c0, c1) in enumerate(_FB_COLS):
            sl, wd = slice(c0, c1), c1 - c0
            for scr, p_ref, h_ref in ((xg, pg_ref, hg_ref), (xv, pv_ref, hv_ref)):
                scr[c, 0:SUB, 0:wd] = jnp.where(first_tile, 0.0, p_ref[SUB:2 * SUB, sl].astype(F32))
                scr[c, SUB:SUB + TE, 0:wd] = h_ref[:, sl].astype(F32)
            wg = wg_ref[:, sl]
            wv = wv_ref[:, sl]
            tg = _taps(xg, c, wd, SUB, TE, 3)
            tv = _taps(xv, c, wd, SUB, TE, 3)
            gg = _conv(tg, wg)
            vv = _conv(tv, wv)
            sg = _sigmoid(gg)
            dav = da_ref[:, sl].astype(F32)
            dgg = dav * vv * (sg * (1.0 + gg * (1.0 - sg)))
            dvv = dav * (gg * sg)
            for dc, tp, w, scr, dh_ref, dw_ref in ((dgg, tg, wg, dg, dhg_ref, dwg_ref), (dvv, tv, wv, dv, dhv_ref, dwv_ref)):
                scr[c, 0:TE, 0:wd] = dc
                dh_ref[:, sl] = _conv_t(scr, c, wd, w).astype(BF16)
                dw_ref[:, sl] += jnp.concatenate([jnp.sum(tp[j] * dc, axis=0, keepdims=True) for j in range(3)], axis=0)

    rev = lambda off: pl.BlockSpec((None, TE, FB), lambda p, i: (p + off, n - 1 - i, 0))
    halo = lambda off: pl.BlockSpec((None, 16, FB), lambda p, i: (p + off, jnp.maximum((n - 1 - i) * hb - 1, 0), 0))
    wspec = lambda off: pl.BlockSpec((None, 3, FB), lambda p, i: (p + off, 0, 0))
    dwspec = pl.BlockSpec((None, 3, FB), lambda p, i: (p, 0, 0))
    return _pallas_call(
        body, name=name, grid=(4, n), in_specs=[rev(0), rev(0), rev(4), halo(0), halo(4), wspec(0), wspec(4)],
        out_specs=[rev(0), rev(0), dwspec, dwspec],
        out_shape=[jax.ShapeDtypeStruct((4, t, FB), BF16), jax.ShapeDtypeStruct((4, t, FB), BF16),
                   jax.ShapeDtypeStruct((4, 3, FB), F32), jax.ShapeDtypeStruct((4, 3, FB), F32)],
        scratch_shapes=[_seq_scratch(FB)] * 4,
        compiler_params=_params(("parallel", "arbitrary"), TE * FB * 24),
    )(da, hid, hid, hid, hid, cw, cw)


def _adamw(lands, w, m, v, name):
    nl, r, c = w.shape
    tr = r
    for cand in (128, 64, 32, 16):
        if r % cand == 0:
            tr = cand
            break
    nr = r // tr
    c1 = 1.0 - ADAM_B1 ** ADAM_STEP
    c2 = 1.0 - ADAM_B2 ** ADAM_STEP

    def body(*refs):
        l_refs, (w_ref, m_ref, v_ref, g_out, d_out, m_out, v_out) = refs[:nl], refs[nl:]
        layer = pl.program_id(0)
        g = None
        for l, l_ref in enumerate(l_refs):
            gl = l_ref[0].astype(F32)
            for i in range(1, NDEV):
                gl = gl + l_ref[i].astype(F32)
            g = gl if g is None else jnp.where(layer == l, gl, g)
        mn = ADAM_B1 * m_ref[...] + (1.0 - ADAM_B1) * g
        vn = ADAM_B2 * v_ref[...] + (1.0 - ADAM_B2) * (g * g)
        g_out[...] = g
        m_out[...] = mn
        v_out[...] = vn
        d_out[...] = -ADAM_LR * ((mn / c1) / (jnp.sqrt(vn / c2) + ADAM_EPS) + ADAM_WD * w_ref[...])

    def land_spec(l):
        return pl.BlockSpec((NDEV, tr, c), lambda ly, i: (0, jnp.where(ly == l, i, 0 if l > 0 else nr - 1), 0))

    spec = pl.BlockSpec((None, tr, c), lambda ly, i: (ly, i, 0))
    shp = jax.ShapeDtypeStruct((nl, r, c), F32)
    return _pallas_call(
        body, name=name, grid=(nl, nr), in_specs=[land_spec(l) for l in range(nl)] + [spec, spec, spec],
        out_specs=[spec] * 4, out_shape=[shp] * 4,
        compiler_params=_params(("arbitrary", "arbitrary"), (11 + 4 * nl) * tr * c * 4),
    )(*lands, w, m, v)


def _layer_fwd(h, p, tag, gather=(), finish=None):
    u = _rmsnorm_fwd(h, p["norm_mix"], f"norm_mix_{tag}")
    qkv_pre = _mm(u, p["w_qkv"], BF16, f"proj_qkv_{tag}")
    z = _mm(u, p["w_z"], BF16, f"proj_z_{tag}")
    ba = _mm(u, p["w_ba"], F32, f"proj_ba_{tag}")
    pool_in = _mm(u, p["w_pl"], F32, f"proj_pool_{tag}")
    gate = _mm(u, p["w_gate"], BF16, f"proj_gate_{tag}")
    qkv = _gdn_pre_fwd(qkv_pre, p["conv_qkv"], f"gdn_pre_{tag}")
    beta, g = _gates_fwd(ba, p["a_row"], p["dt_row"], f"gates_{tag}")
    (o, s_in, vnew, tinv, wsv), gathered = _gdn_fwd(qkv, beta, g, f"gdn_{tag}", gather)
    if finish is not None:
        p = {**p, **finish(gathered)}
    pooled = _pool_fwd(pool_in, f"pool_{tag}")
    pm = _mm_cols(pooled, p["w_pool"], F32, f"pool_mm_{tag}", _NN)
    y = _post_fwd(o, z, gate, pm, p["head_norm"], p["pool_scale"], f"post_{tag}")
    h1 = _mm(y, p["w_out"], F32, f"out_proj_{tag}", res=h)
    u2 = _rmsnorm_fwd(h1, p["norm_ffn"], f"norm_ffn_{tag}")
    hid = _mm_up(u2, p["w_up"], f"up_proj_{tag}")
    act = _mlp_act_fwd(hid, p["conv_ffn"], f"mlp_act_{tag}")
    h2 = _mm_blocks_red(act, p["w_down"], f"down_proj_{tag}", _NN, res=h1)
    saved = dict(h=h, u=u, qkv_pre=qkv_pre, z=z, ba=ba, gate=gate, qkv=qkv, beta=beta, g=g, o=o, s_in=s_in, vnew=vnew,
                 tinv=tinv, wsv=wsv, pooled=pooled, pm=pm, y=y, h1=h1, u2=u2, hid=hid, act=act)
    return h2, saved, gathered, p


def _layer_bwd(dh, dh_b, p, s, tag, scatter=()):
    gr = {}
    da = _mm_to_blocks(dh_b, p["w_down"], f"d_act_{tag}")
    gr["w_down"] = _mm_tn_blocks(s["act"], dh_b, f"dw_down_{tag}", True, False)
    dhg, dhv, dwg, dwv = _mlp_act_bwd(da, s["hid"], p["conv_ffn"], f"mlp_act_bwd_{tag}")
    gr["conv_ffn"] = jnp.concatenate([dwg, dwv], axis=0)
    w_up = p["w_up"]
    du2 = _mm_blocks_red(dhg, w_up[:4], f"d_u2g_{tag}", _NT)
    du2 = _mm_blocks_red(dhv, w_up[4:], f"d_u2v_{tag}", _NT, res=du2)
    gr["w_up"] = jnp.concatenate([_mm_tn_blocks(s["u2"], dhg, f"dw_upg_{tag}", False, True),
                                  _mm_tn_blocks(s["u2"], dhv, f"dw_upv_{tag}", False, True)], axis=0)
    dh1, dh1_b, gr["norm_ffn"] = _rmsnorm_bwd(s["h1"], du2, dh, p["norm_ffn"], f"norm_ffn_bwd_{tag}")
    dy = _mm(dh1_b, p["w_out"], F32, f"d_y_{tag}", dims=_NT)
    gr["w_out"] = _mm_tn(s["y"], dh1_b, f"dw_out_{tag}")
    do, dz, dgate, dpm, gr["head_norm"], gr["pool_scale"] = _post_bwd(
        dy, s["o"], s["z"], s["gate"], s["pm"], p["head_norm"], p["pool_scale"], f"post_bwd_{tag}")
    dpooled = _mm_cols(dpm, p["w_pool"], F32, f"d_pooled_{tag}", _NT)
    gr["w_pool"] = _mm_tn_cols(s["pooled"], dpm, 4, f"dw_pool_{tag}")
    dpool_in = _pool_bwd(dpooled, f"pool_bwd_{tag}")
    own = (gr["w_up"].astype(BF16), gr["w_down"].reshape(NDEV, -1, D).astype(BF16))
    (dqkv, dbeta, dg), landed = _gdn_bwd(s["qkv"], s["beta"], s["g"], do, s["s_in"], s["vnew"], s["tinv"], s["wsv"],
                                         f"gdn_bwd_{tag}", own + tuple(scatter))
    dba, gr["a_log"], gr["dt_bias"] = _gates_bwd(s["ba"], p["a_row"], p["dt_row"], dbeta, dg, f"gates_bwd_{tag}")
    dqkv_pre, gr["conv_qkv"] = _gdn_pre_bwd(s["qkv_pre"], p["conv_qkv"], dqkv, f"gdn_pre_bwd_{tag}")
    du = None
    dws = []
    for nm, dseg, wseg in (("qkv", dqkv_pre, p["w_qkv"]), ("z", dz, p["w_z"]), ("ba", dba, p["w_ba"]),
                           ("pool", dpool_in, p["w_pl"]), ("gate", dgate, p["w_gate"])):
        du = _mm(dseg, wseg, F32, f"d_u_{nm}_{tag}", res=du, dims=_NT)
        dws.append(_mm_tn(s["u"], dseg, f"dw_{nm}_{tag}"))
    gr["w_in"] = jnp.concatenate([dws[0], dws[1], dws[2][:, 0:H], dws[2][:, LANE:LANE + H], dws[3], dws[4]], axis=1)
    dh0, dh0_b, gr["norm_mix"] = _rmsnorm_bwd(s["h"], du, dh1, p["norm_mix"], f"norm_mix_bwd_{tag}")
    return dh0, dh0_b, gr, landed


def _pad_lanes(v8):
    return jnp.pad(v8.reshape(1, H), ((0, 0), (0, LANE - H)))


def _pack(parts, rows, lead=1):
    flat = jnp.concatenate([q.reshape(lead, -1) for q in parts], axis=1)
    flat = jnp.pad(flat, ((0, 0), (0, rows * LANE - flat.shape[1])))
    return flat.reshape((lead, rows, LANE) if lead > 1 else (rows, LANE))


def _unpack(packed, shapes, lead=1):
    flat = packed.reshape(lead, -1)
    out, off = [], 0
    for shp in shapes:
        n = 1
        for s_ in shp:
            n *= s_
        n //= lead
        out.append(flat[:, off:off + n].reshape(shp))
        off += n
    return out


SMALL_ROWS = 336
REPL_ROWS = 64


def kernel(x, meta_tokens, norm_mix, w_in, conv_qkv, a_log, dt_bias, head_norm, w_pool, pool_scale, w_out, norm_ffn, w_up, conv_ffn, w_down, norm_final, loss_target, m_meta_tokens, m_norm_mix, m_w_in, m_conv_qkv, m_a_log, m_dt_bias, m_head_norm, m_w_pool, m_pool_scale, m_w_out, m_norm_ffn, m_w_up, m_conv_ffn, m_w_down, m_norm_final, v_meta_tokens, v_norm_mix, v_w_in, v_conv_qkv, v_a_log, v_dt_bias, v_head_norm, v_w_pool, v_pool_scale, v_w_out, v_norm_ffn, v_w_up, v_conv_ffn, v_w_down, v_norm_final):
    seq = x.shape[1]
    t = ROW0 + seq
    assert t % TE == 0 and t % (MM_TILES * 16) == 0 and t % (GC * CH) == 0
    depth = w_in.shape[0]
    assert depth == 2
    cin = w_in.shape[2]

    def mixer_params(l, g_in, conv_q, conv_f, wp):
        wf = jnp.transpose(g_in, (1, 0, 2)).reshape(D, NDEV * cin)
        zpad = jnp.zeros((D, LANE - H), BF16)
        return dict(
            w_qkv=wf[:, 0:QKV], w_z=wf[:, QKV:QKV + D],
            w_ba=jnp.concatenate([wf[:, 4096:4104], zpad, wf[:, 4104:4112], zpad], axis=1),
            w_pl=wf[:, 4112:4624], w_gate=wf[:, 4624:6672], conv_qkv=conv_q, conv_ffn=conv_f, w_pool=wp,
            norm_mix=norm_mix[l].reshape(1, D), norm_ffn=norm_ffn[l].reshape(1, D),
            pool_scale=pool_scale[l].reshape(1, D), head_norm=head_norm[l].reshape(1, DH),
            a_row=_pad_lanes(a_log[l]), dt_row=_pad_lanes(dt_bias[l]))

    def late_params(g_up, g_out, g_down):
        return dict(w_out=g_out.reshape(D, D), w_up=g_up, w_down=g_down.reshape(4, FB, D))

    small_shapes = [conv_qkv.shape, conv_ffn.shape, w_pool.shape, meta_tokens.shape]
    small = _pack([conv_qkv, conv_ffn, w_pool, meta_tokens], SMALL_ROWS)
    w_in_b, w_up_b, w_out_b, w_down_b = w_in.astype(BF16), w_up.astype(BF16), w_out.astype(BF16), w_down.astype(BF16)
    g_in0, g_small = _gather([w_in_b[0], small], "gather_first")
    s_cq, s_cf, s_wp, s_mt = _unpack(g_small, [(NDEV,) + shp for shp in small_shapes], lead=NDEV)
    conv_qkv_full = jnp.transpose(s_cq, (1, 2, 0, 3)).reshape(depth, 4, QKV)
    conv_ffn_blk = jnp.transpose(s_cf, (1, 0, 2, 3))
    w_pool_full = jnp.transpose(s_wp, (1, 2, 3, 0, 4)).reshape(depth, 4, DH, 2 * DH).astype(BF16)
    meta_full = jnp.transpose(s_mt, (1, 0, 2)).reshape(N_META, D)

    h = jnp.concatenate([jnp.zeros((LEAD, D), F32), meta_full, x[0]], axis=0)
    p0 = mixer_params(0, g_in0, conv_qkv_full[0], conv_ffn_blk[0], w_pool_full[0])
    h, sv0, rest, p0 = _layer_fwd(
        h, p0, "l0", (w_up_b[0], w_out_b[0], w_down_b[0], w_in_b[1], w_up_b[1], w_out_b[1], w_down_b[1]),
        lambda got: late_params(*got[:3]))
    p1 = {**mixer_params(1, rest[3], conv_qkv_full[1], conv_ffn_blk[1], w_pool_full[1]), **late_params(*rest[4:])}
    h, sv1, _, _ = _layer_fwd(h, p1, "l1")
    layers = [p0, p1]
    saved = [sv0, sv1]
    target = jnp.concatenate([jnp.zeros((ROW0, D), F32), loss_target[0]], axis=0)
    dh, dh_b, d_norm_final, loss_row = _loss_bwd(h, target, norm_final.reshape(1, D), "loss")

    def mixer_blocks(gr):
        return (jnp.transpose(gr["w_in"].reshape(D, NDEV, cin), (1, 0, 2)).astype(BF16),
                gr["w_out"].reshape(NDEV, D // NDEV, D).astype(BF16))

    grads = [None] * depth
    dh, dh_b, grads[1], (l_up1, l_down1) = _layer_bwd(dh, dh_b, layers[1], saved[1], "l1")
    dh, dh_b, grads[0], (l_up0, l_down0, l_in1, l_out1) = _layer_bwd(dh, dh_b, layers[0], saved[0], "l0", mixer_blocks(grads[1]))
    grad_x = dh[ROW0:].reshape(1, seq, D)
    d_meta = dh[LEAD:ROW0]

    stk = lambda name: jnp.stack([grads[l][name] for l in range(depth)], axis=0)
    cq = conv_qkv.shape[2]
    pw = w_pool.shape[3]
    s_cq = jnp.transpose(stk("conv_qkv").reshape(depth, 4, NDEV, cq), (2, 0, 1, 3))
    s_cf = jnp.transpose(stk("conv_ffn"), (1, 0, 2, 3))
    s_wp = jnp.transpose(stk("w_pool").reshape(depth, 4, DH, NDEV, pw), (3, 0, 1, 2, 4))
    s_mt = jnp.transpose(d_meta.reshape(N_META, NDEV, D // NDEV), (1, 0, 2))
    b_small = _pack([s_cq, s_cf, s_wp, s_mt], SMALL_ROWS, lead=NDEV)
    l_in0, l_out0, l_small = _scatter([*mixer_blocks(grads[0]), b_small], "exchange_last")

    r_in = _adamw((l_in0, l_in1), w_in, m_w_in, v_w_in, "adamw_w_in")
    r_up = _adamw((l_up0, l_up1), w_up, m_w_up, v_w_up, "adamw_w_up")
    r_out = _adamw((l_out0, l_out1), w_out, m_w_out, v_w_out, "adamw_w_out")
    r_down = _adamw((l_down0, l_down1), w_down, m_w_down, v_w_down, "adamw_w_down")
    r_small = _adamw((l_small,), small[None], _pack([m_conv_qkv, m_conv_ffn, m_w_pool, m_meta_tokens], SMALL_ROWS)[None],
                     _pack([v_conv_qkv, v_conv_ffn, v_w_pool, v_meta_tokens], SMALL_ROWS)[None], "adamw_small")
    r_small = [_unpack(o_[0], small_shapes) for o_ in r_small]

    repl_shapes = [norm_mix.shape, a_log.shape, dt_bias.shape, head_norm.shape, pool_scale.shape, norm_ffn.shape,
                   norm_final.shape, (1,)]
    rp = lambda name, n: jnp.stack([grads[l][name][0, :n] for l in range(depth)], axis=0)
    part = _pack([rp("norm_mix", D), rp("a_log", H), rp("dt_bias", H), rp("head_norm", DH), rp("pool_scale", D),
                  rp("norm_ffn", D), d_norm_final[0], loss_row[0, 0:1]], REPL_ROWS)
    (l_repl,) = _gather([part], "gather_replicated")
    zero1 = jnp.zeros((1,), F32)
    r_repl = _adamw(
        (l_repl,), _pack([norm_mix, a_log, dt_bias, head_norm, pool_scale, norm_ffn, norm_final, zero1], REPL_ROWS)[None],
        _pack([m_norm_mix, m_a_log, m_dt_bias, m_head_norm, m_pool_scale, m_norm_ffn, m_norm_final, zero1], REPL_ROWS)[None],
        _pack([v_norm_mix, v_a_log, v_dt_bias, v_head_norm, v_pool_scale, v_norm_ffn, v_norm_final, zero1], REPL_ROWS)[None],
        "adamw_replicated")
    r_repl = [_unpack(o_[0], repl_shapes) for o_ in r_repl]
    loss = r_repl[0][7].reshape(())

    def leaf(kind):
        sm, rr = r_small[kind], r_repl[kind]
        return [sm[3], rr[0], r_in[kind], sm[0], rr[1], rr[2], rr[3], sm[2], rr[4], r_out[kind], rr[5], r_up[kind],
                sm[1], r_down[kind], rr[6]]

    return (loss, grad_x, *leaf(0), *leaf(1), *leaf(2), *leaf(3))
```

```python
import jax
import jax.numpy as jnp
from jax import lax
from jax.experimental import pallas as pl
from jax.experimental.pallas import tpu as pltpu

F32 = jnp.float32
BF16 = jnp.bfloat16
HI = lax.Precision.HIGHEST
MESH = pl.DeviceIdType.MESH

D = 1024
H = 8
DH = 128
CH = 64
N_META = 16
LEAD = 48
ROW0 = LEAD + N_META
QKV = 3 * D
POOL_W = 512
POOL_WINDOWS = (2, 4, 8, 16)
FB = 704
NDEV = 8
EPS = 1e-6
MM_TILES = 12
TE = 192
LANE = 128
SUB = 8
VMEM_CAP = 56 << 20

ADAM_LR, ADAM_B1, ADAM_B2, ADAM_EPS, ADAM_WD, ADAM_STEP = 0.001, 0.9, 0.999, 1e-08, 0.01, 10

_NN = (((1,), (0,)), ((), ()))
_NT = (((1,), (1,)), ((), ()))
_TN = (((0,), (0,)), ((), ()))


def _dot(a, b, dims=_NN, precision=None):
    return lax.dot_general(a, b, dims, precision=precision, preferred_element_type=F32)


def _bdot(a, b, dims=_NN):
    return _dot(a.astype(BF16), b.astype(BF16), dims)


def _nbytes(shape, dtype):
    n = 1
    for s in shape:
        n *= s
    return n * jnp.dtype(dtype).itemsize


def _params(sem, block_bytes):
    limit = min(VMEM_CAP, 2 * block_bytes + (20 << 20))
    return pltpu.CompilerParams(dimension_semantics=sem, vmem_limit_bytes=limit)


PIN_BYTES = 12 << 20


def _pallas_call(body, *, out_shape, **kw):
    call = pl.pallas_call

    def big(s):
        return len(s.shape) >= 2 and s.shape[-1] >= D and _nbytes(s.shape, s.dtype) >= PIN_BYTES

    pinned = jax.tree.map(lambda s: pltpu.HBM(s.shape, s.dtype) if big(s) else s, out_shape)

    def run(*args):
        return call(body, out_shape=pinned, **kw)(
            *[pltpu.with_memory_space_constraint(a, pltpu.HBM) if big(a) else a for a in args])

    return run


def _sigmoid(x):
    return 1.0 / (1.0 + jnp.exp(-x))


def _col_tile(n):
    for t in (1024, 512, 256, 128):
        if n % t == 0:
            return t
    return n


def _matmul(a, b, *, dims, grid, a_spec, b_spec, o_spec, out_shape, name, red_axis=None, res=None):
    def body(*refs):
        if res is None:
            a_ref, b_ref, o_ref = refs
        else:
            a_ref, b_ref, r_ref, o_ref = refs
        part = _dot(a_ref[...], b_ref[...], dims)
        if red_axis is None:
            if res is not None:
                part = part + r_ref[...]
            o_ref[...] = part.astype(o_ref.dtype)
        else:
            r = pl.program_id(red_axis)

            @pl.when(r == 0)
            def _():
                o_ref[...] = part + r_ref[...] if res is not None else part

            @pl.when(r > 0)
            def _():
                o_ref[...] += part

    def blk(spec, arr):
        return _nbytes([s for s in spec.block_shape if s is not None], arr.dtype)

    ins = [a, b] + ([res] if res is not None else [])
    specs = [a_spec, b_spec] + ([o_spec] if res is not None else [])
    nb = blk(a_spec, a) + blk(b_spec, b) + 2 * _nbytes([s for s in o_spec.block_shape if s is not None], F32)
    sem = tuple("arbitrary" if i == red_axis else "parallel" for i in range(len(grid)))
    return _pallas_call(
        body, name=name, grid=grid, in_specs=specs, out_specs=o_spec, out_shape=out_shape,
        compiler_params=_params(sem, nb),
    )(*ins)


def _row_tiles(m, row_bytes, fixed_bytes, temp_row_bytes=0):
    for nt in (MM_TILES // 2, MM_TILES):
        tm = m // nt
        if 2 * (row_bytes * tm + fixed_bytes) + temp_row_bytes * tm <= VMEM_CAP - (10 << 20):
            return nt
    return MM_TILES


def _mm(a, b, out_dtype, name, res=None, dims=_NN):
    m, k = a.shape
    n = b.shape[1] if dims == _NN else b.shape[0]
    tn = _col_tile(n)
    nt = _row_tiles(m, 2 * k + tn * (jnp.dtype(out_dtype).itemsize + (4 if res is not None else 0)), 2 * k * tn, 4 * tn)
    tm = m // nt
    if dims == _NN:
        b_spec = pl.BlockSpec((k, tn), lambda j, i: (0, j))
    else:
        b_spec = pl.BlockSpec((tn, k), lambda j, i: (j, 0))
    return _matmul(
        a, b, dims=dims, grid=(n // tn, nt), a_spec=pl.BlockSpec((tm, k), lambda j, i: (i, 0)), b_spec=b_spec,
        o_spec=pl.BlockSpec((tm, tn), lambda j, i: (i, j)), out_shape=jax.ShapeDtypeStruct((m, n), out_dtype),
        name=name, res=res)


def _mm_tn(a, g, name):
    m, k = a.shape
    n = g.shape[1]
    tn = _col_tile(n)
    nt = _row_tiles(m, 2 * k + 2 * tn, 4 * k * tn)
    tm = m // nt
    return _matmul(
        a, g, dims=_TN, grid=(n // tn, nt), red_axis=1, a_spec=pl.BlockSpec((tm, k), lambda j, i: (i, 0)),
        b_spec=pl.BlockSpec((tm, tn), lambda j, i: (i, j)), o_spec=pl.BlockSpec((k, tn), lambda j, i: (0, j)),
        out_shape=jax.ShapeDtypeStruct((k, n), F32), name=name)


def _mm_up(u, w_up, name):
    t = u.shape[0]
    g = w_up.shape[0]
    nt = _row_tiles(t, 2 * D + 2 * FB, 2 * D * FB, 4 * FB)
    tm = t // nt
    return _matmul(
        u, w_up, dims=_NN, grid=(g, nt), a_spec=pl.BlockSpec((tm, D), lambda g_, i: (i, 0)),
        b_spec=pl.BlockSpec((None, D, FB), lambda g_, i: (g_, 0, 0)),
        o_spec=pl.BlockSpec((None, tm, FB), lambda g_, i: (g_, i, 0)),
        out_shape=jax.ShapeDtypeStruct((g, t, FB), BF16), name=name)


def _mm_blocks_red(a, b, name, dims, res=None):
    g, t, k = a.shape
    n = b.shape[2] if dims == _NN else b.shape[1]
    nt = _row_tiles(t, 2 * k + n * (8 if res is not None else 4), 2 * k * n, 4 * n)
    tm = t // nt
    return _matmul(
        a, b, dims=dims, grid=(nt, g), red_axis=1, a_spec=pl.BlockSpec((None, tm, k), lambda i, g_: (g_, i, 0)),
        b_spec=pl.BlockSpec((None,) + b.shape[1:], lambda i, g_: (g_, 0, 0)),
        o_spec=pl.BlockSpec((tm, n), lambda i, g_: (i, 0)), out_shape=jax.ShapeDtypeStruct((t, n), F32),
        name=name, res=res)


def _mm_to_blocks(a, b, name):
    t, k = a.shape
    g, n, _ = b.shape
    nt = _row_tiles(t, 2 * k + 2 * n, 2 * k * n, 4 * n)
    tm = t // nt
    return _matmul(
        a, b, dims=_NT, grid=(g, nt), a_spec=pl.BlockSpec((tm, k), lambda g_, i: (i, 0)),
        b_spec=pl.BlockSpec((None, n, k), lambda g_, i: (g_, 0, 0)),
        o_spec=pl.BlockSpec((None, tm, n), lambda g_, i: (g_, i, 0)),
        out_shape=jax.ShapeDtypeStruct((g, t, n), BF16), name=name)


def _mm_tn_blocks(a, g, name, a_blocked, g_blocked):
    nb = a.shape[0] if a_blocked else g.shape[0]
    t = a.shape[-2]
    k, n = a.shape[-1], g.shape[-1]
    nt = _row_tiles(t, 2 * k + 2 * n, 4 * k * n)
    tm = t // nt
    a_spec = (pl.BlockSpec((None, tm, k), lambda g_, i: (g_, i, 0)) if a_blocked
              else pl.BlockSpec((tm, k), lambda g_, i: (i, 0)))
    g_spec = (pl.BlockSpec((None, tm, n), lambda g_, i: (g_, i, 0)) if g_blocked
              else pl.BlockSpec((tm, n), lambda g_, i: (i, 0)))
    return _matmul(
        a, g, dims=_TN, grid=(nb, nt), red_axis=1, a_spec=a_spec, b_spec=g_spec,
        o_spec=pl.BlockSpec((None, k, n), lambda g_, i: (g_, 0, 0)),
        out_shape=jax.ShapeDtypeStruct((nb, k, n), F32), name=name)


def _mm_cols(a, b, out_dtype, name, dims):
    t = a.shape[0]
    g = b.shape[0]
    ka = a.shape[1] // g
    n = b.shape[2] if dims == _NN else b.shape[1]
    tm = t // MM_TILES
    return _matmul(
        a, b, dims=dims, grid=(g, MM_TILES), a_spec=pl.BlockSpec((tm, ka), lambda g_, i: (i, g_)),
        b_spec=pl.BlockSpec((None,) + b.shape[1:], lambda g_, i: (g_, 0, 0)),
        o_spec=pl.BlockSpec((tm, n), lambda g_, i: (i, g_)), out_shape=jax.ShapeDtypeStruct((t, g * n), out_dtype),
        name=name)


def _mm_tn_cols(a, g, nblk, name):
    t = a.shape[0]
    ka, n = a.shape[1] // nblk, g.shape[1] // nblk
    tm = t // MM_TILES
    return _matmul(
        a, g, dims=_TN, grid=(nblk, MM_TILES), red_axis=1, a_spec=pl.BlockSpec((tm, ka), lambda g_, i: (i, g_)),
        b_spec=pl.BlockSpec((tm, n), lambda g_, i: (i, g_)), o_spec=pl.BlockSpec((None, ka, n), lambda g_, i: (g_, 0, 0)),
        out_shape=jax.ShapeDtypeStruct((nblk, ka, n), F32), name=name)


def _rows(cols, n=None):
    if n is None:
        return pl.BlockSpec((TE, cols), lambda i: (i, 0))
    return pl.BlockSpec((TE, cols), lambda i: (n - 1 - i, 0))


def _whole(shape):
    return pl.BlockSpec(shape, lambda *_: (0,) * len(shape))


def _row_ids(i, rows=TE):
    return i * rows + lax.broadcasted_iota(jnp.int32, (rows, 1), 0)


def _rmsnorm_fwd(h, gain, name):
    t = h.shape[0]

    def body(h_ref, g_ref, u_ref):
        x = h_ref[...]
        r = lax.rsqrt(jnp.mean(x * x, axis=-1, keepdims=True) + EPS)
        u_ref[...] = (x * r * g_ref[...]).astype(BF16)

    return _pallas_call(
        body, name=name, grid=(t // TE,), in_specs=[_rows(D), _whole((1, D))], out_specs=_rows(D),
        out_shape=jax.ShapeDtypeStruct((t, D), BF16), compiler_params=_params(("parallel",), 3 * TE * D * 4),
    )(h, gain)


def _rmsnorm_bwd(x, du, dres, gain, name):
    t = x.shape[0]

    def body(x_ref, du_ref, dr_ref, g_ref, dx_ref, dxb_ref, dg_ref):
        i = pl.program_id(0)
        xv = x_ref[...]
        r = lax.rsqrt(jnp.mean(xv * xv, axis=-1, keepdims=True) + EPS)
        gdy = du_ref[...] * g_ref[...]
        dx = dr_ref[...] + r * gdy - xv * (r * r * r) * jnp.mean(xv * gdy, axis=-1, keepdims=True)
        dx = jnp.where(_row_ids(i) >= LEAD, dx, 0.0)
        dx_ref[...] = dx
        dxb_ref[...] = dx.astype(BF16)
        part = jnp.sum(du_ref[...] * xv * r, axis=0, keepdims=True)

        @pl.when(i == 0)
        def _():
            dg_ref[...] = part

        @pl.when(i > 0)
        def _():
            dg_ref[...] += part

    return _pallas_call(
        body, name=name, grid=(t // TE,), in_specs=[_rows(D), _rows(D), _rows(D), _whole((1, D))],
        out_specs=[_rows(D), _rows(D), _whole((1, D))],
        out_shape=[jax.ShapeDtypeStruct((t, D), F32), jax.ShapeDtypeStruct((t, D), BF16),
                   jax.ShapeDtypeStruct((1, D), F32)],
        compiler_params=_params(("arbitrary",), 5 * TE * D * 4),
    )(x, du, dres, gain)


def _loss_bwd(h, target, gain, name):
    t = h.shape[0]

    def body(h_ref, t_ref, g_ref, dx_ref, dxb_ref, dg_ref, loss_ref):
        i = pl.program_id(0)
        xv = h_ref[...]
        gain_v = g_ref[...]
        r = lax.rsqrt(jnp.mean(xv * xv, axis=-1, keepdims=True) + EPS)
        real = _row_ids(i) >= ROW0
        err = jnp.where(real, xv * r * gain_v - t_ref[...], 0.0)
        dy = err * (1.0 / D)
        gdy = dy * gain_v
        dx = r * gdy - xv * (r * r * r) * jnp.mean(xv * gdy, axis=-1, keepdims=True)
        dx_ref[...] = dx
        dxb_ref[...] = dx.astype(BF16)
        dgp = jnp.sum(dy * xv * r, axis=0, keepdims=True)
        lp = 0.5 * jnp.sum(jnp.mean(err * err, axis=-1, keepdims=True), axis=0, keepdims=True)

        @pl.when(i == 0)
        def _():
            dg_ref[...] = dgp
            loss_ref[...] = jnp.broadcast_to(lp, (1, LANE))

        @pl.when(i > 0)
        def _():
            dg_ref[...] += dgp
            loss_ref[...] += jnp.broadcast_to(lp, (1, LANE))

    return _pallas_call(
        body, name=name, grid=(t // TE,), in_specs=[_rows(D), _rows(D), _whole((1, D))],
        out_specs=[_rows(D), _rows(D), _whole((1, D)), _whole((1, LANE))],
        out_shape=[jax.ShapeDtypeStruct((t, D), F32), jax.ShapeDtypeStruct((t, D), BF16),
                   jax.ShapeDtypeStruct((1, D), F32), jax.ShapeDtypeStruct((1, LANE), F32)],
        compiler_params=_params(("arbitrary",), 4 * TE * D * 4),
    )(h, target, gain)


def _seq_scratch(cols):
    return pltpu.VMEM((-(-cols // LANE), TE + SUB, LANE), F32)


def _taps(scr, c, wd, first, n, k):
    return [scr[c, first - (k - 1) + j:first - (k - 1) + j + n, 0:wd] for j in range(k)]


def _stage_history(scr, i):
    @pl.when(i == 0)
    def _():
        scr[...] = jnp.zeros(scr.shape, F32)

    @pl.when(i > 0)
    def _():
        scr[:, 0:SUB, :] = scr[:, TE:TE + SUB, :]


def _stage_future(scr, i):
    @pl.when(i == 0)
    def _():
        scr[...] = jnp.zeros(scr.shape, F32)

    @pl.when(i > 0)
    def _():
        scr[:, TE:TE + SUB, :] = scr[:, 0:SUB, :]


def _conv(tp, w):
    out = w[0:1] * tp[0]
    for j in range(1, len(tp)):
        out = out + w[j:j + 1] * tp[j]
    return out


def _conv_t(ds, c, wd, w):
    k = w.shape[0]
    out = w[k - 1:k] * ds[c, 0:TE, 0:wd]
    for j in range(k - 1):
        out = out + w[j:j + 1] * ds[c, k - 1 - j:k - 1 - j + TE, 0:wd]
    return out


def _gdn_pre_fwd(x, w, name):
    t = x.shape[0]

    def body(x_ref, w_ref, o_ref, xs):
        _stage_history(xs, pl.program_id(0))
        for hh in range(3 * H):
            sl = slice(hh * DH, (hh + 1) * DH)
            xs[hh, SUB:SUB + TE, :] = x_ref[:, sl].astype(F32)
            cv = _conv(_taps(xs, hh, DH, SUB, TE, 4), w_ref[:, sl])
            s = cv * _sigmoid(cv)
            if hh < 2 * H:
                s = s * lax.rsqrt(jnp.sum(s * s, axis=-1, keepdims=True) + EPS)
                if hh < H:
                    s = s * (DH ** -0.5)
            o_ref[:, sl] = s

    return _pallas_call(
        body, name=name, grid=(t // TE,), in_specs=[_rows(QKV), _whole((4, QKV))], out_specs=_rows(QKV),
        out_shape=jax.ShapeDtypeStruct((t, QKV), F32), scratch_shapes=[_seq_scratch(QKV)],
        compiler_params=_params(("arbitrary",), TE * QKV * 8),
    )(x, w)


def _gdn_pre_bwd(x, w, dqkv, name):
    t = x.shape[0]
    n = t // TE
    hb = TE // 16

    def body(x_ref, xp_ref, w_ref, d_ref, dx_ref, dw_ref, xs, ds):
        i = pl.program_id(0)

        @pl.when(i == 0)
        def _():
            dw_ref[...] = jnp.zeros_like(dw_ref)

        _stage_future(ds, i)
        for hh in range(3 * H):
            sl = slice(hh * DH, (hh + 1) * DH)
            xs[hh, 0:SUB, :] = jnp.where(i == n - 1, 0.0, xp_ref[SUB:2 * SUB, sl].astype(F32))
            xs[hh, SUB:SUB + TE, :] = x_ref[:, sl].astype(F32)
            wv = w_ref[:, sl]
            tp = _taps(xs, hh, DH, SUB, TE, 4)
            cv = _conv(tp, wv)
            sg = _sigmoid(cv)
            s = cv * sg
            dsv = d_ref[:, sl]
            if hh < 2 * H:
                if hh < H:
                    dsv = dsv * (DH ** -0.5)
                r = lax.rsqrt(jnp.sum(s * s, axis=-1, keepdims=True) + EPS)
                dsv = r * dsv - s * (r * r * r) * jnp.sum(s * dsv, axis=-1, keepdims=True)
            dcv = dsv * (sg * (1.0 + cv * (1.0 - sg)))
            ds[hh, 0:TE, :] = dcv
            dx_ref[:, sl] = _conv_t(ds, hh, DH, wv).astype(BF16)
            dw_ref[:, sl] += jnp.concatenate([jnp.sum(tp[j] * dcv, axis=0, keepdims=True) for j in range(4)], axis=0)

    return _pallas_call(
        body, name=name, grid=(n,),
        in_specs=[_rows(QKV, n), pl.BlockSpec((16, QKV), lambda i: (jnp.maximum((n - 1 - i) * hb - 1, 0), 0)),
                  _whole((4, QKV)), _rows(QKV, n)],
        out_specs=[_rows(QKV, n), _whole((4, QKV))],
        out_shape=[jax.ShapeDtypeStruct((t, QKV), BF16), jax.ShapeDtypeStruct((4, QKV), F32)],
        scratch_shapes=[_seq_scratch(QKV), _seq_scratch(QKV)],
        compiler_params=_params(("arbitrary",), TE * QKV * 14),
    )(x, x, w, dqkv)


def _softplus(x):
    return jnp.maximum(x, 0.0) + jnp.log(1.0 + jnp.exp(-jnp.abs(x)))


def _gates_fwd(ba, a_row, dt_row, name):
    t = ba.shape[0]

    def body(ba_ref, a_ref, dt_ref, b_out, g_out):
        real = _row_ids(pl.program_id(0)) >= LEAD
        b_out[...] = jnp.where(real, _sigmoid(ba_ref[:, 0:LANE]), 0.0)
        g = -jnp.exp(a_ref[...]) * _softplus(ba_ref[:, LANE:2 * LANE] + dt_ref[...])
        g_out[...] = jnp.where(real, g, 0.0)

    return _pallas_call(
        body, name=name, grid=(t // TE,), in_specs=[_rows(2 * LANE), _whole((1, LANE)), _whole((1, LANE))],
        out_specs=[_rows(LANE), _rows(LANE)],
        out_shape=[jax.ShapeDtypeStruct((t, LANE), F32), jax.ShapeDtypeStruct((t, LANE), F32)],
        compiler_params=_params(("parallel",), TE * LANE * 16),
    )(ba, a_row, dt_row)


def _gates_bwd(ba, a_row, dt_row, dbeta, dg, name):
    t = ba.shape[0]

    def body(ba_ref, a_ref, dt_ref, db_ref, dg_ref, dba_ref, da_out, ddt_out):
        i = pl.program_id(0)
        real = _row_ids(i) >= LEAD
        beta = _sigmoid(ba_ref[:, 0:LANE])
        draw_b = jnp.where(real, db_ref[...] * beta * (1.0 - beta), 0.0)
        pre = ba_ref[:, LANE:2 * LANE] + dt_ref[...]
        neg_a = -jnp.exp(a_ref[...])
        dgv = jnp.where(real, dg_ref[...], 0.0)
        draw_a = dgv * neg_a * _sigmoid(pre)
        dba_ref[:, 0:LANE] = draw_b.astype(BF16)
        dba_ref[:, LANE:2 * LANE] = draw_a.astype(BF16)
        dal = jnp.sum(dgv * neg_a * _softplus(pre), axis=0, keepdims=True)
        ddt = jnp.sum(draw_a, axis=0, keepdims=True)

        @pl.when(i == 0)
        def _():
            da_out[...] = dal
            ddt_out[...] = ddt

        @pl.when(i > 0)
        def _():
            da_out[...] += dal
            ddt_out[...] += ddt

    return _pallas_call(
        body, name=name, grid=(t // TE,),
        in_specs=[_rows(2 * LANE), _whole((1, LANE)), _whole((1, LANE)), _rows(LANE), _rows(LANE)],
        out_specs=[_rows(2 * LANE), _whole((1, LANE)), _whole((1, LANE))],
        out_shape=[jax.ShapeDtypeStruct((t, 2 * LANE), BF16), jax.ShapeDtypeStruct((1, LANE), F32),
                   jax.ShapeDtypeStruct((1, LANE), F32)],
        compiler_params=_params(("arbitrary",), TE * LANE * 24),
    )(ba, a_row, dt_row, dbeta, dg)


_OFFSETS = [(dx, dy, dc) for dx in (0, 1) for dy in (0, 1) for dc in (0, 1)][1:]
NPEER = len(_OFFSETS)
ANY_SPEC = pl.BlockSpec(memory_space=pl.ANY)


def _place():
    return lax.axis_index("x"), lax.axis_index("y"), lax.axis_index("c")


def _index(p):
    return 4 * p[0] + 2 * p[1] + p[2]


def _comm_scratch(n):
    return [pltpu.SemaphoreType.DMA((n * NPEER,)), pltpu.SemaphoreType.DMA((n * NPEER,)), pltpu.SemaphoreType.DMA((n,))]


def _scatter_copies(ins, outs, send, recv):
    me = _place()
    mi = _index(me)
    res = []
    for j, d in enumerate(_OFFSETS):
        peer = tuple(1 - v if bit else v for v, bit in zip(me, d))
        pi = _index(peer)
        for k in range(len(ins)):
            sem = k * NPEER + j
            mine = pltpu.make_async_remote_copy(src_ref=ins[k].at[pi], dst_ref=outs[k].at[mi], send_sem=send.at[sem],
                                                recv_sem=recv.at[sem], device_id=peer, device_id_type=MESH)
            theirs = pltpu.make_async_remote_copy(src_ref=ins[k].at[pi], dst_ref=outs[k].at[pi], send_sem=send.at[sem],
                                                  recv_sem=recv.at[sem], device_id=peer, device_id_type=MESH)
            res.append((mine, theirs))
    return res


def _scatter_own(ins, outs, loc):
    mi = _index(_place())
    return [pltpu.make_async_copy(ins[k].at[mi], outs[k].at[mi], loc.at[k]) for k in range(len(ins))]


def _scatter_start(ins, outs, send, recv, loc):
    for cp in _scatter_own(ins, outs, loc):
        cp.start()
    for mine, _ in _scatter_copies(ins, outs, send, recv):
        mine.start()


def _scatter_wait(ins, outs, send, recv, loc):
    cps = _scatter_copies(ins, outs, send, recv)
    for _, theirs in cps:
        theirs.wait_recv()
    for mine, _ in cps:
        mine.wait_send()
    for cp in _scatter_own(ins, outs, loc):
        cp.wait()


def _gather_parts(ins, outs, send, recv):
    x, y, c = _place()
    chips = [(1 - x, y), (x, 1 - y), (1 - x, 1 - y)]

    def cp(k, slot, src, block, to):
        return pltpu.make_async_remote_copy(src_ref=src, dst_ref=outs[k].at[_index(block)], send_sem=send.at[k * NPEER + slot],
                                            recv_sem=recv.at[k * NPEER + slot], device_id=to, device_id_type=MESH)

    return (x, y, c), (x, y, 1 - c), chips, cp


def _gather_start(ins, outs, send, recv, loc):
    me, sib, chips, cp = _gather_parts(ins, outs, send, recv)
    for k in range(len(ins)):
        pltpu.make_async_copy(ins[k], outs[k].at[_index(me)], loc.at[k]).start()
        cp(k, 0, ins[k], me, sib).start()
        for j, chip in enumerate(chips):
            cp(k, 1 + j, ins[k], me, (*chip, me[2])).start()


def _gather_forward(ins, outs, send, recv, loc):
    me, sib, chips, cp = _gather_parts(ins, outs, send, recv)
    for j, chip in enumerate(chips):
        blk = (*chip, me[2])
        for k in range(len(ins)):
            cp(k, 1 + j, ins[k], blk, me).wait_recv()
            cp(k, 4 + j, outs[k].at[_index(blk)], blk, sib).start()


def _gather_finish(ins, outs, send, recv, loc):
    me, sib, chips, cp = _gather_parts(ins, outs, send, recv)
    for k in range(len(ins)):
        cp(k, 0, ins[k], sib, me).wait_recv()
        for j, chip in enumerate(chips):
            cp(k, 4 + j, ins[k], (*chip, sib[2]), me).wait_recv()
        cp(k, 0, ins[k], me, sib).wait_send()
        for j, chip in enumerate(chips):
            cp(k, 1 + j, ins[k], me, (*chip, me[2])).wait_send()
            cp(k, 4 + j, outs[k].at[_index((*chip, me[2]))], (*chip, me[2]), sib).wait_send()
        pltpu.make_async_copy(ins[k], outs[k].at[_index(me)], loc.at[k]).wait()


def _gathered_shapes(arrs):
    return [jax.ShapeDtypeStruct((NDEV,) + a.shape, a.dtype) for a in arrs]


def _gather(arrs, name):
    n = len(arrs)

    def body(*refs):
        ins, outs, sems = refs[:n], refs[n:2 * n], refs[2 * n:]
        _gather_start(ins, outs, *sems)
        _gather_forward(ins, outs, *sems)
        _gather_finish(ins, outs, *sems)

    return _pallas_call(body, name=name, in_specs=[ANY_SPEC] * n, out_specs=[ANY_SPEC] * n,
                          out_shape=_gathered_shapes(arrs), scratch_shapes=_comm_scratch(n))(*arrs)


def _scatter(arrs, name):
    n = len(arrs)

    def body(*refs):
        ins, outs, sems = refs[:n], refs[n:2 * n], refs[2 * n:]
        _scatter_start(ins, outs, *sems)
        _scatter_wait(ins, outs, *sems)

    return _pallas_call(body, name=name, in_specs=[ANY_SPEC] * n, out_specs=[ANY_SPEC] * n,
                          out_shape=[jax.ShapeDtypeStruct(a.shape, a.dtype) for a in arrs],
                          scratch_shapes=_comm_scratch(n))(*arrs)


_BNN = (((2,), (1,)), ((0,), (0,)))
_BNT = (((2,), (2,)), ((0,), (0,)))
_BTN = (((1,), (1,)), ((0,), (0,)))


def _split(a):
    hi = a.astype(BF16)
    return hi, (a - hi.astype(F32)).astype(BF16)


def _dot3(a, b):
    ah, al = _split(a)
    bh, bl = _split(b)
    m = a.shape[1]
    r = _dot(jnp.concatenate([ah, al], axis=1), bh, _BNN)
    return r[:, :m] + r[:, m:] + _dot(ah, bl, _BNN)


GC = 3


def _rows_of(c):
    return slice(c * CH, (c + 1) * CH)


def _heads(ref, off):
    return jnp.stack([ref[_rows_of(c), off + h * DH:off + (h + 1) * DH] for c in range(GC) for h in range(H)])


def _cols(arrs):
    return jnp.stack([a[:, h:h + 1] for a in arrs for h in range(H)])


def _lanes(a):
    lane = lax.broadcasted_iota(jnp.int32, (CH, LANE), 1)
    out = jnp.zeros((CH, LANE), F32)
    for h in range(H):
        out = jnp.where(lane == h, a[h], out)
    return out


def _chunk_prep(qkv_ref, b_ref, g_ref):
    row = lax.broadcasted_iota(jnp.int32, (CH, CH), 0)
    col = lax.broadcasted_iota(jnp.int32, (CH, CH), 1)
    incl, strict = row >= col, row > col
    gcs = [_dot(incl.astype(F32), g_ref[_rows_of(c), :], precision=HI) for c in range(GC)]
    q, k, v = _heads(qkv_ref, 0), _heads(qkv_ref, D), _heads(qkv_ref, 2 * D)
    bcol, gcol = _cols([b_ref[_rows_of(c), :] for c in range(GC)]), _cols(gcs)
    grow = jnp.stack([gct[h:h + 1, :] for gct in [gc.T for gc in gcs] for h in range(H)])
    glast = _cols([gc[CH - 1:CH, :] for gc in gcs])
    dec = jnp.exp(jnp.where(incl[None], gcol - grow, -1e30))
    kb = k * bcol
    ab = _bdot(jnp.concatenate([kb, q], axis=1), k, _BNT)
    egc, ekc = jnp.exp(gcol), jnp.exp(glast - gcol)
    return dict(row=row, col=col, strict=strict[None], q=q, k=k, v=v, bcol=bcol, dec=dec, kb=kb,
                lm=jnp.where(strict[None], ab[:, :CH] * dec, 0.0), qk=ab[:, CH:] * dec, egc=egc, ekc=ekc,
                gth=jnp.exp(glast), qd=q * egc, kd=k * ekc, vb=v * bcol, kbg=kb * egc)


def _unit_lower_inverse(lm, eye):
    n = -lm
    x = eye + n
    pw = _dot3(n, n)
    for it in range(5):
        if it < 4:
            xp = _dot3(jnp.concatenate([x, pw], axis=1), pw)
            x = x + xp[:, :CH]
            pw = xp[:, CH:]
        else:
            x = x + _dot3(x, pw)
    return x


def _gdn_fwd(qkv, beta, g, name, gather=()):
    t = qkv.shape[0]
    nc = t // CH
    ns = nc // GC
    ng = len(gather)

    def body(qkv_ref, b_ref, g_ref, *rest):
        c_ins, (o_ref, sin_ref, vn_ref, ti_ref, w_ref) = rest[:ng], rest[ng:ng + 5]
        c_outs, state, sems = rest[ng + 5:2 * ng + 5], rest[2 * ng + 5], rest[2 * ng + 6:]
        step = pl.program_id(0)

        @pl.when(step == 0)
        def _():
            state[...] = jnp.zeros_like(state)
            if ng:
                _gather_start(c_ins, c_outs, *sems)

        if ng:
            @pl.when(step == max(ns - 4, 0))
            def _():
                _gather_forward(c_ins, c_outs, *sems)

            @pl.when(step == ns - 1)
            def _():
                _gather_finish(c_ins, c_outs, *sems)

        pr = _chunk_prep(qkv_ref, b_ref, g_ref)
        tinv = _unit_lower_inverse(pr["lm"], (pr["row"] == pr["col"]).astype(F32)[None])
        uw = _bdot(tinv, jnp.concatenate([pr["vb"], pr["kbg"]], axis=2), _BNN)
        u, w = uw[:, :, :DH], uw[:, :, DH:]
        s = state[...]
        for c in range(GC):
            hs = slice(c * H, (c + 1) * H)
            ws = _bdot(jnp.concatenate([w[hs], pr["qd"][hs]], axis=1), s, _BNN)
            vn = u[hs] - ws[:, :CH]
            o = ws[:, CH:] + _bdot(pr["qk"][hs], vn, _BNN)
            sin_ref[c] = s
            ti_ref[c] = tinv[hs]
            s = s * pr["gth"][hs] + _bdot(pr["kd"][hs], vn, _BTN)
            for h in range(H):
                sl = slice(h * DH, (h + 1) * DH)
                o_ref[_rows_of(c), sl] = o[h]
                vn_ref[_rows_of(c), sl] = vn[h]
                w_ref[_rows_of(c), sl] = w[c * H + h]
        state[...] = s

    chunk = lambda cols: pl.BlockSpec((GC * CH, cols), lambda c: (c, 0))
    outs = _pallas_call(
        body, name=name, grid=(ns,), in_specs=[chunk(QKV), chunk(LANE), chunk(LANE)] + [ANY_SPEC] * ng,
        out_specs=[chunk(D), pl.BlockSpec((GC, H, DH, DH), lambda c: (c, 0, 0, 0)), chunk(D),
                   pl.BlockSpec((GC, H, CH, CH), lambda c: (c, 0, 0, 0)), chunk(D)] + [ANY_SPEC] * ng,
        out_shape=[jax.ShapeDtypeStruct((t, D), F32), jax.ShapeDtypeStruct((nc, H, DH, DH), F32),
                   jax.ShapeDtypeStruct((t, D), F32), jax.ShapeDtypeStruct((nc, H, CH, CH), F32),
                   jax.ShapeDtypeStruct((t, D), F32)] + _gathered_shapes(gather),
        scratch_shapes=[pltpu.VMEM((H, DH, DH), F32)] + (_comm_scratch(ng) if ng else []),
        compiler_params=_params(("arbitrary",), 12 << 20),
    )(qkv, beta, g, *gather)
    return outs[:5], outs[5:]


def _gdn_bwd(qkv, beta, g, do, s_in, vnew, tinv, wsv, name, scatter=()):
    t = qkv.shape[0]
    nsteps = t // CH // GC
    ns = len(scatter)

    def body(qkv_ref, b_ref, g_ref, do_ref, sin_ref, vn_ref, ti_ref, w_ref, *rest):
        c_ins, (dqkv_ref, db_ref, dg_ref) = rest[:ns], rest[ns:ns + 3]
        c_outs, dstate, sems = rest[ns + 3:2 * ns + 3], rest[2 * ns + 3], rest[2 * ns + 4:]
        step = pl.program_id(0)

        @pl.when(step == 0)
        def _():
            dstate[...] = jnp.zeros_like(dstate)
            if ns:
                _scatter_start(c_ins, c_outs, *sems)

        if ns:
            @pl.when(step == nsteps - 1)
            def _():
                _scatter_wait(c_ins, c_outs, *sems)

        pr = _chunk_prep(qkv_ref, b_ref, g_ref)
        ti = jnp.concatenate([ti_ref[c] for c in range(GC)], axis=0)
        s = jnp.concatenate([sin_ref[c] for c in range(GC)], axis=0)
        w, vn, doh = _heads(w_ref, 0), _heads(vn_ref, 0), _heads(do_ref, 0)
        dqd = _bdot(doh, s, _BNT)
        dqk = _bdot(doh, vn, _BNT)
        qk_do = _bdot(pr["qk"], doh, _BTN)
        qd_do = _bdot(pr["qd"], doh, _BTN)
        ds = dstate[...]
        dvn_c, dkd_c, dw_c, dgt_c = [None] * GC, [None] * GC, [None] * GC, [None] * GC
        for c in reversed(range(GC)):
            hs = slice(c * H, (c + 1) * H)
            dvn_c[c] = _bdot(pr["kd"][hs], ds, _BNN) + qk_do[hs]
            dkd_c[c] = _bdot(vn[hs], ds, _BNT)
            dw_c[c] = -_bdot(dvn_c[c], s[hs], _BNT)
            dgt_c[c] = jnp.sum(jnp.sum(ds * s[hs], axis=2, keepdims=True), axis=1, keepdims=True)
            ds = ds * pr["gth"][hs] + qd_do[hs] - _bdot(w[hs], dvn_c[c], _BTN)
        dstate[...] = ds
        dvn, dkd, dw, dgt = (jnp.concatenate(parts, axis=0) for parts in (dvn_c, dkd_c, dw_c, dgt_c))
        duw = jnp.concatenate([dvn, dw], axis=2)
        dvk = _bdot(ti, duw, _BTN)
        dvb, dkbg = dvk[:, :, :DH], dvk[:, :, DH:]
        dti = _bdot(duw, jnp.concatenate([pr["vb"], pr["kbg"]], axis=2), _BNT)
        dl = -_dot(_dot(ti, dti, _BTN, precision=HI), ti, _BNT, precision=HI)
        dl = jnp.where(pr["strict"], dl, 0.0)
        dab = jnp.concatenate([dl * pr["dec"], dqk * pr["dec"]], axis=1)
        r1 = _bdot(dab, pr["k"], _BNN)
        dkb = r1[:, :CH] + dkbg * pr["egc"]
        dq = r1[:, CH:] + dqd * pr["egc"]
        dk = _bdot(dab, jnp.concatenate([pr["kb"], pr["q"]], axis=1), _BTN) + dkb * pr["bcol"] + dkd * pr["ekc"]
        m = dl * pr["lm"] + dqk * pr["qk"]
        colsum = _dot(m, jnp.ones((GC * H, CH, LANE), F32), _BTN, precision=HI)[:, :, 0:1]
        kdsum = jnp.sum(dkd * pr["kd"], axis=2, keepdims=True)
        dgc = (jnp.sum(m, axis=2, keepdims=True) - colsum + jnp.sum(dkbg * pr["kbg"], axis=2, keepdims=True)
               + jnp.sum(dqd * pr["qd"], axis=2, keepdims=True) - kdsum)
        dglast = jnp.sum(kdsum, axis=1, keepdims=True) + dgt * pr["gth"]
        last_row = lax.broadcasted_iota(jnp.int32, (1, CH, 1), 1) == CH - 1
        dgc = dgc + jnp.where(last_row, dglast, 0.0)
        dbeta = jnp.sum(dkb * pr["k"], axis=2, keepdims=True) + jnp.sum(dvb * pr["v"], axis=2, keepdims=True)
        dv = dvb * pr["bcol"]
        upper = (pr["row"] <= pr["col"]).astype(F32)
        for c in range(GC):
            hs = slice(c * H, (c + 1) * H)
            for h in range(H):
                dqkv_ref[_rows_of(c), h * DH:(h + 1) * DH] = dq[c * H + h]
                dqkv_ref[_rows_of(c), D + h * DH:D + (h + 1) * DH] = dk[c * H + h]
                dqkv_ref[_rows_of(c), 2 * D + h * DH:2 * D + (h + 1) * DH] = dv[c * H + h]
            db_ref[_rows_of(c), :] = _lanes(dbeta[hs])
            dg_ref[_rows_of(c), :] = _dot(upper, _lanes(dgc[hs]), precision=HI)

    chunk = lambda cols: pl.BlockSpec((GC * CH, cols), lambda c: (nsteps - 1 - c, 0))
    sq = lambda a, b: pl.BlockSpec((GC, H, a, b), lambda c: (nsteps - 1 - c, 0, 0, 0))
    outs = _pallas_call(
        body, name=name, grid=(nsteps,),
        in_specs=[chunk(QKV), chunk(LANE), chunk(LANE), chunk(D), sq(DH, DH), chunk(D), sq(CH, CH), chunk(D)] + [ANY_SPEC] * ns,
        out_specs=[chunk(QKV), chunk(LANE), chunk(LANE)] + [ANY_SPEC] * ns,
        out_shape=[jax.ShapeDtypeStruct((t, QKV), F32), jax.ShapeDtypeStruct((t, LANE), F32),
                   jax.ShapeDtypeStruct((t, LANE), F32)] + [jax.ShapeDtypeStruct(a.shape, a.dtype) for a in scatter],
        scratch_shapes=[pltpu.VMEM((H, DH, DH), F32)] + (_comm_scratch(ns) if ns else []),
        compiler_params=_params(("arbitrary",), 16 << 20),
    )(qkv, beta, g, do, s_in, vnew, tinv, wsv, *scatter)
    return outs[:3], outs[3:]


def _pool_counts(row_ids, win):
    return jnp.minimum(jnp.maximum(row_ids - LEAD, 0) + 1, win).astype(F32)


def _pool_fwd(p, name):
    t = p.shape[0]
    ext = TE + 16

    def body(p_ref, o_ref, carry):
        i = pl.program_id(0)

        @pl.when(i == 0)
        def _():
            carry[...] = jnp.zeros_like(carry)

        ids = _row_ids(i)
        for gi, win in enumerate(POOL_WINDOWS):
            sl = slice(gi * LANE, (gi + 1) * LANE)
            xv = p_ref[:, sl]
            s = jnp.concatenate([carry[:, sl], xv], axis=0)
            sh = 1
            while sh < win:
                s = s + pltpu.roll(s, sh, 0)
                sh *= 2
            o_ref[:, sl] = (s[16:ext] / _pool_counts(ids, win) - xv).astype(BF16)
            carry[:, sl] = xv[TE - 16:TE]

    return _pallas_call(
        body, name=name, grid=(t // TE,), in_specs=[_rows(POOL_W)], out_specs=_rows(POOL_W),
        out_shape=jax.ShapeDtypeStruct((t, POOL_W), BF16), scratch_shapes=[pltpu.VMEM((16, POOL_W), F32)],
        compiler_params=_params(("arbitrary",), TE * POOL_W * 8),
    )(p)


def _pool_bwd(dpo, name):
    t = dpo.shape[0]
    n = t // TE
    ext = TE + 16

    def body(d_ref, o_ref, carry):
        i = pl.program_id(0)

        @pl.when(i == 0)
        def _():
            carry[...] = jnp.zeros_like(carry)

        ids = _row_ids(n - 1 - i)
        for gi, win in enumerate(POOL_WINDOWS):
            sl = slice(gi * LANE, (gi + 1) * LANE)
            dv = d_ref[:, sl]
            rv = dv / _pool_counts(ids, win)
            s = jnp.concatenate([rv, carry[:, sl]], axis=0)
            sh = 1
            while sh < win:
                s = s + pltpu.roll(s, ext - sh, 0)
                sh *= 2
            o_ref[:, sl] = (s[0:TE] - dv).astype(BF16)
            carry[:, sl] = rv[0:16]

    return _pallas_call(
        body, name=name, grid=(n,), in_specs=[_rows(POOL_W, n)], out_specs=_rows(POOL_W, n),
        out_shape=jax.ShapeDtypeStruct((t, POOL_W), BF16), scratch_shapes=[pltpu.VMEM((16, POOL_W), F32)],
        compiler_params=_params(("arbitrary",), TE * POOL_W * 8),
    )(dpo)


def _post_fwd(o, z, gate, pm, hn, ps, name):
    t = o.shape[0]

    def body(o_ref, z_ref, g_ref, pm_ref, hn_ref, ps_ref, y_ref):
        for h in range(H):
            sl = slice(h * DH, (h + 1) * DH)
            ov = o_ref[:, sl]
            zv = z_ref[:, sl].astype(F32)
            r = lax.rsqrt(jnp.mean(ov * ov, axis=-1, keepdims=True) + EPS)
            ya = ov * r * hn_ref[...] * (zv * _sigmoid(zv))
            ga = _sigmoid(g_ref[:, sl].astype(F32))
            gb = _sigmoid(g_ref[:, D + h * DH:D + (h + 1) * DH].astype(F32))
            y_ref[:, sl] = (ga * ya + gb * (pm_ref[:, sl] * ps_ref[:, sl])).astype(BF16)

    return _pallas_call(
        body, name=name, grid=(t // TE,),
        in_specs=[_rows(D), _rows(D), _rows(2 * D), _rows(D), _whole((1, DH)), _whole((1, D))], out_specs=_rows(D),
        out_shape=jax.ShapeDtypeStruct((t, D), BF16), compiler_params=_params(("parallel",), TE * D * 16),
    )(o, z, gate, pm, hn, ps)


def _post_bwd(dy, o, z, gate, pm, hn, ps, name):
    t = o.shape[0]

    def body(dy_ref, o_ref, z_ref, g_ref, pm_ref, hn_ref, ps_ref, do_ref, dz_ref, dgate_ref, dpm_ref, dhn_ref, dps_ref):
        i = pl.program_id(0)

        @pl.when(i == 0)
        def _():
            dhn_ref[...] = jnp.zeros_like(dhn_ref)
            dps_ref[...] = jnp.zeros_like(dps_ref)

        hnv = hn_ref[...]
        dhn = jnp.zeros((1, DH), F32)
        for h in range(H):
            sl = slice(h * DH, (h + 1) * DH)
            slb = slice(D + h * DH, D + (h + 1) * DH)
            dyv = dy_ref[:, sl]
            ov = o_ref[:, sl]
            zv = z_ref[:, sl].astype(F32)
            r = lax.rsqrt(jnp.mean(ov * ov, axis=-1, keepdims=True) + EPS)
            sz = _sigmoid(zv)
            silu = zv * sz
            on = ov * r
            ya = on * hnv * silu
            ga = _sigmoid(g_ref[:, sl].astype(F32))
            gb = _sigmoid(g_ref[:, slb].astype(F32))
            pmv = pm_ref[:, sl]
            psv = ps_ref[:, sl]
            dya = dyv * ga
            dyb = dyv * gb
            dgate_ref[:, sl] = (dyv * ya * ga * (1.0 - ga)).astype(BF16)
            dgate_ref[:, slb] = (dyv * (pmv * psv) * gb * (1.0 - gb)).astype(BF16)
            tt = dya * hnv * silu
            do_ref[:, sl] = r * tt - ov * (r * r * r) * jnp.mean(ov * tt, axis=-1, keepdims=True)
            dz_ref[:, sl] = (dya * on * hnv * (sz * (1.0 + zv * (1.0 - sz)))).astype(BF16)
            dhn = dhn + jnp.sum(dya * on * silu, axis=0, keepdims=True)
            dps_ref[:, sl] += jnp.sum(dyb * pmv, axis=0, keepdims=True)
            dpm_ref[:, sl] = (dyb * psv).astype(BF16)
        dhn_ref[...] += dhn

    return _pallas_call(
        body, name=name, grid=(t // TE,),
        in_specs=[_rows(D), _rows(D), _rows(D), _rows(2 * D), _rows(D), _whole((1, DH)), _whole((1, D))],
        out_specs=[_rows(D), _rows(D), _rows(2 * D), _rows(D), _whole((1, DH)), _whole((1, D))],
        out_shape=[jax.ShapeDtypeStruct((t, D), F32), jax.ShapeDtypeStruct((t, D), BF16),
                   jax.ShapeDtypeStruct((t, 2 * D), BF16), jax.ShapeDtypeStruct((t, D), BF16),
                   jax.ShapeDtypeStruct((1, DH), F32), jax.ShapeDtypeStruct((1, D), F32)],
        compiler_params=_params(("arbitrary",), TE * D * 28),
    )(dy, o, z, gate, pm, hn, ps)


_FB_COLS = [(c, min(c + LANE, FB)) for c in range(0, FB, LANE)]


def _mlp_act_fwd(hid, cw, name):
    t = hid.shape[1]
    n = t // TE

    def body(hg_ref, hv_ref, wg_ref, wv_ref, a_ref, xg, xv):
        i = pl.program_id(1)
        _stage_history(xg, i)
        _stage_history(xv, i)
        for c, (c0, c1) in enumerate(_FB_COLS):
            sl, wd = slice(c0, c1), c1 - c0
            xg[c, SUB:SUB + TE, 0:wd] = hg_ref[:, sl].astype(F32)
            xv[c, SUB:SUB + TE, 0:wd] = hv_ref[:, sl].astype(F32)
            gg = _conv(_taps(xg, c, wd, SUB, TE, 3), wg_ref[:, sl])
            vv = _conv(_taps(xv, c, wd, SUB, TE, 3), wv_ref[:, sl])
            a_ref[:, sl] = (gg * _sigmoid(gg) * vv).astype(BF16)

    hspec = lambda off: pl.BlockSpec((None, TE, FB), lambda p, i: (p + off, i, 0))
    wspec = lambda off: pl.BlockSpec((None, 3, FB), lambda p, i: (p + off, 0, 0))
    return _pallas_call(
        body, name=name, grid=(4, n), in_specs=[hspec(0), hspec(4), wspec(0), wspec(4)],
        out_specs=pl.BlockSpec((None, TE, FB), lambda p, i: (p, i, 0)),
        out_shape=jax.ShapeDtypeStruct((4, t, FB), BF16),
        scratch_shapes=[_seq_scratch(FB), _seq_scratch(FB)],
        compiler_params=_params(("parallel", "arbitrary"), TE * FB * 12),
    )(hid, hid, cw, cw)


def _mlp_act_bwd(da, hid, cw, name):
    t = hid.shape[1]
    n = t // TE
    hb = TE // 16

    def body(da_ref, hg_ref, hv_ref, pg_ref, pv_ref, wg_ref, wv_ref, dhg_ref, dhv_ref, dwg_ref, dwv_ref, xg, xv, dg, dv):
        i = pl.program_id(1)

        @pl.when(i == 0)
        def _():
            dwg_ref[...] = jnp.zeros_like(dwg_ref)
            dwv_ref[...] = jnp.zeros_like(dwv_ref)

        _stage_future(dg, i)
        _stage_future(dv, i)
        first_tile = i == n - 1
        for c, (c0, c1) in enumerate(_FB_COLS):
            sl, wd = slice(c0, c1), c1 - c0
            for scr, p_ref, h_ref in ((xg, pg_ref, hg_ref), (xv, pv_ref, hv_ref)):
                scr[c, 0:SUB, 0:wd] = jnp.where(first_tile, 0.0, p_ref[SUB:2 * SUB, sl].astype(F32))
                scr[c, SUB:SUB + TE, 0:wd] = h_ref[:, sl].astype(F32)
            wg = wg_ref[:, sl]
            wv = wv_ref[:, sl]
            tg = _taps(xg, c, wd, SUB, TE, 3)
            tv = _taps(xv, c, wd, SUB, TE, 3)
            gg = _conv(tg, wg)
            vv = _conv(tv, wv)
            sg = _sigmoid(gg)
            dav = da_ref[:, sl].astype(F32)
            dgg = dav * vv * (sg * (1.0 + gg * (1.0 - sg)))
            dvv = dav * (gg * sg)
            for dc, tp, w, scr, dh_ref, dw_ref in ((dgg, tg, wg, dg, dhg_ref, dwg_ref), (dvv, tv, wv, dv, dhv_ref, dwv_ref)):
                scr[c, 0:TE, 0:wd] = dc
                dh_ref[:, sl] = _conv_t(scr, c, wd, w).astype(BF16)
                dw_ref[:, sl] += jnp.concatenate([jnp.sum(tp[j] * dc, axis=0, keepdims=True) for j in range(3)], axis=0)

    rev = lambda off: pl.BlockSpec((None, TE, FB), lambda p, i: (p + off, n - 1 - i, 0))
    halo = lambda off: pl.BlockSpec((None, 16, FB), lambda p, i: (p + off, jnp.maximum((n - 1 - i) * hb - 1, 0), 0))
    wspec = lambda off: pl.BlockSpec((None, 3, FB), lambda p, i: (p + off, 0, 0))
    dwspec = pl.BlockSpec((None, 3, FB), lambda p, i: (p, 0, 0))
    return _pallas_call(
        body, name=name, grid=(4, n), in_specs=[rev(0), rev(0), rev(4), halo(0), halo(4), wspec(0), wspec(4)],
        out_specs=[rev(0), rev(0), dwspec, dwspec],
        out_shape=[jax.ShapeDtypeStruct((4, t, FB), BF16), jax.ShapeDtypeStruct((4, t, FB), BF16),
                   jax.ShapeDtypeStruct((4, 3, FB), F32), jax.ShapeDtypeStruct((4, 3, FB), F32)],
        scratch_shapes=[_seq_scratch(FB)] * 4,
        compiler_params=_params(("parallel", "arbitrary"), TE * FB * 24),
    )(da, hid, hid, hid, hid, cw, cw)


def _adamw(lands, w, m, v, name):
    nl, r, c = w.shape
    tr = r
    if r * c * 4 > (2 << 20):
        for cand in (128, 64, 32, 16):
            if r % cand == 0:
                tr = cand
                break
    nr = r // tr
    c1 = 1.0 - ADAM_B1 ** ADAM_STEP
    c2 = 1.0 - ADAM_B2 ** ADAM_STEP

    def body(*refs):
        l_refs, (w_ref, m_ref, v_ref, g_out, d_out, m_out, v_out) = refs[:nl], refs[nl:]
        layer = pl.program_id(0)
        g = None
        for l, l_ref in enumerate(l_refs):
            gl = l_ref[0].astype(F32)
            for i in range(1, NDEV):
                gl = gl + l_ref[i].astype(F32)
            g = gl if g is None else jnp.where(layer == l, gl, g)
        mn = ADAM_B1 * m_ref[...] + (1.0 - ADAM_B1) * g
        vn = ADAM_B2 * v_ref[...] + (1.0 - ADAM_B2) * (g * g)
        g_out[...] = g
        m_out[...] = mn
        v_out[...] = vn
        d_out[...] = -ADAM_LR * ((mn / c1) / (jnp.sqrt(vn / c2) + ADAM_EPS) + ADAM_WD * w_ref[...])

    def land_spec(l):
        return pl.BlockSpec((NDEV, tr, c), lambda ly, i: (0, jnp.where(ly == l, i, 0 if l > 0 else nr - 1), 0))

    spec = pl.BlockSpec((None, tr, c), lambda ly, i: (ly, i, 0))
    shp = jax.ShapeDtypeStruct((nl, r, c), F32)
    return _pallas_call(
        body, name=name, grid=(nl, nr), in_specs=[land_spec(l) for l in range(nl)] + [spec, spec, spec],
        out_specs=[spec] * 4, out_shape=[shp] * 4,
        compiler_params=_params(("arbitrary", "arbitrary"), (11 + 4 * nl) * tr * c * 4),
    )(*lands, w, m, v)


def _layer_fwd(h, p, tag, gather=(), finish=None):
    u = _rmsnorm_fwd(h, p["norm_mix"], f"norm_mix_{tag}")
    qkv_pre = _mm(u, p["w_qkv"], BF16, f"proj_qkv_{tag}")
    z = _mm(u, p["w_z"], BF16, f"proj_z_{tag}")
    ba = _mm(u, p["w_ba"], F32, f"proj_ba_{tag}")
    pool_in = _mm(u, p["w_pl"], F32, f"proj_pool_{tag}")
    gate = _mm(u, p["w_gate"], BF16, f"proj_gate_{tag}")
    qkv = _gdn_pre_fwd(qkv_pre, p["conv_qkv"], f"gdn_pre_{tag}")
    beta, g = _gates_fwd(ba, p["a_row"], p["dt_row"], f"gates_{tag}")
    (o, s_in, vnew, tinv, wsv), gathered = _gdn_fwd(qkv, beta, g, f"gdn_{tag}", gather)
    if finish is not None:
        p = {**p, **finish(gathered)}
    pooled = _pool_fwd(pool_in, f"pool_{tag}")
    pm = _mm_cols(pooled, p["w_pool"], F32, f"pool_mm_{tag}", _NN)
    y = _post_fwd(o, z, gate, pm, p["head_norm"], p["pool_scale"], f"post_{tag}")
    h1 = _mm(y, p["w_out"], F32, f"out_proj_{tag}", res=h)
    u2 = _rmsnorm_fwd(h1, p["norm_ffn"], f"norm_ffn_{tag}")
    hid = _mm_up(u2, p["w_up"], f"up_proj_{tag}")
    act = _mlp_act_fwd(hid, p["conv_ffn"], f"mlp_act_{tag}")
    h2 = _mm_blocks_red(act, p["w_down"], f"down_proj_{tag}", _NN, res=h1)
    saved = dict(h=h, u=u, qkv_pre=qkv_pre, z=z, ba=ba, gate=gate, qkv=qkv, beta=beta, g=g, o=o, s_in=s_in, vnew=vnew,
                 tinv=tinv, wsv=wsv, pooled=pooled, pm=pm, y=y, h1=h1, u2=u2, hid=hid, act=act)
    return h2, saved, gathered, p


def _layer_bwd(dh, dh_b, p, s, tag, scatter=()):
    gr = {}
    da = _mm_to_blocks(dh_b, p["w_down"], f"d_act_{tag}")
    gr["w_down"] = _mm_tn_blocks(s["act"], dh_b, f"dw_down_{tag}", True, False)
    dhg, dhv, dwg, dwv = _mlp_act_bwd(da, s["hid"], p["conv_ffn"], f"mlp_act_bwd_{tag}")
    gr["conv_ffn"] = jnp.concatenate([dwg, dwv], axis=0)
    w_up = p["w_up"]
    du2 = _mm_blocks_red(dhg, w_up[:4], f"d_u2g_{tag}", _NT)
    du2 = _mm_blocks_red(dhv, w_up[4:], f"d_u2v_{tag}", _NT, res=du2)
    gr["w_up"] = jnp.concatenate([_mm_tn_blocks(s["u2"], dhg, f"dw_upg_{tag}", False, True),
                                  _mm_tn_blocks(s["u2"], dhv, f"dw_upv_{tag}", False, True)], axis=0)
    dh1, dh1_b, gr["norm_ffn"] = _rmsnorm_bwd(s["h1"], du2, dh, p["norm_ffn"], f"norm_ffn_bwd_{tag}")
    dy = _mm(dh1_b, p["w_out"], F32, f"d_y_{tag}", dims=_NT)
    gr["w_out"] = _mm_tn(s["y"], dh1_b, f"dw_out_{tag}")
    do, dz, dgate, dpm, gr["head_norm"], gr["pool_scale"] = _post_bwd(
        dy, s["o"], s["z"], s["gate"], s["pm"], p["head_norm"], p["pool_scale"], f"post_bwd_{tag}")
    dpooled = _mm_cols(dpm, p["w_pool"], F32, f"d_pooled_{tag}", _NT)
    gr["w_pool"] = _mm_tn_cols(s["pooled"], dpm, 4, f"dw_pool_{tag}")
    dpool_in = _pool_bwd(dpooled, f"pool_bwd_{tag}")
    own = (gr["w_up"].astype(BF16), gr["w_down"].reshape(NDEV, -1, D).astype(BF16))
    (dqkv, dbeta, dg), landed = _gdn_bwd(s["qkv"], s["beta"], s["g"], do, s["s_in"], s["vnew"], s["tinv"], s["wsv"],
                                         f"gdn_bwd_{tag}", own + tuple(scatter))
    dba, gr["a_log"], gr["dt_bias"] = _gates_bwd(s["ba"], p["a_row"], p["dt_row"], dbeta, dg, f"gates_bwd_{tag}")
    dqkv_pre, gr["conv_qkv"] = _gdn_pre_bwd(s["qkv_pre"], p["conv_qkv"], dqkv, f"gdn_pre_bwd_{tag}")
    du = None
    dws = []
    for nm, dseg, wseg in (("qkv", dqkv_pre, p["w_qkv"]), ("z", dz, p["w_z"]), ("ba", dba, p["w_ba"]),
                           ("pool", dpool_in, p["w_pl"]), ("gate", dgate, p["w_gate"])):
        du = _mm(dseg, wseg, F32, f"d_u_{nm}_{tag}", res=du, dims=_NT)
        dws.append(_mm_tn(s["u"], dseg, f"dw_{nm}_{tag}"))
    gr["w_in"] = jnp.concatenate([dws[0], dws[1], dws[2][:, 0:H], dws[2][:, LANE:LANE + H], dws[3], dws[4]], axis=1)
    dh0, dh0_b, gr["norm_mix"] = _rmsnorm_bwd(s["h"], du, dh1, p["norm_mix"], f"norm_mix_bwd_{tag}")
    return dh0, dh0_b, gr, landed


def _pad_lanes(v8):
    return jnp.pad(v8.reshape(1, H), ((0, 0), (0, LANE - H)))


def _pack(parts, rows, lead=1):
    flat = jnp.concatenate([q.reshape(lead, -1) for q in parts], axis=1)
    flat = jnp.pad(flat, ((0, 0), (0, rows * LANE - flat.shape[1])))
    return flat.reshape((lead, rows, LANE) if lead > 1 else (rows, LANE))


def _unpack(packed, shapes, lead=1):
    flat = packed.reshape(lead, -1)
    out, off = [], 0
    for shp in shapes:
        n = 1
        for s_ in shp:
            n *= s_
        n //= lead
        out.append(flat[:, off:off + n].reshape(shp))
        off += n
    return out


SMALL_ROWS = 336
REPL_ROWS = 64


def kernel(x, meta_tokens, norm_mix, w_in, conv_qkv, a_log, dt_bias, head_norm, w_pool, pool_scale, w_out, norm_ffn, w_up, conv_ffn, w_down, norm_final, loss_target, m_meta_tokens, m_norm_mix, m_w_in, m_conv_qkv, m_a_log, m_dt_bias, m_head_norm, m_w_pool, m_pool_scale, m_w_out, m_norm_ffn, m_w_up, m_conv_ffn, m_w_down, m_norm_final, v_meta_tokens, v_norm_mix, v_w_in, v_conv_qkv, v_a_log, v_dt_bias, v_head_norm, v_w_pool, v_pool_scale, v_w_out, v_norm_ffn, v_w_up, v_conv_ffn, v_w_down, v_norm_final):
    seq = x.shape[1]
    t = ROW0 + seq
    assert t % TE == 0 and t % (MM_TILES * 16) == 0 and t % (GC * CH) == 0
    depth = w_in.shape[0]
    assert depth == 2
    cin = w_in.shape[2]

    def mixer_params(l, g_in, conv_q, conv_f, wp):
        wf = jnp.transpose(g_in, (1, 0, 2)).reshape(D, NDEV * cin)
        zpad = jnp.zeros((D, LANE - H), BF16)
        return dict(
            w_qkv=wf[:, 0:QKV], w_z=wf[:, QKV:QKV + D],
            w_ba=jnp.concatenate([wf[:, 4096:4104], zpad, wf[:, 4104:4112], zpad], axis=1),
            w_pl=wf[:, 4112:4624], w_gate=wf[:, 4624:6672], conv_qkv=conv_q, conv_ffn=conv_f, w_pool=wp,
            norm_mix=norm_mix[l].reshape(1, D), norm_ffn=norm_ffn[l].reshape(1, D),
            pool_scale=pool_scale[l].reshape(1, D), head_norm=head_norm[l].reshape(1, DH),
            a_row=_pad_lanes(a_log[l]), dt_row=_pad_lanes(dt_bias[l]))

    def late_params(g_up, g_out, g_down):
        return dict(w_out=g_out.reshape(D, D), w_up=g_up, w_down=g_down.reshape(4, FB, D))

    small_shapes = [conv_qkv.shape, conv_ffn.shape, w_pool.shape, meta_tokens.shape]
    small = _pack([conv_qkv, conv_ffn, w_pool, meta_tokens], SMALL_ROWS)
    w_in_b, w_up_b, w_out_b, w_down_b = w_in.astype(BF16), w_up.astype(BF16), w_out.astype(BF16), w_down.astype(BF16)
    g_in0, g_small = _gather([w_in_b[0], small], "gather_first")
    s_cq, s_cf, s_wp, s_mt = _unpack(g_small, [(NDEV,) + shp for shp in small_shapes], lead=NDEV)
    conv_qkv_full = jnp.transpose(s_cq, (1, 2, 0, 3)).reshape(depth, 4, QKV)
    conv_ffn_blk = jnp.transpose(s_cf, (1, 0, 2, 3))
    w_pool_full = jnp.transpose(s_wp, (1, 2, 3, 0, 4)).reshape(depth, 4, DH, 2 * DH).astype(BF16)
    meta_full = jnp.transpose(s_mt, (1, 0, 2)).reshape(N_META, D)

    h = jnp.concatenate([jnp.zeros((LEAD, D), F32), meta_full, x[0]], axis=0)
    p0 = mixer_params(0, g_in0, conv_qkv_full[0], conv_ffn_blk[0], w_pool_full[0])
    h, sv0, rest, p0 = _layer_fwd(
        h, p0, "l0", (w_up_b[0], w_out_b[0], w_down_b[0], w_in_b[1], w_up_b[1], w_out_b[1], w_down_b[1]),
        lambda got: late_params(*got[:3]))
    p1 = {**mixer_params(1, rest[3], conv_qkv_full[1], conv_ffn_blk[1], w_pool_full[1]), **late_params(*rest[4:])}
    h, sv1, _, _ = _layer_fwd(h, p1, "l1")
    layers = [p0, p1]
    saved = [sv0, sv1]
    target = jnp.concatenate([jnp.zeros((ROW0, D), F32), loss_target[0]], axis=0)
    dh, dh_b, d_norm_final, loss_row = _loss_bwd(h, target, norm_final.reshape(1, D), "loss")

    def mixer_blocks(gr):
        return (jnp.transpose(gr["w_in"].reshape(D, NDEV, cin), (1, 0, 2)).astype(BF16),
                gr["w_out"].reshape(NDEV, D // NDEV, D).astype(BF16))

    grads = [None] * depth
    dh, dh_b, grads[1], (l_up1, l_down1) = _layer_bwd(dh, dh_b, layers[1], saved[1], "l1")
    dh, dh_b, grads[0], (l_up0, l_down0, l_in1, l_out1) = _layer_bwd(dh, dh_b, layers[0], saved[0], "l0", mixer_blocks(grads[1]))
    grad_x = dh[ROW0:].reshape(1, seq, D)
    d_meta = dh[LEAD:ROW0]

    stk = lambda name: jnp.stack([grads[l][name] for l in range(depth)], axis=0)
    cq = conv_qkv.shape[2]
    pw = w_pool.shape[3]
    s_cq = jnp.transpose(stk("conv_qkv").reshape(depth, 4, NDEV, cq), (2, 0, 1, 3))
    s_cf = jnp.transpose(stk("conv_ffn"), (1, 0, 2, 3))
    s_wp = jnp.transpose(stk("w_pool").reshape(depth, 4, DH, NDEV, pw), (3, 0, 1, 2, 4))
    s_mt = jnp.transpose(d_meta.reshape(N_META, NDEV, D // NDEV), (1, 0, 2))
    b_small = _pack([s_cq, s_cf, s_wp, s_mt], SMALL_ROWS, lead=NDEV)
    l_in0, l_out0, l_small = _scatter([*mixer_blocks(grads[0]), b_small], "exchange_last")

    r_in = _adamw((l_in0, l_in1), w_in, m_w_in, v_w_in, "adamw_w_in")
    r_up = _adamw((l_up0, l_up1), w_up, m_w_up, v_w_up, "adamw_w_up")
    r_out = _adamw((l_out0, l_out1), w_out, m_w_out, v_w_out, "adamw_w_out")
    r_down = _adamw((l_down0, l_down1), w_down, m_w_down, v_w_down, "adamw_w_down")
    r_small = _adamw((l_small,), small[None], _pack([m_conv_qkv, m_conv_ffn, m_w_pool, m_meta_tokens], SMALL_ROWS)[None],
                     _pack([v_conv_qkv, v_conv_ffn, v_w_pool, v_meta_tokens], SMALL_ROWS)[None], "adamw_small")
    r_small = [_unpack(o_[0], small_shapes) for o_ in r_small]

    repl_shapes = [norm_mix.shape, a_log.shape, dt_bias.shape, head_norm.shape, pool_scale.shape, norm_ffn.shape,
                   norm_final.shape, (1,)]
    rp = lambda name, n: jnp.stack([grads[l][name][0, :n] for l in range(depth)], axis=0)
    part = _pack([rp("norm_mix", D), rp("a_log", H), rp("dt_bias", H), rp("head_norm", DH), rp("pool_scale", D),
                  rp("norm_ffn", D), d_norm_final[0], loss_row[0, 0:1]], REPL_ROWS)
    (l_repl,) = _gather([part], "gather_replicated")
    zero1 = jnp.zeros((1,), F32)
    r_repl = _adamw(
        (l_repl,), _pack([norm_mix, a_log, dt_bias, head_norm, pool_scale, norm_ffn, norm_final, zero1], REPL_ROWS)[None],
        _pack([m_norm_mix, m_a_log, m_dt_bias, m_head_norm, m_pool_scale, m_norm_ffn, m_norm_final, zero1], REPL_ROWS)[None],
        _pack([v_norm_mix, v_a_log, v_dt_bias, v_head_norm, v_pool_scale, v_norm_ffn, v_norm_final, zero1], REPL_ROWS)[None],
        "adamw_replicated")
    r_repl = [_unpack(o_[0], repl_shapes) for o_ in r_repl]
    loss = r_repl[0][7].reshape(())

    def leaf(kind):
        sm, rr = r_small[kind], r_repl[kind]
        return [sm[3], rr[0], r_in[kind], sm[0], rr[1], rr[2], rr[3], sm[2], rr[4], r_out[kind], rr[5], r_up[kind],
                sm[1], r_down[kind], rr[6]]

    return (loss, grad_x, *leaf(0), *leaf(1), *leaf(2), *leaf(3))
```

```python
import jax
import jax.numpy as jnp
from jax import lax
from jax.experimental import pallas as pl
from jax.experimental.pallas import tpu as pltpu

F32 = jnp.float32
BF16 = jnp.bfloat16
MESH = pl.DeviceIdType.MESH

D = 1024
H = 8
DH = 128
CH = 64
N_META = 16
LEAD = 48
ROW0 = LEAD + N_META
QKV = 3 * D
POOL_W = 512
POOL_WINDOWS = (2, 4, 8, 16)
FB = 704
NDEV = 8
EPS = 1e-6
MM_TILES = 12
TE = 192
LANE = 128
SUB = 8
VMEM_CAP = 56 << 20

ADAM_LR, ADAM_B1, ADAM_B2, ADAM_EPS, ADAM_WD, ADAM_STEP = 0.001, 0.9, 0.999, 1e-08, 0.01, 10

_NN = (((1,), (0,)), ((), ()))
_NT = (((1,), (1,)), ((), ()))
_TN = (((0,), (0,)), ((), ()))


def _dot(a, b, dims=_NN, precision=None):
    return lax.dot_general(a, b, dims, precision=precision, preferred_element_type=F32)


def _bdot(a, b, dims=_NN):
    return _dot(a.astype(BF16), b.astype(BF16), dims)


def _nbytes(shape, dtype):
    n = 1
    for s in shape:
        n *= s
    return n * jnp.dtype(dtype).itemsize


def _params(sem, block_bytes):
    limit = min(VMEM_CAP, 2 * block_bytes + (20 << 20))
    return pltpu.CompilerParams(dimension_semantics=sem, vmem_limit_bytes=limit)


PIN_BYTES = 12 << 20


def _pallas_call(body, *, out_shape, **kw):
    call = pl.pallas_call

    def big(s):
        return len(s.shape) >= 2 and s.shape[-1] >= D and _nbytes(s.shape, s.dtype) >= PIN_BYTES

    pinned = jax.tree.map(lambda s: pltpu.HBM(s.shape, s.dtype) if big(s) else s, out_shape)

    def run(*args):
        return call(body, out_shape=pinned, **kw)(
            *[pltpu.with_memory_space_constraint(a, pltpu.HBM) if big(a) else a for a in args])

    return run


def _sigmoid(x):
    return 1.0 / (1.0 + jnp.exp(-x))


def _col_tile(n):
    for t in (1024, 512, 256, 128):
        if n % t == 0:
            return t
    return n


def _matmul(a, b, *, dims, grid, a_spec, b_spec, o_spec, out_shape, name, red_axis=None, res=None):
    def body(*refs):
        if res is None:
            a_ref, b_ref, o_ref = refs
        else:
            a_ref, b_ref, r_ref, o_ref = refs
        part = _dot(a_ref[...], b_ref[...], dims)
        if red_axis is None:
            if res is not None:
                part = part + r_ref[...]
            o_ref[...] = part.astype(o_ref.dtype)
        else:
            r = pl.program_id(red_axis)

            @pl.when(r == 0)
            def _():
                o_ref[...] = part + r_ref[...] if res is not None else part

            @pl.when(r > 0)
            def _():
                o_ref[...] += part

    def blk(spec, arr):
        return _nbytes([s for s in spec.block_shape if s is not None], arr.dtype)

    ins = [a, b] + ([res] if res is not None else [])
    specs = [a_spec, b_spec] + ([o_spec] if res is not None else [])
    nb = blk(a_spec, a) + blk(b_spec, b) + 2 * _nbytes([s for s in o_spec.block_shape if s is not None], F32)
    sem = tuple("arbitrary" if i == red_axis else "parallel" for i in range(len(grid)))
    return _pallas_call(
        body, name=name, grid=grid, in_specs=specs, out_specs=o_spec, out_shape=out_shape,
        compiler_params=_params(sem, nb),
    )(*ins)


def _row_tiles(m, row_bytes, fixed_bytes, temp_row_bytes=0):
    for nt in (MM_TILES // 2, MM_TILES):
        tm = m // nt
        if 2 * (row_bytes * tm + fixed_bytes) + temp_row_bytes * tm <= VMEM_CAP - (10 << 20):
            return nt
    return MM_TILES


def _mm(a, b, out_dtype, name, res=None, dims=_NN):
    m, k = a.shape
    n = b.shape[1] if dims == _NN else b.shape[0]
    tn = _col_tile(n)
    nt = _row_tiles(m, 2 * k + tn * (jnp.dtype(out_dtype).itemsize + (4 if res is not None else 0)), 2 * k * tn, 4 * tn)
    tm = m // nt
    if dims == _NN:
        b_spec = pl.BlockSpec((k, tn), lambda j, i: (0, j))
    else:
        b_spec = pl.BlockSpec((tn, k), lambda j, i: (j, 0))
    return _matmul(
        a, b, dims=dims, grid=(n // tn, nt), a_spec=pl.BlockSpec((tm, k), lambda j, i: (i, 0)), b_spec=b_spec,
        o_spec=pl.BlockSpec((tm, tn), lambda j, i: (i, j)), out_shape=jax.ShapeDtypeStruct((m, n), out_dtype),
        name=name, res=res)


def _mm_tn(a, g, name):
    m, k = a.shape
    n = g.shape[1]
    tn = _col_tile(n)
    nt = _row_tiles(m, 2 * k + 2 * tn, 4 * k * tn)
    tm = m // nt
    return _matmul(
        a, g, dims=_TN, grid=(n // tn, nt), red_axis=1, a_spec=pl.BlockSpec((tm, k), lambda j, i: (i, 0)),
        b_spec=pl.BlockSpec((tm, tn), lambda j, i: (i, j)), o_spec=pl.BlockSpec((k, tn), lambda j, i: (0, j)),
        out_shape=jax.ShapeDtypeStruct((k, n), F32), name=name)


def _mm_up(u, w_up, name):
    t = u.shape[0]
    g = w_up.shape[0]
    nt = _row_tiles(t, 2 * D + 2 * FB, 2 * D * FB, 4 * FB)
    tm = t // nt
    return _matmul(
        u, w_up, dims=_NN, grid=(g, nt), a_spec=pl.BlockSpec((tm, D), lambda g_, i: (i, 0)),
        b_spec=pl.BlockSpec((None, D, FB), lambda g_, i: (g_, 0, 0)),
        o_spec=pl.BlockSpec((None, tm, FB), lambda g_, i: (g_, i, 0)),
        out_shape=jax.ShapeDtypeStruct((g, t, FB), BF16), name=name)


def _mm_blocks_red(a, b, name, dims, res=None):
    g, t, k = a.shape
    n = b.shape[2] if dims == _NN else b.shape[1]
    nt = _row_tiles(t, 2 * k + n * (8 if res is not None else 4), 2 * k * n, 4 * n)
    tm = t // nt
    return _matmul(
        a, b, dims=dims, grid=(nt, g), red_axis=1, a_spec=pl.BlockSpec((None, tm, k), lambda i, g_: (g_, i, 0)),
        b_spec=pl.BlockSpec((None,) + b.shape[1:], lambda i, g_: (g_, 0, 0)),
        o_spec=pl.BlockSpec((tm, n), lambda i, g_: (i, 0)), out_shape=jax.ShapeDtypeStruct((t, n), F32),
        name=name, res=res)


def _mm_to_blocks(a, b, name):
    t, k = a.shape
    g, n, _ = b.shape
    nt = _row_tiles(t, 2 * k + 2 * n, 2 * k * n, 4 * n)
    tm = t // nt
    return _matmul(
        a, b, dims=_NT, grid=(g, nt), a_spec=pl.BlockSpec((tm, k), lambda g_, i: (i, 0)),
        b_spec=pl.BlockSpec((None, n, k), lambda g_, i: (g_, 0, 0)),
        o_spec=pl.BlockSpec((None, tm, n), lambda g_, i: (g_, i, 0)),
        out_shape=jax.ShapeDtypeStruct((g, t, n), BF16), name=name)


def _mm_tn_blocks(a, g, name, a_blocked, g_blocked):
    nb = a.shape[0] if a_blocked else g.shape[0]
    t = a.shape[-2]
    k, n = a.shape[-1], g.shape[-1]
    nt = _row_tiles(t, 2 * k + 2 * n, 4 * k * n)
    tm = t // nt
    a_spec = (pl.BlockSpec((None, tm, k), lambda g_, i: (g_, i, 0)) if a_blocked
              else pl.BlockSpec((tm, k), lambda g_, i: (i, 0)))
    g_spec = (pl.BlockSpec((None, tm, n), lambda g_, i: (g_, i, 0)) if g_blocked
              else pl.BlockSpec((tm, n), lambda g_, i: (i, 0)))
    return _matmul(
        a, g, dims=_TN, grid=(nb, nt), red_axis=1, a_spec=a_spec, b_spec=g_spec,
        o_spec=pl.BlockSpec((None, k, n), lambda g_, i: (g_, 0, 0)),
        out_shape=jax.ShapeDtypeStruct((nb, k, n), F32), name=name)


def _mm_cols(a, b, out_dtype, name, dims):
    t = a.shape[0]
    g = b.shape[0]
    ka = a.shape[1] // g
    n = b.shape[2] if dims == _NN else b.shape[1]
    tm = t // MM_TILES
    return _matmul(
        a, b, dims=dims, grid=(g, MM_TILES), a_spec=pl.BlockSpec((tm, ka), lambda g_, i: (i, g_)),
        b_spec=pl.BlockSpec((None,) + b.shape[1:], lambda g_, i: (g_, 0, 0)),
        o_spec=pl.BlockSpec((tm, n), lambda g_, i: (i, g_)), out_shape=jax.ShapeDtypeStruct((t, g * n), out_dtype),
        name=name)


def _mm_tn_cols(a, g, nblk, name):
    t = a.shape[0]
    ka, n = a.shape[1] // nblk, g.shape[1] // nblk
    tm = t // MM_TILES
    return _matmul(
        a, g, dims=_TN, grid=(nblk, MM_TILES), red_axis=1, a_spec=pl.BlockSpec((tm, ka), lambda g_, i: (i, g_)),
        b_spec=pl.BlockSpec((tm, n), lambda g_, i: (i, g_)), o_spec=pl.BlockSpec((None, ka, n), lambda g_, i: (g_, 0, 0)),
        out_shape=jax.ShapeDtypeStruct((nblk, ka, n), F32), name=name)


def _rows(cols, n=None):
    if n is None:
        return pl.BlockSpec((TE, cols), lambda i: (i, 0))
    return pl.BlockSpec((TE, cols), lambda i: (n - 1 - i, 0))


def _whole(shape):
    return pl.BlockSpec(shape, lambda *_: (0,) * len(shape))


def _row_ids(i, rows=TE):
    return i * rows + lax.broadcasted_iota(jnp.int32, (rows, 1), 0)


def _rmsnorm_fwd(h, gain, name):
    t = h.shape[0]

    def body(h_ref, g_ref, u_ref):
        x = h_ref[...]
        r = lax.rsqrt(jnp.mean(x * x, axis=-1, keepdims=True) + EPS)
        u_ref[...] = (x * r * g_ref[...]).astype(BF16)

    return _pallas_call(
        body, name=name, grid=(t // TE,), in_specs=[_rows(D), _whole((1, D))], out_specs=_rows(D),
        out_shape=jax.ShapeDtypeStruct((t, D), BF16), compiler_params=_params(("parallel",), 3 * TE * D * 4),
    )(h, gain)


def _rmsnorm_bwd(x, du, dres, gain, name):
    t = x.shape[0]

    def body(x_ref, du_ref, dr_ref, g_ref, dx_ref, dxb_ref, dg_ref):
        i = pl.program_id(0)
        xv = x_ref[...]
        r = lax.rsqrt(jnp.mean(xv * xv, axis=-1, keepdims=True) + EPS)
        gdy = du_ref[...] * g_ref[...]
        dx = dr_ref[...] + r * gdy - xv * (r * r * r) * jnp.mean(xv * gdy, axis=-1, keepdims=True)
        dx = jnp.where(_row_ids(i) >= LEAD, dx, 0.0)
        dx_ref[...] = dx
        dxb_ref[...] = dx.astype(BF16)
        part = jnp.sum(du_ref[...] * xv * r, axis=0, keepdims=True)

        @pl.when(i == 0)
        def _():
            dg_ref[...] = part

        @pl.when(i > 0)
        def _():
            dg_ref[...] += part

    return _pallas_call(
        body, name=name, grid=(t // TE,), in_specs=[_rows(D), _rows(D), _rows(D), _whole((1, D))],
        out_specs=[_rows(D), _rows(D), _whole((1, D))],
        out_shape=[jax.ShapeDtypeStruct((t, D), F32), jax.ShapeDtypeStruct((t, D), BF16),
                   jax.ShapeDtypeStruct((1, D), F32)],
        compiler_params=_params(("arbitrary",), 5 * TE * D * 4),
    )(x, du, dres, gain)


def _loss_bwd(h, target, gain, name):
    t = h.shape[0]

    def body(h_ref, t_ref, g_ref, dx_ref, dxb_ref, dg_ref, loss_ref):
        i = pl.program_id(0)
        xv = h_ref[...]
        gain_v = g_ref[...]
        r = lax.rsqrt(jnp.mean(xv * xv, axis=-1, keepdims=True) + EPS)
        real = _row_ids(i) >= ROW0
        err = jnp.where(real, xv * r * gain_v - t_ref[...], 0.0)
        dy = err * (1.0 / D)
        gdy = dy * gain_v
        dx = r * gdy - xv * (r * r * r) * jnp.mean(xv * gdy, axis=-1, keepdims=True)
        dx_ref[...] = dx
        dxb_ref[...] = dx.astype(BF16)
        dgp = jnp.sum(dy * xv * r, axis=0, keepdims=True)
        lp = 0.5 * jnp.sum(jnp.mean(err * err, axis=-1, keepdims=True), axis=0, keepdims=True)

        @pl.when(i == 0)
        def _():
            dg_ref[...] = dgp
            loss_ref[...] = jnp.broadcast_to(lp, (1, LANE))

        @pl.when(i > 0)
        def _():
            dg_ref[...] += dgp
            loss_ref[...] += jnp.broadcast_to(lp, (1, LANE))

    return _pallas_call(
        body, name=name, grid=(t // TE,), in_specs=[_rows(D), _rows(D), _whole((1, D))],
        out_specs=[_rows(D), _rows(D), _whole((1, D)), _whole((1, LANE))],
        out_shape=[jax.ShapeDtypeStruct((t, D), F32), jax.ShapeDtypeStruct((t, D), BF16),
                   jax.ShapeDtypeStruct((1, D), F32), jax.ShapeDtypeStruct((1, LANE), F32)],
        compiler_params=_params(("arbitrary",), 4 * TE * D * 4),
    )(h, target, gain)


def _seq_scratch(cols):
    return pltpu.VMEM((-(-cols // LANE), TE + SUB, LANE), F32)


def _taps(scr, c, wd, first, n, k):
    return [scr[c, first - (k - 1) + j:first - (k - 1) + j + n, 0:wd] for j in range(k)]


def _stage_history(scr, i):
    @pl.when(i == 0)
    def _():
        scr[...] = jnp.zeros(scr.shape, F32)

    @pl.when(i > 0)
    def _():
        scr[:, 0:SUB, :] = scr[:, TE:TE + SUB, :]


def _stage_future(scr, i):
    @pl.when(i == 0)
    def _():
        scr[...] = jnp.zeros(scr.shape, F32)

    @pl.when(i > 0)
    def _():
        scr[:, TE:TE + SUB, :] = scr[:, 0:SUB, :]


def _conv(tp, w):
    out = w[0:1] * tp[0]
    for j in range(1, len(tp)):
        out = out + w[j:j + 1] * tp[j]
    return out


def _conv_t(ds, c, wd, w):
    k = w.shape[0]
    out = w[k - 1:k] * ds[c, 0:TE, 0:wd]
    for j in range(k - 1):
        out = out + w[j:j + 1] * ds[c, k - 1 - j:k - 1 - j + TE, 0:wd]
    return out


def _gdn_pre_fwd(x, w, name):
    t = x.shape[0]

    def body(x_ref, w_ref, o_ref, xs):
        _stage_history(xs, pl.program_id(0))
        for hh in range(3 * H):
            sl = slice(hh * DH, (hh + 1) * DH)
            xs[hh, SUB:SUB + TE, :] = x_ref[:, sl].astype(F32)
            cv = _conv(_taps(xs, hh, DH, SUB, TE, 4), w_ref[:, sl])
            s = cv * _sigmoid(cv)
            if hh < 2 * H:
                s = s * lax.rsqrt(jnp.sum(s * s, axis=-1, keepdims=True) + EPS)
                if hh < H:
                    s = s * (DH ** -0.5)
            o_ref[:, sl] = s

    return _pallas_call(
        body, name=name, grid=(t // TE,), in_specs=[_rows(QKV), _whole((4, QKV))], out_specs=_rows(QKV),
        out_shape=jax.ShapeDtypeStruct((t, QKV), F32), scratch_shapes=[_seq_scratch(QKV)],
        compiler_params=_params(("arbitrary",), TE * QKV * 8),
    )(x, w)


def _gdn_pre_bwd(x, w, dqkv, name):
    t = x.shape[0]
    n = t // TE
    hb = TE // 16

    def body(x_ref, xp_ref, w_ref, d_ref, dx_ref, dw_ref, xs, ds):
        i = pl.program_id(0)

        @pl.when(i == 0)
        def _():
            dw_ref[...] = jnp.zeros_like(dw_ref)

        _stage_future(ds, i)
        for hh in range(3 * H):
            sl = slice(hh * DH, (hh + 1) * DH)
            xs[hh, 0:SUB, :] = jnp.where(i == n - 1, 0.0, xp_ref[SUB:2 * SUB, sl].astype(F32))
            xs[hh, SUB:SUB + TE, :] = x_ref[:, sl].astype(F32)
            wv = w_ref[:, sl]
            tp = _taps(xs, hh, DH, SUB, TE, 4)
            cv = _conv(tp, wv)
            sg = _sigmoid(cv)
            s = cv * sg
            dsv = d_ref[:, sl]
            if hh < 2 * H:
                if hh < H:
                    dsv = dsv * (DH ** -0.5)
                r = lax.rsqrt(jnp.sum(s * s, axis=-1, keepdims=True) + EPS)
                dsv = r * dsv - s * (r * r * r) * jnp.sum(s * dsv, axis=-1, keepdims=True)
            dcv = dsv * (sg * (1.0 + cv * (1.0 - sg)))
            ds[hh, 0:TE, :] = dcv
            dx_ref[:, sl] = _conv_t(ds, hh, DH, wv).astype(BF16)
            dw_ref[:, sl] += jnp.concatenate([jnp.sum(tp[j] * dcv, axis=0, keepdims=True) for j in range(4)], axis=0)

    return _pallas_call(
        body, name=name, grid=(n,),
        in_specs=[_rows(QKV, n), pl.BlockSpec((16, QKV), lambda i: (jnp.maximum((n - 1 - i) * hb - 1, 0), 0)),
                  _whole((4, QKV)), _rows(QKV, n)],
        out_specs=[_rows(QKV, n), _whole((4, QKV))],
        out_shape=[jax.ShapeDtypeStruct((t, QKV), BF16), jax.ShapeDtypeStruct((4, QKV), F32)],
        scratch_shapes=[_seq_scratch(QKV), _seq_scratch(QKV)],
        compiler_params=_params(("arbitrary",), TE * QKV * 14),
    )(x, x, w, dqkv)


def _softplus(x):
    return jnp.maximum(x, 0.0) + jnp.log(1.0 + jnp.exp(-jnp.abs(x)))


def _gates_fwd(ba, a_row, dt_row, name):
    t = ba.shape[0]

    def body(ba_ref, a_ref, dt_ref, b_out, g_out):
        real = _row_ids(pl.program_id(0)) >= LEAD
        b_out[...] = jnp.where(real, _sigmoid(ba_ref[:, 0:LANE]), 0.0)
        g = -jnp.exp(a_ref[...]) * _softplus(ba_ref[:, LANE:2 * LANE] + dt_ref[...])
        g_out[...] = jnp.where(real, g, 0.0)

    return _pallas_call(
        body, name=name, grid=(t // TE,), in_specs=[_rows(2 * LANE), _whole((1, LANE)), _whole((1, LANE))],
        out_specs=[_rows(LANE), _rows(LANE)],
        out_shape=[jax.ShapeDtypeStruct((t, LANE), F32), jax.ShapeDtypeStruct((t, LANE), F32)],
        compiler_params=_params(("parallel",), TE * LANE * 16),
    )(ba, a_row, dt_row)


def _gates_bwd(ba, a_row, dt_row, dbeta, dg, name):
    t = ba.shape[0]

    def body(ba_ref, a_ref, dt_ref, db_ref, dg_ref, dba_ref, da_out, ddt_out):
        i = pl.program_id(0)
        real = _row_ids(i) >= LEAD
        beta = _sigmoid(ba_ref[:, 0:LANE])
        draw_b = jnp.where(real, db_ref[...] * beta * (1.0 - beta), 0.0)
        pre = ba_ref[:, LANE:2 * LANE] + dt_ref[...]
        neg_a = -jnp.exp(a_ref[...])
        dgv = jnp.where(real, dg_ref[...], 0.0)
        draw_a = dgv * neg_a * _sigmoid(pre)
        dba_ref[:, 0:LANE] = draw_b.astype(BF16)
        dba_ref[:, LANE:2 * LANE] = draw_a.astype(BF16)
        dal = jnp.sum(dgv * neg_a * _softplus(pre), axis=0, keepdims=True)
        ddt = jnp.sum(draw_a, axis=0, keepdims=True)

        @pl.when(i == 0)
        def _():
            da_out[...] = dal
            ddt_out[...] = ddt

        @pl.when(i > 0)
        def _():
            da_out[...] += dal
            ddt_out[...] += ddt

    return _pallas_call(
        body, name=name, grid=(t // TE,),
        in_specs=[_rows(2 * LANE), _whole((1, LANE)), _whole((1, LANE)), _rows(LANE), _rows(LANE)],
        out_specs=[_rows(2 * LANE), _whole((1, LANE)), _whole((1, LANE))],
        out_shape=[jax.ShapeDtypeStruct((t, 2 * LANE), BF16), jax.ShapeDtypeStruct((1, LANE), F32),
                   jax.ShapeDtypeStruct((1, LANE), F32)],
        compiler_params=_params(("arbitrary",), TE * LANE * 24),
    )(ba, a_row, dt_row, dbeta, dg)


_OFFSETS = [(dx, dy, dc) for dx in (0, 1) for dy in (0, 1) for dc in (0, 1)][1:]
NPEER = len(_OFFSETS)
ANY_SPEC = pl.BlockSpec(memory_space=pl.ANY)


def _place():
    return lax.axis_index("x"), lax.axis_index("y"), lax.axis_index("c")


def _index(p):
    return 4 * p[0] + 2 * p[1] + p[2]


def _comm_scratch(n):
    return [pltpu.SemaphoreType.DMA((n * NPEER,)), pltpu.SemaphoreType.DMA((n * NPEER,)), pltpu.SemaphoreType.DMA((n,))]


def _scatter_copies(ins, outs, send, recv):
    me = _place()
    mi = _index(me)
    res = []
    for j, d in enumerate(_OFFSETS):
        peer = tuple(1 - v if bit else v for v, bit in zip(me, d))
        pi = _index(peer)
        for k in range(len(ins)):
            sem = k * NPEER + j
            mine = pltpu.make_async_remote_copy(src_ref=ins[k].at[pi], dst_ref=outs[k].at[mi], send_sem=send.at[sem],
                                                recv_sem=recv.at[sem], device_id=peer, device_id_type=MESH)
            theirs = pltpu.make_async_remote_copy(src_ref=ins[k].at[pi], dst_ref=outs[k].at[pi], send_sem=send.at[sem],
                                                  recv_sem=recv.at[sem], device_id=peer, device_id_type=MESH)
            res.append((mine, theirs))
    return res


def _scatter_own(ins, outs, loc):
    mi = _index(_place())
    return [pltpu.make_async_copy(ins[k].at[mi], outs[k].at[mi], loc.at[k]) for k in range(len(ins))]


def _scatter_start(ins, outs, send, recv, loc):
    for cp in _scatter_own(ins, outs, loc):
        cp.start()
    for mine, _ in _scatter_copies(ins, outs, send, recv):
        mine.start()


def _scatter_wait(ins, outs, send, recv, loc):
    cps = _scatter_copies(ins, outs, send, recv)
    for _, theirs in cps:
        theirs.wait_recv()
    for mine, _ in cps:
        mine.wait_send()
    for cp in _scatter_own(ins, outs, loc):
        cp.wait()


def _gather_parts(ins, outs, send, recv):
    x, y, c = _place()
    chips = [(1 - x, y), (x, 1 - y), (1 - x, 1 - y)]

    def cp(k, slot, src, block, to):
        return pltpu.make_async_remote_copy(src_ref=src, dst_ref=outs[k].at[_index(block)], send_sem=send.at[k * NPEER + slot],
                                            recv_sem=recv.at[k * NPEER + slot], device_id=to, device_id_type=MESH)

    return (x, y, c), (x, y, 1 - c), chips, cp


def _gather_start(ins, outs, send, recv, loc):
    me, sib, chips, cp = _gather_parts(ins, outs, send, recv)
    for k in range(len(ins)):
        pltpu.make_async_copy(ins[k], outs[k].at[_index(me)], loc.at[k]).start()
        cp(k, 0, ins[k], me, sib).start()
        for j, chip in enumerate(chips):
            cp(k, 1 + j, ins[k], me, (*chip, me[2])).start()


def _gather_forward(ins, outs, send, recv, loc):
    me, sib, chips, cp = _gather_parts(ins, outs, send, recv)
    for j, chip in enumerate(chips):
        blk = (*chip, me[2])
        for k in range(len(ins)):
            cp(k, 1 + j, ins[k], blk, me).wait_recv()
            cp(k, 4 + j, outs[k].at[_index(blk)], blk, sib).start()


def _gather_finish(ins, outs, send, recv, loc):
    me, sib, chips, cp = _gather_parts(ins, outs, send, recv)
    for k in range(len(ins)):
        cp(k, 0, ins[k], sib, me).wait_recv()
        for j, chip in enumerate(chips):
            cp(k, 4 + j, ins[k], (*chip, sib[2]), me).wait_recv()
        cp(k, 0, ins[k], me, sib).wait_send()
        for j, chip in enumerate(chips):
            cp(k, 1 + j, ins[k], me, (*chip, me[2])).wait_send()
            cp(k, 4 + j, outs[k].at[_index((*chip, me[2]))], (*chip, me[2]), sib).wait_send()
        pltpu.make_async_copy(ins[k], outs[k].at[_index(me)], loc.at[k]).wait()


def _gathered_shapes(arrs):
    return [jax.ShapeDtypeStruct((NDEV,) + a.shape, a.dtype) for a in arrs]


def _gather(arrs, name):
    n = len(arrs)

    def body(*refs):
        ins, outs, sems = refs[:n], refs[n:2 * n], refs[2 * n:]
        _gather_start(ins, outs, *sems)
        _gather_forward(ins, outs, *sems)
        _gather_finish(ins, outs, *sems)

    return _pallas_call(body, name=name, in_specs=[ANY_SPEC] * n, out_specs=[ANY_SPEC] * n,
                          out_shape=_gathered_shapes(arrs), scratch_shapes=_comm_scratch(n))(*arrs)


def _scatter(arrs, name):
    n = len(arrs)

    def body(*refs):
        ins, outs, sems = refs[:n], refs[n:2 * n], refs[2 * n:]
        _scatter_start(ins, outs, *sems)
        _scatter_wait(ins, outs, *sems)

    return _pallas_call(body, name=name, in_specs=[ANY_SPEC] * n, out_specs=[ANY_SPEC] * n,
                          out_shape=[jax.ShapeDtypeStruct(a.shape, a.dtype) for a in arrs],
                          scratch_shapes=_comm_scratch(n))(*arrs)


_BNN = (((2,), (1,)), ((0,), (0,)))
_BNT = (((2,), (2,)), ((0,), (0,)))
_BTN = (((1,), (1,)), ((0,), (0,)))


def _split(a):
    hi = a.astype(BF16)
    return hi, (a - hi.astype(F32)).astype(BF16)


def _dot3(a, b, dims=_BNN):
    ah, al = _split(a)
    bh, bl = _split(b)
    if dims != _BNN:
        return _dot(ah, bh, dims) + _dot(al, bh, dims) + _dot(ah, bl, dims)
    m = a.shape[1]
    r = _dot(jnp.concatenate([ah, al], axis=1), bh, _BNN)
    return r[:, :m] + r[:, m:] + _dot(ah, bl, _BNN)


def _tri_sum(mask, x):
    x1 = x.astype(BF16)
    r1 = x - x1.astype(F32)
    x2 = r1.astype(BF16)
    x3 = (r1 - x2.astype(F32)).astype(BF16)
    mb = mask.astype(BF16)
    return _dot(mb, x1) + _dot(mb, x2) + _dot(mb, x3)


GC = 3


def _rows_of(c):
    return slice(c * CH, (c + 1) * CH)


def _heads(ref, off):
    return jnp.stack([ref[_rows_of(c), off + h * DH:off + (h + 1) * DH] for c in range(GC) for h in range(H)])


def _cols(arrs):
    return jnp.stack([a[:, h:h + 1] for a in arrs for h in range(H)])


def _lanes(a):
    lane = lax.broadcasted_iota(jnp.int32, (CH, LANE), 1)
    out = jnp.zeros((CH, LANE), F32)
    for h in range(H):
        out = jnp.where(lane == h, a[h], out)
    return out


def _chunk_prep(qkv_ref, b_ref, g_ref):
    row = lax.broadcasted_iota(jnp.int32, (CH, CH), 0)
    col = lax.broadcasted_iota(jnp.int32, (CH, CH), 1)
    incl, strict = row >= col, row > col
    gcs = [_tri_sum(incl, g_ref[_rows_of(c), :]) for c in range(GC)]
    q, k, v = _heads(qkv_ref, 0), _heads(qkv_ref, D), _heads(qkv_ref, 2 * D)
    bcol, gcol = _cols([b_ref[_rows_of(c), :] for c in range(GC)]), _cols(gcs)
    grow = jnp.stack([gct[h:h + 1, :] for gct in [gc.T for gc in gcs] for h in range(H)])
    glast = _cols([gc[CH - 1:CH, :] for gc in gcs])
    dec = jnp.exp(jnp.where(incl[None], gcol - grow, -1e30))
    kb = k * bcol
    ab = _bdot(jnp.concatenate([kb, q], axis=1), k, _BNT)
    egc, ekc = jnp.exp(gcol), jnp.exp(glast - gcol)
    return dict(row=row, col=col, strict=strict[None], q=q, k=k, v=v, bcol=bcol, dec=dec, kb=kb,
                lm=jnp.where(strict[None], ab[:, :CH] * dec, 0.0), qk=ab[:, CH:] * dec, egc=egc, ekc=ekc,
                gth=jnp.exp(glast), qd=q * egc, kd=k * ekc, vb=v * bcol, kbg=kb * egc)


def _unit_lower_inverse(lm, eye):
    n = -lm
    x = eye + n
    pw = _dot3(n, n)
    for it in range(5):
        if it < 4:
            xp = _dot3(jnp.concatenate([x, pw], axis=1), pw)
            x = x + xp[:, :CH]
            pw = xp[:, CH:]
        else:
            x = x + _dot3(x, pw)
    return x


def _gdn_fwd(qkv, beta, g, name, gather=()):
    t = qkv.shape[0]
    nc = t // CH
    ns = nc // GC
    ng = len(gather)

    def body(qkv_ref, b_ref, g_ref, *rest):
        c_ins, (o_ref, sin_ref, vn_ref, ti_ref, w_ref) = rest[:ng], rest[ng:ng + 5]
        c_outs, state, sems = rest[ng + 5:2 * ng + 5], rest[2 * ng + 5], rest[2 * ng + 6:]
        step = pl.program_id(0)

        @pl.when(step == 0)
        def _():
            state[...] = jnp.zeros_like(state)
            if ng:
                _gather_start(c_ins, c_outs, *sems)

        if ng:
            @pl.when(step == max(ns - 4, 0))
            def _():
                _gather_forward(c_ins, c_outs, *sems)

            @pl.when(step == ns - 1)
            def _():
                _gather_finish(c_ins, c_outs, *sems)

        pr = _chunk_prep(qkv_ref, b_ref, g_ref)
        tinv = _unit_lower_inverse(pr["lm"], (pr["row"] == pr["col"]).astype(F32)[None])
        uw = _bdot(tinv, jnp.concatenate([pr["vb"], pr["kbg"]], axis=2), _BNN)
        u, w = uw[:, :, :DH], uw[:, :, DH:]
        s = state[...]
        for c in range(GC):
            hs = slice(c * H, (c + 1) * H)
            ws = _bdot(jnp.concatenate([w[hs], pr["qd"][hs]], axis=1), s, _BNN)
            vn = u[hs] - ws[:, :CH]
            o = ws[:, CH:] + _bdot(pr["qk"][hs], vn, _BNN)
            sin_ref[c] = s
            ti_ref[c] = tinv[hs]
            s = s * pr["gth"][hs] + _bdot(pr["kd"][hs], vn, _BTN)
            for h in range(H):
                sl = slice(h * DH, (h + 1) * DH)
                o_ref[_rows_of(c), sl] = o[h]
                vn_ref[_rows_of(c), sl] = vn[h]
                w_ref[_rows_of(c), sl] = w[c * H + h]
        state[...] = s

    chunk = lambda cols: pl.BlockSpec((GC * CH, cols), lambda c: (c, 0))
    outs = _pallas_call(
        body, name=name, grid=(ns,), in_specs=[chunk(QKV), chunk(LANE), chunk(LANE)] + [ANY_SPEC] * ng,
        out_specs=[chunk(D), pl.BlockSpec((GC, H, DH, DH), lambda c: (c, 0, 0, 0)), chunk(D),
                   pl.BlockSpec((GC, H, CH, CH), lambda c: (c, 0, 0, 0)), chunk(D)] + [ANY_SPEC] * ng,
        out_shape=[jax.ShapeDtypeStruct((t, D), F32), jax.ShapeDtypeStruct((nc, H, DH, DH), F32),
                   jax.ShapeDtypeStruct((t, D), F32), jax.ShapeDtypeStruct((nc, H, CH, CH), F32),
                   jax.ShapeDtypeStruct((t, D), F32)] + _gathered_shapes(gather),
        scratch_shapes=[pltpu.VMEM((H, DH, DH), F32)] + (_comm_scratch(ng) if ng else []),
        compiler_params=_params(("arbitrary",), 12 << 20),
    )(qkv, beta, g, *gather)
    return outs[:5], outs[5:]


def _gdn_bwd(qkv, beta, g, do, s_in, vnew, tinv, wsv, name, scatter=()):
    t = qkv.shape[0]
    nsteps = t // CH // GC
    ns = len(scatter)

    def body(qkv_ref, b_ref, g_ref, do_ref, sin_ref, vn_ref, ti_ref, w_ref, *rest):
        c_ins, (dqkv_ref, db_ref, dg_ref) = rest[:ns], rest[ns:ns + 3]
        c_outs, dstate, sems = rest[ns + 3:2 * ns + 3], rest[2 * ns + 3], rest[2 * ns + 4:]
        step = pl.program_id(0)

        @pl.when(step == 0)
        def _():
            dstate[...] = jnp.zeros_like(dstate)
            if ns:
                _scatter_start(c_ins, c_outs, *sems)

        if ns:
            @pl.when(step == nsteps - 1)
            def _():
                _scatter_wait(c_ins, c_outs, *sems)

        pr = _chunk_prep(qkv_ref, b_ref, g_ref)
        ti = jnp.concatenate([ti_ref[c] for c in range(GC)], axis=0)
        s = jnp.concatenate([sin_ref[c] for c in range(GC)], axis=0)
        w, vn, doh = _heads(w_ref, 0), _heads(vn_ref, 0), _heads(do_ref, 0)
        dqd = _bdot(doh, s, _BNT)
        dqk = _bdot(doh, vn, _BNT)
        qk_do = _bdot(pr["qk"], doh, _BTN)
        qd_do = _bdot(pr["qd"], doh, _BTN)
        ds = dstate[...]
        dvn_c, dkd_c, dw_c, dgt_c = [None] * GC, [None] * GC, [None] * GC, [None] * GC
        for c in reversed(range(GC)):
            hs = slice(c * H, (c + 1) * H)
            dvn_c[c] = _bdot(pr["kd"][hs], ds, _BNN) + qk_do[hs]
            dkd_c[c] = _bdot(vn[hs], ds, _BNT)
            dw_c[c] = -_bdot(dvn_c[c], s[hs], _BNT)
            dgt_c[c] = jnp.sum(jnp.sum(ds * s[hs], axis=2, keepdims=True), axis=1, keepdims=True)
            ds = ds * pr["gth"][hs] + qd_do[hs] - _bdot(w[hs], dvn_c[c], _BTN)
        dstate[...] = ds
        dvn, dkd, dw, dgt = (jnp.concatenate(parts, axis=0) for parts in (dvn_c, dkd_c, dw_c, dgt_c))
        duw = jnp.concatenate([dvn, dw], axis=2)
        dvk = _bdot(ti, duw, _BTN)
        dvb, dkbg = dvk[:, :, :DH], dvk[:, :, DH:]
        dti = _bdot(duw, jnp.concatenate([pr["vb"], pr["kbg"]], axis=2), _BNT)
        dl = -_dot3(_dot3(ti, dti, _BTN), ti, _BNT)
        dl = jnp.where(pr["strict"], dl, 0.0)
        dab = jnp.concatenate([dl * pr["dec"], dqk * pr["dec"]], axis=1)
        r1 = _bdot(dab, pr["k"], _BNN)
        dkb = r1[:, :CH] + dkbg * pr["egc"]
        dq = r1[:, CH:] + dqd * pr["egc"]
        dk = _bdot(dab, jnp.concatenate([pr["kb"], pr["q"]], axis=1), _BTN) + dkb * pr["bcol"] + dkd * pr["ekc"]
        m = dl * pr["lm"] + dqk * pr["qk"]
        mh, ml = _split(m)
        ones = jnp.ones((GC * H, CH, LANE), BF16)
        colsum = (_dot(mh, ones, _BTN) + _dot(ml, ones, _BTN))[:, :, 0:1]
        kdsum = jnp.sum(dkd * pr["kd"], axis=2, keepdims=True)
        dgc = (jnp.sum(m, axis=2, keepdims=True) - colsum + jnp.sum(dkbg * pr["kbg"], axis=2, keepdims=True)
               + jnp.sum(dqd * pr["qd"], axis=2, keepdims=True) - kdsum)
        dglast = jnp.sum(kdsum, axis=1, keepdims=True) + dgt * pr["gth"]
        last_row = lax.broadcasted_iota(jnp.int32, (1, CH, 1), 1) == CH - 1
        dgc = dgc + jnp.where(last_row, dglast, 0.0)
        dbeta = jnp.sum(dkb * pr["k"], axis=2, keepdims=True) + jnp.sum(dvb * pr["v"], axis=2, keepdims=True)
        dv = dvb * pr["bcol"]
        upper = pr["row"] <= pr["col"]
        for c in range(GC):
            hs = slice(c * H, (c + 1) * H)
            for h in range(H):
                dqkv_ref[_rows_of(c), h * DH:(h + 1) * DH] = dq[c * H + h]
                dqkv_ref[_rows_of(c), D + h * DH:D + (h + 1) * DH] = dk[c * H + h]
                dqkv_ref[_rows_of(c), 2 * D + h * DH:2 * D + (h + 1) * DH] = dv[c * H + h]
            db_ref[_rows_of(c), :] = _lanes(dbeta[hs])
            dg_ref[_rows_of(c), :] = _tri_sum(upper, _lanes(dgc[hs]))

    chunk = lambda cols: pl.BlockSpec((GC * CH, cols), lambda c: (nsteps - 1 - c, 0))
    sq = lambda a, b: pl.BlockSpec((GC, H, a, b), lambda c: (nsteps - 1 - c, 0, 0, 0))
    outs = _pallas_call(
        body, name=name, grid=(nsteps,),
        in_specs=[chunk(QKV), chunk(LANE), chunk(LANE), chunk(D), sq(DH, DH), chunk(D), sq(CH, CH), chunk(D)] + [ANY_SPEC] * ns,
        out_specs=[chunk(QKV), chunk(LANE), chunk(LANE)] + [ANY_SPEC] * ns,
        out_shape=[jax.ShapeDtypeStruct((t, QKV), F32), jax.ShapeDtypeStruct((t, LANE), F32),
                   jax.ShapeDtypeStruct((t, LANE), F32)] + [jax.ShapeDtypeStruct(a.shape, a.dtype) for a in scatter],
        scratch_shapes=[pltpu.VMEM((H, DH, DH), F32)] + (_comm_scratch(ns) if ns else []),
        compiler_params=_params(("arbitrary",), 16 << 20),
    )(qkv, beta, g, do, s_in, vnew, tinv, wsv, *scatter)
    return outs[:3], outs[3:]


def _pool_counts(row_ids, win):
    return jnp.minimum(jnp.maximum(row_ids - LEAD, 0) + 1, win).astype(F32)


def _pool_fwd(p, name):
    t = p.shape[0]
    ext = TE + 16

    def body(p_ref, o_ref, carry):
        i = pl.program_id(0)

        @pl.when(i == 0)
        def _():
            carry[...] = jnp.zeros_like(carry)

        ids = _row_ids(i)
        for gi, win in enumerate(POOL_WINDOWS):
            sl = slice(gi * LANE, (gi + 1) * LANE)
            xv = p_ref[:, sl]
            s = jnp.concatenate([carry[:, sl], xv], axis=0)
            sh = 1
            while sh < win:
                s = s + pltpu.roll(s, sh, 0)
                sh *= 2
            o_ref[:, sl] = (s[16:ext] / _pool_counts(ids, win) - xv).astype(BF16)
            carry[:, sl] = xv[TE - 16:TE]

    return _pallas_call(
        body, name=name, grid=(t // TE,), in_specs=[_rows(POOL_W)], out_specs=_rows(POOL_W),
        out_shape=jax.ShapeDtypeStruct((t, POOL_W), BF16), scratch_shapes=[pltpu.VMEM((16, POOL_W), F32)],
        compiler_params=_params(("arbitrary",), TE * POOL_W * 8),
    )(p)


def _pool_bwd(dpo, name):
    t = dpo.shape[0]
    n = t // TE
    ext = TE + 16

    def body(d_ref, o_ref, carry):
        i = pl.program_id(0)

        @pl.when(i == 0)
        def _():
            carry[...] = jnp.zeros_like(carry)

        ids = _row_ids(n - 1 - i)
        for gi, win in enumerate(POOL_WINDOWS):
            sl = slice(gi * LANE, (gi + 1) * LANE)
            dv = d_ref[:, sl]
            rv = dv / _pool_counts(ids, win)
            s = jnp.concatenate([rv, carry[:, sl]], axis=0)
            sh = 1
            while sh < win:
                s = s + pltpu.roll(s, ext - sh, 0)
                sh *= 2
            o_ref[:, sl] = (s[0:TE] - dv).astype(BF16)
            carry[:, sl] = rv[0:16]

    return _pallas_call(
        body, name=name, grid=(n,), in_specs=[_rows(POOL_W, n)], out_specs=_rows(POOL_W, n),
        out_shape=jax.ShapeDtypeStruct((t, POOL_W), BF16), scratch_shapes=[pltpu.VMEM((16, POOL_W), F32)],
        compiler_params=_params(("arbitrary",), TE * POOL_W * 8),
    )(dpo)


def _post_fwd(o, z, gate, pm, hn, ps, name):
    t = o.shape[0]

    def body(o_ref, z_ref, g_ref, pm_ref, hn_ref, ps_ref, y_ref):
        for h in range(H):
            sl = slice(h * DH, (h + 1) * DH)
            ov = o_ref[:, sl]
            zv = z_ref[:, sl].astype(F32)
            r = lax.rsqrt(jnp.mean(ov * ov, axis=-1, keepdims=True) + EPS)
            ya = ov * r * hn_ref[...] * (zv * _sigmoid(zv))
            ga = _sigmoid(g_ref[:, sl].astype(F32))
            gb = _sigmoid(g_ref[:, D + h * DH:D + (h + 1) * DH].astype(F32))
            y_ref[:, sl] = (ga * ya + gb * (pm_ref[:, sl] * ps_ref[:, sl])).astype(BF16)

    return _pallas_call(
        body, name=name, grid=(t // TE,),
        in_specs=[_rows(D), _rows(D), _rows(2 * D), _rows(D), _whole((1, DH)), _whole((1, D))], out_specs=_rows(D),
        out_shape=jax.ShapeDtypeStruct((t, D), BF16), compiler_params=_params(("parallel",), TE * D * 16),
    )(o, z, gate, pm, hn, ps)


def _post_bwd(dy, o, z, gate, pm, hn, ps, name):
    t = o.shape[0]

    def body(dy_ref, o_ref, z_ref, g_ref, pm_ref, hn_ref, ps_ref, do_ref, dz_ref, dgate_ref, dpm_ref, dhn_ref, dps_ref):
        i = pl.program_id(0)

        @pl.when(i == 0)
        def _():
            dhn_ref[...] = jnp.zeros_like(dhn_ref)
            dps_ref[...] = jnp.zeros_like(dps_ref)

        hnv = hn_ref[...]
        dhn = jnp.zeros((1, DH), F32)
        for h in range(H):
            sl = slice(h * DH, (h + 1) * DH)
            slb = slice(D + h * DH, D + (h + 1) * DH)
            dyv = dy_ref[:, sl]
            ov = o_ref[:, sl]
            zv = z_ref[:, sl].astype(F32)
            r = lax.rsqrt(jnp.mean(ov * ov, axis=-1, keepdims=True) + EPS)
            sz = _sigmoid(zv)
            silu = zv * sz
            on = ov * r
            ya = on * hnv * silu
            ga = _sigmoid(g_ref[:, sl].astype(F32))
            gb = _sigmoid(g_ref[:, slb].astype(F32))
            pmv = pm_ref[:, sl]
            psv = ps_ref[:, sl]
            dya = dyv * ga
            dyb = dyv * gb
            dgate_ref[:, sl] = (dyv * ya * ga * (1.0 - ga)).astype(BF16)
            dgate_ref[:, slb] = (dyv * (pmv * psv) * gb * (1.0 - gb)).astype(BF16)
            tt = dya * hnv * silu
            do_ref[:, sl] = r * tt - ov * (r * r * r) * jnp.mean(ov * tt, axis=-1, keepdims=True)
            dz_ref[:, sl] = (dya * on * hnv * (sz * (1.0 + zv * (1.0 - sz)))).astype(BF16)
            dhn = dhn + jnp.sum(dya * on * silu, axis=0, keepdims=True)
            dps_ref[:, sl] += jnp.sum(dyb * pmv, axis=0, keepdims=True)
            dpm_ref[:, sl] = (dyb * psv).astype(BF16)
        dhn_ref[...] += dhn

    return _pallas_call(
        body, name=name, grid=(t // TE,),
        in_specs=[_rows(D), _rows(D), _rows(D), _rows(2 * D), _rows(D), _whole((1, DH)), _whole((1, D))],
        out_specs=[_rows(D), _rows(D), _rows(2 * D), _rows(D), _whole((1, DH)), _whole((1, D))],
        out_shape=[jax.ShapeDtypeStruct((t, D), F32), jax.ShapeDtypeStruct((t, D), BF16),
                   jax.ShapeDtypeStruct((t, 2 * D), BF16), jax.ShapeDtypeStruct((t, D), BF16),
                   jax.ShapeDtypeStruct((1, DH), F32), jax.ShapeDtypeStruct((1, D), F32)],
        compiler_params=_params(("arbitrary",), TE * D * 28),
    )(dy, o, z, gate, pm, hn, ps)


_FB_COLS = [(c, min(c + LANE, FB)) for c in range(0, FB, LANE)]


def _mlp_act_fwd(hid, cw, name):
    t = hid.shape[1]
    n = t // TE

    def body(hg_ref, hv_ref, wg_ref, wv_ref, a_ref, xg, xv):
        i = pl.program_id(1)
        _stage_history(xg, i)
        _stage_history(xv, i)
        for c, (c0, c1) in enumerate(_FB_COLS):
            sl, wd = slice(c0, c1), c1 - c0
            xg[c, SUB:SUB + TE, 0:wd] = hg_ref[:, sl].astype(F32)
            xv[c, SUB:SUB + TE, 0:wd] = hv_ref[:, sl].astype(F32)
            gg = _conv(_taps(xg, c, wd, SUB, TE, 3), wg_ref[:, sl])
            vv = _conv(_taps(xv, c, wd, SUB, TE, 3), wv_ref[:, sl])
            a_ref[:, sl] = (gg * _sigmoid(gg) * vv).astype(BF16)

    hspec = lambda off: pl.BlockSpec((None, TE, FB), lambda p, i: (p + off, i, 0))
    wspec = lambda off: pl.BlockSpec((None, 3, FB), lambda p, i: (p + off, 0, 0))
    return _pallas_call(
        body, name=name, grid=(4, n), in_specs=[hspec(0), hspec(4), wspec(0), wspec(4)],
        out_specs=pl.BlockSpec((None, TE, FB), lambda p, i: (p, i, 0)),
        out_shape=jax.ShapeDtypeStruct((4, t, FB), BF16),
        scratch_shapes=[_seq_scratch(FB), _seq_scratch(FB)],
        compiler_params=_params(("parallel", "arbitrary"), TE * FB * 12),
    )(hid, hid, cw, cw)


def _mlp_act_bwd(da, hid, cw, name):
    t = hid.shape[1]
    n = t // TE
    hb = TE // 16

    def body(da_ref, hg_ref, hv_ref, pg_ref, pv_ref, wg_ref, wv_ref, dhg_ref, dhv_ref, dwg_ref, dwv_ref, xg, xv, dg, dv):
        i = pl.program_id(1)

        @pl.when(i == 0)
        def _():
            dwg_ref[...] = jnp.zeros_like(dwg_ref)
            dwv_ref[...] = jnp.zeros_like(dwv_ref)

        _stage_future(dg, i)
        _stage_future(dv, i)
        first_tile = i == n - 1
        for c, (c0, c1) in enumerate(_FB_COLS):
            sl, wd = slice(c0, c1), c1 - c0
            for scr, p_ref, h_ref in ((xg, pg_ref, hg_ref), (xv, pv_ref, hv_ref)):
                scr[c, 0:SUB, 0:wd] = jnp.where(first_tile, 0.0, p_ref[SUB:2 * SUB, sl].astype(F32))
                scr[c, SUB:SUB + TE, 0:wd] = h_ref[:, sl].astype(F32)
            wg = wg_ref[:, sl]
            wv = wv_ref[:, sl]
            tg = _taps(xg, c, wd, SUB, TE, 3)
            tv = _taps(xv, c, wd, SUB, TE, 3)
            gg = _conv(tg, wg)
            vv = _conv(tv, wv)
            sg = _sigmoid(gg)
            dav = da_ref[:, sl].astype(F32)
            dgg = dav * vv * (sg * (1.0 + gg * (1.0 - sg)))
            dvv = dav * (gg * sg)
            for dc, tp, w, scr, dh_ref, dw_ref in ((dgg, tg, wg, dg, dhg_ref, dwg_ref), (dvv, tv, wv, dv, dhv_ref, dwv_ref)):
                scr[c, 0:TE, 0:wd] = dc
                dh_ref[:, sl] = _conv_t(scr, c, wd, w).astype(BF16)
                dw_ref[:, sl] += jnp.concatenate([jnp.sum(tp[j] * dc, axis=0, keepdims=True) for j in range(3)], axis=0)

    rev = lambda off: pl.BlockSpec((None, TE, FB), lambda p, i: (p + off, n - 1 - i, 0))
    halo = lambda off: pl.BlockSpec((None, 16, FB), lambda p, i: (p + off, jnp.maximum((n - 1 - i) * hb - 1, 0), 0))
    wspec = lambda off: pl.BlockSpec((None, 3, FB), lambda p, i: (p + off, 0, 0))
    dwspec = pl.BlockSpec((None, 3, FB), lambda p, i: (p, 0, 0))
    return _pallas_call(
        body, name=name, grid=(4, n), in_specs=[rev(0), rev(0), rev(4), halo(0), halo(4), wspec(0), wspec(4)],
        out_specs=[rev(0), rev(0), dwspec, dwspec],
        out_shape=[jax.ShapeDtypeStruct((4, t, FB), BF16), jax.ShapeDtypeStruct((4, t, FB), BF16),
                   jax.ShapeDtypeStruct((4, 3, FB), F32), jax.ShapeDtypeStruct((4, 3, FB), F32)],
        scratch_shapes=[_seq_scratch(FB)] * 4,
        compiler_params=_params(("parallel", "arbitrary"), TE * FB * 24),
    )(da, hid, hid, hid, hid, cw, cw)


def _adamw(lands, w, m, v, name):
    nl, r, c = w.shape
    tr = r
    if r * c * 4 > (2 << 20):
        for cand in (128, 64, 32, 16):
            if r % cand == 0:
                tr = cand
                break
    nr = r // tr
    c1 = 1.0 - ADAM_B1 ** ADAM_STEP
    c2 = 1.0 - ADAM_B2 ** ADAM_STEP

    def body(*refs):
        l_refs, (w_ref, m_ref, v_ref, g_out, d_out, m_out, v_out) = refs[:nl], refs[nl:]
        layer = pl.program_id(0)
        g = None
        for l, l_ref in enumerate(l_refs):
            gl = l_ref[0].astype(F32)
            for i in range(1, NDEV):
                gl = gl + l_ref[i].astype(F32)
            g = gl if g is None else jnp.where(layer == l, gl, g)
        mn = ADAM_B1 * m_ref[...] + (1.0 - ADAM_B1) * g
        vn = ADAM_B2 * v_ref[...] + (1.0 - ADAM_B2) * (g * g)
        g_out[...] = g
        m_out[...] = mn
        v_out[...] = vn
        d_out[...] = -ADAM_LR * ((mn / c1) / (jnp.sqrt(vn / c2) + ADAM_EPS) + ADAM_WD * w_ref[...])

    def land_spec(l):
        return pl.BlockSpec((NDEV, tr, c), lambda ly, i: (0, jnp.where(ly == l, i, 0 if l > 0 else nr - 1), 0))

    spec = pl.BlockSpec((None, tr, c), lambda ly, i: (ly, i, 0))
    shp = jax.ShapeDtypeStruct((nl, r, c), F32)
    return _pallas_call(
        body, name=name, grid=(nl, nr), in_specs=[land_spec(l) for l in range(nl)] + [spec, spec, spec],
        out_specs=[spec] * 4, out_shape=[shp] * 4,
        compiler_params=_params(("arbitrary", "arbitrary"), (11 + 4 * nl) * tr * c * 4),
    )(*lands, w, m, v)


def _layer_fwd(h, p, tag, gather=(), finish=None):
    u = _rmsnorm_fwd(h, p["norm_mix"], f"norm_mix_{tag}")
    qkv_pre = _mm(u, p["w_qkv"], BF16, f"proj_qkv_{tag}")
    z = _mm(u, p["w_z"], BF16, f"proj_z_{tag}")
    ba = _mm(u, p["w_ba"], F32, f"proj_ba_{tag}")
    pool_in = _mm(u, p["w_pl"], F32, f"proj_pool_{tag}")
    gate = _mm(u, p["w_gate"], BF16, f"proj_gate_{tag}")
    qkv = _gdn_pre_fwd(qkv_pre, p["conv_qkv"], f"gdn_pre_{tag}")
    beta, g = _gates_fwd(ba, p["a_row"], p["dt_row"], f"gates_{tag}")
    (o, s_in, vnew, tinv, wsv), gathered = _gdn_fwd(qkv, beta, g, f"gdn_{tag}", gather)
    if finish is not None:
        p = {**p, **finish(gathered)}
    pooled = _pool_fwd(pool_in, f"pool_{tag}")
    pm = _mm_cols(pooled, p["w_pool"], F32, f"pool_mm_{tag}", _NN)
    y = _post_fwd(o, z, gate, pm, p["head_norm"], p["pool_scale"], f"post_{tag}")
    h1 = _mm(y, p["w_out"], F32, f"out_proj_{tag}", res=h)
    u2 = _rmsnorm_fwd(h1, p["norm_ffn"], f"norm_ffn_{tag}")
    hid = _mm_up(u2, p["w_up"], f"up_proj_{tag}")
    act = _mlp_act_fwd(hid, p["conv_ffn"], f"mlp_act_{tag}")
    h2 = _mm_blocks_red(act, p["w_down"], f"down_proj_{tag}", _NN, res=h1)
    saved = dict(h=h, u=u, qkv_pre=qkv_pre, z=z, ba=ba, gate=gate, qkv=qkv, beta=beta, g=g, o=o, s_in=s_in, vnew=vnew,
                 tinv=tinv, wsv=wsv, pooled=pooled, pm=pm, y=y, h1=h1, u2=u2, hid=hid, act=act)
    return h2, saved, gathered, p


def _layer_bwd(dh, dh_b, p, s, tag, scatter=()):
    gr = {}
    da = _mm_to_blocks(dh_b, p["w_down"], f"d_act_{tag}")
    gr["w_down"] = _mm_tn_blocks(s["act"], dh_b, f"dw_down_{tag}", True, False)
    dhg, dhv, dwg, dwv = _mlp_act_bwd(da, s["hid"], p["conv_ffn"], f"mlp_act_bwd_{tag}")
    gr["conv_ffn"] = jnp.concatenate([dwg, dwv], axis=0)
    w_up = p["w_up"]
    du2 = _mm_blocks_red(dhg, w_up[:4], f"d_u2g_{tag}", _NT)
    du2 = _mm_blocks_red(dhv, w_up[4:], f"d_u2v_{tag}", _NT, res=du2)
    gr["w_up"] = jnp.concatenate([_mm_tn_blocks(s["u2"], dhg, f"dw_upg_{tag}", False, True),
                                  _mm_tn_blocks(s["u2"], dhv, f"dw_upv_{tag}", False, True)], axis=0)
    dh1, dh1_b, gr["norm_ffn"] = _rmsnorm_bwd(s["h1"], du2, dh, p["norm_ffn"], f"norm_ffn_bwd_{tag}")
    dy = _mm(dh1_b, p["w_out"], F32, f"d_y_{tag}", dims=_NT)
    gr["w_out"] = _mm_tn(s["y"], dh1_b, f"dw_out_{tag}")
    do, dz, dgate, dpm, gr["head_norm"], gr["pool_scale"] = _post_bwd(
        dy, s["o"], s["z"], s["gate"], s["pm"], p["head_norm"], p["pool_scale"], f"post_bwd_{tag}")
    dpooled = _mm_cols(dpm, p["w_pool"], F32, f"d_pooled_{tag}", _NT)
    gr["w_pool"] = _mm_tn_cols(s["pooled"], dpm, 4, f"dw_pool_{tag}")
    dpool_in = _pool_bwd(dpooled, f"pool_bwd_{tag}")
    own = (gr["w_up"].astype(BF16), gr["w_down"].reshape(NDEV, -1, D).astype(BF16),
           gr["w_out"].reshape(NDEV, D // NDEV, D).astype(BF16))
    (dqkv, dbeta, dg), landed = _gdn_bwd(s["qkv"], s["beta"], s["g"], do, s["s_in"], s["vnew"], s["tinv"], s["wsv"],
                                         f"gdn_bwd_{tag}", own + tuple(scatter))
    dba, gr["a_log"], gr["dt_bias"] = _gates_bwd(s["ba"], p["a_row"], p["dt_row"], dbeta, dg, f"gates_bwd_{tag}")
    dqkv_pre, gr["conv_qkv"] = _gdn_pre_bwd(s["qkv_pre"], p["conv_qkv"], dqkv, f"gdn_pre_bwd_{tag}")
    du = None
    dws = []
    for nm, dseg, wseg in (("qkv", dqkv_pre, p["w_qkv"]), ("z", dz, p["w_z"]), ("ba", dba, p["w_ba"]),
                           ("pool", dpool_in, p["w_pl"]), ("gate", dgate, p["w_gate"])):
        du = _mm(dseg, wseg, F32, f"d_u_{nm}_{tag}", res=du, dims=_NT)
        dws.append(_mm_tn(s["u"], dseg, f"dw_{nm}_{tag}"))
    gr["w_in"] = jnp.concatenate([dws[0], dws[1], dws[2][:, 0:H], dws[2][:, LANE:LANE + H], dws[3], dws[4]], axis=1)
    dh0, dh0_b, gr["norm_mix"] = _rmsnorm_bwd(s["h"], du, dh1, p["norm_mix"], f"norm_mix_bwd_{tag}")
    return dh0, dh0_b, gr, landed


def _pad_lanes(v8):
    return jnp.pad(v8.reshape(1, H), ((0, 0), (0, LANE - H)))


def _pack(parts, rows, lead=1):
    flat = jnp.concatenate([q.reshape(lead, -1) for q in parts], axis=1)
    flat = jnp.pad(flat, ((0, 0), (0, rows * LANE - flat.shape[1])))
    return flat.reshape((lead, rows, LANE) if lead > 1 else (rows, LANE))


def _unpack(packed, shapes, lead=1):
    flat = packed.reshape(lead, -1)
    out, off = [], 0
    for shp in shapes:
        n = 1
        for s_ in shp:
            n *= s_
        n //= lead
        out.append(flat[:, off:off + n].reshape(shp))
        off += n
    return out


SMALL_ROWS = 336
REPL_ROWS = 64


def kernel(x, meta_tokens, norm_mix, w_in, conv_qkv, a_log, dt_bias, head_norm, w_pool, pool_scale, w_out, norm_ffn, w_up, conv_ffn, w_down, norm_final, loss_target, m_meta_tokens, m_norm_mix, m_w_in, m_conv_qkv, m_a_log, m_dt_bias, m_head_norm, m_w_pool, m_pool_scale, m_w_out, m_norm_ffn, m_w_up, m_conv_ffn, m_w_down, m_norm_final, v_meta_tokens, v_norm_mix, v_w_in, v_conv_qkv, v_a_log, v_dt_bias, v_head_norm, v_w_pool, v_pool_scale, v_w_out, v_norm_ffn, v_w_up, v_conv_ffn, v_w_down, v_norm_final):
    seq = x.shape[1]
    t = ROW0 + seq
    assert t % TE == 0 and t % (MM_TILES * 16) == 0 and t % (GC * CH) == 0
    depth = w_in.shape[0]
    assert depth == 2
    cin = w_in.shape[2]

    def mixer_params(l, g_in, conv_q, conv_f, wp):
        wf = jnp.transpose(g_in, (1, 0, 2)).reshape(D, NDEV * cin)
        zpad = jnp.zeros((D, LANE - H), BF16)
        return dict(
            w_qkv=wf[:, 0:QKV], w_z=wf[:, QKV:QKV + D],
            w_ba=jnp.concatenate([wf[:, 4096:4104], zpad, wf[:, 4104:4112], zpad], axis=1),
            w_pl=wf[:, 4112:4624], w_gate=wf[:, 4624:6672], conv_qkv=conv_q, conv_ffn=conv_f, w_pool=wp,
            norm_mix=norm_mix[l].reshape(1, D), norm_ffn=norm_ffn[l].reshape(1, D),
            pool_scale=pool_scale[l].reshape(1, D), head_norm=head_norm[l].reshape(1, DH),
            a_row=_pad_lanes(a_log[l]), dt_row=_pad_lanes(dt_bias[l]))

    def late_params(g_up, g_out, g_down):
        return dict(w_out=g_out.reshape(D, D), w_up=g_up, w_down=g_down.reshape(4, FB, D))

    small_shapes = [conv_qkv.shape, conv_ffn.shape, w_pool.shape, meta_tokens.shape]
    small = _pack([conv_qkv, conv_ffn, w_pool, meta_tokens], SMALL_ROWS)
    w_in_b, w_up_b, w_out_b, w_down_b = w_in.astype(BF16), w_up.astype(BF16), w_out.astype(BF16), w_down.astype(BF16)
    g_in0, g_small = _gather([w_in_b[0], small], "gather_first")
    s_cq, s_cf, s_wp, s_mt = _unpack(g_small, [(NDEV,) + shp for shp in small_shapes], lead=NDEV)
    conv_qkv_full = jnp.transpose(s_cq, (1, 2, 0, 3)).reshape(depth, 4, QKV)
    conv_ffn_blk = jnp.transpose(s_cf, (1, 0, 2, 3))
    w_pool_full = jnp.transpose(s_wp, (1, 2, 3, 0, 4)).reshape(depth, 4, DH, 2 * DH).astype(BF16)
    meta_full = jnp.transpose(s_mt, (1, 0, 2)).reshape(N_META, D)

    h = jnp.concatenate([jnp.zeros((LEAD, D), F32), meta_full, x[0]], axis=0)
    p0 = mixer_params(0, g_in0, conv_qkv_full[0], conv_ffn_blk[0], w_pool_full[0])
    h, sv0, rest, p0 = _layer_fwd(
        h, p0, "l0", (w_up_b[0], w_out_b[0], w_down_b[0], w_in_b[1], w_up_b[1], w_out_b[1], w_down_b[1]),
        lambda got: late_params(*got[:3]))
    p1 = {**mixer_params(1, rest[3], conv_qkv_full[1], conv_ffn_blk[1], w_pool_full[1]), **late_params(*rest[4:])}
    h, sv1, _, _ = _layer_fwd(h, p1, "l1")
    layers = [p0, p1]
    saved = [sv0, sv1]
    target = jnp.concatenate([jnp.zeros((ROW0, D), F32), loss_target[0]], axis=0)
    dh, dh_b, d_norm_final, loss_row = _loss_bwd(h, target, norm_final.reshape(1, D), "loss")

    def w_in_blocks(gr):
        return jnp.transpose(gr["w_in"].reshape(D, NDEV, cin), (1, 0, 2)).astype(BF16)

    grads = [None] * depth
    dh, dh_b, grads[1], (l_up1, l_down1, l_out1) = _layer_bwd(dh, dh_b, layers[1], saved[1], "l1")
    dh, dh_b, grads[0], (l_up0, l_down0, l_out0, l_in1) = _layer_bwd(dh, dh_b, layers[0], saved[0], "l0", (w_in_blocks(grads[1]),))
    grad_x = dh[ROW0:].reshape(1, seq, D)
    d_meta = dh[LEAD:ROW0]

    stk = lambda name: jnp.stack([grads[l][name] for l in range(depth)], axis=0)
    cq = conv_qkv.shape[2]
    pw = w_pool.shape[3]
    s_cq = jnp.transpose(stk("conv_qkv").reshape(depth, 4, NDEV, cq), (2, 0, 1, 3))
    s_cf = jnp.transpose(stk("conv_ffn"), (1, 0, 2, 3))
    s_wp = jnp.transpose(stk("w_pool").reshape(depth, 4, DH, NDEV, pw), (3, 0, 1, 2, 4))
    s_mt = jnp.transpose(d_meta.reshape(N_META, NDEV, D // NDEV), (1, 0, 2))
    b_small = _pack([s_cq, s_cf, s_wp, s_mt], SMALL_ROWS, lead=NDEV)
    l_in0, l_small = _scatter([w_in_blocks(grads[0]), b_small], "exchange_last")

    r_in = _adamw((l_in0, l_in1), w_in, m_w_in, v_w_in, "adamw_w_in")
    r_up = _adamw((l_up0, l_up1), w_up, m_w_up, v_w_up, "adamw_w_up")
    r_out = _adamw((l_out0, l_out1), w_out, m_w_out, v_w_out, "adamw_w_out")
    r_down = _adamw((l_down0, l_down1), w_down, m_w_down, v_w_down, "adamw_w_down")
    r_small = _adamw((l_small,), small[None], _pack([m_conv_qkv, m_conv_ffn, m_w_pool, m_meta_tokens], SMALL_ROWS)[None],
                     _pack([v_conv_qkv, v_conv_ffn, v_w_pool, v_meta_tokens], SMALL_ROWS)[None], "adamw_small")
    r_small = [_unpack(o_[0], small_shapes) for o_ in r_small]

    repl_shapes = [norm_mix.shape, a_log.shape, dt_bias.shape, head_norm.shape, pool_scale.shape, norm_ffn.shape,
                   norm_final.shape, (1,)]
    rp = lambda name, n: jnp.stack([grads[l][name][0, :n] for l in range(depth)], axis=0)
    part = _pack([rp("norm_mix", D), rp("a_log", H), rp("dt_bias", H), rp("head_norm", DH), rp("pool_scale", D),
                  rp("norm_ffn", D), d_norm_final[0], loss_row[0, 0:1]], REPL_ROWS)
    (l_repl,) = _gather([part], "gather_replicated")
    zero1 = jnp.zeros((1,), F32)
    r_repl = _adamw(
        (l_repl,), _pack([norm_mix, a_log, dt_bias, head_norm, pool_scale, norm_ffn, norm_final, zero1], REPL_ROWS)[None],
        _pack([m_norm_mix, m_a_log, m_dt_bias, m_head_norm, m_pool_scale, m_norm_ffn, m_norm_final, zero1], REPL_ROWS)[None],
        _pack([v_norm_mix, v_a_log, v_dt_bias, v_head_norm, v_pool_scale, v_norm_ffn, v_norm_final, zero1], REPL_ROWS)[None],
        "adamw_replicated")
    r_repl = [_unpack(o_[0], repl_shapes) for o_ in r_repl]
    loss = r_repl[0][7].reshape(())

    def leaf(kind):
        sm, rr = r_small[kind], r_repl[kind]
        return [sm[3], rr[0], r_in[kind], sm[0], rr[1], rr[2], rr[3], sm[2], rr[4], r_out[kind], rr[5], r_up[kind],
                sm[1], r_down[kind], rr[6]]

    return (loss, grad_x, *leaf(0), *leaf(1), *leaf(2), *leaf(3))
```

```python
import jax
import jax.numpy as jnp
from jax import lax
from jax.experimental import pallas as pl
from jax.experimental.pallas import tpu as pltpu

F32 = jnp.float32
BF16 = jnp.bfloat16
MESH = pl.DeviceIdType.MESH

D = 1024
H = 8
DH = 128
CH = 64
N_META = 16
LEAD = 48
ROW0 = LEAD + N_META
QKV = 3 * D
POOL_W = 512
POOL_WINDOWS = (2, 4, 8, 16)
FB = 704
NDEV = 8
EPS = 1e-6
MM_TILES = 12
TE = 192
LANE = 128
SUB = 8
VMEM_CAP = 56 << 20

ADAM_LR, ADAM_B1, ADAM_B2, ADAM_EPS, ADAM_WD, ADAM_STEP = 0.001, 0.9, 0.999, 1e-08, 0.01, 10

_NN = (((1,), (0,)), ((), ()))
_NT = (((1,), (1,)), ((), ()))
_TN = (((0,), (0,)), ((), ()))


def _dot(a, b, dims=_NN, precision=None):
    return lax.dot_general(a, b, dims, precision=precision, preferred_element_type=F32)


def _bdot(a, b, dims=_NN):
    return _dot(a.astype(BF16), b.astype(BF16), dims)


def _nbytes(shape, dtype):
    n = 1
    for s in shape:
        n *= s
    return n * jnp.dtype(dtype).itemsize


def _params(sem, block_bytes):
    limit = min(VMEM_CAP, 2 * block_bytes + (20 << 20))
    return pltpu.CompilerParams(dimension_semantics=sem, vmem_limit_bytes=limit)


PIN_BYTES = 12 << 20


def _pallas_call(body, *, out_shape, **kw):
    call = pl.pallas_call

    def big(s):
        return len(s.shape) >= 2 and s.shape[-1] >= D and _nbytes(s.shape, s.dtype) >= PIN_BYTES

    pinned = jax.tree.map(lambda s: pltpu.HBM(s.shape, s.dtype) if big(s) else s, out_shape)

    def run(*args):
        return call(body, out_shape=pinned, **kw)(
            *[pltpu.with_memory_space_constraint(a, pltpu.HBM) if big(a) else a for a in args])

    return run


def _sigmoid(x):
    return 1.0 / (1.0 + jnp.exp(-x))


def _col_tile(n):
    for t in (1024, 512, 256, 128):
        if n % t == 0:
            return t
    return n


def _matmul(a, b, *, dims, grid, a_spec, b_spec, o_spec, out_shape, name, red_axis=None, res=None):
    o_blk = tuple(s for s in o_spec.block_shape if s is not None)
    via_scratch = red_axis is not None and out_shape.dtype != F32

    def body(*refs):
        if res is None:
            a_ref, b_ref, o_ref = refs[:3]
        else:
            a_ref, b_ref, r_ref, o_ref = refs[:4]
        part = _dot(a_ref[...], b_ref[...], dims)
        if red_axis is None:
            if res is not None:
                part = part + r_ref[...]
            o_ref[...] = part.astype(o_ref.dtype)
        else:
            acc = refs[-1] if via_scratch else o_ref
            r = pl.program_id(red_axis)

            @pl.when(r == 0)
            def _():
                acc[...] = part + r_ref[...] if res is not None else part

            @pl.when(r > 0)
            def _():
                acc[...] += part

            if via_scratch:
                @pl.when(r == grid[red_axis] - 1)
                def _():
                    o_ref[...] = acc[...].astype(o_ref.dtype)

    def blk(spec, arr):
        return _nbytes([s for s in spec.block_shape if s is not None], arr.dtype)

    ins = [a, b] + ([res] if res is not None else [])
    specs = [a_spec, b_spec] + ([o_spec] if res is not None else [])
    nb = blk(a_spec, a) + blk(b_spec, b) + 2 * _nbytes(o_blk, F32)
    sem = tuple("arbitrary" if i == red_axis else "parallel" for i in range(len(grid)))
    return _pallas_call(
        body, name=name, grid=grid, in_specs=specs, out_specs=o_spec, out_shape=out_shape,
        scratch_shapes=[pltpu.VMEM(o_blk, F32)] if via_scratch else [],
        compiler_params=_params(sem, nb),
    )(*ins)


def _row_tiles(m, row_bytes, fixed_bytes, temp_row_bytes=0):
    for nt in (MM_TILES // 2, MM_TILES):
        tm = m // nt
        if 2 * (row_bytes * tm + fixed_bytes) + temp_row_bytes * tm <= VMEM_CAP - (10 << 20):
            return nt
    return MM_TILES


def _mm(a, b, out_dtype, name, res=None, dims=_NN):
    m, k = a.shape
    n = b.shape[1] if dims == _NN else b.shape[0]
    tn = _col_tile(n)
    nt = _row_tiles(m, 2 * k + tn * (jnp.dtype(out_dtype).itemsize + (4 if res is not None else 0)), 2 * k * tn, 4 * tn)
    tm = m // nt
    if dims == _NN:
        b_spec = pl.BlockSpec((k, tn), lambda j, i: (0, j))
    else:
        b_spec = pl.BlockSpec((tn, k), lambda j, i: (j, 0))
    return _matmul(
        a, b, dims=dims, grid=(n // tn, nt), a_spec=pl.BlockSpec((tm, k), lambda j, i: (i, 0)), b_spec=b_spec,
        o_spec=pl.BlockSpec((tm, tn), lambda j, i: (i, j)), out_shape=jax.ShapeDtypeStruct((m, n), out_dtype),
        name=name, res=res)


def _mm_tn(a, g, name, out_dtype=F32):
    m, k = a.shape
    n = g.shape[1]
    tn = _col_tile(n)
    nt = _row_tiles(m, 2 * k + 2 * tn, 4 * k * tn)
    tm = m // nt
    return _matmul(
        a, g, dims=_TN, grid=(n // tn, nt), red_axis=1, a_spec=pl.BlockSpec((tm, k), lambda j, i: (i, 0)),
        b_spec=pl.BlockSpec((tm, tn), lambda j, i: (i, j)), o_spec=pl.BlockSpec((k, tn), lambda j, i: (0, j)),
        out_shape=jax.ShapeDtypeStruct((k, n), out_dtype), name=name)


def _mm_up(u, w_up, name):
    t = u.shape[0]
    g = w_up.shape[0]
    nt = _row_tiles(t, 2 * D + 2 * FB, 2 * D * FB, 4 * FB)
    tm = t // nt
    return _matmul(
        u, w_up, dims=_NN, grid=(g, nt), a_spec=pl.BlockSpec((tm, D), lambda g_, i: (i, 0)),
        b_spec=pl.BlockSpec((None, D, FB), lambda g_, i: (g_, 0, 0)),
        o_spec=pl.BlockSpec((None, tm, FB), lambda g_, i: (g_, i, 0)),
        out_shape=jax.ShapeDtypeStruct((g, t, FB), BF16), name=name)


def _mm_blocks_red(a, b, name, dims, res=None, b_off=0):
    g, t, k = a.shape
    n = b.shape[2] if dims == _NN else b.shape[1]
    nt = _row_tiles(t, 2 * k + n * (8 if res is not None else 4), 2 * k * n, 4 * n)
    tm = t // nt
    return _matmul(
        a, b, dims=dims, grid=(nt, g), red_axis=1, a_spec=pl.BlockSpec((None, tm, k), lambda i, g_: (g_, i, 0)),
        b_spec=pl.BlockSpec((None,) + b.shape[1:], lambda i, g_: (g_ + b_off, 0, 0)),
        o_spec=pl.BlockSpec((tm, n), lambda i, g_: (i, 0)), out_shape=jax.ShapeDtypeStruct((t, n), F32),
        name=name, res=res)


def _mm_to_blocks(a, b, name):
    t, k = a.shape
    g, n, _ = b.shape
    nt = _row_tiles(t, 2 * k + 2 * n, 2 * k * n, 4 * n)
    tm = t // nt
    return _matmul(
        a, b, dims=_NT, grid=(g, nt), a_spec=pl.BlockSpec((tm, k), lambda g_, i: (i, 0)),
        b_spec=pl.BlockSpec((None, n, k), lambda g_, i: (g_, 0, 0)),
        o_spec=pl.BlockSpec((None, tm, n), lambda g_, i: (g_, i, 0)),
        out_shape=jax.ShapeDtypeStruct((g, t, n), BF16), name=name)


def _mm_tn_blocks(a, g, name, a_blocked, g_blocked, out_dtype=F32):
    nb = a.shape[0] if a_blocked else g.shape[0]
    t = a.shape[-2]
    k, n = a.shape[-1], g.shape[-1]
    nt = _row_tiles(t, 2 * k + 2 * n, 4 * k * n)
    tm = t // nt
    a_spec = (pl.BlockSpec((None, tm, k), lambda g_, i: (g_, i, 0)) if a_blocked
              else pl.BlockSpec((tm, k), lambda g_, i: (i, 0)))
    g_spec = (pl.BlockSpec((None, tm, n), lambda g_, i: (g_, i, 0)) if g_blocked
              else pl.BlockSpec((tm, n), lambda g_, i: (i, 0)))
    return _matmul(
        a, g, dims=_TN, grid=(nb, nt), red_axis=1, a_spec=a_spec, b_spec=g_spec,
        o_spec=pl.BlockSpec((None, k, n), lambda g_, i: (g_, 0, 0)),
        out_shape=jax.ShapeDtypeStruct((nb, k, n), out_dtype), name=name)


def _mm_cols(a, b, out_dtype, name, dims):
    t = a.shape[0]
    g = b.shape[0]
    ka = a.shape[1] // g
    n = b.shape[2] if dims == _NN else b.shape[1]
    tm = t // MM_TILES
    return _matmul(
        a, b, dims=dims, grid=(g, MM_TILES), a_spec=pl.BlockSpec((tm, ka), lambda g_, i: (i, g_)),
        b_spec=pl.BlockSpec((None,) + b.shape[1:], lambda g_, i: (g_, 0, 0)),
        o_spec=pl.BlockSpec((tm, n), lambda g_, i: (i, g_)), out_shape=jax.ShapeDtypeStruct((t, g * n), out_dtype),
        name=name)


def _mm_tn_cols(a, g, nblk, name):
    t = a.shape[0]
    ka, n = a.shape[1] // nblk, g.shape[1] // nblk
    tm = t // MM_TILES
    return _matmul(
        a, g, dims=_TN, grid=(nblk, MM_TILES), red_axis=1, a_spec=pl.BlockSpec((tm, ka), lambda g_, i: (i, g_)),
        b_spec=pl.BlockSpec((tm, n), lambda g_, i: (i, g_)), o_spec=pl.BlockSpec((None, ka, n), lambda g_, i: (g_, 0, 0)),
        out_shape=jax.ShapeDtypeStruct((nblk, ka, n), F32), name=name)


def _rows(cols, n=None):
    if n is None:
        return pl.BlockSpec((TE, cols), lambda i: (i, 0))
    return pl.BlockSpec((TE, cols), lambda i: (n - 1 - i, 0))


def _whole(shape):
    return pl.BlockSpec(shape, lambda *_: (0,) * len(shape))


def _row_ids(i, rows=TE):
    return i * rows + lax.broadcasted_iota(jnp.int32, (rows, 1), 0)


def _rmsnorm_fwd(h, gain, name):
    t = h.shape[0]

    def body(h_ref, g_ref, u_ref):
        x = h_ref[...]
        r = lax.rsqrt(jnp.mean(x * x, axis=-1, keepdims=True) + EPS)
        u_ref[...] = (x * r * g_ref[...]).astype(BF16)

    return _pallas_call(
        body, name=name, grid=(t // TE,), in_specs=[_rows(D), _whole((1, D))], out_specs=_rows(D),
        out_shape=jax.ShapeDtypeStruct((t, D), BF16), compiler_params=_params(("parallel",), 3 * TE * D * 4),
    )(h, gain)


def _rmsnorm_bwd(x, du, dres, gain, name):
    t = x.shape[0]

    def body(x_ref, du_ref, dr_ref, g_ref, dx_ref, dxb_ref, dg_ref):
        i = pl.program_id(0)
        xv = x_ref[...]
        r = lax.rsqrt(jnp.mean(xv * xv, axis=-1, keepdims=True) + EPS)
        gdy = du_ref[...] * g_ref[...]
        dx = dr_ref[...] + r * gdy - xv * (r * r * r) * jnp.mean(xv * gdy, axis=-1, keepdims=True)
        dx = jnp.where(_row_ids(i) >= LEAD, dx, 0.0)
        dx_ref[...] = dx
        dxb_ref[...] = dx.astype(BF16)
        part = jnp.sum(du_ref[...] * xv * r, axis=0, keepdims=True)

        @pl.when(i == 0)
        def _():
            dg_ref[...] = part

        @pl.when(i > 0)
        def _():
            dg_ref[...] += part

    return _pallas_call(
        body, name=name, grid=(t // TE,), in_specs=[_rows(D), _rows(D), _rows(D), _whole((1, D))],
        out_specs=[_rows(D), _rows(D), _whole((1, D))],
        out_shape=[jax.ShapeDtypeStruct((t, D), F32), jax.ShapeDtypeStruct((t, D), BF16),
                   jax.ShapeDtypeStruct((1, D), F32)],
        compiler_params=_params(("arbitrary",), 5 * TE * D * 4),
    )(x, du, dres, gain)


def _loss_bwd(h, target, gain, name):
    t = h.shape[0]

    def body(h_ref, t_ref, g_ref, dx_ref, dxb_ref, dg_ref, loss_ref):
        i = pl.program_id(0)
        xv = h_ref[...]
        gain_v = g_ref[...]
        r = lax.rsqrt(jnp.mean(xv * xv, axis=-1, keepdims=True) + EPS)
        real = _row_ids(i) >= ROW0
        err = jnp.where(real, xv * r * gain_v - t_ref[...], 0.0)
        dy = err * (1.0 / D)
        gdy = dy * gain_v
        dx = r * gdy - xv * (r * r * r) * jnp.mean(xv * gdy, axis=-1, keepdims=True)
        dx_ref[...] = dx
        dxb_ref[...] = dx.astype(BF16)
        dgp = jnp.sum(dy * xv * r, axis=0, keepdims=True)
        lp = 0.5 * jnp.sum(jnp.mean(err * err, axis=-1, keepdims=True), axis=0, keepdims=True)

        @pl.when(i == 0)
        def _():
            dg_ref[...] = dgp
            loss_ref[...] = jnp.broadcast_to(lp, (1, LANE))

        @pl.when(i > 0)
        def _():
            dg_ref[...] += dgp
            loss_ref[...] += jnp.broadcast_to(lp, (1, LANE))

    return _pallas_call(
        body, name=name, grid=(t // TE,), in_specs=[_rows(D), _rows(D), _whole((1, D))],
        out_specs=[_rows(D), _rows(D), _whole((1, D)), _whole((1, LANE))],
        out_shape=[jax.ShapeDtypeStruct((t, D), F32), jax.ShapeDtypeStruct((t, D), BF16),
                   jax.ShapeDtypeStruct((1, D), F32), jax.ShapeDtypeStruct((1, LANE), F32)],
        compiler_params=_params(("arbitrary",), 4 * TE * D * 4),
    )(h, target, gain)


def _seq_scratch(cols):
    return pltpu.VMEM((-(-cols // LANE), TE + SUB, LANE), F32)


def _taps(scr, c, wd, first, n, k):
    return [scr[c, first - (k - 1) + j:first - (k - 1) + j + n, 0:wd] for j in range(k)]


def _stage_history(scr, i):
    @pl.when(i == 0)
    def _():
        scr[...] = jnp.zeros(scr.shape, F32)

    @pl.when(i > 0)
    def _():
        scr[:, 0:SUB, :] = scr[:, TE:TE + SUB, :]


def _stage_future(scr, i):
    @pl.when(i == 0)
    def _():
        scr[...] = jnp.zeros(scr.shape, F32)

    @pl.when(i > 0)
    def _():
        scr[:, TE:TE + SUB, :] = scr[:, 0:SUB, :]


def _conv(tp, w):
    out = w[0:1] * tp[0]
    for j in range(1, len(tp)):
        out = out + w[j:j + 1] * tp[j]
    return out


def _conv_t(ds, c, wd, w):
    k = w.shape[0]
    out = w[k - 1:k] * ds[c, 0:TE, 0:wd]
    for j in range(k - 1):
        out = out + w[j:j + 1] * ds[c, k - 1 - j:k - 1 - j + TE, 0:wd]
    return out


def _gdn_pre_fwd(x, w, name):
    t = x.shape[0]

    def body(x_ref, w_ref, o_ref, xs):
        _stage_history(xs, pl.program_id(0))
        for hh in range(3 * H):
            sl = slice(hh * DH, (hh + 1) * DH)
            xs[hh, SUB:SUB + TE, :] = x_ref[:, sl].astype(F32)
            cv = _conv(_taps(xs, hh, DH, SUB, TE, 4), w_ref[:, sl])
            s = cv * _sigmoid(cv)
            if hh < 2 * H:
                s = s * lax.rsqrt(jnp.sum(s * s, axis=-1, keepdims=True) + EPS)
                if hh < H:
                    s = s * (DH ** -0.5)
            o_ref[:, sl] = s

    return _pallas_call(
        body, name=name, grid=(t // TE,), in_specs=[_rows(QKV), _whole((4, QKV))], out_specs=_rows(QKV),
        out_shape=jax.ShapeDtypeStruct((t, QKV), F32), scratch_shapes=[_seq_scratch(QKV)],
        compiler_params=_params(("arbitrary",), TE * QKV * 8),
    )(x, w)


def _gdn_pre_bwd(x, w, dqkv, name):
    t = x.shape[0]
    n = t // TE
    hb = TE // 16

    def body(x_ref, xp_ref, w_ref, d_ref, dx_ref, dw_ref, xs, ds):
        i = pl.program_id(0)

        @pl.when(i == 0)
        def _():
            dw_ref[...] = jnp.zeros_like(dw_ref)

        _stage_future(ds, i)
        for hh in range(3 * H):
            sl = slice(hh * DH, (hh + 1) * DH)
            xs[hh, 0:SUB, :] = jnp.where(i == n - 1, 0.0, xp_ref[SUB:2 * SUB, sl].astype(F32))
            xs[hh, SUB:SUB + TE, :] = x_ref[:, sl].astype(F32)
            wv = w_ref[:, sl]
            tp = _taps(xs, hh, DH, SUB, TE, 4)
            cv = _conv(tp, wv)
            sg = _sigmoid(cv)
            s = cv * sg
            dsv = d_ref[:, sl]
            if hh < 2 * H:
                if hh < H:
                    dsv = dsv * (DH ** -0.5)
                r = lax.rsqrt(jnp.sum(s * s, axis=-1, keepdims=True) + EPS)
                dsv = r * dsv - s * (r * r * r) * jnp.sum(s * dsv, axis=-1, keepdims=True)
            dcv = dsv * (sg * (1.0 + cv * (1.0 - sg)))
            ds[hh, 0:TE, :] = dcv
            dx_ref[:, sl] = _conv_t(ds, hh, DH, wv).astype(BF16)
            dw_ref[:, sl] += jnp.concatenate([jnp.sum(tp[j] * dcv, axis=0, keepdims=True) for j in range(4)], axis=0)

    return _pallas_call(
        body, name=name, grid=(n,),
        in_specs=[_rows(QKV, n), pl.BlockSpec((16, QKV), lambda i: (jnp.maximum((n - 1 - i) * hb - 1, 0), 0)),
                  _whole((4, QKV)), _rows(QKV, n)],
        out_specs=[_rows(QKV, n), _whole((4, QKV))],
        out_shape=[jax.ShapeDtypeStruct((t, QKV), BF16), jax.ShapeDtypeStruct((4, QKV), F32)],
        scratch_shapes=[_seq_scratch(QKV), _seq_scratch(QKV)],
        compiler_params=_params(("arbitrary",), TE * QKV * 14),
    )(x, x, w, dqkv)


def _softplus(x):
    return jnp.maximum(x, 0.0) + jnp.log(1.0 + jnp.exp(-jnp.abs(x)))


def _gates_fwd(ba, a_row, dt_row, name):
    t = ba.shape[0]

    def body(ba_ref, a_ref, dt_ref, b_out, g_out):
        real = _row_ids(pl.program_id(0)) >= LEAD
        b_out[...] = jnp.where(real, _sigmoid(ba_ref[:, 0:LANE]), 0.0)
        g = -jnp.exp(a_ref[...]) * _softplus(ba_ref[:, LANE:2 * LANE] + dt_ref[...])
        g_out[...] = jnp.where(real, g, 0.0)

    return _pallas_call(
        body, name=name, grid=(t // TE,), in_specs=[_rows(2 * LANE), _whole((1, LANE)), _whole((1, LANE))],
        out_specs=[_rows(LANE), _rows(LANE)],
        out_shape=[jax.ShapeDtypeStruct((t, LANE), F32), jax.ShapeDtypeStruct((t, LANE), F32)],
        compiler_params=_params(("parallel",), TE * LANE * 16),
    )(ba, a_row, dt_row)


def _gates_bwd(ba, a_row, dt_row, dbeta, dg, name):
    t = ba.shape[0]

    def body(ba_ref, a_ref, dt_ref, db_ref, dg_ref, dba_ref, da_out, ddt_out):
        i = pl.program_id(0)
        real = _row_ids(i) >= LEAD
        beta = _sigmoid(ba_ref[:, 0:LANE])
        draw_b = jnp.where(real, db_ref[...] * beta * (1.0 - beta), 0.0)
        pre = ba_ref[:, LANE:2 * LANE] + dt_ref[...]
        neg_a = -jnp.exp(a_ref[...])
        dgv = jnp.where(real, dg_ref[...], 0.0)
        draw_a = dgv * neg_a * _sigmoid(pre)
        dba_ref[:, 0:LANE] = draw_b.astype(BF16)
        dba_ref[:, LANE:2 * LANE] = draw_a.astype(BF16)
        dal = jnp.sum(dgv * neg_a * _softplus(pre), axis=0, keepdims=True)
        ddt = jnp.sum(draw_a, axis=0, keepdims=True)

        @pl.when(i == 0)
        def _():
            da_out[...] = dal
            ddt_out[...] = ddt

        @pl.when(i > 0)
        def _():
            da_out[...] += dal
            ddt_out[...] += ddt

    return _pallas_call(
        body, name=name, grid=(t // TE,),
        in_specs=[_rows(2 * LANE), _whole((1, LANE)), _whole((1, LANE)), _rows(LANE), _rows(LANE)],
        out_specs=[_rows(2 * LANE), _whole((1, LANE)), _whole((1, LANE))],
        out_shape=[jax.ShapeDtypeStruct((t, 2 * LANE), BF16), jax.ShapeDtypeStruct((1, LANE), F32),
                   jax.ShapeDtypeStruct((1, LANE), F32)],
        compiler_params=_params(("arbitrary",), TE * LANE * 24),
    )(ba, a_row, dt_row, dbeta, dg)


_OFFSETS = [(dx, dy, dc) for dx in (0, 1) for dy in (0, 1) for dc in (0, 1)][1:]
NPEER = len(_OFFSETS)
ANY_SPEC = pl.BlockSpec(memory_space=pl.ANY)


def _place():
    return lax.axis_index("x"), lax.axis_index("y"), lax.axis_index("c")


def _index(p):
    return 4 * p[0] + 2 * p[1] + p[2]


def _comm_scratch(n):
    return [pltpu.SemaphoreType.DMA((n * NPEER,)), pltpu.SemaphoreType.DMA((n * NPEER,)), pltpu.SemaphoreType.DMA((n,))]


def _scatter_copies(ins, outs, send, recv):
    me = _place()
    mi = _index(me)
    res = []
    for j, d in enumerate(_OFFSETS):
        peer = tuple(1 - v if bit else v for v, bit in zip(me, d))
        pi = _index(peer)
        for k in range(len(ins)):
            sem = k * NPEER + j
            mine = pltpu.make_async_remote_copy(src_ref=ins[k].at[pi], dst_ref=outs[k].at[mi], send_sem=send.at[sem],
                                                recv_sem=recv.at[sem], device_id=peer, device_id_type=MESH)
            theirs = pltpu.make_async_remote_copy(src_ref=ins[k].at[pi], dst_ref=outs[k].at[pi], send_sem=send.at[sem],
                                                  recv_sem=recv.at[sem], device_id=peer, device_id_type=MESH)
            res.append((mine, theirs))
    return res


def _scatter_own(ins, outs, loc):
    mi = _index(_place())
    return [pltpu.make_async_copy(ins[k].at[mi], outs[k].at[mi], loc.at[k]) for k in range(len(ins))]


def _scatter_start(ins, outs, send, recv, loc):
    for cp in _scatter_own(ins, outs, loc):
        cp.start()
    for mine, _ in _scatter_copies(ins, outs, send, recv):
        mine.start()


def _scatter_wait(ins, outs, send, recv, loc):
    cps = _scatter_copies(ins, outs, send, recv)
    for _, theirs in cps:
        theirs.wait_recv()
    for mine, _ in cps:
        mine.wait_send()
    for cp in _scatter_own(ins, outs, loc):
        cp.wait()


def _gather_parts(ins, outs, send, recv):
    x, y, c = _place()
    chips = [(1 - x, y), (x, 1 - y), (1 - x, 1 - y)]

    def cp(k, slot, src, block, to):
        return pltpu.make_async_remote_copy(src_ref=src, dst_ref=outs[k].at[_index(block)], send_sem=send.at[k * NPEER + slot],
                                            recv_sem=recv.at[k * NPEER + slot], device_id=to, device_id_type=MESH)

    return (x, y, c), (x, y, 1 - c), chips, cp


def _gather_start(ins, outs, send, recv, loc):
    me, sib, chips, cp = _gather_parts(ins, outs, send, recv)
    for k in range(len(ins)):
        pltpu.make_async_copy(ins[k], outs[k].at[_index(me)], loc.at[k]).start()
        cp(k, 0, ins[k], me, sib).start()
        for j, chip in enumerate(chips):
            cp(k, 1 + j, ins[k], me, (*chip, me[2])).start()


def _gather_forward(ins, outs, send, recv, loc):
    me, sib, chips, cp = _gather_parts(ins, outs, send, recv)
    for j, chip in enumerate(chips):
        blk = (*chip, me[2])
        for k in range(len(ins)):
            cp(k, 1 + j, ins[k], blk, me).wait_recv()
            cp(k, 4 + j, outs[k].at[_index(blk)], blk, sib).start()


def _gather_finish(ins, outs, send, recv, loc):
    me, sib, chips, cp = _gather_parts(ins, outs, send, recv)
    for k in range(len(ins)):
        cp(k, 0, ins[k], sib, me).wait_recv()
        for j, chip in enumerate(chips):
            cp(k, 4 + j, ins[k], (*chip, sib[2]), me).wait_recv()
        cp(k, 0, ins[k], me, sib).wait_send()
        for j, chip in enumerate(chips):
            cp(k, 1 + j, ins[k], me, (*chip, me[2])).wait_send()
            cp(k, 4 + j, outs[k].at[_index((*chip, me[2]))], (*chip, me[2]), sib).wait_send()
        pltpu.make_async_copy(ins[k], outs[k].at[_index(me)], loc.at[k]).wait()


def _gathered_shapes(arrs):
    return [jax.ShapeDtypeStruct((NDEV,) + a.shape, a.dtype) for a in arrs]


def _gather(arrs, name):
    n = len(arrs)

    def body(*refs):
        ins, outs, sems = refs[:n], refs[n:2 * n], refs[2 * n:]
        _gather_start(ins, outs, *sems)
        _gather_forward(ins, outs, *sems)
        _gather_finish(ins, outs, *sems)

    return _pallas_call(body, name=name, in_specs=[ANY_SPEC] * n, out_specs=[ANY_SPEC] * n,
                          out_shape=_gathered_shapes(arrs), scratch_shapes=_comm_scratch(n))(*arrs)


def _scatter(arrs, name):
    n = len(arrs)

    def body(*refs):
        ins, outs, sems = refs[:n], refs[n:2 * n], refs[2 * n:]
        _scatter_start(ins, outs, *sems)
        _scatter_wait(ins, outs, *sems)

    return _pallas_call(body, name=name, in_specs=[ANY_SPEC] * n, out_specs=[ANY_SPEC] * n,
                          out_shape=[jax.ShapeDtypeStruct(a.shape, a.dtype) for a in arrs],
                          scratch_shapes=_comm_scratch(n))(*arrs)


_BNN = (((2,), (1,)), ((0,), (0,)))
_BNT = (((2,), (2,)), ((0,), (0,)))
_BTN = (((1,), (1,)), ((0,), (0,)))


def _split(a):
    hi = a.astype(BF16)
    return hi, (a - hi.astype(F32)).astype(BF16)


def _dot3(a, b, dims=_BNN):
    ah, al = _split(a)
    bh, bl = _split(b)
    if dims != _BNN:
        return _dot(ah, bh, dims) + _dot(al, bh, dims) + _dot(ah, bl, dims)
    m = a.shape[1]
    r = _dot(jnp.concatenate([ah, al], axis=1), bh, _BNN)
    return r[:, :m] + r[:, m:] + _dot(ah, bl, _BNN)


def _tri_sum(mask, x):
    x1 = x.astype(BF16)
    r1 = x - x1.astype(F32)
    x2 = r1.astype(BF16)
    x3 = (r1 - x2.astype(F32)).astype(BF16)
    mb = mask.astype(BF16)
    return _dot(mb, x1) + _dot(mb, x2) + _dot(mb, x3)


GC = 3


def _rows_of(c):
    return slice(c * CH, (c + 1) * CH)


def _heads(ref, off):
    return jnp.stack([ref[_rows_of(c), off + h * DH:off + (h + 1) * DH] for c in range(GC) for h in range(H)])


def _cols(arrs):
    return jnp.stack([a[:, h:h + 1] for a in arrs for h in range(H)])


def _lanes(a):
    lane = lax.broadcasted_iota(jnp.int32, (CH, LANE), 1)
    out = jnp.zeros((CH, LANE), F32)
    for h in range(H):
        out = jnp.where(lane == h, a[h], out)
    return out


def _chunk_prep(qkv_ref, b_ref, g_ref):
    row = lax.broadcasted_iota(jnp.int32, (CH, CH), 0)
    col = lax.broadcasted_iota(jnp.int32, (CH, CH), 1)
    incl, strict = row >= col, row > col
    gcs = [_tri_sum(incl, g_ref[_rows_of(c), :]) for c in range(GC)]
    q, k, v = _heads(qkv_ref, 0), _heads(qkv_ref, D), _heads(qkv_ref, 2 * D)
    bcol, gcol = _cols([b_ref[_rows_of(c), :] for c in range(GC)]), _cols(gcs)
    grow = jnp.stack([gct[h:h + 1, :] for gct in [gc.T for gc in gcs] for h in range(H)])
    glast = _cols([gc[CH - 1:CH, :] for gc in gcs])
    dec = jnp.exp(jnp.where(incl[None], gcol - grow, -1e30))
    kb = k * bcol
    ab = _bdot(jnp.concatenate([kb, q], axis=1), k, _BNT)
    egc, ekc = jnp.exp(gcol), jnp.exp(glast - gcol)
    return dict(row=row, col=col, strict=strict[None], q=q, k=k, v=v, bcol=bcol, dec=dec, kb=kb,
                lm=jnp.where(strict[None], ab[:, :CH] * dec, 0.0), qk=ab[:, CH:] * dec, egc=egc, ekc=ekc,
                gth=jnp.exp(glast), qd=q * egc, kd=k * ekc, vb=v * bcol, kbg=kb * egc)


def _unit_lower_inverse(lm, eye):
    n = -lm
    x = eye + n
    pw = _dot3(n, n)
    for it in range(5):
        if it < 4:
            xp = _dot3(jnp.concatenate([x, pw], axis=1), pw)
            x = x + xp[:, :CH]
            pw = xp[:, CH:]
        else:
            x = x + _dot3(x, pw)
    return x


def _gdn_fwd(qkv, beta, g, name, gather=()):
    t = qkv.shape[0]
    nc = t // CH
    ns = nc // GC
    ng = len(gather)

    def body(qkv_ref, b_ref, g_ref, *rest):
        c_ins, (o_ref, sin_ref, vn_ref, ti_ref, w_ref) = rest[:ng], rest[ng:ng + 5]
        c_outs, state, sems = rest[ng + 5:2 * ng + 5], rest[2 * ng + 5], rest[2 * ng + 6:]
        step = pl.program_id(0)

        @pl.when(step == 0)
        def _():
            state[...] = jnp.zeros_like(state)
            if ng:
                _gather_start(c_ins, c_outs, *sems)

        if ng:
            @pl.when(step == max(ns - 4, 0))
            def _():
                _gather_forward(c_ins, c_outs, *sems)

            @pl.when(step == ns - 1)
            def _():
                _gather_finish(c_ins, c_outs, *sems)

        pr = _chunk_prep(qkv_ref, b_ref, g_ref)
        tinv = _unit_lower_inverse(pr["lm"], (pr["row"] == pr["col"]).astype(F32)[None])
        uw = _bdot(tinv, jnp.concatenate([pr["vb"], pr["kbg"]], axis=2), _BNN)
        u, w = uw[:, :, :DH], uw[:, :, DH:]
        s = state[...]
        for c in range(GC):
            hs = slice(c * H, (c + 1) * H)
            ws = _bdot(jnp.concatenate([w[hs], pr["qd"][hs]], axis=1), s, _BNN)
            vn = u[hs] - ws[:, :CH]
            o = ws[:, CH:] + _bdot(pr["qk"][hs], vn, _BNN)
            sin_ref[c] = s
            ti_ref[c] = tinv[hs]
            s = s * pr["gth"][hs] + _bdot(pr["kd"][hs], vn, _BTN)
            for h in range(H):
                sl = slice(h * DH, (h + 1) * DH)
                o_ref[_rows_of(c), sl] = o[h]
                vn_ref[_rows_of(c), sl] = vn[h]
                w_ref[_rows_of(c), sl] = w[c * H + h]
        state[...] = s

    chunk = lambda cols: pl.BlockSpec((GC * CH, cols), lambda c: (c, 0))
    outs = _pallas_call(
        body, name=name, grid=(ns,), in_specs=[chunk(QKV), chunk(LANE), chunk(LANE)] + [ANY_SPEC] * ng,
        out_specs=[chunk(D), pl.BlockSpec((GC, H, DH, DH), lambda c: (c, 0, 0, 0)), chunk(D),
                   pl.BlockSpec((GC, H, CH, CH), lambda c: (c, 0, 0, 0)), chunk(D)] + [ANY_SPEC] * ng,
        out_shape=[jax.ShapeDtypeStruct((t, D), F32), jax.ShapeDtypeStruct((nc, H, DH, DH), F32),
                   jax.ShapeDtypeStruct((t, D), F32), jax.ShapeDtypeStruct((nc, H, CH, CH), F32),
                   jax.ShapeDtypeStruct((t, D), F32)] + _gathered_shapes(gather),
        scratch_shapes=[pltpu.VMEM((H, DH, DH), F32)] + (_comm_scratch(ng) if ng else []),
        compiler_params=_params(("arbitrary",), 12 << 20),
    )(qkv, beta, g, *gather)
    return outs[:5], outs[5:]


def _gdn_bwd(qkv, beta, g, do, s_in, vnew, tinv, wsv, name, scatter=()):
    t = qkv.shape[0]
    nsteps = t // CH // GC
    ns = len(scatter)

    def body(qkv_ref, b_ref, g_ref, do_ref, sin_ref, vn_ref, ti_ref, w_ref, *rest):
        c_ins, (dqkv_ref, db_ref, dg_ref) = rest[:ns], rest[ns:ns + 3]
        c_outs, dstate, sems = rest[ns + 3:2 * ns + 3], rest[2 * ns + 3], rest[2 * ns + 4:]
        step = pl.program_id(0)

        @pl.when(step == 0)
        def _():
            dstate[...] = jnp.zeros_like(dstate)
            if ns:
                _scatter_start(c_ins, c_outs, *sems)

        if ns:
            @pl.when(step == nsteps - 1)
            def _():
                _scatter_wait(c_ins, c_outs, *sems)

        pr = _chunk_prep(qkv_ref, b_ref, g_ref)
        ti = jnp.concatenate([ti_ref[c] for c in range(GC)], axis=0)
        s = jnp.concatenate([sin_ref[c] for c in range(GC)], axis=0)
        w, vn, doh = _heads(w_ref, 0), _heads(vn_ref, 0), _heads(do_ref, 0)
        dqd = _bdot(doh, s, _BNT)
        dqk = _bdot(doh, vn, _BNT)
        qk_do = _bdot(pr["qk"], doh, _BTN)
        qd_do = _bdot(pr["qd"], doh, _BTN)
        ds = dstate[...]
        dvn_c, dkd_c, dw_c, dgt_c = [None] * GC, [None] * GC, [None] * GC, [None] * GC
        for c in reversed(range(GC)):
            hs = slice(c * H, (c + 1) * H)
            dvn_c[c] = _bdot(pr["kd"][hs], ds, _BNN) + qk_do[hs]
            dkd_c[c] = _bdot(vn[hs], ds, _BNT)
            dw_c[c] = -_bdot(dvn_c[c], s[hs], _BNT)
            dgt_c[c] = jnp.sum(jnp.sum(ds * s[hs], axis=2, keepdims=True), axis=1, keepdims=True)
            ds = ds * pr["gth"][hs] + qd_do[hs] - _bdot(w[hs], dvn_c[c], _BTN)
        dstate[...] = ds
        dvn, dkd, dw, dgt = (jnp.concatenate(parts, axis=0) for parts in (dvn_c, dkd_c, dw_c, dgt_c))
        duw = jnp.concatenate([dvn, dw], axis=2)
        dvk = _bdot(ti, duw, _BTN)
        dvb, dkbg = dvk[:, :, :DH], dvk[:, :, DH:]
        dti = _bdot(duw, jnp.concatenate([pr["vb"], pr["kbg"]], axis=2), _BNT)
        dl = -_dot3(_dot3(ti, dti, _BTN), ti, _BNT)
        dl = jnp.where(pr["strict"], dl, 0.0)
        dab = jnp.concatenate([dl * pr["dec"], dqk * pr["dec"]], axis=1)
        r1 = _bdot(dab, pr["k"], _BNN)
        dkb = r1[:, :CH] + dkbg * pr["egc"]
        dq = r1[:, CH:] + dqd * pr["egc"]
        dk = _bdot(dab, jnp.concatenate([pr["kb"], pr["q"]], axis=1), _BTN) + dkb * pr["bcol"] + dkd * pr["ekc"]
        m = dl * pr["lm"] + dqk * pr["qk"]
        mh, ml = _split(m)
        ones = jnp.ones((GC * H, CH, LANE), BF16)
        colsum = (_dot(mh, ones, _BTN) + _dot(ml, ones, _BTN))[:, :, 0:1]
        kdsum = jnp.sum(dkd * pr["kd"], axis=2, keepdims=True)
        dgc = (jnp.sum(m, axis=2, keepdims=True) - colsum + jnp.sum(dkbg * pr["kbg"], axis=2, keepdims=True)
               + jnp.sum(dqd * pr["qd"], axis=2, keepdims=True) - kdsum)
        dglast = jnp.sum(kdsum, axis=1, keepdims=True) + dgt * pr["gth"]
        last_row = lax.broadcasted_iota(jnp.int32, (1, CH, 1), 1) == CH - 1
        dgc = dgc + jnp.where(last_row, dglast, 0.0)
        dbeta = jnp.sum(dkb * pr["k"], axis=2, keepdims=True) + jnp.sum(dvb * pr["v"], axis=2, keepdims=True)
        dv = dvb * pr["bcol"]
        upper = pr["row"] <= pr["col"]
        for c in range(GC):
            hs = slice(c * H, (c + 1) * H)
            for h in range(H):
                dqkv_ref[_rows_of(c), h * DH:(h + 1) * DH] = dq[c * H + h]
                dqkv_ref[_rows_of(c), D + h * DH:D + (h + 1) * DH] = dk[c * H + h]
                dqkv_ref[_rows_of(c), 2 * D + h * DH:2 * D + (h + 1) * DH] = dv[c * H + h]
            db_ref[_rows_of(c), :] = _lanes(dbeta[hs])
            dg_ref[_rows_of(c), :] = _tri_sum(upper, _lanes(dgc[hs]))

    chunk = lambda cols: pl.BlockSpec((GC * CH, cols), lambda c: (nsteps - 1 - c, 0))
    sq = lambda a, b: pl.BlockSpec((GC, H, a, b), lambda c: (nsteps - 1 - c, 0, 0, 0))
    outs = _pallas_call(
        body, name=name, grid=(nsteps,),
        in_specs=[chunk(QKV), chunk(LANE), chunk(LANE), chunk(D), sq(DH, DH), chunk(D), sq(CH, CH), chunk(D)] + [ANY_SPEC] * ns,
        out_specs=[chunk(QKV), chunk(LANE), chunk(LANE)] + [ANY_SPEC] * ns,
        out_shape=[jax.ShapeDtypeStruct((t, QKV), F32), jax.ShapeDtypeStruct((t, LANE), F32),
                   jax.ShapeDtypeStruct((t, LANE), F32)] + [jax.ShapeDtypeStruct(a.shape, a.dtype) for a in scatter],
        scratch_shapes=[pltpu.VMEM((H, DH, DH), F32)] + (_comm_scratch(ns) if ns else []),
        compiler_params=_params(("arbitrary",), 16 << 20),
    )(qkv, beta, g, do, s_in, vnew, tinv, wsv, *scatter)
    return outs[:3], outs[3:]


def _pool_counts(row_ids, win):
    return jnp.minimum(jnp.maximum(row_ids - LEAD, 0) + 1, win).astype(F32)


def _pool_fwd(p, name):
    t = p.shape[0]
    ext = TE + 16

    def body(p_ref, o_ref, carry):
        i = pl.program_id(0)

        @pl.when(i == 0)
        def _():
            carry[...] = jnp.zeros_like(carry)

        ids = _row_ids(i)
        for gi, win in enumerate(POOL_WINDOWS):
            sl = slice(gi * LANE, (gi + 1) * LANE)
            xv = p_ref[:, sl]
            s = jnp.concatenate([carry[:, sl], xv], axis=0)
            sh = 1
            while sh < win:
                s = s + pltpu.roll(s, sh, 0)
                sh *= 2
            o_ref[:, sl] = (s[16:ext] / _pool_counts(ids, win) - xv).astype(BF16)
            carry[:, sl] = xv[TE - 16:TE]

    return _pallas_call(
        body, name=name, grid=(t // TE,), in_specs=[_rows(POOL_W)], out_specs=_rows(POOL_W),
        out_shape=jax.ShapeDtypeStruct((t, POOL_W), BF16), scratch_shapes=[pltpu.VMEM((16, POOL_W), F32)],
        compiler_params=_params(("arbitrary",), TE * POOL_W * 8),
    )(p)


def _pool_bwd(dpo, name):
    t = dpo.shape[0]
    n = t // TE
    ext = TE + 16

    def body(d_ref, o_ref, carry):
        i = pl.program_id(0)

        @pl.when(i == 0)
        def _():
            carry[...] = jnp.zeros_like(carry)

        ids = _row_ids(n - 1 - i)
        for gi, win in enumerate(POOL_WINDOWS):
            sl = slice(gi * LANE, (gi + 1) * LANE)
            dv = d_ref[:, sl]
            rv = dv / _pool_counts(ids, win)
            s = jnp.concatenate([rv, carry[:, sl]], axis=0)
            sh = 1
            while sh < win:
                s = s + pltpu.roll(s, ext - sh, 0)
                sh *= 2
            o_ref[:, sl] = (s[0:TE] - dv).astype(BF16)
            carry[:, sl] = rv[0:16]

    return _pallas_call(
        body, name=name, grid=(n,), in_specs=[_rows(POOL_W, n)], out_specs=_rows(POOL_W, n),
        out_shape=jax.ShapeDtypeStruct((t, POOL_W), BF16), scratch_shapes=[pltpu.VMEM((16, POOL_W), F32)],
        compiler_params=_params(("arbitrary",), TE * POOL_W * 8),
    )(dpo)


def _post_fwd(o, z, gate, pm, hn, ps, name):
    t = o.shape[0]

    def body(o_ref, z_ref, g_ref, pm_ref, hn_ref, ps_ref, y_ref):
        for h in range(H):
            sl = slice(h * DH, (h + 1) * DH)
            ov = o_ref[:, sl]
            zv = z_ref[:, sl].astype(F32)
            r = lax.rsqrt(jnp.mean(ov * ov, axis=-1, keepdims=True) + EPS)
            ya = ov * r * hn_ref[...] * (zv * _sigmoid(zv))
            ga = _sigmoid(g_ref[:, sl].astype(F32))
            gb = _sigmoid(g_ref[:, D + h * DH:D + (h + 1) * DH].astype(F32))
            y_ref[:, sl] = (ga * ya + gb * (pm_ref[:, sl] * ps_ref[:, sl])).astype(BF16)

    return _pallas_call(
        body, name=name, grid=(t // TE,),
        in_specs=[_rows(D), _rows(D), _rows(2 * D), _rows(D), _whole((1, DH)), _whole((1, D))], out_specs=_rows(D),
        out_shape=jax.ShapeDtypeStruct((t, D), BF16), compiler_params=_params(("parallel",), TE * D * 16),
    )(o, z, gate, pm, hn, ps)


def _post_bwd(dy, o, z, gate, pm, hn, ps, name):
    t = o.shape[0]

    def body(dy_ref, o_ref, z_ref, g_ref, pm_ref, hn_ref, ps_ref, do_ref, dz_ref, dgate_ref, dpm_ref, dhn_ref, dps_ref):
        i = pl.program_id(0)

        @pl.when(i == 0)
        def _():
            dhn_ref[...] = jnp.zeros_like(dhn_ref)
            dps_ref[...] = jnp.zeros_like(dps_ref)

        hnv = hn_ref[...]
        dhn = jnp.zeros((1, DH), F32)
        for h in range(H):
            sl = slice(h * DH, (h + 1) * DH)
            slb = slice(D + h * DH, D + (h + 1) * DH)
            dyv = dy_ref[:, sl]
            ov = o_ref[:, sl]
            zv = z_ref[:, sl].astype(F32)
            r = lax.rsqrt(jnp.mean(ov * ov, axis=-1, keepdims=True) + EPS)
            sz = _sigmoid(zv)
            silu = zv * sz
            on = ov * r
            ya = on * hnv * silu
            ga = _sigmoid(g_ref[:, sl].astype(F32))
            gb = _sigmoid(g_ref[:, slb].astype(F32))
            pmv = pm_ref[:, sl]
            psv = ps_ref[:, sl]
            dya = dyv * ga
            dyb = dyv * gb
            dgate_ref[:, sl] = (dyv * ya * ga * (1.0 - ga)).astype(BF16)
            dgate_ref[:, slb] = (dyv * (pmv * psv) * gb * (1.0 - gb)).astype(BF16)
            tt = dya * hnv * silu
            do_ref[:, sl] = r * tt - ov * (r * r * r) * jnp.mean(ov * tt, axis=-1, keepdims=True)
            dz_ref[:, sl] = (dya * on * hnv * (sz * (1.0 + zv * (1.0 - sz)))).astype(BF16)
            dhn = dhn + jnp.sum(dya * on * silu, axis=0, keepdims=True)
            dps_ref[:, sl] += jnp.sum(dyb * pmv, axis=0, keepdims=True)
            dpm_ref[:, sl] = (dyb * psv).astype(BF16)
        dhn_ref[...] += dhn

    return _pallas_call(
        body, name=name, grid=(t // TE,),
        in_specs=[_rows(D), _rows(D), _rows(D), _rows(2 * D), _rows(D), _whole((1, DH)), _whole((1, D))],
        out_specs=[_rows(D), _rows(D), _rows(2 * D), _rows(D), _whole((1, DH)), _whole((1, D))],
        out_shape=[jax.ShapeDtypeStruct((t, D), F32), jax.ShapeDtypeStruct((t, D), BF16),
                   jax.ShapeDtypeStruct((t, 2 * D), BF16), jax.ShapeDtypeStruct((t, D), BF16),
                   jax.ShapeDtypeStruct((1, DH), F32), jax.ShapeDtypeStruct((1, D), F32)],
        compiler_params=_params(("arbitrary",), TE * D * 28),
    )(dy, o, z, gate, pm, hn, ps)


_FB_COLS = [(c, min(c + LANE, FB)) for c in range(0, FB, LANE)]


def _mlp_act_fwd(hid, cw, name):
    t = hid.shape[1]
    n = t // TE

    def body(hg_ref, hv_ref, wg_ref, wv_ref, a_ref, xg, xv):
        i = pl.program_id(1)
        _stage_history(xg, i)
        _stage_history(xv, i)
        for c, (c0, c1) in enumerate(_FB_COLS):
            sl, wd = slice(c0, c1), c1 - c0
            xg[c, SUB:SUB + TE, 0:wd] = hg_ref[:, sl].astype(F32)
            xv[c, SUB:SUB + TE, 0:wd] = hv_ref[:, sl].astype(F32)
            gg = _conv(_taps(xg, c, wd, SUB, TE, 3), wg_ref[:, sl])
            vv = _conv(_taps(xv, c, wd, SUB, TE, 3), wv_ref[:, sl])
            a_ref[:, sl] = (gg * _sigmoid(gg) * vv).astype(BF16)

    hspec = lambda off: pl.BlockSpec((None, TE, FB), lambda p, i: (p + off, i, 0))
    wspec = lambda off: pl.BlockSpec((None, 3, FB), lambda p, i: (p + off, 0, 0))
    return _pallas_call(
        body, name=name, grid=(4, n), in_specs=[hspec(0), hspec(4), wspec(0), wspec(4)],
        out_specs=pl.BlockSpec((None, TE, FB), lambda p, i: (p, i, 0)),
        out_shape=jax.ShapeDtypeStruct((4, t, FB), BF16),
        scratch_shapes=[_seq_scratch(FB), _seq_scratch(FB)],
        compiler_params=_params(("parallel", "arbitrary"), TE * FB * 12),
    )(hid, hid, cw, cw)


def _mlp_act_bwd(da, hid, cw, name):
    t = hid.shape[1]
    n = t // TE
    hb = TE // 16

    def body(da_ref, hg_ref, hv_ref, pg_ref, pv_ref, wg_ref, wv_ref, dhg_ref, dhv_ref, dwg_ref, dwv_ref, xg, xv, dg, dv):
        i = pl.program_id(1)

        @pl.when(i == 0)
        def _():
            dwg_ref[...] = jnp.zeros_like(dwg_ref)
            dwv_ref[...] = jnp.zeros_like(dwv_ref)

        _stage_future(dg, i)
        _stage_future(dv, i)
        first_tile = i == n - 1
        for c, (c0, c1) in enumerate(_FB_COLS):
            sl, wd = slice(c0, c1), c1 - c0
            for scr, p_ref, h_ref in ((xg, pg_ref, hg_ref), (xv, pv_ref, hv_ref)):
                scr[c, 0:SUB, 0:wd] = jnp.where(first_tile, 0.0, p_ref[SUB:2 * SUB, sl].astype(F32))
                scr[c, SUB:SUB + TE, 0:wd] = h_ref[:, sl].astype(F32)
            wg = wg_ref[:, sl]
            wv = wv_ref[:, sl]
            tg = _taps(xg, c, wd, SUB, TE, 3)
            tv = _taps(xv, c, wd, SUB, TE, 3)
            gg = _conv(tg, wg)
            vv = _conv(tv, wv)
            sg = _sigmoid(gg)
            dav = da_ref[:, sl].astype(F32)
            dgg = dav * vv * (sg * (1.0 + gg * (1.0 - sg)))
            dvv = dav * (gg * sg)
            for dc, tp, w, scr, dh_ref, dw_ref in ((dgg, tg, wg, dg, dhg_ref, dwg_ref), (dvv, tv, wv, dv, dhv_ref, dwv_ref)):
                scr[c, 0:TE, 0:wd] = dc
                dh_ref[:, sl] = _conv_t(scr, c, wd, w).astype(BF16)
                dw_ref[:, sl] += jnp.concatenate([jnp.sum(tp[j] * dc, axis=0, keepdims=True) for j in range(3)], axis=0)

    rev = lambda off: pl.BlockSpec((None, TE, FB), lambda p, i: (p + off, n - 1 - i, 0))
    halo = lambda off: pl.BlockSpec((None, 16, FB), lambda p, i: (p + off, jnp.maximum((n - 1 - i) * hb - 1, 0), 0))
    wspec = lambda off: pl.BlockSpec((None, 3, FB), lambda p, i: (p + off, 0, 0))
    dwspec = pl.BlockSpec((None, 3, FB), lambda p, i: (p, 0, 0))
    return _pallas_call(
        body, name=name, grid=(4, n), in_specs=[rev(0), rev(0), rev(4), halo(0), halo(4), wspec(0), wspec(4)],
        out_specs=[rev(0), rev(0), dwspec, dwspec],
        out_shape=[jax.ShapeDtypeStruct((4, t, FB), BF16), jax.ShapeDtypeStruct((4, t, FB), BF16),
                   jax.ShapeDtypeStruct((4, 3, FB), F32), jax.ShapeDtypeStruct((4, 3, FB), F32)],
        scratch_shapes=[_seq_scratch(FB)] * 4,
        compiler_params=_params(("parallel", "arbitrary"), TE * FB * 24),
    )(da, hid, hid, hid, hid, cw, cw)


def _adamw(lands, w, m, v, name):
    nl, r, c = w.shape
    tr = r
    if r * c * 4 > (2 << 20):
        for cand in (128, 64, 32, 16):
            if r % cand == 0:
                tr = cand
                break
    nr = r // tr
    c1 = 1.0 - ADAM_B1 ** ADAM_STEP
    c2 = 1.0 - ADAM_B2 ** ADAM_STEP

    def body(*refs):
        l_refs, (w_ref, m_ref, v_ref, g_out, d_out, m_out, v_out) = refs[:nl], refs[nl:]
        layer = pl.program_id(0)
        g = None
        for l, l_ref in enumerate(l_refs):
            gl = l_ref[0].astype(F32)
            for i in range(1, NDEV):
                gl = gl + l_ref[i].astype(F32)
            g = gl if g is None else jnp.where(layer == l, gl, g)
        mn = ADAM_B1 * m_ref[...] + (1.0 - ADAM_B1) * g
        vn = ADAM_B2 * v_ref[...] + (1.0 - ADAM_B2) * (g * g)
        g_out[...] = g
        m_out[...] = mn
        v_out[...] = vn
        d_out[...] = -ADAM_LR * ((mn / c1) / (jnp.sqrt(vn / c2) + ADAM_EPS) + ADAM_WD * w_ref[...])

    def land_spec(l):
        return pl.BlockSpec((NDEV, tr, c), lambda ly, i: (0, jnp.where(ly == l, i, 0 if l > 0 else nr - 1), 0))

    spec = pl.BlockSpec((None, tr, c), lambda ly, i: (ly, i, 0))
    shp = jax.ShapeDtypeStruct((nl, r, c), F32)
    return _pallas_call(
        body, name=name, grid=(nl, nr), in_specs=[land_spec(l) for l in range(nl)] + [spec, spec, spec],
        out_specs=[spec] * 4, out_shape=[shp] * 4,
        compiler_params=_params(("arbitrary", "arbitrary"), (11 + 4 * nl) * tr * c * 4),
    )(*lands, w, m, v)


def _layer_fwd(h, p, tag, gather=(), finish=None):
    u = _rmsnorm_fwd(h, p["norm_mix"], f"norm_mix_{tag}")
    qkv_pre = _mm(u, p["w_qkv"], BF16, f"proj_qkv_{tag}")
    z = _mm(u, p["w_z"], BF16, f"proj_z_{tag}")
    ba = _mm(u, p["w_ba"], F32, f"proj_ba_{tag}")
    pool_in = _mm(u, p["w_pl"], F32, f"proj_pool_{tag}")
    gate = _mm(u, p["w_gate"], BF16, f"proj_gate_{tag}")
    qkv = _gdn_pre_fwd(qkv_pre, p["conv_qkv"], f"gdn_pre_{tag}")
    beta, g = _gates_fwd(ba, p["a_row"], p["dt_row"], f"gates_{tag}")
    (o, s_in, vnew, tinv, wsv), gathered = _gdn_fwd(qkv, beta, g, f"gdn_{tag}", gather)
    if finish is not None:
        p = {**p, **finish(gathered)}
    pooled = _pool_fwd(pool_in, f"pool_{tag}")
    pm = _mm_cols(pooled, p["w_pool"], F32, f"pool_mm_{tag}", _NN)
    y = _post_fwd(o, z, gate, pm, p["head_norm"], p["pool_scale"], f"post_{tag}")
    h1 = _mm(y, p["w_out"], F32, f"out_proj_{tag}", res=h)
    u2 = _rmsnorm_fwd(h1, p["norm_ffn"], f"norm_ffn_{tag}")
    hid = _mm_up(u2, p["w_up"], f"up_proj_{tag}")
    act = _mlp_act_fwd(hid, p["conv_ffn"], f"mlp_act_{tag}")
    h2 = _mm_blocks_red(act, p["w_down"], f"down_proj_{tag}", _NN, res=h1)
    saved = dict(h=h, u=u, qkv_pre=qkv_pre, z=z, ba=ba, gate=gate, qkv=qkv, beta=beta, g=g, o=o, s_in=s_in, vnew=vnew,
                 tinv=tinv, wsv=wsv, pooled=pooled, pm=pm, y=y, h1=h1, u2=u2, hid=hid, act=act)
    return h2, saved, gathered, p


def _layer_bwd(dh, dh_b, p, s, tag, scatter=()):
    gr = {}
    da = _mm_to_blocks(dh_b, p["w_down"], f"d_act_{tag}")
    gr["w_down"] = _mm_tn_blocks(s["act"], dh_b, f"dw_down_{tag}", True, False, BF16)
    dhg, dhv, dwg, dwv = _mlp_act_bwd(da, s["hid"], p["conv_ffn"], f"mlp_act_bwd_{tag}")
    gr["conv_ffn"] = jnp.concatenate([dwg, dwv], axis=0)
    w_up = p["w_up"]
    du2 = _mm_blocks_red(dhg, w_up, f"d_u2g_{tag}", _NT)
    du2 = _mm_blocks_red(dhv, w_up, f"d_u2v_{tag}", _NT, res=du2, b_off=4)
    gr["w_up"] = jnp.concatenate([_mm_tn_blocks(s["u2"], dhg, f"dw_upg_{tag}", False, True, BF16),
                                  _mm_tn_blocks(s["u2"], dhv, f"dw_upv_{tag}", False, True, BF16)], axis=0)
    dh1, dh1_b, gr["norm_ffn"] = _rmsnorm_bwd(s["h1"], du2, dh, p["norm_ffn"], f"norm_ffn_bwd_{tag}")
    dy = _mm(dh1_b, p["w_out"], F32, f"d_y_{tag}", dims=_NT)
    gr["w_out"] = _mm_tn(s["y"], dh1_b, f"dw_out_{tag}", BF16)
    do, dz, dgate, dpm, gr["head_norm"], gr["pool_scale"] = _post_bwd(
        dy, s["o"], s["z"], s["gate"], s["pm"], p["head_norm"], p["pool_scale"], f"post_bwd_{tag}")
    dpooled = _mm_cols(dpm, p["w_pool"], F32, f"d_pooled_{tag}", _NT)
    gr["w_pool"] = _mm_tn_cols(s["pooled"], dpm, 4, f"dw_pool_{tag}")
    dpool_in = _pool_bwd(dpooled, f"pool_bwd_{tag}")
    own = (gr["w_up"].astype(BF16), gr["w_down"].reshape(NDEV, -1, D).astype(BF16),
           gr["w_out"].reshape(NDEV, D // NDEV, D).astype(BF16))
    (dqkv, dbeta, dg), landed = _gdn_bwd(s["qkv"], s["beta"], s["g"], do, s["s_in"], s["vnew"], s["tinv"], s["wsv"],
                                         f"gdn_bwd_{tag}", own + tuple(scatter))
    dba, gr["a_log"], gr["dt_bias"] = _gates_bwd(s["ba"], p["a_row"], p["dt_row"], dbeta, dg, f"gates_bwd_{tag}")
    dqkv_pre, gr["conv_qkv"] = _gdn_pre_bwd(s["qkv_pre"], p["conv_qkv"], dqkv, f"gdn_pre_bwd_{tag}")
    du = None
    dws = []
    for nm, dseg, wseg in (("qkv", dqkv_pre, p["w_qkv"]), ("z", dz, p["w_z"]), ("ba", dba, p["w_ba"]),
                           ("pool", dpool_in, p["w_pl"]), ("gate", dgate, p["w_gate"])):
        du = _mm(dseg, wseg, F32, f"d_u_{nm}_{tag}", res=du, dims=_NT)
        dws.append(_mm_tn(s["u"], dseg, f"dw_{nm}_{tag}", BF16))
    gr["w_in"] = jnp.concatenate([dws[0], dws[1], dws[2][:, 0:H], dws[2][:, LANE:LANE + H], dws[3], dws[4]], axis=1)
    dh0, dh0_b, gr["norm_mix"] = _rmsnorm_bwd(s["h"], du, dh1, p["norm_mix"], f"norm_mix_bwd_{tag}")
    return dh0, dh0_b, gr, landed


def _pad_lanes(v8):
    return jnp.pad(v8.reshape(1, H), ((0, 0), (0, LANE - H)))


def _pack(parts, rows, lead=1):
    flat = jnp.concatenate([q.reshape(lead, -1) for q in parts], axis=1)
    flat = jnp.pad(flat, ((0, 0), (0, rows * LANE - flat.shape[1])))
    return flat.reshape((lead, rows, LANE) if lead > 1 else (rows, LANE))


def _unpack(packed, shapes, lead=1):
    flat = packed.reshape(lead, -1)
    out, off = [], 0
    for shp in shapes:
        n = 1
        for s_ in shp:
            n *= s_
        n //= lead
        out.append(flat[:, off:off + n].reshape(shp))
        off += n
    return out


SMALL_ROWS = 336
REPL_ROWS = 64


def kernel(x, meta_tokens, norm_mix, w_in, conv_qkv, a_log, dt_bias, head_norm, w_pool, pool_scale, w_out, norm_ffn, w_up, conv_ffn, w_down, norm_final, loss_target, m_meta_tokens, m_norm_mix, m_w_in, m_conv_qkv, m_a_log, m_dt_bias, m_head_norm, m_w_pool, m_pool_scale, m_w_out, m_norm_ffn, m_w_up, m_conv_ffn, m_w_down, m_norm_final, v_meta_tokens, v_norm_mix, v_w_in, v_conv_qkv, v_a_log, v_dt_bias, v_head_norm, v_w_pool, v_pool_scale, v_w_out, v_norm_ffn, v_w_up, v_conv_ffn, v_w_down, v_norm_final):
    seq = x.shape[1]
    t = ROW0 + seq
    assert t % TE == 0 and t % (MM_TILES * 16) == 0 and t % (GC * CH) == 0
    depth = w_in.shape[0]
    assert depth == 2
    cin = w_in.shape[2]

    def mixer_params(l, g_in, conv_q, conv_f, wp):
        wf = jnp.transpose(g_in, (1, 0, 2)).reshape(D, NDEV * cin)
        zpad = jnp.zeros((D, LANE - H), BF16)
        return dict(
            w_qkv=wf[:, 0:QKV], w_z=wf[:, QKV:QKV + D],
            w_ba=jnp.concatenate([wf[:, 4096:4104], zpad, wf[:, 4104:4112], zpad], axis=1),
            w_pl=wf[:, 4112:4624], w_gate=wf[:, 4624:6672], conv_qkv=conv_q, conv_ffn=conv_f, w_pool=wp,
            norm_mix=norm_mix[l].reshape(1, D), norm_ffn=norm_ffn[l].reshape(1, D),
            pool_scale=pool_scale[l].reshape(1, D), head_norm=head_norm[l].reshape(1, DH),
            a_row=_pad_lanes(a_log[l]), dt_row=_pad_lanes(dt_bias[l]))

    def late_params(g_up, g_out, g_down):
        return dict(w_out=g_out.reshape(D, D), w_up=g_up, w_down=g_down.reshape(4, FB, D))

    small_shapes = [conv_qkv.shape, conv_ffn.shape, w_pool.shape, meta_tokens.shape]
    small = _pack([conv_qkv, conv_ffn, w_pool, meta_tokens], SMALL_ROWS)
    w_in_b, w_up_b, w_out_b, w_down_b = w_in.astype(BF16), w_up.astype(BF16), w_out.astype(BF16), w_down.astype(BF16)
    g_in0, g_small = _gather([w_in_b[0], small], "gather_first")
    s_cq, s_cf, s_wp, s_mt = _unpack(g_small, [(NDEV,) + shp for shp in small_shapes], lead=NDEV)
    conv_qkv_full = jnp.transpose(s_cq, (1, 2, 0, 3)).reshape(depth, 4, QKV)
    conv_ffn_blk = jnp.transpose(s_cf, (1, 0, 2, 3))
    w_pool_full = jnp.transpose(s_wp, (1, 2, 3, 0, 4)).reshape(depth, 4, DH, 2 * DH).astype(BF16)
    meta_full = jnp.transpose(s_mt, (1, 0, 2)).reshape(N_META, D)

    h = jnp.concatenate([jnp.zeros((LEAD, D), F32), meta_full, x[0]], axis=0)
    p0 = mixer_params(0, g_in0, conv_qkv_full[0], conv_ffn_blk[0], w_pool_full[0])
    h, sv0, rest, p0 = _layer_fwd(
        h, p0, "l0", (w_up_b[0], w_out_b[0], w_down_b[0], w_in_b[1], w_up_b[1], w_out_b[1], w_down_b[1]),
        lambda got: late_params(*got[:3]))
    p1 = {**mixer_params(1, rest[3], conv_qkv_full[1], conv_ffn_blk[1], w_pool_full[1]), **late_params(*rest[4:])}
    h, sv1, _, _ = _layer_fwd(h, p1, "l1")
    layers = [p0, p1]
    saved = [sv0, sv1]
    target = jnp.concatenate([jnp.zeros((ROW0, D), F32), loss_target[0]], axis=0)
    dh, dh_b, d_norm_final, loss_row = _loss_bwd(h, target, norm_final.reshape(1, D), "loss")

    def w_in_blocks(gr):
        return jnp.transpose(gr["w_in"].reshape(D, NDEV, cin), (1, 0, 2)).astype(BF16)

    grads = [None] * depth
    dh, dh_b, grads[1], (l_up1, l_down1, l_out1) = _layer_bwd(dh, dh_b, layers[1], saved[1], "l1")
    dh, dh_b, grads[0], (l_up0, l_down0, l_out0, l_in1) = _layer_bwd(dh, dh_b, layers[0], saved[0], "l0", (w_in_blocks(grads[1]),))
    grad_x = dh[ROW0:].reshape(1, seq, D)
    d_meta = dh[LEAD:ROW0]

    stk = lambda name: jnp.stack([grads[l][name] for l in range(depth)], axis=0)
    cq = conv_qkv.shape[2]
    pw = w_pool.shape[3]
    s_cq = jnp.transpose(stk("conv_qkv").reshape(depth, 4, NDEV, cq), (2, 0, 1, 3))
    s_cf = jnp.transpose(stk("conv_ffn"), (1, 0, 2, 3))
    s_wp = jnp.transpose(stk("w_pool").reshape(depth, 4, DH, NDEV, pw), (3, 0, 1, 2, 4))
    s_mt = jnp.transpose(d_meta.reshape(N_META, NDEV, D // NDEV), (1, 0, 2))
    b_small = _pack([s_cq, s_cf, s_wp, s_mt], SMALL_ROWS, lead=NDEV)
    l_in0, l_small = _scatter([w_in_blocks(grads[0]), b_small], "exchange_last")

    r_in = _adamw((l_in0, l_in1), w_in, m_w_in, v_w_in, "adamw_w_in")
    r_up = _adamw((l_up0, l_up1), w_up, m_w_up, v_w_up, "adamw_w_up")
    r_out = _adamw((l_out0, l_out1), w_out, m_w_out, v_w_out, "adamw_w_out")
    r_down = _adamw((l_down0, l_down1), w_down, m_w_down, v_w_down, "adamw_w_down")
    r_small = _adamw((l_small,), small[None], _pack([m_conv_qkv, m_conv_ffn, m_w_pool, m_meta_tokens], SMALL_ROWS)[None],
                     _pack([v_conv_qkv, v_conv_ffn, v_w_pool, v_meta_tokens], SMALL_ROWS)[None], "adamw_small")
    r_small = [_unpack(o_[0], small_shapes) for o_ in r_small]

    repl_shapes = [norm_mix.shape, a_log.shape, dt_bias.shape, head_norm.shape, pool_scale.shape, norm_ffn.shape,
                   norm_final.shape, (1,)]
    rp = lambda name, n: jnp.stack([grads[l][name][0, :n] for l in range(depth)], axis=0)
    part = _pack([rp("norm_mix", D), rp("a_log", H), rp("dt_bias", H), rp("head_norm", DH), rp("pool_scale", D),
                  rp("norm_ffn", D), d_norm_final[0], loss_row[0, 0:1]], REPL_ROWS)
    (l_repl,) = _gather([part], "gather_replicated")
    zero1 = jnp.zeros((1,), F32)
    r_repl = _adamw(
        (l_repl,), _pack([norm_mix, a_log, dt_bias, head_norm, pool_scale, norm_ffn, norm_final, zero1], REPL_ROWS)[None],
        _pack([m_norm_mix, m_a_log, m_dt_bias, m_head_norm, m_pool_scale, m_norm_ffn, m_norm_final, zero1], REPL_ROWS)[None],
        _pack([v_norm_mix, v_a_log, v_dt_bias, v_head_norm, v_pool_scale, v_norm_ffn, v_norm_final, zero1], REPL_ROWS)[None],
        "adamw_replicated")
    r_repl = [_unpack(o_[0], repl_shapes) for o_ in r_repl]
    loss = r_repl[0][7].reshape(())

    def leaf(kind):
        sm, rr = r_small[kind], r_repl[kind]
        return [sm[3], rr[0], r_in[kind], sm[0], rr[1], rr[2], rr[3], sm[2], rr[4], r_out[kind], rr[5], r_up[kind],
                sm[1], r_down[kind], rr[6]]

    return (loss, grad_x, *leaf(0), *leaf(1), *leaf(2), *leaf(3))
```

```python
import jax
import jax.numpy as jnp
from jax import lax
from jax.experimental import pallas as pl
from jax.experimental.pallas import tpu as pltpu

F32 = jnp.float32
BF16 = jnp.bfloat16
MESH = pl.DeviceIdType.MESH

D = 1024
H = 8
DH = 128
CH = 64
N_META = 16
LEAD = 48
ROW0 = LEAD + N_META
QKV = 3 * D
POOL_W = 512
POOL_WINDOWS = (2, 4, 8, 16)
FB = 704
NDEV = 8
EPS = 1e-6
MM_TILES = 12
TE = 192
LANE = 128
SUB = 8
VMEM_CAP = 56 << 20

ADAM_LR, ADAM_B1, ADAM_B2, ADAM_EPS, ADAM_WD, ADAM_STEP = 0.001, 0.9, 0.999, 1e-08, 0.01, 10

_NN = (((1,), (0,)), ((), ()))
_NT = (((1,), (1,)), ((), ()))
_TN = (((0,), (0,)), ((), ()))


def _dot(a, b, dims=_NN, precision=None):
    return lax.dot_general(a, b, dims, precision=precision, preferred_element_type=F32)


def _bdot(a, b, dims=_NN):
    return _dot(a.astype(BF16), b.astype(BF16), dims)


def _nbytes(shape, dtype):
    n = 1
    for s in shape:
        n *= s
    return n * jnp.dtype(dtype).itemsize


def _params(sem, block_bytes):
    limit = min(VMEM_CAP, 2 * block_bytes + (20 << 20))
    return pltpu.CompilerParams(dimension_semantics=sem, vmem_limit_bytes=limit)


PIN_BYTES = 12 << 20


def _pallas_call(body, *, out_shape, **kw):
    call = pl.pallas_call

    def big(s):
        return len(s.shape) >= 2 and s.shape[-1] >= D and _nbytes(s.shape, s.dtype) >= PIN_BYTES

    pinned = jax.tree.map(lambda s: pltpu.HBM(s.shape, s.dtype) if big(s) else s, out_shape)

    def run(*args):
        return call(body, out_shape=pinned, **kw)(
            *[pltpu.with_memory_space_constraint(a, pltpu.HBM) if big(a) else a for a in args])

    return run


def _sigmoid(x):
    return 1.0 / (1.0 + jnp.exp(-x))


def _col_tile(n):
    for t in (1024, 512, 256, 128):
        if n % t == 0:
            return t
    return n


def _matmul(a, b, *, dims, grid, a_spec, b_spec, o_spec, out_shape, name, red_axis=None, res=None):
    o_blk = tuple(s for s in o_spec.block_shape if s is not None)
    via_scratch = red_axis is not None and out_shape.dtype != F32

    def body(*refs):
        if res is None:
            a_ref, b_ref, o_ref = refs[:3]
        else:
            a_ref, b_ref, r_ref, o_ref = refs[:4]
        part = _dot(a_ref[...], b_ref[...], dims)
        if red_axis is None:
            if res is not None:
                part = part + r_ref[...]
            o_ref[...] = part.astype(o_ref.dtype)
        else:
            acc = refs[-1] if via_scratch else o_ref
            r = pl.program_id(red_axis)

            @pl.when(r == 0)
            def _():
                acc[...] = part + r_ref[...] if res is not None else part

            @pl.when(r > 0)
            def _():
                acc[...] += part

            if via_scratch:
                @pl.when(r == grid[red_axis] - 1)
                def _():
                    o_ref[...] = acc[...].astype(o_ref.dtype)

    def blk(spec, arr):
        return _nbytes([s for s in spec.block_shape if s is not None], arr.dtype)

    ins = [a, b] + ([res] if res is not None else [])
    specs = [a_spec, b_spec] + ([o_spec] if res is not None else [])
    nb = blk(a_spec, a) + blk(b_spec, b) + 2 * _nbytes(o_blk, F32)
    sem = tuple("arbitrary" if i == red_axis else "parallel" for i in range(len(grid)))
    return _pallas_call(
        body, name=name, grid=grid, in_specs=specs, out_specs=o_spec, out_shape=out_shape,
        scratch_shapes=[pltpu.VMEM(o_blk, F32)] if via_scratch else [],
        compiler_params=_params(sem, nb),
    )(*ins)


def _row_tiles(m, row_bytes, fixed_bytes, temp_row_bytes=0):
    for nt in (MM_TILES // 2, MM_TILES):
        tm = m // nt
        if 2 * (row_bytes * tm + fixed_bytes) + temp_row_bytes * tm <= VMEM_CAP - (10 << 20):
            return nt
    return MM_TILES


def _mm(a, b, out_dtype, name, res=None, dims=_NN):
    m, k = a.shape
    n = b.shape[1] if dims == _NN else b.shape[0]
    tn = _col_tile(n)
    nt = _row_tiles(m, 2 * k + tn * (jnp.dtype(out_dtype).itemsize + (4 if res is not None else 0)), 2 * k * tn, 4 * tn)
    tm = m // nt
    if dims == _NN:
        b_spec = pl.BlockSpec((k, tn), lambda j, i: (0, j))
    else:
        b_spec = pl.BlockSpec((tn, k), lambda j, i: (j, 0))
    return _matmul(
        a, b, dims=dims, grid=(n // tn, nt), a_spec=pl.BlockSpec((tm, k), lambda j, i: (i, 0)), b_spec=b_spec,
        o_spec=pl.BlockSpec((tm, tn), lambda j, i: (i, j)), out_shape=jax.ShapeDtypeStruct((m, n), out_dtype),
        name=name, res=res)


def _mm_du2(dhg, dhv, w_up, name):
    g, t, k = dhg.shape
    n = w_up.shape[1]
    nt = _row_tiles(t, 4 * k + 4 * n, 4 * k * n, 4 * n)
    tm = t // nt

    def body(ag_ref, av_ref, bg_ref, bv_ref, o_ref):
        part = _dot(ag_ref[...], bg_ref[...], _NT) + _dot(av_ref[...], bv_ref[...], _NT)
        r = pl.program_id(1)

        @pl.when(r == 0)
        def _():
            o_ref[...] = part

        @pl.when(r > 0)
        def _():
            o_ref[...] += part

    a_spec = pl.BlockSpec((None, tm, k), lambda i, g_: (g_, i, 0))
    return _pallas_call(
        body, name=name, grid=(nt, g),
        in_specs=[a_spec, a_spec, pl.BlockSpec((None, n, k), lambda i, g_: (g_, 0, 0)),
                  pl.BlockSpec((None, n, k), lambda i, g_: (g_ + g, 0, 0))],
        out_specs=pl.BlockSpec((tm, n), lambda i, g_: (i, 0)), out_shape=jax.ShapeDtypeStruct((t, n), F32),
        compiler_params=_params(("parallel", "arbitrary"), 4 * tm * k + 4 * n * k + 8 * tm * n),
    )(dhg, dhv, w_up, w_up)


def _mm_sum_nt(pairs, name):
    m, n = pairs[0][0].shape[0], pairs[0][1].shape[0]
    tm, tn = m // MM_TILES, min(512, n)
    np_ = len(pairs)

    def body(*refs):
        acc = _dot(refs[0][...], refs[1][...], _NT)
        for k in range(1, np_):
            acc = acc + _dot(refs[2 * k][...], refs[2 * k + 1][...], _NT)
        refs[-1][...] = acc

    specs, ins, nb = [], [], 2 * tm * tn * 4
    for a, b in pairs:
        k = a.shape[1]
        specs += [pl.BlockSpec((tm, k), lambda j, i: (i, 0)), pl.BlockSpec((tn, k), lambda j, i: (j, 0))]
        ins += [a, b]
        nb += 2 * k * (tm + tn)
    return _pallas_call(
        body, name=name, grid=(n // tn, MM_TILES), in_specs=specs, out_specs=pl.BlockSpec((tm, tn), lambda j, i: (i, j)),
        out_shape=jax.ShapeDtypeStruct((m, n), F32), compiler_params=_params(("parallel", "parallel"), nb),
    )(*ins)


def _mm_tn(a, g, name, out_dtype=F32):
    m, k = a.shape
    n = g.shape[1]
    tn = _col_tile(n)
    nt = _row_tiles(m, 2 * k + 2 * tn, 4 * k * tn)
    tm = m // nt
    return _matmul(
        a, g, dims=_TN, grid=(n // tn, nt), red_axis=1, a_spec=pl.BlockSpec((tm, k), lambda j, i: (i, 0)),
        b_spec=pl.BlockSpec((tm, tn), lambda j, i: (i, j)), o_spec=pl.BlockSpec((k, tn), lambda j, i: (0, j)),
        out_shape=jax.ShapeDtypeStruct((k, n), out_dtype), name=name)


def _mm_up(u, w_up, name):
    t = u.shape[0]
    g = w_up.shape[0]
    nt = _row_tiles(t, 2 * D + 2 * FB, 2 * D * FB, 4 * FB)
    tm = t // nt
    return _matmul(
        u, w_up, dims=_NN, grid=(g, nt), a_spec=pl.BlockSpec((tm, D), lambda g_, i: (i, 0)),
        b_spec=pl.BlockSpec((None, D, FB), lambda g_, i: (g_, 0, 0)),
        o_spec=pl.BlockSpec((None, tm, FB), lambda g_, i: (g_, i, 0)),
        out_shape=jax.ShapeDtypeStruct((g, t, FB), BF16), name=name)


def _mm_blocks_red(a, b, name, dims, res=None):
    g, t, k = a.shape
    n = b.shape[2] if dims == _NN else b.shape[1]
    nt = _row_tiles(t, 2 * k + n * (8 if res is not None else 4), 2 * k * n, 4 * n)
    tm = t // nt
    return _matmul(
        a, b, dims=dims, grid=(nt, g), red_axis=1, a_spec=pl.BlockSpec((None, tm, k), lambda i, g_: (g_, i, 0)),
        b_spec=pl.BlockSpec((None,) + b.shape[1:], lambda i, g_: (g_, 0, 0)),
        o_spec=pl.BlockSpec((tm, n), lambda i, g_: (i, 0)), out_shape=jax.ShapeDtypeStruct((t, n), F32),
        name=name, res=res)


def _mm_to_blocks(a, b, name):
    t, k = a.shape
    g, n, _ = b.shape
    nt = _row_tiles(t, 2 * k + 2 * n, 2 * k * n, 4 * n)
    tm = t // nt
    return _matmul(
        a, b, dims=_NT, grid=(g, nt), a_spec=pl.BlockSpec((tm, k), lambda g_, i: (i, 0)),
        b_spec=pl.BlockSpec((None, n, k), lambda g_, i: (g_, 0, 0)),
        o_spec=pl.BlockSpec((None, tm, n), lambda g_, i: (g_, i, 0)),
        out_shape=jax.ShapeDtypeStruct((g, t, n), BF16), name=name)


def _mm_tn_blocks(a, g, name, a_blocked, g_blocked, out_dtype=F32):
    nb = a.shape[0] if a_blocked else g.shape[0]
    t = a.shape[-2]
    k, n = a.shape[-1], g.shape[-1]
    nt = _row_tiles(t, 2 * k + 2 * n, 4 * k * n)
    tm = t // nt
    a_spec = (pl.BlockSpec((None, tm, k), lambda g_, i: (g_, i, 0)) if a_blocked
              else pl.BlockSpec((tm, k), lambda g_, i: (i, 0)))
    g_spec = (pl.BlockSpec((None, tm, n), lambda g_, i: (g_, i, 0)) if g_blocked
              else pl.BlockSpec((tm, n), lambda g_, i: (i, 0)))
    return _matmul(
        a, g, dims=_TN, grid=(nb, nt), red_axis=1, a_spec=a_spec, b_spec=g_spec,
        o_spec=pl.BlockSpec((None, k, n), lambda g_, i: (g_, 0, 0)),
        out_shape=jax.ShapeDtypeStruct((nb, k, n), out_dtype), name=name)


def _mm_cols(a, b, out_dtype, name, dims):
    t = a.shape[0]
    g = b.shape[0]
    ka = a.shape[1] // g
    n = b.shape[2] if dims == _NN else b.shape[1]
    tm = t // MM_TILES
    return _matmul(
        a, b, dims=dims, grid=(g, MM_TILES), a_spec=pl.BlockSpec((tm, ka), lambda g_, i: (i, g_)),
        b_spec=pl.BlockSpec((None,) + b.shape[1:], lambda g_, i: (g_, 0, 0)),
        o_spec=pl.BlockSpec((tm, n), lambda g_, i: (i, g_)), out_shape=jax.ShapeDtypeStruct((t, g * n), out_dtype),
        name=name)


def _mm_tn_cols(a, g, nblk, name):
    t = a.shape[0]
    ka, n = a.shape[1] // nblk, g.shape[1] // nblk
    tm = t // MM_TILES
    return _matmul(
        a, g, dims=_TN, grid=(nblk, MM_TILES), red_axis=1, a_spec=pl.BlockSpec((tm, ka), lambda g_, i: (i, g_)),
        b_spec=pl.BlockSpec((tm, n), lambda g_, i: (i, g_)), o_spec=pl.BlockSpec((None, ka, n), lambda g_, i: (g_, 0, 0)),
        out_shape=jax.ShapeDtypeStruct((nblk, ka, n), F32), name=name)


def _rows(cols, n=None):
    if n is None:
        return pl.BlockSpec((TE, cols), lambda i: (i, 0))
    return pl.BlockSpec((TE, cols), lambda i: (n - 1 - i, 0))


def _whole(shape):
    return pl.BlockSpec(shape, lambda *_: (0,) * len(shape))


def _row_ids(i, rows=TE):
    return i * rows + lax.broadcasted_iota(jnp.int32, (rows, 1), 0)


def _rmsnorm_fwd(h, gain, name):
    t = h.shape[0]

    def body(h_ref, g_ref, u_ref):
        x = h_ref[...]
        r = lax.rsqrt(jnp.mean(x * x, axis=-1, keepdims=True) + EPS)
        u_ref[...] = (x * r * g_ref[...]).astype(BF16)

    return _pallas_call(
        body, name=name, grid=(t // TE,), in_specs=[_rows(D), _whole((1, D))], out_specs=_rows(D),
        out_shape=jax.ShapeDtypeStruct((t, D), BF16), compiler_params=_params(("parallel",), 3 * TE * D * 4),
    )(h, gain)


def _rmsnorm_bwd(x, du, dres, gain, name):
    t = x.shape[0]

    def body(x_ref, du_ref, dr_ref, g_ref, dx_ref, dxb_ref, dg_ref):
        i = pl.program_id(0)
        xv = x_ref[...]
        r = lax.rsqrt(jnp.mean(xv * xv, axis=-1, keepdims=True) + EPS)
        gdy = du_ref[...] * g_ref[...]
        dx = dr_ref[...] + r * gdy - xv * (r * r * r) * jnp.mean(xv * gdy, axis=-1, keepdims=True)
        dx = jnp.where(_row_ids(i) >= LEAD, dx, 0.0)
        dx_ref[...] = dx
        dxb_ref[...] = dx.astype(BF16)
        part = jnp.sum(du_ref[...] * xv * r, axis=0, keepdims=True)

        @pl.when(i == 0)
        def _():
            dg_ref[...] = part

        @pl.when(i > 0)
        def _():
            dg_ref[...] += part

    return _pallas_call(
        body, name=name, grid=(t // TE,), in_specs=[_rows(D), _rows(D), _rows(D), _whole((1, D))],
        out_specs=[_rows(D), _rows(D), _whole((1, D))],
        out_shape=[jax.ShapeDtypeStruct((t, D), F32), jax.ShapeDtypeStruct((t, D), BF16),
                   jax.ShapeDtypeStruct((1, D), F32)],
        compiler_params=_params(("arbitrary",), 5 * TE * D * 4),
    )(x, du, dres, gain)


def _loss_bwd(h, target, gain, name):
    t = h.shape[0]

    def body(h_ref, t_ref, g_ref, dx_ref, dxb_ref, dg_ref, loss_ref):
        i = pl.program_id(0)
        xv = h_ref[...]
        gain_v = g_ref[...]
        r = lax.rsqrt(jnp.mean(xv * xv, axis=-1, keepdims=True) + EPS)
        real = _row_ids(i) >= ROW0
        err = jnp.where(real, xv * r * gain_v - t_ref[...], 0.0)
        dy = err * (1.0 / D)
        gdy = dy * gain_v
        dx = r * gdy - xv * (r * r * r) * jnp.mean(xv * gdy, axis=-1, keepdims=True)
        dx_ref[...] = dx
        dxb_ref[...] = dx.astype(BF16)
        dgp = jnp.sum(dy * xv * r, axis=0, keepdims=True)
        lp = 0.5 * jnp.sum(jnp.mean(err * err, axis=-1, keepdims=True), axis=0, keepdims=True)

        @pl.when(i == 0)
        def _():
            dg_ref[...] = dgp
            loss_ref[...] = jnp.broadcast_to(lp, (1, LANE))

        @pl.when(i > 0)
        def _():
            dg_ref[...] += dgp
            loss_ref[...] += jnp.broadcast_to(lp, (1, LANE))

    return _pallas_call(
        body, name=name, grid=(t // TE,), in_specs=[_rows(D), _rows(D), _whole((1, D))],
        out_specs=[_rows(D), _rows(D), _whole((1, D)), _whole((1, LANE))],
        out_shape=[jax.ShapeDtypeStruct((t, D), F32), jax.ShapeDtypeStruct((t, D), BF16),
                   jax.ShapeDtypeStruct((1, D), F32), jax.ShapeDtypeStruct((1, LANE), F32)],
        compiler_params=_params(("arbitrary",), 4 * TE * D * 4),
    )(h, target, gain)


def _seq_scratch(cols):
    return pltpu.VMEM((-(-cols // LANE), TE + SUB, LANE), F32)


def _taps(scr, c, wd, first, n, k):
    return [scr[c, first - (k - 1) + j:first - (k - 1) + j + n, 0:wd] for j in range(k)]


def _stage_history(scr, i):
    @pl.when(i == 0)
    def _():
        scr[...] = jnp.zeros(scr.shape, F32)

    @pl.when(i > 0)
    def _():
        scr[:, 0:SUB, :] = scr[:, TE:TE + SUB, :]


def _stage_future(scr, i):
    @pl.when(i == 0)
    def _():
        scr[...] = jnp.zeros(scr.shape, F32)

    @pl.when(i > 0)
    def _():
        scr[:, TE:TE + SUB, :] = scr[:, 0:SUB, :]


def _conv(tp, w):
    out = w[0:1] * tp[0]
    for j in range(1, len(tp)):
        out = out + w[j:j + 1] * tp[j]
    return out


def _conv_t(ds, c, wd, w):
    k = w.shape[0]
    out = w[k - 1:k] * ds[c, 0:TE, 0:wd]
    for j in range(k - 1):
        out = out + w[j:j + 1] * ds[c, k - 1 - j:k - 1 - j + TE, 0:wd]
    return out


def _gdn_pre_fwd(x, w, name):
    t = x.shape[0]

    def body(x_ref, w_ref, o_ref, xs):
        _stage_history(xs, pl.program_id(0))
        for hh in range(3 * H):
            sl = slice(hh * DH, (hh + 1) * DH)
            xs[hh, SUB:SUB + TE, :] = x_ref[:, sl].astype(F32)
            cv = _conv(_taps(xs, hh, DH, SUB, TE, 4), w_ref[:, sl])
            s = cv * _sigmoid(cv)
            if hh < 2 * H:
                s = s * lax.rsqrt(jnp.sum(s * s, axis=-1, keepdims=True) + EPS)
                if hh < H:
                    s = s * (DH ** -0.5)
            o_ref[:, sl] = s

    return _pallas_call(
        body, name=name, grid=(t // TE,), in_specs=[_rows(QKV), _whole((4, QKV))], out_specs=_rows(QKV),
        out_shape=jax.ShapeDtypeStruct((t, QKV), F32), scratch_shapes=[_seq_scratch(QKV)],
        compiler_params=_params(("arbitrary",), TE * QKV * 8),
    )(x, w)


def _gdn_pre_bwd(x, w, dqkv, name):
    t = x.shape[0]
    n = t // TE
    hb = TE // 16

    def body(x_ref, xp_ref, w_ref, d_ref, dx_ref, dw_ref, xs, ds):
        i = pl.program_id(0)

        @pl.when(i == 0)
        def _():
            dw_ref[...] = jnp.zeros_like(dw_ref)

        _stage_future(ds, i)
        for hh in range(3 * H):
            sl = slice(hh * DH, (hh + 1) * DH)
            xs[hh, 0:SUB, :] = jnp.where(i == n - 1, 0.0, xp_ref[SUB:2 * SUB, sl].astype(F32))
            xs[hh, SUB:SUB + TE, :] = x_ref[:, sl].astype(F32)
            wv = w_ref[:, sl]
            tp = _taps(xs, hh, DH, SUB, TE, 4)
            cv = _conv(tp, wv)
            sg = _sigmoid(cv)
            s = cv * sg
            dsv = d_ref[:, sl]
            if hh < 2 * H:
                if hh < H:
                    dsv = dsv * (DH ** -0.5)
                r = lax.rsqrt(jnp.sum(s * s, axis=-1, keepdims=True) + EPS)
                dsv = r * dsv - s * (r * r * r) * jnp.sum(s * dsv, axis=-1, keepdims=True)
            dcv = dsv * (sg * (1.0 + cv * (1.0 - sg)))
            ds[hh, 0:TE, :] = dcv
            dx_ref[:, sl] = _conv_t(ds, hh, DH, wv).astype(BF16)
            dw_ref[:, sl] += jnp.concatenate([jnp.sum(tp[j] * dcv, axis=0, keepdims=True) for j in range(4)], axis=0)

    return _pallas_call(
        body, name=name, grid=(n,),
        in_specs=[_rows(QKV, n), pl.BlockSpec((16, QKV), lambda i: (jnp.maximum((n - 1 - i) * hb - 1, 0), 0)),
                  _whole((4, QKV)), _rows(QKV, n)],
        out_specs=[_rows(QKV, n), _whole((4, QKV))],
        out_shape=[jax.ShapeDtypeStruct((t, QKV), BF16), jax.ShapeDtypeStruct((4, QKV), F32)],
        scratch_shapes=[_seq_scratch(QKV), _seq_scratch(QKV)],
        compiler_params=_params(("arbitrary",), TE * QKV * 14),
    )(x, x, w, dqkv)


def _softplus(x):
    return jnp.maximum(x, 0.0) + jnp.log(1.0 + jnp.exp(-jnp.abs(x)))


def _gates_fwd(ba, a_row, dt_row, name):
    t = ba.shape[0]

    def body(ba_ref, a_ref, dt_ref, b_out, g_out):
        real = _row_ids(pl.program_id(0)) >= LEAD
        b_out[...] = jnp.where(real, _sigmoid(ba_ref[:, 0:LANE]), 0.0)
        g = -jnp.exp(a_ref[...]) * _softplus(ba_ref[:, LANE:2 * LANE] + dt_ref[...])
        g_out[...] = jnp.where(real, g, 0.0)

    return _pallas_call(
        body, name=name, grid=(t // TE,), in_specs=[_rows(2 * LANE), _whole((1, LANE)), _whole((1, LANE))],
        out_specs=[_rows(LANE), _rows(LANE)],
        out_shape=[jax.ShapeDtypeStruct((t, LANE), F32), jax.ShapeDtypeStruct((t, LANE), F32)],
        compiler_params=_params(("parallel",), TE * LANE * 16),
    )(ba, a_row, dt_row)


def _gates_bwd(ba, a_row, dt_row, dbeta, dg, name):
    t = ba.shape[0]

    def body(ba_ref, a_ref, dt_ref, db_ref, dg_ref, dba_ref, da_out, ddt_out):
        i = pl.program_id(0)
        real = _row_ids(i) >= LEAD
        beta = _sigmoid(ba_ref[:, 0:LANE])
        draw_b = jnp.where(real, db_ref[...] * beta * (1.0 - beta), 0.0)
        pre = ba_ref[:, LANE:2 * LANE] + dt_ref[...]
        neg_a = -jnp.exp(a_ref[...])
        dgv = jnp.where(real, dg_ref[...], 0.0)
        draw_a = dgv * neg_a * _sigmoid(pre)
        dba_ref[:, 0:LANE] = draw_b.astype(BF16)
        dba_ref[:, LANE:2 * LANE] = draw_a.astype(BF16)
        dal = jnp.sum(dgv * neg_a * _softplus(pre), axis=0, keepdims=True)
        ddt = jnp.sum(draw_a, axis=0, keepdims=True)

        @pl.when(i == 0)
        def _():
            da_out[...] = dal
            ddt_out[...] = ddt

        @pl.when(i > 0)
        def _():
            da_out[...] += dal
            ddt_out[...] += ddt

    return _pallas_call(
        body, name=name, grid=(t // TE,),
        in_specs=[_rows(2 * LANE), _whole((1, LANE)), _whole((1, LANE)), _rows(LANE), _rows(LANE)],
        out_specs=[_rows(2 * LANE), _whole((1, LANE)), _whole((1, LANE))],
        out_shape=[jax.ShapeDtypeStruct((t, 2 * LANE), BF16), jax.ShapeDtypeStruct((1, LANE), F32),
                   jax.ShapeDtypeStruct((1, LANE), F32)],
        compiler_params=_params(("arbitrary",), TE * LANE * 24),
    )(ba, a_row, dt_row, dbeta, dg)


_OFFSETS = [(dx, dy, dc) for dx in (0, 1) for dy in (0, 1) for dc in (0, 1)][1:]
NPEER = len(_OFFSETS)
ANY_SPEC = pl.BlockSpec(memory_space=pl.ANY)


def _place():
    return lax.axis_index("x"), lax.axis_index("y"), lax.axis_index("c")


def _index(p):
    return 4 * p[0] + 2 * p[1] + p[2]


def _comm_scratch(n):
    return [pltpu.SemaphoreType.DMA((n * NPEER,)), pltpu.SemaphoreType.DMA((n * NPEER,)), pltpu.SemaphoreType.DMA((n,))]


def _scatter_copies(ins, outs, send, recv):
    me = _place()
    mi = _index(me)
    res = []
    for j, d in enumerate(_OFFSETS):
        peer = tuple(1 - v if bit else v for v, bit in zip(me, d))
        pi = _index(peer)
        for k in range(len(ins)):
            sem = k * NPEER + j
            mine = pltpu.make_async_remote_copy(src_ref=ins[k].at[pi], dst_ref=outs[k].at[mi], send_sem=send.at[sem],
                                                recv_sem=recv.at[sem], device_id=peer, device_id_type=MESH)
            theirs = pltpu.make_async_remote_copy(src_ref=ins[k].at[pi], dst_ref=outs[k].at[pi], send_sem=send.at[sem],
                                                  recv_sem=recv.at[sem], device_id=peer, device_id_type=MESH)
            res.append((mine, theirs))
    return res


def _scatter_own(ins, outs, loc):
    mi = _index(_place())
    return [pltpu.make_async_copy(ins[k].at[mi], outs[k].at[mi], loc.at[k]) for k in range(len(ins))]


def _scatter_start(ins, outs, send, recv, loc):
    for cp in _scatter_own(ins, outs, loc):
        cp.start()
    for mine, _ in _scatter_copies(ins, outs, send, recv):
        mine.start()


def _scatter_wait(ins, outs, send, recv, loc):
    cps = _scatter_copies(ins, outs, send, recv)
    for _, theirs in cps:
        theirs.wait_recv()
    for mine, _ in cps:
        mine.wait_send()
    for cp in _scatter_own(ins, outs, loc):
        cp.wait()


def _gather_parts(ins, outs, send, recv):
    x, y, c = _place()
    chips = [(1 - x, y), (x, 1 - y), (1 - x, 1 - y)]

    def cp(k, slot, src, block, to):
        return pltpu.make_async_remote_copy(src_ref=src, dst_ref=outs[k].at[_index(block)], send_sem=send.at[k * NPEER + slot],
                                            recv_sem=recv.at[k * NPEER + slot], device_id=to, device_id_type=MESH)

    return (x, y, c), (x, y, 1 - c), chips, cp


def _gather_start(ins, outs, send, recv, loc):
    me, sib, chips, cp = _gather_parts(ins, outs, send, recv)
    for k in range(len(ins)):
        pltpu.make_async_copy(ins[k], outs[k].at[_index(me)], loc.at[k]).start()
        cp(k, 0, ins[k], me, sib).start()
        for j, chip in enumerate(chips):
            cp(k, 1 + j, ins[k], me, (*chip, me[2])).start()


def _gather_forward(ins, outs, send, recv, loc):
    me, sib, chips, cp = _gather_parts(ins, outs, send, recv)
    for j, chip in enumerate(chips):
        blk = (*chip, me[2])
        for k in range(len(ins)):
            cp(k, 1 + j, ins[k], blk, me).wait_recv()
            cp(k, 4 + j, outs[k].at[_index(blk)], blk, sib).start()


def _gather_finish(ins, outs, send, recv, loc):
    me, sib, chips, cp = _gather_parts(ins, outs, send, recv)
    for k in range(len(ins)):
        cp(k, 0, ins[k], sib, me).wait_recv()
        for j, chip in enumerate(chips):
            cp(k, 4 + j, ins[k], (*chip, sib[2]), me).wait_recv()
        cp(k, 0, ins[k], me, sib).wait_send()
        for j, chip in enumerate(chips):
            cp(k, 1 + j, ins[k], me, (*chip, me[2])).wait_send()
            cp(k, 4 + j, outs[k].at[_index((*chip, me[2]))], (*chip, me[2]), sib).wait_send()
        pltpu.make_async_copy(ins[k], outs[k].at[_index(me)], loc.at[k]).wait()


def _gathered_shapes(arrs):
    return [jax.ShapeDtypeStruct((NDEV,) + a.shape, a.dtype) for a in arrs]


def _gather(arrs, name):
    n = len(arrs)

    def body(*refs):
        ins, outs, sems = refs[:n], refs[n:2 * n], refs[2 * n:]
        _gather_start(ins, outs, *sems)
        _gather_forward(ins, outs, *sems)
        _gather_finish(ins, outs, *sems)

    return _pallas_call(body, name=name, in_specs=[ANY_SPEC] * n, out_specs=[ANY_SPEC] * n,
                          out_shape=_gathered_shapes(arrs), scratch_shapes=_comm_scratch(n))(*arrs)


def _scatter(arrs, name):
    n = len(arrs)

    def body(*refs):
        ins, outs, sems = refs[:n], refs[n:2 * n], refs[2 * n:]
        _scatter_start(ins, outs, *sems)
        _scatter_wait(ins, outs, *sems)

    return _pallas_call(body, name=name, in_specs=[ANY_SPEC] * n, out_specs=[ANY_SPEC] * n,
                          out_shape=[jax.ShapeDtypeStruct(a.shape, a.dtype) for a in arrs],
                          scratch_shapes=_comm_scratch(n))(*arrs)


_BNN = (((2,), (1,)), ((0,), (0,)))
_BNT = (((2,), (2,)), ((0,), (0,)))
_BTN = (((1,), (1,)), ((0,), (0,)))


def _split(a):
    hi = a.astype(BF16)
    return hi, (a - hi.astype(F32)).astype(BF16)


def _dot3(a, b, dims=_BNN):
    ah, al = _split(a)
    bh, bl = _split(b)
    if dims != _BNN:
        return _dot(ah, bh, dims) + _dot(al, bh, dims) + _dot(ah, bl, dims)
    m = a.shape[1]
    r = _dot(jnp.concatenate([ah, al], axis=1), bh, _BNN)
    return r[:, :m] + r[:, m:] + _dot(ah, bl, _BNN)


def _tri_sum(mask, x):
    x1 = x.astype(BF16)
    r1 = x - x1.astype(F32)
    x2 = r1.astype(BF16)
    x3 = (r1 - x2.astype(F32)).astype(BF16)
    mb = mask.astype(BF16)
    return _dot(mb, x1) + _dot(mb, x2) + _dot(mb, x3)


GC = 3


def _rows_of(c):
    return slice(c * CH, (c + 1) * CH)


def _heads(ref, off):
    return jnp.stack([ref[_rows_of(c), off + h * DH:off + (h + 1) * DH] for c in range(GC) for h in range(H)])


def _cols(arrs):
    return jnp.stack([a[:, h:h + 1] for a in arrs for h in range(H)])


def _lanes(a):
    lane = lax.broadcasted_iota(jnp.int32, (CH, LANE), 1)
    out = jnp.zeros((CH, LANE), F32)
    for h in range(H):
        out = jnp.where(lane == h, a[h], out)
    return out


def _chunk_prep(qkv_ref, b_ref, g_ref):
    row = lax.broadcasted_iota(jnp.int32, (CH, CH), 0)
    col = lax.broadcasted_iota(jnp.int32, (CH, CH), 1)
    incl, strict = row >= col, row > col
    gcs = [_tri_sum(incl, g_ref[_rows_of(c), :]) for c in range(GC)]
    q, k, v = _heads(qkv_ref, 0), _heads(qkv_ref, D), _heads(qkv_ref, 2 * D)
    bcol, gcol = _cols([b_ref[_rows_of(c), :] for c in range(GC)]), _cols(gcs)
    grow = jnp.stack([gct[h:h + 1, :] for gct in [gc.T for gc in gcs] for h in range(H)])
    glast = _cols([gc[CH - 1:CH, :] for gc in gcs])
    dec = jnp.exp(jnp.where(incl[None], gcol - grow, -1e30))
    kb = k * bcol
    ab = _bdot(jnp.concatenate([kb, q], axis=1), k, _BNT)
    egc, ekc = jnp.exp(gcol), jnp.exp(glast - gcol)
    return dict(row=row, col=col, strict=strict[None], q=q, k=k, v=v, bcol=bcol, dec=dec, kb=kb,
                lm=jnp.where(strict[None], ab[:, :CH] * dec, 0.0), qk=ab[:, CH:] * dec, egc=egc, ekc=ekc,
                gth=jnp.exp(glast), qd=q * egc, kd=k * ekc, vb=v * bcol, kbg=kb * egc)


def _unit_lower_inverse(lm, eye):
    n = -lm
    x = eye + n
    pw = _dot3(n, n)
    for it in range(5):
        if it < 4:
            xp = _dot3(jnp.concatenate([x, pw], axis=1), pw)
            x = x + xp[:, :CH]
            pw = xp[:, CH:]
        else:
            x = x + _dot3(x, pw)
    return x


def _gdn_fwd(qkv, beta, g, name, gather=()):
    t = qkv.shape[0]
    nc = t // CH
    ns = nc // GC
    ng = len(gather)

    def body(qkv_ref, b_ref, g_ref, *rest):
        c_ins, (o_ref, sin_ref, vn_ref, ti_ref, w_ref) = rest[:ng], rest[ng:ng + 5]
        c_outs, state, sems = rest[ng + 5:2 * ng + 5], rest[2 * ng + 5], rest[2 * ng + 6:]
        step = pl.program_id(0)

        @pl.when(step == 0)
        def _():
            state[...] = jnp.zeros_like(state)
            if ng:
                _gather_start(c_ins, c_outs, *sems)

        if ng:
            @pl.when(step == max(ns - 4, 0))
            def _():
                _gather_forward(c_ins, c_outs, *sems)

            @pl.when(step == ns - 1)
            def _():
                _gather_finish(c_ins, c_outs, *sems)

        pr = _chunk_prep(qkv_ref, b_ref, g_ref)
        tinv = _unit_lower_inverse(pr["lm"], (pr["row"] == pr["col"]).astype(F32)[None])
        uw = _bdot(tinv, jnp.concatenate([pr["vb"], pr["kbg"]], axis=2), _BNN)
        u, w = uw[:, :, :DH], uw[:, :, DH:]
        s = state[...]
        for c in range(GC):
            hs = slice(c * H, (c + 1) * H)
            ws = _bdot(jnp.concatenate([w[hs], pr["qd"][hs]], axis=1), s, _BNN)
            vn = u[hs] - ws[:, :CH]
            o = ws[:, CH:] + _bdot(pr["qk"][hs], vn, _BNN)
            sin_ref[c] = s
            ti_ref[c] = tinv[hs]
            s = s * pr["gth"][hs] + _bdot(pr["kd"][hs], vn, _BTN)
            for h in range(H):
                sl = slice(h * DH, (h + 1) * DH)
                o_ref[_rows_of(c), sl] = o[h]
                vn_ref[_rows_of(c), sl] = vn[h]
                w_ref[_rows_of(c), sl] = w[c * H + h]
        state[...] = s

    chunk = lambda cols: pl.BlockSpec((GC * CH, cols), lambda c: (c, 0))
    outs = _pallas_call(
        body, name=name, grid=(ns,), in_specs=[chunk(QKV), chunk(LANE), chunk(LANE)] + [ANY_SPEC] * ng,
        out_specs=[chunk(D), pl.BlockSpec((GC, H, DH, DH), lambda c: (c, 0, 0, 0)), chunk(D),
                   pl.BlockSpec((GC, H, CH, CH), lambda c: (c, 0, 0, 0)), chunk(D)] + [ANY_SPEC] * ng,
        out_shape=[jax.ShapeDtypeStruct((t, D), F32), jax.ShapeDtypeStruct((nc, H, DH, DH), F32),
                   jax.ShapeDtypeStruct((t, D), F32), jax.ShapeDtypeStruct((nc, H, CH, CH), F32),
                   jax.ShapeDtypeStruct((t, D), F32)] + _gathered_shapes(gather),
        scratch_shapes=[pltpu.VMEM((H, DH, DH), F32)] + (_comm_scratch(ng) if ng else []),
        compiler_params=_params(("arbitrary",), 12 << 20),
    )(qkv, beta, g, *gather)
    return outs[:5], outs[5:]


def _gdn_bwd(qkv, beta, g, do, s_in, vnew, tinv, wsv, name, scatter=()):
    t = qkv.shape[0]
    nsteps = t // CH // GC
    ns = len(scatter)

    def body(qkv_ref, b_ref, g_ref, do_ref, sin_ref, vn_ref, ti_ref, w_ref, *rest):
        c_ins, (dqkv_ref, db_ref, dg_ref) = rest[:ns], rest[ns:ns + 3]
        c_outs, dstate, sems = rest[ns + 3:2 * ns + 3], rest[2 * ns + 3], rest[2 * ns + 4:]
        step = pl.program_id(0)

        @pl.when(step == 0)
        def _():
            dstate[...] = jnp.zeros_like(dstate)
            if ns:
                _scatter_start(c_ins, c_outs, *sems)

        if ns:
            @pl.when(step == nsteps - 1)
            def _():
                _scatter_wait(c_ins, c_outs, *sems)

        pr = _chunk_prep(qkv_ref, b_ref, g_ref)
        ti = jnp.concatenate([ti_ref[c] for c in range(GC)], axis=0)
        s = jnp.concatenate([sin_ref[c] for c in range(GC)], axis=0)
        w, vn, doh = _heads(w_ref, 0), _heads(vn_ref, 0), _heads(do_ref, 0)
        dqd = _bdot(doh, s, _BNT)
        dqk = _bdot(doh, vn, _BNT)
        qk_do = _bdot(pr["qk"], doh, _BTN)
        qd_do = _bdot(pr["qd"], doh, _BTN)
        ds = dstate[...]
        dvn_c, dkd_c, dw_c, dgt_c = [None] * GC, [None] * GC, [None] * GC, [None] * GC
        for c in reversed(range(GC)):
            hs = slice(c * H, (c + 1) * H)
            dvn_c[c] = _bdot(pr["kd"][hs], ds, _BNN) + qk_do[hs]
            dkd_c[c] = _bdot(vn[hs], ds, _BNT)
            dw_c[c] = -_bdot(dvn_c[c], s[hs], _BNT)
            dgt_c[c] = jnp.sum(jnp.sum(ds * s[hs], axis=2, keepdims=True), axis=1, keepdims=True)
            ds = ds * pr["gth"][hs] + qd_do[hs] - _bdot(w[hs], dvn_c[c], _BTN)
        dstate[...] = ds
        dvn, dkd, dw, dgt = (jnp.concatenate(parts, axis=0) for parts in (dvn_c, dkd_c, dw_c, dgt_c))
        duw = jnp.concatenate([dvn, dw], axis=2)
        dvk = _bdot(ti, duw, _BTN)
        dvb, dkbg = dvk[:, :, :DH], dvk[:, :, DH:]
        dti = _bdot(duw, jnp.concatenate([pr["vb"], pr["kbg"]], axis=2), _BNT)
        dl = -_dot3(_dot3(ti, dti, _BTN), ti, _BNT)
        dl = jnp.where(pr["strict"], dl, 0.0)
        dab = jnp.concatenate([dl * pr["dec"], dqk * pr["dec"]], axis=1)
        r1 = _bdot(dab, pr["k"], _BNN)
        dkb = r1[:, :CH] + dkbg * pr["egc"]
        dq = r1[:, CH:] + dqd * pr["egc"]
        dk = _bdot(dab, jnp.concatenate([pr["kb"], pr["q"]], axis=1), _BTN) + dkb * pr["bcol"] + dkd * pr["ekc"]
        m = dl * pr["lm"] + dqk * pr["qk"]
        mh, ml = _split(m)
        ones = jnp.ones((GC * H, CH, LANE), BF16)
        colsum = (_dot(mh, ones, _BTN) + _dot(ml, ones, _BTN))[:, :, 0:1]
        kdsum = jnp.sum(dkd * pr["kd"], axis=2, keepdims=True)
        dgc = (jnp.sum(m, axis=2, keepdims=True) - colsum + jnp.sum(dkbg * pr["kbg"], axis=2, keepdims=True)
               + jnp.sum(dqd * pr["qd"], axis=2, keepdims=True) - kdsum)
        dglast = jnp.sum(kdsum, axis=1, keepdims=True) + dgt * pr["gth"]
        last_row = lax.broadcasted_iota(jnp.int32, (1, CH, 1), 1) == CH - 1
        dgc = dgc + jnp.where(last_row, dglast, 0.0)
        dbeta = jnp.sum(dkb * pr["k"], axis=2, keepdims=True) + jnp.sum(dvb * pr["v"], axis=2, keepdims=True)
        dv = dvb * pr["bcol"]
        upper = pr["row"] <= pr["col"]
        for c in range(GC):
            hs = slice(c * H, (c + 1) * H)
            for h in range(H):
                dqkv_ref[_rows_of(c), h * DH:(h + 1) * DH] = dq[c * H + h]
                dqkv_ref[_rows_of(c), D + h * DH:D + (h + 1) * DH] = dk[c * H + h]
                dqkv_ref[_rows_of(c), 2 * D + h * DH:2 * D + (h + 1) * DH] = dv[c * H + h]
            db_ref[_rows_of(c), :] = _lanes(dbeta[hs])
            dg_ref[_rows_of(c), :] = _tri_sum(upper, _lanes(dgc[hs]))

    chunk = lambda cols: pl.BlockSpec((GC * CH, cols), lambda c: (nsteps - 1 - c, 0))
    sq = lambda a, b: pl.BlockSpec((GC, H, a, b), lambda c: (nsteps - 1 - c, 0, 0, 0))
    outs = _pallas_call(
        body, name=name, grid=(nsteps,),
        in_specs=[chunk(QKV), chunk(LANE), chunk(LANE), chunk(D), sq(DH, DH), chunk(D), sq(CH, CH), chunk(D)] + [ANY_SPEC] * ns,
        out_specs=[chunk(QKV), chunk(LANE), chunk(LANE)] + [ANY_SPEC] * ns,
        out_shape=[jax.ShapeDtypeStruct((t, QKV), F32), jax.ShapeDtypeStruct((t, LANE), F32),
                   jax.ShapeDtypeStruct((t, LANE), F32)] + [jax.ShapeDtypeStruct(a.shape, a.dtype) for a in scatter],
        scratch_shapes=[pltpu.VMEM((H, DH, DH), F32)] + (_comm_scratch(ns) if ns else []),
        compiler_params=_params(("arbitrary",), 16 << 20),
    )(qkv, beta, g, do, s_in, vnew, tinv, wsv, *scatter)
    return outs[:3], outs[3:]


def _pool_counts(row_ids, win):
    return jnp.minimum(jnp.maximum(row_ids - LEAD, 0) + 1, win).astype(F32)


def _pool_fwd(p, name):
    t = p.shape[0]
    ext = TE + 16

    def body(p_ref, o_ref, carry):
        i = pl.program_id(0)

        @pl.when(i == 0)
        def _():
            carry[...] = jnp.zeros_like(carry)

        ids = _row_ids(i)
        for gi, win in enumerate(POOL_WINDOWS):
            sl = slice(gi * LANE, (gi + 1) * LANE)
            xv = p_ref[:, sl]
            s = jnp.concatenate([carry[:, sl], xv], axis=0)
            sh = 1
            while sh < win:
                s = s + pltpu.roll(s, sh, 0)
                sh *= 2
            o_ref[:, sl] = (s[16:ext] / _pool_counts(ids, win) - xv).astype(BF16)
            carry[:, sl] = xv[TE - 16:TE]

    return _pallas_call(
        body, name=name, grid=(t // TE,), in_specs=[_rows(POOL_W)], out_specs=_rows(POOL_W),
        out_shape=jax.ShapeDtypeStruct((t, POOL_W), BF16), scratch_shapes=[pltpu.VMEM((16, POOL_W), F32)],
        compiler_params=_params(("arbitrary",), TE * POOL_W * 8),
    )(p)


def _pool_bwd(dpo, name):
    t = dpo.shape[0]
    n = t // TE
    ext = TE + 16

    def body(d_ref, o_ref, carry):
        i = pl.program_id(0)

        @pl.when(i == 0)
        def _():
            carry[...] = jnp.zeros_like(carry)

        ids = _row_ids(n - 1 - i)
        for gi, win in enumerate(POOL_WINDOWS):
            sl = slice(gi * LANE, (gi + 1) * LANE)
            dv = d_ref[:, sl]
            rv = dv / _pool_counts(ids, win)
            s = jnp.concatenate([rv, carry[:, sl]], axis=0)
            sh = 1
            while sh < win:
                s = s + pltpu.roll(s, ext - sh, 0)
                sh *= 2
            o_ref[:, sl] = (s[0:TE] - dv).astype(BF16)
            carry[:, sl] = rv[0:16]

    return _pallas_call(
        body, name=name, grid=(n,), in_specs=[_rows(POOL_W, n)], out_specs=_rows(POOL_W, n),
        out_shape=jax.ShapeDtypeStruct((t, POOL_W), BF16), scratch_shapes=[pltpu.VMEM((16, POOL_W), F32)],
        compiler_params=_params(("arbitrary",), TE * POOL_W * 8),
    )(dpo)


def _post_fwd(o, z, gate, pm, hn, ps, name):
    t = o.shape[0]

    def body(o_ref, z_ref, g_ref, pm_ref, hn_ref, ps_ref, y_ref):
        for h in range(H):
            sl = slice(h * DH, (h + 1) * DH)
            ov = o_ref[:, sl]
            zv = z_ref[:, sl].astype(F32)
            r = lax.rsqrt(jnp.mean(ov * ov, axis=-1, keepdims=True) + EPS)
            ya = ov * r * hn_ref[...] * (zv * _sigmoid(zv))
            ga = _sigmoid(g_ref[:, sl].astype(F32))
            gb = _sigmoid(g_ref[:, D + h * DH:D + (h + 1) * DH].astype(F32))
            y_ref[:, sl] = (ga * ya + gb * (pm_ref[:, sl] * ps_ref[:, sl])).astype(BF16)

    return _pallas_call(
        body, name=name, grid=(t // TE,),
        in_specs=[_rows(D), _rows(D), _rows(2 * D), _rows(D), _whole((1, DH)), _whole((1, D))], out_specs=_rows(D),
        out_shape=jax.ShapeDtypeStruct((t, D), BF16), compiler_params=_params(("parallel",), TE * D * 16),
    )(o, z, gate, pm, hn, ps)


def _post_bwd(dy, o, z, gate, pm, hn, ps, name):
    t = o.shape[0]

    def body(dy_ref, o_ref, z_ref, g_ref, pm_ref, hn_ref, ps_ref, do_ref, dz_ref, dgate_ref, dpm_ref, dhn_ref, dps_ref):
        i = pl.program_id(0)

        @pl.when(i == 0)
        def _():
            dhn_ref[...] = jnp.zeros_like(dhn_ref)
            dps_ref[...] = jnp.zeros_like(dps_ref)

        hnv = hn_ref[...]
        dhn = jnp.zeros((1, DH), F32)
        for h in range(H):
            sl = slice(h * DH, (h + 1) * DH)
            slb = slice(D + h * DH, D + (h + 1) * DH)
            dyv = dy_ref[:, sl]
            ov = o_ref[:, sl]
            zv = z_ref[:, sl].astype(F32)
            r = lax.rsqrt(jnp.mean(ov * ov, axis=-1, keepdims=True) + EPS)
            sz = _sigmoid(zv)
            silu = zv * sz
            on = ov * r
            ya = on * hnv * silu
            ga = _sigmoid(g_ref[:, sl].astype(F32))
            gb = _sigmoid(g_ref[:, slb].astype(F32))
            pmv = pm_ref[:, sl]
            psv = ps_ref[:, sl]
            dya = dyv * ga
            dyb = dyv * gb
            dgate_ref[:, sl] = (dyv * ya * ga * (1.0 - ga)).astype(BF16)
            dgate_ref[:, slb] = (dyv * (pmv * psv) * gb * (1.0 - gb)).astype(BF16)
            tt = dya * hnv * silu
            do_ref[:, sl] = r * tt - ov * (r * r * r) * jnp.mean(ov * tt, axis=-1, keepdims=True)
            dz_ref[:, sl] = (dya * on * hnv * (sz * (1.0 + zv * (1.0 - sz)))).astype(BF16)
            dhn = dhn + jnp.sum(dya * on * silu, axis=0, keepdims=True)
            dps_ref[:, sl] += jnp.sum(dyb * pmv, axis=0, keepdims=True)
            dpm_ref[:, sl] = (dyb * psv).astype(BF16)
        dhn_ref[...] += dhn

    return _pallas_call(
        body, name=name, grid=(t // TE,),
        in_specs=[_rows(D), _rows(D), _rows(D), _rows(2 * D), _rows(D), _whole((1, DH)), _whole((1, D))],
        out_specs=[_rows(D), _rows(D), _rows(2 * D), _rows(D), _whole((1, DH)), _whole((1, D))],
        out_shape=[jax.ShapeDtypeStruct((t, D), F32), jax.ShapeDtypeStruct((t, D), BF16),
                   jax.ShapeDtypeStruct((t, 2 * D), BF16), jax.ShapeDtypeStruct((t, D), BF16),
                   jax.ShapeDtypeStruct((1, DH), F32), jax.ShapeDtypeStruct((1, D), F32)],
        compiler_params=_params(("arbitrary",), TE * D * 28),
    )(dy, o, z, gate, pm, hn, ps)


_FB_COLS = [(c, min(c + LANE, FB)) for c in range(0, FB, LANE)]


def _mlp_act_fwd(hid, cw, name):
    t = hid.shape[1]
    n = t // TE

    def body(hg_ref, hv_ref, wg_ref, wv_ref, a_ref, xg, xv):
        i = pl.program_id(1)
        _stage_history(xg, i)
        _stage_history(xv, i)
        for c, (c0, c1) in enumerate(_FB_COLS):
            sl, wd = slice(c0, c1), c1 - c0
            xg[c, SUB:SUB + TE, 0:wd] = hg_ref[:, sl].astype(F32)
            xv[c, SUB:SUB + TE, 0:wd] = hv_ref[:, sl].astype(F32)
            gg = _conv(_taps(xg, c, wd, SUB, TE, 3), wg_ref[:, sl])
            vv = _conv(_taps(xv, c, wd, SUB, TE, 3), wv_ref[:, sl])
            a_ref[:, sl] = (gg * _sigmoid(gg) * vv).astype(BF16)

    hspec = lambda off: pl.BlockSpec((None, TE, FB), lambda p, i: (p + off, i, 0))
    wspec = lambda off: pl.BlockSpec((None, 3, FB), lambda p, i: (p + off, 0, 0))
    return _pallas_call(
        body, name=name, grid=(4, n), in_specs=[hspec(0), hspec(4), wspec(0), wspec(4)],
        out_specs=pl.BlockSpec((None, TE, FB), lambda p, i: (p, i, 0)),
        out_shape=jax.ShapeDtypeStruct((4, t, FB), BF16),
        scratch_shapes=[_seq_scratch(FB), _seq_scratch(FB)],
        compiler_params=_params(("parallel", "arbitrary"), TE * FB * 12),
    )(hid, hid, cw, cw)


def _mlp_act_bwd(da, hid, cw, name):
    t = hid.shape[1]
    n = t // TE
    hb = TE // 16

    def body(da_ref, hg_ref, hv_ref, pg_ref, pv_ref, wg_ref, wv_ref, dhg_ref, dhv_ref, dwg_ref, dwv_ref, xg, xv, dg, dv):
        i = pl.program_id(1)

        @pl.when(i == 0)
        def _():
            dwg_ref[...] = jnp.zeros_like(dwg_ref)
            dwv_ref[...] = jnp.zeros_like(dwv_ref)

        _stage_future(dg, i)
        _stage_future(dv, i)
        first_tile = i == n - 1
        for c, (c0, c1) in enumerate(_FB_COLS):
            sl, wd = slice(c0, c1), c1 - c0
            for scr, p_ref, h_ref in ((xg, pg_ref, hg_ref), (xv, pv_ref, hv_ref)):
                scr[c, 0:SUB, 0:wd] = jnp.where(first_tile, 0.0, p_ref[SUB:2 * SUB, sl].astype(F32))
                scr[c, SUB:SUB + TE, 0:wd] = h_ref[:, sl].astype(F32)
            wg = wg_ref[:, sl]
            wv = wv_ref[:, sl]
            tg = _taps(xg, c, wd, SUB, TE, 3)
            tv = _taps(xv, c, wd, SUB, TE, 3)
            gg = _conv(tg, wg)
            vv = _conv(tv, wv)
            sg = _sigmoid(gg)
            dav = da_ref[:, sl].astype(F32)
            dgg = dav * vv * (sg * (1.0 + gg * (1.0 - sg)))
            dvv = dav * (gg * sg)
            for dc, tp, w, scr, dh_ref, dw_ref in ((dgg, tg, wg, dg, dhg_ref, dwg_ref), (dvv, tv, wv, dv, dhv_ref, dwv_ref)):
                scr[c, 0:TE, 0:wd] = dc
                dh_ref[:, sl] = _conv_t(scr, c, wd, w).astype(BF16)
                dw_ref[:, sl] += jnp.concatenate([jnp.sum(tp[j] * dc, axis=0, keepdims=True) for j in range(3)], axis=0)

    rev = lambda off: pl.BlockSpec((None, TE, FB), lambda p, i: (p + off, n - 1 - i, 0))
    halo = lambda off: pl.BlockSpec((None, 16, FB), lambda p, i: (p + off, jnp.maximum((n - 1 - i) * hb - 1, 0), 0))
    wspec = lambda off: pl.BlockSpec((None, 3, FB), lambda p, i: (p + off, 0, 0))
    dwspec = pl.BlockSpec((None, 3, FB), lambda p, i: (p, 0, 0))
    return _pallas_call(
        body, name=name, grid=(4, n), in_specs=[rev(0), rev(0), rev(4), halo(0), halo(4), wspec(0), wspec(4)],
        out_specs=[rev(0), rev(0), dwspec, dwspec],
        out_shape=[jax.ShapeDtypeStruct((4, t, FB), BF16), jax.ShapeDtypeStruct((4, t, FB), BF16),
                   jax.ShapeDtypeStruct((4, 3, FB), F32), jax.ShapeDtypeStruct((4, 3, FB), F32)],
        scratch_shapes=[_seq_scratch(FB)] * 4,
        compiler_params=_params(("parallel", "arbitrary"), TE * FB * 24),
    )(da, hid, hid, hid, hid, cw, cw)


def _adamw(lands, w, m, v, name):
    nl, r, c = w.shape
    tr = r
    if r * c * 4 > (2 << 20):
        for cand in (128, 64, 32, 16):
            if r % cand == 0:
                tr = cand
                break
    nr = r // tr
    c1 = 1.0 - ADAM_B1 ** ADAM_STEP
    c2 = 1.0 - ADAM_B2 ** ADAM_STEP

    def body(*refs):
        l_refs, (w_ref, m_ref, v_ref, g_out, d_out, m_out, v_out) = refs[:nl], refs[nl:]
        layer = pl.program_id(0)
        g = None
        for l, l_ref in enumerate(l_refs):
            gl = l_ref[0].astype(F32)
            for i in range(1, NDEV):
                gl = gl + l_ref[i].astype(F32)
            g = gl if g is None else jnp.where(layer == l, gl, g)
        mn = ADAM_B1 * m_ref[...] + (1.0 - ADAM_B1) * g
        vn = ADAM_B2 * v_ref[...] + (1.0 - ADAM_B2) * (g * g)
        g_out[...] = g
        m_out[...] = mn
        v_out[...] = vn
        d_out[...] = -ADAM_LR * ((mn / c1) / (jnp.sqrt(vn / c2) + ADAM_EPS) + ADAM_WD * w_ref[...])

    def land_spec(l):
        return pl.BlockSpec((NDEV, tr, c), lambda ly, i: (0, jnp.where(ly == l, i, 0 if l > 0 else nr - 1), 0))

    spec = pl.BlockSpec((None, tr, c), lambda ly, i: (ly, i, 0))
    shp = jax.ShapeDtypeStruct((nl, r, c), F32)
    return _pallas_call(
        body, name=name, grid=(nl, nr), in_specs=[land_spec(l) for l in range(nl)] + [spec, spec, spec],
        out_specs=[spec] * 4, out_shape=[shp] * 4,
        compiler_params=_params(("arbitrary", "arbitrary"), (11 + 4 * nl) * tr * c * 4),
    )(*lands, w, m, v)


def _layer_fwd(h, p, tag, gather=(), finish=None):
    u = _rmsnorm_fwd(h, p["norm_mix"], f"norm_mix_{tag}")
    qkv_pre = _mm(u, p["w_qkv"], BF16, f"proj_qkv_{tag}")
    z = _mm(u, p["w_z"], BF16, f"proj_z_{tag}")
    ba = _mm(u, p["w_ba"], F32, f"proj_ba_{tag}")
    pool_in = _mm(u, p["w_pl"], F32, f"proj_pool_{tag}")
    gate = _mm(u, p["w_gate"], BF16, f"proj_gate_{tag}")
    qkv = _gdn_pre_fwd(qkv_pre, p["conv_qkv"], f"gdn_pre_{tag}")
    beta, g = _gates_fwd(ba, p["a_row"], p["dt_row"], f"gates_{tag}")
    (o, s_in, vnew, tinv, wsv), gathered = _gdn_fwd(qkv, beta, g, f"gdn_{tag}", gather)
    if finish is not None:
        p = {**p, **finish(gathered)}
    pooled = _pool_fwd(pool_in, f"pool_{tag}")
    pm = _mm_cols(pooled, p["w_pool"], F32, f"pool_mm_{tag}", _NN)
    y = _post_fwd(o, z, gate, pm, p["head_norm"], p["pool_scale"], f"post_{tag}")
    h1 = _mm(y, p["w_out"], F32, f"out_proj_{tag}", res=h)
    u2 = _rmsnorm_fwd(h1, p["norm_ffn"], f"norm_ffn_{tag}")
    hid = _mm_up(u2, p["w_up"], f"up_proj_{tag}")
    act = _mlp_act_fwd(hid, p["conv_ffn"], f"mlp_act_{tag}")
    h2 = _mm_blocks_red(act, p["w_down"], f"down_proj_{tag}", _NN, res=h1)
    saved = dict(h=h, u=u, qkv_pre=qkv_pre, z=z, ba=ba, gate=gate, qkv=qkv, beta=beta, g=g, o=o, s_in=s_in, vnew=vnew,
                 tinv=tinv, wsv=wsv, pooled=pooled, pm=pm, y=y, h1=h1, u2=u2, hid=hid, act=act)
    return h2, saved, gathered, p


def _layer_bwd(dh, dh_b, p, s, tag, scatter=()):
    gr = {}
    da = _mm_to_blocks(dh_b, p["w_down"], f"d_act_{tag}")
    gr["w_down"] = _mm_tn_blocks(s["act"], dh_b, f"dw_down_{tag}", True, False, BF16)
    dhg, dhv, dwg, dwv = _mlp_act_bwd(da, s["hid"], p["conv_ffn"], f"mlp_act_bwd_{tag}")
    gr["conv_ffn"] = jnp.concatenate([dwg, dwv], axis=0)
    w_up = p["w_up"]
    du2 = _mm_du2(dhg, dhv, w_up, f"d_u2_{tag}")
    gr["w_up"] = jnp.concatenate([_mm_tn_blocks(s["u2"], dhg, f"dw_upg_{tag}", False, True, BF16),
                                  _mm_tn_blocks(s["u2"], dhv, f"dw_upv_{tag}", False, True, BF16)], axis=0)
    dh1, dh1_b, gr["norm_ffn"] = _rmsnorm_bwd(s["h1"], du2, dh, p["norm_ffn"], f"norm_ffn_bwd_{tag}")
    dy = _mm(dh1_b, p["w_out"], F32, f"d_y_{tag}", dims=_NT)
    gr["w_out"] = _mm_tn(s["y"], dh1_b, f"dw_out_{tag}", BF16)
    do, dz, dgate, dpm, gr["head_norm"], gr["pool_scale"] = _post_bwd(
        dy, s["o"], s["z"], s["gate"], s["pm"], p["head_norm"], p["pool_scale"], f"post_bwd_{tag}")
    dpooled = _mm_cols(dpm, p["w_pool"], F32, f"d_pooled_{tag}", _NT)
    gr["w_pool"] = _mm_tn_cols(s["pooled"], dpm, 4, f"dw_pool_{tag}")
    dpool_in = _pool_bwd(dpooled, f"pool_bwd_{tag}")
    own = (gr["w_up"].astype(BF16), gr["w_down"].reshape(NDEV, -1, D).astype(BF16),
           gr["w_out"].reshape(NDEV, D // NDEV, D).astype(BF16))
    (dqkv, dbeta, dg), landed = _gdn_bwd(s["qkv"], s["beta"], s["g"], do, s["s_in"], s["vnew"], s["tinv"], s["wsv"],
                                         f"gdn_bwd_{tag}", own + tuple(scatter))
    dba, gr["a_log"], gr["dt_bias"] = _gates_bwd(s["ba"], p["a_row"], p["dt_row"], dbeta, dg, f"gates_bwd_{tag}")
    dqkv_pre, gr["conv_qkv"] = _gdn_pre_bwd(s["qkv_pre"], p["conv_qkv"], dqkv, f"gdn_pre_bwd_{tag}")
    segs = (("qkv", dqkv_pre, p["w_qkv"]), ("z", dz, p["w_z"]), ("ba", dba, p["w_ba"]),
            ("pool", dpool_in, p["w_pl"]), ("gate", dgate, p["w_gate"]))
    du = _mm_sum_nt([(dseg, wseg) for _, dseg, wseg in segs], f"d_u_{tag}")
    dws = [_mm_tn(s["u"], dseg, f"dw_{nm}_{tag}", BF16) for nm, dseg, _ in segs]
    gr["w_in"] = jnp.concatenate([dws[0], dws[1], dws[2][:, 0:H], dws[2][:, LANE:LANE + H], dws[3], dws[4]], axis=1)
    dh0, dh0_b, gr["norm_mix"] = _rmsnorm_bwd(s["h"], du, dh1, p["norm_mix"], f"norm_mix_bwd_{tag}")
    return dh0, dh0_b, gr, landed


def _pad_lanes(v8):
    return jnp.pad(v8.reshape(1, H), ((0, 0), (0, LANE - H)))


def _pack(parts, rows, lead=1):
    flat = jnp.concatenate([q.reshape(lead, -1) for q in parts], axis=1)
    flat = jnp.pad(flat, ((0, 0), (0, rows * LANE - flat.shape[1])))
    return flat.reshape((lead, rows, LANE) if lead > 1 else (rows, LANE))


def _unpack(packed, shapes, lead=1):
    flat = packed.reshape(lead, -1)
    out, off = [], 0
    for shp in shapes:
        n = 1
        for s_ in shp:
            n *= s_
        n //= lead
        out.append(flat[:, off:off + n].reshape(shp))
        off += n
    return out


SMALL_ROWS = 336
REPL_ROWS = 64


def kernel(x, meta_tokens, norm_mix, w_in, conv_qkv, a_log, dt_bias, head_norm, w_pool, pool_scale, w_out, norm_ffn, w_up, conv_ffn, w_down, norm_final, loss_target, m_meta_tokens, m_norm_mix, m_w_in, m_conv_qkv, m_a_log, m_dt_bias, m_head_norm, m_w_pool, m_pool_scale, m_w_out, m_norm_ffn, m_w_up, m_conv_ffn, m_w_down, m_norm_final, v_meta_tokens, v_norm_mix, v_w_in, v_conv_qkv, v_a_log, v_dt_bias, v_head_norm, v_w_pool, v_pool_scale, v_w_out, v_norm_ffn, v_w_up, v_conv_ffn, v_w_down, v_norm_final):
    seq = x.shape[1]
    t = ROW0 + seq
    assert t % TE == 0 and t % (MM_TILES * 16) == 0 and t % (GC * CH) == 0
    depth = w_in.shape[0]
    assert depth == 2
    cin = w_in.shape[2]

    def mixer_params(l, g_in, conv_q, conv_f, wp):
        wf = jnp.transpose(g_in, (1, 0, 2)).reshape(D, NDEV * cin)
        zpad = jnp.zeros((D, LANE - H), BF16)
        return dict(
            w_qkv=wf[:, 0:QKV], w_z=wf[:, QKV:QKV + D],
            w_ba=jnp.concatenate([wf[:, 4096:4104], zpad, wf[:, 4104:4112], zpad], axis=1),
            w_pl=wf[:, 4112:4624], w_gate=wf[:, 4624:6672], conv_qkv=conv_q, conv_ffn=conv_f, w_pool=wp,
            norm_mix=norm_mix[l].reshape(1, D), norm_ffn=norm_ffn[l].reshape(1, D),
            pool_scale=pool_scale[l].reshape(1, D), head_norm=head_norm[l].reshape(1, DH),
            a_row=_pad_lanes(a_log[l]), dt_row=_pad_lanes(dt_bias[l]))

    def late_params(g_up, g_out, g_down):
        return dict(w_out=g_out.reshape(D, D), w_up=g_up, w_down=g_down.reshape(4, FB, D))

    small_shapes = [conv_qkv.shape, conv_ffn.shape, w_pool.shape, meta_tokens.shape]
    small = _pack([conv_qkv, conv_ffn, w_pool, meta_tokens], SMALL_ROWS)
    w_in_b, w_up_b, w_out_b, w_down_b = w_in.astype(BF16), w_up.astype(BF16), w_out.astype(BF16), w_down.astype(BF16)
    g_in0, g_small = _gather([w_in_b[0], small], "gather_first")
    s_cq, s_cf, s_wp, s_mt = _unpack(g_small, [(NDEV,) + shp for shp in small_shapes], lead=NDEV)
    conv_qkv_full = jnp.transpose(s_cq, (1, 2, 0, 3)).reshape(depth, 4, QKV)
    conv_ffn_blk = jnp.transpose(s_cf, (1, 0, 2, 3))
    w_pool_full = jnp.transpose(s_wp, (1, 2, 3, 0, 4)).reshape(depth, 4, DH, 2 * DH).astype(BF16)
    meta_full = jnp.transpose(s_mt, (1, 0, 2)).reshape(N_META, D)

    h = jnp.concatenate([jnp.zeros((LEAD, D), F32), meta_full, x[0]], axis=0)
    p0 = mixer_params(0, g_in0, conv_qkv_full[0], conv_ffn_blk[0], w_pool_full[0])
    h, sv0, rest, p0 = _layer_fwd(
        h, p0, "l0", (w_up_b[0], w_out_b[0], w_down_b[0], w_in_b[1], w_up_b[1], w_out_b[1], w_down_b[1]),
        lambda got: late_params(*got[:3]))
    p1 = {**mixer_params(1, rest[3], conv_qkv_full[1], conv_ffn_blk[1], w_pool_full[1]), **late_params(*rest[4:])}
    h, sv1, _, _ = _layer_fwd(h, p1, "l1")
    layers = [p0, p1]
    saved = [sv0, sv1]
    target = jnp.concatenate([jnp.zeros((ROW0, D), F32), loss_target[0]], axis=0)
    dh, dh_b, d_norm_final, loss_row = _loss_bwd(h, target, norm_final.reshape(1, D), "loss")

    def w_in_blocks(gr):
        return jnp.transpose(gr["w_in"].reshape(D, NDEV, cin), (1, 0, 2)).astype(BF16)

    grads = [None] * depth
    dh, dh_b, grads[1], (l_up1, l_down1, l_out1) = _layer_bwd(dh, dh_b, layers[1], saved[1], "l1")
    dh, dh_b, grads[0], (l_up0, l_down0, l_out0, l_in1) = _layer_bwd(dh, dh_b, layers[0], saved[0], "l0", (w_in_blocks(grads[1]),))
    grad_x = dh[ROW0:].reshape(1, seq, D)
    d_meta = dh[LEAD:ROW0]

    stk = lambda name: jnp.stack([grads[l][name] for l in range(depth)], axis=0)
    cq = conv_qkv.shape[2]
    pw = w_pool.shape[3]
    s_cq = jnp.transpose(stk("conv_qkv").reshape(depth, 4, NDEV, cq), (2, 0, 1, 3))
    s_cf = jnp.transpose(stk("conv_ffn"), (1, 0, 2, 3))
    s_wp = jnp.transpose(stk("w_pool").reshape(depth, 4, DH, NDEV, pw), (3, 0, 1, 2, 4))
    s_mt = jnp.transpose(d_meta.reshape(N_META, NDEV, D // NDEV), (1, 0, 2))
    b_small = _pack([s_cq, s_cf, s_wp, s_mt], SMALL_ROWS, lead=NDEV)
    l_in0, l_small = _scatter([w_in_blocks(grads[0]), b_small], "exchange_last")

    r_in = _adamw((l_in0, l_in1), w_in, m_w_in, v_w_in, "adamw_w_in")
    r_up = _adamw((l_up0, l_up1), w_up, m_w_up, v_w_up, "adamw_w_up")
    r_out = _adamw((l_out0, l_out1), w_out, m_w_out, v_w_out, "adamw_w_out")
    r_down = _adamw((l_down0, l_down1), w_down, m_w_down, v_w_down, "adamw_w_down")
    r_small = _adamw((l_small,), small[None], _pack([m_conv_qkv, m_conv_ffn, m_w_pool, m_meta_tokens], SMALL_ROWS)[None],
                     _pack([v_conv_qkv, v_conv_ffn, v_w_pool, v_meta_tokens], SMALL_ROWS)[None], "adamw_small")
    r_small = [_unpack(o_[0], small_shapes) for o_ in r_small]

    repl_shapes = [norm_mix.shape, a_log.shape, dt_bias.shape, head_norm.shape, pool_scale.shape, norm_ffn.shape,
                   norm_final.shape, (1,)]
    rp = lambda name, n: jnp.stack([grads[l][name][0, :n] for l in range(depth)], axis=0)
    part = _pack([rp("norm_mix", D), rp("a_log", H), rp("dt_bias", H), rp("head_norm", DH), rp("pool_scale", D),
                  rp("norm_ffn", D), d_norm_final[0], loss_row[0, 0:1]], REPL_ROWS)
    (l_repl,) = _gather([part], "gather_replicated")
    zero1 = jnp.zeros((1,), F32)
    r_repl = _adamw(
        (l_repl,), _pack([norm_mix, a_log, dt_bias, head_norm, pool_scale, norm_ffn, norm_final, zero1], REPL_ROWS)[None],
        _pack([m_norm_mix, m_a_log, m_dt_bias, m_head_norm, m_pool_scale, m_norm_ffn, m_norm_final, zero1], REPL_ROWS)[None],
        _pack([v_norm_mix, v_a_log, v_dt_bias, v_head_norm, v_pool_scale, v_norm_ffn, v_norm_final, zero1], REPL_ROWS)[None],
        "adamw_replicated")
    r_repl = [_unpack(o_[0], repl_shapes) for o_ in r_repl]
    loss = r_repl[0][7].reshape(())

    def leaf(kind):
        sm, rr = r_small[kind], r_repl[kind]
        return [sm[3], rr[0], r_in[kind], sm[0], rr[1], rr[2], rr[3], sm[2], rr[4], r_out[kind], rr[5], r_up[kind],
                sm[1], r_down[kind], rr[6]]

    return (loss, grad_x, *leaf(0), *leaf(1), *leaf(2), *leaf(3))
```

```python
import jax
import jax.numpy as jnp
from jax import lax
from jax.experimental import pallas as pl
from jax.experimental.pallas import tpu as pltpu

F32 = jnp.float32
BF16 = jnp.bfloat16
MESH = pl.DeviceIdType.MESH

D = 1024
H = 8
DH = 128
CH = 64
N_META = 16
LEAD = 48
ROW0 = LEAD + N_META
QKV = 3 * D
POOL_W = 512
POOL_WINDOWS = (2, 4, 8, 16)
FB = 704
NDEV = 8
EPS = 1e-6
MM_TILES = 12
TE = 192
LANE = 128
SUB = 8
VMEM_CAP = 56 << 20

ADAM_LR, ADAM_B1, ADAM_B2, ADAM_EPS, ADAM_WD, ADAM_STEP = 0.001, 0.9, 0.999, 1e-08, 0.01, 10

_NN = (((1,), (0,)), ((), ()))
_NT = (((1,), (1,)), ((), ()))
_TN = (((0,), (0,)), ((), ()))


def _dot(a, b, dims=_NN, precision=None):
    return lax.dot_general(a, b, dims, precision=precision, preferred_element_type=F32)


def _bdot(a, b, dims=_NN):
    return _dot(a.astype(BF16), b.astype(BF16), dims)


def _nbytes(shape, dtype):
    n = 1
    for s in shape:
        n *= s
    return n * jnp.dtype(dtype).itemsize


def _params(sem, block_bytes):
    limit = min(VMEM_CAP, 2 * block_bytes + (20 << 20))
    return pltpu.CompilerParams(dimension_semantics=sem, vmem_limit_bytes=limit)


PIN_BYTES = 12 << 20


def _pallas_call(body, *, out_shape, **kw):
    call = pl.pallas_call

    def big(s):
        return len(s.shape) >= 2 and s.shape[-1] >= D and _nbytes(s.shape, s.dtype) >= PIN_BYTES

    pinned = jax.tree.map(lambda s: pltpu.HBM(s.shape, s.dtype) if big(s) else s, out_shape)

    def run(*args):
        return call(body, out_shape=pinned, **kw)(
            *[pltpu.with_memory_space_constraint(a, pltpu.HBM) if big(a) else a for a in args])

    return run


def _sigmoid(x):
    return 1.0 / (1.0 + jnp.exp(-x))


def _col_tile(n):
    for t in (1024, 512, 256, 128):
        if n % t == 0:
            return t
    return n


def _matmul(a, b, *, dims, grid, a_spec, b_spec, o_spec, out_shape, name, red_axis=None, res=None):
    o_blk = tuple(s for s in o_spec.block_shape if s is not None)
    via_scratch = red_axis is not None and out_shape.dtype != F32

    def body(*refs):
        if res is None:
            a_ref, b_ref, o_ref = refs[:3]
        else:
            a_ref, b_ref, r_ref, o_ref = refs[:4]
        part = _dot(a_ref[...], b_ref[...], dims)
        if red_axis is None:
            if res is not None:
                part = part + r_ref[...]
            o_ref[...] = part.astype(o_ref.dtype)
        else:
            acc = refs[-1] if via_scratch else o_ref
            r = pl.program_id(red_axis)

            @pl.when(r == 0)
            def _():
                acc[...] = part + r_ref[...] if res is not None else part

            @pl.when(r > 0)
            def _():
                acc[...] += part

            if via_scratch:
                @pl.when(r == grid[red_axis] - 1)
                def _():
                    o_ref[...] = acc[...].astype(o_ref.dtype)

    def blk(spec, arr):
        return _nbytes([s for s in spec.block_shape if s is not None], arr.dtype)

    ins = [a, b] + ([res] if res is not None else [])
    specs = [a_spec, b_spec] + ([o_spec] if res is not None else [])
    nb = blk(a_spec, a) + blk(b_spec, b) + 2 * _nbytes(o_blk, F32)
    sem = tuple("arbitrary" if i == red_axis else "parallel" for i in range(len(grid)))
    return _pallas_call(
        body, name=name, grid=grid, in_specs=specs, out_specs=o_spec, out_shape=out_shape,
        scratch_shapes=[pltpu.VMEM(o_blk, F32)] if via_scratch else [],
        compiler_params=_params(sem, nb),
    )(*ins)


def _row_tiles(m, row_bytes, fixed_bytes, temp_row_bytes=0):
    for nt in (MM_TILES // 2, MM_TILES):
        tm = m // nt
        if 2 * (row_bytes * tm + fixed_bytes) + temp_row_bytes * tm <= VMEM_CAP - (10 << 20):
            return nt
    return MM_TILES


def _mm(a, b, out_dtype, name, res=None, dims=_NN):
    m, k = a.shape
    n = b.shape[1] if dims == _NN else b.shape[0]
    tn = _col_tile(n)
    nt = _row_tiles(m, 2 * k + tn * (jnp.dtype(out_dtype).itemsize + (4 if res is not None else 0)), 2 * k * tn, 4 * tn)
    tm = m // nt
    if dims == _NN:
        b_spec = pl.BlockSpec((k, tn), lambda j, i: (0, j))
    else:
        b_spec = pl.BlockSpec((tn, k), lambda j, i: (j, 0))
    return _matmul(
        a, b, dims=dims, grid=(n // tn, nt), a_spec=pl.BlockSpec((tm, k), lambda j, i: (i, 0)), b_spec=b_spec,
        o_spec=pl.BlockSpec((tm, tn), lambda j, i: (i, j)), out_shape=jax.ShapeDtypeStruct((m, n), out_dtype),
        name=name, res=res)


def _mm_du2(dhg, dhv, w_up, name):
    g, t, k = dhg.shape
    n = w_up.shape[1]
    nt = _row_tiles(t, 4 * k + 4 * n, 4 * k * n, 4 * n)
    tm = t // nt

    def body(ag_ref, av_ref, bg_ref, bv_ref, o_ref):
        part = _dot(ag_ref[...], bg_ref[...], _NT) + _dot(av_ref[...], bv_ref[...], _NT)
        r = pl.program_id(1)

        @pl.when(r == 0)
        def _():
            o_ref[...] = part

        @pl.when(r > 0)
        def _():
            o_ref[...] += part

    a_spec = pl.BlockSpec((None, tm, k), lambda i, g_: (g_, i, 0))
    return _pallas_call(
        body, name=name, grid=(nt, g),
        in_specs=[a_spec, a_spec, pl.BlockSpec((None, n, k), lambda i, g_: (g_, 0, 0)),
                  pl.BlockSpec((None, n, k), lambda i, g_: (g_ + g, 0, 0))],
        out_specs=pl.BlockSpec((tm, n), lambda i, g_: (i, 0)), out_shape=jax.ShapeDtypeStruct((t, n), F32),
        compiler_params=_params(("parallel", "arbitrary"), 4 * tm * k + 4 * n * k + 8 * tm * n),
    )(dhg, dhv, w_up, w_up)


def _mm_sum_nt(pairs, name):
    m, n = pairs[0][0].shape[0], pairs[0][1].shape[0]
    tm, tn = m // MM_TILES, min(512, n)
    np_ = len(pairs)

    def body(*refs):
        acc = _dot(refs[0][...], refs[1][...], _NT)
        for k in range(1, np_):
            acc = acc + _dot(refs[2 * k][...], refs[2 * k + 1][...], _NT)
        refs[-1][...] = acc

    specs, ins, nb = [], [], 2 * tm * tn * 4
    for a, b in pairs:
        k = a.shape[1]
        specs += [pl.BlockSpec((tm, k), lambda j, i: (i, 0)), pl.BlockSpec((tn, k), lambda j, i: (j, 0))]
        ins += [a, b]
        nb += 2 * k * (tm + tn)
    return _pallas_call(
        body, name=name, grid=(n // tn, MM_TILES), in_specs=specs, out_specs=pl.BlockSpec((tm, tn), lambda j, i: (i, j)),
        out_shape=jax.ShapeDtypeStruct((m, n), F32), compiler_params=_params(("parallel", "parallel"), nb),
    )(*ins)


def _mm_tn(a, g, name, out_dtype=F32):
    m, k = a.shape
    n = g.shape[1]
    tn = _col_tile(n)
    nt = _row_tiles(m, 2 * k + 2 * tn, 4 * k * tn)
    tm = m // nt
    return _matmul(
        a, g, dims=_TN, grid=(n // tn, nt), red_axis=1, a_spec=pl.BlockSpec((tm, k), lambda j, i: (i, 0)),
        b_spec=pl.BlockSpec((tm, tn), lambda j, i: (i, j)), o_spec=pl.BlockSpec((k, tn), lambda j, i: (0, j)),
        out_shape=jax.ShapeDtypeStruct((k, n), out_dtype), name=name)


def _mm_up(u, w_up, name):
    t = u.shape[0]
    g = w_up.shape[0]
    nt = _row_tiles(t, 2 * D + 2 * FB, 2 * D * FB, 4 * FB)
    tm = t // nt
    return _matmul(
        u, w_up, dims=_NN, grid=(g, nt), a_spec=pl.BlockSpec((tm, D), lambda g_, i: (i, 0)),
        b_spec=pl.BlockSpec((None, D, FB), lambda g_, i: (g_, 0, 0)),
        o_spec=pl.BlockSpec((None, tm, FB), lambda g_, i: (g_, i, 0)),
        out_shape=jax.ShapeDtypeStruct((g, t, FB), BF16), name=name)


def _mm_blocks_red(a, b, name, dims, res=None):
    g, t, k = a.shape
    n = b.shape[2] if dims == _NN else b.shape[1]
    nt = _row_tiles(t, 2 * k + n * (8 if res is not None else 4), 2 * k * n, 4 * n)
    tm = t // nt
    return _matmul(
        a, b, dims=dims, grid=(nt, g), red_axis=1, a_spec=pl.BlockSpec((None, tm, k), lambda i, g_: (g_, i, 0)),
        b_spec=pl.BlockSpec((None,) + b.shape[1:], lambda i, g_: (g_, 0, 0)),
        o_spec=pl.BlockSpec((tm, n), lambda i, g_: (i, 0)), out_shape=jax.ShapeDtypeStruct((t, n), F32),
        name=name, res=res)


def _mm_to_blocks(a, b, name):
    t, k = a.shape
    g, n, _ = b.shape
    nt = _row_tiles(t, 2 * k + 2 * n, 2 * k * n, 4 * n)
    tm = t // nt
    return _matmul(
        a, b, dims=_NT, grid=(g, nt), a_spec=pl.BlockSpec((tm, k), lambda g_, i: (i, 0)),
        b_spec=pl.BlockSpec((None, n, k), lambda g_, i: (g_, 0, 0)),
        o_spec=pl.BlockSpec((None, tm, n), lambda g_, i: (g_, i, 0)),
        out_shape=jax.ShapeDtypeStruct((g, t, n), BF16), name=name)


def _mm_tn_blocks(a, g, name, a_blocked, g_blocked, out_dtype=F32):
    nb = a.shape[0] if a_blocked else g.shape[0]
    t = a.shape[-2]
    k, n = a.shape[-1], g.shape[-1]
    nt = _row_tiles(t, 2 * k + 2 * n, 4 * k * n)
    tm = t // nt
    a_spec = (pl.BlockSpec((None, tm, k), lambda g_, i: (g_, i, 0)) if a_blocked
              else pl.BlockSpec((tm, k), lambda g_, i: (i, 0)))
    g_spec = (pl.BlockSpec((None, tm, n), lambda g_, i: (g_, i, 0)) if g_blocked
              else pl.BlockSpec((tm, n), lambda g_, i: (i, 0)))
    return _matmul(
        a, g, dims=_TN, grid=(nb, nt), red_axis=1, a_spec=a_spec, b_spec=g_spec,
        o_spec=pl.BlockSpec((None, k, n), lambda g_, i: (g_, 0, 0)),
        out_shape=jax.ShapeDtypeStruct((nb, k, n), out_dtype), name=name)


def _mm_cols(a, b, out_dtype, name, dims):
    t = a.shape[0]
    g = b.shape[0]
    ka = a.shape[1] // g
    n = b.shape[2] if dims == _NN else b.shape[1]
    tm = t // MM_TILES
    return _matmul(
        a, b, dims=dims, grid=(g, MM_TILES), a_spec=pl.BlockSpec((tm, ka), lambda g_, i: (i, g_)),
        b_spec=pl.BlockSpec((None,) + b.shape[1:], lambda g_, i: (g_, 0, 0)),
        o_spec=pl.BlockSpec((tm, n), lambda g_, i: (i, g_)), out_shape=jax.ShapeDtypeStruct((t, g * n), out_dtype),
        name=name)


def _mm_tn_cols(a, g, nblk, name):
    t = a.shape[0]
    ka, n = a.shape[1] // nblk, g.shape[1] // nblk
    tm = t // MM_TILES
    return _matmul(
        a, g, dims=_TN, grid=(nblk, MM_TILES), red_axis=1, a_spec=pl.BlockSpec((tm, ka), lambda g_, i: (i, g_)),
        b_spec=pl.BlockSpec((tm, n), lambda g_, i: (i, g_)), o_spec=pl.BlockSpec((None, ka, n), lambda g_, i: (g_, 0, 0)),
        out_shape=jax.ShapeDtypeStruct((nblk, ka, n), F32), name=name)


def _rows(cols, n=None):
    if n is None:
        return pl.BlockSpec((TE, cols), lambda i: (i, 0))
    return pl.BlockSpec((TE, cols), lambda i: (n - 1 - i, 0))


def _whole(shape):
    return pl.BlockSpec(shape, lambda *_: (0,) * len(shape))


def _row_ids(i, rows=TE):
    return i * rows + lax.broadcasted_iota(jnp.int32, (rows, 1), 0)


def _rmsnorm_fwd(h, gain, name):
    t = h.shape[0]

    def body(h_ref, g_ref, u_ref):
        x = h_ref[...]
        r = lax.rsqrt(jnp.mean(x * x, axis=-1, keepdims=True) + EPS)
        u_ref[...] = (x * r * g_ref[...]).astype(BF16)

    return _pallas_call(
        body, name=name, grid=(t // TE,), in_specs=[_rows(D), _whole((1, D))], out_specs=_rows(D),
        out_shape=jax.ShapeDtypeStruct((t, D), BF16), compiler_params=_params(("parallel",), 3 * TE * D * 4),
    )(h, gain)


def _rmsnorm_bwd(x, du, dres, gain, name):
    t = x.shape[0]

    def body(x_ref, du_ref, dr_ref, g_ref, dx_ref, dxb_ref, dg_ref):
        i = pl.program_id(0)
        xv = x_ref[...]
        r = lax.rsqrt(jnp.mean(xv * xv, axis=-1, keepdims=True) + EPS)
        gdy = du_ref[...] * g_ref[...]
        dx = dr_ref[...] + r * gdy - xv * (r * r * r) * jnp.mean(xv * gdy, axis=-1, keepdims=True)
        dx = jnp.where(_row_ids(i) >= LEAD, dx, 0.0)
        dx_ref[...] = dx
        dxb_ref[...] = dx.astype(BF16)
        part = jnp.sum(du_ref[...] * xv * r, axis=0, keepdims=True)

        @pl.when(i == 0)
        def _():
            dg_ref[...] = part

        @pl.when(i > 0)
        def _():
            dg_ref[...] += part

    return _pallas_call(
        body, name=name, grid=(t // TE,), in_specs=[_rows(D), _rows(D), _rows(D), _whole((1, D))],
        out_specs=[_rows(D), _rows(D), _whole((1, D))],
        out_shape=[jax.ShapeDtypeStruct((t, D), F32), jax.ShapeDtypeStruct((t, D), BF16),
                   jax.ShapeDtypeStruct((1, D), F32)],
        compiler_params=_params(("arbitrary",), 5 * TE * D * 4),
    )(x, du, dres, gain)


def _loss_bwd(h, target, gain, name):
    t = h.shape[0]

    def body(h_ref, t_ref, g_ref, dx_ref, dxb_ref, dg_ref, loss_ref):
        i = pl.program_id(0)
        xv = h_ref[...]
        gain_v = g_ref[...]
        r = lax.rsqrt(jnp.mean(xv * xv, axis=-1, keepdims=True) + EPS)
        real = _row_ids(i) >= ROW0
        err = jnp.where(real, xv * r * gain_v - t_ref[...], 0.0)
        dy = err * (1.0 / D)
        gdy = dy * gain_v
        dx = r * gdy - xv * (r * r * r) * jnp.mean(xv * gdy, axis=-1, keepdims=True)
        dx_ref[...] = dx
        dxb_ref[...] = dx.astype(BF16)
        dgp = jnp.sum(dy * xv * r, axis=0, keepdims=True)
        lp = 0.5 * jnp.sum(jnp.mean(err * err, axis=-1, keepdims=True), axis=0, keepdims=True)

        @pl.when(i == 0)
        def _():
            dg_ref[...] = dgp
            loss_ref[...] = jnp.broadcast_to(lp, (1, LANE))

        @pl.when(i > 0)
        def _():
            dg_ref[...] += dgp
            loss_ref[...] += jnp.broadcast_to(lp, (1, LANE))

    return _pallas_call(
        body, name=name, grid=(t // TE,), in_specs=[_rows(D), _rows(D), _whole((1, D))],
        out_specs=[_rows(D), _rows(D), _whole((1, D)), _whole((1, LANE))],
        out_shape=[jax.ShapeDtypeStruct((t, D), F32), jax.ShapeDtypeStruct((t, D), BF16),
                   jax.ShapeDtypeStruct((1, D), F32), jax.ShapeDtypeStruct((1, LANE), F32)],
        compiler_params=_params(("arbitrary",), 4 * TE * D * 4),
    )(h, target, gain)


def _seq_scratch(cols):
    return pltpu.VMEM((-(-cols // LANE), TE + SUB, LANE), F32)


def _taps(scr, c, wd, first, n, k):
    return [scr[c, first - (k - 1) + j:first - (k - 1) + j + n, 0:wd] for j in range(k)]


def _stage_history(scr, i):
    @pl.when(i == 0)
    def _():
        scr[...] = jnp.zeros(scr.shape, F32)

    @pl.when(i > 0)
    def _():
        scr[:, 0:SUB, :] = scr[:, TE:TE + SUB, :]


def _stage_future(scr, i):
    @pl.when(i == 0)
    def _():
        scr[...] = jnp.zeros(scr.shape, F32)

    @pl.when(i > 0)
    def _():
        scr[:, TE:TE + SUB, :] = scr[:, 0:SUB, :]


def _conv(tp, w):
    out = w[0:1] * tp[0]
    for j in range(1, len(tp)):
        out = out + w[j:j + 1] * tp[j]
    return out


def _conv_t(ds, c, wd, w):
    k = w.shape[0]
    out = w[k - 1:k] * ds[c, 0:TE, 0:wd]
    for j in range(k - 1):
        out = out + w[j:j + 1] * ds[c, k - 1 - j:k - 1 - j + TE, 0:wd]
    return out


def _gdn_pre_fwd(x, w, name):
    t = x.shape[0]

    def body(x_ref, w_ref, o_ref, xs):
        _stage_history(xs, pl.program_id(0))
        for hh in range(3 * H):
            sl = slice(hh * DH, (hh + 1) * DH)
            xs[hh, SUB:SUB + TE, :] = x_ref[:, sl].astype(F32)
            cv = _conv(_taps(xs, hh, DH, SUB, TE, 4), w_ref[:, sl])
            s = cv * _sigmoid(cv)
            if hh < 2 * H:
                s = s * lax.rsqrt(jnp.sum(s * s, axis=-1, keepdims=True) + EPS)
                if hh < H:
                    s = s * (DH ** -0.5)
            o_ref[:, sl] = s

    return _pallas_call(
        body, name=name, grid=(t // TE,), in_specs=[_rows(QKV), _whole((4, QKV))], out_specs=_rows(QKV),
        out_shape=jax.ShapeDtypeStruct((t, QKV), F32), scratch_shapes=[_seq_scratch(QKV)],
        compiler_params=_params(("arbitrary",), TE * QKV * 8),
    )(x, w)


def _gdn_pre_bwd(x, w, dqkv, name):
    t = x.shape[0]
    n = t // TE
    hb = TE // 16

    def body(x_ref, xp_ref, w_ref, d_ref, dx_ref, dw_ref, xs, ds):
        i = pl.program_id(0)

        @pl.when(i == 0)
        def _():
            dw_ref[...] = jnp.zeros_like(dw_ref)

        _stage_future(ds, i)
        for hh in range(3 * H):
            sl = slice(hh * DH, (hh + 1) * DH)
            xs[hh, 0:SUB, :] = jnp.where(i == n - 1, 0.0, xp_ref[SUB:2 * SUB, sl].astype(F32))
            xs[hh, SUB:SUB + TE, :] = x_ref[:, sl].astype(F32)
            wv = w_ref[:, sl]
            tp = _taps(xs, hh, DH, SUB, TE, 4)
            cv = _conv(tp, wv)
            sg = _sigmoid(cv)
            s = cv * sg
            dsv = d_ref[:, sl]
            if hh < 2 * H:
                if hh < H:
                    dsv = dsv * (DH ** -0.5)
                r = lax.rsqrt(jnp.sum(s * s, axis=-1, keepdims=True) + EPS)
                dsv = r * dsv - s * (r * r * r) * jnp.sum(s * dsv, axis=-1, keepdims=True)
            dcv = dsv * (sg * (1.0 + cv * (1.0 - sg)))
            ds[hh, 0:TE, :] = dcv
            dx_ref[:, sl] = _conv_t(ds, hh, DH, wv).astype(BF16)
            dw_ref[:, sl] += jnp.concatenate([jnp.sum(tp[j] * dcv, axis=0, keepdims=True) for j in range(4)], axis=0)

    return _pallas_call(
        body, name=name, grid=(n,),
        in_specs=[_rows(QKV, n), pl.BlockSpec((16, QKV), lambda i: (jnp.maximum((n - 1 - i) * hb - 1, 0), 0)),
                  _whole((4, QKV)), _rows(QKV, n)],
        out_specs=[_rows(QKV, n), _whole((4, QKV))],
        out_shape=[jax.ShapeDtypeStruct((t, QKV), BF16), jax.ShapeDtypeStruct((4, QKV), F32)],
        scratch_shapes=[_seq_scratch(QKV), _seq_scratch(QKV)],
        compiler_params=_params(("arbitrary",), TE * QKV * 14),
    )(x, x, w, dqkv)


def _softplus(x):
    return jnp.maximum(x, 0.0) + jnp.log(1.0 + jnp.exp(-jnp.abs(x)))


def _gates_fwd(ba, a_row, dt_row, name):
    t = ba.shape[0]

    def body(ba_ref, a_ref, dt_ref, b_out, g_out):
        real = _row_ids(pl.program_id(0)) >= LEAD
        b_out[...] = jnp.where(real, _sigmoid(ba_ref[:, 0:LANE]), 0.0)
        g = -jnp.exp(a_ref[...]) * _softplus(ba_ref[:, LANE:2 * LANE] + dt_ref[...])
        g_out[...] = jnp.where(real, g, 0.0)

    return _pallas_call(
        body, name=name, grid=(t // TE,), in_specs=[_rows(2 * LANE), _whole((1, LANE)), _whole((1, LANE))],
        out_specs=[_rows(LANE), _rows(LANE)],
        out_shape=[jax.ShapeDtypeStruct((t, LANE), F32), jax.ShapeDtypeStruct((t, LANE), F32)],
        compiler_params=_params(("parallel",), TE * LANE * 16),
    )(ba, a_row, dt_row)


def _gates_bwd(ba, a_row, dt_row, dbeta, dg, name):
    t = ba.shape[0]

    def body(ba_ref, a_ref, dt_ref, db_ref, dg_ref, dba_ref, da_out, ddt_out):
        i = pl.program_id(0)
        real = _row_ids(i) >= LEAD
        beta = _sigmoid(ba_ref[:, 0:LANE])
        draw_b = jnp.where(real, db_ref[...] * beta * (1.0 - beta), 0.0)
        pre = ba_ref[:, LANE:2 * LANE] + dt_ref[...]
        neg_a = -jnp.exp(a_ref[...])
        dgv = jnp.where(real, dg_ref[...], 0.0)
        draw_a = dgv * neg_a * _sigmoid(pre)
        dba_ref[:, 0:LANE] = draw_b.astype(BF16)
        dba_ref[:, LANE:2 * LANE] = draw_a.astype(BF16)
        dal = jnp.sum(dgv * neg_a * _softplus(pre), axis=0, keepdims=True)
        ddt = jnp.sum(draw_a, axis=0, keepdims=True)

        @pl.when(i == 0)
        def _():
            da_out[...] = dal
            ddt_out[...] = ddt

        @pl.when(i > 0)
        def _():
            da_out[...] += dal
            ddt_out[...] += ddt

    return _pallas_call(
        body, name=name, grid=(t // TE,),
        in_specs=[_rows(2 * LANE), _whole((1, LANE)), _whole((1, LANE)), _rows(LANE), _rows(LANE)],
        out_specs=[_rows(2 * LANE), _whole((1, LANE)), _whole((1, LANE))],
        out_shape=[jax.ShapeDtypeStruct((t, 2 * LANE), BF16), jax.ShapeDtypeStruct((1, LANE), F32),
                   jax.ShapeDtypeStruct((1, LANE), F32)],
        compiler_params=_params(("arbitrary",), TE * LANE * 24),
    )(ba, a_row, dt_row, dbeta, dg)


_OFFSETS = [(dx, dy, dc) for dx in (0, 1) for dy in (0, 1) for dc in (0, 1)][1:]
NPEER = len(_OFFSETS)
ANY_SPEC = pl.BlockSpec(memory_space=pl.ANY)


def _place():
    return lax.axis_index("x"), lax.axis_index("y"), lax.axis_index("c")


def _index(p):
    return 4 * p[0] + 2 * p[1] + p[2]


def _comm_scratch(n):
    return [pltpu.SemaphoreType.DMA((n * NPEER,)), pltpu.SemaphoreType.DMA((n * NPEER,)), pltpu.SemaphoreType.DMA((n,))]


def _scatter_copies(ins, outs, send, recv):
    me = _place()
    mi = _index(me)
    res = []
    for j, d in enumerate(_OFFSETS):
        peer = tuple(1 - v if bit else v for v, bit in zip(me, d))
        pi = _index(peer)
        for k in range(len(ins)):
            sem = k * NPEER + j
            mine = pltpu.make_async_remote_copy(src_ref=ins[k].at[pi], dst_ref=outs[k].at[mi], send_sem=send.at[sem],
                                                recv_sem=recv.at[sem], device_id=peer, device_id_type=MESH)
            theirs = pltpu.make_async_remote_copy(src_ref=ins[k].at[pi], dst_ref=outs[k].at[pi], send_sem=send.at[sem],
                                                  recv_sem=recv.at[sem], device_id=peer, device_id_type=MESH)
            res.append((mine, theirs))
    return res


def _scatter_own(ins, outs, loc):
    mi = _index(_place())
    return [pltpu.make_async_copy(ins[k].at[mi], outs[k].at[mi], loc.at[k]) for k in range(len(ins))]


def _scatter_start(ins, outs, send, recv, loc):
    for cp in _scatter_own(ins, outs, loc):
        cp.start()
    for mine, _ in _scatter_copies(ins, outs, send, recv):
        mine.start()


def _scatter_wait(ins, outs, send, recv, loc):
    cps = _scatter_copies(ins, outs, send, recv)
    for _, theirs in cps:
        theirs.wait_recv()
    for mine, _ in cps:
        mine.wait_send()
    for cp in _scatter_own(ins, outs, loc):
        cp.wait()


def _gather_parts(ins, outs, send, recv):
    x, y, c = _place()
    chips = [(1 - x, y), (x, 1 - y), (1 - x, 1 - y)]

    def cp(k, slot, src, block, to):
        return pltpu.make_async_remote_copy(src_ref=src, dst_ref=outs[k].at[_index(block)], send_sem=send.at[k * NPEER + slot],
                                            recv_sem=recv.at[k * NPEER + slot], device_id=to, device_id_type=MESH)

    return (x, y, c), (x, y, 1 - c), chips, cp


def _gather_start(ins, outs, send, recv, loc):
    me, sib, chips, cp = _gather_parts(ins, outs, send, recv)
    for k in range(len(ins)):
        pltpu.make_async_copy(ins[k], outs[k].at[_index(me)], loc.at[k]).start()
        cp(k, 0, ins[k], me, sib).start()
        for j, chip in enumerate(chips):
            cp(k, 1 + j, ins[k], me, (*chip, me[2])).start()


def _gather_forward(ins, outs, send, recv, loc):
    me, sib, chips, cp = _gather_parts(ins, outs, send, recv)
    for j, chip in enumerate(chips):
        blk = (*chip, me[2])
        for k in range(len(ins)):
            cp(k, 1 + j, ins[k], blk, me).wait_recv()
            cp(k, 4 + j, outs[k].at[_index(blk)], blk, sib).start()


def _gather_finish(ins, outs, send, recv, loc):
    me, sib, chips, cp = _gather_parts(ins, outs, send, recv)
    for k in range(len(ins)):
        cp(k, 0, ins[k], sib, me).wait_recv()
        for j, chip in enumerate(chips):
            cp(k, 4 + j, ins[k], (*chip, sib[2]), me).wait_recv()
        cp(k, 0, ins[k], me, sib).wait_send()
        for j, chip in enumerate(chips):
            cp(k, 1 + j, ins[k], me, (*chip, me[2])).wait_send()
            cp(k, 4 + j, outs[k].at[_index((*chip, me[2]))], (*chip, me[2]), sib).wait_send()
        pltpu.make_async_copy(ins[k], outs[k].at[_index(me)], loc.at[k]).wait()


def _gathered_shapes(arrs):
    return [jax.ShapeDtypeStruct((NDEV,) + a.shape, a.dtype) for a in arrs]


def _gather(arrs, name):
    n = len(arrs)

    def body(*refs):
        ins, outs, sems = refs[:n], refs[n:2 * n], refs[2 * n:]
        _gather_start(ins, outs, *sems)
        _gather_forward(ins, outs, *sems)
        _gather_finish(ins, outs, *sems)

    return _pallas_call(body, name=name, in_specs=[ANY_SPEC] * n, out_specs=[ANY_SPEC] * n,
                          out_shape=_gathered_shapes(arrs), scratch_shapes=_comm_scratch(n))(*arrs)


def _scatter(arrs, name):
    n = len(arrs)

    def body(*refs):
        ins, outs, sems = refs[:n], refs[n:2 * n], refs[2 * n:]
        _scatter_start(ins, outs, *sems)
        _scatter_wait(ins, outs, *sems)

    return _pallas_call(body, name=name, in_specs=[ANY_SPEC] * n, out_specs=[ANY_SPEC] * n,
                          out_shape=[jax.ShapeDtypeStruct(a.shape, a.dtype) for a in arrs],
                          scratch_shapes=_comm_scratch(n))(*arrs)


_BNN = (((2,), (1,)), ((0,), (0,)))
_BNT = (((2,), (2,)), ((0,), (0,)))
_BTN = (((1,), (1,)), ((0,), (0,)))


def _split(a):
    hi = a.astype(BF16)
    return hi, (a - hi.astype(F32)).astype(BF16)


def _dot3(a, b, dims=_BNN):
    ah, al = _split(a)
    bh, bl = _split(b)
    if dims != _BNN:
        return _dot(ah, bh, dims) + _dot(al, bh, dims) + _dot(ah, bl, dims)
    m = a.shape[1]
    r = _dot(jnp.concatenate([ah, al], axis=1), bh, _BNN)
    return r[:, :m] + r[:, m:] + _dot(ah, bl, _BNN)


def _tri_sum(mask, x):
    x1 = x.astype(BF16)
    r1 = x - x1.astype(F32)
    x2 = r1.astype(BF16)
    x3 = (r1 - x2.astype(F32)).astype(BF16)
    mb = mask.astype(BF16)
    return _dot(mb, x1) + _dot(mb, x2) + _dot(mb, x3)


GC = 3


def _rows_of(c):
    return slice(c * CH, (c + 1) * CH)


def _heads(ref, off):
    return jnp.stack([ref[_rows_of(c), off + h * DH:off + (h + 1) * DH] for c in range(GC) for h in range(H)])


def _cols(arrs):
    return jnp.stack([a[:, h:h + 1] for a in arrs for h in range(H)])


def _lanes(a):
    lane = lax.broadcasted_iota(jnp.int32, (CH, LANE), 1)
    out = jnp.zeros((CH, LANE), F32)
    for h in range(H):
        out = jnp.where(lane == h, a[h], out)
    return out


def _chunk_prep(qkv_ref, b_ref, g_ref):
    row = lax.broadcasted_iota(jnp.int32, (CH, CH), 0)
    col = lax.broadcasted_iota(jnp.int32, (CH, CH), 1)
    incl, strict = row >= col, row > col
    gcs = [_tri_sum(incl, g_ref[_rows_of(c), :]) for c in range(GC)]
    q, k, v = _heads(qkv_ref, 0), _heads(qkv_ref, D), _heads(qkv_ref, 2 * D)
    bcol, gcol = _cols([b_ref[_rows_of(c), :] for c in range(GC)]), _cols(gcs)
    grow = jnp.stack([gct[h:h + 1, :] for gct in [gc.T for gc in gcs] for h in range(H)])
    glast = _cols([gc[CH - 1:CH, :] for gc in gcs])
    dec = jnp.exp(jnp.where(incl[None], gcol - grow, -1e30))
    kb = k * bcol
    ab = _bdot(jnp.concatenate([kb, q], axis=1), k, _BNT)
    egc, ekc = jnp.exp(gcol), jnp.exp(glast - gcol)
    return dict(row=row, col=col, strict=strict[None], q=q, k=k, v=v, bcol=bcol, dec=dec, kb=kb,
                lm=jnp.where(strict[None], ab[:, :CH] * dec, 0.0), qk=ab[:, CH:] * dec, egc=egc, ekc=ekc,
                gth=jnp.exp(glast), qd=q * egc, kd=k * ekc, vb=v * bcol, kbg=kb * egc)


def _unit_lower_inverse(lm, eye):
    n = -lm
    x = eye + n
    pw = _dot3(n, n)
    for it in range(5):
        if it < 4:
            xp = _dot3(jnp.concatenate([x, pw], axis=1), pw)
            x = x + xp[:, :CH]
            pw = xp[:, CH:]
        else:
            x = x + _dot3(x, pw)
    return x


def _gdn_fwd(qkv, beta, g, name, gather=()):
    t = qkv.shape[0]
    nc = t // CH
    ns = nc // GC
    ng = len(gather)

    def body(qkv_ref, b_ref, g_ref, *rest):
        c_ins, (o_ref, sin_ref, vn_ref, ti_ref, w_ref) = rest[:ng], rest[ng:ng + 5]
        c_outs, state, sems = rest[ng + 5:2 * ng + 5], rest[2 * ng + 5], rest[2 * ng + 6:]
        step = pl.program_id(0)

        @pl.when(step == 0)
        def _():
            state[...] = jnp.zeros_like(state)
            if ng:
                _gather_start(c_ins, c_outs, *sems)

        if ng:
            @pl.when(step == max(ns - 4, 0))
            def _():
                _gather_forward(c_ins, c_outs, *sems)

            @pl.when(step == ns - 1)
            def _():
                _gather_finish(c_ins, c_outs, *sems)

        pr = _chunk_prep(qkv_ref, b_ref, g_ref)
        tinv = _unit_lower_inverse(pr["lm"], (pr["row"] == pr["col"]).astype(F32)[None])
        uw = _bdot(tinv, jnp.concatenate([pr["vb"], pr["kbg"]], axis=2), _BNN)
        u, w = uw[:, :, :DH], uw[:, :, DH:]
        s = state[...]
        for c in range(GC):
            hs = slice(c * H, (c + 1) * H)
            ws = _bdot(jnp.concatenate([w[hs], pr["qd"][hs]], axis=1), s, _BNN)
            vn = u[hs] - ws[:, :CH]
            o = ws[:, CH:] + _bdot(pr["qk"][hs], vn, _BNN)
            sin_ref[c] = s
            ti_ref[c] = tinv[hs]
            s = s * pr["gth"][hs] + _bdot(pr["kd"][hs], vn, _BTN)
            for h in range(H):
                sl = slice(h * DH, (h + 1) * DH)
                o_ref[_rows_of(c), sl] = o[h]
                vn_ref[_rows_of(c), sl] = vn[h]
                w_ref[_rows_of(c), sl] = w[c * H + h]
        state[...] = s

    chunk = lambda cols: pl.BlockSpec((GC * CH, cols), lambda c: (c, 0))
    outs = _pallas_call(
        body, name=name, grid=(ns,), in_specs=[chunk(QKV), chunk(LANE), chunk(LANE)] + [ANY_SPEC] * ng,
        out_specs=[chunk(D), pl.BlockSpec((GC, H, DH, DH), lambda c: (c, 0, 0, 0)), chunk(D),
                   pl.BlockSpec((GC, H, CH, CH), lambda c: (c, 0, 0, 0)), chunk(D)] + [ANY_SPEC] * ng,
        out_shape=[jax.ShapeDtypeStruct((t, D), F32), jax.ShapeDtypeStruct((nc, H, DH, DH), F32),
                   jax.ShapeDtypeStruct((t, D), F32), jax.ShapeDtypeStruct((nc, H, CH, CH), F32),
                   jax.ShapeDtypeStruct((t, D), F32)] + _gathered_shapes(gather),
        scratch_shapes=[pltpu.VMEM((H, DH, DH), F32)] + (_comm_scratch(ng) if ng else []),
        compiler_params=_params(("arbitrary",), 12 << 20),
    )(qkv, beta, g, *gather)
    return outs[:5], outs[5:]


def _gdn_bwd(qkv, beta, g, do, s_in, vnew, tinv, wsv, name, scatter=()):
    t = qkv.shape[0]
    nsteps = t // CH // GC
    ns = len(scatter)

    def body(qkv_ref, b_ref, g_ref, do_ref, sin_ref, vn_ref, ti_ref, w_ref, *rest):
        c_ins, (dqkv_ref, db_ref, dg_ref) = rest[:ns], rest[ns:ns + 3]
        c_outs, dstate, sems = rest[ns + 3:2 * ns + 3], rest[2 * ns + 3], rest[2 * ns + 4:]
        step = pl.program_id(0)

        @pl.when(step == 0)
        def _():
            dstate[...] = jnp.zeros_like(dstate)
            if ns:
                _scatter_start(c_ins, c_outs, *sems)

        if ns:
            @pl.when(step == nsteps - 1)
            def _():
                _scatter_wait(c_ins, c_outs, *sems)

        pr = _chunk_prep(qkv_ref, b_ref, g_ref)
        ti = jnp.concatenate([ti_ref[c] for c in range(GC)], axis=0)
        s = jnp.concatenate([sin_ref[c] for c in range(GC)], axis=0)
        w, vn, doh = _heads(w_ref, 0), _heads(vn_ref, 0), _heads(do_ref, 0)
        dqd = _bdot(doh, s, _BNT)
        dqk = _bdot(doh, vn, _BNT)
        qk_do = _bdot(pr["qk"], doh, _BTN)
        qd_do = _bdot(pr["qd"], doh, _BTN)
        ds = dstate[...]
        dvn_c, dkd_c, dw_c, dgt_c = [None] * GC, [None] * GC, [None] * GC, [None] * GC
        for c in reversed(range(GC)):
            hs = slice(c * H, (c + 1) * H)
            dvn_c[c] = _bdot(pr["kd"][hs], ds, _BNN) + qk_do[hs]
            dkd_c[c] = _bdot(vn[hs], ds, _BNT)
            dw_c[c] = -_bdot(dvn_c[c], s[hs], _BNT)
            dgt_c[c] = jnp.sum(jnp.sum(ds * s[hs], axis=2, keepdims=True), axis=1, keepdims=True)
            ds = ds * pr["gth"][hs] + qd_do[hs] - _bdot(w[hs], dvn_c[c], _BTN)
        dstate[...] = ds
        dvn, dkd, dw, dgt = (jnp.concatenate(parts, axis=0) for parts in (dvn_c, dkd_c, dw_c, dgt_c))
        duw = jnp.concatenate([dvn, dw], axis=2)
        dvk = _bdot(ti, duw, _BTN)
        dvb, dkbg = dvk[:, :, :DH], dvk[:, :, DH:]
        dti = _bdot(duw, jnp.concatenate([pr["vb"], pr["kbg"]], axis=2), _BNT)
        dl = -_dot3(_dot3(ti, dti, _BTN), ti, _BNT)
        dl = jnp.where(pr["strict"], dl, 0.0)
        dab = jnp.concatenate([dl * pr["dec"], dqk * pr["dec"]], axis=1)
        r1 = _bdot(dab, pr["k"], _BNN)
        dkb = r1[:, :CH] + dkbg * pr["egc"]
        dq = r1[:, CH:] + dqd * pr["egc"]
        dk = _bdot(dab, jnp.concatenate([pr["kb"], pr["q"]], axis=1), _BTN) + dkb * pr["bcol"] + dkd * pr["ekc"]
        m = dl * pr["lm"] + dqk * pr["qk"]
        mh, ml = _split(m)
        ones = jnp.ones((GC * H, CH, LANE), BF16)
        colsum = (_dot(mh, ones, _BTN) + _dot(ml, ones, _BTN))[:, :, 0:1]
        kdsum = jnp.sum(dkd * pr["kd"], axis=2, keepdims=True)
        dgc = (jnp.sum(m, axis=2, keepdims=True) - colsum + jnp.sum(dkbg * pr["kbg"], axis=2, keepdims=True)
               + jnp.sum(dqd * pr["qd"], axis=2, keepdims=True) - kdsum)
        dglast = jnp.sum(kdsum, axis=1, keepdims=True) + dgt * pr["gth"]
        last_row = lax.broadcasted_iota(jnp.int32, (1, CH, 1), 1) == CH - 1
        dgc = dgc + jnp.where(last_row, dglast, 0.0)
        dbeta = jnp.sum(dkb * pr["k"], axis=2, keepdims=True) + jnp.sum(dvb * pr["v"], axis=2, keepdims=True)
        dv = dvb * pr["bcol"]
        upper = pr["row"] <= pr["col"]
        for c in range(GC):
            hs = slice(c * H, (c + 1) * H)
            for h in range(H):
                dqkv_ref[_rows_of(c), h * DH:(h + 1) * DH] = dq[c * H + h]
                dqkv_ref[_rows_of(c), D + h * DH:D + (h + 1) * DH] = dk[c * H + h]
                dqkv_ref[_rows_of(c), 2 * D + h * DH:2 * D + (h + 1) * DH] = dv[c * H + h]
            db_ref[_rows_of(c), :] = _lanes(dbeta[hs])
            dg_ref[_rows_of(c), :] = _tri_sum(upper, _lanes(dgc[hs]))

    chunk = lambda cols: pl.BlockSpec((GC * CH, cols), lambda c: (nsteps - 1 - c, 0))
    sq = lambda a, b: pl.BlockSpec((GC, H, a, b), lambda c: (nsteps - 1 - c, 0, 0, 0))
    outs = _pallas_call(
        body, name=name, grid=(nsteps,),
        in_specs=[chunk(QKV), chunk(LANE), chunk(LANE), chunk(D), sq(DH, DH), chunk(D), sq(CH, CH), chunk(D)] + [ANY_SPEC] * ns,
        out_specs=[chunk(QKV), chunk(LANE), chunk(LANE)] + [ANY_SPEC] * ns,
        out_shape=[jax.ShapeDtypeStruct((t, QKV), F32), jax.ShapeDtypeStruct((t, LANE), F32),
                   jax.ShapeDtypeStruct((t, LANE), F32)] + [jax.ShapeDtypeStruct(a.shape, a.dtype) for a in scatter],
        scratch_shapes=[pltpu.VMEM((H, DH, DH), F32)] + (_comm_scratch(ns) if ns else []),
        compiler_params=_params(("arbitrary",), 16 << 20),
    )(qkv, beta, g, do, s_in, vnew, tinv, wsv, *scatter)
    return outs[:3], outs[3:]


def _pool_counts(row_ids, win):
    return jnp.minimum(jnp.maximum(row_ids - LEAD, 0) + 1, win).astype(F32)


def _pool_fwd(p, name):
    t = p.shape[0]
    ext = TE + 16

    def body(p_ref, o_ref, carry):
        i = pl.program_id(0)

        @pl.when(i == 0)
        def _():
            carry[...] = jnp.zeros_like(carry)

        ids = _row_ids(i)
        for gi, win in enumerate(POOL_WINDOWS):
            sl = slice(gi * LANE, (gi + 1) * LANE)
            xv = p_ref[:, sl]
            s = jnp.concatenate([carry[:, sl], xv], axis=0)
            sh = 1
            while sh < win:
                s = s + pltpu.roll(s, sh, 0)
                sh *= 2
            o_ref[:, sl] = (s[16:ext] / _pool_counts(ids, win) - xv).astype(BF16)
            carry[:, sl] = xv[TE - 16:TE]

    return _pallas_call(
        body, name=name, grid=(t // TE,), in_specs=[_rows(POOL_W)], out_specs=_rows(POOL_W),
        out_shape=jax.ShapeDtypeStruct((t, POOL_W), BF16), scratch_shapes=[pltpu.VMEM((16, POOL_W), F32)],
        compiler_params=_params(("arbitrary",), TE * POOL_W * 8),
    )(p)


def _pool_bwd(dpo, name):
    t = dpo.shape[0]
    n = t // TE
    ext = TE + 16

    def body(d_ref, o_ref, carry):
        i = pl.program_id(0)

        @pl.when(i == 0)
        def _():
            carry[...] = jnp.zeros_like(carry)

        ids = _row_ids(n - 1 - i)
        for gi, win in enumerate(POOL_WINDOWS):
            sl = slice(gi * LANE, (gi + 1) * LANE)
            dv = d_ref[:, sl]
            rv = dv / _pool_counts(ids, win)
            s = jnp.concatenate([rv, carry[:, sl]], axis=0)
            sh = 1
            while sh < win:
                s = s + pltpu.roll(s, ext - sh, 0)
                sh *= 2
            o_ref[:, sl] = (s[0:TE] - dv).astype(BF16)
            carry[:, sl] = rv[0:16]

    return _pallas_call(
        body, name=name, grid=(n,), in_specs=[_rows(POOL_W, n)], out_specs=_rows(POOL_W, n),
        out_shape=jax.ShapeDtypeStruct((t, POOL_W), BF16), scratch_shapes=[pltpu.VMEM((16, POOL_W), F32)],
        compiler_params=_params(("arbitrary",), TE * POOL_W * 8),
    )(dpo)


def _post_fwd(o, z, gate, pm, hn, ps, name):
    t = o.shape[0]

    def body(o_ref, z_ref, g_ref, pm_ref, hn_ref, ps_ref, y_ref):
        for h in range(H):
            sl = slice(h * DH, (h + 1) * DH)
            ov = o_ref[:, sl]
            zv = z_ref[:, sl].astype(F32)
            r = lax.rsqrt(jnp.mean(ov * ov, axis=-1, keepdims=True) + EPS)
            ya = ov * r * hn_ref[...] * (zv * _sigmoid(zv))
            ga = _sigmoid(g_ref[:, sl].astype(F32))
            gb = _sigmoid(g_ref[:, D + h * DH:D + (h + 1) * DH].astype(F32))
            y_ref[:, sl] = (ga * ya + gb * (pm_ref[:, sl] * ps_ref[:, sl])).astype(BF16)

    return _pallas_call(
        body, name=name, grid=(t // TE,),
        in_specs=[_rows(D), _rows(D), _rows(2 * D), _rows(D), _whole((1, DH)), _whole((1, D))], out_specs=_rows(D),
        out_shape=jax.ShapeDtypeStruct((t, D), BF16), compiler_params=_params(("parallel",), TE * D * 16),
    )(o, z, gate, pm, hn, ps)


def _post_bwd(dy, o, z, gate, pm, hn, ps, name):
    t = o.shape[0]

    def body(dy_ref, o_ref, z_ref, g_ref, pm_ref, hn_ref, ps_ref, do_ref, dz_ref, dgate_ref, dpm_ref, dhn_ref, dps_ref):
        i = pl.program_id(0)

        @pl.when(i == 0)
        def _():
            dhn_ref[...] = jnp.zeros_like(dhn_ref)
            dps_ref[...] = jnp.zeros_like(dps_ref)

        hnv = hn_ref[...]
        dhn = jnp.zeros((1, DH), F32)
        for h in range(H):
            sl = slice(h * DH, (h + 1) * DH)
            slb = slice(D + h * DH, D + (h + 1) * DH)
            dyv = dy_ref[:, sl]
            ov = o_ref[:, sl]
            zv = z_ref[:, sl].astype(F32)
            r = lax.rsqrt(jnp.mean(ov * ov, axis=-1, keepdims=True) + EPS)
            sz = _sigmoid(zv)
            silu = zv * sz
            on = ov * r
            ya = on * hnv * silu
            ga = _sigmoid(g_ref[:, sl].astype(F32))
            gb = _sigmoid(g_ref[:, slb].astype(F32))
            pmv = pm_ref[:, sl]
            psv = ps_ref[:, sl]
            dya = dyv * ga
            dyb = dyv * gb
            dgate_ref[:, sl] = (dyv * ya * ga * (1.0 - ga)).astype(BF16)
            dgate_ref[:, slb] = (dyv * (pmv * psv) * gb * (1.0 - gb)).astype(BF16)
            tt = dya * hnv * silu
            do_ref[:, sl] = r * tt - ov * (r * r * r) * jnp.mean(ov * tt, axis=-1, keepdims=True)
            dz_ref[:, sl] = (dya * on * hnv * (sz * (1.0 + zv * (1.0 - sz)))).astype(BF16)
            dhn = dhn + jnp.sum(dya * on * silu, axis=0, keepdims=True)
            dps_ref[:, sl] += jnp.sum(dyb * pmv, axis=0, keepdims=True)
            dpm_ref[:, sl] = (dyb * psv).astype(BF16)
        dhn_ref[...] += dhn

    return _pallas_call(
        body, name=name, grid=(t // TE,),
        in_specs=[_rows(D), _rows(D), _rows(D), _rows(2 * D), _rows(D), _whole((1, DH)), _whole((1, D))],
        out_specs=[_rows(D), _rows(D), _rows(2 * D), _rows(D), _whole((1, DH)), _whole((1, D))],
        out_shape=[jax.ShapeDtypeStruct((t, D), F32), jax.ShapeDtypeStruct((t, D), BF16),
                   jax.ShapeDtypeStruct((t, 2 * D), BF16), jax.ShapeDtypeStruct((t, D), BF16),
                   jax.ShapeDtypeStruct((1, DH), F32), jax.ShapeDtypeStruct((1, D), F32)],
        compiler_params=_params(("arbitrary",), TE * D * 28),
    )(dy, o, z, gate, pm, hn, ps)


_FB_COLS = [(c, min(c + LANE, FB)) for c in range(0, FB, LANE)]


def _mlp_act_fwd(hid, cw, name):
    t = hid.shape[1]
    n = t // TE

    def body(hg_ref, hv_ref, wg_ref, wv_ref, a_ref, xg, xv):
        i = pl.program_id(1)
        _stage_history(xg, i)
        _stage_history(xv, i)
        for c, (c0, c1) in enumerate(_FB_COLS):
            sl, wd = slice(c0, c1), c1 - c0
            xg[c, SUB:SUB + TE, 0:wd] = hg_ref[:, sl].astype(F32)
            xv[c, SUB:SUB + TE, 0:wd] = hv_ref[:, sl].astype(F32)
            gg = _conv(_taps(xg, c, wd, SUB, TE, 3), wg_ref[:, sl])
            vv = _conv(_taps(xv, c, wd, SUB, TE, 3), wv_ref[:, sl])
            a_ref[:, sl] = (gg * _sigmoid(gg) * vv).astype(BF16)

    hspec = lambda off: pl.BlockSpec((None, TE, FB), lambda p, i: (p + off, i, 0))
    wspec = lambda off: pl.BlockSpec((None, 3, FB), lambda p, i: (p + off, 0, 0))
    return _pallas_call(
        body, name=name, grid=(4, n), in_specs=[hspec(0), hspec(4), wspec(0), wspec(4)],
        out_specs=pl.BlockSpec((None, TE, FB), lambda p, i: (p, i, 0)),
        out_shape=jax.ShapeDtypeStruct((4, t, FB), BF16),
        scratch_shapes=[_seq_scratch(FB), _seq_scratch(FB)],
        compiler_params=_params(("parallel", "arbitrary"), TE * FB * 12),
    )(hid, hid, cw, cw)


def _mlp_act_bwd(da, hid, cw, name):
    t = hid.shape[1]
    n = t // TE
    hb = TE // 16

    def body(da_ref, hg_ref, hv_ref, pg_ref, pv_ref, wg_ref, wv_ref, dhg_ref, dhv_ref, dwg_ref, dwv_ref, xg, xv, dg, dv):
        i = pl.program_id(1)

        @pl.when(i == 0)
        def _():
            dwg_ref[...] = jnp.zeros_like(dwg_ref)
            dwv_ref[...] = jnp.zeros_like(dwv_ref)

        _stage_future(dg, i)
        _stage_future(dv, i)
        first_tile = i == n - 1
        for c, (c0, c1) in enumerate(_FB_COLS):
            sl, wd = slice(c0, c1), c1 - c0
            for scr, p_ref, h_ref in ((xg, pg_ref, hg_ref), (xv, pv_ref, hv_ref)):
                scr[c, 0:SUB, 0:wd] = jnp.where(first_tile, 0.0, p_ref[SUB:2 * SUB, sl].astype(F32))
                scr[c, SUB:SUB + TE, 0:wd] = h_ref[:, sl].astype(F32)
            wg = wg_ref[:, sl]
            wv = wv_ref[:, sl]
            tg = _taps(xg, c, wd, SUB, TE, 3)
            tv = _taps(xv, c, wd, SUB, TE, 3)
            gg = _conv(tg, wg)
            vv = _conv(tv, wv)
            sg = _sigmoid(gg)
            dav = da_ref[:, sl].astype(F32)
            dgg = dav * vv * (sg * (1.0 + gg * (1.0 - sg)))
            dvv = dav * (gg * sg)
            for dc, tp, w, scr, dh_ref, dw_ref in ((dgg, tg, wg, dg, dhg_ref, dwg_ref), (dvv, tv, wv, dv, dhv_ref, dwv_ref)):
                scr[c, 0:TE, 0:wd] = dc
                dh_ref[:, sl] = _conv_t(scr, c, wd, w).astype(BF16)
                dw_ref[:, sl] += jnp.concatenate([jnp.sum(tp[j] * dc, axis=0, keepdims=True) for j in range(3)], axis=0)

    rev = lambda off: pl.BlockSpec((None, TE, FB), lambda p, i: (p + off, n - 1 - i, 0))
    halo = lambda off: pl.BlockSpec((None, 16, FB), lambda p, i: (p + off, jnp.maximum((n - 1 - i) * hb - 1, 0), 0))
    wspec = lambda off: pl.BlockSpec((None, 3, FB), lambda p, i: (p + off, 0, 0))
    dwspec = pl.BlockSpec((None, 3, FB), lambda p, i: (p, 0, 0))
    return _pallas_call(
        body, name=name, grid=(4, n), in_specs=[rev(0), rev(0), rev(4), halo(0), halo(4), wspec(0), wspec(4)],
        out_specs=[rev(0), rev(0), dwspec, dwspec],
        out_shape=[jax.ShapeDtypeStruct((4, t, FB), BF16), jax.ShapeDtypeStruct((4, t, FB), BF16),
                   jax.ShapeDtypeStruct((4, 3, FB), F32), jax.ShapeDtypeStruct((4, 3, FB), F32)],
        scratch_shapes=[_seq_scratch(FB)] * 4,
        compiler_params=_params(("parallel", "arbitrary"), TE * FB * 24),
    )(da, hid, hid, hid, hid, cw, cw)


def _adamw(lands, w, m, v, name):
    nl, r, c = w.shape
    tr = r
    if r * c * 4 > (2 << 20):
        for cand in (128, 64, 32, 16):
            if r % cand == 0:
                tr = cand
                break
    nr = r // tr
    c1 = 1.0 - ADAM_B1 ** ADAM_STEP
    c2 = 1.0 - ADAM_B2 ** ADAM_STEP

    def body(*refs):
        l_refs, (w_ref, m_ref, v_ref, g_out, d_out, m_out, v_out) = refs[:nl], refs[nl:]
        layer = pl.program_id(0)
        g = None
        for l, l_ref in enumerate(l_refs):
            gl = l_ref[0].astype(F32)
            for i in range(1, NDEV):
                gl = gl + l_ref[i].astype(F32)
            g = gl if g is None else jnp.where(layer == l, gl, g)
        mn = ADAM_B1 * m_ref[...] + (1.0 - ADAM_B1) * g
        vn = ADAM_B2 * v_ref[...] + (1.0 - ADAM_B2) * (g * g)
        g_out[...] = g
        m_out[...] = mn
        v_out[...] = vn
        d_out[...] = -ADAM_LR * ((mn / c1) / (jnp.sqrt(vn / c2) + ADAM_EPS) + ADAM_WD * w_ref[...])

    def land_spec(l):
        return pl.BlockSpec((NDEV, tr, c), lambda ly, i: (0, jnp.where(ly == l, i, 0 if l > 0 else nr - 1), 0))

    spec = pl.BlockSpec((None, tr, c), lambda ly, i: (ly, i, 0))
    shp = jax.ShapeDtypeStruct((nl, r, c), F32)
    return _pallas_call(
        body, name=name, grid=(nl, nr), in_specs=[land_spec(l) for l in range(nl)] + [spec, spec, spec],
        out_specs=[spec] * 4, out_shape=[shp] * 4,
        compiler_params=_params(("arbitrary", "arbitrary"), (11 + 4 * nl) * tr * c * 4),
    )(*lands, w, m, v)


def _layer_fwd(h, p, tag, gather=(), finish=None):
    u = _rmsnorm_fwd(h, p["norm_mix"], f"norm_mix_{tag}")
    qkv_pre = _mm(u, p["w_qkv"], BF16, f"proj_qkv_{tag}")
    z = _mm(u, p["w_z"], BF16, f"proj_z_{tag}")
    ba = _mm(u, p["w_ba"], F32, f"proj_ba_{tag}")
    pool_in = _mm(u, p["w_pl"], F32, f"proj_pool_{tag}")
    gate = _mm(u, p["w_gate"], BF16, f"proj_gate_{tag}")
    qkv = _gdn_pre_fwd(qkv_pre, p["conv_qkv"], f"gdn_pre_{tag}")
    beta, g = _gates_fwd(ba, p["a_row"], p["dt_row"], f"gates_{tag}")
    (o, s_in, vnew, tinv, wsv), gathered = _gdn_fwd(qkv, beta, g, f"gdn_{tag}", gather)
    if finish is not None:
        p = {**p, **finish(gathered)}
    pooled = _pool_fwd(pool_in, f"pool_{tag}")
    pm = _mm_cols(pooled, p["w_pool"], F32, f"pool_mm_{tag}", _NN)
    y = _post_fwd(o, z, gate, pm, p["head_norm"], p["pool_scale"], f"post_{tag}")
    h1 = _mm(y, p["w_out"], F32, f"out_proj_{tag}", res=h)
    u2 = _rmsnorm_fwd(h1, p["norm_ffn"], f"norm_ffn_{tag}")
    hid = _mm_up(u2, p["w_up"], f"up_proj_{tag}")
    act = _mlp_act_fwd(hid, p["conv_ffn"], f"mlp_act_{tag}")
    h2 = _mm_blocks_red(act, p["w_down"], f"down_proj_{tag}", _NN, res=h1)
    saved = dict(h=h, u=u, qkv_pre=qkv_pre, z=z, ba=ba, gate=gate, qkv=qkv, beta=beta, g=g, o=o, s_in=s_in, vnew=vnew,
                 tinv=tinv, wsv=wsv, pooled=pooled, pm=pm, y=y, h1=h1, u2=u2, hid=hid, act=act)
    return h2, saved, gathered, p


def _layer_bwd(dh, dh_b, p, s, tag, scatter=()):
    gr = {}
    da = _mm_to_blocks(dh_b, p["w_down"], f"d_act_{tag}")
    gr["w_down"] = _mm_tn_blocks(s["act"], dh_b, f"dw_down_{tag}", True, False, BF16)
    dhg, dhv, dwg, dwv = _mlp_act_bwd(da, s["hid"], p["conv_ffn"], f"mlp_act_bwd_{tag}")
    gr["conv_ffn"] = jnp.concatenate([dwg, dwv], axis=0)
    w_up = p["w_up"]
    du2 = _mm_du2(dhg, dhv, w_up, f"d_u2_{tag}")
    gr["w_up"] = jnp.concatenate([_mm_tn_blocks(s["u2"], dhg, f"dw_upg_{tag}", False, True, BF16),
                                  _mm_tn_blocks(s["u2"], dhv, f"dw_upv_{tag}", False, True, BF16)], axis=0)
    dh1, dh1_b, gr["norm_ffn"] = _rmsnorm_bwd(s["h1"], du2, dh, p["norm_ffn"], f"norm_ffn_bwd_{tag}")
    dy = _mm(dh1_b, p["w_out"], F32, f"d_y_{tag}", dims=_NT)
    gr["w_out"] = _mm_tn(s["y"], dh1_b, f"dw_out_{tag}", BF16)
    do, dz, dgate, dpm, gr["head_norm"], gr["pool_scale"] = _post_bwd(
        dy, s["o"], s["z"], s["gate"], s["pm"], p["head_norm"], p["pool_scale"], f"post_bwd_{tag}")
    dpooled = _mm_cols(dpm, p["w_pool"], F32, f"d_pooled_{tag}", _NT)
    gr["w_pool"] = _mm_tn_cols(s["pooled"], dpm, 4, f"dw_pool_{tag}")
    dpool_in = _pool_bwd(dpooled, f"pool_bwd_{tag}")
    own = (gr["w_up"].astype(BF16), gr["w_down"].reshape(NDEV, -1, D).astype(BF16),
           gr["w_out"].reshape(NDEV, D // NDEV, D).astype(BF16))
    (dqkv, dbeta, dg), landed = _gdn_bwd(s["qkv"], s["beta"], s["g"], do, s["s_in"], s["vnew"], s["tinv"], s["wsv"],
                                         f"gdn_bwd_{tag}", own + tuple(scatter))
    dba, gr["a_log"], gr["dt_bias"] = _gates_bwd(s["ba"], p["a_row"], p["dt_row"], dbeta, dg, f"gates_bwd_{tag}")
    dqkv_pre, gr["conv_qkv"] = _gdn_pre_bwd(s["qkv_pre"], p["conv_qkv"], dqkv, f"gdn_pre_bwd_{tag}")
    segs = (("qkv", dqkv_pre, p["w_qkv"]), ("z", dz, p["w_z"]), ("ba", dba, p["w_ba"]),
            ("pool", dpool_in, p["w_pl"]), ("gate", dgate, p["w_gate"]))
    du = _mm_sum_nt([(dseg, wseg) for _, dseg, wseg in segs], f"d_u_{tag}")
    dws = [_mm_tn(s["u"], dseg, f"dw_{nm}_{tag}", BF16) for nm, dseg, _ in segs]
    gr["w_in"] = jnp.concatenate([dws[0], dws[1], dws[2][:, 0:H], dws[2][:, LANE:LANE + H], dws[3], dws[4]], axis=1)
    dh0, dh0_b, gr["norm_mix"] = _rmsnorm_bwd(s["h"], du, dh1, p["norm_mix"], f"norm_mix_bwd_{tag}")
    return dh0, dh0_b, gr, landed


def _pad_lanes(v8):
    return jnp.pad(v8.reshape(1, H), ((0, 0), (0, LANE - H)))


def _pack(parts, rows, lead=1):
    flat = jnp.concatenate([q.reshape(lead, -1) for q in parts], axis=1)
    flat = jnp.pad(flat, ((0, 0), (0, rows * LANE - flat.shape[1])))
    return flat.reshape((lead, rows, LANE) if lead > 1 else (rows, LANE))


def _unpack(packed, shapes, lead=1):
    flat = packed.reshape(lead, -1)
    out, off = [], 0
    for shp in shapes:
        n = 1
        for s_ in shp:
            n *= s_
        n //= lead
        out.append(flat[:, off:off + n].reshape(shp))
        off += n
    return out


SMALL_ROWS = 336
REPL_ROWS = 64


def kernel(x, meta_tokens, norm_mix, w_in, conv_qkv, a_log, dt_bias, head_norm, w_pool, pool_scale, w_out, norm_ffn, w_up, conv_ffn, w_down, norm_final, loss_target, m_meta_tokens, m_norm_mix, m_w_in, m_conv_qkv, m_a_log, m_dt_bias, m_head_norm, m_w_pool, m_pool_scale, m_w_out, m_norm_ffn, m_w_up, m_conv_ffn, m_w_down, m_norm_final, v_meta_tokens, v_norm_mix, v_w_in, v_conv_qkv, v_a_log, v_dt_bias, v_head_norm, v_w_pool, v_pool_scale, v_w_out, v_norm_ffn, v_w_up, v_conv_ffn, v_w_down, v_norm_final):
    seq = x.shape[1]
    t = ROW0 + seq
    assert t % TE == 0 and t % (MM_TILES * 16) == 0 and t % (GC * CH) == 0
    depth = w_in.shape[0]
    assert depth == 2
    cin = w_in.shape[2]

    def mixer_params(l, g_in, conv_q, conv_f, wp):
        wf = jnp.transpose(g_in, (1, 0, 2)).reshape(D, NDEV * cin)
        zpad = jnp.zeros((D, LANE - H), BF16)
        return dict(
            w_qkv=wf[:, 0:QKV], w_z=wf[:, QKV:QKV + D],
            w_ba=jnp.concatenate([wf[:, 4096:4104], zpad, wf[:, 4104:4112], zpad], axis=1),
            w_pl=wf[:, 4112:4624], w_gate=wf[:, 4624:6672], conv_qkv=conv_q, conv_ffn=conv_f, w_pool=wp,
            norm_mix=norm_mix[l].reshape(1, D), norm_ffn=norm_ffn[l].reshape(1, D),
            pool_scale=pool_scale[l].reshape(1, D), head_norm=head_norm[l].reshape(1, DH),
            a_row=_pad_lanes(a_log[l]), dt_row=_pad_lanes(dt_bias[l]))

    def late_params(g_up, g_out, g_down):
        return dict(w_out=g_out.reshape(D, D), w_up=g_up, w_down=g_down.reshape(4, FB, D))

    small_shapes = [conv_qkv.shape, conv_ffn.shape, w_pool.shape, meta_tokens.shape]
    small = _pack([conv_qkv, conv_ffn, w_pool, meta_tokens], SMALL_ROWS)
    w_in_b, w_up_b, w_out_b, w_down_b = w_in.astype(BF16), w_up.astype(BF16), w_out.astype(BF16), w_down.astype(BF16)
    g_in0, g_small = _gather([w_in_b[0], small], "gather_first")
    s_cq, s_cf, s_wp, s_mt = _unpack(g_small, [(NDEV,) + shp for shp in small_shapes], lead=NDEV)
    conv_qkv_full = jnp.transpose(s_cq, (1, 2, 0, 3)).reshape(depth, 4, QKV)
    conv_ffn_blk = jnp.transpose(s_cf, (1, 0, 2, 3))
    w_pool_full = jnp.transpose(s_wp, (1, 2, 3, 0, 4)).reshape(depth, 4, DH, 2 * DH).astype(BF16)
    meta_full = jnp.transpose(s_mt, (1, 0, 2)).reshape(N_META, D)

    h = jnp.concatenate([jnp.zeros((LEAD, D), F32), meta_full, x[0]], axis=0)
    p0 = mixer_params(0, g_in0, conv_qkv_full[0], conv_ffn_blk[0], w_pool_full[0])
    h, sv0, got0, p0 = _layer_fwd(h, p0, "l0", (w_up_b[0], w_out_b[0], w_down_b[0], w_in_b[1]),
                                  lambda got: late_params(*got[:3]))
    p1 = mixer_params(1, got0[3], conv_qkv_full[1], conv_ffn_blk[1], w_pool_full[1])
    h, sv1, _, p1 = _layer_fwd(h, p1, "l1", (w_up_b[1], w_out_b[1], w_down_b[1]), lambda got: late_params(*got))
    layers = [p0, p1]
    saved = [sv0, sv1]
    target = jnp.concatenate([jnp.zeros((ROW0, D), F32), loss_target[0]], axis=0)
    dh, dh_b, d_norm_final, loss_row = _loss_bwd(h, target, norm_final.reshape(1, D), "loss")

    def w_in_blocks(gr):
        return jnp.transpose(gr["w_in"].reshape(D, NDEV, cin), (1, 0, 2)).astype(BF16)

    grads = [None] * depth
    dh, dh_b, grads[1], (l_up1, l_down1, l_out1) = _layer_bwd(dh, dh_b, layers[1], saved[1], "l1")
    dh, dh_b, grads[0], (l_up0, l_down0, l_out0, l_in1) = _layer_bwd(dh, dh_b, layers[0], saved[0], "l0", (w_in_blocks(grads[1]),))
    grad_x = dh[ROW0:].reshape(1, seq, D)
    d_meta = dh[LEAD:ROW0]

    stk = lambda name: jnp.stack([grads[l][name] for l in range(depth)], axis=0)
    cq = conv_qkv.shape[2]
    pw = w_pool.shape[3]
    s_cq = jnp.transpose(stk("conv_qkv").reshape(depth, 4, NDEV, cq), (2, 0, 1, 3))
    s_cf = jnp.transpose(stk("conv_ffn"), (1, 0, 2, 3))
    s_wp = jnp.transpose(stk("w_pool").reshape(depth, 4, DH, NDEV, pw), (3, 0, 1, 2, 4))
    s_mt = jnp.transpose(d_meta.reshape(N_META, NDEV, D // NDEV), (1, 0, 2))
    b_small = _pack([s_cq, s_cf, s_wp, s_mt], SMALL_ROWS, lead=NDEV)
    l_in0, l_small = _scatter([w_in_blocks(grads[0]), b_small], "exchange_last")

    r_in = _adamw((l_in0, l_in1), w_in, m_w_in, v_w_in, "adamw_w_in")
    r_up = _adamw((l_up0, l_up1), w_up, m_w_up, v_w_up, "adamw_w_up")
    r_out = _adamw((l_out0, l_out1), w_out, m_w_out, v_w_out, "adamw_w_out")
    r_down = _adamw((l_down0, l_down1), w_down, m_w_down, v_w_down, "adamw_w_down")
    r_small = _adamw((l_small,), small[None], _pack([m_conv_qkv, m_conv_ffn, m_w_pool, m_meta_tokens], SMALL_ROWS)[None],
                     _pack([v_conv_qkv, v_conv_ffn, v_w_pool, v_meta_tokens], SMALL_ROWS)[None], "adamw_small")
    r_small = [_unpack(o_[0], small_shapes) for o_ in r_small]

    repl_shapes = [norm_mix.shape, a_log.shape, dt_bias.shape, head_norm.shape, pool_scale.shape, norm_ffn.shape,
                   norm_final.shape, (1,)]
    rp = lambda name, n: jnp.stack([grads[l][name][0, :n] for l in range(depth)], axis=0)
    part = _pack([rp("norm_mix", D), rp("a_log", H), rp("dt_bias", H), rp("head_norm", DH), rp("pool_scale", D),
                  rp("norm_ffn", D), d_norm_final[0], loss_row[0, 0:1]], REPL_ROWS)
    (l_repl,) = _gather([part], "gather_replicated")
    zero1 = jnp.zeros((1,), F32)
    r_repl = _adamw(
        (l_repl,), _pack([norm_mix, a_log, dt_bias, head_norm, pool_scale, norm_ffn, norm_final, zero1], REPL_ROWS)[None],
        _pack([m_norm_mix, m_a_log, m_dt_bias, m_head_norm, m_pool_scale, m_norm_ffn, m_norm_final, zero1], REPL_ROWS)[None],
        _pack([v_norm_mix, v_a_log, v_dt_bias, v_head_norm, v_pool_scale, v_norm_ffn, v_norm_final, zero1], REPL_ROWS)[None],
        "adamw_replicated")
    r_repl = [_unpack(o_[0], repl_shapes) for o_ in r_repl]
    loss = r_repl[0][7].reshape(())

    def leaf(kind):
        sm, rr = r_small[kind], r_repl[kind]
        return [sm[3], rr[0], r_in[kind], sm[0], rr[1], rr[2], rr[3], sm[2], rr[4], r_out[kind], rr[5], r_up[kind],
                sm[1], r_down[kind], rr[6]]

    return (loss, grad_x, *leaf(0), *leaf(1), *leaf(2), *leaf(3))
```

```python
import jax
import jax.numpy as jnp
from jax import lax
from jax.experimental import pallas as pl
from jax.experimental.pallas import tpu as pltpu

F32 = jnp.float32
BF16 = jnp.bfloat16
MESH = pl.DeviceIdType.MESH

D = 1024
H = 8
DH = 128
CH = 64
N_META = 16
LEAD = 48
ROW0 = LEAD + N_META
QKV = 3 * D
POOL_W = 512
POOL_WINDOWS = (2, 4, 8, 16)
FB = 704
NDEV = 8
EPS = 1e-6
MM_TILES = 12
TE = 192
LANE = 128
SUB = 8
VMEM_CAP = 56 << 20

ADAM_LR, ADAM_B1, ADAM_B2, ADAM_EPS, ADAM_WD, ADAM_STEP = 0.001, 0.9, 0.999, 1e-08, 0.01, 10

_NN = (((1,), (0,)), ((), ()))
_NT = (((1,), (1,)), ((), ()))
_TN = (((0,), (0,)), ((), ()))


def _dot(a, b, dims=_NN, precision=None):
    return lax.dot_general(a, b, dims, precision=precision, preferred_element_type=F32)


def _bdot(a, b, dims=_NN):
    return _dot(a.astype(BF16), b.astype(BF16), dims)


def _nbytes(shape, dtype):
    n = 1
    for s in shape:
        n *= s
    return n * jnp.dtype(dtype).itemsize


def _params(sem, block_bytes):
    limit = min(VMEM_CAP, 2 * block_bytes + (20 << 20))
    return pltpu.CompilerParams(dimension_semantics=sem, vmem_limit_bytes=limit)


PIN_BYTES = 12 << 20


def _pallas_call(body, *, out_shape, **kw):
    call = pl.pallas_call

    def big(s):
        return len(s.shape) >= 2 and s.shape[-1] >= D and _nbytes(s.shape, s.dtype) >= PIN_BYTES

    pinned = jax.tree.map(lambda s: pltpu.HBM(s.shape, s.dtype) if big(s) else s, out_shape)

    def run(*args):
        return call(body, out_shape=pinned, **kw)(
            *[pltpu.with_memory_space_constraint(a, pltpu.HBM) if big(a) else a for a in args])

    return run


def _sigmoid(x):
    return 1.0 / (1.0 + jnp.exp(-x))


def _col_tile(n):
    for t in (1024, 512, 256, 128):
        if n % t == 0:
            return t
    return n


def _matmul(a, b, *, dims, grid, a_spec, b_spec, o_spec, out_shape, name, red_axis=None, res=None):
    o_blk = tuple(s for s in o_spec.block_shape if s is not None)
    via_scratch = red_axis is not None and out_shape.dtype != F32

    def body(*refs):
        if res is None:
            a_ref, b_ref, o_ref = refs[:3]
        else:
            a_ref, b_ref, r_ref, o_ref = refs[:4]
        part = _dot(a_ref[...], b_ref[...], dims)
        if red_axis is None:
            if res is not None:
                part = part + r_ref[...]
            o_ref[...] = part.astype(o_ref.dtype)
        else:
            acc = refs[-1] if via_scratch else o_ref
            r = pl.program_id(red_axis)

            @pl.when(r == 0)
            def _():
                acc[...] = part + r_ref[...] if res is not None else part

            @pl.when(r > 0)
            def _():
                acc[...] += part

            if via_scratch:
                @pl.when(r == grid[red_axis] - 1)
                def _():
                    o_ref[...] = acc[...].astype(o_ref.dtype)

    def blk(spec, arr):
        return _nbytes([s for s in spec.block_shape if s is not None], arr.dtype)

    ins = [a, b] + ([res] if res is not None else [])
    specs = [a_spec, b_spec] + ([o_spec] if res is not None else [])
    nb = blk(a_spec, a) + blk(b_spec, b) + 2 * _nbytes(o_blk, F32)
    sem = tuple("arbitrary" if i == red_axis else "parallel" for i in range(len(grid)))
    return _pallas_call(
        body, name=name, grid=grid, in_specs=specs, out_specs=o_spec, out_shape=out_shape,
        scratch_shapes=[pltpu.VMEM(o_blk, F32)] if via_scratch else [],
        compiler_params=_params(sem, nb),
    )(*ins)


def _row_tiles(m, row_bytes, fixed_bytes, temp_row_bytes=0):
    for nt in (MM_TILES // 2, MM_TILES):
        tm = m // nt
        if 2 * (row_bytes * tm + fixed_bytes) + temp_row_bytes * tm <= VMEM_CAP - (10 << 20):
            return nt
    return MM_TILES


def _mm(a, b, out_dtype, name, res=None, dims=_NN):
    m, k = a.shape
    n = b.shape[1] if dims == _NN else b.shape[0]
    tn = _col_tile(n)
    nt = _row_tiles(m, 2 * k + tn * (jnp.dtype(out_dtype).itemsize + (4 if res is not None else 0)), 2 * k * tn, 4 * tn)
    tm = m // nt
    if dims == _NN:
        b_spec = pl.BlockSpec((k, tn), lambda j, i: (0, j))
    else:
        b_spec = pl.BlockSpec((tn, k), lambda j, i: (j, 0))
    return _matmul(
        a, b, dims=dims, grid=(n // tn, nt), a_spec=pl.BlockSpec((tm, k), lambda j, i: (i, 0)), b_spec=b_spec,
        o_spec=pl.BlockSpec((tm, tn), lambda j, i: (i, j)), out_shape=jax.ShapeDtypeStruct((m, n), out_dtype),
        name=name, res=res)


def _mm_du2(dhg, dhv, w_up, name):
    g, t, k = dhg.shape
    n = w_up.shape[1]
    nt = _row_tiles(t, 4 * k + 4 * n, 4 * k * n, 4 * n)
    tm = t // nt

    def body(ag_ref, av_ref, bg_ref, bv_ref, o_ref):
        part = _dot(ag_ref[...], bg_ref[...], _NT) + _dot(av_ref[...], bv_ref[...], _NT)
        r = pl.program_id(1)

        @pl.when(r == 0)
        def _():
            o_ref[...] = part

        @pl.when(r > 0)
        def _():
            o_ref[...] += part

    a_spec = pl.BlockSpec((None, tm, k), lambda i, g_: (g_, i, 0))
    return _pallas_call(
        body, name=name, grid=(nt, g),
        in_specs=[a_spec, a_spec, pl.BlockSpec((None, n, k), lambda i, g_: (g_, 0, 0)),
                  pl.BlockSpec((None, n, k), lambda i, g_: (g_ + g, 0, 0))],
        out_specs=pl.BlockSpec((tm, n), lambda i, g_: (i, 0)), out_shape=jax.ShapeDtypeStruct((t, n), F32),
        compiler_params=_params(("parallel", "arbitrary"), 4 * tm * k + 4 * n * k + 8 * tm * n),
    )(dhg, dhv, w_up, w_up)


def _mm_sum_nt(pairs, name):
    m, n = pairs[0][0].shape[0], pairs[0][1].shape[0]
    tm, tn = m // MM_TILES, min(512, n)
    np_ = len(pairs)

    def body(*refs):
        acc = _dot(refs[0][...], refs[1][...], _NT)
        for k in range(1, np_):
            acc = acc + _dot(refs[2 * k][...], refs[2 * k + 1][...], _NT)
        refs[-1][...] = acc

    specs, ins, nb = [], [], 2 * tm * tn * 4
    for a, b in pairs:
        k = a.shape[1]
        specs += [pl.BlockSpec((tm, k), lambda j, i: (i, 0)), pl.BlockSpec((tn, k), lambda j, i: (j, 0))]
        ins += [a, b]
        nb += 2 * k * (tm + tn)
    return _pallas_call(
        body, name=name, grid=(n // tn, MM_TILES), in_specs=specs, out_specs=pl.BlockSpec((tm, tn), lambda j, i: (i, j)),
        out_shape=jax.ShapeDtypeStruct((m, n), F32), compiler_params=_params(("parallel", "parallel"), nb),
    )(*ins)


def _mm_tn(a, g, name, out_dtype=F32):
    m, k = a.shape
    n = g.shape[1]
    tn = _col_tile(n)
    nt = _row_tiles(m, 2 * k + 2 * tn, 4 * k * tn)
    tm = m // nt
    return _matmul(
        a, g, dims=_TN, grid=(n // tn, nt), red_axis=1, a_spec=pl.BlockSpec((tm, k), lambda j, i: (i, 0)),
        b_spec=pl.BlockSpec((tm, tn), lambda j, i: (i, j)), o_spec=pl.BlockSpec((k, tn), lambda j, i: (0, j)),
        out_shape=jax.ShapeDtypeStruct((k, n), out_dtype), name=name)


def _mm_up(u, w_up, name):
    t = u.shape[0]
    g = w_up.shape[0]
    nt = _row_tiles(t, 2 * D + 2 * FB, 2 * D * FB, 4 * FB)
    tm = t // nt
    return _matmul(
        u, w_up, dims=_NN, grid=(g, nt), a_spec=pl.BlockSpec((tm, D), lambda g_, i: (i, 0)),
        b_spec=pl.BlockSpec((None, D, FB), lambda g_, i: (g_, 0, 0)),
        o_spec=pl.BlockSpec((None, tm, FB), lambda g_, i: (g_, i, 0)),
        out_shape=jax.ShapeDtypeStruct((g, t, FB), BF16), name=name)


def _mm_blocks_red(a, b, name, dims, res=None):
    g, t, k = a.shape
    n = b.shape[2] if dims == _NN else b.shape[1]
    nt = _row_tiles(t, 2 * k + n * (8 if res is not None else 4), 2 * k * n, 4 * n)
    tm = t // nt
    return _matmul(
        a, b, dims=dims, grid=(nt, g), red_axis=1, a_spec=pl.BlockSpec((None, tm, k), lambda i, g_: (g_, i, 0)),
        b_spec=pl.BlockSpec((None,) + b.shape[1:], lambda i, g_: (g_, 0, 0)),
        o_spec=pl.BlockSpec((tm, n), lambda i, g_: (i, 0)), out_shape=jax.ShapeDtypeStruct((t, n), F32),
        name=name, res=res)


def _mm_to_blocks(a, b, name):
    t, k = a.shape
    g, n, _ = b.shape
    nt = _row_tiles(t, 2 * k + 2 * n, 2 * k * n, 4 * n)
    tm = t // nt
    return _matmul(
        a, b, dims=_NT, grid=(g, nt), a_spec=pl.BlockSpec((tm, k), lambda g_, i: (i, 0)),
        b_spec=pl.BlockSpec((None, n, k), lambda g_, i: (g_, 0, 0)),
        o_spec=pl.BlockSpec((None, tm, n), lambda g_, i: (g_, i, 0)),
        out_shape=jax.ShapeDtypeStruct((g, t, n), BF16), name=name)


def _mm_tn_blocks(a, g, name, a_blocked, g_blocked, out_dtype=F32):
    nb = a.shape[0] if a_blocked else g.shape[0]
    t = a.shape[-2]
    k, n = a.shape[-1], g.shape[-1]
    nt = _row_tiles(t, 2 * k + 2 * n, 4 * k * n)
    tm = t // nt
    a_spec = (pl.BlockSpec((None, tm, k), lambda g_, i: (g_, i, 0)) if a_blocked
              else pl.BlockSpec((tm, k), lambda g_, i: (i, 0)))
    g_spec = (pl.BlockSpec((None, tm, n), lambda g_, i: (g_, i, 0)) if g_blocked
              else pl.BlockSpec((tm, n), lambda g_, i: (i, 0)))
    return _matmul(
        a, g, dims=_TN, grid=(nb, nt), red_axis=1, a_spec=a_spec, b_spec=g_spec,
        o_spec=pl.BlockSpec((None, k, n), lambda g_, i: (g_, 0, 0)),
        out_shape=jax.ShapeDtypeStruct((nb, k, n), out_dtype), name=name)


def _rows(cols, n=None):
    if n is None:
        return pl.BlockSpec((TE, cols), lambda i: (i, 0))
    return pl.BlockSpec((TE, cols), lambda i: (n - 1 - i, 0))


def _whole(shape):
    return pl.BlockSpec(shape, lambda *_: (0,) * len(shape))


def _row_ids(i, rows=TE):
    return i * rows + lax.broadcasted_iota(jnp.int32, (rows, 1), 0)


def _rmsnorm_fwd(h, gain, name):
    t = h.shape[0]

    def body(h_ref, g_ref, u_ref):
        x = h_ref[...]
        r = lax.rsqrt(jnp.mean(x * x, axis=-1, keepdims=True) + EPS)
        u_ref[...] = (x * r * g_ref[...]).astype(BF16)

    return _pallas_call(
        body, name=name, grid=(t // TE,), in_specs=[_rows(D), _whole((1, D))], out_specs=_rows(D),
        out_shape=jax.ShapeDtypeStruct((t, D), BF16), compiler_params=_params(("parallel",), 3 * TE * D * 4),
    )(h, gain)


def _rmsnorm_bwd(x, du, dres, gain, name):
    t = x.shape[0]

    def body(x_ref, du_ref, dr_ref, g_ref, dx_ref, dxb_ref, dg_ref):
        i = pl.program_id(0)
        xv = x_ref[...]
        r = lax.rsqrt(jnp.mean(xv * xv, axis=-1, keepdims=True) + EPS)
        gdy = du_ref[...] * g_ref[...]
        dx = dr_ref[...] + r * gdy - xv * (r * r * r) * jnp.mean(xv * gdy, axis=-1, keepdims=True)
        dx = jnp.where(_row_ids(i) >= LEAD, dx, 0.0)
        dx_ref[...] = dx
        dxb_ref[...] = dx.astype(BF16)
        part = jnp.sum(du_ref[...] * xv * r, axis=0, keepdims=True)

        @pl.when(i == 0)
        def _():
            dg_ref[...] = part

        @pl.when(i > 0)
        def _():
            dg_ref[...] += part

    return _pallas_call(
        body, name=name, grid=(t // TE,), in_specs=[_rows(D), _rows(D), _rows(D), _whole((1, D))],
        out_specs=[_rows(D), _rows(D), _whole((1, D))],
        out_shape=[jax.ShapeDtypeStruct((t, D), F32), jax.ShapeDtypeStruct((t, D), BF16),
                   jax.ShapeDtypeStruct((1, D), F32)],
        compiler_params=_params(("arbitrary",), 5 * TE * D * 4),
    )(x, du, dres, gain)


def _loss_bwd(h, target, gain, name):
    t = h.shape[0]

    def body(h_ref, t_ref, g_ref, dx_ref, dxb_ref, dg_ref, loss_ref):
        i = pl.program_id(0)
        xv = h_ref[...]
        gain_v = g_ref[...]
        r = lax.rsqrt(jnp.mean(xv * xv, axis=-1, keepdims=True) + EPS)
        real = _row_ids(i) >= ROW0
        err = jnp.where(real, xv * r * gain_v - t_ref[...], 0.0)
        dy = err * (1.0 / D)
        gdy = dy * gain_v
        dx = r * gdy - xv * (r * r * r) * jnp.mean(xv * gdy, axis=-1, keepdims=True)
        dx_ref[...] = dx
        dxb_ref[...] = dx.astype(BF16)
        dgp = jnp.sum(dy * xv * r, axis=0, keepdims=True)
        lp = 0.5 * jnp.sum(jnp.mean(err * err, axis=-1, keepdims=True), axis=0, keepdims=True)

        @pl.when(i == 0)
        def _():
            dg_ref[...] = dgp
            loss_ref[...] = jnp.broadcast_to(lp, (1, LANE))

        @pl.when(i > 0)
        def _():
            dg_ref[...] += dgp
            loss_ref[...] += jnp.broadcast_to(lp, (1, LANE))

    return _pallas_call(
        body, name=name, grid=(t // TE,), in_specs=[_rows(D), _rows(D), _whole((1, D))],
        out_specs=[_rows(D), _rows(D), _whole((1, D)), _whole((1, LANE))],
        out_shape=[jax.ShapeDtypeStruct((t, D), F32), jax.ShapeDtypeStruct((t, D), BF16),
                   jax.ShapeDtypeStruct((1, D), F32), jax.ShapeDtypeStruct((1, LANE), F32)],
        compiler_params=_params(("arbitrary",), 4 * TE * D * 4),
    )(h, target, gain)


def _seq_scratch(cols):
    return pltpu.VMEM((-(-cols // LANE), TE + SUB, LANE), F32)


def _taps(scr, c, wd, first, n, k):
    return [scr[c, first - (k - 1) + j:first - (k - 1) + j + n, 0:wd] for j in range(k)]


def _stage_history(scr, i):
    @pl.when(i == 0)
    def _():
        scr[...] = jnp.zeros(scr.shape, F32)

    @pl.when(i > 0)
    def _():
        scr[:, 0:SUB, :] = scr[:, TE:TE + SUB, :]


def _stage_future(scr, i):
    @pl.when(i == 0)
    def _():
        scr[...] = jnp.zeros(scr.shape, F32)

    @pl.when(i > 0)
    def _():
        scr[:, TE:TE + SUB, :] = scr[:, 0:SUB, :]


def _conv(tp, w):
    out = w[0:1] * tp[0]
    for j in range(1, len(tp)):
        out = out + w[j:j + 1] * tp[j]
    return out


def _conv_t(ds, c, wd, w):
    k = w.shape[0]
    out = w[k - 1:k] * ds[c, 0:TE, 0:wd]
    for j in range(k - 1):
        out = out + w[j:j + 1] * ds[c, k - 1 - j:k - 1 - j + TE, 0:wd]
    return out


def _gdn_pre_fwd(x, w, name):
    t = x.shape[0]

    def body(x_ref, w_ref, o_ref, xs):
        _stage_history(xs, pl.program_id(0))
        for hh in range(3 * H):
            sl = slice(hh * DH, (hh + 1) * DH)
            xs[hh, SUB:SUB + TE, :] = x_ref[:, sl].astype(F32)
            cv = _conv(_taps(xs, hh, DH, SUB, TE, 4), w_ref[:, sl])
            s = cv * _sigmoid(cv)
            if hh < 2 * H:
                s = s * lax.rsqrt(jnp.sum(s * s, axis=-1, keepdims=True) + EPS)
                if hh < H:
                    s = s * (DH ** -0.5)
            o_ref[:, sl] = s

    return _pallas_call(
        body, name=name, grid=(t // TE,), in_specs=[_rows(QKV), _whole((4, QKV))], out_specs=_rows(QKV),
        out_shape=jax.ShapeDtypeStruct((t, QKV), F32), scratch_shapes=[_seq_scratch(QKV)],
        compiler_params=_params(("arbitrary",), TE * QKV * 8),
    )(x, w)


def _gdn_pre_bwd(x, w, dqkv, name):
    t = x.shape[0]
    n = t // TE
    hb = TE // 16

    def body(x_ref, xp_ref, w_ref, d_ref, dx_ref, dw_ref, xs, ds):
        i = pl.program_id(0)

        @pl.when(i == 0)
        def _():
            dw_ref[...] = jnp.zeros_like(dw_ref)

        _stage_future(ds, i)
        for hh in range(3 * H):
            sl = slice(hh * DH, (hh + 1) * DH)
            xs[hh, 0:SUB, :] = jnp.where(i == n - 1, 0.0, xp_ref[SUB:2 * SUB, sl].astype(F32))
            xs[hh, SUB:SUB + TE, :] = x_ref[:, sl].astype(F32)
            wv = w_ref[:, sl]
            tp = _taps(xs, hh, DH, SUB, TE, 4)
            cv = _conv(tp, wv)
            sg = _sigmoid(cv)
            s = cv * sg
            dsv = d_ref[:, sl]
            if hh < 2 * H:
                if hh < H:
                    dsv = dsv * (DH ** -0.5)
                r = lax.rsqrt(jnp.sum(s * s, axis=-1, keepdims=True) + EPS)
                dsv = r * dsv - s * (r * r * r) * jnp.sum(s * dsv, axis=-1, keepdims=True)
            dcv = dsv * (sg * (1.0 + cv * (1.0 - sg)))
            ds[hh, 0:TE, :] = dcv
            dx_ref[:, sl] = _conv_t(ds, hh, DH, wv).astype(BF16)
            dw_ref[:, sl] += jnp.concatenate([jnp.sum(tp[j] * dcv, axis=0, keepdims=True) for j in range(4)], axis=0)

    return _pallas_call(
        body, name=name, grid=(n,),
        in_specs=[_rows(QKV, n), pl.BlockSpec((16, QKV), lambda i: (jnp.maximum((n - 1 - i) * hb - 1, 0), 0)),
                  _whole((4, QKV)), _rows(QKV, n)],
        out_specs=[_rows(QKV, n), _whole((4, QKV))],
        out_shape=[jax.ShapeDtypeStruct((t, QKV), BF16), jax.ShapeDtypeStruct((4, QKV), F32)],
        scratch_shapes=[_seq_scratch(QKV), _seq_scratch(QKV)],
        compiler_params=_params(("arbitrary",), TE * QKV * 14),
    )(x, x, w, dqkv)


def _softplus(x):
    return jnp.maximum(x, 0.0) + jnp.log(1.0 + jnp.exp(-jnp.abs(x)))


def _gates_fwd(ba, a_row, dt_row, name):
    t = ba.shape[0]

    def body(ba_ref, a_ref, dt_ref, b_out, g_out):
        real = _row_ids(pl.program_id(0)) >= LEAD
        b_out[...] = jnp.where(real, _sigmoid(ba_ref[:, 0:LANE]), 0.0)
        g = -jnp.exp(a_ref[...]) * _softplus(ba_ref[:, LANE:2 * LANE] + dt_ref[...])
        g_out[...] = jnp.where(real, g, 0.0)

    return _pallas_call(
        body, name=name, grid=(t // TE,), in_specs=[_rows(2 * LANE), _whole((1, LANE)), _whole((1, LANE))],
        out_specs=[_rows(LANE), _rows(LANE)],
        out_shape=[jax.ShapeDtypeStruct((t, LANE), F32), jax.ShapeDtypeStruct((t, LANE), F32)],
        compiler_params=_params(("parallel",), TE * LANE * 16),
    )(ba, a_row, dt_row)


def _gates_bwd(ba, a_row, dt_row, dbeta, dg, name):
    t = ba.shape[0]

    def body(ba_ref, a_ref, dt_ref, db_ref, dg_ref, dba_ref, da_out, ddt_out):
        i = pl.program_id(0)
        real = _row_ids(i) >= LEAD
        beta = _sigmoid(ba_ref[:, 0:LANE])
        draw_b = jnp.where(real, db_ref[...] * beta * (1.0 - beta), 0.0)
        pre = ba_ref[:, LANE:2 * LANE] + dt_ref[...]
        neg_a = -jnp.exp(a_ref[...])
        dgv = jnp.where(real, dg_ref[...], 0.0)
        draw_a = dgv * neg_a * _sigmoid(pre)
        dba_ref[:, 0:LANE] = draw_b.astype(BF16)
        dba_ref[:, LANE:2 * LANE] = draw_a.astype(BF16)
        dal = jnp.sum(dgv * neg_a * _softplus(pre), axis=0, keepdims=True)
        ddt = jnp.sum(draw_a, axis=0, keepdims=True)

        @pl.when(i == 0)
        def _():
            da_out[...] = dal
            ddt_out[...] = ddt

        @pl.when(i > 0)
        def _():
            da_out[...] += dal
            ddt_out[...] += ddt

    return _pallas_call(
        body, name=name, grid=(t // TE,),
        in_specs=[_rows(2 * LANE), _whole((1, LANE)), _whole((1, LANE)), _rows(LANE), _rows(LANE)],
        out_specs=[_rows(2 * LANE), _whole((1, LANE)), _whole((1, LANE))],
        out_shape=[jax.ShapeDtypeStruct((t, 2 * LANE), BF16), jax.ShapeDtypeStruct((1, LANE), F32),
                   jax.ShapeDtypeStruct((1, LANE), F32)],
        compiler_params=_params(("arbitrary",), TE * LANE * 24),
    )(ba, a_row, dt_row, dbeta, dg)


_OFFSETS = [(dx, dy, dc) for dx in (0, 1) for dy in (0, 1) for dc in (0, 1)][1:]
NPEER = len(_OFFSETS)
ANY_SPEC = pl.BlockSpec(memory_space=pl.ANY)


def _place():
    return lax.axis_index("x"), lax.axis_index("y"), lax.axis_index("c")


def _index(p):
    return 4 * p[0] + 2 * p[1] + p[2]


def _comm_scratch(n):
    return [pltpu.SemaphoreType.DMA((n * NPEER,)), pltpu.SemaphoreType.DMA((n * NPEER,)), pltpu.SemaphoreType.DMA((n,))]


def _scatter_copies(ins, outs, send, recv):
    me = _place()
    mi = _index(me)
    res = []
    for j, d in enumerate(_OFFSETS):
        peer = tuple(1 - v if bit else v for v, bit in zip(me, d))
        pi = _index(peer)
        for k in range(len(ins)):
            sem = k * NPEER + j
            mine = pltpu.make_async_remote_copy(src_ref=ins[k].at[pi], dst_ref=outs[k].at[mi], send_sem=send.at[sem],
                                                recv_sem=recv.at[sem], device_id=peer, device_id_type=MESH)
            theirs = pltpu.make_async_remote_copy(src_ref=ins[k].at[pi], dst_ref=outs[k].at[pi], send_sem=send.at[sem],
                                                  recv_sem=recv.at[sem], device_id=peer, device_id_type=MESH)
            res.append((mine, theirs))
    return res


def _scatter_own(ins, outs, loc):
    mi = _index(_place())
    return [pltpu.make_async_copy(ins[k].at[mi], outs[k].at[mi], loc.at[k]) for k in range(len(ins))]


def _scatter_start(ins, outs, send, recv, loc):
    for cp in _scatter_own(ins, outs, loc):
        cp.start()
    for mine, _ in _scatter_copies(ins, outs, send, recv):
        mine.start()


def _scatter_wait(ins, outs, send, recv, loc):
    cps = _scatter_copies(ins, outs, send, recv)
    for _, theirs in cps:
        theirs.wait_recv()
    for mine, _ in cps:
        mine.wait_send()
    for cp in _scatter_own(ins, outs, loc):
        cp.wait()


def _gather_parts(ins, outs, send, recv):
    x, y, c = _place()
    chips = [(1 - x, y), (x, 1 - y), (1 - x, 1 - y)]

    def cp(k, slot, src, block, to):
        return pltpu.make_async_remote_copy(src_ref=src, dst_ref=outs[k].at[_index(block)], send_sem=send.at[k * NPEER + slot],
                                            recv_sem=recv.at[k * NPEER + slot], device_id=to, device_id_type=MESH)

    return (x, y, c), (x, y, 1 - c), chips, cp


def _gather_start(ins, outs, send, recv, loc):
    me, sib, chips, cp = _gather_parts(ins, outs, send, recv)
    for k in range(len(ins)):
        pltpu.make_async_copy(ins[k], outs[k].at[_index(me)], loc.at[k]).start()
        cp(k, 0, ins[k], me, sib).start()
        for j, chip in enumerate(chips):
            cp(k, 1 + j, ins[k], me, (*chip, me[2])).start()


def _gather_forward(ins, outs, send, recv, loc):
    me, sib, chips, cp = _gather_parts(ins, outs, send, recv)
    for j, chip in enumerate(chips):
        blk = (*chip, me[2])
        for k in range(len(ins)):
            cp(k, 1 + j, ins[k], blk, me).wait_recv()
            cp(k, 4 + j, outs[k].at[_index(blk)], blk, sib).start()


def _gather_finish(ins, outs, send, recv, loc):
    me, sib, chips, cp = _gather_parts(ins, outs, send, recv)
    for k in range(len(ins)):
        cp(k, 0, ins[k], sib, me).wait_recv()
        for j, chip in enumerate(chips):
            cp(k, 4 + j, ins[k], (*chip, sib[2]), me).wait_recv()
        cp(k, 0, ins[k], me, sib).wait_send()
        for j, chip in enumerate(chips):
            cp(k, 1 + j, ins[k], me, (*chip, me[2])).wait_send()
            cp(k, 4 + j, outs[k].at[_index((*chip, me[2]))], (*chip, me[2]), sib).wait_send()
        pltpu.make_async_copy(ins[k], outs[k].at[_index(me)], loc.at[k]).wait()


def _gathered_shapes(arrs):
    return [jax.ShapeDtypeStruct((NDEV,) + a.shape, a.dtype) for a in arrs]


def _gather(arrs, name):
    n = len(arrs)

    def body(*refs):
        ins, outs, sems = refs[:n], refs[n:2 * n], refs[2 * n:]
        _gather_start(ins, outs, *sems)
        _gather_forward(ins, outs, *sems)
        _gather_finish(ins, outs, *sems)

    return _pallas_call(body, name=name, in_specs=[ANY_SPEC] * n, out_specs=[ANY_SPEC] * n,
                          out_shape=_gathered_shapes(arrs), scratch_shapes=_comm_scratch(n))(*arrs)


def _scatter(arrs, name):
    n = len(arrs)

    def body(*refs):
        ins, outs, sems = refs[:n], refs[n:2 * n], refs[2 * n:]
        _scatter_start(ins, outs, *sems)
        _scatter_wait(ins, outs, *sems)

    return _pallas_call(body, name=name, in_specs=[ANY_SPEC] * n, out_specs=[ANY_SPEC] * n,
                          out_shape=[jax.ShapeDtypeStruct(a.shape, a.dtype) for a in arrs],
                          scratch_shapes=_comm_scratch(n))(*arrs)


_BNN = (((2,), (1,)), ((0,), (0,)))
_BNT = (((2,), (2,)), ((0,), (0,)))
_BTN = (((1,), (1,)), ((0,), (0,)))


def _split(a):
    hi = a.astype(BF16)
    return hi, (a - hi.astype(F32)).astype(BF16)


def _dot3(a, b, dims=_BNN):
    ah, al = _split(a)
    bh, bl = _split(b)
    if dims != _BNN:
        return _dot(ah, bh, dims) + _dot(al, bh, dims) + _dot(ah, bl, dims)
    m = a.shape[1]
    r = _dot(jnp.concatenate([ah, al], axis=1), bh, _BNN)
    return r[:, :m] + r[:, m:] + _dot(ah, bl, _BNN)


def _tri_sum(mask, x):
    x1 = x.astype(BF16)
    r1 = x - x1.astype(F32)
    x2 = r1.astype(BF16)
    x3 = (r1 - x2.astype(F32)).astype(BF16)
    mb = mask.astype(BF16)
    return _dot(mb, x1) + _dot(mb, x2) + _dot(mb, x3)


GC = 3


def _rows_of(c):
    return slice(c * CH, (c + 1) * CH)


def _heads(ref, off):
    return jnp.stack([ref[_rows_of(c), off + h * DH:off + (h + 1) * DH] for c in range(GC) for h in range(H)])


def _cols(arrs):
    return jnp.stack([a[:, h:h + 1] for a in arrs for h in range(H)])


def _lanes(a):
    lane = lax.broadcasted_iota(jnp.int32, (CH, LANE), 1)
    out = jnp.zeros((CH, LANE), F32)
    for h in range(H):
        out = jnp.where(lane == h, a[h], out)
    return out


def _chunk_prep(qkv_ref, b_ref, g_ref):
    row = lax.broadcasted_iota(jnp.int32, (CH, CH), 0)
    col = lax.broadcasted_iota(jnp.int32, (CH, CH), 1)
    incl, strict = row >= col, row > col
    gcs = [_tri_sum(incl, g_ref[_rows_of(c), :]) for c in range(GC)]
    q, k, v = _heads(qkv_ref, 0), _heads(qkv_ref, D), _heads(qkv_ref, 2 * D)
    bcol, gcol = _cols([b_ref[_rows_of(c), :] for c in range(GC)]), _cols(gcs)
    grow = jnp.stack([gct[h:h + 1, :] for gct in [gc.T for gc in gcs] for h in range(H)])
    glast = _cols([gc[CH - 1:CH, :] for gc in gcs])
    dec = jnp.exp(jnp.where(incl[None], gcol - grow, -1e30))
    kb = k * bcol
    ab = _bdot(jnp.concatenate([kb, q], axis=1), k, _BNT)
    egc, ekc = jnp.exp(gcol), jnp.exp(glast - gcol)
    return dict(row=row, col=col, strict=strict[None], q=q, k=k, v=v, bcol=bcol, dec=dec, kb=kb,
                lm=jnp.where(strict[None], ab[:, :CH] * dec, 0.0), qk=ab[:, CH:] * dec, egc=egc, ekc=ekc,
                gth=jnp.exp(glast), qd=q * egc, kd=k * ekc, vb=v * bcol, kbg=kb * egc)


def _unit_lower_inverse(lm, eye):
    n = -lm
    x = eye + n
    pw = _dot3(n, n)
    for it in range(5):
        if it < 4:
            xp = _dot3(jnp.concatenate([x, pw], axis=1), pw)
            x = x + xp[:, :CH]
            pw = xp[:, CH:]
        else:
            x = x + _dot3(x, pw)
    return x


def _gdn_fwd(qkv, beta, g, name, gather=()):
    t = qkv.shape[0]
    nc = t // CH
    ns = nc // GC
    ng = len(gather)

    def body(qkv_ref, b_ref, g_ref, *rest):
        c_ins, (o_ref, sin_ref, vn_ref, ti_ref, w_ref) = rest[:ng], rest[ng:ng + 5]
        c_outs, state, sems = rest[ng + 5:2 * ng + 5], rest[2 * ng + 5], rest[2 * ng + 6:]
        step = pl.program_id(0)

        @pl.when(step == 0)
        def _():
            state[...] = jnp.zeros_like(state)
            if ng:
                _gather_start(c_ins, c_outs, *sems)

        if ng:
            @pl.when(step == max(ns - 4, 0))
            def _():
                _gather_forward(c_ins, c_outs, *sems)

            @pl.when(step == ns - 1)
            def _():
                _gather_finish(c_ins, c_outs, *sems)

        pr = _chunk_prep(qkv_ref, b_ref, g_ref)
        tinv = _unit_lower_inverse(pr["lm"], (pr["row"] == pr["col"]).astype(F32)[None])
        uw = _bdot(tinv, jnp.concatenate([pr["vb"], pr["kbg"]], axis=2), _BNN)
        u, w = uw[:, :, :DH], uw[:, :, DH:]
        s = state[...]
        for c in range(GC):
            hs = slice(c * H, (c + 1) * H)
            ws = _bdot(jnp.concatenate([w[hs], pr["qd"][hs]], axis=1), s, _BNN)
            vn = u[hs] - ws[:, :CH]
            o = ws[:, CH:] + _bdot(pr["qk"][hs], vn, _BNN)
            sin_ref[c] = s
            ti_ref[c] = tinv[hs]
            s = s * pr["gth"][hs] + _bdot(pr["kd"][hs], vn, _BTN)
            for h in range(H):
                sl = slice(h * DH, (h + 1) * DH)
                o_ref[_rows_of(c), sl] = o[h]
                vn_ref[_rows_of(c), sl] = vn[h]
                w_ref[_rows_of(c), sl] = w[c * H + h]
        state[...] = s

    chunk = lambda cols: pl.BlockSpec((GC * CH, cols), lambda c: (c, 0))
    outs = _pallas_call(
        body, name=name, grid=(ns,), in_specs=[chunk(QKV), chunk(LANE), chunk(LANE)] + [ANY_SPEC] * ng,
        out_specs=[chunk(D), pl.BlockSpec((GC, H, DH, DH), lambda c: (c, 0, 0, 0)), chunk(D),
                   pl.BlockSpec((GC, H, CH, CH), lambda c: (c, 0, 0, 0)), chunk(D)] + [ANY_SPEC] * ng,
        out_shape=[jax.ShapeDtypeStruct((t, D), F32), jax.ShapeDtypeStruct((nc, H, DH, DH), F32),
                   jax.ShapeDtypeStruct((t, D), F32), jax.ShapeDtypeStruct((nc, H, CH, CH), F32),
                   jax.ShapeDtypeStruct((t, D), F32)] + _gathered_shapes(gather),
        scratch_shapes=[pltpu.VMEM((H, DH, DH), F32)] + (_comm_scratch(ng) if ng else []),
        compiler_params=_params(("arbitrary",), 12 << 20),
    )(qkv, beta, g, *gather)
    return outs[:5], outs[5:]


def _gdn_bwd(qkv, beta, g, do, s_in, vnew, tinv, wsv, name, scatter=()):
    t = qkv.shape[0]
    nsteps = t // CH // GC
    ns = len(scatter)

    def body(qkv_ref, b_ref, g_ref, do_ref, sin_ref, vn_ref, ti_ref, w_ref, *rest):
        c_ins, (dqkv_ref, db_ref, dg_ref) = rest[:ns], rest[ns:ns + 3]
        c_outs, dstate, sems = rest[ns + 3:2 * ns + 3], rest[2 * ns + 3], rest[2 * ns + 4:]
        step = pl.program_id(0)

        @pl.when(step == 0)
        def _():
            dstate[...] = jnp.zeros_like(dstate)
            if ns:
                _scatter_start(c_ins, c_outs, *sems)

        if ns:
            @pl.when(step == nsteps - 1)
            def _():
                _scatter_wait(c_ins, c_outs, *sems)

        pr = _chunk_prep(qkv_ref, b_ref, g_ref)
        ti = jnp.concatenate([ti_ref[c] for c in range(GC)], axis=0)
        s = jnp.concatenate([sin_ref[c] for c in range(GC)], axis=0)
        w, vn, doh = _heads(w_ref, 0), _heads(vn_ref, 0), _heads(do_ref, 0)
        dqd = _bdot(doh, s, _BNT)
        dqk = _bdot(doh, vn, _BNT)
        qk_do = _bdot(pr["qk"], doh, _BTN)
        qd_do = _bdot(pr["qd"], doh, _BTN)
        ds = dstate[...]
        dvn_c, dkd_c, dw_c, dgt_c = [None] * GC, [None] * GC, [None] * GC, [None] * GC
        for c in reversed(range(GC)):
            hs = slice(c * H, (c + 1) * H)
            dvn_c[c] = _bdot(pr["kd"][hs], ds, _BNN) + qk_do[hs]
            dkd_c[c] = _bdot(vn[hs], ds, _BNT)
            dw_c[c] = -_bdot(dvn_c[c], s[hs], _BNT)
            dgt_c[c] = jnp.sum(jnp.sum(ds * s[hs], axis=2, keepdims=True), axis=1, keepdims=True)
            ds = ds * pr["gth"][hs] + qd_do[hs] - _bdot(w[hs], dvn_c[c], _BTN)
        dstate[...] = ds
        dvn, dkd, dw, dgt = (jnp.concatenate(parts, axis=0) for parts in (dvn_c, dkd_c, dw_c, dgt_c))
        duw = jnp.concatenate([dvn, dw], axis=2)
        dvk = _bdot(ti, duw, _BTN)
        dvb, dkbg = dvk[:, :, :DH], dvk[:, :, DH:]
        dti = _bdot(duw, jnp.concatenate([pr["vb"], pr["kbg"]], axis=2), _BNT)
        dl = -_dot3(_dot3(ti, dti, _BTN), ti, _BNT)
        dl = jnp.where(pr["strict"], dl, 0.0)
        dab = jnp.concatenate([dl * pr["dec"], dqk * pr["dec"]], axis=1)
        r1 = _bdot(dab, pr["k"], _BNN)
        dkb = r1[:, :CH] + dkbg * pr["egc"]
        dq = r1[:, CH:] + dqd * pr["egc"]
        dk = _bdot(dab, jnp.concatenate([pr["kb"], pr["q"]], axis=1), _BTN) + dkb * pr["bcol"] + dkd * pr["ekc"]
        m = dl * pr["lm"] + dqk * pr["qk"]
        mh, ml = _split(m)
        ones = jnp.ones((GC * H, CH, LANE), BF16)
        colsum = (_dot(mh, ones, _BTN) + _dot(ml, ones, _BTN))[:, :, 0:1]
        kdsum = jnp.sum(dkd * pr["kd"], axis=2, keepdims=True)
        dgc = (jnp.sum(m, axis=2, keepdims=True) - colsum + jnp.sum(dkbg * pr["kbg"], axis=2, keepdims=True)
               + jnp.sum(dqd * pr["qd"], axis=2, keepdims=True) - kdsum)
        dglast = jnp.sum(kdsum, axis=1, keepdims=True) + dgt * pr["gth"]
        last_row = lax.broadcasted_iota(jnp.int32, (1, CH, 1), 1) == CH - 1
        dgc = dgc + jnp.where(last_row, dglast, 0.0)
        dbeta = jnp.sum(dkb * pr["k"], axis=2, keepdims=True) + jnp.sum(dvb * pr["v"], axis=2, keepdims=True)
        dv = dvb * pr["bcol"]
        upper = pr["row"] <= pr["col"]
        for c in range(GC):
            hs = slice(c * H, (c + 1) * H)
            for h in range(H):
                dqkv_ref[_rows_of(c), h * DH:(h + 1) * DH] = dq[c * H + h]
                dqkv_ref[_rows_of(c), D + h * DH:D + (h + 1) * DH] = dk[c * H + h]
                dqkv_ref[_rows_of(c), 2 * D + h * DH:2 * D + (h + 1) * DH] = dv[c * H + h]
            db_ref[_rows_of(c), :] = _lanes(dbeta[hs])
            dg_ref[_rows_of(c), :] = _tri_sum(upper, _lanes(dgc[hs]))

    chunk = lambda cols: pl.BlockSpec((GC * CH, cols), lambda c: (nsteps - 1 - c, 0))
    sq = lambda a, b: pl.BlockSpec((GC, H, a, b), lambda c: (nsteps - 1 - c, 0, 0, 0))
    outs = _pallas_call(
        body, name=name, grid=(nsteps,),
        in_specs=[chunk(QKV), chunk(LANE), chunk(LANE), chunk(D), sq(DH, DH), chunk(D), sq(CH, CH), chunk(D)] + [ANY_SPEC] * ns,
        out_specs=[chunk(QKV), chunk(LANE), chunk(LANE)] + [ANY_SPEC] * ns,
        out_shape=[jax.ShapeDtypeStruct((t, QKV), F32), jax.ShapeDtypeStruct((t, LANE), F32),
                   jax.ShapeDtypeStruct((t, LANE), F32)] + [jax.ShapeDtypeStruct(a.shape, a.dtype) for a in scatter],
        scratch_shapes=[pltpu.VMEM((H, DH, DH), F32)] + (_comm_scratch(ns) if ns else []),
        compiler_params=_params(("arbitrary",), 16 << 20),
    )(qkv, beta, g, do, s_in, vnew, tinv, wsv, *scatter)
    return outs[:3], outs[3:]


def _pool_counts(row_ids, win):
    return jnp.minimum(jnp.maximum(row_ids - LEAD, 0) + 1, win).astype(F32)


def _pool_fwd(p, name):
    t = p.shape[0]
    ext = TE + 16

    def body(p_ref, o_ref, carry):
        i = pl.program_id(0)

        @pl.when(i == 0)
        def _():
            carry[...] = jnp.zeros_like(carry)

        ids = _row_ids(i)
        for gi, win in enumerate(POOL_WINDOWS):
            sl = slice(gi * LANE, (gi + 1) * LANE)
            xv = p_ref[:, sl]
            s = jnp.concatenate([carry[:, sl], xv], axis=0)
            sh = 1
            while sh < win:
                s = s + pltpu.roll(s, sh, 0)
                sh *= 2
            o_ref[:, sl] = (s[16:ext] / _pool_counts(ids, win) - xv).astype(BF16)
            carry[:, sl] = xv[TE - 16:TE]

    return _pallas_call(
        body, name=name, grid=(t // TE,), in_specs=[_rows(POOL_W)], out_specs=_rows(POOL_W),
        out_shape=jax.ShapeDtypeStruct((t, POOL_W), BF16), scratch_shapes=[pltpu.VMEM((16, POOL_W), F32)],
        compiler_params=_params(("arbitrary",), TE * POOL_W * 8),
    )(p)


def _pool_bwd(dpo, name):
    t = dpo.shape[0]
    n = t // TE
    ext = TE + 16

    def body(d_ref, o_ref, carry):
        i = pl.program_id(0)

        @pl.when(i == 0)
        def _():
            carry[...] = jnp.zeros_like(carry)

        ids = _row_ids(n - 1 - i)
        for gi, win in enumerate(POOL_WINDOWS):
            sl = slice(gi * LANE, (gi + 1) * LANE)
            dv = d_ref[:, sl]
            rv = dv / _pool_counts(ids, win)
            s = jnp.concatenate([rv, carry[:, sl]], axis=0)
            sh = 1
            while sh < win:
                s = s + pltpu.roll(s, ext - sh, 0)
                sh *= 2
            o_ref[:, sl] = (s[0:TE] - dv).astype(BF16)
            carry[:, sl] = rv[0:16]

    return _pallas_call(
        body, name=name, grid=(n,), in_specs=[_rows(POOL_W, n)], out_specs=_rows(POOL_W, n),
        out_shape=jax.ShapeDtypeStruct((t, POOL_W), BF16), scratch_shapes=[pltpu.VMEM((16, POOL_W), F32)],
        compiler_params=_params(("arbitrary",), TE * POOL_W * 8),
    )(dpo)


def _post_fwd(o, z, gate, pm, hn, ps, name):
    t = o.shape[0]

    def body(o_ref, z_ref, g_ref, pm_ref, hn_ref, ps_ref, y_ref):
        for h in range(H):
            sl = slice(h * DH, (h + 1) * DH)
            ov = o_ref[:, sl]
            zv = z_ref[:, sl].astype(F32)
            r = lax.rsqrt(jnp.mean(ov * ov, axis=-1, keepdims=True) + EPS)
            ya = ov * r * hn_ref[...] * (zv * _sigmoid(zv))
            ga = _sigmoid(g_ref[:, sl].astype(F32))
            gb = _sigmoid(g_ref[:, D + h * DH:D + (h + 1) * DH].astype(F32))
            y_ref[:, sl] = (ga * ya + gb * (pm_ref[:, sl] * ps_ref[:, sl])).astype(BF16)

    return _pallas_call(
        body, name=name, grid=(t // TE,),
        in_specs=[_rows(D), _rows(D), _rows(2 * D), _rows(D), _whole((1, DH)), _whole((1, D))], out_specs=_rows(D),
        out_shape=jax.ShapeDtypeStruct((t, D), BF16), compiler_params=_params(("parallel",), TE * D * 16),
    )(o, z, gate, pm, hn, ps)


def _post_bwd(dy, o, z, gate, pm, hn, ps, name):
    t = o.shape[0]

    def body(dy_ref, o_ref, z_ref, g_ref, pm_ref, hn_ref, ps_ref, do_ref, dz_ref, dgate_ref, dpm_ref, dhn_ref, dps_ref):
        i = pl.program_id(0)

        @pl.when(i == 0)
        def _():
            dhn_ref[...] = jnp.zeros_like(dhn_ref)
            dps_ref[...] = jnp.zeros_like(dps_ref)

        hnv = hn_ref[...]
        dhn = jnp.zeros((1, DH), F32)
        for h in range(H):
            sl = slice(h * DH, (h + 1) * DH)
            slb = slice(D + h * DH, D + (h + 1) * DH)
            dyv = dy_ref[:, sl]
            ov = o_ref[:, sl]
            zv = z_ref[:, sl].astype(F32)
            r = lax.rsqrt(jnp.mean(ov * ov, axis=-1, keepdims=True) + EPS)
            sz = _sigmoid(zv)
            silu = zv * sz
            on = ov * r
            ya = on * hnv * silu
            ga = _sigmoid(g_ref[:, sl].astype(F32))
            gb = _sigmoid(g_ref[:, slb].astype(F32))
            pmv = pm_ref[:, sl]
            psv = ps_ref[:, sl]
            dya = dyv * ga
            dyb = dyv * gb
            dgate_ref[:, sl] = (dyv * ya * ga * (1.0 - ga)).astype(BF16)
            dgate_ref[:, slb] = (dyv * (pmv * psv) * gb * (1.0 - gb)).astype(BF16)
            tt = dya * hnv * silu
            do_ref[:, sl] = r * tt - ov * (r * r * r) * jnp.mean(ov * tt, axis=-1, keepdims=True)
            dz_ref[:, sl] = (dya * on * hnv * (sz * (1.0 + zv * (1.0 - sz)))).astype(BF16)
            dhn = dhn + jnp.sum(dya * on * silu, axis=0, keepdims=True)
            dps_ref[:, sl] += jnp.sum(dyb * pmv, axis=0, keepdims=True)
            dpm_ref[:, sl] = (dyb * psv).astype(BF16)
        dhn_ref[...] += dhn

    return _pallas_call(
        body, name=name, grid=(t // TE,),
        in_specs=[_rows(D), _rows(D), _rows(D), _rows(2 * D), _rows(D), _whole((1, DH)), _whole((1, D))],
        out_specs=[_rows(D), _rows(D), _rows(2 * D), _rows(D), _whole((1, DH)), _whole((1, D))],
        out_shape=[jax.ShapeDtypeStruct((t, D), F32), jax.ShapeDtypeStruct((t, D), BF16),
                   jax.ShapeDtypeStruct((t, 2 * D), BF16), jax.ShapeDtypeStruct((t, D), BF16),
                   jax.ShapeDtypeStruct((1, DH), F32), jax.ShapeDtypeStruct((1, D), F32)],
        compiler_params=_params(("arbitrary",), TE * D * 28),
    )(dy, o, z, gate, pm, hn, ps)


_FB_COLS = [(c, min(c + LANE, FB)) for c in range(0, FB, LANE)]


def _mlp_act_fwd(hid, cw, name):
    t = hid.shape[1]
    n = t // TE

    def body(hg_ref, hv_ref, wg_ref, wv_ref, a_ref, xg, xv):
        i = pl.program_id(1)
        _stage_history(xg, i)
        _stage_history(xv, i)
        for c, (c0, c1) in enumerate(_FB_COLS):
            sl, wd = slice(c0, c1), c1 - c0
            xg[c, SUB:SUB + TE, 0:wd] = hg_ref[:, sl].astype(F32)
            xv[c, SUB:SUB + TE, 0:wd] = hv_ref[:, sl].astype(F32)
            gg = _conv(_taps(xg, c, wd, SUB, TE, 3), wg_ref[:, sl])
            vv = _conv(_taps(xv, c, wd, SUB, TE, 3), wv_ref[:, sl])
            a_ref[:, sl] = (gg * _sigmoid(gg) * vv).astype(BF16)

    hspec = lambda off: pl.BlockSpec((None, TE, FB), lambda p, i: (p + off, i, 0))
    wspec = lambda off: pl.BlockSpec((None, 3, FB), lambda p, i: (p + off, 0, 0))
    return _pallas_call(
        body, name=name, grid=(4, n), in_specs=[hspec(0), hspec(4), wspec(0), wspec(4)],
        out_specs=pl.BlockSpec((None, TE, FB), lambda p, i: (p, i, 0)),
        out_shape=jax.ShapeDtypeStruct((4, t, FB), BF16),
        scratch_shapes=[_seq_scratch(FB), _seq_scratch(FB)],
        compiler_params=_params(("parallel", "arbitrary"), TE * FB * 12),
    )(hid, hid, cw, cw)


def _mlp_act_bwd(da, hid, cw, name):
    t = hid.shape[1]
    n = t // TE
    hb = TE // 16

    def body(da_ref, hg_ref, hv_ref, pg_ref, pv_ref, wg_ref, wv_ref, dhg_ref, dhv_ref, dwg_ref, dwv_ref, xg, xv, dg, dv):
        i = pl.program_id(1)

        @pl.when(i == 0)
        def _():
            dwg_ref[...] = jnp.zeros_like(dwg_ref)
            dwv_ref[...] = jnp.zeros_like(dwv_ref)

        _stage_future(dg, i)
        _stage_future(dv, i)
        first_tile = i == n - 1
        for c, (c0, c1) in enumerate(_FB_COLS):
            sl, wd = slice(c0, c1), c1 - c0
            for scr, p_ref, h_ref in ((xg, pg_ref, hg_ref), (xv, pv_ref, hv_ref)):
                scr[c, 0:SUB, 0:wd] = jnp.where(first_tile, 0.0, p_ref[SUB:2 * SUB, sl].astype(F32))
                scr[c, SUB:SUB + TE, 0:wd] = h_ref[:, sl].astype(F32)
            wg = wg_ref[:, sl]
            wv = wv_ref[:, sl]
            tg = _taps(xg, c, wd, SUB, TE, 3)
            tv = _taps(xv, c, wd, SUB, TE, 3)
            gg = _conv(tg, wg)
            vv = _conv(tv, wv)
            sg = _sigmoid(gg)
            dav = da_ref[:, sl].astype(F32)
            dgg = dav * vv * (sg * (1.0 + gg * (1.0 - sg)))
            dvv = dav * (gg * sg)
            for dc, tp, w, scr, dh_ref, dw_ref in ((dgg, tg, wg, dg, dhg_ref, dwg_ref), (dvv, tv, wv, dv, dhv_ref, dwv_ref)):
                scr[c, 0:TE, 0:wd] = dc
                dh_ref[:, sl] = _conv_t(scr, c, wd, w).astype(BF16)
                dw_ref[:, sl] += jnp.concatenate([jnp.sum(tp[j] * dc, axis=0, keepdims=True) for j in range(3)], axis=0)

    rev = lambda off: pl.BlockSpec((None, TE, FB), lambda p, i: (p + off, n - 1 - i, 0))
    halo = lambda off: pl.BlockSpec((None, 16, FB), lambda p, i: (p + off, jnp.maximum((n - 1 - i) * hb - 1, 0), 0))
    wspec = lambda off: pl.BlockSpec((None, 3, FB), lambda p, i: (p + off, 0, 0))
    dwspec = pl.BlockSpec((None, 3, FB), lambda p, i: (p, 0, 0))
    return _pallas_call(
        body, name=name, grid=(4, n), in_specs=[rev(0), rev(0), rev(4), halo(0), halo(4), wspec(0), wspec(4)],
        out_specs=[rev(0), rev(0), dwspec, dwspec],
        out_shape=[jax.ShapeDtypeStruct((4, t, FB), BF16), jax.ShapeDtypeStruct((4, t, FB), BF16),
                   jax.ShapeDtypeStruct((4, 3, FB), F32), jax.ShapeDtypeStruct((4, 3, FB), F32)],
        scratch_shapes=[_seq_scratch(FB)] * 4,
        compiler_params=_params(("parallel", "arbitrary"), TE * FB * 24),
    )(da, hid, hid, hid, hid, cw, cw)


def _adamw(lands, w, m, v, name):
    nl, r, c = w.shape
    tr = r
    if r * c * 4 > (2 << 20):
        for cand in (128, 64, 32, 16):
            if r % cand == 0:
                tr = cand
                break
    nr = r // tr
    c1 = 1.0 - ADAM_B1 ** ADAM_STEP
    c2 = 1.0 - ADAM_B2 ** ADAM_STEP

    def body(*refs):
        l_refs, (w_ref, m_ref, v_ref, g_out, d_out, m_out, v_out) = refs[:nl], refs[nl:]
        layer = pl.program_id(0)
        g = None
        for l, l_ref in enumerate(l_refs):
            gl = l_ref[0].astype(F32)
            for i in range(1, NDEV):
                gl = gl + l_ref[i].astype(F32)
            g = gl if g is None else jnp.where(layer == l, gl, g)
        mn = ADAM_B1 * m_ref[...] + (1.0 - ADAM_B1) * g
        vn = ADAM_B2 * v_ref[...] + (1.0 - ADAM_B2) * (g * g)
        g_out[...] = g
        m_out[...] = mn
        v_out[...] = vn
        d_out[...] = -ADAM_LR * ((mn / c1) / (jnp.sqrt(vn / c2) + ADAM_EPS) + ADAM_WD * w_ref[...])

    def land_spec(l):
        return pl.BlockSpec((NDEV, tr, c), lambda ly, i: (0, jnp.where(ly == l, i, 0 if l > 0 else nr - 1), 0))

    spec = pl.BlockSpec((None, tr, c), lambda ly, i: (ly, i, 0))
    shp = jax.ShapeDtypeStruct((nl, r, c), F32)
    return _pallas_call(
        body, name=name, grid=(nl, nr), in_specs=[land_spec(l) for l in range(nl)] + [spec, spec, spec],
        out_specs=[spec] * 4, out_shape=[shp] * 4,
        compiler_params=_params(("arbitrary", "arbitrary"), (11 + 4 * nl) * tr * c * 4),
    )(*lands, w, m, v)


def _layer_fwd(h, p, tag, gather=(), finish=None):
    u = _rmsnorm_fwd(h, p["norm_mix"], f"norm_mix_{tag}")
    qkv_pre = _mm(u, p["w_qkv"], BF16, f"proj_qkv_{tag}")
    z = _mm(u, p["w_z"], BF16, f"proj_z_{tag}")
    ba = _mm(u, p["w_ba"], F32, f"proj_ba_{tag}")
    pool_in = _mm(u, p["w_pl"], F32, f"proj_pool_{tag}")
    gate = _mm(u, p["w_gate"], BF16, f"proj_gate_{tag}")
    qkv = _gdn_pre_fwd(qkv_pre, p["conv_qkv"], f"gdn_pre_{tag}")
    beta, g = _gates_fwd(ba, p["a_row"], p["dt_row"], f"gates_{tag}")
    (o, s_in, vnew, tinv, wsv), gathered = _gdn_fwd(qkv, beta, g, f"gdn_{tag}", gather)
    if finish is not None:
        p = {**p, **finish(gathered)}
    pooled = _pool_fwd(pool_in, f"pool_{tag}")
    pm = _mm(pooled, p["w_pool_bd"], F32, f"pool_mm_{tag}")
    y = _post_fwd(o, z, gate, pm, p["head_norm"], p["pool_scale"], f"post_{tag}")
    h1 = _mm(y, p["w_out"], F32, f"out_proj_{tag}", res=h)
    u2 = _rmsnorm_fwd(h1, p["norm_ffn"], f"norm_ffn_{tag}")
    hid = _mm_up(u2, p["w_up"], f"up_proj_{tag}")
    act = _mlp_act_fwd(hid, p["conv_ffn"], f"mlp_act_{tag}")
    h2 = _mm_blocks_red(act, p["w_down"], f"down_proj_{tag}", _NN, res=h1)
    saved = dict(h=h, u=u, qkv_pre=qkv_pre, z=z, ba=ba, gate=gate, qkv=qkv, beta=beta, g=g, o=o, s_in=s_in, vnew=vnew,
                 tinv=tinv, wsv=wsv, pooled=pooled, pm=pm, y=y, h1=h1, u2=u2, hid=hid, act=act)
    return h2, saved, gathered, p


def _layer_bwd(dh, dh_b, p, s, tag, scatter=()):
    gr = {}
    da = _mm_to_blocks(dh_b, p["w_down"], f"d_act_{tag}")
    gr["w_down"] = _mm_tn_blocks(s["act"], dh_b, f"dw_down_{tag}", True, False, BF16)
    dhg, dhv, dwg, dwv = _mlp_act_bwd(da, s["hid"], p["conv_ffn"], f"mlp_act_bwd_{tag}")
    gr["conv_ffn"] = jnp.concatenate([dwg, dwv], axis=0)
    w_up = p["w_up"]
    du2 = _mm_du2(dhg, dhv, w_up, f"d_u2_{tag}")
    gr["w_up"] = jnp.concatenate([_mm_tn_blocks(s["u2"], dhg, f"dw_upg_{tag}", False, True, BF16),
                                  _mm_tn_blocks(s["u2"], dhv, f"dw_upv_{tag}", False, True, BF16)], axis=0)
    dh1, dh1_b, gr["norm_ffn"] = _rmsnorm_bwd(s["h1"], du2, dh, p["norm_ffn"], f"norm_ffn_bwd_{tag}")
    dy = _mm(dh1_b, p["w_out"], F32, f"d_y_{tag}", dims=_NT)
    gr["w_out"] = _mm_tn(s["y"], dh1_b, f"dw_out_{tag}", BF16)
    do, dz, dgate, dpm, gr["head_norm"], gr["pool_scale"] = _post_bwd(
        dy, s["o"], s["z"], s["gate"], s["pm"], p["head_norm"], p["pool_scale"], f"post_bwd_{tag}")
    dpooled = _mm(dpm, p["w_pool_bd"], F32, f"d_pooled_{tag}", dims=_NT)
    dw_bd = _mm_tn(s["pooled"], dpm, f"dw_pool_{tag}")
    gr["w_pool"] = jnp.stack([dw_bd[g * DH:(g + 1) * DH, g * 2 * DH:(g + 1) * 2 * DH] for g in range(4)], axis=0)
    dpool_in = _pool_bwd(dpooled, f"pool_bwd_{tag}")
    own = (gr["w_up"].astype(BF16), gr["w_down"].reshape(NDEV, -1, D).astype(BF16),
           gr["w_out"].reshape(NDEV, D // NDEV, D).astype(BF16))
    (dqkv, dbeta, dg), landed = _gdn_bwd(s["qkv"], s["beta"], s["g"], do, s["s_in"], s["vnew"], s["tinv"], s["wsv"],
                                         f"gdn_bwd_{tag}", own + tuple(scatter))
    dba, gr["a_log"], gr["dt_bias"] = _gates_bwd(s["ba"], p["a_row"], p["dt_row"], dbeta, dg, f"gates_bwd_{tag}")
    dqkv_pre, gr["conv_qkv"] = _gdn_pre_bwd(s["qkv_pre"], p["conv_qkv"], dqkv, f"gdn_pre_bwd_{tag}")
    segs = (("qkv", dqkv_pre, p["w_qkv"]), ("z", dz, p["w_z"]), ("ba", dba, p["w_ba"]),
            ("pool", dpool_in, p["w_pl"]), ("gate", dgate, p["w_gate"]))
    du = _mm_sum_nt([(dseg, wseg) for _, dseg, wseg in segs], f"d_u_{tag}")
    dws = [_mm_tn(s["u"], dseg, f"dw_{nm}_{tag}", BF16) for nm, dseg, _ in segs]
    gr["w_in"] = jnp.concatenate([dws[0], dws[1], dws[2][:, 0:H], dws[2][:, LANE:LANE + H], dws[3], dws[4]], axis=1)
    dh0, dh0_b, gr["norm_mix"] = _rmsnorm_bwd(s["h"], du, dh1, p["norm_mix"], f"norm_mix_bwd_{tag}")
    return dh0, dh0_b, gr, landed


def _pool_block_diag(wp):
    out = jnp.zeros((POOL_W, D), wp.dtype)
    for g in range(4):
        out = out.at[g * DH:(g + 1) * DH, g * 2 * DH:(g + 1) * 2 * DH].set(wp[g])
    return out


def _pad_lanes(v8):
    return jnp.pad(v8.reshape(1, H), ((0, 0), (0, LANE - H)))


def _pack(parts, rows, lead=1):
    flat = jnp.concatenate([q.reshape(lead, -1) for q in parts], axis=1)
    flat = jnp.pad(flat, ((0, 0), (0, rows * LANE - flat.shape[1])))
    return flat.reshape((lead, rows, LANE) if lead > 1 else (rows, LANE))


def _unpack(packed, shapes, lead=1):
    flat = packed.reshape(lead, -1)
    out, off = [], 0
    for shp in shapes:
        n = 1
        for s_ in shp:
            n *= s_
        n //= lead
        out.append(flat[:, off:off + n].reshape(shp))
        off += n
    return out


SMALL_ROWS = 336
REPL_ROWS = 64


def kernel(x, meta_tokens, norm_mix, w_in, conv_qkv, a_log, dt_bias, head_norm, w_pool, pool_scale, w_out, norm_ffn, w_up, conv_ffn, w_down, norm_final, loss_target, m_meta_tokens, m_norm_mix, m_w_in, m_conv_qkv, m_a_log, m_dt_bias, m_head_norm, m_w_pool, m_pool_scale, m_w_out, m_norm_ffn, m_w_up, m_conv_ffn, m_w_down, m_norm_final, v_meta_tokens, v_norm_mix, v_w_in, v_conv_qkv, v_a_log, v_dt_bias, v_head_norm, v_w_pool, v_pool_scale, v_w_out, v_norm_ffn, v_w_up, v_conv_ffn, v_w_down, v_norm_final):
    seq = x.shape[1]
    t = ROW0 + seq
    assert t % TE == 0 and t % (MM_TILES * 16) == 0 and t % (GC * CH) == 0
    depth = w_in.shape[0]
    assert depth == 2
    cin = w_in.shape[2]

    def mixer_params(l, g_in, conv_q, conv_f, wp):
        wf = jnp.transpose(g_in, (1, 0, 2)).reshape(D, NDEV * cin)
        zpad = jnp.zeros((D, LANE - H), BF16)
        return dict(
            w_qkv=wf[:, 0:QKV], w_z=wf[:, QKV:QKV + D],
            w_ba=jnp.concatenate([wf[:, 4096:4104], zpad, wf[:, 4104:4112], zpad], axis=1),
            w_pl=wf[:, 4112:4624], w_gate=wf[:, 4624:6672], conv_qkv=conv_q, conv_ffn=conv_f,
            w_pool_bd=_pool_block_diag(wp),
            norm_mix=norm_mix[l].reshape(1, D), norm_ffn=norm_ffn[l].reshape(1, D),
            pool_scale=pool_scale[l].reshape(1, D), head_norm=head_norm[l].reshape(1, DH),
            a_row=_pad_lanes(a_log[l]), dt_row=_pad_lanes(dt_bias[l]))

    def late_params(g_up, g_out, g_down):
        return dict(w_out=g_out.reshape(D, D), w_up=g_up, w_down=g_down.reshape(4, FB, D))

    small_shapes = [conv_qkv.shape, conv_ffn.shape, w_pool.shape, meta_tokens.shape]
    small = _pack([conv_qkv, conv_ffn, w_pool, meta_tokens], SMALL_ROWS)
    w_in_b, w_up_b, w_out_b, w_down_b = w_in.astype(BF16), w_up.astype(BF16), w_out.astype(BF16), w_down.astype(BF16)
    g_in0, g_small = _gather([w_in_b[0], small], "gather_first")
    s_cq, s_cf, s_wp, s_mt = _unpack(g_small, [(NDEV,) + shp for shp in small_shapes], lead=NDEV)
    conv_qkv_full = jnp.transpose(s_cq, (1, 2, 0, 3)).reshape(depth, 4, QKV)
    conv_ffn_blk = jnp.transpose(s_cf, (1, 0, 2, 3))
    w_pool_full = jnp.transpose(s_wp, (1, 2, 3, 0, 4)).reshape(depth, 4, DH, 2 * DH).astype(BF16)
    meta_full = jnp.transpose(s_mt, (1, 0, 2)).reshape(N_META, D)

    h = jnp.concatenate([jnp.zeros((LEAD, D), F32), meta_full, x[0]], axis=0)
    p0 = mixer_params(0, g_in0, conv_qkv_full[0], conv_ffn_blk[0], w_pool_full[0])
    h, sv0, got0, p0 = _layer_fwd(h, p0, "l0", (w_up_b[0], w_out_b[0], w_down_b[0], w_in_b[1]),
                                  lambda got: late_params(*got[:3]))
    p1 = mixer_params(1, got0[3], conv_qkv_full[1], conv_ffn_blk[1], w_pool_full[1])
    h, sv1, _, p1 = _layer_fwd(h, p1, "l1", (w_up_b[1], w_out_b[1], w_down_b[1]), lambda got: late_params(*got))
    layers = [p0, p1]
    saved = [sv0, sv1]
    target = jnp.concatenate([jnp.zeros((ROW0, D), F32), loss_target[0]], axis=0)
    dh, dh_b, d_norm_final, loss_row = _loss_bwd(h, target, norm_final.reshape(1, D), "loss")

    def w_in_blocks(gr):
        return jnp.transpose(gr["w_in"].reshape(D, NDEV, cin), (1, 0, 2)).astype(BF16)

    grads = [None] * depth
    dh, dh_b, grads[1], (l_up1, l_down1, l_out1) = _layer_bwd(dh, dh_b, layers[1], saved[1], "l1")
    dh, dh_b, grads[0], (l_up0, l_down0, l_out0, l_in1) = _layer_bwd(dh, dh_b, layers[0], saved[0], "l0", (w_in_blocks(grads[1]),))
    grad_x = dh[ROW0:].reshape(1, seq, D)
    d_meta = dh[LEAD:ROW0]

    stk = lambda name: jnp.stack([grads[l][name] for l in range(depth)], axis=0)
    cq = conv_qkv.shape[2]
    pw = w_pool.shape[3]
    s_cq = jnp.transpose(stk("conv_qkv").reshape(depth, 4, NDEV, cq), (2, 0, 1, 3))
    s_cf = jnp.transpose(stk("conv_ffn"), (1, 0, 2, 3))
    s_wp = jnp.transpose(stk("w_pool").reshape(depth, 4, DH, NDEV, pw), (3, 0, 1, 2, 4))
    s_mt = jnp.transpose(d_meta.reshape(N_META, NDEV, D // NDEV), (1, 0, 2))
    b_small = _pack([s_cq, s_cf, s_wp, s_mt], SMALL_ROWS, lead=NDEV)
    l_in0, l_small = _scatter([w_in_blocks(grads[0]), b_small], "exchange_last")

    r_in = _adamw((l_in0, l_in1), w_in, m_w_in, v_w_in, "adamw_w_in")
    r_up = _adamw((l_up0, l_up1), w_up, m_w_up, v_w_up, "adamw_w_up")
    r_out = _adamw((l_out0, l_out1), w_out, m_w_out, v_w_out, "adamw_w_out")
    r_down = _adamw((l_down0, l_down1), w_down, m_w_down, v_w_down, "adamw_w_down")
    r_small = _adamw((l_small,), small[None], _pack([m_conv_qkv, m_conv_ffn, m_w_pool, m_meta_tokens], SMALL_ROWS)[None],
                     _pack([v_conv_qkv, v_conv_ffn, v_w_pool, v_meta_tokens], SMALL_ROWS)[None], "adamw_small")
    r_small = [_unpack(o_[0], small_shapes) for o_ in r_small]

    repl_shapes = [norm_mix.shape, a_log.shape, dt_bias.shape, head_norm.shape, pool_scale.shape, norm_ffn.shape,
                   norm_final.shape, (1,)]
    rp = lambda name, n: jnp.stack([grads[l][name][0, :n] for l in range(depth)], axis=0)
    part = _pack([rp("norm_mix", D), rp("a_log", H), rp("dt_bias", H), rp("head_norm", DH), rp("pool_scale", D),
                  rp("norm_ffn", D), d_norm_final[0], loss_row[0, 0:1]], REPL_ROWS)
    (l_repl,) = _gather([part], "gather_replicated")
    zero1 = jnp.zeros((1,), F32)
    r_repl = _adamw(
        (l_repl,), _pack([norm_mix, a_log, dt_bias, head_norm, pool_scale, norm_ffn, norm_final, zero1], REPL_ROWS)[None],
        _pack([m_norm_mix, m_a_log, m_dt_bias, m_head_norm, m_pool_scale, m_norm_ffn, m_norm_final, zero1], REPL_ROWS)[None],
        _pack([v_norm_mix, v_a_log, v_dt_bias, v_head_norm, v_pool_scale, v_norm_ffn, v_norm_final, zero1], REPL_ROWS)[None],
        "adamw_replicated")
    r_repl = [_unpack(o_[0], repl_shapes) for o_ in r_repl]
    loss = r_repl[0][7].reshape(())

    def leaf(kind):
        sm, rr = r_small[kind], r_repl[kind]
        return [sm[3], rr[0], r_in[kind], sm[0], rr[1], rr[2], rr[3], sm[2], rr[4], r_out[kind], rr[5], r_up[kind],
                sm[1], r_down[kind], rr[6]]

    return (loss, grad_x, *leaf(0), *leaf(1), *leaf(2), *leaf(3))
```

```python
import jax
import jax.numpy as jnp
from jax import lax
from jax.experimental import pallas as pl
from jax.experimental.pallas import tpu as pltpu

F32 = jnp.float32
BF16 = jnp.bfloat16
MESH = pl.DeviceIdType.MESH

D = 1024
H = 8
DH = 128
CH = 64
N_META = 16
LEAD = 48
ROW0 = LEAD + N_META
QKV = 3 * D
POOL_W = 512
POOL_WINDOWS = (2, 4, 8, 16)
FB = 704
NDEV = 8
EPS = 1e-6
MM_TILES = 12
TE = 192
LANE = 128
SUB = 8
VMEM_CAP = 56 << 20

ADAM_LR, ADAM_B1, ADAM_B2, ADAM_EPS, ADAM_WD, ADAM_STEP = 0.001, 0.9, 0.999, 1e-08, 0.01, 10

_NN = (((1,), (0,)), ((), ()))
_NT = (((1,), (1,)), ((), ()))
_TN = (((0,), (0,)), ((), ()))


def _dot(a, b, dims=_NN, precision=None):
    return lax.dot_general(a, b, dims, precision=precision, preferred_element_type=F32)


def _bdot(a, b, dims=_NN):
    return _dot(a.astype(BF16), b.astype(BF16), dims)


def _nbytes(shape, dtype):
    n = 1
    for s in shape:
        n *= s
    return n * jnp.dtype(dtype).itemsize


def _params(sem, block_bytes):
    limit = min(VMEM_CAP, 2 * block_bytes + (20 << 20))
    return pltpu.CompilerParams(dimension_semantics=sem, vmem_limit_bytes=limit)


PIN_BYTES = 12 << 20


def _pallas_call(body, *, out_shape, **kw):
    call = pl.pallas_call

    def big(s):
        return len(s.shape) >= 2 and s.shape[-1] >= D and _nbytes(s.shape, s.dtype) >= PIN_BYTES

    pinned = jax.tree.map(lambda s: pltpu.HBM(s.shape, s.dtype) if big(s) else s, out_shape)

    def run(*args):
        return call(body, out_shape=pinned, **kw)(
            *[pltpu.with_memory_space_constraint(a, pltpu.HBM) if big(a) else a for a in args])

    return run


def _sigmoid(x):
    return 1.0 / (1.0 + jnp.exp(-x))


def _col_tile(n):
    for t in (1024, 512, 256, 128):
        if n % t == 0:
            return t
    return n


def _matmul(a, b, *, dims, grid, a_spec, b_spec, o_spec, out_shape, name, red_axis=None, res=None):
    o_blk = tuple(s for s in o_spec.block_shape if s is not None)
    via_scratch = red_axis is not None and out_shape.dtype != F32

    def body(*refs):
        if res is None:
            a_ref, b_ref, o_ref = refs[:3]
        else:
            a_ref, b_ref, r_ref, o_ref = refs[:4]
        part = _dot(a_ref[...], b_ref[...], dims)
        if red_axis is None:
            if res is not None:
                part = part + r_ref[...]
            o_ref[...] = part.astype(o_ref.dtype)
        else:
            acc = refs[-1] if via_scratch else o_ref
            r = pl.program_id(red_axis)

            @pl.when(r == 0)
            def _():
                acc[...] = part + r_ref[...] if res is not None else part

            @pl.when(r > 0)
            def _():
                acc[...] += part

            if via_scratch:
                @pl.when(r == grid[red_axis] - 1)
                def _():
                    o_ref[...] = acc[...].astype(o_ref.dtype)

    def blk(spec, arr):
        return _nbytes([s for s in spec.block_shape if s is not None], arr.dtype)

    ins = [a, b] + ([res] if res is not None else [])
    specs = [a_spec, b_spec] + ([o_spec] if res is not None else [])
    nb = blk(a_spec, a) + blk(b_spec, b) + 2 * _nbytes(o_blk, F32)
    sem = tuple("arbitrary" if i == red_axis else "parallel" for i in range(len(grid)))
    return _pallas_call(
        body, name=name, grid=grid, in_specs=specs, out_specs=o_spec, out_shape=out_shape,
        scratch_shapes=[pltpu.VMEM(o_blk, F32)] if via_scratch else [],
        compiler_params=_params(sem, nb),
    )(*ins)


def _row_tiles(m, row_bytes, fixed_bytes, temp_row_bytes=0):
    for nt in (MM_TILES // 2, MM_TILES):
        tm = m // nt
        if 2 * (row_bytes * tm + fixed_bytes) + temp_row_bytes * tm <= VMEM_CAP - (10 << 20):
            return nt
    return MM_TILES


def _mm(a, b, out_dtype, name, res=None, dims=_NN):
    m, k = a.shape
    n = b.shape[1] if dims == _NN else b.shape[0]
    tn = _col_tile(n)
    nt = _row_tiles(m, 2 * k + tn * (jnp.dtype(out_dtype).itemsize + (4 if res is not None else 0)), 2 * k * tn, 4 * tn)
    tm = m // nt
    if dims == _NN:
        b_spec = pl.BlockSpec((k, tn), lambda j, i: (0, j))
    else:
        b_spec = pl.BlockSpec((tn, k), lambda j, i: (j, 0))
    return _matmul(
        a, b, dims=dims, grid=(n // tn, nt), a_spec=pl.BlockSpec((tm, k), lambda j, i: (i, 0)), b_spec=b_spec,
        o_spec=pl.BlockSpec((tm, tn), lambda j, i: (i, j)), out_shape=jax.ShapeDtypeStruct((m, n), out_dtype),
        name=name, res=res)


def _mm_du2(dhg, dhv, w_up, name):
    g, t, k = dhg.shape
    n = w_up.shape[1]
    nt = _row_tiles(t, 4 * k + 4 * n, 4 * k * n, 4 * n)
    tm = t // nt

    def body(ag_ref, av_ref, bg_ref, bv_ref, o_ref):
        part = _dot(ag_ref[...], bg_ref[...], _NT) + _dot(av_ref[...], bv_ref[...], _NT)
        r = pl.program_id(1)

        @pl.when(r == 0)
        def _():
            o_ref[...] = part

        @pl.when(r > 0)
        def _():
            o_ref[...] += part

    a_spec = pl.BlockSpec((None, tm, k), lambda i, g_: (g_, i, 0))
    return _pallas_call(
        body, name=name, grid=(nt, g),
        in_specs=[a_spec, a_spec, pl.BlockSpec((None, n, k), lambda i, g_: (g_, 0, 0)),
                  pl.BlockSpec((None, n, k), lambda i, g_: (g_ + g, 0, 0))],
        out_specs=pl.BlockSpec((tm, n), lambda i, g_: (i, 0)), out_shape=jax.ShapeDtypeStruct((t, n), F32),
        compiler_params=_params(("parallel", "arbitrary"), 4 * tm * k + 4 * n * k + 8 * tm * n),
    )(dhg, dhv, w_up, w_up)


def _mm_sum_nt(pairs, name):
    m, n = pairs[0][0].shape[0], pairs[0][1].shape[0]
    tm, tn = m // MM_TILES, min(512, n)
    np_ = len(pairs)

    def body(*refs):
        acc = _dot(refs[0][...], refs[1][...], _NT)
        for k in range(1, np_):
            acc = acc + _dot(refs[2 * k][...], refs[2 * k + 1][...], _NT)
        refs[-1][...] = acc

    specs, ins, nb = [], [], 2 * tm * tn * 4
    for a, b in pairs:
        k = a.shape[1]
        specs += [pl.BlockSpec((tm, k), lambda j, i: (i, 0)), pl.BlockSpec((tn, k), lambda j, i: (j, 0))]
        ins += [a, b]
        nb += 2 * k * (tm + tn)
    return _pallas_call(
        body, name=name, grid=(n // tn, MM_TILES), in_specs=specs, out_specs=pl.BlockSpec((tm, tn), lambda j, i: (i, j)),
        out_shape=jax.ShapeDtypeStruct((m, n), F32), compiler_params=_params(("parallel", "parallel"), nb),
    )(*ins)


def _mm_tn(a, g, name, out_dtype=F32):
    m, k = a.shape
    n = g.shape[1]
    tn = _col_tile(n)
    nt = _row_tiles(m, 2 * k + 2 * tn, 4 * k * tn)
    tm = m // nt
    return _matmul(
        a, g, dims=_TN, grid=(n // tn, nt), red_axis=1, a_spec=pl.BlockSpec((tm, k), lambda j, i: (i, 0)),
        b_spec=pl.BlockSpec((tm, tn), lambda j, i: (i, j)), o_spec=pl.BlockSpec((k, tn), lambda j, i: (0, j)),
        out_shape=jax.ShapeDtypeStruct((k, n), out_dtype), name=name)


def _mm_up(u, w_up, name):
    t = u.shape[0]
    g = w_up.shape[0]
    nt = _row_tiles(t, 2 * D + 2 * FB, 2 * D * FB, 4 * FB)
    tm = t // nt
    return _matmul(
        u, w_up, dims=_NN, grid=(g, nt), a_spec=pl.BlockSpec((tm, D), lambda g_, i: (i, 0)),
        b_spec=pl.BlockSpec((None, D, FB), lambda g_, i: (g_, 0, 0)),
        o_spec=pl.BlockSpec((None, tm, FB), lambda g_, i: (g_, i, 0)),
        out_shape=jax.ShapeDtypeStruct((g, t, FB), BF16), name=name)


def _mm_blocks_red(a, b, name, dims, res=None):
    g, t, k = a.shape
    n = b.shape[2] if dims == _NN else b.shape[1]
    nt = _row_tiles(t, 2 * k + n * (8 if res is not None else 4), 2 * k * n, 4 * n)
    tm = t // nt
    return _matmul(
        a, b, dims=dims, grid=(nt, g), red_axis=1, a_spec=pl.BlockSpec((None, tm, k), lambda i, g_: (g_, i, 0)),
        b_spec=pl.BlockSpec((None,) + b.shape[1:], lambda i, g_: (g_, 0, 0)),
        o_spec=pl.BlockSpec((tm, n), lambda i, g_: (i, 0)), out_shape=jax.ShapeDtypeStruct((t, n), F32),
        name=name, res=res)


def _mm_to_blocks(a, b, name):
    t, k = a.shape
    g, n, _ = b.shape
    nt = _row_tiles(t, 2 * k + 2 * n, 2 * k * n, 4 * n)
    tm = t // nt
    return _matmul(
        a, b, dims=_NT, grid=(g, nt), a_spec=pl.BlockSpec((tm, k), lambda g_, i: (i, 0)),
        b_spec=pl.BlockSpec((None, n, k), lambda g_, i: (g_, 0, 0)),
        o_spec=pl.BlockSpec((None, tm, n), lambda g_, i: (g_, i, 0)),
        out_shape=jax.ShapeDtypeStruct((g, t, n), BF16), name=name)


def _mm_tn_blocks(a, g, name, a_blocked, g_blocked, out_dtype=F32):
    nb = a.shape[0] if a_blocked else g.shape[0]
    t = a.shape[-2]
    k, n = a.shape[-1], g.shape[-1]
    nt = _row_tiles(t, 2 * k + 2 * n, 4 * k * n)
    tm = t // nt
    a_spec = (pl.BlockSpec((None, tm, k), lambda g_, i: (g_, i, 0)) if a_blocked
              else pl.BlockSpec((tm, k), lambda g_, i: (i, 0)))
    g_spec = (pl.BlockSpec((None, tm, n), lambda g_, i: (g_, i, 0)) if g_blocked
              else pl.BlockSpec((tm, n), lambda g_, i: (i, 0)))
    return _matmul(
        a, g, dims=_TN, grid=(nb, nt), red_axis=1, a_spec=a_spec, b_spec=g_spec,
        o_spec=pl.BlockSpec((None, k, n), lambda g_, i: (g_, 0, 0)),
        out_shape=jax.ShapeDtypeStruct((nb, k, n), out_dtype), name=name)


def _rows(cols, n=None):
    if n is None:
        return pl.BlockSpec((TE, cols), lambda i: (i, 0))
    return pl.BlockSpec((TE, cols), lambda i: (n - 1 - i, 0))


def _whole(shape):
    return pl.BlockSpec(shape, lambda *_: (0,) * len(shape))


def _row_ids(i, rows=TE):
    return i * rows + lax.broadcasted_iota(jnp.int32, (rows, 1), 0)


def _rmsnorm_fwd(h, gain, name):
    t = h.shape[0]

    def body(h_ref, g_ref, u_ref):
        x = h_ref[...]
        r = lax.rsqrt(jnp.mean(x * x, axis=-1, keepdims=True) + EPS)
        u_ref[...] = (x * r * g_ref[...]).astype(BF16)

    return _pallas_call(
        body, name=name, grid=(t // TE,), in_specs=[_rows(D), _whole((1, D))], out_specs=_rows(D),
        out_shape=jax.ShapeDtypeStruct((t, D), BF16), compiler_params=_params(("parallel",), 3 * TE * D * 4),
    )(h, gain)


def _rmsnorm_bwd(x, du, dres, gain, name):
    t = x.shape[0]

    def body(x_ref, du_ref, dr_ref, g_ref, dx_ref, dxb_ref, dg_ref):
        i = pl.program_id(0)
        xv = x_ref[...]
        r = lax.rsqrt(jnp.mean(xv * xv, axis=-1, keepdims=True) + EPS)
        gdy = du_ref[...] * g_ref[...]
        dx = dr_ref[...] + r * gdy - xv * (r * r * r) * jnp.mean(xv * gdy, axis=-1, keepdims=True)
        dx = jnp.where(_row_ids(i) >= LEAD, dx, 0.0)
        dx_ref[...] = dx
        dxb_ref[...] = dx.astype(BF16)
        part = jnp.sum(du_ref[...] * xv * r, axis=0, keepdims=True)

        @pl.when(i == 0)
        def _():
            dg_ref[...] = part

        @pl.when(i > 0)
        def _():
            dg_ref[...] += part

    return _pallas_call(
        body, name=name, grid=(t // TE,), in_specs=[_rows(D), _rows(D), _rows(D), _whole((1, D))],
        out_specs=[_rows(D), _rows(D), _whole((1, D))],
        out_shape=[jax.ShapeDtypeStruct((t, D), F32), jax.ShapeDtypeStruct((t, D), BF16),
                   jax.ShapeDtypeStruct((1, D), F32)],
        compiler_params=_params(("arbitrary",), 5 * TE * D * 4),
    )(x, du, dres, gain)


def _loss_bwd(h, target, gain, name):
    t = h.shape[0]

    def body(h_ref, t_ref, g_ref, dx_ref, dxb_ref, dg_ref, loss_ref):
        i = pl.program_id(0)
        xv = h_ref[...]
        gain_v = g_ref[...]
        r = lax.rsqrt(jnp.mean(xv * xv, axis=-1, keepdims=True) + EPS)
        real = _row_ids(i) >= ROW0
        err = jnp.where(real, xv * r * gain_v - t_ref[...], 0.0)
        dy = err * (1.0 / D)
        gdy = dy * gain_v
        dx = r * gdy - xv * (r * r * r) * jnp.mean(xv * gdy, axis=-1, keepdims=True)
        dx_ref[...] = dx
        dxb_ref[...] = dx.astype(BF16)
        dgp = jnp.sum(dy * xv * r, axis=0, keepdims=True)
        lp = 0.5 * jnp.sum(jnp.mean(err * err, axis=-1, keepdims=True), axis=0, keepdims=True)

        @pl.when(i == 0)
        def _():
            dg_ref[...] = dgp
            loss_ref[...] = jnp.broadcast_to(lp, (1, LANE))

        @pl.when(i > 0)
        def _():
            dg_ref[...] += dgp
            loss_ref[...] += jnp.broadcast_to(lp, (1, LANE))

    return _pallas_call(
        body, name=name, grid=(t // TE,), in_specs=[_rows(D), _rows(D), _whole((1, D))],
        out_specs=[_rows(D), _rows(D), _whole((1, D)), _whole((1, LANE))],
        out_shape=[jax.ShapeDtypeStruct((t, D), F32), jax.ShapeDtypeStruct((t, D), BF16),
                   jax.ShapeDtypeStruct((1, D), F32), jax.ShapeDtypeStruct((1, LANE), F32)],
        compiler_params=_params(("arbitrary",), 4 * TE * D * 4),
    )(h, target, gain)


def _seq_scratch(cols):
    return pltpu.VMEM((-(-cols // LANE), TE + SUB, LANE), F32)


def _taps(scr, c, wd, first, n, k):
    return [scr[c, first - (k - 1) + j:first - (k - 1) + j + n, 0:wd] for j in range(k)]


def _stage_history(scr, i):
    @pl.when(i == 0)
    def _():
        scr[...] = jnp.zeros(scr.shape, F32)

    @pl.when(i > 0)
    def _():
        scr[:, 0:SUB, :] = scr[:, TE:TE + SUB, :]


def _stage_future(scr, i):
    @pl.when(i == 0)
    def _():
        scr[...] = jnp.zeros(scr.shape, F32)

    @pl.when(i > 0)
    def _():
        scr[:, TE:TE + SUB, :] = scr[:, 0:SUB, :]


def _conv(tp, w):
    out = w[0:1] * tp[0]
    for j in range(1, len(tp)):
        out = out + w[j:j + 1] * tp[j]
    return out


def _conv_t(ds, c, wd, w):
    k = w.shape[0]
    out = w[k - 1:k] * ds[c, 0:TE, 0:wd]
    for j in range(k - 1):
        out = out + w[j:j + 1] * ds[c, k - 1 - j:k - 1 - j + TE, 0:wd]
    return out


def _gdn_pre_fwd(x, w, name):
    t = x.shape[0]

    def body(x_ref, w_ref, o_ref, xs):
        _stage_history(xs, pl.program_id(0))
        for hh in range(3 * H):
            sl = slice(hh * DH, (hh + 1) * DH)
            xs[hh, SUB:SUB + TE, :] = x_ref[:, sl].astype(F32)
            cv = _conv(_taps(xs, hh, DH, SUB, TE, 4), w_ref[:, sl])
            s = cv * _sigmoid(cv)
            if hh < 2 * H:
                s = s * lax.rsqrt(jnp.sum(s * s, axis=-1, keepdims=True) + EPS)
                if hh < H:
                    s = s * (DH ** -0.5)
            o_ref[:, sl] = s

    return _pallas_call(
        body, name=name, grid=(t // TE,), in_specs=[_rows(QKV), _whole((4, QKV))], out_specs=_rows(QKV),
        out_shape=jax.ShapeDtypeStruct((t, QKV), F32), scratch_shapes=[_seq_scratch(QKV)],
        compiler_params=_params(("arbitrary",), TE * QKV * 8),
    )(x, w)


def _gdn_pre_bwd(x, w, dqkv, name):
    t = x.shape[0]
    n = t // TE
    hb = TE // 16

    def body(x_ref, xp_ref, w_ref, d_ref, dx_ref, dw_ref, xs, ds):
        i = pl.program_id(0)

        @pl.when(i == 0)
        def _():
            dw_ref[...] = jnp.zeros_like(dw_ref)

        _stage_future(ds, i)
        for hh in range(3 * H):
            sl = slice(hh * DH, (hh + 1) * DH)
            xs[hh, 0:SUB, :] = jnp.where(i == n - 1, 0.0, xp_ref[SUB:2 * SUB, sl].astype(F32))
            xs[hh, SUB:SUB + TE, :] = x_ref[:, sl].astype(F32)
            wv = w_ref[:, sl]
            tp = _taps(xs, hh, DH, SUB, TE, 4)
            cv = _conv(tp, wv)
            sg = _sigmoid(cv)
            s = cv * sg
            dsv = d_ref[:, sl]
            if hh < 2 * H:
                if hh < H:
                    dsv = dsv * (DH ** -0.5)
                r = lax.rsqrt(jnp.sum(s * s, axis=-1, keepdims=True) + EPS)
                dsv = r * dsv - s * (r * r * r) * jnp.sum(s * dsv, axis=-1, keepdims=True)
            dcv = dsv * (sg * (1.0 + cv * (1.0 - sg)))
            ds[hh, 0:TE, :] = dcv
            dx_ref[:, sl] = _conv_t(ds, hh, DH, wv).astype(BF16)
            dw_ref[:, sl] += jnp.concatenate([jnp.sum(tp[j] * dcv, axis=0, keepdims=True) for j in range(4)], axis=0)

    return _pallas_call(
        body, name=name, grid=(n,),
        in_specs=[_rows(QKV, n), pl.BlockSpec((16, QKV), lambda i: (jnp.maximum((n - 1 - i) * hb - 1, 0), 0)),
                  _whole((4, QKV)), _rows(QKV, n)],
        out_specs=[_rows(QKV, n), _whole((4, QKV))],
        out_shape=[jax.ShapeDtypeStruct((t, QKV), BF16), jax.ShapeDtypeStruct((4, QKV), F32)],
        scratch_shapes=[_seq_scratch(QKV), _seq_scratch(QKV)],
        compiler_params=_params(("arbitrary",), TE * QKV * 14),
    )(x, x, w, dqkv)


def _softplus(x):
    return jnp.maximum(x, 0.0) + jnp.log(1.0 + jnp.exp(-jnp.abs(x)))


def _gates_fwd(ba, a_row, dt_row, name):
    t = ba.shape[0]

    def body(ba_ref, a_ref, dt_ref, b_out, g_out):
        real = _row_ids(pl.program_id(0)) >= LEAD
        b_out[...] = jnp.where(real, _sigmoid(ba_ref[:, 0:LANE]), 0.0)
        g = -jnp.exp(a_ref[...]) * _softplus(ba_ref[:, LANE:2 * LANE] + dt_ref[...])
        g_out[...] = jnp.where(real, g, 0.0)

    return _pallas_call(
        body, name=name, grid=(t // TE,), in_specs=[_rows(2 * LANE), _whole((1, LANE)), _whole((1, LANE))],
        out_specs=[_rows(LANE), _rows(LANE)],
        out_shape=[jax.ShapeDtypeStruct((t, LANE), F32), jax.ShapeDtypeStruct((t, LANE), F32)],
        compiler_params=_params(("parallel",), TE * LANE * 16),
    )(ba, a_row, dt_row)


def _gates_bwd(ba, a_row, dt_row, dbeta, dg, name):
    t = ba.shape[0]

    def body(ba_ref, a_ref, dt_ref, db_ref, dg_ref, dba_ref, da_out, ddt_out):
        i = pl.program_id(0)
        real = _row_ids(i) >= LEAD
        beta = _sigmoid(ba_ref[:, 0:LANE])
        draw_b = jnp.where(real, db_ref[...] * beta * (1.0 - beta), 0.0)
        pre = ba_ref[:, LANE:2 * LANE] + dt_ref[...]
        neg_a = -jnp.exp(a_ref[...])
        dgv = jnp.where(real, dg_ref[...], 0.0)
        draw_a = dgv * neg_a * _sigmoid(pre)
        dba_ref[:, 0:LANE] = draw_b.astype(BF16)
        dba_ref[:, LANE:2 * LANE] = draw_a.astype(BF16)
        dal = jnp.sum(dgv * neg_a * _softplus(pre), axis=0, keepdims=True)
        ddt = jnp.sum(draw_a, axis=0, keepdims=True)

        @pl.when(i == 0)
        def _():
            da_out[...] = dal
            ddt_out[...] = ddt

        @pl.when(i > 0)
        def _():
            da_out[...] += dal
            ddt_out[...] += ddt

    return _pallas_call(
        body, name=name, grid=(t // TE,),
        in_specs=[_rows(2 * LANE), _whole((1, LANE)), _whole((1, LANE)), _rows(LANE), _rows(LANE)],
        out_specs=[_rows(2 * LANE), _whole((1, LANE)), _whole((1, LANE))],
        out_shape=[jax.ShapeDtypeStruct((t, 2 * LANE), BF16), jax.ShapeDtypeStruct((1, LANE), F32),
                   jax.ShapeDtypeStruct((1, LANE), F32)],
        compiler_params=_params(("arbitrary",), TE * LANE * 24),
    )(ba, a_row, dt_row, dbeta, dg)


_OFFSETS = [(dx, dy, dc) for dx in (0, 1) for dy in (0, 1) for dc in (0, 1)][1:]
NPEER = len(_OFFSETS)
ANY_SPEC = pl.BlockSpec(memory_space=pl.ANY)


def _place():
    return lax.axis_index("x"), lax.axis_index("y"), lax.axis_index("c")


def _index(p):
    return 4 * p[0] + 2 * p[1] + p[2]


def _comm_scratch(n):
    return [pltpu.SemaphoreType.DMA((n * NPEER,)), pltpu.SemaphoreType.DMA((n * NPEER,)), pltpu.SemaphoreType.DMA((n,))]


def _scatter_copies(ins, outs, send, recv):
    me = _place()
    mi = _index(me)
    res = []
    for j, d in enumerate(_OFFSETS):
        peer = tuple(1 - v if bit else v for v, bit in zip(me, d))
        pi = _index(peer)
        for k in range(len(ins)):
            sem = k * NPEER + j
            mine = pltpu.make_async_remote_copy(src_ref=ins[k].at[pi], dst_ref=outs[k].at[mi], send_sem=send.at[sem],
                                                recv_sem=recv.at[sem], device_id=peer, device_id_type=MESH)
            theirs = pltpu.make_async_remote_copy(src_ref=ins[k].at[pi], dst_ref=outs[k].at[pi], send_sem=send.at[sem],
                                                  recv_sem=recv.at[sem], device_id=peer, device_id_type=MESH)
            res.append((mine, theirs))
    return res


def _scatter_own(ins, outs, loc):
    mi = _index(_place())
    return [pltpu.make_async_copy(ins[k].at[mi], outs[k].at[mi], loc.at[k]) for k in range(len(ins))]


def _scatter_start(ins, outs, send, recv, loc):
    for cp in _scatter_own(ins, outs, loc):
        cp.start()
    for mine, _ in _scatter_copies(ins, outs, send, recv):
        mine.start()


def _scatter_wait(ins, outs, send, recv, loc):
    cps = _scatter_copies(ins, outs, send, recv)
    for _, theirs in cps:
        theirs.wait_recv()
    for mine, _ in cps:
        mine.wait_send()
    for cp in _scatter_own(ins, outs, loc):
        cp.wait()


def _gather_parts(ins, outs, send, recv):
    x, y, c = _place()
    chips = [(1 - x, y), (x, 1 - y), (1 - x, 1 - y)]

    def cp(k, slot, src, block, to):
        return pltpu.make_async_remote_copy(src_ref=src, dst_ref=outs[k].at[_index(block)], send_sem=send.at[k * NPEER + slot],
                                            recv_sem=recv.at[k * NPEER + slot], device_id=to, device_id_type=MESH)

    return (x, y, c), (x, y, 1 - c), chips, cp


def _gather_start(ins, outs, send, recv, loc):
    me, sib, chips, cp = _gather_parts(ins, outs, send, recv)
    for k in range(len(ins)):
        pltpu.make_async_copy(ins[k], outs[k].at[_index(me)], loc.at[k]).start()
        cp(k, 0, ins[k], me, sib).start()
        for j, chip in enumerate(chips):
            cp(k, 1 + j, ins[k], me, (*chip, me[2])).start()


def _gather_forward(ins, outs, send, recv, loc):
    me, sib, chips, cp = _gather_parts(ins, outs, send, recv)
    for j, chip in enumerate(chips):
        blk = (*chip, me[2])
        for k in range(len(ins)):
            cp(k, 1 + j, ins[k], blk, me).wait_recv()
            cp(k, 4 + j, outs[k].at[_index(blk)], blk, sib).start()


def _gather_finish(ins, outs, send, recv, loc):
    me, sib, chips, cp = _gather_parts(ins, outs, send, recv)
    for k in range(len(ins)):
        cp(k, 0, ins[k], sib, me).wait_recv()
        for j, chip in enumerate(chips):
            cp(k, 4 + j, ins[k], (*chip, sib[2]), me).wait_recv()
        cp(k, 0, ins[k], me, sib).wait_send()
        for j, chip in enumerate(chips):
            cp(k, 1 + j, ins[k], me, (*chip, me[2])).wait_send()
            cp(k, 4 + j, outs[k].at[_index((*chip, me[2]))], (*chip, me[2]), sib).wait_send()
        pltpu.make_async_copy(ins[k], outs[k].at[_index(me)], loc.at[k]).wait()


def _gathered_shapes(arrs):
    return [jax.ShapeDtypeStruct((NDEV,) + a.shape, a.dtype) for a in arrs]


def _gather(arrs, name):
    n = len(arrs)

    def body(*refs):
        ins, outs, sems = refs[:n], refs[n:2 * n], refs[2 * n:]
        _gather_start(ins, outs, *sems)
        _gather_forward(ins, outs, *sems)
        _gather_finish(ins, outs, *sems)

    return _pallas_call(body, name=name, in_specs=[ANY_SPEC] * n, out_specs=[ANY_SPEC] * n,
                          out_shape=_gathered_shapes(arrs), scratch_shapes=_comm_scratch(n))(*arrs)


def _scatter(arrs, name):
    n = len(arrs)

    def body(*refs):
        ins, outs, sems = refs[:n], refs[n:2 * n], refs[2 * n:]
        _scatter_start(ins, outs, *sems)
        _scatter_wait(ins, outs, *sems)

    return _pallas_call(body, name=name, in_specs=[ANY_SPEC] * n, out_specs=[ANY_SPEC] * n,
                          out_shape=[jax.ShapeDtypeStruct(a.shape, a.dtype) for a in arrs],
                          scratch_shapes=_comm_scratch(n))(*arrs)


_BNN = (((2,), (1,)), ((0,), (0,)))
_BNT = (((2,), (2,)), ((0,), (0,)))
_BTN = (((1,), (1,)), ((0,), (0,)))


def _split(a):
    hi = a.astype(BF16)
    return hi, (a - hi.astype(F32)).astype(BF16)


def _dot3(a, b, dims=_BNN):
    ah, al = _split(a)
    bh, bl = _split(b)
    if dims != _BNN:
        return _dot(ah, bh, dims) + _dot(al, bh, dims) + _dot(ah, bl, dims)
    m = a.shape[1]
    r = _dot(jnp.concatenate([ah, al], axis=1), bh, _BNN)
    return r[:, :m] + r[:, m:] + _dot(ah, bl, _BNN)


def _tri_sum(mask, x):
    x1 = x.astype(BF16)
    r1 = x - x1.astype(F32)
    x2 = r1.astype(BF16)
    x3 = (r1 - x2.astype(F32)).astype(BF16)
    mb = mask.astype(BF16)
    return _dot(mb, x1) + _dot(mb, x2) + _dot(mb, x3)


GC = 3


def _rows_of(c):
    return slice(c * CH, (c + 1) * CH)


def _heads(ref, off):
    return jnp.stack([ref[_rows_of(c), off + h * DH:off + (h + 1) * DH] for c in range(GC) for h in range(H)])


def _cols(arrs):
    return jnp.stack([a[:, h:h + 1] for a in arrs for h in range(H)])


def _lanes(a):
    lane = lax.broadcasted_iota(jnp.int32, (CH, LANE), 1)
    out = jnp.zeros((CH, LANE), F32)
    for h in range(H):
        out = jnp.where(lane == h, a[h], out)
    return out


def _chunk_prep(qkv_ref, b_ref, g_ref):
    row = lax.broadcasted_iota(jnp.int32, (CH, CH), 0)
    col = lax.broadcasted_iota(jnp.int32, (CH, CH), 1)
    incl, strict = row >= col, row > col
    gcs = [_tri_sum(incl, g_ref[_rows_of(c), :]) for c in range(GC)]
    q, k, v = _heads(qkv_ref, 0), _heads(qkv_ref, D), _heads(qkv_ref, 2 * D)
    bcol, gcol = _cols([b_ref[_rows_of(c), :] for c in range(GC)]), _cols(gcs)
    grow = jnp.stack([gct[h:h + 1, :] for gct in [gc.T for gc in gcs] for h in range(H)])
    glast = _cols([gc[CH - 1:CH, :] for gc in gcs])
    dec = jnp.exp(jnp.where(incl[None], gcol - grow, -1e30))
    kb = k * bcol
    ab = _bdot(jnp.concatenate([kb, q], axis=1), k, _BNT)
    egc, ekc = jnp.exp(gcol), jnp.exp(glast - gcol)
    return dict(row=row, col=col, strict=strict[None], q=q, k=k, v=v, bcol=bcol, dec=dec, kb=kb,
                lm=jnp.where(strict[None], ab[:, :CH] * dec, 0.0), qk=ab[:, CH:] * dec, egc=egc, ekc=ekc,
                gth=jnp.exp(glast), qd=q * egc, kd=k * ekc, vb=v * bcol, kbg=kb * egc)


def _unit_lower_inverse(lm, eye):
    n = -lm
    x = eye + n
    pw = _dot3(n, n)
    for it in range(5):
        if it < 4:
            xp = _dot3(jnp.concatenate([x, pw], axis=1), pw)
            x = x + xp[:, :CH]
            pw = xp[:, CH:]
        else:
            x = x + _dot3(x, pw)
    return x


def _gdn_fwd(qkv, beta, g, name, gather=()):
    t = qkv.shape[0]
    nc = t // CH
    ns = nc // GC
    ng = len(gather)

    def body(qkv_ref, b_ref, g_ref, *rest):
        c_ins, (o_ref, sin_ref, vn_ref, ti_ref, w_ref) = rest[:ng], rest[ng:ng + 5]
        c_outs, state, sems = rest[ng + 5:2 * ng + 5], rest[2 * ng + 5], rest[2 * ng + 6:]
        step = pl.program_id(0)

        @pl.when(step == 0)
        def _():
            state[...] = jnp.zeros_like(state)
            if ng:
                _gather_start(c_ins, c_outs, *sems)

        if ng:
            @pl.when(step == max(ns - 4, 0))
            def _():
                _gather_forward(c_ins, c_outs, *sems)

            @pl.when(step == ns - 1)
            def _():
                _gather_finish(c_ins, c_outs, *sems)

        pr = _chunk_prep(qkv_ref, b_ref, g_ref)
        tinv = _unit_lower_inverse(pr["lm"], (pr["row"] == pr["col"]).astype(F32)[None])
        uw = _bdot(tinv, jnp.concatenate([pr["vb"], pr["kbg"]], axis=2), _BNN)
        u, w = uw[:, :, :DH], uw[:, :, DH:]
        s = state[...]
        for c in range(GC):
            hs = slice(c * H, (c + 1) * H)
            ws = _bdot(jnp.concatenate([w[hs], pr["qd"][hs]], axis=1), s, _BNN)
            vn = u[hs] - ws[:, :CH]
            o = ws[:, CH:] + _bdot(pr["qk"][hs], vn, _BNN)
            sin_ref[c] = s
            ti_ref[c] = tinv[hs]
            s = s * pr["gth"][hs] + _bdot(pr["kd"][hs], vn, _BTN)
            for h in range(H):
                sl = slice(h * DH, (h + 1) * DH)
                o_ref[_rows_of(c), sl] = o[h]
                vn_ref[_rows_of(c), sl] = vn[h]
                w_ref[_rows_of(c), sl] = w[c * H + h]
        state[...] = s

    chunk = lambda cols: pl.BlockSpec((GC * CH, cols), lambda c: (c, 0))
    outs = _pallas_call(
        body, name=name, grid=(ns,), in_specs=[chunk(QKV), chunk(LANE), chunk(LANE)] + [ANY_SPEC] * ng,
        out_specs=[chunk(D), pl.BlockSpec((GC, H, DH, DH), lambda c: (c, 0, 0, 0)), chunk(D),
                   pl.BlockSpec((GC, H, CH, CH), lambda c: (c, 0, 0, 0)), chunk(D)] + [ANY_SPEC] * ng,
        out_shape=[jax.ShapeDtypeStruct((t, D), F32), jax.ShapeDtypeStruct((nc, H, DH, DH), F32),
                   jax.ShapeDtypeStruct((t, D), F32), jax.ShapeDtypeStruct((nc, H, CH, CH), F32),
                   jax.ShapeDtypeStruct((t, D), F32)] + _gathered_shapes(gather),
        scratch_shapes=[pltpu.VMEM((H, DH, DH), F32)] + (_comm_scratch(ng) if ng else []),
        compiler_params=_params(("arbitrary",), 12 << 20),
    )(qkv, beta, g, *gather)
    return outs[:5], outs[5:]


def _gdn_bwd(qkv, beta, g, do, s_in, vnew, tinv, wsv, name, scatter=()):
    t = qkv.shape[0]
    nsteps = t // CH // GC
    ns = len(scatter)

    def body(qkv_ref, b_ref, g_ref, do_ref, sin_ref, vn_ref, ti_ref, w_ref, *rest):
        c_ins, (dqkv_ref, db_ref, dg_ref) = rest[:ns], rest[ns:ns + 3]
        c_outs, dstate, sems = rest[ns + 3:2 * ns + 3], rest[2 * ns + 3], rest[2 * ns + 4:]
        step = pl.program_id(0)

        @pl.when(step == 0)
        def _():
            dstate[...] = jnp.zeros_like(dstate)
            if ns:
                _scatter_start(c_ins, c_outs, *sems)

        if ns:
            @pl.when(step == nsteps - 1)
            def _():
                _scatter_wait(c_ins, c_outs, *sems)

        pr = _chunk_prep(qkv_ref, b_ref, g_ref)
        ti = jnp.concatenate([ti_ref[c] for c in range(GC)], axis=0)
        s = jnp.concatenate([sin_ref[c] for c in range(GC)], axis=0)
        w, vn, doh = _heads(w_ref, 0), _heads(vn_ref, 0), _heads(do_ref, 0)
        dqd = _bdot(doh, s, _BNT)
        dqk = _bdot(doh, vn, _BNT)
        qk_do = _bdot(pr["qk"], doh, _BTN)
        qd_do = _bdot(pr["qd"], doh, _BTN)
        ds = dstate[...]
        dvn_c, dkd_c, dw_c, dgt_c = [None] * GC, [None] * GC, [None] * GC, [None] * GC
        for c in reversed(range(GC)):
            hs = slice(c * H, (c + 1) * H)
            dvn_c[c] = _bdot(pr["kd"][hs], ds, _BNN) + qk_do[hs]
            dkd_c[c] = _bdot(vn[hs], ds, _BNT)
            dw_c[c] = -_bdot(dvn_c[c], s[hs], _BNT)
            dgt_c[c] = jnp.sum(jnp.sum(ds * s[hs], axis=2, keepdims=True), axis=1, keepdims=True)
            ds = ds * pr["gth"][hs] + qd_do[hs] - _bdot(w[hs], dvn_c[c], _BTN)
        dstate[...] = ds
        dvn, dkd, dw, dgt = (jnp.concatenate(parts, axis=0) for parts in (dvn_c, dkd_c, dw_c, dgt_c))
        duw = jnp.concatenate([dvn, dw], axis=2)
        dvk = _bdot(ti, duw, _BTN)
        dvb, dkbg = dvk[:, :, :DH], dvk[:, :, DH:]
        dti = _bdot(duw, jnp.concatenate([pr["vb"], pr["kbg"]], axis=2), _BNT)
        dl = -_dot3(_dot3(ti, dti, _BTN), ti, _BNT)
        dl = jnp.where(pr["strict"], dl, 0.0)
        dab = jnp.concatenate([dl * pr["dec"], dqk * pr["dec"]], axis=1)
        r1 = _bdot(dab, pr["k"], _BNN)
        dkb = r1[:, :CH] + dkbg * pr["egc"]
        dq = r1[:, CH:] + dqd * pr["egc"]
        dk = _bdot(dab, jnp.concatenate([pr["kb"], pr["q"]], axis=1), _BTN) + dkb * pr["bcol"] + dkd * pr["ekc"]
        m = dl * pr["lm"] + dqk * pr["qk"]
        mh, ml = _split(m)
        ones = jnp.ones((GC * H, CH, LANE), BF16)
        colsum = (_dot(mh, ones, _BTN) + _dot(ml, ones, _BTN))[:, :, 0:1]
        kdsum = jnp.sum(dkd * pr["kd"], axis=2, keepdims=True)
        dgc = (jnp.sum(m, axis=2, keepdims=True) - colsum + jnp.sum(dkbg * pr["kbg"], axis=2, keepdims=True)
               + jnp.sum(dqd * pr["qd"], axis=2, keepdims=True) - kdsum)
        dglast = jnp.sum(kdsum, axis=1, keepdims=True) + dgt * pr["gth"]
        last_row = lax.broadcasted_iota(jnp.int32, (1, CH, 1), 1) == CH - 1
        dgc = dgc + jnp.where(last_row, dglast, 0.0)
        dbeta = jnp.sum(dkb * pr["k"], axis=2, keepdims=True) + jnp.sum(dvb * pr["v"], axis=2, keepdims=True)
        dv = dvb * pr["bcol"]
        upper = pr["row"] <= pr["col"]
        for c in range(GC):
            hs = slice(c * H, (c + 1) * H)
            for h in range(H):
                dqkv_ref[_rows_of(c), h * DH:(h + 1) * DH] = dq[c * H + h]
                dqkv_ref[_rows_of(c), D + h * DH:D + (h + 1) * DH] = dk[c * H + h]
                dqkv_ref[_rows_of(c), 2 * D + h * DH:2 * D + (h + 1) * DH] = dv[c * H + h]
            db_ref[_rows_of(c), :] = _lanes(dbeta[hs])
            dg_ref[_rows_of(c), :] = _tri_sum(upper, _lanes(dgc[hs]))

    chunk = lambda cols: pl.BlockSpec((GC * CH, cols), lambda c: (nsteps - 1 - c, 0))
    sq = lambda a, b: pl.BlockSpec((GC, H, a, b), lambda c: (nsteps - 1 - c, 0, 0, 0))
    outs = _pallas_call(
        body, name=name, grid=(nsteps,),
        in_specs=[chunk(QKV), chunk(LANE), chunk(LANE), chunk(D), sq(DH, DH), chunk(D), sq(CH, CH), chunk(D)] + [ANY_SPEC] * ns,
        out_specs=[chunk(QKV), chunk(LANE), chunk(LANE)] + [ANY_SPEC] * ns,
        out_shape=[jax.ShapeDtypeStruct((t, QKV), F32), jax.ShapeDtypeStruct((t, LANE), F32),
                   jax.ShapeDtypeStruct((t, LANE), F32)] + [jax.ShapeDtypeStruct(a.shape, a.dtype) for a in scatter],
        scratch_shapes=[pltpu.VMEM((H, DH, DH), F32)] + (_comm_scratch(ns) if ns else []),
        compiler_params=_params(("arbitrary",), 16 << 20),
    )(qkv, beta, g, do, s_in, vnew, tinv, wsv, *scatter)
    return outs[:3], outs[3:]


def _pool_counts(row_ids, win):
    return jnp.minimum(jnp.maximum(row_ids - LEAD, 0) + 1, win).astype(F32)


def _pool_fwd(p, name):
    t = p.shape[0]
    ext = TE + 16

    def body(p_ref, o_ref, carry):
        i = pl.program_id(0)

        @pl.when(i == 0)
        def _():
            carry[...] = jnp.zeros_like(carry)

        ids = _row_ids(i)
        for gi, win in enumerate(POOL_WINDOWS):
            sl = slice(gi * LANE, (gi + 1) * LANE)
            xv = p_ref[:, sl]
            s = jnp.concatenate([carry[:, sl], xv], axis=0)
            sh = 1
            while sh < win:
                s = s + pltpu.roll(s, sh, 0)
                sh *= 2
            o_ref[:, sl] = (s[16:ext] / _pool_counts(ids, win) - xv).astype(BF16)
            carry[:, sl] = xv[TE - 16:TE]

    return _pallas_call(
        body, name=name, grid=(t // TE,), in_specs=[_rows(POOL_W)], out_specs=_rows(POOL_W),
        out_shape=jax.ShapeDtypeStruct((t, POOL_W), BF16), scratch_shapes=[pltpu.VMEM((16, POOL_W), F32)],
        compiler_params=_params(("arbitrary",), TE * POOL_W * 8),
    )(p)


def _pool_bwd(dpo, name):
    t = dpo.shape[0]
    n = t // TE
    ext = TE + 16

    def body(d_ref, o_ref, carry):
        i = pl.program_id(0)

        @pl.when(i == 0)
        def _():
            carry[...] = jnp.zeros_like(carry)

        ids = _row_ids(n - 1 - i)
        for gi, win in enumerate(POOL_WINDOWS):
            sl = slice(gi * LANE, (gi + 1) * LANE)
            dv = d_ref[:, sl]
            rv = dv / _pool_counts(ids, win)
            s = jnp.concatenate([rv, carry[:, sl]], axis=0)
            sh = 1
            while sh < win:
                s = s + pltpu.roll(s, ext - sh, 0)
                sh *= 2
            o_ref[:, sl] = (s[0:TE] - dv).astype(BF16)
            carry[:, sl] = rv[0:16]

    return _pallas_call(
        body, name=name, grid=(n,), in_specs=[_rows(POOL_W, n)], out_specs=_rows(POOL_W, n),
        out_shape=jax.ShapeDtypeStruct((t, POOL_W), BF16), scratch_shapes=[pltpu.VMEM((16, POOL_W), F32)],
        compiler_params=_params(("arbitrary",), TE * POOL_W * 8),
    )(dpo)


def _post_out_fwd(o, z, gate, pm, hn, ps, w_out, h, name):
    t = o.shape[0]
    tm = t // MM_TILES
    rows = lambda cols: pl.BlockSpec((tm, cols), lambda i: (i, 0))

    def body(o_ref, z_ref, g_ref, pm_ref, hn_ref, ps_ref, w_ref, h_ref, y_ref, h1_ref):
        for hd in range(H):
            sl = slice(hd * DH, (hd + 1) * DH)
            ov = o_ref[:, sl]
            zv = z_ref[:, sl].astype(F32)
            r = lax.rsqrt(jnp.mean(ov * ov, axis=-1, keepdims=True) + EPS)
            ya = ov * r * hn_ref[...] * (zv * _sigmoid(zv))
            ga = _sigmoid(g_ref[:, sl].astype(F32))
            gb = _sigmoid(g_ref[:, D + hd * DH:D + (hd + 1) * DH].astype(F32))
            y_ref[:, sl] = (ga * ya + gb * (pm_ref[:, sl] * ps_ref[:, sl])).astype(BF16)
        h1_ref[...] = h_ref[...] + _dot(y_ref[...], w_ref[...])

    return _pallas_call(
        body, name=name, grid=(MM_TILES,),
        in_specs=[rows(D), rows(D), rows(2 * D), rows(D), _whole((1, DH)), _whole((1, D)), _whole((D, D)), rows(D)],
        out_specs=[rows(D), rows(D)],
        out_shape=[jax.ShapeDtypeStruct((t, D), BF16), jax.ShapeDtypeStruct((t, D), F32)],
        compiler_params=_params(("parallel",), tm * D * 26 + 2 * D * D),
    )(o, z, gate, pm, hn, ps, w_out, h)


def _post_bwd(dy, o, z, gate, pm, hn, ps, name):
    t = o.shape[0]

    def body(dy_ref, o_ref, z_ref, g_ref, pm_ref, hn_ref, ps_ref, do_ref, dz_ref, dgate_ref, dpm_ref, dhn_ref, dps_ref):
        i = pl.program_id(0)

        @pl.when(i == 0)
        def _():
            dhn_ref[...] = jnp.zeros_like(dhn_ref)
            dps_ref[...] = jnp.zeros_like(dps_ref)

        hnv = hn_ref[...]
        dhn = jnp.zeros((1, DH), F32)
        for h in range(H):
            sl = slice(h * DH, (h + 1) * DH)
            slb = slice(D + h * DH, D + (h + 1) * DH)
            dyv = dy_ref[:, sl]
            ov = o_ref[:, sl]
            zv = z_ref[:, sl].astype(F32)
            r = lax.rsqrt(jnp.mean(ov * ov, axis=-1, keepdims=True) + EPS)
            sz = _sigmoid(zv)
            silu = zv * sz
            on = ov * r
            ya = on * hnv * silu
            ga = _sigmoid(g_ref[:, sl].astype(F32))
            gb = _sigmoid(g_ref[:, slb].astype(F32))
            pmv = pm_ref[:, sl]
            psv = ps_ref[:, sl]
            dya = dyv * ga
            dyb = dyv * gb
            dgate_ref[:, sl] = (dyv * ya * ga * (1.0 - ga)).astype(BF16)
            dgate_ref[:, slb] = (dyv * (pmv * psv) * gb * (1.0 - gb)).astype(BF16)
            tt = dya * hnv * silu
            do_ref[:, sl] = r * tt - ov * (r * r * r) * jnp.mean(ov * tt, axis=-1, keepdims=True)
            dz_ref[:, sl] = (dya * on * hnv * (sz * (1.0 + zv * (1.0 - sz)))).astype(BF16)
            dhn = dhn + jnp.sum(dya * on * silu, axis=0, keepdims=True)
            dps_ref[:, sl] += jnp.sum(dyb * pmv, axis=0, keepdims=True)
            dpm_ref[:, sl] = (dyb * psv).astype(BF16)
        dhn_ref[...] += dhn

    return _pallas_call(
        body, name=name, grid=(t // TE,),
        in_specs=[_rows(D), _rows(D), _rows(D), _rows(2 * D), _rows(D), _whole((1, DH)), _whole((1, D))],
        out_specs=[_rows(D), _rows(D), _rows(2 * D), _rows(D), _whole((1, DH)), _whole((1, D))],
        out_shape=[jax.ShapeDtypeStruct((t, D), F32), jax.ShapeDtypeStruct((t, D), BF16),
                   jax.ShapeDtypeStruct((t, 2 * D), BF16), jax.ShapeDtypeStruct((t, D), BF16),
                   jax.ShapeDtypeStruct((1, DH), F32), jax.ShapeDtypeStruct((1, D), F32)],
        compiler_params=_params(("arbitrary",), TE * D * 28),
    )(dy, o, z, gate, pm, hn, ps)


_FB_COLS = [(c, min(c + LANE, FB)) for c in range(0, FB, LANE)]


def _mlp_act_fwd(hid, cw, name):
    t = hid.shape[1]
    n = t // TE

    def body(hg_ref, hv_ref, wg_ref, wv_ref, a_ref, xg, xv):
        i = pl.program_id(1)
        _stage_history(xg, i)
        _stage_history(xv, i)
        for c, (c0, c1) in enumerate(_FB_COLS):
            sl, wd = slice(c0, c1), c1 - c0
            xg[c, SUB:SUB + TE, 0:wd] = hg_ref[:, sl].astype(F32)
            xv[c, SUB:SUB + TE, 0:wd] = hv_ref[:, sl].astype(F32)
            gg = _conv(_taps(xg, c, wd, SUB, TE, 3), wg_ref[:, sl])
            vv = _conv(_taps(xv, c, wd, SUB, TE, 3), wv_ref[:, sl])
            a_ref[:, sl] = (gg * _sigmoid(gg) * vv).astype(BF16)

    hspec = lambda off: pl.BlockSpec((None, TE, FB), lambda p, i: (p + off, i, 0))
    wspec = lambda off: pl.BlockSpec((None, 3, FB), lambda p, i: (p + off, 0, 0))
    return _pallas_call(
        body, name=name, grid=(4, n), in_specs=[hspec(0), hspec(4), wspec(0), wspec(4)],
        out_specs=pl.BlockSpec((None, TE, FB), lambda p, i: (p, i, 0)),
        out_shape=jax.ShapeDtypeStruct((4, t, FB), BF16),
        scratch_shapes=[_seq_scratch(FB), _seq_scratch(FB)],
        compiler_params=_params(("parallel", "arbitrary"), TE * FB * 12),
    )(hid, hid, cw, cw)


def _mlp_act_bwd(da, hid, cw, name):
    t = hid.shape[1]
    n = t // TE
    hb = TE // 16

    def body(da_ref, hg_ref, hv_ref, pg_ref, pv_ref, wg_ref, wv_ref, dhg_ref, dhv_ref, dwg_ref, dwv_ref, xg, xv, dg, dv):
        i = pl.program_id(1)

        @pl.when(i == 0)
        def _():
            dwg_ref[...] = jnp.zeros_like(dwg_ref)
            dwv_ref[...] = jnp.zeros_like(dwv_ref)

        _stage_future(dg, i)
        _stage_future(dv, i)
        first_tile = i == n - 1
        for c, (c0, c1) in enumerate(_FB_COLS):
            sl, wd = slice(c0, c1), c1 - c0
            for scr, p_ref, h_ref in ((xg, pg_ref, hg_ref), (xv, pv_ref, hv_ref)):
                scr[c, 0:SUB, 0:wd] = jnp.where(first_tile, 0.0, p_ref[SUB:2 * SUB, sl].astype(F32))
                scr[c, SUB:SUB + TE, 0:wd] = h_ref[:, sl].astype(F32)
            wg = wg_ref[:, sl]
            wv = wv_ref[:, sl]
            tg = _taps(xg, c, wd, SUB, TE, 3)
            tv = _taps(xv, c, wd, SUB, TE, 3)
            gg = _conv(tg, wg)
            vv = _conv(tv, wv)
            sg = _sigmoid(gg)
            dav = da_ref[:, sl].astype(F32)
            dgg = dav * vv * (sg * (1.0 + gg * (1.0 - sg)))
            dvv = dav * (gg * sg)
            for dc, tp, w, scr, dh_ref, dw_ref in ((dgg, tg, wg, dg, dhg_ref, dwg_ref), (dvv, tv, wv, dv, dhv_ref, dwv_ref)):
                scr[c, 0:TE, 0:wd] = dc
                dh_ref[:, sl] = _conv_t(scr, c, wd, w).astype(BF16)
                dw_ref[:, sl] += jnp.concatenate([jnp.sum(tp[j] * dc, axis=0, keepdims=True) for j in range(3)], axis=0)

    rev = lambda off: pl.BlockSpec((None, TE, FB), lambda p, i: (p + off, n - 1 - i, 0))
    halo = lambda off: pl.BlockSpec((None, 16, FB), lambda p, i: (p + off, jnp.maximum((n - 1 - i) * hb - 1, 0), 0))
    wspec = lambda off: pl.BlockSpec((None, 3, FB), lambda p, i: (p + off, 0, 0))
    dwspec = pl.BlockSpec((None, 3, FB), lambda p, i: (p, 0, 0))
    return _pallas_call(
        body, name=name, grid=(4, n), in_specs=[rev(0), rev(0), rev(4), halo(0), halo(4), wspec(0), wspec(4)],
        out_specs=[rev(0), rev(0), dwspec, dwspec],
        out_shape=[jax.ShapeDtypeStruct((4, t, FB), BF16), jax.ShapeDtypeStruct((4, t, FB), BF16),
                   jax.ShapeDtypeStruct((4, 3, FB), F32), jax.ShapeDtypeStruct((4, 3, FB), F32)],
        scratch_shapes=[_seq_scratch(FB)] * 4,
        compiler_params=_params(("parallel", "arbitrary"), TE * FB * 24),
    )(da, hid, hid, hid, hid, cw, cw)


def _adamw(lands, w, m, v, name):
    nl, r, c = w.shape
    tr = r
    if r * c * 4 > (2 << 20):
        for cand in (128, 64, 32, 16):
            if r % cand == 0:
                tr = cand
                break
    nr = r // tr
    c1 = 1.0 - ADAM_B1 ** ADAM_STEP
    c2 = 1.0 - ADAM_B2 ** ADAM_STEP

    def body(*refs):
        l_refs, (w_ref, m_ref, v_ref, g_out, d_out, m_out, v_out) = refs[:nl], refs[nl:]
        layer = pl.program_id(0)
        g = None
        for l, l_ref in enumerate(l_refs):
            gl = l_ref[0].astype(F32)
            for i in range(1, NDEV):
                gl = gl + l_ref[i].astype(F32)
            g = gl if g is None else jnp.where(layer == l, gl, g)
        mn = ADAM_B1 * m_ref[...] + (1.0 - ADAM_B1) * g
        vn = ADAM_B2 * v_ref[...] + (1.0 - ADAM_B2) * (g * g)
        g_out[...] = g
        m_out[...] = mn
        v_out[...] = vn
        d_out[...] = -ADAM_LR * ((mn / c1) / (jnp.sqrt(vn / c2) + ADAM_EPS) + ADAM_WD * w_ref[...])

    def land_spec(l):
        return pl.BlockSpec((NDEV, tr, c), lambda ly, i: (0, jnp.where(ly == l, i, 0 if l > 0 else nr - 1), 0))

    spec = pl.BlockSpec((None, tr, c), lambda ly, i: (ly, i, 0))
    shp = jax.ShapeDtypeStruct((nl, r, c), F32)
    return _pallas_call(
        body, name=name, grid=(nl, nr), in_specs=[land_spec(l) for l in range(nl)] + [spec, spec, spec],
        out_specs=[spec] * 4, out_shape=[shp] * 4,
        compiler_params=_params(("arbitrary", "arbitrary"), (11 + 4 * nl) * tr * c * 4),
    )(*lands, w, m, v)


def _layer_fwd(h, p, tag, gather=(), finish=None):
    u = _rmsnorm_fwd(h, p["norm_mix"], f"norm_mix_{tag}")
    qkv_pre = _mm(u, p["w_qkv"], BF16, f"proj_qkv_{tag}")
    z = _mm(u, p["w_z"], BF16, f"proj_z_{tag}")
    ba = _mm(u, p["w_ba"], F32, f"proj_ba_{tag}")
    pool_in = _mm(u, p["w_pl"], F32, f"proj_pool_{tag}")
    gate = _mm(u, p["w_gate"], BF16, f"proj_gate_{tag}")
    qkv = _gdn_pre_fwd(qkv_pre, p["conv_qkv"], f"gdn_pre_{tag}")
    beta, g = _gates_fwd(ba, p["a_row"], p["dt_row"], f"gates_{tag}")
    (o, s_in, vnew, tinv, wsv), gathered = _gdn_fwd(qkv, beta, g, f"gdn_{tag}", gather)
    if finish is not None:
        p = {**p, **finish(gathered)}
    pooled = _pool_fwd(pool_in, f"pool_{tag}")
    pm = _mm(pooled, p["w_pool_bd"], F32, f"pool_mm_{tag}")
    y, h1 = _post_out_fwd(o, z, gate, pm, p["head_norm"], p["pool_scale"], p["w_out"], h, f"post_out_{tag}")
    u2 = _rmsnorm_fwd(h1, p["norm_ffn"], f"norm_ffn_{tag}")
    hid = _mm_up(u2, p["w_up"], f"up_proj_{tag}")
    act = _mlp_act_fwd(hid, p["conv_ffn"], f"mlp_act_{tag}")
    h2 = _mm_blocks_red(act, p["w_down"], f"down_proj_{tag}", _NN, res=h1)
    saved = dict(h=h, u=u, qkv_pre=qkv_pre, z=z, ba=ba, gate=gate, qkv=qkv, beta=beta, g=g, o=o, s_in=s_in, vnew=vnew,
                 tinv=tinv, wsv=wsv, pooled=pooled, pm=pm, y=y, h1=h1, u2=u2, hid=hid, act=act)
    return h2, saved, gathered, p


def _layer_bwd(dh, dh_b, p, s, tag, scatter=()):
    gr = {}
    da = _mm_to_blocks(dh_b, p["w_down"], f"d_act_{tag}")
    gr["w_down"] = _mm_tn_blocks(s["act"], dh_b, f"dw_down_{tag}", True, False, BF16)
    dhg, dhv, dwg, dwv = _mlp_act_bwd(da, s["hid"], p["conv_ffn"], f"mlp_act_bwd_{tag}")
    gr["conv_ffn"] = jnp.concatenate([dwg, dwv], axis=0)
    w_up = p["w_up"]
    du2 = _mm_du2(dhg, dhv, w_up, f"d_u2_{tag}")
    gr["w_up"] = jnp.concatenate([_mm_tn_blocks(s["u2"], dhg, f"dw_upg_{tag}", False, True, BF16),
                                  _mm_tn_blocks(s["u2"], dhv, f"dw_upv_{tag}", False, True, BF16)], axis=0)
    dh1, dh1_b, gr["norm_ffn"] = _rmsnorm_bwd(s["h1"], du2, dh, p["norm_ffn"], f"norm_ffn_bwd_{tag}")
    dy = _mm(dh1_b, p["w_out"], F32, f"d_y_{tag}", dims=_NT)
    gr["w_out"] = _mm_tn(s["y"], dh1_b, f"dw_out_{tag}", BF16)
    do, dz, dgate, dpm, gr["head_norm"], gr["pool_scale"] = _post_bwd(
        dy, s["o"], s["z"], s["gate"], s["pm"], p["head_norm"], p["pool_scale"], f"post_bwd_{tag}")
    dpooled = _mm(dpm, p["w_pool_bd"], F32, f"d_pooled_{tag}", dims=_NT)
    dw_bd = _mm_tn(s["pooled"], dpm, f"dw_pool_{tag}")
    gr["w_pool"] = jnp.stack([dw_bd[g * DH:(g + 1) * DH, g * 2 * DH:(g + 1) * 2 * DH] for g in range(4)], axis=0)
    dpool_in = _pool_bwd(dpooled, f"pool_bwd_{tag}")
    own = (gr["w_up"].astype(BF16), gr["w_down"].reshape(NDEV, -1, D).astype(BF16),
           gr["w_out"].reshape(NDEV, D // NDEV, D).astype(BF16))
    (dqkv, dbeta, dg), landed = _gdn_bwd(s["qkv"], s["beta"], s["g"], do, s["s_in"], s["vnew"], s["tinv"], s["wsv"],
                                         f"gdn_bwd_{tag}", own + tuple(scatter))
    dba, gr["a_log"], gr["dt_bias"] = _gates_bwd(s["ba"], p["a_row"], p["dt_row"], dbeta, dg, f"gates_bwd_{tag}")
    dqkv_pre, gr["conv_qkv"] = _gdn_pre_bwd(s["qkv_pre"], p["conv_qkv"], dqkv, f"gdn_pre_bwd_{tag}")
    segs = (("qkv", dqkv_pre, p["w_qkv"]), ("z", dz, p["w_z"]), ("ba", dba, p["w_ba"]),
            ("pool", dpool_in, p["w_pl"]), ("gate", dgate, p["w_gate"]))
    du = _mm_sum_nt([(dseg, wseg) for _, dseg, wseg in segs], f"d_u_{tag}")
    dws = [_mm_tn(s["u"], dseg, f"dw_{nm}_{tag}", BF16) for nm, dseg, _ in segs]
    gr["w_in"] = jnp.concatenate([dws[0], dws[1], dws[2][:, 0:H], dws[2][:, LANE:LANE + H], dws[3], dws[4]], axis=1)
    dh0, dh0_b, gr["norm_mix"] = _rmsnorm_bwd(s["h"], du, dh1, p["norm_mix"], f"norm_mix_bwd_{tag}")
    return dh0, dh0_b, gr, landed


def _pool_block_diag(wp):
    out = jnp.zeros((POOL_W, D), wp.dtype)
    for g in range(4):
        out = out.at[g * DH:(g + 1) * DH, g * 2 * DH:(g + 1) * 2 * DH].set(wp[g])
    return out


def _pad_lanes(v8):
    return jnp.pad(v8.reshape(1, H), ((0, 0), (0, LANE - H)))


def _pack(parts, rows, lead=1):
    flat = jnp.concatenate([q.reshape(lead, -1) for q in parts], axis=1)
    flat = jnp.pad(flat, ((0, 0), (0, rows * LANE - flat.shape[1])))
    return flat.reshape((lead, rows, LANE) if lead > 1 else (rows, LANE))


def _unpack(packed, shapes, lead=1):
    flat = packed.reshape(lead, -1)
    out, off = [], 0
    for shp in shapes:
        n = 1
        for s_ in shp:
            n *= s_
        n //= lead
        out.append(flat[:, off:off + n].reshape(shp))
        off += n
    return out


SMALL_ROWS = 336
REPL_ROWS = 64


def kernel(x, meta_tokens, norm_mix, w_in, conv_qkv, a_log, dt_bias, head_norm, w_pool, pool_scale, w_out, norm_ffn, w_up, conv_ffn, w_down, norm_final, loss_target, m_meta_tokens, m_norm_mix, m_w_in, m_conv_qkv, m_a_log, m_dt_bias, m_head_norm, m_w_pool, m_pool_scale, m_w_out, m_norm_ffn, m_w_up, m_conv_ffn, m_w_down, m_norm_final, v_meta_tokens, v_norm_mix, v_w_in, v_conv_qkv, v_a_log, v_dt_bias, v_head_norm, v_w_pool, v_pool_scale, v_w_out, v_norm_ffn, v_w_up, v_conv_ffn, v_w_down, v_norm_final):
    seq = x.shape[1]
    t = ROW0 + seq
    assert t % TE == 0 and t % (MM_TILES * 16) == 0 and t % (GC * CH) == 0
    depth = w_in.shape[0]
    assert depth == 2
    cin = w_in.shape[2]

    def mixer_params(l, g_in, conv_q, conv_f, wp):
        wf = jnp.transpose(g_in, (1, 0, 2)).reshape(D, NDEV * cin)
        zpad = jnp.zeros((D, LANE - H), BF16)
        return dict(
            w_qkv=wf[:, 0:QKV], w_z=wf[:, QKV:QKV + D],
            w_ba=jnp.concatenate([wf[:, 4096:4104], zpad, wf[:, 4104:4112], zpad], axis=1),
            w_pl=wf[:, 4112:4624], w_gate=wf[:, 4624:6672], conv_qkv=conv_q, conv_ffn=conv_f,
            w_pool_bd=_pool_block_diag(wp),
            norm_mix=norm_mix[l].reshape(1, D), norm_ffn=norm_ffn[l].reshape(1, D),
            pool_scale=pool_scale[l].reshape(1, D), head_norm=head_norm[l].reshape(1, DH),
            a_row=_pad_lanes(a_log[l]), dt_row=_pad_lanes(dt_bias[l]))

    def late_params(g_up, g_out, g_down):
        return dict(w_out=g_out.reshape(D, D), w_up=g_up, w_down=g_down.reshape(4, FB, D))

    small_shapes = [conv_qkv.shape, conv_ffn.shape, w_pool.shape, meta_tokens.shape]
    small = _pack([conv_qkv, conv_ffn, w_pool, meta_tokens], SMALL_ROWS)
    w_in_b, w_up_b, w_out_b, w_down_b = w_in.astype(BF16), w_up.astype(BF16), w_out.astype(BF16), w_down.astype(BF16)
    g_in0, g_small = _gather([w_in_b[0], small], "gather_first")
    s_cq, s_cf, s_wp, s_mt = _unpack(g_small, [(NDEV,) + shp for shp in small_shapes], lead=NDEV)
    conv_qkv_full = jnp.transpose(s_cq, (1, 2, 0, 3)).reshape(depth, 4, QKV)
    conv_ffn_blk = jnp.transpose(s_cf, (1, 0, 2, 3))
    w_pool_full = jnp.transpose(s_wp, (1, 2, 3, 0, 4)).reshape(depth, 4, DH, 2 * DH).astype(BF16)
    meta_full = jnp.transpose(s_mt, (1, 0, 2)).reshape(N_META, D)

    h = jnp.concatenate([jnp.zeros((LEAD, D), F32), meta_full, x[0]], axis=0)
    p0 = mixer_params(0, g_in0, conv_qkv_full[0], conv_ffn_blk[0], w_pool_full[0])
    h, sv0, got0, p0 = _layer_fwd(h, p0, "l0", (w_up_b[0], w_out_b[0], w_down_b[0], w_in_b[1]),
                                  lambda got: late_params(*got[:3]))
    p1 = mixer_params(1, got0[3], conv_qkv_full[1], conv_ffn_blk[1], w_pool_full[1])
    h, sv1, _, p1 = _layer_fwd(h, p1, "l1", (w_up_b[1], w_out_b[1], w_down_b[1]), lambda got: late_params(*got))
    layers = [p0, p1]
    saved = [sv0, sv1]
    target = jnp.concatenate([jnp.zeros((ROW0, D), F32), loss_target[0]], axis=0)
    dh, dh_b, d_norm_final, loss_row = _loss_bwd(h, target, norm_final.reshape(1, D), "loss")

    def w_in_blocks(gr):
        return jnp.transpose(gr["w_in"].reshape(D, NDEV, cin), (1, 0, 2)).astype(BF16)

    grads = [None] * depth
    dh, dh_b, grads[1], (l_up1, l_down1, l_out1) = _layer_bwd(dh, dh_b, layers[1], saved[1], "l1")
    dh, dh_b, grads[0], (l_up0, l_down0, l_out0, l_in1) = _layer_bwd(dh, dh_b, layers[0], saved[0], "l0", (w_in_blocks(grads[1]),))
    grad_x = dh[ROW0:].reshape(1, seq, D)
    d_meta = dh[LEAD:ROW0]

    stk = lambda name: jnp.stack([grads[l][name] for l in range(depth)], axis=0)
    cq = conv_qkv.shape[2]
    pw = w_pool.shape[3]
    s_cq = jnp.transpose(stk("conv_qkv").reshape(depth, 4, NDEV, cq), (2, 0, 1, 3))
    s_cf = jnp.transpose(stk("conv_ffn"), (1, 0, 2, 3))
    s_wp = jnp.transpose(stk("w_pool").reshape(depth, 4, DH, NDEV, pw), (3, 0, 1, 2, 4))
    s_mt = jnp.transpose(d_meta.reshape(N_META, NDEV, D // NDEV), (1, 0, 2))
    b_small = _pack([s_cq, s_cf, s_wp, s_mt], SMALL_ROWS, lead=NDEV)
    l_in0, l_small = _scatter([w_in_blocks(grads[0]), b_small], "exchange_last")

    r_in = _adamw((l_in0, l_in1), w_in, m_w_in, v_w_in, "adamw_w_in")
    r_up = _adamw((l_up0, l_up1), w_up, m_w_up, v_w_up, "adamw_w_up")
    r_out = _adamw((l_out0, l_out1), w_out, m_w_out, v_w_out, "adamw_w_out")
    r_down = _adamw((l_down0, l_down1), w_down, m_w_down, v_w_down, "adamw_w_down")
    r_small = _adamw((l_small,), small[None], _pack([m_conv_qkv, m_conv_ffn, m_w_pool, m_meta_tokens], SMALL_ROWS)[None],
                     _pack([v_conv_qkv, v_conv_ffn, v_w_pool, v_meta_tokens], SMALL_ROWS)[None], "adamw_small")
    r_small = [_unpack(o_[0], small_shapes) for o_ in r_small]

    repl_shapes = [norm_mix.shape, a_log.shape, dt_bias.shape, head_norm.shape, pool_scale.shape, norm_ffn.shape,
                   norm_final.shape, (1,)]
    rp = lambda name, n: jnp.stack([grads[l][name][0, :n] for l in range(depth)], axis=0)
    part = _pack([rp("norm_mix", D), rp("a_log", H), rp("dt_bias", H), rp("head_norm", DH), rp("pool_scale", D),
                  rp("norm_ffn", D), d_norm_final[0], loss_row[0, 0:1]], REPL_ROWS)
    (l_repl,) = _gather([part], "gather_replicated")
    zero1 = jnp.zeros((1,), F32)
    r_repl = _adamw(
        (l_repl,), _pack([norm_mix, a_log, dt_bias, head_norm, pool_scale, norm_ffn, norm_final, zero1], REPL_ROWS)[None],
        _pack([m_norm_mix, m_a_log, m_dt_bias, m_head_norm, m_pool_scale, m_norm_ffn, m_norm_final, zero1], REPL_ROWS)[None],
        _pack([v_norm_mix, v_a_log, v_dt_bias, v_head_norm, v_pool_scale, v_norm_ffn, v_norm_final, zero1], REPL_ROWS)[None],
        "adamw_replicated")
    r_repl = [_unpack(o_[0], repl_shapes) for o_ in r_repl]
    loss = r_repl[0][7].reshape(())

    def leaf(kind):
        sm, rr = r_small[kind], r_repl[kind]
        return [sm[3], rr[0], r_in[kind], sm[0], rr[1], rr[2], rr[3], sm[2], rr[4], r_out[kind], rr[5], r_up[kind],
                sm[1], r_down[kind], rr[6]]

    return (loss, grad_x, *leaf(0), *leaf(1), *leaf(2), *leaf(3))
```
